```python
import jax
import jax.numpy as jnp
from jax import lax
import numpy as np

D_MODEL = 2048
BATCH = 8
SEQ = 8192
DEPTH = 1

CTX_LEN = 256
GRID_W = 64
D_CONV = D_MODEL
CONV_WIDTH = 3
RET_HEADS = 8
RET_DK = D_MODEL // RET_HEADS
RET_DV = 2 * D_MODEL // RET_HEADS
RET_CHUNK = 128
ROPE_BASE = 10000.0
D_FF = 4 * D_MODEL
N_MOD = 6
EPS = 1e-6
IN_NAMES = ('conv_b', 'conv_c', 'conv_x', 'q', 'k', 'v', 'g', 'gate_conv', 'gate_ret')
IN_WIDTHS = (D_CONV, D_CONV, D_CONV, RET_HEADS * RET_DK, RET_HEADS * RET_DK,
             RET_HEADS * RET_DV, RET_HEADS * RET_DV, D_MODEL, D_MODEL)
IN_OFFSETS = tuple(int(o) for o in np.cumsum((0,) + IN_WIDTHS))
D_IN = IN_OFFSETS[-1]

kernel_name = 'hybrid_shortconv_retention_flow_block'


def _rmsnorm(x, gain):
    xf = x.astype(jnp.float32)
    y = xf * lax.rsqrt(jnp.mean(xf * xf, axis=-1, keepdims=True) + EPS)
    return (y * gain.astype(jnp.float32)).astype(x.dtype)


def _modulate(x, gain, shift, scale):
    return _rmsnorm(x, gain) * (1 + scale[:, None, :]) + shift[:, None, :]


def _combined_projection(h, w_in, names):
    out = {}
    for i, name in enumerate(IN_NAMES):
        if name in names:
            out[name] = h @ w_in[:, IN_OFFSETS[i]:IN_OFFSETS[i + 1]]
    return out


def _flip(t):
    return jnp.flip(t, axis=1)


def _centred_conv(u, conv_w):
    length = u.shape[-2]
    half = CONV_WIDTH // 2
    pad = [(0, 0)] * (u.ndim - 2) + [(half, half), (0, 0)]
    up = jnp.pad(u, pad)
    return sum(conv_w[i] * lax.slice_in_dim(up, i, i + length, axis=u.ndim - 2)
               for i in range(CONV_WIDTH))


def _short_conv_branch(z, conv_w, w_conv_out, rows):
    u = z['conv_c'] * z['conv_x']
    if rows is None:
        y = _centred_conv(u, conv_w)
    else:
        b, l, ch = u.shape
        y = _centred_conv(u.reshape(b, rows, GRID_W, ch), conv_w).reshape(b, l, ch)
    return (z['conv_b'] * y) @ w_conv_out


def _rotary(t, pos):
    half = RET_DK // 2
    inv_freq = 1.0 / (ROPE_BASE ** jnp.linspace(0.0, 1.0, half, dtype=jnp.float32))
    ang = pos[:, None] * inv_freq[None, :]
    cos = jnp.cos(ang)[None, :, None, :].astype(t.dtype)
    sin = jnp.sin(ang)[None, :, None, :].astype(t.dtype)
    t1, t2 = t[..., :half], t[..., half:]
    return jnp.concatenate([t1 * cos - t2 * sin, t1 * sin + t2 * cos], axis=-1)


def _heads(t, dim):
    b, l, _ = t.shape
    return t.reshape(b, l, RET_HEADS, dim)


def _retention_scan(q, k, v, log_gamma, state0):
    b, l, h, _ = q.shape
    n_chunks = l // RET_CHUNK
    dt = q.dtype
    idx = jnp.arange(RET_CHUNK, dtype=jnp.float32)
    rel = idx[:, None] - idx[None, :]
    intra = jnp.where(rel[None] >= 0,
                      jnp.exp(log_gamma[:, None, None] * jnp.maximum(rel, 0.0)[None]),
                      0.0).astype(dt)
    q_decay = jnp.exp(log_gamma[:, None] * (idx[None, :] + 1.0)).astype(dt)
    k_decay = jnp.exp(log_gamma[:, None] * (RET_CHUNK - 1.0 - idx[None, :])).astype(dt)
    chunk_decay = jnp.exp(log_gamma * RET_CHUNK).astype(dt)[None, :, None, None]

    def to_chunks(t):
        return jnp.moveaxis(t.reshape(b, n_chunks, RET_CHUNK, h, t.shape[-1]), 1, 0)

    def step(state, qkv):
        qc, kc, vc = qkv
        scores = jnp.einsum('bihd,bjhd->bhij', qc, kc) * intra
        o = jnp.einsum('bhij,bjhe->bihe', scores, vc)
        o = o + jnp.einsum('bihd,hi,bhde->bihe', qc, q_decay, state)
        state = chunk_decay * state + jnp.einsum('bjhd,hj,bjhe->bhde', kc, k_decay, vc)
        return state, o

    state, o = lax.scan(step, state0.astype(dt), (to_chunks(q), to_chunks(k), to_chunks(v)))
    return jnp.moveaxis(o, 0, 1).reshape(b, l, h, v.shape[-1]), state


def _final_state(k, v, log_gamma):
    l = k.shape[1]
    w = jnp.exp((l - 1.0 - jnp.arange(l, dtype=jnp.float32))[:, None]
                * log_gamma[None, :]).astype(k.dtype)
    return jnp.einsum('bjhd,jh,bjhe->bhde', k, w, v)


def _retention_out(o, g, w_ret_out):
    of = o.astype(jnp.float32)
    mu = jnp.mean(of, axis=-1, keepdims=True)
    var = jnp.mean(jnp.square(of - mu), axis=-1, keepdims=True)
    on = ((of - mu) * lax.rsqrt(var + EPS)).astype(g.dtype)
    b, l = o.shape[:2]
    return (jax.nn.silu(g) * on.reshape(b, l, RET_HEADS * RET_DV)) @ w_ret_out


def _merge(z, y_conv, y_ret, w_o):
    return (jax.nn.sigmoid(z['gate_conv']) * y_conv
            + jax.nn.sigmoid(z['gate_ret']) * y_ret) @ w_o


def _sqrelu_mlp(h, w_ff1, w_ff2):
    return jnp.square(jax.nn.relu(h @ w_ff1)) @ w_ff2


def _fwd_setup_inputs(seed: int = 0) -> dict:
    key = jax.random.key(seed)
    ks = jax.random.split(key, 20)
    f32 = jnp.float32

    def nrm(k, shape, scale):
        return jax.random.normal(k, shape, f32) * scale

    gamma = 1.0 - 2.0 ** (-5.0 - np.arange(RET_HEADS))
    decay_logit = jnp.asarray(np.log(gamma / (1.0 - gamma)), dtype=f32)
    return {
        'x': nrm(ks[0], (BATCH, SEQ, D_MODEL), 1.0),
        'c': nrm(ks[1], (BATCH, D_MODEL), 1.0),
        'ctx': nrm(ks[2], (BATCH, CTX_LEN, D_MODEL), 1.0),
        'c_ctx': nrm(ks[3], (D_MODEL,), 1.0),
        'w_mod': nrm(ks[4], (DEPTH, D_MODEL, N_MOD * D_MODEL), 0.5 * D_MODEL ** -0.5),
        'b_mod': nrm(ks[5], (DEPTH, N_MOD * D_MODEL), 0.02),
        'norm1_g': 1.0 + nrm(ks[6], (DEPTH, D_MODEL), 0.02),
        'w_in': nrm(ks[7], (DEPTH, D_MODEL, D_IN), D_MODEL ** -0.5),
        'conv_w': nrm(ks[8], (DEPTH, CONV_WIDTH, D_CONV), CONV_WIDTH ** -0.5),
        'w_conv_out': nrm(ks[9], (DEPTH, D_CONV, D_MODEL), D_CONV ** -0.5),
        'ret_decay_fwd': decay_logit[None, :] + nrm(ks[10], (DEPTH, RET_HEADS), 0.1),
        'ret_decay_bwd': decay_logit[None, :] + nrm(ks[11], (DEPTH, RET_HEADS), 0.1),
        'w_ret_out': nrm(ks[12], (DEPTH, RET_HEADS * RET_DV, D_MODEL), (RET_HEADS * RET_DV) ** -0.5),
        'w_o': nrm(ks[13], (DEPTH, D_MODEL, D_MODEL), D_MODEL ** -0.5),
        'norm2_g': 1.0 + nrm(ks[14], (DEPTH, D_MODEL), 0.02),
        'w_ff1': nrm(ks[15], (DEPTH, D_MODEL, D_FF), D_MODEL ** -0.5),
        'w_ff2': nrm(ks[16], (DEPTH, D_FF, D_MODEL), D_FF ** -0.5),
        'final_g': 1.0 + nrm(ks[17], (D_MODEL,), 0.02),
    }


def _fwd_reference(x, c, ctx, c_ctx, w_mod, b_mod, norm1_g, w_in, conv_w, w_conv_out,
              ret_decay_fwd, ret_decay_bwd, w_ret_out, w_o, norm2_g, w_ff1, w_ff2, final_g):
    b, seq, _ = x.shape
    rows = seq // GRID_W
    ctx_len = ctx.shape[1]
    pos_ctx = jnp.arange(ctx_len, dtype=jnp.float32)
    pos_lat = ctx_len + jnp.arange(seq, dtype=jnp.float32)
    h_ctx = ctx
    for layer in range(DEPTH):
        last = layer == DEPTH - 1
        mod_l = jnp.split(jax.nn.silu(c) @ w_mod[layer] + b_mod[layer], N_MOD, axis=-1)
        mod_c = jnp.split((jax.nn.silu(c_ctx) @ w_mod[layer] + b_mod[layer])[None], N_MOD, axis=-1)
        lg_f = jax.nn.log_sigmoid(ret_decay_fwd[layer].astype(jnp.float32))
        lg_b = jax.nn.log_sigmoid(ret_decay_bwd[layer].astype(jnp.float32))

        a_c = _modulate(h_ctx, norm1_g[layer], mod_c[0], mod_c[1])
        if last:
            zc = _combined_projection(a_c, w_in[layer], ('k', 'v'))
            kc = _rotary(_heads(zc['k'], RET_DK), pos_ctx) * RET_DK ** -0.5
            vc = _heads(zc['v'], RET_DV)
            st_f = _final_state(kc, vc, lg_f)
            st_b = _final_state(_flip(kc), _flip(vc), lg_b)
        else:
            zc = _combined_projection(a_c, w_in[layer], IN_NAMES)
            qc = _rotary(_heads(zc['q'], RET_DK), pos_ctx)
            kc = _rotary(_heads(zc['k'], RET_DK), pos_ctx) * RET_DK ** -0.5
            vc = _heads(zc['v'], RET_DV)
            zero = jnp.zeros((b, RET_HEADS, RET_DK, RET_DV), vc.dtype)
            oc_f, st_f = _retention_scan(qc, kc, vc, lg_f, zero)
            oc_b, st_b = _retention_scan(_flip(qc), _flip(kc), _flip(vc), lg_b, zero)
            y_c = _merge(zc,
                         _short_conv_branch(zc, conv_w[layer], w_conv_out[layer], None),
                         _retention_out(oc_f + _flip(oc_b), zc['g'], w_ret_out[layer]),
                         w_o[layer])
            h_ctx = h_ctx + mod_c[2][:, None, :] * y_c

        a_l = _modulate(x, norm1_g[layer], mod_l[0], mod_l[1])
        zl = _combined_projection(a_l, w_in[layer], IN_NAMES)
        ql = _rotary(_heads(zl['q'], RET_DK), pos_lat)
        kl = _rotary(_heads(zl['k'], RET_DK), pos_lat) * RET_DK ** -0.5
        vl = _heads(zl['v'], RET_DV)
        ol_f, _ = _retention_scan(ql, kl, vl, lg_f, st_f)
        ol_b, _ = _retention_scan(_flip(ql), _flip(kl), _flip(vl), lg_b, st_b)
        y_l = _merge(zl,
                     _short_conv_branch(zl, conv_w[layer], w_conv_out[layer], rows),
                     _retention_out(ol_f + _flip(ol_b), zl['g'], w_ret_out[layer]),
                     w_o[layer])
        x = x + mod_l[2][:, None, :] * y_l

        x = x + mod_l[5][:, None, :] * _sqrelu_mlp(
            _modulate(x, norm2_g[layer], mod_l[3], mod_l[4]), w_ff1[layer], w_ff2[layer])
        if not last:
            h_ctx = h_ctx + mod_c[5][:, None, :] * _sqrelu_mlp(
                _modulate(h_ctx, norm2_g[layer], mod_c[3], mod_c[4]), w_ff1[layer], w_ff2[layer])
    return _rmsnorm(x, final_g)


import jax as _jax
import jax.numpy as _jnp

TWIN_FORMAT = 'train_step'
FWD_PARAMS = ['x', 'c', 'ctx', 'c_ctx', 'w_mod', 'b_mod', 'norm1_g', 'w_in', 'conv_w', 'w_conv_out', 'ret_decay_fwd', 'ret_decay_bwd', 'w_ret_out', 'w_o', 'norm2_g', 'w_ff1', 'w_ff2', 'final_g']
TWIN_WEIGHTS = ['c_ctx', 'w_mod', 'b_mod', 'norm1_g', 'w_in', 'conv_w', 'w_conv_out', 'ret_decay_fwd', 'ret_decay_bwd', 'w_ret_out', 'w_o', 'norm2_g', 'w_ff1', 'w_ff2', 'final_g']
TWIN_DIFF_INPUT = 'x'
TWIN_INPUTS = ['x', 'c', 'ctx', 'c_ctx', 'w_mod', 'b_mod', 'norm1_g', 'w_in', 'conv_w', 'w_conv_out', 'ret_decay_fwd', 'ret_decay_bwd', 'w_ret_out', 'w_o', 'norm2_g', 'w_ff1', 'w_ff2', 'final_g', 'loss_target', 'm_c_ctx', 'm_w_mod', 'm_b_mod', 'm_norm1_g', 'm_w_in', 'm_conv_w', 'm_w_conv_out', 'm_ret_decay_fwd', 'm_ret_decay_bwd', 'm_w_ret_out', 'm_w_o', 'm_norm2_g', 'm_w_ff1', 'm_w_ff2', 'm_final_g', 'v_c_ctx', 'v_w_mod', 'v_b_mod', 'v_norm1_g', 'v_w_in', 'v_conv_w', 'v_w_conv_out', 'v_ret_decay_fwd', 'v_ret_decay_bwd', 'v_w_ret_out', 'v_w_o', 'v_norm2_g', 'v_w_ff1', 'v_w_ff2', 'v_final_g']
TWIN_OUTPUTS = ['loss', 'grad_x', 'grad_c_ctx', 'grad_w_mod', 'grad_b_mod', 'grad_norm1_g', 'grad_w_in', 'grad_conv_w', 'grad_w_conv_out', 'grad_ret_decay_fwd', 'grad_ret_decay_bwd', 'grad_w_ret_out', 'grad_w_o', 'grad_norm2_g', 'grad_w_ff1', 'grad_w_ff2', 'grad_final_g', 'delta_c_ctx', 'delta_w_mod', 'delta_b_mod', 'delta_norm1_g', 'delta_w_in', 'delta_conv_w', 'delta_w_conv_out', 'delta_ret_decay_fwd', 'delta_ret_decay_bwd', 'delta_w_ret_out', 'delta_w_o', 'delta_norm2_g', 'delta_w_ff1', 'delta_w_ff2', 'delta_final_g', 'new_m_c_ctx', 'new_m_w_mod', 'new_m_b_mod', 'new_m_norm1_g', 'new_m_w_in', 'new_m_conv_w', 'new_m_w_conv_out', 'new_m_ret_decay_fwd', 'new_m_ret_decay_bwd', 'new_m_w_ret_out', 'new_m_w_o', 'new_m_norm2_g', 'new_m_w_ff1', 'new_m_w_ff2', 'new_m_final_g', 'new_v_c_ctx', 'new_v_w_mod', 'new_v_b_mod', 'new_v_norm1_g', 'new_v_w_in', 'new_v_conv_w', 'new_v_w_conv_out', 'new_v_ret_decay_fwd', 'new_v_ret_decay_bwd', 'new_v_w_ret_out', 'new_v_w_o', 'new_v_norm2_g', 'new_v_w_ff1', 'new_v_w_ff2', 'new_v_final_g']
TWIN_LEAF_KINDS = {'loss': 'loss', 'grad_x': 'grad_x', 'grad_c_ctx': 'grad_w', 'grad_w_mod': 'grad_w', 'grad_b_mod': 'grad_w', 'grad_norm1_g': 'grad_w', 'grad_w_in': 'grad_w', 'grad_conv_w': 'grad_w', 'grad_w_conv_out': 'grad_w', 'grad_ret_decay_fwd': 'grad_w', 'grad_ret_decay_bwd': 'grad_w', 'grad_w_ret_out': 'grad_w', 'grad_w_o': 'grad_w', 'grad_norm2_g': 'grad_w', 'grad_w_ff1': 'grad_w', 'grad_w_ff2': 'grad_w', 'grad_final_g': 'grad_w', 'delta_c_ctx': 'delta_w', 'delta_w_mod': 'delta_w', 'delta_b_mod': 'delta_w', 'delta_norm1_g': 'delta_w', 'delta_w_in': 'delta_w', 'delta_conv_w': 'delta_w', 'delta_w_conv_out': 'delta_w', 'delta_ret_decay_fwd': 'delta_w', 'delta_ret_decay_bwd': 'delta_w', 'delta_w_ret_out': 'delta_w', 'delta_w_o': 'delta_w', 'delta_norm2_g': 'delta_w', 'delta_w_ff1': 'delta_w', 'delta_w_ff2': 'delta_w', 'delta_final_g': 'delta_w', 'new_m_c_ctx': 'new_m', 'new_m_w_mod': 'new_m', 'new_m_b_mod': 'new_m', 'new_m_norm1_g': 'new_m', 'new_m_w_in': 'new_m', 'new_m_conv_w': 'new_m', 'new_m_w_conv_out': 'new_m', 'new_m_ret_decay_fwd': 'new_m', 'new_m_ret_decay_bwd': 'new_m', 'new_m_w_ret_out': 'new_m', 'new_m_w_o': 'new_m', 'new_m_norm2_g': 'new_m', 'new_m_w_ff1': 'new_m', 'new_m_w_ff2': 'new_m', 'new_m_final_g': 'new_m', 'new_v_c_ctx': 'new_v', 'new_v_w_mod': 'new_v', 'new_v_b_mod': 'new_v', 'new_v_norm1_g': 'new_v', 'new_v_w_in': 'new_v', 'new_v_conv_w': 'new_v', 'new_v_w_conv_out': 'new_v', 'new_v_ret_decay_fwd': 'new_v', 'new_v_ret_decay_bwd': 'new_v', 'new_v_w_ret_out': 'new_v', 'new_v_w_o': 'new_v', 'new_v_norm2_g': 'new_v', 'new_v_w_ff1': 'new_v', 'new_v_w_ff2': 'new_v', 'new_v_final_g': 'new_v'}


def _forward(args):
    return _fwd_reference(*[args[k] for k in FWD_PARAMS])


def _output_shape():
    def fwd():
        inp = _fwd_setup_inputs(0)
        return _fwd_reference(*[inp[k] for k in FWD_PARAMS])
    out = _jax.eval_shape(fwd)
    return out.shape, out.dtype

N_MICROBATCH = 1
ADAM_LR = 0.001
ADAM_B1 = 0.9
ADAM_B2 = 0.999
ADAM_EPS = 1e-08
ADAM_WD = 0.01
ADAM_STEP = 10
PER_EXAMPLE_BATCH_AXIS = {'x': 0, 'c': 0, 'ctx': 0, 'loss_target': 0}
SHARED_INPUTS = []
_WEIGHT_DTYPES = {'c_ctx': _jnp.float32, 'w_mod': _jnp.float32, 'b_mod': _jnp.float32, 'norm1_g': _jnp.float32, 'w_in': _jnp.float32, 'conv_w': _jnp.float32, 'w_conv_out': _jnp.float32, 'ret_decay_fwd': _jnp.float32, 'ret_decay_bwd': _jnp.float32, 'w_ret_out': _jnp.float32, 'w_o': _jnp.float32, 'norm2_g': _jnp.float32, 'w_ff1': _jnp.float32, 'w_ff2': _jnp.float32, 'final_g': _jnp.float32}
MOMENT_SCALE = {'c_ctx': 8.980833e-03, 'w_mod': 6.422011e-02, 'b_mod': 1.211632e-01, 'norm1_g': 5.119668e-02, 'w_in': 1.608649e-02, 'conv_w': 2.557869e-02, 'w_conv_out': 2.511024e-02, 'ret_decay_fwd': 4.131036e-02, 'ret_decay_bwd': 9.628572e-02, 'w_ret_out': 1.326955e-02, 'w_o': 2.841070e-02, 'norm2_g': 5.611918e-02, 'w_ff1': 2.830741e-02, 'w_ff2': 5.401260e-02, 'final_g': 3.211081e+01}


def _to_microbatches(a, axis):
    t = _jnp.moveaxis(a, axis, 0)
    t = t.reshape((N_MICROBATCH, t.shape[0] // N_MICROBATCH) + t.shape[1:])
    return _jnp.moveaxis(t, 1, axis + 1)


def setup_inputs(seed: int = 0) -> dict:
    inp = _fwd_setup_inputs(seed)
    key = _jax.random.fold_in(_jax.random.key(seed), 7919)
    shape, _ = _output_shape()
    out = dict(inp)
    out["loss_target"] = _jax.random.normal(_jax.random.fold_in(key, 0), shape, _jnp.float32)
    for i, name in enumerate(TWIN_WEIGHTS):
        w = inp[name].astype(_jnp.float32)
        if MOMENT_SCALE is None:
            s = _jnp.sqrt(_jnp.mean(_jnp.square(w)) + 1e-30)
        else:
            s = MOMENT_SCALE[name]
        km, kv = _jax.random.split(_jax.random.fold_in(key, i + 1))
        out[name] = w
        out["m_" + name] = s * _jax.random.normal(km, w.shape, _jnp.float32)
        out["v_" + name] = (s * s) * _jax.random.uniform(kv, w.shape, _jnp.float32, 0.5, 1.5)
    if N_MICROBATCH > 1:
        for name, axis in PER_EXAMPLE_BATCH_AXIS.items():
            out[name] = _to_microbatches(out[name], axis)
    return {'x': out['x'], 'c': out['c'], 'ctx': out['ctx'], 'c_ctx': out['c_ctx'], 'w_mod': out['w_mod'], 'b_mod': out['b_mod'], 'norm1_g': out['norm1_g'], 'w_in': out['w_in'], 'conv_w': out['conv_w'], 'w_conv_out': out['w_conv_out'], 'ret_decay_fwd': out['ret_decay_fwd'], 'ret_decay_bwd': out['ret_decay_bwd'], 'w_ret_out': out['w_ret_out'], 'w_o': out['w_o'], 'norm2_g': out['norm2_g'], 'w_ff1': out['w_ff1'], 'w_ff2': out['w_ff2'], 'final_g': out['final_g'], 'loss_target': out['loss_target'], 'm_c_ctx': out['m_c_ctx'], 'm_w_mod': out['m_w_mod'], 'm_b_mod': out['m_b_mod'], 'm_norm1_g': out['m_norm1_g'], 'm_w_in': out['m_w_in'], 'm_conv_w': out['m_conv_w'], 'm_w_conv_out': out['m_w_conv_out'], 'm_ret_decay_fwd': out['m_ret_decay_fwd'], 'm_ret_decay_bwd': out['m_ret_decay_bwd'], 'm_w_ret_out': out['m_w_ret_out'], 'm_w_o': out['m_w_o'], 'm_norm2_g': out['m_norm2_g'], 'm_w_ff1': out['m_w_ff1'], 'm_w_ff2': out['m_w_ff2'], 'm_final_g': out['m_final_g'], 'v_c_ctx': out['v_c_ctx'], 'v_w_mod': out['v_w_mod'], 'v_b_mod': out['v_b_mod'], 'v_norm1_g': out['v_norm1_g'], 'v_w_in': out['v_w_in'], 'v_conv_w': out['v_conv_w'], 'v_w_conv_out': out['v_w_conv_out'], 'v_ret_decay_fwd': out['v_ret_decay_fwd'], 'v_ret_decay_bwd': out['v_ret_decay_bwd'], 'v_w_ret_out': out['v_w_ret_out'], 'v_w_o': out['v_w_o'], 'v_norm2_g': out['v_norm2_g'], 'v_w_ff1': out['v_w_ff1'], 'v_w_ff2': out['v_w_ff2'], 'v_final_g': out['v_final_g']}


def _loss(weights, diff, rest, loss_target):
    with _jax.named_scope("forward"):
        args = {**rest, TWIN_DIFF_INPUT: diff, **{k: w.astype(_WEIGHT_DTYPES[k]) for k, w in weights.items()}}
        y = _forward(args)
    with _jax.named_scope("loss_head"):
        err = _jnp.square(y.astype(_jnp.float32) - loss_target)
        return 0.5 * _jnp.sum(_jnp.mean(err, axis=-1)) if err.ndim else 0.5 * err


def _adamw(w, g, m, v):
    m = ADAM_B1 * m + (1.0 - ADAM_B1) * g
    v = ADAM_B2 * v + (1.0 - ADAM_B2) * _jnp.square(g)
    m_hat = m / (1.0 - ADAM_B1 ** ADAM_STEP)
    v_hat = v / (1.0 - ADAM_B2 ** ADAM_STEP)
    delta = -ADAM_LR * (m_hat / (_jnp.sqrt(v_hat) + ADAM_EPS) + ADAM_WD * w)
    return delta, m, v


def reference(x, c, ctx, c_ctx, w_mod, b_mod, norm1_g, w_in, conv_w, w_conv_out, ret_decay_fwd, ret_decay_bwd, w_ret_out, w_o, norm2_g, w_ff1, w_ff2, final_g, loss_target, m_c_ctx, m_w_mod, m_b_mod, m_norm1_g, m_w_in, m_conv_w, m_w_conv_out, m_ret_decay_fwd, m_ret_decay_bwd, m_w_ret_out, m_w_o, m_norm2_g, m_w_ff1, m_w_ff2, m_final_g, v_c_ctx, v_w_mod, v_b_mod, v_norm1_g, v_w_in, v_conv_w, v_w_conv_out, v_ret_decay_fwd, v_ret_decay_bwd, v_w_ret_out, v_w_o, v_norm2_g, v_w_ff1, v_w_ff2, v_final_g):
    given = dict(x=x, c=c, ctx=ctx, c_ctx=c_ctx, w_mod=w_mod, b_mod=b_mod, norm1_g=norm1_g, w_in=w_in, conv_w=conv_w, w_conv_out=w_conv_out, ret_decay_fwd=ret_decay_fwd, ret_decay_bwd=ret_decay_bwd, w_ret_out=w_ret_out, w_o=w_o, norm2_g=norm2_g, w_ff1=w_ff1, w_ff2=w_ff2, final_g=final_g, loss_target=loss_target, m_c_ctx=m_c_ctx, m_w_mod=m_w_mod, m_b_mod=m_b_mod, m_norm1_g=m_norm1_g, m_w_in=m_w_in, m_conv_w=m_conv_w, m_w_conv_out=m_w_conv_out, m_ret_decay_fwd=m_ret_decay_fwd, m_ret_decay_bwd=m_ret_decay_bwd, m_w_ret_out=m_w_ret_out, m_w_o=m_w_o, m_norm2_g=m_norm2_g, m_w_ff1=m_w_ff1, m_w_ff2=m_w_ff2, m_final_g=m_final_g, v_c_ctx=v_c_ctx, v_w_mod=v_w_mod, v_b_mod=v_b_mod, v_norm1_g=v_norm1_g, v_w_in=v_w_in, v_conv_w=v_conv_w, v_w_conv_out=v_w_conv_out, v_ret_decay_fwd=v_ret_decay_fwd, v_ret_decay_bwd=v_ret_decay_bwd, v_w_ret_out=v_w_ret_out, v_w_o=v_w_o, v_norm2_g=v_norm2_g, v_w_ff1=v_w_ff1, v_w_ff2=v_w_ff2, v_final_g=v_final_g)
    weights = {n: given[n] for n in TWIN_WEIGHTS}
    shared = {n: given[n] for n in SHARED_INPUTS}
    per_example = {n: given[n] for n in ['x', 'c', 'ctx']}
    grad_fn = _jax.value_and_grad(_loss, argnums=(0, 1))

    def one_microbatch(ex, loss_target):
        ex = dict(ex)
        diff = ex.pop(TWIN_DIFF_INPUT)
        return grad_fn(weights, diff, {**shared, **ex}, loss_target)

    if N_MICROBATCH == 1:
        loss, (grad_w, grad_x) = one_microbatch(per_example, given["loss_target"])
    else:
        def body(carry, xs):
            loss_sum, grad_sum = carry
            l_k, (gw_k, gx_k) = one_microbatch(xs[0], xs[1])
            with _jax.named_scope("update"):
                return (loss_sum + l_k, _jax.tree.map(_jnp.add, grad_sum, gw_k)), gx_k

        init = (_jnp.zeros((), _jnp.float32), _jax.tree.map(_jnp.zeros_like, weights))
        (loss, grad_w), grad_x = _jax.lax.scan(body, init, (per_example, given["loss_target"]))
    with _jax.named_scope("update"):
        delta_w, new_m, new_v = {}, {}, {}
        for n in TWIN_WEIGHTS:
            delta_w[n], new_m[n], new_v[n] = _adamw(weights[n], grad_w[n], given["m_" + n], given["v_" + n])
    return (loss, grad_x, *[grad_w[n] for n in TWIN_WEIGHTS], *[delta_w[n] for n in TWIN_WEIGHTS],
            *[new_m[n] for n in TWIN_WEIGHTS], *[new_v[n] for n in TWIN_WEIGHTS])
```

```python
import functools

import jax
import jax.numpy as jnp
from jax import lax
from jax.experimental import pallas as pl
from jax.experimental.pallas import tpu as pltpu

F32 = jnp.float32
BF16 = jnp.bfloat16
MESH = pl.DeviceIdType.MESH

N_DEV = 8
HEADS = 8
N_MOD = 6
N_IN = 11
GRID_W = 64
CHUNK = 128
ROPE_BASE = 10000.0
EPS = 1e-6
ADAM_LR, ADAM_B1, ADAM_B2, ADAM_EPS, ADAM_WD, ADAM_STEP = 0.001, 0.9, 0.999, 1e-08, 0.01, 10
VMEM_LIMIT = 56 * 1024 * 1024
HIGHEST = lax.Precision.HIGHEST
CB, CC, CX, CQ, CK, CV, CG, CGC, CGR = 0, 1, 2, 3, 4, 5, 7, 9, 10


def _tile(n, target, mult):
    t = (min(target, n) // mult) * mult
    while t >= mult:
        if n % t == 0:
            return t
        t -= mult
    return n


def _params(sem=None):
    return pltpu.CompilerParams(dimension_semantics=sem, vmem_limit_bytes=VMEM_LIMIT)


def _sig(v):
    return 1.0 / (1.0 + jnp.exp(-v))


def _coords():
    return lax.axis_index("x"), lax.axis_index("y"), lax.axis_index("c")


def _flip(p, m):
    return tuple(1 - v if (m >> s) & 1 else v for v, s in zip(p, (2, 1, 0)))


def _index(p):
    return 4 * p[0] + 2 * p[1] + p[2]


def _small_allgather(x, name):
    r, n = x.shape

    def body(x_ref, out_ref, send_sems, recv_sems):
        me = _coords()
        out_ref[pl.ds(_index(me), 1)] = x_ref[...][None]
        sent = []
        for m in range(1, N_DEV):
            cp = pltpu.make_async_remote_copy(
                src_ref=x_ref, dst_ref=out_ref.at[_index(me)], send_sem=send_sems.at[m - 1],
                recv_sem=recv_sems.at[m - 1], device_id=_flip(me, m), device_id_type=MESH)
            cp.start()
            sent.append(cp)
        for m in range(1, N_DEV):
            pltpu.make_async_remote_copy(
                src_ref=x_ref, dst_ref=out_ref.at[_index(_flip(me, m))], send_sem=send_sems.at[m - 1],
                recv_sem=recv_sems.at[m - 1], device_id=_flip(me, m), device_id_type=MESH).wait_recv()
        for cp in sent:
            cp.wait_send()

    return pl.pallas_call(
        body, name=name, out_shape=jax.ShapeDtypeStruct((N_DEV, r, n), x.dtype),
        in_specs=[pl.BlockSpec(memory_space=pltpu.VMEM)], out_specs=pl.BlockSpec(memory_space=pltpu.VMEM),
        scratch_shapes=[pltpu.SemaphoreType.DMA((N_DEV - 1,)), pltpu.SemaphoreType.DMA((N_DEV - 1,))],
    )(x)


def _window(ref, j, r, c, axis):
    if axis == 0:
        return ref.at[pl.ds(pl.multiple_of(j * r, 8), r), :]
    return ref.at[:, pl.ds(pl.multiple_of(j * c, 128), c)]


def _allgather(x, axis, name):
    r, c = x.shape
    full = (N_DEV * r, c) if axis == 0 else (r, N_DEV * c)

    def body(x_ref, out_ref, send_sems, recv_sems, local_sem):
        me = _coords()
        sibling = _flip(me, 1)
        chips = [4, 2, 6]

        def copy(k, block, to, src=None):
            dst = _window(out_ref, _index(block), r, c, axis)
            return pltpu.make_async_remote_copy(
                src_ref=dst if src is None else src, dst_ref=dst, send_sem=send_sems.at[k],
                recv_sem=recv_sems.at[k], device_id=to, device_id_type=MESH)

        mine = pltpu.make_async_copy(x_ref, _window(out_ref, _index(me), r, c, axis), local_sem)
        mine.start()
        first = [copy(0, me, sibling, src=x_ref)]
        first += [copy(1 + j, me, _flip(me, m), src=x_ref) for j, m in enumerate(chips)]
        for cp in first:
            cp.start()
        passed = [copy(4 + j, _flip(me, m), sibling) for j, m in enumerate(chips)]
        for j, m in enumerate(chips):
            copy(1 + j, _flip(me, m), me).wait_recv()
            passed[j].start()
        copy(0, sibling, me).wait_recv()
        for j, m in enumerate(chips):
            copy(4 + j, _flip(sibling, m), me).wait_recv()
        for cp in first + passed:
            cp.wait_send()
        mine.wait()

    return pl.pallas_call(
        body, name=name, out_shape=jax.ShapeDtypeStruct(full, x.dtype),
        in_specs=[pl.BlockSpec(memory_space=pl.ANY)], out_specs=pl.BlockSpec(memory_space=pl.ANY),
        scratch_shapes=[pltpu.SemaphoreType.DMA((N_DEV - 1,)), pltpu.SemaphoreType.DMA((N_DEV - 1,)),
                        pltpu.SemaphoreType.DMA(())],
    )(x)


def _scatter_exchange(g, axis, name):
    if axis == 0:
        r, c = g.shape[0] // N_DEV, g.shape[1]
    else:
        r, c = g.shape[0], g.shape[1] // N_DEV

    def body(g_ref, out_ref, send_sems, recv_sems, local_sem):
        me = _coords()
        mine = pltpu.make_async_copy(_window(g_ref, _index(me), r, c, axis), out_ref.at[_index(me)], local_sem)
        mine.start()
        sent = []
        for m in range(1, N_DEV):
            peer = _flip(me, m)
            cp = pltpu.make_async_remote_copy(
                src_ref=_window(g_ref, _index(peer), r, c, axis), dst_ref=out_ref.at[_index(me)],
                send_sem=send_sems.at[m - 1], recv_sem=recv_sems.at[m - 1], device_id=peer, device_id_type=MESH)
            cp.start()
            sent.append(cp)
        for m in range(1, N_DEV):
            peer = _flip(me, m)
            pltpu.make_async_remote_copy(
                src_ref=_window(g_ref, _index(me), r, c, axis), dst_ref=out_ref.at[_index(peer)],
                send_sem=send_sems.at[m - 1], recv_sem=recv_sems.at[m - 1], device_id=peer,
                device_id_type=MESH).wait_recv()
        for cp in sent:
            cp.wait_send()
        mine.wait()

    return pl.pallas_call(
        body, name=name, out_shape=jax.ShapeDtypeStruct((N_DEV, r, c), g.dtype),
        in_specs=[pl.BlockSpec(memory_space=pl.ANY)], out_specs=pl.BlockSpec(memory_space=pl.ANY),
        scratch_shapes=[pltpu.SemaphoreType.DMA((N_DEV - 1,)), pltpu.SemaphoreType.DMA((N_DEV - 1,)),
                        pltpu.SemaphoreType.DMA(())],
    )(g)


def _mm(a, b, mode, out_dtypes, name, tm=1024, tn=1024, tk=512, extras=(), epilogue=None):
    if mode == "nn":
        (m, k), n = a.shape, b.shape[1]
    elif mode == "nt":
        (m, k), n = a.shape, b.shape[0]
    else:
        (k, m), n = a.shape, b.shape[1]
    tm, tn, tk = _tile(m, tm, 8), _tile(n, tn, 128), _tile(k, tk, 128)
    nk = k // tk
    if mode == "tn":
        a_spec = pl.BlockSpec((tk, tm), lambda i, j, kk: (kk, i))
    else:
        a_spec = pl.BlockSpec((tm, tk), lambda i, j, kk: (i, kk))
    if mode == "nt":
        b_spec = pl.BlockSpec((tn, tk), lambda i, j, kk: (j, kk))
    else:
        b_spec = pl.BlockSpec((tk, tn), lambda i, j, kk: (kk, j))
    dims = {"nn": (((1,), (0,)), ((), ())), "nt": (((1,), (1,)), ((), ())), "tn": (((0,), (0,)), ((), ()))}[mode]
    ex_specs = [pl.BlockSpec((tm, tn), functools.partial(lambda i, j, kk, off: (i, j + off), off=off))
                for _, off in extras]
    n_ex, n_out = len(extras), len(out_dtypes)

    def body(*refs):
        a_ref, b_ref = refs[0], refs[1]
        ex_refs = refs[2:2 + n_ex]
        out_refs = refs[2 + n_ex:2 + n_ex + n_out]
        acc = refs[-1]
        kk = pl.program_id(2)

        @pl.when(kk == 0)
        def _():
            acc[...] = jnp.zeros_like(acc)

        acc[...] += lax.dot_general(a_ref[...], b_ref[...], dims, preferred_element_type=F32)

        @pl.when(kk == nk - 1)
        def _():
            res = acc[...]
            res = epilogue(res, *[e[...] for e in ex_refs]) if epilogue is not None else (res,)
            for o_ref, val in zip(out_refs, res):
                o_ref[...] = val.astype(o_ref.dtype)

    outs = pl.pallas_call(
        body, name=name, grid=(m // tm, n // tn, nk),
        in_specs=[a_spec, b_spec] + ex_specs,
        out_specs=[pl.BlockSpec((tm, tn), lambda i, j, kk: (i, j)) for _ in out_dtypes],
        out_shape=[jax.ShapeDtypeStruct((m, n), dt) for dt in out_dtypes],
        scratch_shapes=[pltpu.VMEM((tm, tn), F32)],
        compiler_params=_params(("parallel", "parallel", "arbitrary")),
    )(a, b, *[e for e, _ in extras])
    return outs if n_out > 1 else outs[0]


def _rspec(tr, w, cb=0, off=0):
    return pl.BlockSpec((tr, w), lambda i: (i + off, cb))


def _cspec(shape):
    return pl.BlockSpec(shape, lambda i: (0,) * len(shape))


def _rms(xf):
    rstd = lax.rsqrt(jnp.mean(xf * xf, axis=-1, keepdims=True) + EPS)
    return xf * rstd, rstd


def _rms_bwd(dn, n, rstd):
    return rstd * (dn - n * jnp.mean(dn * n, axis=-1, keepdims=True))


def _colsum(v):
    return jnp.sum(v, axis=0, keepdims=True)


def _total(v):
    return jnp.sum(jnp.sum(v, axis=1, keepdims=True), axis=0, keepdims=True)


def _modulate(x, vec, name):
    rows, d = x.shape
    tr = _tile(rows, 256, 8)

    def body(x_ref, v_ref, a_ref):
        n, _ = _rms(x_ref[...])
        a_ref[...] = ((n * v_ref[0:1, :]) * (1.0 + v_ref[2:3, :]) + v_ref[1:2, :]).astype(BF16)

    return pl.pallas_call(
        body, name=name, grid=(rows // tr,), in_specs=[_rspec(tr, d), _cspec(vec.shape)],
        out_specs=_rspec(tr, d), out_shape=jax.ShapeDtypeStruct((rows, d), BF16),
        compiler_params=_params(("parallel",)))(x, vec)


def _resid_modulate(x, y, vec, name):
    rows, d = x.shape
    tr = _tile(rows, 256, 8)

    def body(x_ref, y_ref, v_ref, x1_ref, a_ref):
        x1 = x_ref[...] + v_ref[3:4, :] * y_ref[...]
        x1_ref[...] = x1
        n, _ = _rms(x1)
        a_ref[...] = ((n * v_ref[0:1, :]) * (1.0 + v_ref[2:3, :]) + v_ref[1:2, :]).astype(BF16)

    return pl.pallas_call(
        body, name=name, grid=(rows // tr,), in_specs=[_rspec(tr, d), _rspec(tr, d), _cspec(vec.shape)],
        out_specs=[_rspec(tr, d), _rspec(tr, d)],
        out_shape=[jax.ShapeDtypeStruct((rows, d), F32), jax.ShapeDtypeStruct((rows, d), BF16)],
        compiler_params=_params(("parallel",)))(x, y, vec)


def _loss_head(x1, f, target, vec, name):
    rows, d = x1.shape
    tr = _tile(rows, 256, 8)

    def body(x1_ref, f_ref, t_ref, v_ref, dx2_ref, df_ref, acc_ref):
        @pl.when(pl.program_id(0) == 0)
        def _():
            acc_ref[...] = jnp.zeros_like(acc_ref)

        gate, gain = v_ref[0:1, :], v_ref[1:2, :]
        fv = f_ref[...]
        n, rstd = _rms(x1_ref[...] + gate * fv)
        err = n * gain - t_ref[...]
        dy = err / d
        dx2 = _rms_bwd(dy * gain, n, rstd)
        dx2_ref[...] = dx2
        df_ref[...] = (dx2 * gate).astype(BF16)
        acc_ref[0:1, :] += _colsum(dx2 * fv)
        acc_ref[1:2, :] += _colsum(dy * n)
        acc_ref[2:3, :] += _colsum(err * err)

    return pl.pallas_call(
        body, name=name, grid=(rows // tr,),
        in_specs=[_rspec(tr, d), _rspec(tr, d), _rspec(tr, d), _cspec(vec.shape)],
        out_specs=[_rspec(tr, d), _rspec(tr, d), _cspec((8, d))],
        out_shape=[jax.ShapeDtypeStruct((rows, d), F32), jax.ShapeDtypeStruct((rows, d), BF16),
                   jax.ShapeDtypeStruct((8, d), F32)],
        compiler_params=_params(("arbitrary",)))(x1, f, target, vec)


def _modulate_bwd(da, x, vec, name, da_off=0, dx_in=None, y=None):
    rows, d = x.shape
    tr = _tile(rows, 256, 8)
    off = da_off // tr
    has_dx, has_y = dx_in is not None, y is not None

    def body(*refs):
        da_ref, x_ref, v_ref = refs[0], refs[1], refs[2]
        pos = 3
        dxin_ref = refs[pos] if has_dx else None
        pos += has_dx
        y_ref = refs[pos] if has_y else None
        pos += has_y
        dx_ref = refs[pos]
        dy_ref = refs[pos + 1] if has_y else None
        acc_ref = refs[-1]

        @pl.when(pl.program_id(0) == 0)
        def _():
            acc_ref[...] = jnp.zeros_like(acc_ref)

        gain, scale1 = v_ref[0:1, :], 1.0 + v_ref[2:3, :]
        dav = da_ref[...]
        n, rstd = _rms(x_ref[...])
        dx = _rms_bwd(dav * gain * scale1, n, rstd)
        if has_dx:
            dx = dx + dxin_ref[...]
        dx_ref[...] = dx
        acc_ref[0:1, :] += _colsum(dav)
        acc_ref[1:2, :] += _colsum(dav * (n * gain))
        acc_ref[2:3, :] += _colsum(dav * n * scale1)
        if has_y:
            acc_ref[3:4, :] += _colsum(dx * y_ref[...])
            dy_ref[...] = (dx * v_ref[3:4, :]).astype(BF16)

    ins = [da, x, vec] + ([dx_in] if has_dx else []) + ([y] if has_y else [])
    in_specs = [_rspec(tr, d, off=off), _rspec(tr, d), _cspec(vec.shape)] + [_rspec(tr, d)] * (has_dx + has_y)
    out_specs = [_rspec(tr, d)] + ([_rspec(tr, d)] if has_y else []) + [_cspec((8, d))]
    out_shape = ([jax.ShapeDtypeStruct((rows, d), F32)] + ([jax.ShapeDtypeStruct((rows, d), BF16)] if has_y else [])
                 + [jax.ShapeDtypeStruct((8, d), F32)])
    return pl.pallas_call(
        body, name=name, grid=(rows // tr,), in_specs=in_specs, out_specs=out_specs, out_shape=out_shape,
        compiler_params=_params(("arbitrary",)))(*ins)


def _conv_terms(cc, cx, w_ref, tr):
    t = lax.broadcasted_iota(jnp.int32, (tr, 1), 0) % GRID_W
    first, last = t == 0, t == GRID_W - 1
    u = cc * cx
    prev = jnp.where(first, 0.0, pltpu.roll(u, 1, 0))
    nxt = jnp.where(last, 0.0, pltpu.roll(u, tr - 1, 0))
    y = w_ref[0:1, :] * prev + w_ref[1:2, :] * u + w_ref[2:3, :] * nxt
    return u, prev, nxt, y, first, last


def _conv_fwd(z, conv_w, rows, name):
    d = conv_w.shape[1]
    tr = _tile(rows, 256, GRID_W)

    def body(cb_ref, cc_ref, cx_ref, w_ref, out_ref):
        y = _conv_terms(cc_ref[...], cx_ref[...], w_ref, tr)[3]
        out_ref[...] = (cb_ref[...] * y).astype(BF16)

    return pl.pallas_call(
        body, name=name, grid=(rows // tr,),
        in_specs=[_rspec(tr, d, CB), _rspec(tr, d, CC), _rspec(tr, d, CX), _cspec(conv_w.shape)],
        out_specs=_rspec(tr, d), out_shape=jax.ShapeDtypeStruct((rows, d), BF16),
        compiler_params=_params(("parallel",)))(z, z, z, conv_w)


def _conv_bwd(dpc, z, conv_w, name):
    rows, d = dpc.shape
    tr = _tile(rows, 256, GRID_W)

    def body(dpc_ref, cb_ref, cc_ref, cx_ref, w_ref, dz_ref, acc_ref):
        @pl.when(pl.program_id(0) == 0)
        def _():
            acc_ref[...] = jnp.zeros_like(acc_ref)

        cc, cx, dpcv = cc_ref[...], cx_ref[...], dpc_ref[...]
        u, prev, nxt, y, first, last = _conv_terms(cc, cx, w_ref, tr)
        dy = dpcv * cb_ref[...]
        dy_next = jnp.where(last, 0.0, pltpu.roll(dy, tr - 1, 0))
        dy_prev = jnp.where(first, 0.0, pltpu.roll(dy, 1, 0))
        du = w_ref[0:1, :] * dy_next + w_ref[1:2, :] * dy + w_ref[2:3, :] * dy_prev
        dz_ref[:, 0:d] = (dpcv * y).astype(BF16)
        dz_ref[:, d:2 * d] = (du * cx).astype(BF16)
        dz_ref[:, 2 * d:3 * d] = (du * cc).astype(BF16)
        acc_ref[0:1, :] += _colsum(dy * prev)
        acc_ref[1:2, :] += _colsum(dy * u)
        acc_ref[2:3, :] += _colsum(dy * nxt)

    return pl.pallas_call(
        body, name=name, grid=(rows // tr,),
        in_specs=[_rspec(tr, d), _rspec(tr, d, CB), _rspec(tr, d, CC), _rspec(tr, d, CX), _cspec(conv_w.shape)],
        out_specs=[_rspec(tr, 3 * d), _cspec((8, d))],
        out_shape=[jax.ShapeDtypeStruct((rows, 3 * d), BF16), jax.ShapeDtypeStruct((8, d), F32)],
        compiler_params=_params(("arbitrary",)))(dpc, z, z, z, conv_w)


def _rotary_fwd(z, cos, sin, n_lat, name):
    rows = z.shape[0]
    d = z.shape[1] // N_IN
    dk = d // HEADS
    half = dk // 2
    tr = _tile(n_lat, 256, 8)
    tr = _tile(rows - n_lat, tr, 8)
    lat_blocks = n_lat // tr

    def body(q_ref, k_ref, v0_ref, v1_ref, cos_ref, sin_ref, qo_ref, ko_ref, vo_ref):
        cs, sn = cos_ref[...], sin_ref[...]
        keep = jnp.where(pl.program_id(0) < lat_blocks, 1.0, 0.0)
        for src, dst, scale in ((q_ref, qo_ref, keep), (k_ref, ko_ref, dk ** -0.5)):
            for h in range(HEADS):
                lo, mid, hi = h * dk, h * dk + half, (h + 1) * dk
                t1, t2 = src[:, lo:mid], src[:, mid:hi]
                dst[:, lo:mid] = ((t1 * cs - t2 * sn) * scale).astype(BF16)
                dst[:, mid:hi] = ((t1 * sn + t2 * cs) * scale).astype(BF16)
        vo_ref[:, 0:d] = v0_ref[...].astype(BF16)
        vo_ref[:, d:2 * d] = v1_ref[...].astype(BF16)

    return pl.pallas_call(
        body, name=name, grid=(rows // tr,),
        in_specs=[_rspec(tr, d, CQ), _rspec(tr, d, CK), _rspec(tr, d, CV), _rspec(tr, d, CV + 1),
                  _rspec(tr, half), _rspec(tr, half)],
        out_specs=[_rspec(tr, d), _rspec(tr, d), _rspec(tr, 2 * d)],
        out_shape=[jax.ShapeDtypeStruct((rows, d), BF16), jax.ShapeDtypeStruct((rows, d), BF16),
                   jax.ShapeDtypeStruct((rows, 2 * d), BF16)],
        compiler_params=_params(("parallel",)))(z, z, z, z, cos, sin)


def _qkv_bwd(dq_f, dq_b, dk_f, dk_b, dv_f, dv_b, cos, sin, n_lat, n_ctx, name):
    rows = n_lat + n_ctx
    d = dq_f.shape[1]
    dk = d // HEADS
    half = dk // 2
    tr = _tile(n_ctx, 128, 8)
    lat_blocks, ctx_blocks = n_lat // tr, n_ctx // tr

    def fmap(i):
        return (jnp.where(i < lat_blocks, i + ctx_blocks, i - lat_blocks), 0)

    def bmap(i):
        return (i + ctx_blocks, 0)

    def body(qf_ref, qb_ref, kf_ref, kb_ref, vf_ref, vb_ref, cos_ref, sin_ref, qo_ref, ko_ref, vo_ref):
        cs, sn = cos_ref[...], sin_ref[...]
        keep = jnp.where(pl.program_id(0) < lat_blocks, 1.0, 0.0)
        for fa, fb, dst, scale in ((qf_ref, qb_ref, qo_ref, keep), (kf_ref, kb_ref, ko_ref, dk ** -0.5)):
            for h in range(HEADS):
                lo, mid, hi = h * dk, h * dk + half, (h + 1) * dk
                d1 = fa[:, lo:mid] + fb[:, lo:mid]
                d2 = fa[:, mid:hi] + fb[:, mid:hi]
                dst[:, lo:mid] = ((d1 * cs + d2 * sn) * scale).astype(BF16)
                dst[:, mid:hi] = ((d2 * cs - d1 * sn) * scale).astype(BF16)
        vo_ref[...] = (vf_ref[...] + vb_ref[...]).astype(BF16)

    return pl.pallas_call(
        body, name=name, grid=(rows // tr,),
        in_specs=[pl.BlockSpec((tr, d), fmap), pl.BlockSpec((tr, d), bmap), pl.BlockSpec((tr, d), fmap),
                  pl.BlockSpec((tr, d), bmap), pl.BlockSpec((tr, 2 * d), fmap), pl.BlockSpec((tr, 2 * d), bmap),
                  _rspec(tr, half), _rspec(tr, half)],
        out_specs=[_rspec(tr, d), _rspec(tr, d), _rspec(tr, 2 * d)],
        out_shape=[jax.ShapeDtypeStruct((rows, d), BF16), jax.ShapeDtypeStruct((rows, d), BF16),
                   jax.ShapeDtypeStruct((rows, 2 * d), BF16)],
        compiler_params=_params(("parallel",)))(dq_f, dq_b, dk_f, dk_b, dv_f, dv_b, cos, sin)


def _gn_fwd(o_f, o_b, z, rows, row_off, name):
    d = z.shape[1] // N_IN
    dv = 2 * d // HEADS
    tr = _tile(row_off, 128, 8)
    off = row_off // tr

    def body(of_ref, ob_ref, g0_ref, g1_ref, r_ref):
        for h in range(HEADS):
            lo, hi = h * dv, (h + 1) * dv
            g_ref, glo = (g0_ref, lo) if hi <= d else (g1_ref, lo - d)
            o = of_ref[:, lo:hi] + ob_ref[:, lo:hi]
            cen = o - jnp.mean(o, axis=-1, keepdims=True)
            on = cen * lax.rsqrt(jnp.mean(cen * cen, axis=-1, keepdims=True) + EPS)
            g = g_ref[:, glo:glo + dv]
            r_ref[:, lo:hi] = (g * _sig(g) * on).astype(BF16)

    return pl.pallas_call(
        body, name=name, grid=(rows // tr,),
        in_specs=[_rspec(tr, 2 * d, off=off), _rspec(tr, 2 * d, off=off), _rspec(tr, d, CG), _rspec(tr, d, CG + 1)],
        out_specs=_rspec(tr, 2 * d), out_shape=jax.ShapeDtypeStruct((rows, 2 * d), BF16),
        compiler_params=_params(("parallel",)))(o_f, o_b, z, z)


def _gn_bwd(dr, o_f, o_b, z, row_off, name):
    rows = dr.shape[0]
    d = z.shape[1] // N_IN
    dv = 2 * d // HEADS
    tr = _tile(row_off, 128, 8)
    off = row_off // tr

    def body(dr_ref, of_ref, ob_ref, g0_ref, g1_ref, do_ref, dg_ref):
        for h in range(HEADS):
            lo, hi = h * dv, (h + 1) * dv
            g_ref, glo = (g0_ref, lo) if hi <= d else (g1_ref, lo - d)
            o = of_ref[:, lo:hi] + ob_ref[:, lo:hi]
            cen = o - jnp.mean(o, axis=-1, keepdims=True)
            rstd = lax.rsqrt(jnp.mean(cen * cen, axis=-1, keepdims=True) + EPS)
            on = cen * rstd
            g = g_ref[:, glo:glo + dv]
            sg = _sig(g)
            drv = dr_ref[:, lo:hi]
            dg_ref[:, lo:hi] = (drv * on * (sg * (1.0 + g * (1.0 - sg)))).astype(BF16)
            don = drv * (g * sg)
            do = rstd * (don - jnp.mean(don, axis=-1, keepdims=True)
                         - on * jnp.mean(don * on, axis=-1, keepdims=True))
            do_ref[:, lo:hi] = do.astype(BF16)

    return pl.pallas_call(
        body, name=name, grid=(rows // tr,),
        in_specs=[_rspec(tr, 2 * d), _rspec(tr, 2 * d, off=off), _rspec(tr, 2 * d, off=off),
                  _rspec(tr, d, CG), _rspec(tr, d, CG + 1)],
        out_specs=[_rspec(tr, 2 * d), _rspec(tr, 2 * d)],
        out_shape=[jax.ShapeDtypeStruct((rows, 2 * d), BF16), jax.ShapeDtypeStruct((rows, 2 * d), BF16)],
        compiler_params=_params(("parallel",)))(dr, o_f, o_b, z, z)


def _decays(lg, rev):
    row = lax.broadcasted_iota(jnp.int32, (CHUNK, CHUNK), 0)
    col = lax.broadcasted_iota(jnp.int32, (CHUNK, CHUNK), 1)
    rel = ((col - row) if rev else (row - col)).astype(F32)
    mask = jnp.where(rel >= 0, jnp.exp(lg * jnp.maximum(rel, 0.0)), 0.0)
    r = lax.broadcasted_iota(jnp.int32, (CHUNK, 1), 0)
    rr = ((CHUNK - 1 - r) if rev else r).astype(F32)
    chunk_decay = jnp.exp(lg * jnp.full((1, 1), float(CHUNK), F32))
    return rr, mask, jnp.exp(lg * (rr + 1.0)), jnp.exp(lg * (CHUNK - 1.0 - rr)), chunk_decay


_NT = (((1,), (1,)), ((), ()))
_TN = (((0,), (0,)), ((), ()))


def _dot(a, b, dims=None):
    if dims is None:
        return jnp.dot(a, b, preferred_element_type=F32)
    return lax.dot_general(a, b, dims, preferred_element_type=F32)


def _ret_fwd(q, k, v, lg, rev, name):
    tt, d = q.shape
    dk, dv = d // HEADS, 2 * d // HEADS
    nc = tt // CHUNK

    def cmap(h, i):
        return ((nc - 1 - i) if rev else i, h)

    def body(lg_ref, q_ref, k_ref, v_ref, o_ref, s_ref, state):
        h = pl.program_id(0)

        @pl.when(pl.program_id(1) == 0)
        def _():
            state[...] = jnp.zeros_like(state)

        lgh = lg_ref[h]
        _, mask, qd, kd, cd = _decays(lgh, rev)
        qv, kv, vv = q_ref[...], k_ref[...], v_ref[...]
        st = state[...]
        p = _dot(qv, kv, _NT) * mask
        o_ref[...] = _dot(p.astype(BF16), vv) + _dot((qv * qd).astype(BF16), st.astype(BF16))
        s_ref[...] = st
        state[...] = cd * st + _dot((kv * kd).astype(BF16), vv, _TN)

    return pl.pallas_call(
        body, name=name, grid=(HEADS, nc),
        in_specs=[pl.BlockSpec(memory_space=pltpu.SMEM), pl.BlockSpec((CHUNK, dk), cmap),
                  pl.BlockSpec((CHUNK, dk), cmap), pl.BlockSpec((CHUNK, dv), cmap)],
        out_specs=[pl.BlockSpec((CHUNK, dv), cmap),
                   pl.BlockSpec((None, None, dk, dv), lambda h, i: (cmap(h, i)[0], h, 0, 0))],
        out_shape=[jax.ShapeDtypeStruct((tt, 2 * d), F32), jax.ShapeDtypeStruct((nc, HEADS, dk, dv), F32)],
        scratch_shapes=[pltpu.VMEM((dk, dv), F32)],
        compiler_params=_params(("parallel", "arbitrary")))(lg, q, k, v)


def _ret_bwd(q, k, v, do, states, lg, rev, name):
    tt, d = q.shape
    dk, dv = d // HEADS, 2 * d // HEADS
    nc = tt // CHUNK

    def cmap(h, i):
        return (i if rev else (nc - 1 - i), h)

    def body(lg_ref, q_ref, k_ref, v_ref, do_ref, s_ref, dq_ref, dk_ref, dv_ref, dlg_ref, dstate):
        h = pl.program_id(0)

        @pl.when(pl.program_id(1) == 0)
        def _():
            dstate[...] = jnp.zeros_like(dstate)
            dlg_ref[...] = jnp.zeros_like(dlg_ref)

        lgh = lg_ref[h]
        rr, mask, qd, kd, cd = _decays(lgh, rev)
        qv, kv, vv, dov = q_ref[...], k_ref[...], v_ref[...], do_ref[...]
        st, dst = s_ref[...], dstate[...]
        st_b, dst_b = st.astype(BF16), dst.astype(BF16)
        p = _dot(qv, kv, _NT) * mask
        da = (_dot(dov, vv, _NT) * mask).astype(BF16)
        dq = _dot(da, kv) + _dot(dov, st_b, _NT) * qd
        dkk = _dot(da, qv, _TN) + _dot(vv, dst_b, _NT) * kd
        dq_ref[...] = dq
        dk_ref[...] = dkk
        dv_ref[...] = _dot(p.astype(BF16), dov, _TN) + _dot((kv * kd).astype(BF16), dst_b)
        dnew = cd * dst + _dot((qv * qd).astype(BF16), dov, _TN)
        dstate[...] = dnew
        qf, kf = qv.astype(F32), kv.astype(F32)
        rows = rr * (jnp.sum(qf * dq, axis=1, keepdims=True) - jnp.sum(kf * dkk, axis=1, keepdims=True))
        dlg_ref[...] += _total(rows) + CHUNK * _total(st * dnew)

    return pl.pallas_call(
        body, name=name, grid=(HEADS, nc),
        in_specs=[pl.BlockSpec(memory_space=pltpu.SMEM), pl.BlockSpec((CHUNK, dk), cmap),
                  pl.BlockSpec((CHUNK, dk), cmap), pl.BlockSpec((CHUNK, dv), cmap), pl.BlockSpec((CHUNK, dv), cmap),
                  pl.BlockSpec((None, None, dk, dv), lambda h, i: (cmap(h, i)[0], h, 0, 0))],
        out_specs=[pl.BlockSpec((CHUNK, dk), cmap), pl.BlockSpec((CHUNK, dk), cmap), pl.BlockSpec((CHUNK, dv), cmap),
                   pl.BlockSpec((None, 8, 128), lambda h, i: (h, 0, 0))],
        out_shape=[jax.ShapeDtypeStruct((tt, d), F32), jax.ShapeDtypeStruct((tt, d), F32),
                   jax.ShapeDtypeStruct((tt, 2 * d), F32), jax.ShapeDtypeStruct((HEADS, 8, 128), F32)],
        scratch_shapes=[pltpu.VMEM((dk, dv), F32)],
        compiler_params=_params(("parallel", "arbitrary")))(lg, q, k, v, do, states)


def _silu(v):
    return v * _sig(v)


def _mod_proj(cs, w_mod, b_loc, dec, name):
    nb = w_mod.shape[1]

    def body(cs_ref, w_ref, b_ref, dec_ref, out_ref, lg_ref):
        out_ref[...] = jnp.dot(_silu(cs_ref[...]), w_ref[...], preferred_element_type=F32, precision=HIGHEST) + b_ref[...]
        a = dec_ref[...]
        lg_ref[...] = jnp.minimum(a, 0.0) - jnp.log1p(jnp.exp(-jnp.abs(a)))

    return pl.pallas_call(
        body, name=name,
        out_shape=[jax.ShapeDtypeStruct((16, nb), F32), jax.ShapeDtypeStruct(dec.shape, F32)],
        compiler_params=_params())(cs, w_mod, b_loc, dec)


def _mod_grad(cs, dm, w_mod, name):
    d, nb = w_mod.shape

    def body(cs_ref, dm_ref, w_ref, gw_ref, part_ref):
        dmv = dm_ref[...]
        gw_ref[...] = lax.dot_general(_silu(cs_ref[...]), dmv, _TN, preferred_element_type=F32, precision=HIGHEST)
        part_ref[...] = lax.dot_general(dmv, w_ref[...], _NT, preferred_element_type=F32, precision=HIGHEST)

    return pl.pallas_call(
        body, name=name,
        out_shape=[jax.ShapeDtypeStruct((d, nb), F32), jax.ShapeDtypeStruct((16, d), F32)],
        compiler_params=_params())(cs, dm, w_mod)


def _reduce_small(gathered, dec, n_feat, name):
    _, rows, d = gathered.shape

    def body(g_ref, dec_ref, red_ref, misc_ref):
        total = g_ref[0]
        for i in range(1, N_DEV):
            total = total + g_ref[i]
        red_ref[...] = total
        misc_ref[...] = jnp.zeros_like(misc_ref)
        misc_ref[0:1, :] = jnp.zeros((1, 128), F32) + (0.5 / n_feat) * _total(total[16:17, :])
        misc_ref[1:3, :] = total[14:16, 0:128] * _sig(-dec_ref[0:2, :])

    return pl.pallas_call(
        body, name=name,
        out_shape=[jax.ShapeDtypeStruct((rows, d), F32), jax.ShapeDtypeStruct((8, 128), F32)],
        compiler_params=_params())(gathered, dec)


def _c_ctx_grad(parts, c_ctx, name):
    d = c_ctx.shape[1]

    def body(p_ref, c_ref, out_ref):
        total = p_ref[0]
        for i in range(1, N_DEV):
            total = total + p_ref[i]
        cv = c_ref[...]
        sg = _sig(cv)
        out_ref[...] = total[8:9, :] * (sg * (1.0 + cv * (1.0 - sg)))

    return pl.pallas_call(body, name=name, out_shape=jax.ShapeDtypeStruct((1, d), F32),
                          compiler_params=_params())(parts, c_ctx)


def _adamw(parts, w, m, v, name):
    n_parts, rows, cols = parts.shape
    row_bytes = cols * (parts.dtype.itemsize * n_parts + 7 * 4)
    tr = _tile(rows, max(16, (4 * 1024 * 1024) // row_bytes), 16 if rows % 16 == 0 else 8)

    def body(p_ref, w_ref, m_ref, v_ref, g_ref, d_ref, mo_ref, vo_ref):
        g = p_ref[0].astype(F32)
        for i in range(1, n_parts):
            g = g + p_ref[i].astype(F32)
        m2 = ADAM_B1 * m_ref[...] + (1.0 - ADAM_B1) * g
        v2 = ADAM_B2 * v_ref[...] + (1.0 - ADAM_B2) * jnp.square(g)
        m_hat = m2 / (1.0 - ADAM_B1 ** ADAM_STEP)
        v_hat = v2 / (1.0 - ADAM_B2 ** ADAM_STEP)
        g_ref[...] = g
        d_ref[...] = -ADAM_LR * (m_hat / (jnp.sqrt(v_hat) + ADAM_EPS) + ADAM_WD * w_ref[...])
        mo_ref[...] = m2
        vo_ref[...] = v2

    spec = _rspec(tr, cols)
    return pl.pallas_call(
        body, name=name, grid=(rows // tr,),
        in_specs=[pl.BlockSpec((n_parts, tr, cols), lambda i: (0, i, 0)), spec, spec, spec],
        out_specs=[spec] * 4, out_shape=[jax.ShapeDtypeStruct((rows, cols), F32)] * 4,
        compiler_params=_params(("parallel",)))(parts, w, m, v)


def _rope_tables(pos, dk):
    half = dk // 2
    inv_freq = 1.0 / (ROPE_BASE ** jnp.linspace(0.0, 1.0, half, dtype=F32))
    ang = pos[:, None] * inv_freq[None, :]
    return jnp.cos(ang), jnp.sin(ang)


def _pad_lanes(v, width):
    return jnp.pad(v, ((0, 0), (0, width - v.shape[1])))


def kernel(x, c, ctx, c_ctx, w_mod, b_mod, norm1_g, w_in, conv_w, w_conv_out, ret_decay_fwd, ret_decay_bwd, w_ret_out, w_o, norm2_g, w_ff1, w_ff2, final_g, loss_target, m_c_ctx, m_w_mod, m_b_mod, m_norm1_g, m_w_in, m_conv_w, m_w_conv_out, m_ret_decay_fwd, m_ret_decay_bwd, m_w_ret_out, m_w_o, m_norm2_g, m_w_ff1, m_w_ff2, m_final_g, v_c_ctx, v_w_mod, v_b_mod, v_norm1_g, v_w_in, v_conv_w, v_w_conv_out, v_ret_decay_fwd, v_ret_decay_bwd, v_w_ret_out, v_w_o, v_norm2_g, v_w_ff1, v_w_ff2, v_final_g):
    n_lat, d = x.shape[1], x.shape[2]
    n_ctx = ctx.shape[1]
    dk = d // HEADS
    nb = w_mod.shape[2]
    me = 4 * lax.axis_index("x") + 2 * lax.axis_index("y") + lax.axis_index("c")
    xl, ctxl, target = x[0], ctx[0], loss_target[0]

    w_in_f = _allgather(w_in[0].astype(BF16), 1, "ag_w_in")
    w_conv_out_f = _allgather(w_conv_out[0].astype(BF16), 0, "ag_w_conv_out")
    w_ret_out_f = _allgather(w_ret_out[0].astype(BF16), 0, "ag_w_ret_out")
    w_o_f = _allgather(w_o[0].astype(BF16), 0, "ag_w_o")
    w_ff1_f = _allgather(w_ff1[0].astype(BF16), 1, "ag_w_ff1")
    w_ff2_f = _allgather(w_ff2[0].astype(BF16), 0, "ag_w_ff2")
    conv_w_f = _small_allgather(conv_w[0], "ag_conv_w").transpose(1, 0, 2).reshape(3, d)

    c_all = _small_allgather(c, "ag_c").reshape(N_DEV, d)
    cs = jnp.concatenate([c_all, c_ctx[None], jnp.zeros((7, d), F32)], axis=0)
    dec = jnp.pad(jnp.concatenate([ret_decay_fwd, ret_decay_bwd], axis=0), ((0, 6), (0, 128 - HEADS)))
    b_loc = lax.dynamic_slice(b_mod, (0, me * nb), (1, nb))
    modp, lgs = _mod_proj(cs, w_mod[0], b_loc, dec, "mod_proj")
    modp_all = _small_allgather(modp, "ag_mod")
    mod_l = lax.dynamic_index_in_dim(modp_all, me, axis=1, keepdims=False).reshape(N_MOD, d)
    mod_c = modp_all[:, 8, :].reshape(N_MOD, d)
    lg_f, lg_b = lgs[0, :HEADS], lgs[1, :HEADS]
    zero_row = jnp.zeros((1, d), F32)
    vec1_l = jnp.concatenate([norm1_g, mod_l[0:1], mod_l[1:2], zero_row], axis=0)
    vec1_c = jnp.concatenate([norm1_g, mod_c[0:1], mod_c[1:2], zero_row], axis=0)
    vec2 = jnp.concatenate([norm2_g, mod_l[3:4], mod_l[4:5], mod_l[2:3]], axis=0)
    vec3 = jnp.concatenate([mod_l[5:6], final_g[None]], axis=0)

    a_all = jnp.concatenate([_modulate(xl, vec1_l, "modulate1"), _modulate(ctxl, vec1_c, "modulate1_ctx")], axis=0)
    z = _mm(a_all, w_in_f, "nn", [F32], "proj_in")
    pc = _conv_fwd(z, conv_w_f, n_lat, "conv_fwd")
    pos = jnp.concatenate([n_ctx + jnp.arange(n_lat, dtype=F32), jnp.arange(n_ctx, dtype=F32)])
    cos, sin = _rope_tables(pos, dk)
    q_m, k_m, v_m = _rotary_fwd(z, cos, sin, n_lat, "rotary_fwd")
    q_s, k_s, v_s = (jnp.concatenate([t[n_lat:], t], axis=0) for t in (q_m, k_m, v_m))
    o_f, st_f = _ret_fwd(q_s, k_s, v_s, lg_f, False, "ret_fwd_f")
    o_b, st_b = _ret_fwd(q_s, k_s, v_s, lg_b, True, "ret_fwd_b")
    r = _gn_fwd(o_f, o_b, z, n_lat, n_ctx, "gn_fwd")
    y_conv = _mm(pc, w_conv_out_f, "nn", [F32], "proj_conv_out", tn=2048)
    tn_d = _tile(d, 1024, 128)
    gate_offs = (CGC * d // tn_d, CGR * d // tn_d)

    def merge(acc, yc, gc, gr):
        return acc, _sig(gc) * yc + _sig(gr) * acc

    y_ret, mg = _mm(r, w_ret_out_f, "nn", [F32, BF16], "proj_ret_out", tm=512, tn=tn_d,
                    extras=[(y_conv, 0), (z, gate_offs[0]), (z, gate_offs[1])], epilogue=merge)
    y_l = _mm(mg, w_o_f, "nn", [F32], "proj_o", tn=2048)
    x1, a2 = _resid_modulate(xl, y_l, vec2, "resid_modulate2")

    def sqrelu(acc):
        return acc, jnp.square(jnp.maximum(acc, 0.0))

    hff, s = _mm(a2, w_ff1_f, "nn", [F32, BF16], "ff1", epilogue=sqrelu)
    f = _mm(s, w_ff2_f, "nn", [F32], "ff2", tn=2048)
    dx2, df, acc3 = _loss_head(x1, f, target, vec3, "loss_head")

    def d_sqrelu(acc, h):
        return (acc * (2.0 * jnp.maximum(h, 0.0)),)

    dh = _mm(df, w_ff2_f, "nt", [BF16], "ff2_dx", extras=[(hff, 0)], epilogue=d_sqrelu)
    g_ff2 = _mm(s, df, "tn", [BF16], "ff2_dw")
    da2 = _mm(dh, w_ff1_f, "nt", [F32], "ff1_dx", tn=2048)
    g_ff1 = _mm(a2, dh, "tn", [BF16], "ff1_dw")
    dx1, dyl, acc2 = _modulate_bwd(da2, x1, vec2, "modulate2_bwd", dx_in=dx2, y=y_l)

    def d_merge(acc, yc, yr, gc, gr):
        sc, sr = _sig(gc), _sig(gr)
        return acc * sc, acc * sr, acc * yc * (sc * (1.0 - sc)), acc * yr * (sr * (1.0 - sr))

    dyc, dyr, dgc, dgr = _mm(dyl, w_o_f, "nt", [BF16] * 4, "proj_o_dx", tm=512, tn=tn_d,
                             extras=[(y_conv, 0), (y_ret, 0), (z, gate_offs[0]), (z, gate_offs[1])], epilogue=d_merge)
    g_o = _mm(mg, dyl, "tn", [BF16], "proj_o_dw")
    dpc = _mm(dyc, w_conv_out_f, "nt", [F32], "proj_conv_out_dx", tn=2048)
    g_conv_out = _mm(pc, dyc, "tn", [BF16], "proj_conv_out_dw")
    dz_conv, acc_conv = _conv_bwd(dpc, z, conv_w_f, "conv_bwd")
    dr = _mm(dyr, w_ret_out_f, "nt", [F32], "proj_ret_out_dx")
    g_ret_out = _mm(r, dyr, "tn", [BF16], "proj_ret_out_dw")
    do, dz_g = _gn_bwd(dr, o_f, o_b, z, n_ctx, "gn_bwd")
    do_s = jnp.pad(do, ((n_ctx, n_ctx), (0, 0)))
    dq_f, dk_f, dv_f, dlg_f = _ret_bwd(q_s, k_s, v_s, do_s, st_f, lg_f, False, "ret_bwd_f")
    dq_b, dk_b, dv_b, dlg_b = _ret_bwd(q_s, k_s, v_s, do_s, st_b, lg_b, True, "ret_bwd_b")
    dz_q, dz_k, dz_v = _qkv_bwd(dq_f, dq_b, dk_f, dk_b, dv_f, dv_b, cos, sin, n_lat, n_ctx, "qkv_bwd")

    def with_ctx_rows(t):
        return jnp.pad(t, ((0, n_ctx), (0, 0)))

    dz = jnp.concatenate([with_ctx_rows(dz_conv), dz_q, dz_k, dz_v, with_ctx_rows(dz_g), with_ctx_rows(dgc),
                          with_ctx_rows(dgr)], axis=1)
    da_all = _mm(dz, w_in_f, "nt", [F32], "proj_in_dx", tn=2048)
    g_in = _mm(a_all, dz, "tn", [BF16], "proj_in_dw", tk=768)
    grad_x, acc1 = _modulate_bwd(da_all, xl, vec1_l, "modulate1_bwd", dx_in=dx1)
    _, acc1c = _modulate_bwd(da_all, ctxl, vec1_c, "modulate1_ctx_bwd", da_off=n_lat)

    lane_pad = functools.partial(_pad_lanes, width=d)
    packet = jnp.concatenate([
        acc1[2:3] + acc1c[2:3], acc2[2:3], acc3[1:2],
        acc1[0:1], acc1[1:2], acc2[3:4], acc2[0:1], acc2[1:2], acc3[0:1],
        acc1c[0:1], acc1c[1:2],
        acc_conv[0:3],
        lane_pad(dlg_f[:, 0, 0][None]), lane_pad(dlg_b[:, 0, 0][None]),
        acc3[2:3],
        jnp.zeros((7, d), F32)], axis=0)
    gathered = _small_allgather(packet, "ag_small")
    red, misc = _reduce_small(gathered, dec, d, "reduce_small")
    dmod_ctx = jnp.concatenate([red[9], red[10], jnp.zeros((4 * d,), F32)])
    dmod_all = jnp.concatenate([gathered[:, 3:9, :].reshape(N_DEV, N_MOD * d), dmod_ctx[None]], axis=0)
    dm = jnp.pad(lax.dynamic_slice(dmod_all, (0, me * nb), (N_DEV + 1, nb)), ((0, 7), (0, 0)))
    g_mod, c_part = _mod_grad(cs, dm, w_mod[0], "mod_grad")
    g_c_ctx = _c_ctx_grad(_small_allgather(c_part, "ag_c_ctx"), c_ctx[None], "c_ctx_grad")
    g_b_mod = red[3:9].reshape(1, N_MOD * d) + dmod_ctx[None]
    g_conv_w = lax.dynamic_slice(red[11:14], (0, me * (d // N_DEV)), (3, d // N_DEV))

    def big(g_full, axis, w, m, v, name):
        parts = _scatter_exchange(g_full, axis, "rs_" + name)
        return _adamw(parts, w[0], m[0], v[0], "adamw_" + name)

    res = {
        "w_mod": _adamw(g_mod[None], w_mod[0], m_w_mod[0], v_w_mod[0], "adamw_w_mod"),
        "w_in": big(g_in, 1, w_in, m_w_in, v_w_in, "w_in"),
        "w_conv_out": big(g_conv_out, 0, w_conv_out, m_w_conv_out, v_w_conv_out, "w_conv_out"),
        "w_ret_out": big(g_ret_out, 0, w_ret_out, m_w_ret_out, v_w_ret_out, "w_ret_out"),
        "w_o": big(g_o, 0, w_o, m_w_o, v_w_o, "w_o"),
        "w_ff1": big(g_ff1, 1, w_ff1, m_w_ff1, v_w_ff1, "w_ff1"),
        "w_ff2": big(g_ff2, 0, w_ff2, m_w_ff2, v_w_ff2, "w_ff2"),
    }
    res = {k: tuple(t[None] for t in val) for k, val in res.items()}

    small = [("c_ctx", g_c_ctx, c_ctx, m_c_ctx, v_c_ctx), ("b_mod", g_b_mod, b_mod, m_b_mod, v_b_mod),
             ("norm1_g", red[0:1], norm1_g, m_norm1_g, v_norm1_g), ("conv_w", g_conv_w, conv_w, m_conv_w, v_conv_w),
             ("ret_decay_fwd", misc[1:2, :HEADS], ret_decay_fwd, m_ret_decay_fwd, v_ret_decay_fwd),
             ("ret_decay_bwd", misc[2:3, :HEADS], ret_decay_bwd, m_ret_decay_bwd, v_ret_decay_bwd),
             ("norm2_g", red[1:2], norm2_g, m_norm2_g, v_norm2_g), ("final_g", red[2:3], final_g, m_final_g, v_final_g)]

    def flat(t):
        t = t.reshape(-1)
        return jnp.pad(t, (0, (-t.shape[0]) % 1024))

    packed = [jnp.concatenate([flat(item[j]) for item in small]).reshape(-1, 128) for j in range(1, 5)]
    outs = _adamw(packed[0][None], packed[1], packed[2], packed[3], "adamw_small")
    start = 0
    for name, _, w, _, _ in small:
        size = w.size
        res[name] = tuple(o.reshape(-1)[start:start + size].reshape(w.shape) for o in outs)
        start += size + (-size) % 1024

    order = ["c_ctx", "w_mod", "b_mod", "norm1_g", "w_in", "conv_w", "w_conv_out", "ret_decay_fwd", "ret_decay_bwd",
             "w_ret_out", "w_o", "norm2_g", "w_ff1", "w_ff2", "final_g"]
    loss = misc[0, 0]
    return (loss, grad_x[None], *[res[n][0] for n in order], *[res[n][1] for n in order],
            *[res[n][2] for n in order], *[res[n][3] for n in order])
```

```python
import functools

import jax
import jax.numpy as jnp
from jax import lax
from jax.experimental import pallas as pl
from jax.experimental.pallas import tpu as pltpu

F32 = jnp.float32
BF16 = jnp.bfloat16
MESH = pl.DeviceIdType.MESH

N_DEV = 8
HEADS = 8
N_MOD = 6
N_IN = 11
GRID_W = 64
CHUNK = 128
ROPE_BASE = 10000.0
EPS = 1e-6
ADAM_LR, ADAM_B1, ADAM_B2, ADAM_EPS, ADAM_WD, ADAM_STEP = 0.001, 0.9, 0.999, 1e-08, 0.01, 10
VMEM_LIMIT = 56 * 1024 * 1024
HIGHEST = lax.Precision.HIGHEST
CB, CC, CX, CQ, CK, CV, CG, CGC, CGR = 0, 1, 2, 3, 4, 5, 7, 9, 10


def _tile(n, target, mult):
    t = (min(target, n) // mult) * mult
    while t >= mult:
        if n % t == 0:
            return t
        t -= mult
    return n


def _params(sem=None):
    return pltpu.CompilerParams(dimension_semantics=sem, vmem_limit_bytes=VMEM_LIMIT)


def _sig(v):
    return 1.0 / (1.0 + jnp.exp(-v))


def _coords():
    return lax.axis_index("x"), lax.axis_index("y"), lax.axis_index("c")


def _flip(p, m):
    return tuple(1 - v if (m >> s) & 1 else v for v, s in zip(p, (2, 1, 0)))


def _index(p):
    return 4 * p[0] + 2 * p[1] + p[2]


def _small_allgather(x, name):
    r, n = x.shape

    def body(x_ref, out_ref, send_sems, recv_sems):
        me = _coords()
        out_ref[pl.ds(_index(me), 1)] = x_ref[...][None]
        sent = []
        for m in range(1, N_DEV):
            cp = pltpu.make_async_remote_copy(
                src_ref=x_ref, dst_ref=out_ref.at[_index(me)], send_sem=send_sems.at[m - 1],
                recv_sem=recv_sems.at[m - 1], device_id=_flip(me, m), device_id_type=MESH)
            cp.start()
            sent.append(cp)
        for m in range(1, N_DEV):
            pltpu.make_async_remote_copy(
                src_ref=x_ref, dst_ref=out_ref.at[_index(_flip(me, m))], send_sem=send_sems.at[m - 1],
                recv_sem=recv_sems.at[m - 1], device_id=_flip(me, m), device_id_type=MESH).wait_recv()
        for cp in sent:
            cp.wait_send()

    return pl.pallas_call(
        body, name=name, out_shape=jax.ShapeDtypeStruct((N_DEV, r, n), x.dtype),
        in_specs=[pl.BlockSpec(memory_space=pltpu.VMEM)], out_specs=pl.BlockSpec(memory_space=pltpu.VMEM),
        scratch_shapes=[pltpu.SemaphoreType.DMA((N_DEV - 1,)), pltpu.SemaphoreType.DMA((N_DEV - 1,))],
    )(x)


def _window(ref, j, r, c, axis):
    if axis == 0:
        return ref.at[pl.ds(pl.multiple_of(j * r, 8), r), :]
    return ref.at[:, pl.ds(pl.multiple_of(j * c, 128), c)]


def _allgather(x, axis, name):
    r, c = x.shape
    full = (N_DEV * r, c) if axis == 0 else (r, N_DEV * c)

    def body(x_ref, out_ref, send_sems, recv_sems, local_sem):
        me = _coords()
        sibling = _flip(me, 1)
        chips = [4, 2, 6]

        def copy(k, block, to, src=None):
            dst = _window(out_ref, _index(block), r, c, axis)
            return pltpu.make_async_remote_copy(
                src_ref=dst if src is None else src, dst_ref=dst, send_sem=send_sems.at[k],
                recv_sem=recv_sems.at[k], device_id=to, device_id_type=MESH)

        mine = pltpu.make_async_copy(x_ref, _window(out_ref, _index(me), r, c, axis), local_sem)
        mine.start()
        first = [copy(0, me, sibling, src=x_ref)]
        first += [copy(1 + j, me, _flip(me, m), src=x_ref) for j, m in enumerate(chips)]
        for cp in first:
            cp.start()
        passed = [copy(4 + j, _flip(me, m), sibling) for j, m in enumerate(chips)]
        for j, m in enumerate(chips):
            copy(1 + j, _flip(me, m), me).wait_recv()
            passed[j].start()
        copy(0, sibling, me).wait_recv()
        for j, m in enumerate(chips):
            copy(4 + j, _flip(sibling, m), me).wait_recv()
        for cp in first + passed:
            cp.wait_send()
        mine.wait()

    return pl.pallas_call(
        body, name=name, out_shape=jax.ShapeDtypeStruct(full, x.dtype),
        in_specs=[pl.BlockSpec(memory_space=pl.ANY)], out_specs=pl.BlockSpec(memory_space=pl.ANY),
        scratch_shapes=[pltpu.SemaphoreType.DMA((N_DEV - 1,)), pltpu.SemaphoreType.DMA((N_DEV - 1,)),
                        pltpu.SemaphoreType.DMA(())],
    )(x)


def _scatter_exchange(g, axis, name):
    if axis == 0:
        r, c = g.shape[0] // N_DEV, g.shape[1]
    else:
        r, c = g.shape[0], g.shape[1] // N_DEV

    def body(g_ref, out_ref, send_sems, recv_sems, local_sem):
        me = _coords()
        mine = pltpu.make_async_copy(_window(g_ref, _index(me), r, c, axis), out_ref.at[_index(me)], local_sem)
        mine.start()
        sent = []
        for m in range(1, N_DEV):
            peer = _flip(me, m)
            cp = pltpu.make_async_remote_copy(
                src_ref=_window(g_ref, _index(peer), r, c, axis), dst_ref=out_ref.at[_index(me)],
                send_sem=send_sems.at[m - 1], recv_sem=recv_sems.at[m - 1], device_id=peer, device_id_type=MESH)
            cp.start()
            sent.append(cp)
        for m in range(1, N_DEV):
            peer = _flip(me, m)
            pltpu.make_async_remote_copy(
                src_ref=_window(g_ref, _index(me), r, c, axis), dst_ref=out_ref.at[_index(peer)],
                send_sem=send_sems.at[m - 1], recv_sem=recv_sems.at[m - 1], device_id=peer,
                device_id_type=MESH).wait_recv()
        for cp in sent:
            cp.wait_send()
        mine.wait()

    return pl.pallas_call(
        body, name=name, out_shape=jax.ShapeDtypeStruct((N_DEV, r, c), g.dtype),
        in_specs=[pl.BlockSpec(memory_space=pl.ANY)], out_specs=pl.BlockSpec(memory_space=pl.ANY),
        scratch_shapes=[pltpu.SemaphoreType.DMA((N_DEV - 1,)), pltpu.SemaphoreType.DMA((N_DEV - 1,)),
                        pltpu.SemaphoreType.DMA(())],
    )(g)


def _mm(a, b, mode, out_dtypes, name, tm=1024, tn=1024, tk=2048, extras=(), epilogue=None):
    if mode == "nn":
        (m, k), n = a.shape, b.shape[1]
    elif mode == "nt":
        (m, k), n = a.shape, b.shape[0]
    else:
        (k, m), n = a.shape, b.shape[1]
    tm, tn = _tile(m, tm, 8), _tile(n, tn, 128)
    tk = _tile(k, tk, 16 if mode == "tn" else 128)
    nk = k // tk
    swap = nk == 1 and (k * n + (n // tn) * m * k) < (m * k + (m // tm) * k * n)

    def ij(p, q):
        return (q, p) if swap else (p, q)

    def spec(shape, fn):
        return pl.BlockSpec(shape, lambda p, q, kk: fn(*ij(p, q), kk))

    a_spec = spec((tk, tm), lambda i, j, kk: (kk, i)) if mode == "tn" else spec((tm, tk), lambda i, j, kk: (i, kk))
    b_spec = spec((tn, tk), lambda i, j, kk: (j, kk)) if mode == "nt" else spec((tk, tn), lambda i, j, kk: (kk, j))
    dims = {"nn": (((1,), (0,)), ((), ())), "nt": (((1,), (1,)), ((), ())), "tn": (((0,), (0,)), ((), ()))}[mode]
    ex_specs = [spec((tm, tn), functools.partial(lambda i, j, kk, off: (i, j + off), off=off)) for _, off in extras]
    n_ex, n_out = len(extras), len(out_dtypes)

    def body(*refs):
        a_ref, b_ref = refs[0], refs[1]
        ex_refs = refs[2:2 + n_ex]
        out_refs = refs[2 + n_ex:2 + n_ex + n_out]

        def product():
            return lax.dot_general(a_ref[...], b_ref[...], dims, preferred_element_type=F32)

        def finish(res):
            res = epilogue(res, *[e[...] for e in ex_refs]) if epilogue is not None else (res,)
            for o_ref, val in zip(out_refs, res):
                o_ref[...] = val.astype(o_ref.dtype)

        if nk == 1:
            finish(product())
            return
        acc = refs[-1]
        kk = pl.program_id(2)

        @pl.when(kk == 0)
        def _():
            acc[...] = product()

        @pl.when((kk > 0) & (kk < nk - 1))
        def _():
            acc[...] += product()

        @pl.when(kk == nk - 1)
        def _():
            finish(acc[...] + product())

    outs = pl.pallas_call(
        body, name=name, grid=(*ij(m // tm, n // tn), nk),
        in_specs=[a_spec, b_spec] + ex_specs,
        out_specs=[spec((tm, tn), lambda i, j, kk: (i, j)) for _ in out_dtypes],
        out_shape=[jax.ShapeDtypeStruct((m, n), dt) for dt in out_dtypes],
        scratch_shapes=[pltpu.VMEM((tm, tn), F32)] if nk > 1 else [],
        compiler_params=_params(("parallel", "parallel", "arbitrary")),
    )(a, b, *[e for e, _ in extras])
    return outs if n_out > 1 else outs[0]


def _rspec(tr, w, cb=0, off=0):
    return pl.BlockSpec((tr, w), lambda i: (i + off, cb))


def _cspec(shape):
    return pl.BlockSpec(shape, lambda i: (0,) * len(shape))


def _rms(xf):
    rstd = lax.rsqrt(jnp.mean(xf * xf, axis=-1, keepdims=True) + EPS)
    return xf * rstd, rstd


def _rms_bwd(dn, n, rstd):
    return rstd * (dn - n * jnp.mean(dn * n, axis=-1, keepdims=True))


def _colsum(v):
    return jnp.sum(v, axis=0, keepdims=True)


def _total(v):
    return jnp.sum(jnp.sum(v, axis=1, keepdims=True), axis=0, keepdims=True)


def _modulate(x, vec, name):
    rows, d = x.shape
    tr = _tile(rows, 256, 8)

    def body(x_ref, v_ref, a_ref):
        n, _ = _rms(x_ref[...])
        a_ref[...] = ((n * v_ref[0:1, :]) * (1.0 + v_ref[2:3, :]) + v_ref[1:2, :]).astype(BF16)

    return pl.pallas_call(
        body, name=name, grid=(rows // tr,), in_specs=[_rspec(tr, d), _cspec(vec.shape)],
        out_specs=_rspec(tr, d), out_shape=jax.ShapeDtypeStruct((rows, d), BF16),
        compiler_params=_params(("parallel",)))(x, vec)


def _resid_modulate(x, y, vec, name):
    rows, d = x.shape
    tr = _tile(rows, 256, 8)

    def body(x_ref, y_ref, v_ref, x1_ref, a_ref):
        x1 = x_ref[...] + v_ref[3:4, :] * y_ref[...]
        x1_ref[...] = x1
        n, _ = _rms(x1)
        a_ref[...] = ((n * v_ref[0:1, :]) * (1.0 + v_ref[2:3, :]) + v_ref[1:2, :]).astype(BF16)

    return pl.pallas_call(
        body, name=name, grid=(rows // tr,), in_specs=[_rspec(tr, d), _rspec(tr, d), _cspec(vec.shape)],
        out_specs=[_rspec(tr, d), _rspec(tr, d)],
        out_shape=[jax.ShapeDtypeStruct((rows, d), F32), jax.ShapeDtypeStruct((rows, d), BF16)],
        compiler_params=_params(("parallel",)))(x, y, vec)


def _loss_head(x1, f, target, vec, name):
    rows, d = x1.shape
    tr = _tile(rows, 256, 8)

    def body(x1_ref, f_ref, t_ref, v_ref, dx2_ref, df_ref, acc_ref):
        @pl.when(pl.program_id(0) == 0)
        def _():
            acc_ref[...] = jnp.zeros_like(acc_ref)

        gate, gain = v_ref[0:1, :], v_ref[1:2, :]
        fv = f_ref[...]
        n, rstd = _rms(x1_ref[...] + gate * fv)
        err = n * gain - t_ref[...]
        dy = err / d
        dx2 = _rms_bwd(dy * gain, n, rstd)
        dx2_ref[...] = dx2
        df_ref[...] = (dx2 * gate).astype(BF16)
        acc_ref[0:1, :] += _colsum(dx2 * fv)
        acc_ref[1:2, :] += _colsum(dy * n)
        acc_ref[2:3, :] += _colsum(err * err)

    return pl.pallas_call(
        body, name=name, grid=(rows // tr,),
        in_specs=[_rspec(tr, d), _rspec(tr, d), _rspec(tr, d), _cspec(vec.shape)],
        out_specs=[_rspec(tr, d), _rspec(tr, d), _cspec((8, d))],
        out_shape=[jax.ShapeDtypeStruct((rows, d), F32), jax.ShapeDtypeStruct((rows, d), BF16),
                   jax.ShapeDtypeStruct((8, d), F32)],
        compiler_params=_params(("arbitrary",)))(x1, f, target, vec)


def _modulate_bwd(da, x, vec, name, da_off=0, dx_in=None, y=None):
    rows, d = x.shape
    tr = _tile(rows, 256, 8)
    off = da_off // tr
    has_dx, has_y = dx_in is not None, y is not None

    def body(*refs):
        da_ref, x_ref, v_ref = refs[0], refs[1], refs[2]
        pos = 3
        dxin_ref = refs[pos] if has_dx else None
        pos += has_dx
        y_ref = refs[pos] if has_y else None
        pos += has_y
        dx_ref = refs[pos]
        dy_ref = refs[pos + 1] if has_y else None
        acc_ref = refs[-1]

        @pl.when(pl.program_id(0) == 0)
        def _():
            acc_ref[...] = jnp.zeros_like(acc_ref)

        gain, scale1 = v_ref[0:1, :], 1.0 + v_ref[2:3, :]
        dav = da_ref[...]
        n, rstd = _rms(x_ref[...])
        dx = _rms_bwd(dav * gain * scale1, n, rstd)
        if has_dx:
            dx = dx + dxin_ref[...]
        dx_ref[...] = dx
        acc_ref[0:1, :] += _colsum(dav)
        acc_ref[1:2, :] += _colsum(dav * (n * gain))
        acc_ref[2:3, :] += _colsum(dav * n * scale1)
        if has_y:
            acc_ref[3:4, :] += _colsum(dx * y_ref[...])
            dy_ref[...] = (dx * v_ref[3:4, :]).astype(BF16)

    ins = [da, x, vec] + ([dx_in] if has_dx else []) + ([y] if has_y else [])
    in_specs = [_rspec(tr, d, off=off), _rspec(tr, d), _cspec(vec.shape)] + [_rspec(tr, d)] * (has_dx + has_y)
    out_specs = [_rspec(tr, d)] + ([_rspec(tr, d)] if has_y else []) + [_cspec((8, d))]
    out_shape = ([jax.ShapeDtypeStruct((rows, d), F32)] + ([jax.ShapeDtypeStruct((rows, d), BF16)] if has_y else [])
                 + [jax.ShapeDtypeStruct((8, d), F32)])
    return pl.pallas_call(
        body, name=name, grid=(rows // tr,), in_specs=in_specs, out_specs=out_specs, out_shape=out_shape,
        compiler_params=_params(("arbitrary",)))(*ins)


def _conv_terms(cc, cx, w_ref, tr):
    t = lax.broadcasted_iota(jnp.int32, (tr, 1), 0) % GRID_W
    first, last = t == 0, t == GRID_W - 1
    u = cc * cx
    prev = jnp.where(first, 0.0, pltpu.roll(u, 1, 0))
    nxt = jnp.where(last, 0.0, pltpu.roll(u, tr - 1, 0))
    y = w_ref[0:1, :] * prev + w_ref[1:2, :] * u + w_ref[2:3, :] * nxt
    return u, prev, nxt, y, first, last


def _conv_fwd(z, conv_w, rows, name):
    d = conv_w.shape[1]
    tr = _tile(rows, 256, GRID_W)

    def body(cb_ref, cc_ref, cx_ref, w_ref, out_ref):
        y = _conv_terms(cc_ref[...], cx_ref[...], w_ref, tr)[3]
        out_ref[...] = (cb_ref[...] * y).astype(BF16)

    return pl.pallas_call(
        body, name=name, grid=(rows // tr,),
        in_specs=[_rspec(tr, d, CB), _rspec(tr, d, CC), _rspec(tr, d, CX), _cspec(conv_w.shape)],
        out_specs=_rspec(tr, d), out_shape=jax.ShapeDtypeStruct((rows, d), BF16),
        compiler_params=_params(("parallel",)))(z, z, z, conv_w)


def _conv_bwd(dpc, z, conv_w, name):
    rows, d = dpc.shape
    tr = _tile(rows, 256, GRID_W)

    def body(dpc_ref, cb_ref, cc_ref, cx_ref, w_ref, dz_ref, acc_ref):
        @pl.when(pl.program_id(0) == 0)
        def _():
            acc_ref[...] = jnp.zeros_like(acc_ref)

        cc, cx, dpcv = cc_ref[...], cx_ref[...], dpc_ref[...]
        u, prev, nxt, y, first, last = _conv_terms(cc, cx, w_ref, tr)
        dy = dpcv * cb_ref[...]
        dy_next = jnp.where(last, 0.0, pltpu.roll(dy, tr - 1, 0))
        dy_prev = jnp.where(first, 0.0, pltpu.roll(dy, 1, 0))
        du = w_ref[0:1, :] * dy_next + w_ref[1:2, :] * dy + w_ref[2:3, :] * dy_prev
        dz_ref[:, 0:d] = (dpcv * y).astype(BF16)
        dz_ref[:, d:2 * d] = (du * cx).astype(BF16)
        dz_ref[:, 2 * d:3 * d] = (du * cc).astype(BF16)
        acc_ref[0:1, :] += _colsum(dy * prev)
        acc_ref[1:2, :] += _colsum(dy * u)
        acc_ref[2:3, :] += _colsum(dy * nxt)

    return pl.pallas_call(
        body, name=name, grid=(rows // tr,),
        in_specs=[_rspec(tr, d), _rspec(tr, d, CB), _rspec(tr, d, CC), _rspec(tr, d, CX), _cspec(conv_w.shape)],
        out_specs=[_rspec(tr, 3 * d), _cspec((8, d))],
        out_shape=[jax.ShapeDtypeStruct((rows, 3 * d), BF16), jax.ShapeDtypeStruct((8, d), F32)],
        compiler_params=_params(("arbitrary",)))(dpc, z, z, z, conv_w)


def _rotary_fwd(z, cos, sin, n_lat, name):
    rows = z.shape[0]
    d = z.shape[1] // N_IN
    dk = d // HEADS
    half = dk // 2
    tr = _tile(n_lat, 256, 8)
    tr = _tile(rows - n_lat, tr, 8)
    lat_blocks = n_lat // tr

    def body(q_ref, k_ref, v0_ref, v1_ref, cos_ref, sin_ref, qo_ref, ko_ref, vo_ref):
        cs, sn = cos_ref[...], sin_ref[...]
        keep = jnp.where(pl.program_id(0) < lat_blocks, 1.0, 0.0)
        for src, dst, scale in ((q_ref, qo_ref, keep), (k_ref, ko_ref, dk ** -0.5)):
            for h in range(HEADS):
                lo, mid, hi = h * dk, h * dk + half, (h + 1) * dk
                t1, t2 = src[:, lo:mid], src[:, mid:hi]
                dst[:, lo:mid] = ((t1 * cs - t2 * sn) * scale).astype(BF16)
                dst[:, mid:hi] = ((t1 * sn + t2 * cs) * scale).astype(BF16)
        vo_ref[:, 0:d] = v0_ref[...].astype(BF16)
        vo_ref[:, d:2 * d] = v1_ref[...].astype(BF16)

    return pl.pallas_call(
        body, name=name, grid=(rows // tr,),
        in_specs=[_rspec(tr, d, CQ), _rspec(tr, d, CK), _rspec(tr, d, CV), _rspec(tr, d, CV + 1),
                  _rspec(tr, half), _rspec(tr, half)],
        out_specs=[_rspec(tr, d), _rspec(tr, d), _rspec(tr, 2 * d)],
        out_shape=[jax.ShapeDtypeStruct((rows, d), BF16), jax.ShapeDtypeStruct((rows, d), BF16),
                   jax.ShapeDtypeStruct((rows, 2 * d), BF16)],
        compiler_params=_params(("parallel",)))(z, z, z, z, cos, sin)


def _qkv_bwd(dq_f, dq_b, dk_f, dk_b, dv_f, dv_b, cos, sin, n_lat, n_ctx, name):
    rows = n_lat + n_ctx
    d = dq_f.shape[1]
    dk = d // HEADS
    half = dk // 2
    tr = _tile(n_ctx, 128, 8)
    lat_blocks, ctx_blocks = n_lat // tr, n_ctx // tr

    def fmap(i):
        return (jnp.where(i < lat_blocks, i + ctx_blocks, i - lat_blocks), 0)

    def bmap(i):
        return (i + ctx_blocks, 0)

    def body(qf_ref, qb_ref, kf_ref, kb_ref, vf_ref, vb_ref, cos_ref, sin_ref, qo_ref, ko_ref, vo_ref):
        cs, sn = cos_ref[...], sin_ref[...]
        keep = jnp.where(pl.program_id(0) < lat_blocks, 1.0, 0.0)
        for fa, fb, dst, scale in ((qf_ref, qb_ref, qo_ref, keep), (kf_ref, kb_ref, ko_ref, dk ** -0.5)):
            for h in range(HEADS):
                lo, mid, hi = h * dk, h * dk + half, (h + 1) * dk
                d1 = fa[:, lo:mid] + fb[:, lo:mid]
                d2 = fa[:, mid:hi] + fb[:, mid:hi]
                dst[:, lo:mid] = ((d1 * cs + d2 * sn) * scale).astype(BF16)
                dst[:, mid:hi] = ((d2 * cs - d1 * sn) * scale).astype(BF16)
        vo_ref[...] = (vf_ref[...] + vb_ref[...]).astype(BF16)

    return pl.pallas_call(
        body, name=name, grid=(rows // tr,),
        in_specs=[pl.BlockSpec((tr, d), fmap), pl.BlockSpec((tr, d), bmap), pl.BlockSpec((tr, d), fmap),
                  pl.BlockSpec((tr, d), bmap), pl.BlockSpec((tr, 2 * d), fmap), pl.BlockSpec((tr, 2 * d), bmap),
                  _rspec(tr, half), _rspec(tr, half)],
        out_specs=[_rspec(tr, d), _rspec(tr, d), _rspec(tr, 2 * d)],
        out_shape=[jax.ShapeDtypeStruct((rows, d), BF16), jax.ShapeDtypeStruct((rows, d), BF16),
                   jax.ShapeDtypeStruct((rows, 2 * d), BF16)],
        compiler_params=_params(("parallel",)))(dq_f, dq_b, dk_f, dk_b, dv_f, dv_b, cos, sin)


def _gn_fwd(o_f, o_b, z, rows, row_off, name):
    d = z.shape[1] // N_IN
    dv = 2 * d // HEADS
    tr = _tile(row_off, 128, 8)
    off = row_off // tr

    def body(of_ref, ob_ref, g0_ref, g1_ref, r_ref):
        for h in range(HEADS):
            lo, hi = h * dv, (h + 1) * dv
            g_ref, glo = (g0_ref, lo) if hi <= d else (g1_ref, lo - d)
            o = of_ref[:, lo:hi] + ob_ref[:, lo:hi]
            cen = o - jnp.mean(o, axis=-1, keepdims=True)
            on = cen * lax.rsqrt(jnp.mean(cen * cen, axis=-1, keepdims=True) + EPS)
            g = g_ref[:, glo:glo + dv]
            r_ref[:, lo:hi] = (g * _sig(g) * on).astype(BF16)

    return pl.pallas_call(
        body, name=name, grid=(rows // tr,),
        in_specs=[_rspec(tr, 2 * d, off=off), _rspec(tr, 2 * d, off=off), _rspec(tr, d, CG), _rspec(tr, d, CG + 1)],
        out_specs=_rspec(tr, 2 * d), out_shape=jax.ShapeDtypeStruct((rows, 2 * d), BF16),
        compiler_params=_params(("parallel",)))(o_f, o_b, z, z)


def _gn_bwd(dr, o_f, o_b, z, row_off, name):
    rows = dr.shape[0]
    d = z.shape[1] // N_IN
    dv = 2 * d // HEADS
    tr = _tile(row_off, 128, 8)
    off = row_off // tr

    def body(dr_ref, of_ref, ob_ref, g0_ref, g1_ref, do_ref, dg_ref):
        for h in range(HEADS):
            lo, hi = h * dv, (h + 1) * dv
            g_ref, glo = (g0_ref, lo) if hi <= d else (g1_ref, lo - d)
            o = of_ref[:, lo:hi] + ob_ref[:, lo:hi]
            cen = o - jnp.mean(o, axis=-1, keepdims=True)
            rstd = lax.rsqrt(jnp.mean(cen * cen, axis=-1, keepdims=True) + EPS)
            on = cen * rstd
            g = g_ref[:, glo:glo + dv]
            sg = _sig(g)
            drv = dr_ref[:, lo:hi]
            dg_ref[:, lo:hi] = (drv * on * (sg * (1.0 + g * (1.0 - sg)))).astype(BF16)
            don = drv * (g * sg)
            do = rstd * (don - jnp.mean(don, axis=-1, keepdims=True)
                         - on * jnp.mean(don * on, axis=-1, keepdims=True))
            do_ref[:, lo:hi] = do.astype(BF16)

    return pl.pallas_call(
        body, name=name, grid=(rows // tr,),
        in_specs=[_rspec(tr, 2 * d), _rspec(tr, 2 * d, off=off), _rspec(tr, 2 * d, off=off),
                  _rspec(tr, d, CG), _rspec(tr, d, CG + 1)],
        out_specs=[_rspec(tr, 2 * d), _rspec(tr, 2 * d)],
        out_shape=[jax.ShapeDtypeStruct((rows, 2 * d), BF16), jax.ShapeDtypeStruct((rows, 2 * d), BF16)],
        compiler_params=_params(("parallel",)))(dr, o_f, o_b, z, z)


def _decays(lg, rev):
    row = lax.broadcasted_iota(jnp.int32, (CHUNK, CHUNK), 0)
    col = lax.broadcasted_iota(jnp.int32, (CHUNK, CHUNK), 1)
    rel = ((col - row) if rev else (row - col)).astype(F32)
    mask = jnp.where(rel >= 0, jnp.exp(lg * jnp.maximum(rel, 0.0)), 0.0)
    r = lax.broadcasted_iota(jnp.int32, (CHUNK, 1), 0)
    rr = ((CHUNK - 1 - r) if rev else r).astype(F32)
    chunk_decay = jnp.exp(lg * jnp.full((1, 1), float(CHUNK), F32))
    return rr, mask, jnp.exp(lg * (rr + 1.0)), jnp.exp(lg * (CHUNK - 1.0 - rr)), chunk_decay


_NT = (((1,), (1,)), ((), ()))
_TN = (((0,), (0,)), ((), ()))


def _dot(a, b, dims=None):
    if dims is None:
        return jnp.dot(a, b, preferred_element_type=F32)
    return lax.dot_general(a, b, dims, preferred_element_type=F32)


def _ret_fwd(q, k, v, lg, rev, name):
    tt, d = q.shape
    dk, dv = d // HEADS, 2 * d // HEADS
    nc = tt // CHUNK

    def cmap(h, i):
        return ((nc - 1 - i) if rev else i, h)

    def body(lg_ref, q_ref, k_ref, v_ref, o_ref, s_ref, state):
        h = pl.program_id(0)

        @pl.when(pl.program_id(1) == 0)
        def _():
            state[...] = jnp.zeros_like(state)

        lgh = lg_ref[h]
        _, mask, qd, kd, cd = _decays(lgh, rev)
        qv, kv, vv = q_ref[...], k_ref[...], v_ref[...]
        st = state[...]
        p = _dot(qv, kv, _NT) * mask
        o_ref[...] = _dot(p.astype(BF16), vv) + _dot((qv * qd).astype(BF16), st.astype(BF16))
        s_ref[...] = st
        state[...] = cd * st + _dot((kv * kd).astype(BF16), vv, _TN)

    return pl.pallas_call(
        body, name=name, grid=(HEADS, nc),
        in_specs=[pl.BlockSpec(memory_space=pltpu.SMEM), pl.BlockSpec((CHUNK, dk), cmap),
                  pl.BlockSpec((CHUNK, dk), cmap), pl.BlockSpec((CHUNK, dv), cmap)],
        out_specs=[pl.BlockSpec((CHUNK, dv), cmap),
                   pl.BlockSpec((None, None, dk, dv), lambda h, i: (cmap(h, i)[0], h, 0, 0))],
        out_shape=[jax.ShapeDtypeStruct((tt, 2 * d), F32), jax.ShapeDtypeStruct((nc, HEADS, dk, dv), F32)],
        scratch_shapes=[pltpu.VMEM((dk, dv), F32)],
        compiler_params=_params(("parallel", "arbitrary")))(lg, q, k, v)


def _ret_bwd(q, k, v, do, states, lg, rev, name):
    tt, d = q.shape
    dk, dv = d // HEADS, 2 * d // HEADS
    nc = tt // CHUNK

    def cmap(h, i):
        return (i if rev else (nc - 1 - i), h)

    def body(lg_ref, q_ref, k_ref, v_ref, do_ref, s_ref, dq_ref, dk_ref, dv_ref, dlg_ref, dstate):
        h = pl.program_id(0)

        @pl.when(pl.program_id(1) == 0)
        def _():
            dstate[...] = jnp.zeros_like(dstate)
            dlg_ref[...] = jnp.zeros_like(dlg_ref)

        lgh = lg_ref[h]
        rr, mask, qd, kd, cd = _decays(lgh, rev)
        qv, kv, vv, dov = q_ref[...], k_ref[...], v_ref[...], do_ref[...]
        st, dst = s_ref[...], dstate[...]
        st_b, dst_b = st.astype(BF16), dst.astype(BF16)
        p = _dot(qv, kv, _NT) * mask
        da = (_dot(dov, vv, _NT) * mask).astype(BF16)
        dq = _dot(da, kv) + _dot(dov, st_b, _NT) * qd
        dkk = _dot(da, qv, _TN) + _dot(vv, dst_b, _NT) * kd
        dq_ref[...] = dq
        dk_ref[...] = dkk
        dv_ref[...] = _dot(p.astype(BF16), dov, _TN) + _dot((kv * kd).astype(BF16), dst_b)
        dnew = cd * dst + _dot((qv * qd).astype(BF16), dov, _TN)
        dstate[...] = dnew
        qf, kf = qv.astype(F32), kv.astype(F32)
        rows = rr * (jnp.sum(qf * dq, axis=1, keepdims=True) - jnp.sum(kf * dkk, axis=1, keepdims=True))
        dlg_ref[...] += _total(rows) + CHUNK * _total(st * dnew)

    return pl.pallas_call(
        body, name=name, grid=(HEADS, nc),
        in_specs=[pl.BlockSpec(memory_space=pltpu.SMEM), pl.BlockSpec((CHUNK, dk), cmap),
                  pl.BlockSpec((CHUNK, dk), cmap), pl.BlockSpec((CHUNK, dv), cmap), pl.BlockSpec((CHUNK, dv), cmap),
                  pl.BlockSpec((None, None, dk, dv), lambda h, i: (cmap(h, i)[0], h, 0, 0))],
        out_specs=[pl.BlockSpec((CHUNK, dk), cmap), pl.BlockSpec((CHUNK, dk), cmap), pl.BlockSpec((CHUNK, dv), cmap),
                   pl.BlockSpec((None, 8, 128), lambda h, i: (h, 0, 0))],
        out_shape=[jax.ShapeDtypeStruct((tt, d), F32), jax.ShapeDtypeStruct((tt, d), F32),
                   jax.ShapeDtypeStruct((tt, 2 * d), F32), jax.ShapeDtypeStruct((HEADS, 8, 128), F32)],
        scratch_shapes=[pltpu.VMEM((dk, dv), F32)],
        compiler_params=_params(("parallel", "arbitrary")))(lg, q, k, v, do, states)


def _silu(v):
    return v * _sig(v)


def _mod_proj(cs, w_mod, b_loc, dec, name):
    nb = w_mod.shape[1]

    def body(cs_ref, w_ref, b_ref, dec_ref, out_ref, lg_ref):
        out_ref[...] = jnp.dot(_silu(cs_ref[...]), w_ref[...], preferred_element_type=F32, precision=HIGHEST) + b_ref[...]
        a = dec_ref[...]
        lg_ref[...] = jnp.minimum(a, 0.0) - jnp.log1p(jnp.exp(-jnp.abs(a)))

    return pl.pallas_call(
        body, name=name,
        out_shape=[jax.ShapeDtypeStruct((16, nb), F32), jax.ShapeDtypeStruct(dec.shape, F32)],
        compiler_params=_params())(cs, w_mod, b_loc, dec)


def _mod_grad(cs, dm, w_mod, name):
    d, nb = w_mod.shape

    def body(cs_ref, dm_ref, w_ref, gw_ref, part_ref):
        dmv = dm_ref[...]
        gw_ref[...] = lax.dot_general(_silu(cs_ref[...]), dmv, _TN, preferred_element_type=F32, precision=HIGHEST)
        part_ref[...] = lax.dot_general(dmv, w_ref[...], _NT, preferred_element_type=F32, precision=HIGHEST)

    return pl.pallas_call(
        body, name=name,
        out_shape=[jax.ShapeDtypeStruct((d, nb), F32), jax.ShapeDtypeStruct((16, d), F32)],
        compiler_params=_params())(cs, dm, w_mod)


def _reduce_small(gathered, dec, n_feat, name):
    _, rows, d = gathered.shape

    def body(g_ref, dec_ref, red_ref, misc_ref):
        total = g_ref[0]
        for i in range(1, N_DEV):
            total = total + g_ref[i]
        red_ref[...] = total
        misc_ref[...] = jnp.zeros_like(misc_ref)
        misc_ref[0:1, :] = jnp.zeros((1, 128), F32) + (0.5 / n_feat) * _total(total[16:17, :])
        misc_ref[1:3, :] = total[14:16, 0:128] * _sig(-dec_ref[0:2, :])

    return pl.pallas_call(
        body, name=name,
        out_shape=[jax.ShapeDtypeStruct((rows, d), F32), jax.ShapeDtypeStruct((8, 128), F32)],
        compiler_params=_params())(gathered, dec)


def _c_ctx_grad(parts, c_ctx, name):
    d = c_ctx.shape[1]

    def body(p_ref, c_ref, out_ref):
        total = p_ref[0]
        for i in range(1, N_DEV):
            total = total + p_ref[i]
        cv = c_ref[...]
        sg = _sig(cv)
        out_ref[...] = total[8:9, :] * (sg * (1.0 + cv * (1.0 - sg)))

    return pl.pallas_call(body, name=name, out_shape=jax.ShapeDtypeStruct((1, d), F32),
                          compiler_params=_params())(parts, c_ctx)


def _adamw(parts, w, m, v, name):
    n_parts, rows, cols = parts.shape
    row_bytes = cols * (parts.dtype.itemsize * n_parts + 7 * 4)
    tr = _tile(rows, max(16, (4 * 1024 * 1024) // row_bytes), 16 if rows % 16 == 0 else 8)

    def body(p_ref, w_ref, m_ref, v_ref, g_ref, d_ref, mo_ref, vo_ref):
        g = p_ref[0].astype(F32)
        for i in range(1, n_parts):
            g = g + p_ref[i].astype(F32)
        m2 = ADAM_B1 * m_ref[...] + (1.0 - ADAM_B1) * g
        v2 = ADAM_B2 * v_ref[...] + (1.0 - ADAM_B2) * jnp.square(g)
        m_hat = m2 / (1.0 - ADAM_B1 ** ADAM_STEP)
        v_hat = v2 / (1.0 - ADAM_B2 ** ADAM_STEP)
        g_ref[...] = g
        d_ref[...] = -ADAM_LR * (m_hat / (jnp.sqrt(v_hat) + ADAM_EPS) + ADAM_WD * w_ref[...])
        mo_ref[...] = m2
        vo_ref[...] = v2

    spec = _rspec(tr, cols)
    return pl.pallas_call(
        body, name=name, grid=(rows // tr,),
        in_specs=[pl.BlockSpec((n_parts, tr, cols), lambda i: (0, i, 0)), spec, spec, spec],
        out_specs=[spec] * 4, out_shape=[jax.ShapeDtypeStruct((rows, cols), F32)] * 4,
        compiler_params=_params(("parallel",)))(parts, w, m, v)


def _rope_tables(pos, dk):
    half = dk // 2
    inv_freq = 1.0 / (ROPE_BASE ** jnp.linspace(0.0, 1.0, half, dtype=F32))
    ang = pos[:, None] * inv_freq[None, :]
    return jnp.cos(ang), jnp.sin(ang)


def _pad_lanes(v, width):
    return jnp.pad(v, ((0, 0), (0, width - v.shape[1])))


def kernel(x, c, ctx, c_ctx, w_mod, b_mod, norm1_g, w_in, conv_w, w_conv_out, ret_decay_fwd, ret_decay_bwd, w_ret_out, w_o, norm2_g, w_ff1, w_ff2, final_g, loss_target, m_c_ctx, m_w_mod, m_b_mod, m_norm1_g, m_w_in, m_conv_w, m_w_conv_out, m_ret_decay_fwd, m_ret_decay_bwd, m_w_ret_out, m_w_o, m_norm2_g, m_w_ff1, m_w_ff2, m_final_g, v_c_ctx, v_w_mod, v_b_mod, v_norm1_g, v_w_in, v_conv_w, v_w_conv_out, v_ret_decay_fwd, v_ret_decay_bwd, v_w_ret_out, v_w_o, v_norm2_g, v_w_ff1, v_w_ff2, v_final_g):
    n_lat, d = x.shape[1], x.shape[2]
    n_ctx = ctx.shape[1]
    dk = d // HEADS
    nb = w_mod.shape[2]
    me = 4 * lax.axis_index("x") + 2 * lax.axis_index("y") + lax.axis_index("c")
    xl, ctxl, target = x[0], ctx[0], loss_target[0]

    w_in_f = _allgather(w_in[0].astype(BF16), 1, "ag_w_in")
    w_conv_out_f = _allgather(w_conv_out[0].astype(BF16), 0, "ag_w_conv_out")
    w_ret_out_f = _allgather(w_ret_out[0].astype(BF16), 0, "ag_w_ret_out")
    w_o_f = _allgather(w_o[0].astype(BF16), 0, "ag_w_o")
    w_ff1_f = _allgather(w_ff1[0].astype(BF16), 1, "ag_w_ff1")
    w_ff2_f = _allgather(w_ff2[0].astype(BF16), 0, "ag_w_ff2")
    conv_w_f = _small_allgather(conv_w[0], "ag_conv_w").transpose(1, 0, 2).reshape(3, d)

    c_all = _small_allgather(c, "ag_c").reshape(N_DEV, d)
    cs = jnp.concatenate([c_all, c_ctx[None], jnp.zeros((7, d), F32)], axis=0)
    dec = jnp.pad(jnp.concatenate([ret_decay_fwd, ret_decay_bwd], axis=0), ((0, 6), (0, 128 - HEADS)))
    b_loc = lax.dynamic_slice(b_mod, (0, me * nb), (1, nb))
    modp, lgs = _mod_proj(cs, w_mod[0], b_loc, dec, "mod_proj")
    modp_all = _small_allgather(modp, "ag_mod")
    mod_l = lax.dynamic_index_in_dim(modp_all, me, axis=1, keepdims=False).reshape(N_MOD, d)
    mod_c = modp_all[:, 8, :].reshape(N_MOD, d)
    lg_f, lg_b = lgs[0, :HEADS], lgs[1, :HEADS]
    zero_row = jnp.zeros((1, d), F32)
    vec1_l = jnp.concatenate([norm1_g, mod_l[0:1], mod_l[1:2], zero_row], axis=0)
    vec1_c = jnp.concatenate([norm1_g, mod_c[0:1], mod_c[1:2], zero_row], axis=0)
    vec2 = jnp.concatenate([norm2_g, mod_l[3:4], mod_l[4:5], mod_l[2:3]], axis=0)
    vec3 = jnp.concatenate([mod_l[5:6], final_g[None]], axis=0)

    a_all = jnp.concatenate([_modulate(xl, vec1_l, "modulate1"), _modulate(ctxl, vec1_c, "modulate1_ctx")], axis=0)
    z = _mm(a_all, w_in_f, "nn", [F32], "proj_in", tn=2048)
    pc = _conv_fwd(z, conv_w_f, n_lat, "conv_fwd")
    pos = jnp.concatenate([n_ctx + jnp.arange(n_lat, dtype=F32), jnp.arange(n_ctx, dtype=F32)])
    cos, sin = _rope_tables(pos, dk)
    q_m, k_m, v_m = _rotary_fwd(z, cos, sin, n_lat, "rotary_fwd")
    q_s, k_s, v_s = (jnp.concatenate([t[n_lat:], t], axis=0) for t in (q_m, k_m, v_m))
    o_f, st_f = _ret_fwd(q_s, k_s, v_s, lg_f, False, "ret_fwd_f")
    o_b, st_b = _ret_fwd(q_s, k_s, v_s, lg_b, True, "ret_fwd_b")
    r = _gn_fwd(o_f, o_b, z, n_lat, n_ctx, "gn_fwd")
    y_conv = _mm(pc, w_conv_out_f, "nn", [F32], "proj_conv_out", tn=2048)
    tn_d = _tile(d, 1024, 128)
    gate_offs = (CGC * d // tn_d, CGR * d // tn_d)

    def merge(acc, yc, gc, gr):
        return acc, _sig(gc) * yc + _sig(gr) * acc

    y_ret, mg = _mm(r, w_ret_out_f, "nn", [F32, BF16], "proj_ret_out", tm=512, tn=tn_d,
                    extras=[(y_conv, 0), (z, gate_offs[0]), (z, gate_offs[1])], epilogue=merge)
    y_l = _mm(mg, w_o_f, "nn", [F32], "proj_o", tn=2048)
    x1, a2 = _resid_modulate(xl, y_l, vec2, "resid_modulate2")

    def sqrelu(acc):
        return acc, jnp.square(jnp.maximum(acc, 0.0))

    hff, s = _mm(a2, w_ff1_f, "nn", [F32, BF16], "ff1", epilogue=sqrelu)
    f = _mm(s, w_ff2_f, "nn", [F32], "ff2")
    dx2, df, acc3 = _loss_head(x1, f, target, vec3, "loss_head")

    def d_sqrelu(acc, h):
        return (acc * (2.0 * jnp.maximum(h, 0.0)),)

    dh = _mm(df, w_ff2_f, "nt", [BF16], "ff2_dx", extras=[(hff, 0)], epilogue=d_sqrelu)
    g_ff2 = _mm(s, df, "tn", [BF16], "ff2_dw")
    da2 = _mm(dh, w_ff1_f, "nt", [F32], "ff1_dx")
    g_ff1 = _mm(a2, dh, "tn", [BF16], "ff1_dw")
    dx1, dyl, acc2 = _modulate_bwd(da2, x1, vec2, "modulate2_bwd", dx_in=dx2, y=y_l)

    def d_merge(acc, yc, yr, gc, gr):
        sc, sr = _sig(gc), _sig(gr)
        return acc * sc, acc * sr, acc * yc * (sc * (1.0 - sc)), acc * yr * (sr * (1.0 - sr))

    dyc, dyr, dgc, dgr = _mm(dyl, w_o_f, "nt", [BF16] * 4, "proj_o_dx", tm=512, tn=tn_d,
                             extras=[(y_conv, 0), (y_ret, 0), (z, gate_offs[0]), (z, gate_offs[1])], epilogue=d_merge)
    g_o = _mm(mg, dyl, "tn", [BF16], "proj_o_dw")
    dpc = _mm(dyc, w_conv_out_f, "nt", [F32], "proj_conv_out_dx", tn=2048)
    g_conv_out = _mm(pc, dyc, "tn", [BF16], "proj_conv_out_dw")
    dz_conv, acc_conv = _conv_bwd(dpc, z, conv_w_f, "conv_bwd")
    dr = _mm(dyr, w_ret_out_f, "nt", [F32], "proj_ret_out_dx")
    g_ret_out = _mm(r, dyr, "tn", [BF16], "proj_ret_out_dw")
    do, dz_g = _gn_bwd(dr, o_f, o_b, z, n_ctx, "gn_bwd")
    do_s = jnp.pad(do, ((n_ctx, n_ctx), (0, 0)))
    dq_f, dk_f, dv_f, dlg_f = _ret_bwd(q_s, k_s, v_s, do_s, st_f, lg_f, False, "ret_bwd_f")
    dq_b, dk_b, dv_b, dlg_b = _ret_bwd(q_s, k_s, v_s, do_s, st_b, lg_b, True, "ret_bwd_b")
    dz_q, dz_k, dz_v = _qkv_bwd(dq_f, dq_b, dk_f, dk_b, dv_f, dv_b, cos, sin, n_lat, n_ctx, "qkv_bwd")

    def with_ctx_rows(t):
        return jnp.pad(t, ((0, n_ctx), (0, 0)))

    dz = jnp.concatenate([with_ctx_rows(dz_conv), dz_q, dz_k, dz_v, with_ctx_rows(dz_g), with_ctx_rows(dgc),
                          with_ctx_rows(dgr)], axis=1)
    da_all = _mm(dz, w_in_f, "nt", [F32], "proj_in_dx", tn=2048)
    g_in = _mm(a_all, dz, "tn", [BF16], "proj_in_dw", tn=2048, tk=(n_lat + n_ctx) // 4)
    grad_x, acc1 = _modulate_bwd(da_all, xl, vec1_l, "modulate1_bwd", dx_in=dx1)
    _, acc1c = _modulate_bwd(da_all, ctxl, vec1_c, "modulate1_ctx_bwd", da_off=n_lat)

    lane_pad = functools.partial(_pad_lanes, width=d)
    packet = jnp.concatenate([
        acc1[2:3] + acc1c[2:3], acc2[2:3], acc3[1:2],
        acc1[0:1], acc1[1:2], acc2[3:4], acc2[0:1], acc2[1:2], acc3[0:1],
        acc1c[0:1], acc1c[1:2],
        acc_conv[0:3],
        lane_pad(dlg_f[:, 0, 0][None]), lane_pad(dlg_b[:, 0, 0][None]),
        acc3[2:3],
        jnp.zeros((7, d), F32)], axis=0)
    gathered = _small_allgather(packet, "ag_small")
    red, misc = _reduce_small(gathered, dec, d, "reduce_small")
    dmod_ctx = jnp.concatenate([red[9], red[10], jnp.zeros((4 * d,), F32)])
    dmod_all = jnp.concatenate([gathered[:, 3:9, :].reshape(N_DEV, N_MOD * d), dmod_ctx[None]], axis=0)
    dm = jnp.pad(lax.dynamic_slice(dmod_all, (0, me * nb), (N_DEV + 1, nb)), ((0, 7), (0, 0)))
    g_mod, c_part = _mod_grad(cs, dm, w_mod[0], "mod_grad")
    g_c_ctx = _c_ctx_grad(_small_allgather(c_part, "ag_c_ctx"), c_ctx[None], "c_ctx_grad")
    g_b_mod = red[3:9].reshape(1, N_MOD * d) + dmod_ctx[None]
    g_conv_w = lax.dynamic_slice(red[11:14], (0, me * (d // N_DEV)), (3, d // N_DEV))

    def big(g_full, axis, w, m, v, name):
        parts = _scatter_exchange(g_full, axis, "rs_" + name)
        return _adamw(parts, w[0], m[0], v[0], "adamw_" + name)

    res = {
        "w_mod": _adamw(g_mod[None], w_mod[0], m_w_mod[0], v_w_mod[0], "adamw_w_mod"),
        "w_in": big(g_in, 1, w_in, m_w_in, v_w_in, "w_in"),
        "w_conv_out": big(g_conv_out, 0, w_conv_out, m_w_conv_out, v_w_conv_out, "w_conv_out"),
        "w_ret_out": big(g_ret_out, 0, w_ret_out, m_w_ret_out, v_w_ret_out, "w_ret_out"),
        "w_o": big(g_o, 0, w_o, m_w_o, v_w_o, "w_o"),
        "w_ff1": big(g_ff1, 1, w_ff1, m_w_ff1, v_w_ff1, "w_ff1"),
        "w_ff2": big(g_ff2, 0, w_ff2, m_w_ff2, v_w_ff2, "w_ff2"),
    }
    res = {k: tuple(t[None] for t in val) for k, val in res.items()}

    small = [("c_ctx", g_c_ctx, c_ctx, m_c_ctx, v_c_ctx), ("b_mod", g_b_mod, b_mod, m_b_mod, v_b_mod),
             ("norm1_g", red[0:1], norm1_g, m_norm1_g, v_norm1_g), ("conv_w", g_conv_w, conv_w, m_conv_w, v_conv_w),
             ("ret_decay_fwd", misc[1:2, :HEADS], ret_decay_fwd, m_ret_decay_fwd, v_ret_decay_fwd),
             ("ret_decay_bwd", misc[2:3, :HEADS], ret_decay_bwd, m_ret_decay_bwd, v_ret_decay_bwd),
             ("norm2_g", red[1:2], norm2_g, m_norm2_g, v_norm2_g), ("final_g", red[2:3], final_g, m_final_g, v_final_g)]

    def flat(t):
        t = t.reshape(-1)
        return jnp.pad(t, (0, (-t.shape[0]) % 1024))

    packed = [jnp.concatenate([flat(item[j]) for item in small]).reshape(-1, 128) for j in range(1, 5)]
    outs = _adamw(packed[0][None], packed[1], packed[2], packed[3], "adamw_small")
    start = 0
    for name, _, w, _, _ in small:
        size = w.size
        res[name] = tuple(o.reshape(-1)[start:start + size].reshape(w.shape) for o in outs)
        start += size + (-size) % 1024

    order = ["c_ctx", "w_mod", "b_mod", "norm1_g", "w_in", "conv_w", "w_conv_out", "ret_decay_fwd", "ret_decay_bwd",
             "w_ret_out", "w_o", "norm2_g", "w_ff1", "w_ff2", "final_g"]
    loss = misc[0, 0]
    return (loss, grad_x[None], *[res[n][0] for n in order], *[res[n][1] for n in order],
            *[res[n][2] for n in order], *[res[n][3] for n in order])
```

```python
import functools

import jax
import jax.numpy as jnp
from jax import lax
from jax.experimental import pallas as pl
from jax.experimental.pallas import tpu as pltpu

F32 = jnp.float32
BF16 = jnp.bfloat16
MESH = pl.DeviceIdType.MESH

N_DEV = 8
HEADS = 8
N_MOD = 6
N_IN = 11
GRID_W = 64
CHUNK = 128
ROPE_BASE = 10000.0
EPS = 1e-6
ADAM_LR, ADAM_B1, ADAM_B2, ADAM_EPS, ADAM_WD, ADAM_STEP = 0.001, 0.9, 0.999, 1e-08, 0.01, 10
VMEM_LIMIT = 56 * 1024 * 1024
HIGHEST = lax.Precision.HIGHEST
CB, CC, CX, CQ, CK, CV, CG, CGC, CGR = 0, 1, 2, 3, 4, 5, 7, 9, 10


def _tile(n, target, mult):
    t = (min(target, n) // mult) * mult
    while t >= mult:
        if n % t == 0:
            return t
        t -= mult
    return n


def _params(sem=None):
    return pltpu.CompilerParams(dimension_semantics=sem, vmem_limit_bytes=VMEM_LIMIT)


def _sig(v):
    return 1.0 / (1.0 + jnp.exp(-v))


def _coords():
    return lax.axis_index("x"), lax.axis_index("y"), lax.axis_index("c")


def _flip(p, m):
    return tuple(1 - v if (m >> s) & 1 else v for v, s in zip(p, (2, 1, 0)))


def _index(p):
    return 4 * p[0] + 2 * p[1] + p[2]


def _small_allgather(x, name):
    r, n = x.shape

    def body(x_ref, out_ref, send_sems, recv_sems):
        me = _coords()
        out_ref[pl.ds(_index(me), 1)] = x_ref[...][None]
        sent = []
        for m in range(1, N_DEV):
            cp = pltpu.make_async_remote_copy(
                src_ref=x_ref, dst_ref=out_ref.at[_index(me)], send_sem=send_sems.at[m - 1],
                recv_sem=recv_sems.at[m - 1], device_id=_flip(me, m), device_id_type=MESH)
            cp.start()
            sent.append(cp)
        for m in range(1, N_DEV):
            pltpu.make_async_remote_copy(
                src_ref=x_ref, dst_ref=out_ref.at[_index(_flip(me, m))], send_sem=send_sems.at[m - 1],
                recv_sem=recv_sems.at[m - 1], device_id=_flip(me, m), device_id_type=MESH).wait_recv()
        for cp in sent:
            cp.wait_send()

    return pl.pallas_call(
        body, name=name, out_shape=jax.ShapeDtypeStruct((N_DEV, r, n), x.dtype),
        in_specs=[pl.BlockSpec(memory_space=pltpu.VMEM)], out_specs=pl.BlockSpec(memory_space=pltpu.VMEM),
        scratch_shapes=[pltpu.SemaphoreType.DMA((N_DEV - 1,)), pltpu.SemaphoreType.DMA((N_DEV - 1,))],
    )(x)


def _window(ref, j, r, c, axis):
    if axis == 0:
        return ref.at[pl.ds(pl.multiple_of(j * r, 8), r), :]
    return ref.at[:, pl.ds(pl.multiple_of(j * c, 128), c)]


def _allgather(x, axis, name):
    r, c = x.shape
    full = (N_DEV * r, c) if axis == 0 else (r, N_DEV * c)

    def body(x_ref, out_ref, send_sems, recv_sems, local_sem):
        me = _coords()
        sibling = _flip(me, 1)
        chips = [4, 2, 6]

        def copy(k, block, to, src=None):
            dst = _window(out_ref, _index(block), r, c, axis)
            return pltpu.make_async_remote_copy(
                src_ref=dst if src is None else src, dst_ref=dst, send_sem=send_sems.at[k],
                recv_sem=recv_sems.at[k], device_id=to, device_id_type=MESH)

        mine = pltpu.make_async_copy(x_ref, _window(out_ref, _index(me), r, c, axis), local_sem)
        mine.start()
        first = [copy(0, me, sibling, src=x_ref)]
        first += [copy(1 + j, me, _flip(me, m), src=x_ref) for j, m in enumerate(chips)]
        for cp in first:
            cp.start()
        passed = [copy(4 + j, _flip(me, m), sibling) for j, m in enumerate(chips)]
        for j, m in enumerate(chips):
            copy(1 + j, _flip(me, m), me).wait_recv()
            passed[j].start()
        copy(0, sibling, me).wait_recv()
        for j, m in enumerate(chips):
            copy(4 + j, _flip(sibling, m), me).wait_recv()
        for cp in first + passed:
            cp.wait_send()
        mine.wait()

    return pl.pallas_call(
        body, name=name, out_shape=jax.ShapeDtypeStruct(full, x.dtype),
        in_specs=[pl.BlockSpec(memory_space=pl.ANY)], out_specs=pl.BlockSpec(memory_space=pl.ANY),
        scratch_shapes=[pltpu.SemaphoreType.DMA((N_DEV - 1,)), pltpu.SemaphoreType.DMA((N_DEV - 1,)),
                        pltpu.SemaphoreType.DMA(())],
    )(x)


def _scatter_exchange(g, axis, name):
    if axis == 0:
        r, c = g.shape[0] // N_DEV, g.shape[1]
    else:
        r, c = g.shape[0], g.shape[1] // N_DEV

    def body(g_ref, out_ref, send_sems, recv_sems, local_sem):
        me = _coords()
        mine = pltpu.make_async_copy(_window(g_ref, _index(me), r, c, axis), out_ref.at[_index(me)], local_sem)
        mine.start()
        sent = []
        for m in range(1, N_DEV):
            peer = _flip(me, m)
            cp = pltpu.make_async_remote_copy(
                src_ref=_window(g_ref, _index(peer), r, c, axis), dst_ref=out_ref.at[_index(me)],
                send_sem=send_sems.at[m - 1], recv_sem=recv_sems.at[m - 1], device_id=peer, device_id_type=MESH)
            cp.start()
            sent.append(cp)
        for m in range(1, N_DEV):
            peer = _flip(me, m)
            pltpu.make_async_remote_copy(
                src_ref=_window(g_ref, _index(me), r, c, axis), dst_ref=out_ref.at[_index(peer)],
                send_sem=send_sems.at[m - 1], recv_sem=recv_sems.at[m - 1], device_id=peer,
                device_id_type=MESH).wait_recv()
        for cp in sent:
            cp.wait_send()
        mine.wait()

    return pl.pallas_call(
        body, name=name, out_shape=jax.ShapeDtypeStruct((N_DEV, r, c), g.dtype),
        in_specs=[pl.BlockSpec(memory_space=pl.ANY)], out_specs=pl.BlockSpec(memory_space=pl.ANY),
        scratch_shapes=[pltpu.SemaphoreType.DMA((N_DEV - 1,)), pltpu.SemaphoreType.DMA((N_DEV - 1,)),
                        pltpu.SemaphoreType.DMA(())],
    )(g)


_HBM = pl.BlockSpec(memory_space=pltpu.HBM)
_SEM = pl.BlockSpec(memory_space=pltpu.SEMAPHORE)
_ANY = pl.BlockSpec(memory_space=pl.ANY)
_EFFECT = pltpu.SideEffectType.DATAFLOW_SIDE_EFFECTING


def _exchange_start(src, axis, gather, name, after=()):
    if gather:
        r, c = src.shape
        land_shape = (N_DEV * r, c) if axis == 0 else (r, N_DEV * c)
    else:
        r, c = (src.shape[0] // N_DEV, src.shape[1]) if axis == 0 else (src.shape[0], src.shape[1] // N_DEV)
        land_shape = (N_DEV, r, c)
    n_after = len(after)

    def body(*refs):
        src_ref, land_ref = refs[0], refs[1]
        send_sems, recv_sems, _, _, token = refs[2 + n_after:]
        me = _coords()
        dst = _window(land_ref, _index(me), r, c, axis) if gather else land_ref.at[_index(me)]
        for m in range(1, N_DEV):
            peer = _flip(me, m)
            pltpu.make_async_remote_copy(
                src_ref=src_ref if gather else _window(src_ref, _index(peer), r, c, axis), dst_ref=dst,
                send_sem=send_sems.at[m - 1], recv_sem=recv_sems.at[m - 1], device_id=peer,
                device_id_type=MESH).start()
        token[...] = jnp.zeros_like(token)

    land = pltpu.with_memory_space_constraint(lax.empty(land_shape, src.dtype), pltpu.HBM)
    outs = pl.pallas_call(
        body, name=name,
        out_shape=(pltpu.SemaphoreType.DMA((N_DEV - 1,)), pltpu.SemaphoreType.DMA((N_DEV - 1,)),
                   pltpu.HBM(src.shape, src.dtype), pltpu.HBM(land_shape, src.dtype),
                   jax.ShapeDtypeStruct((8, 128), F32)),
        in_specs=[_HBM, _HBM] + [_ANY] * n_after,
        out_specs=(_SEM, _SEM, _HBM, _HBM, pl.BlockSpec(memory_space=pltpu.VMEM)),
        input_output_aliases={0: 2, 1: 3},
        compiler_params=pltpu.CompilerParams(has_side_effects=_EFFECT),
    )(pltpu.with_memory_space_constraint(src, pltpu.HBM), land, *after)
    return (outs[:4], (axis, gather, r, c)), outs[4]


def _exchange_wait(handle, after, name):
    (send_sems, recv_sems, src_thru, land_thru), (axis, gather, r, c) = handle

    def body(src_ref, land_ref, send_sems, recv_sems, after_ref, src_dead, got_ref):
        me = _coords()
        for m in range(1, N_DEV):
            peer = _flip(me, m)
            if gather:
                mine, theirs = src_ref, _window(land_ref, _index(peer), r, c, axis)
            else:
                mine, theirs = _window(src_ref, _index(peer), r, c, axis), land_ref.at[_index(peer)]
            copy = pltpu.make_async_remote_copy(
                src_ref=mine, dst_ref=theirs, send_sem=send_sems.at[m - 1], recv_sem=recv_sems.at[m - 1],
                device_id=peer, device_id_type=MESH)
            copy.wait_send()
            copy.wait_recv()

    src_done, land = pl.pallas_call(
        body, name=name, out_shape=(pltpu.HBM(src_thru.shape, src_thru.dtype), pltpu.HBM(land_thru.shape, land_thru.dtype)),
        in_specs=[_HBM, _HBM, _SEM, _SEM, _ANY], out_specs=(_HBM, _HBM), input_output_aliases={0: 0, 1: 1},
        compiler_params=pltpu.CompilerParams(has_side_effects=_EFFECT),
    )(src_thru, land_thru, send_sems, recv_sems, after)

    def place_own(src_ref, land_ref, out_ref, sem):
        me = _coords()
        if gather:
            own = pltpu.make_async_copy(src_ref, _window(out_ref, _index(me), r, c, axis), sem)
        else:
            own = pltpu.make_async_copy(_window(src_ref, _index(me), r, c, axis), out_ref.at[_index(me)], sem)
        own.start()
        own.wait()

    return pl.pallas_call(
        place_own, name=name + "_own", out_shape=jax.ShapeDtypeStruct(land.shape, land.dtype),
        in_specs=[_ANY, _ANY], out_specs=_ANY, input_output_aliases={1: 0},
        scratch_shapes=[pltpu.SemaphoreType.DMA(())],
    )(src_done, land)


def _mm(a, b, mode, out_dtypes, name, tm=1024, tn=1024, tk=2048, extras=(), epilogue=None, dep=None):
    if mode == "nn":
        (m, k), n = a.shape, b.shape[1]
    elif mode == "nt":
        (m, k), n = a.shape, b.shape[0]
    else:
        (k, m), n = a.shape, b.shape[1]
    tm, tn = _tile(m, tm, 8), _tile(n, tn, 128)
    tk = _tile(k, tk, 16 if mode == "tn" else 128)
    nk = k // tk
    swap = nk == 1 and (k * n + (n // tn) * m * k) < (m * k + (m // tm) * k * n)

    def ij(p, q):
        return (q, p) if swap else (p, q)

    def spec(shape, fn):
        return pl.BlockSpec(shape, lambda p, q, kk: fn(*ij(p, q), kk))

    a_spec = spec((tk, tm), lambda i, j, kk: (kk, i)) if mode == "tn" else spec((tm, tk), lambda i, j, kk: (i, kk))
    b_spec = spec((tn, tk), lambda i, j, kk: (j, kk)) if mode == "nt" else spec((tk, tn), lambda i, j, kk: (kk, j))
    dims = {"nn": (((1,), (0,)), ((), ())), "nt": (((1,), (1,)), ((), ())), "tn": (((0,), (0,)), ((), ()))}[mode]
    ex_specs = [spec((tm, tn), functools.partial(lambda i, j, kk, off: (i, j + off), off=off)) for _, off in extras]
    deps = [] if dep is None else [dep]
    dep_specs = [pl.BlockSpec(dep.shape, lambda p, q, kk: (0, 0))] if deps else []
    n_ex, n_out = len(extras), len(out_dtypes)
    n_in = 2 + n_ex + len(deps)

    def body(*refs):
        a_ref, b_ref = refs[0], refs[1]
        ex_refs = refs[2:2 + n_ex]
        out_refs = refs[n_in:n_in + n_out]

        def product():
            return lax.dot_general(a_ref[...], b_ref[...], dims, preferred_element_type=F32)

        def finish(res):
            res = epilogue(res, *[e[...] for e in ex_refs]) if epilogue is not None else (res,)
            for o_ref, val in zip(out_refs, res):
                o_ref[...] = val.astype(o_ref.dtype)

        if nk == 1:
            finish(product())
            return
        acc = refs[-1]
        kk = pl.program_id(2)

        @pl.when(kk == 0)
        def _():
            acc[...] = product()

        @pl.when((kk > 0) & (kk < nk - 1))
        def _():
            acc[...] += product()

        @pl.when(kk == nk - 1)
        def _():
            finish(acc[...] + product())

    outs = pl.pallas_call(
        body, name=name, grid=(*ij(m // tm, n // tn), nk),
        in_specs=[a_spec, b_spec] + ex_specs + dep_specs,
        out_specs=[spec((tm, tn), lambda i, j, kk: (i, j)) for _ in out_dtypes],
        out_shape=[jax.ShapeDtypeStruct((m, n), dt) for dt in out_dtypes],
        scratch_shapes=[pltpu.VMEM((tm, tn), F32)] if nk > 1 else [],
        compiler_params=_params(("parallel", "parallel", "arbitrary")),
    )(a, b, *[e for e, _ in extras], *deps)
    return outs if n_out > 1 else outs[0]


def _rspec(tr, w, cb=0, off=0):
    return pl.BlockSpec((tr, w), lambda i: (i + off, cb))


def _cspec(shape):
    return pl.BlockSpec(shape, lambda i: (0,) * len(shape))


def _rms(xf):
    rstd = lax.rsqrt(jnp.mean(xf * xf, axis=-1, keepdims=True) + EPS)
    return xf * rstd, rstd


def _rms_bwd(dn, n, rstd):
    return rstd * (dn - n * jnp.mean(dn * n, axis=-1, keepdims=True))


def _colsum(v):
    return jnp.sum(v, axis=0, keepdims=True)


def _total(v):
    return jnp.sum(jnp.sum(v, axis=1, keepdims=True), axis=0, keepdims=True)


def _modulate(x, vec, name):
    rows, d = x.shape
    tr = _tile(rows, 256, 8)

    def body(x_ref, v_ref, a_ref):
        n, _ = _rms(x_ref[...])
        a_ref[...] = ((n * v_ref[0:1, :]) * (1.0 + v_ref[2:3, :]) + v_ref[1:2, :]).astype(BF16)

    return pl.pallas_call(
        body, name=name, grid=(rows // tr,), in_specs=[_rspec(tr, d), _cspec(vec.shape)],
        out_specs=_rspec(tr, d), out_shape=jax.ShapeDtypeStruct((rows, d), BF16),
        compiler_params=_params(("parallel",)))(x, vec)


def _resid_modulate(x, y, vec, name):
    rows, d = x.shape
    tr = _tile(rows, 256, 8)

    def body(x_ref, y_ref, v_ref, x1_ref, a_ref):
        x1 = x_ref[...] + v_ref[3:4, :] * y_ref[...]
        x1_ref[...] = x1
        n, _ = _rms(x1)
        a_ref[...] = ((n * v_ref[0:1, :]) * (1.0 + v_ref[2:3, :]) + v_ref[1:2, :]).astype(BF16)

    return pl.pallas_call(
        body, name=name, grid=(rows // tr,), in_specs=[_rspec(tr, d), _rspec(tr, d), _cspec(vec.shape)],
        out_specs=[_rspec(tr, d), _rspec(tr, d)],
        out_shape=[jax.ShapeDtypeStruct((rows, d), F32), jax.ShapeDtypeStruct((rows, d), BF16)],
        compiler_params=_params(("parallel",)))(x, y, vec)


def _loss_head(x1, f, target, vec, name):
    rows, d = x1.shape
    tr = _tile(rows, 256, 8)

    def body(x1_ref, f_ref, t_ref, v_ref, dx2_ref, df_ref, acc_ref):
        @pl.when(pl.program_id(0) == 0)
        def _():
            acc_ref[...] = jnp.zeros_like(acc_ref)

        gate, gain = v_ref[0:1, :], v_ref[1:2, :]
        fv = f_ref[...]
        n, rstd = _rms(x1_ref[...] + gate * fv)
        err = n * gain - t_ref[...]
        dy = err / d
        dx2 = _rms_bwd(dy * gain, n, rstd)
        dx2_ref[...] = dx2
        df_ref[...] = (dx2 * gate).astype(BF16)
        acc_ref[0:1, :] += _colsum(dx2 * fv)
        acc_ref[1:2, :] += _colsum(dy * n)
        acc_ref[2:3, :] += _colsum(err * err)

    return pl.pallas_call(
        body, name=name, grid=(rows // tr,),
        in_specs=[_rspec(tr, d), _rspec(tr, d), _rspec(tr, d), _cspec(vec.shape)],
        out_specs=[_rspec(tr, d), _rspec(tr, d), _cspec((8, d))],
        out_shape=[jax.ShapeDtypeStruct((rows, d), F32), jax.ShapeDtypeStruct((rows, d), BF16),
                   jax.ShapeDtypeStruct((8, d), F32)],
        compiler_params=_params(("arbitrary",)))(x1, f, target, vec)


def _modulate_bwd(da, x, vec, name, da_off=0, dx_in=None, y=None):
    rows, d = x.shape
    tr = _tile(rows, 256, 8)
    off = da_off // tr
    has_dx, has_y = dx_in is not None, y is not None

    def body(*refs):
        da_ref, x_ref, v_ref = refs[0], refs[1], refs[2]
        pos = 3
        dxin_ref = refs[pos] if has_dx else None
        pos += has_dx
        y_ref = refs[pos] if has_y else None
        pos += has_y
        dx_ref = refs[pos]
        dy_ref = refs[pos + 1] if has_y else None
        acc_ref = refs[-1]

        @pl.when(pl.program_id(0) == 0)
        def _():
            acc_ref[...] = jnp.zeros_like(acc_ref)

        gain, scale1 = v_ref[0:1, :], 1.0 + v_ref[2:3, :]
        dav = da_ref[...]
        n, rstd = _rms(x_ref[...])
        dx = _rms_bwd(dav * gain * scale1, n, rstd)
        if has_dx:
            dx = dx + dxin_ref[...]
        dx_ref[...] = dx
        acc_ref[0:1, :] += _colsum(dav)
        acc_ref[1:2, :] += _colsum(dav * (n * gain))
        acc_ref[2:3, :] += _colsum(dav * n * scale1)
        if has_y:
            acc_ref[3:4, :] += _colsum(dx * y_ref[...])
            dy_ref[...] = (dx * v_ref[3:4, :]).astype(BF16)

    ins = [da, x, vec] + ([dx_in] if has_dx else []) + ([y] if has_y else [])
    in_specs = [_rspec(tr, d, off=off), _rspec(tr, d), _cspec(vec.shape)] + [_rspec(tr, d)] * (has_dx + has_y)
    out_specs = [_rspec(tr, d)] + ([_rspec(tr, d)] if has_y else []) + [_cspec((8, d))]
    out_shape = ([jax.ShapeDtypeStruct((rows, d), F32)] + ([jax.ShapeDtypeStruct((rows, d), BF16)] if has_y else [])
                 + [jax.ShapeDtypeStruct((8, d), F32)])
    return pl.pallas_call(
        body, name=name, grid=(rows // tr,), in_specs=in_specs, out_specs=out_specs, out_shape=out_shape,
        compiler_params=_params(("arbitrary",)))(*ins)


def _conv_terms(cc, cx, w_ref, tr):
    t = lax.broadcasted_iota(jnp.int32, (tr, 1), 0) % GRID_W
    first, last = t == 0, t == GRID_W - 1
    u = cc * cx
    prev = jnp.where(first, 0.0, pltpu.roll(u, 1, 0))
    nxt = jnp.where(last, 0.0, pltpu.roll(u, tr - 1, 0))
    y = w_ref[0:1, :] * prev + w_ref[1:2, :] * u + w_ref[2:3, :] * nxt
    return u, prev, nxt, y, first, last


def _conv_fwd(z, conv_w, rows, name):
    d = conv_w.shape[1]
    tr = _tile(rows, 256, GRID_W)

    def body(cb_ref, cc_ref, cx_ref, w_ref, out_ref):
        y = _conv_terms(cc_ref[...], cx_ref[...], w_ref, tr)[3]
        out_ref[...] = (cb_ref[...] * y).astype(BF16)

    return pl.pallas_call(
        body, name=name, grid=(rows // tr,),
        in_specs=[_rspec(tr, d, CB), _rspec(tr, d, CC), _rspec(tr, d, CX), _cspec(conv_w.shape)],
        out_specs=_rspec(tr, d), out_shape=jax.ShapeDtypeStruct((rows, d), BF16),
        compiler_params=_params(("parallel",)))(z, z, z, conv_w)


def _conv_bwd(dpc, z, conv_w, name):
    rows, d = dpc.shape
    tr = _tile(rows, 256, GRID_W)

    def body(dpc_ref, cb_ref, cc_ref, cx_ref, w_ref, dz_ref, acc_ref):
        @pl.when(pl.program_id(0) == 0)
        def _():
            acc_ref[...] = jnp.zeros_like(acc_ref)

        cc, cx, dpcv = cc_ref[...], cx_ref[...], dpc_ref[...]
        u, prev, nxt, y, first, last = _conv_terms(cc, cx, w_ref, tr)
        dy = dpcv * cb_ref[...]
        dy_next = jnp.where(last, 0.0, pltpu.roll(dy, tr - 1, 0))
        dy_prev = jnp.where(first, 0.0, pltpu.roll(dy, 1, 0))
        du = w_ref[0:1, :] * dy_next + w_ref[1:2, :] * dy + w_ref[2:3, :] * dy_prev
        dz_ref[:, 0:d] = (dpcv * y).astype(BF16)
        dz_ref[:, d:2 * d] = (du * cx).astype(BF16)
        dz_ref[:, 2 * d:3 * d] = (du * cc).astype(BF16)
        acc_ref[0:1, :] += _colsum(dy * prev)
        acc_ref[1:2, :] += _colsum(dy * u)
        acc_ref[2:3, :] += _colsum(dy * nxt)

    return pl.pallas_call(
        body, name=name, grid=(rows // tr,),
        in_specs=[_rspec(tr, d), _rspec(tr, d, CB), _rspec(tr, d, CC), _rspec(tr, d, CX), _cspec(conv_w.shape)],
        out_specs=[_rspec(tr, 3 * d), _cspec((8, d))],
        out_shape=[jax.ShapeDtypeStruct((rows, 3 * d), BF16), jax.ShapeDtypeStruct((8, d), F32)],
        compiler_params=_params(("arbitrary",)))(dpc, z, z, z, conv_w)


def _rotary_fwd(z, cos, sin, n_lat, name):
    rows = z.shape[0]
    d = z.shape[1] // N_IN
    dk = d // HEADS
    half = dk // 2
    tr = _tile(n_lat, 256, 8)
    tr = _tile(rows - n_lat, tr, 8)
    lat_blocks = n_lat // tr

    def body(q_ref, k_ref, v0_ref, v1_ref, cos_ref, sin_ref, qo_ref, ko_ref, vo_ref):
        cs, sn = cos_ref[...], sin_ref[...]
        keep = jnp.where(pl.program_id(0) < lat_blocks, 1.0, 0.0)
        for src, dst, scale in ((q_ref, qo_ref, keep), (k_ref, ko_ref, dk ** -0.5)):
            for h in range(HEADS):
                lo, mid, hi = h * dk, h * dk + half, (h + 1) * dk
                t1, t2 = src[:, lo:mid], src[:, mid:hi]
                dst[:, lo:mid] = ((t1 * cs - t2 * sn) * scale).astype(BF16)
                dst[:, mid:hi] = ((t1 * sn + t2 * cs) * scale).astype(BF16)
        vo_ref[:, 0:d] = v0_ref[...].astype(BF16)
        vo_ref[:, d:2 * d] = v1_ref[...].astype(BF16)

    return pl.pallas_call(
        body, name=name, grid=(rows // tr,),
        in_specs=[_rspec(tr, d, CQ), _rspec(tr, d, CK), _rspec(tr, d, CV), _rspec(tr, d, CV + 1),
                  _rspec(tr, half), _rspec(tr, half)],
        out_specs=[_rspec(tr, d), _rspec(tr, d), _rspec(tr, 2 * d)],
        out_shape=[jax.ShapeDtypeStruct((rows, d), BF16), jax.ShapeDtypeStruct((rows, d), BF16),
                   jax.ShapeDtypeStruct((rows, 2 * d), BF16)],
        compiler_params=_params(("parallel",)))(z, z, z, z, cos, sin)


def _qkv_bwd(dq_f, dq_b, dk_f, dk_b, dv_f, dv_b, cos, sin, n_lat, n_ctx, name):
    rows = n_lat + n_ctx
    d = dq_f.shape[1]
    dk = d // HEADS
    half = dk // 2
    tr = _tile(n_ctx, 128, 8)
    lat_blocks, ctx_blocks = n_lat // tr, n_ctx // tr

    def fmap(i):
        return (jnp.where(i < lat_blocks, i + ctx_blocks, i - lat_blocks), 0)

    def bmap(i):
        return (i + ctx_blocks, 0)

    def body(qf_ref, qb_ref, kf_ref, kb_ref, vf_ref, vb_ref, cos_ref, sin_ref, qo_ref, ko_ref, vo_ref):
        cs, sn = cos_ref[...], sin_ref[...]
        keep = jnp.where(pl.program_id(0) < lat_blocks, 1.0, 0.0)
        for fa, fb, dst, scale in ((qf_ref, qb_ref, qo_ref, keep), (kf_ref, kb_ref, ko_ref, dk ** -0.5)):
            for h in range(HEADS):
                lo, mid, hi = h * dk, h * dk + half, (h + 1) * dk
                d1 = fa[:, lo:mid] + fb[:, lo:mid]
                d2 = fa[:, mid:hi] + fb[:, mid:hi]
                dst[:, lo:mid] = ((d1 * cs + d2 * sn) * scale).astype(BF16)
                dst[:, mid:hi] = ((d2 * cs - d1 * sn) * scale).astype(BF16)
        vo_ref[...] = (vf_ref[...] + vb_ref[...]).astype(BF16)

    return pl.pallas_call(
        body, name=name, grid=(rows // tr,),
        in_specs=[pl.BlockSpec((tr, d), fmap), pl.BlockSpec((tr, d), bmap), pl.BlockSpec((tr, d), fmap),
                  pl.BlockSpec((tr, d), bmap), pl.BlockSpec((tr, 2 * d), fmap), pl.BlockSpec((tr, 2 * d), bmap),
                  _rspec(tr, half), _rspec(tr, half)],
        out_specs=[_rspec(tr, d), _rspec(tr, d), _rspec(tr, 2 * d)],
        out_shape=[jax.ShapeDtypeStruct((rows, d), BF16), jax.ShapeDtypeStruct((rows, d), BF16),
                   jax.ShapeDtypeStruct((rows, 2 * d), BF16)],
        compiler_params=_params(("parallel",)))(dq_f, dq_b, dk_f, dk_b, dv_f, dv_b, cos, sin)


def _gn_fwd(o_f, o_b, z, rows, row_off, name):
    d = z.shape[1] // N_IN
    dv = 2 * d // HEADS
    tr = _tile(row_off, 128, 8)
    off = row_off // tr

    def body(of_ref, ob_ref, g0_ref, g1_ref, r_ref):
        for h in range(HEADS):
            lo, hi = h * dv, (h + 1) * dv
            g_ref, glo = (g0_ref, lo) if hi <= d else (g1_ref, lo - d)
            o = of_ref[:, lo:hi] + ob_ref[:, lo:hi]
            cen = o - jnp.mean(o, axis=-1, keepdims=True)
            on = cen * lax.rsqrt(jnp.mean(cen * cen, axis=-1, keepdims=True) + EPS)
            g = g_ref[:, glo:glo + dv]
            r_ref[:, lo:hi] = (g * _sig(g) * on).astype(BF16)

    return pl.pallas_call(
        body, name=name, grid=(rows // tr,),
        in_specs=[_rspec(tr, 2 * d, off=off), _rspec(tr, 2 * d, off=off), _rspec(tr, d, CG), _rspec(tr, d, CG + 1)],
        out_specs=_rspec(tr, 2 * d), out_shape=jax.ShapeDtypeStruct((rows, 2 * d), BF16),
        compiler_params=_params(("parallel",)))(o_f, o_b, z, z)


def _gn_bwd(dr, o_f, o_b, z, row_off, name):
    rows = dr.shape[0]
    d = z.shape[1] // N_IN
    dv = 2 * d // HEADS
    tr = _tile(row_off, 128, 8)
    off = row_off // tr

    def body(dr_ref, of_ref, ob_ref, g0_ref, g1_ref, do_ref, dg_ref):
        for h in range(HEADS):
            lo, hi = h * dv, (h + 1) * dv
            g_ref, glo = (g0_ref, lo) if hi <= d else (g1_ref, lo - d)
            o = of_ref[:, lo:hi] + ob_ref[:, lo:hi]
            cen = o - jnp.mean(o, axis=-1, keepdims=True)
            rstd = lax.rsqrt(jnp.mean(cen * cen, axis=-1, keepdims=True) + EPS)
            on = cen * rstd
            g = g_ref[:, glo:glo + dv]
            sg = _sig(g)
            drv = dr_ref[:, lo:hi]
            dg_ref[:, lo:hi] = (drv * on * (sg * (1.0 + g * (1.0 - sg)))).astype(BF16)
            don = drv * (g * sg)
            do = rstd * (don - jnp.mean(don, axis=-1, keepdims=True)
                         - on * jnp.mean(don * on, axis=-1, keepdims=True))
            do_ref[:, lo:hi] = do.astype(BF16)

    return pl.pallas_call(
        body, name=name, grid=(rows // tr,),
        in_specs=[_rspec(tr, 2 * d), _rspec(tr, 2 * d, off=off), _rspec(tr, 2 * d, off=off),
                  _rspec(tr, d, CG), _rspec(tr, d, CG + 1)],
        out_specs=[_rspec(tr, 2 * d), _rspec(tr, 2 * d)],
        out_shape=[jax.ShapeDtypeStruct((rows, 2 * d), BF16), jax.ShapeDtypeStruct((rows, 2 * d), BF16)],
        compiler_params=_params(("parallel",)))(dr, o_f, o_b, z, z)


def _decays(lg, rev):
    row = lax.broadcasted_iota(jnp.int32, (CHUNK, CHUNK), 0)
    col = lax.broadcasted_iota(jnp.int32, (CHUNK, CHUNK), 1)
    rel = ((col - row) if rev else (row - col)).astype(F32)
    mask = jnp.where(rel >= 0, jnp.exp(lg * jnp.maximum(rel, 0.0)), 0.0)
    r = lax.broadcasted_iota(jnp.int32, (CHUNK, 1), 0)
    rr = ((CHUNK - 1 - r) if rev else r).astype(F32)
    chunk_decay = jnp.exp(lg * jnp.full((1, 1), float(CHUNK), F32))
    return rr, mask, jnp.exp(lg * (rr + 1.0)), jnp.exp(lg * (CHUNK - 1.0 - rr)), chunk_decay


_NT = (((1,), (1,)), ((), ()))
_TN = (((0,), (0,)), ((), ()))


def _dot(a, b, dims=None):
    if dims is None:
        return jnp.dot(a, b, preferred_element_type=F32)
    return lax.dot_general(a, b, dims, preferred_element_type=F32)


def _ret_fwd(q, k, v, lg, rev, name):
    tt, d = q.shape
    dk, dv = d // HEADS, 2 * d // HEADS
    nc = tt // CHUNK

    def cmap(h, i):
        return ((nc - 1 - i) if rev else i, h)

    def body(lg_ref, q_ref, k_ref, v_ref, o_ref, s_ref, state):
        h = pl.program_id(0)

        @pl.when(pl.program_id(1) == 0)
        def _():
            state[...] = jnp.zeros_like(state)

        lgh = lg_ref[h]
        _, mask, qd, kd, cd = _decays(lgh, rev)
        qv, kv, vv = q_ref[...], k_ref[...], v_ref[...]
        st = state[...]
        p = _dot(qv, kv, _NT) * mask
        o_ref[...] = _dot(p.astype(BF16), vv) + _dot((qv * qd).astype(BF16), st.astype(BF16))
        s_ref[...] = st
        state[...] = cd * st + _dot((kv * kd).astype(BF16), vv, _TN)

    return pl.pallas_call(
        body, name=name, grid=(HEADS, nc),
        in_specs=[pl.BlockSpec(memory_space=pltpu.SMEM), pl.BlockSpec((CHUNK, dk), cmap),
                  pl.BlockSpec((CHUNK, dk), cmap), pl.BlockSpec((CHUNK, dv), cmap)],
        out_specs=[pl.BlockSpec((CHUNK, dv), cmap),
                   pl.BlockSpec((None, None, dk, dv), lambda h, i: (cmap(h, i)[0], h, 0, 0))],
        out_shape=[jax.ShapeDtypeStruct((tt, 2 * d), F32), jax.ShapeDtypeStruct((nc, HEADS, dk, dv), F32)],
        scratch_shapes=[pltpu.VMEM((dk, dv), F32)],
        compiler_params=_params(("parallel", "arbitrary")))(lg, q, k, v)


def _ret_bwd(q, k, v, do, states, lg, rev, name):
    tt, d = q.shape
    dk, dv = d // HEADS, 2 * d // HEADS
    nc = tt // CHUNK

    def cmap(h, i):
        return (i if rev else (nc - 1 - i), h)

    def body(lg_ref, q_ref, k_ref, v_ref, do_ref, s_ref, dq_ref, dk_ref, dv_ref, dlg_ref, dstate):
        h = pl.program_id(0)

        @pl.when(pl.program_id(1) == 0)
        def _():
            dstate[...] = jnp.zeros_like(dstate)
            dlg_ref[...] = jnp.zeros_like(dlg_ref)

        lgh = lg_ref[h]
        rr, mask, qd, kd, cd = _decays(lgh, rev)
        qv, kv, vv, dov = q_ref[...], k_ref[...], v_ref[...], do_ref[...]
        st, dst = s_ref[...], dstate[...]
        st_b, dst_b = st.astype(BF16), dst.astype(BF16)
        p = _dot(qv, kv, _NT) * mask
        da = (_dot(dov, vv, _NT) * mask).astype(BF16)
        dq = _dot(da, kv) + _dot(dov, st_b, _NT) * qd
        dkk = _dot(da, qv, _TN) + _dot(vv, dst_b, _NT) * kd
        dq_ref[...] = dq
        dk_ref[...] = dkk
        dv_ref[...] = _dot(p.astype(BF16), dov, _TN) + _dot((kv * kd).astype(BF16), dst_b)
        dnew = cd * dst + _dot((qv * qd).astype(BF16), dov, _TN)
        dstate[...] = dnew
        qf, kf = qv.astype(F32), kv.astype(F32)
        rows = rr * (jnp.sum(qf * dq, axis=1, keepdims=True) - jnp.sum(kf * dkk, axis=1, keepdims=True))
        dlg_ref[...] += _total(rows) + CHUNK * _total(st * dnew)

    return pl.pallas_call(
        body, name=name, grid=(HEADS, nc),
        in_specs=[pl.BlockSpec(memory_space=pltpu.SMEM), pl.BlockSpec((CHUNK, dk), cmap),
                  pl.BlockSpec((CHUNK, dk), cmap), pl.BlockSpec((CHUNK, dv), cmap), pl.BlockSpec((CHUNK, dv), cmap),
                  pl.BlockSpec((None, None, dk, dv), lambda h, i: (cmap(h, i)[0], h, 0, 0))],
        out_specs=[pl.BlockSpec((CHUNK, dk), cmap), pl.BlockSpec((CHUNK, dk), cmap), pl.BlockSpec((CHUNK, dv), cmap),
                   pl.BlockSpec((None, 8, 128), lambda h, i: (h, 0, 0))],
        out_shape=[jax.ShapeDtypeStruct((tt, d), F32), jax.ShapeDtypeStruct((tt, d), F32),
                   jax.ShapeDtypeStruct((tt, 2 * d), F32), jax.ShapeDtypeStruct((HEADS, 8, 128), F32)],
        scratch_shapes=[pltpu.VMEM((dk, dv), F32)],
        compiler_params=_params(("parallel", "arbitrary")))(lg, q, k, v, do, states)


def _silu(v):
    return v * _sig(v)


def _mod_proj(cs, w_mod, b_loc, dec, name):
    nb = w_mod.shape[1]

    def body(cs_ref, w_ref, b_ref, dec_ref, out_ref, lg_ref):
        out_ref[...] = jnp.dot(_silu(cs_ref[...]), w_ref[...], preferred_element_type=F32, precision=HIGHEST) + b_ref[...]
        a = dec_ref[...]
        lg_ref[...] = jnp.minimum(a, 0.0) - jnp.log1p(jnp.exp(-jnp.abs(a)))

    return pl.pallas_call(
        body, name=name,
        out_shape=[jax.ShapeDtypeStruct((16, nb), F32), jax.ShapeDtypeStruct(dec.shape, F32)],
        compiler_params=_params())(cs, w_mod, b_loc, dec)


def _mod_grad(cs, dm, w_mod, name):
    d, nb = w_mod.shape

    def body(cs_ref, dm_ref, w_ref, gw_ref, part_ref):
        dmv = dm_ref[...]
        gw_ref[...] = lax.dot_general(_silu(cs_ref[...]), dmv, _TN, preferred_element_type=F32, precision=HIGHEST)
        part_ref[...] = lax.dot_general(dmv, w_ref[...], _NT, preferred_element_type=F32, precision=HIGHEST)

    return pl.pallas_call(
        body, name=name,
        out_shape=[jax.ShapeDtypeStruct((d, nb), F32), jax.ShapeDtypeStruct((16, d), F32)],
        compiler_params=_params())(cs, dm, w_mod)


def _reduce_small(gathered, dec, n_feat, name):
    _, rows, d = gathered.shape

    def body(g_ref, dec_ref, red_ref, misc_ref):
        total = g_ref[0]
        for i in range(1, N_DEV):
            total = total + g_ref[i]
        red_ref[...] = total
        misc_ref[...] = jnp.zeros_like(misc_ref)
        misc_ref[0:1, :] = jnp.zeros((1, 128), F32) + (0.5 / n_feat) * _total(total[16:17, :])
        misc_ref[1:3, :] = total[14:16, 0:128] * _sig(-dec_ref[0:2, :])

    return pl.pallas_call(
        body, name=name,
        out_shape=[jax.ShapeDtypeStruct((rows, d), F32), jax.ShapeDtypeStruct((8, 128), F32)],
        compiler_params=_params())(gathered, dec)


def _c_ctx_grad(parts, c_ctx, name):
    d = c_ctx.shape[1]

    def body(p_ref, c_ref, out_ref):
        total = p_ref[0]
        for i in range(1, N_DEV):
            total = total + p_ref[i]
        cv = c_ref[...]
        sg = _sig(cv)
        out_ref[...] = total[8:9, :] * (sg * (1.0 + cv * (1.0 - sg)))

    return pl.pallas_call(body, name=name, out_shape=jax.ShapeDtypeStruct((1, d), F32),
                          compiler_params=_params())(parts, c_ctx)


def _adamw(parts, w, m, v, name):
    n_parts, rows, cols = parts.shape
    row_bytes = cols * (parts.dtype.itemsize * n_parts + 7 * 4)
    tr = _tile(rows, max(16, (4 * 1024 * 1024) // row_bytes), 16 if rows % 16 == 0 else 8)

    def body(p_ref, w_ref, m_ref, v_ref, g_ref, d_ref, mo_ref, vo_ref):
        g = p_ref[0].astype(F32)
        for i in range(1, n_parts):
            g = g + p_ref[i].astype(F32)
        m2 = ADAM_B1 * m_ref[...] + (1.0 - ADAM_B1) * g
        v2 = ADAM_B2 * v_ref[...] + (1.0 - ADAM_B2) * jnp.square(g)
        m_hat = m2 / (1.0 - ADAM_B1 ** ADAM_STEP)
        v_hat = v2 / (1.0 - ADAM_B2 ** ADAM_STEP)
        g_ref[...] = g
        d_ref[...] = -ADAM_LR * (m_hat / (jnp.sqrt(v_hat) + ADAM_EPS) + ADAM_WD * w_ref[...])
        mo_ref[...] = m2
        vo_ref[...] = v2

    spec = _rspec(tr, cols)
    return pl.pallas_call(
        body, name=name, grid=(rows // tr,),
        in_specs=[pl.BlockSpec((n_parts, tr, cols), lambda i: (0, i, 0)), spec, spec, spec],
        out_specs=[spec] * 4, out_shape=[jax.ShapeDtypeStruct((rows, cols), F32)] * 4,
        compiler_params=_params(("parallel",)))(parts, w, m, v)


def _rope_tables(pos, dk):
    half = dk // 2
    inv_freq = 1.0 / (ROPE_BASE ** jnp.linspace(0.0, 1.0, half, dtype=F32))
    ang = pos[:, None] * inv_freq[None, :]
    return jnp.cos(ang), jnp.sin(ang)


def _pad_lanes(v, width):
    return jnp.pad(v, ((0, 0), (0, width - v.shape[1])))


def kernel(x, c, ctx, c_ctx, w_mod, b_mod, norm1_g, w_in, conv_w, w_conv_out, ret_decay_fwd, ret_decay_bwd, w_ret_out, w_o, norm2_g, w_ff1, w_ff2, final_g, loss_target, m_c_ctx, m_w_mod, m_b_mod, m_norm1_g, m_w_in, m_conv_w, m_w_conv_out, m_ret_decay_fwd, m_ret_decay_bwd, m_w_ret_out, m_w_o, m_norm2_g, m_w_ff1, m_w_ff2, m_final_g, v_c_ctx, v_w_mod, v_b_mod, v_norm1_g, v_w_in, v_conv_w, v_w_conv_out, v_ret_decay_fwd, v_ret_decay_bwd, v_w_ret_out, v_w_o, v_norm2_g, v_w_ff1, v_w_ff2, v_final_g):
    n_lat, d = x.shape[1], x.shape[2]
    n_ctx = ctx.shape[1]
    dk = d // HEADS
    nb = w_mod.shape[2]
    me = 4 * lax.axis_index("x") + 2 * lax.axis_index("y") + lax.axis_index("c")
    xl, ctxl, target = x[0], ctx[0], loss_target[0]

    w_in_f = _allgather(w_in[0].astype(BF16), 1, "ag_w_in")
    conv_w_f = _small_allgather(conv_w[0], "ag_conv_w").transpose(1, 0, 2).reshape(3, d)

    c_all = _small_allgather(c, "ag_c").reshape(N_DEV, d)
    cs = jnp.concatenate([c_all, c_ctx[None], jnp.zeros((7, d), F32)], axis=0)
    dec = jnp.pad(jnp.concatenate([ret_decay_fwd, ret_decay_bwd], axis=0), ((0, 6), (0, 128 - HEADS)))
    b_loc = lax.dynamic_slice(b_mod, (0, me * nb), (1, nb))
    modp, lgs = _mod_proj(cs, w_mod[0], b_loc, dec, "mod_proj")
    modp_all = _small_allgather(modp, "ag_mod")
    mod_l = lax.dynamic_index_in_dim(modp_all, me, axis=1, keepdims=False).reshape(N_MOD, d)
    mod_c = modp_all[:, 8, :].reshape(N_MOD, d)
    lg_f, lg_b = lgs[0, :HEADS], lgs[1, :HEADS]
    zero_row = jnp.zeros((1, d), F32)
    vec1_l = jnp.concatenate([norm1_g, mod_l[0:1], mod_l[1:2], zero_row], axis=0)
    vec1_c = jnp.concatenate([norm1_g, mod_c[0:1], mod_c[1:2], zero_row], axis=0)
    vec2 = jnp.concatenate([norm2_g, mod_l[3:4], mod_l[4:5], mod_l[2:3]], axis=0)
    vec3 = jnp.concatenate([mod_l[5:6], final_g[None]], axis=0)

    later, after = {}, (w_in_f, conv_w_f, modp_all)
    for wname, w, axis in (("w_conv_out", w_conv_out, 0), ("w_ret_out", w_ret_out, 0), ("w_o", w_o, 0),
                           ("w_ff1", w_ff1, 1), ("w_ff2", w_ff2, 0)):
        later[wname], token = _exchange_start(w[0].astype(BF16), axis, True, "ag_" + wname + "_start", after=after)
        after = (token,)

    a_all = jnp.concatenate([_modulate(xl, vec1_l, "modulate1"), _modulate(ctxl, vec1_c, "modulate1_ctx")], axis=0)
    z = _mm(a_all, w_in_f, "nn", [F32], "proj_in", tn=2048, dep=token)
    pc = _conv_fwd(z, conv_w_f, n_lat, "conv_fwd")
    pos = jnp.concatenate([n_ctx + jnp.arange(n_lat, dtype=F32), jnp.arange(n_ctx, dtype=F32)])
    cos, sin = _rope_tables(pos, dk)
    q_m, k_m, v_m = _rotary_fwd(z, cos, sin, n_lat, "rotary_fwd")
    q_s, k_s, v_s = (jnp.concatenate([t[n_lat:], t], axis=0) for t in (q_m, k_m, v_m))
    o_f, st_f = _ret_fwd(q_s, k_s, v_s, lg_f, False, "ret_fwd_f")
    o_b, st_b = _ret_fwd(q_s, k_s, v_s, lg_b, True, "ret_fwd_b")
    r = _gn_fwd(o_f, o_b, z, n_lat, n_ctx, "gn_fwd")
    w_conv_out_f = _exchange_wait(later["w_conv_out"], r, "ag_w_conv_out_wait")
    w_ret_out_f = _exchange_wait(later["w_ret_out"], r, "ag_w_ret_out_wait")
    w_o_f = _exchange_wait(later["w_o"], r, "ag_w_o_wait")
    y_conv = _mm(pc, w_conv_out_f, "nn", [F32], "proj_conv_out", tn=2048)
    tn_d = _tile(d, 1024, 128)
    gate_offs = (CGC * d // tn_d, CGR * d // tn_d)

    def merge(acc, yc, gc, gr):
        return acc, _sig(gc) * yc + _sig(gr) * acc

    y_ret, mg = _mm(r, w_ret_out_f, "nn", [F32, BF16], "proj_ret_out", tm=512, tn=tn_d,
                    extras=[(y_conv, 0), (z, gate_offs[0]), (z, gate_offs[1])], epilogue=merge)
    y_l = _mm(mg, w_o_f, "nn", [F32], "proj_o", tn=2048)
    x1, a2 = _resid_modulate(xl, y_l, vec2, "resid_modulate2")

    def sqrelu(acc):
        return acc, jnp.square(jnp.maximum(acc, 0.0))

    w_ff1_f = _exchange_wait(later["w_ff1"], a2, "ag_w_ff1_wait")
    hff, s = _mm(a2, w_ff1_f, "nn", [F32, BF16], "ff1", epilogue=sqrelu)
    w_ff2_f = _exchange_wait(later["w_ff2"], s, "ag_w_ff2_wait")
    f = _mm(s, w_ff2_f, "nn", [F32], "ff2")
    dx2, df, acc3 = _loss_head(x1, f, target, vec3, "loss_head")

    def d_sqrelu(acc, h):
        return (acc * (2.0 * jnp.maximum(h, 0.0)),)

    dh = _mm(df, w_ff2_f, "nt", [BF16], "ff2_dx", extras=[(hff, 0)], epilogue=d_sqrelu)
    sent = {}
    sent["w_ff2"], token = _exchange_start(_mm(s, df, "tn", [BF16], "ff2_dw"), 0, False, "rs_w_ff2_start")
    da2 = _mm(dh, w_ff1_f, "nt", [F32], "ff1_dx", dep=token)
    sent["w_ff1"], token = _exchange_start(_mm(a2, dh, "tn", [BF16], "ff1_dw"), 1, False, "rs_w_ff1_start")
    dx1, dyl, acc2 = _modulate_bwd(da2, x1, vec2, "modulate2_bwd", dx_in=dx2, y=y_l)

    def d_merge(acc, yc, yr, gc, gr):
        sc, sr = _sig(gc), _sig(gr)
        return acc * sc, acc * sr, acc * yc * (sc * (1.0 - sc)), acc * yr * (sr * (1.0 - sr))

    dyc, dyr, dgc, dgr = _mm(dyl, w_o_f, "nt", [BF16] * 4, "proj_o_dx", tm=512, tn=tn_d,
                             extras=[(y_conv, 0), (y_ret, 0), (z, gate_offs[0]), (z, gate_offs[1])], epilogue=d_merge,
                             dep=token)
    sent["w_o"], token = _exchange_start(_mm(mg, dyl, "tn", [BF16], "proj_o_dw"), 0, False, "rs_w_o_start")
    dpc = _mm(dyc, w_conv_out_f, "nt", [F32], "proj_conv_out_dx", tn=2048, dep=token)
    sent["w_conv_out"], token = _exchange_start(_mm(pc, dyc, "tn", [BF16], "proj_conv_out_dw"), 0, False,
                                                "rs_w_conv_out_start")
    dz_conv, acc_conv = _conv_bwd(dpc, z, conv_w_f, "conv_bwd")
    dr = _mm(dyr, w_ret_out_f, "nt", [F32], "proj_ret_out_dx", dep=token)
    sent["w_ret_out"], token = _exchange_start(_mm(r, dyr, "tn", [BF16], "proj_ret_out_dw"), 0, False,
                                               "rs_w_ret_out_start")
    do, dz_g = _gn_bwd(dr, o_f, o_b, z, n_ctx, "gn_bwd")
    do_s = jnp.pad(do, ((n_ctx, n_ctx), (0, 0)))
    dq_f, dk_f, dv_f, dlg_f = _ret_bwd(q_s, k_s, v_s, do_s, st_f, lg_f, False, "ret_bwd_f")
    dq_b, dk_b, dv_b, dlg_b = _ret_bwd(q_s, k_s, v_s, do_s, st_b, lg_b, True, "ret_bwd_b")
    dz_q, dz_k, dz_v = _qkv_bwd(dq_f, dq_b, dk_f, dk_b, dv_f, dv_b, cos, sin, n_lat, n_ctx, "qkv_bwd")

    def with_ctx_rows(t):
        return jnp.pad(t, ((0, n_ctx), (0, 0)))

    dz = jnp.concatenate([with_ctx_rows(dz_conv), dz_q, dz_k, dz_v, with_ctx_rows(dz_g), with_ctx_rows(dgc),
                          with_ctx_rows(dgr)], axis=1)
    g_in = _mm(a_all, dz, "tn", [BF16], "proj_in_dw", tn=2048, tk=(n_lat + n_ctx) // 4, dep=token)
    sent["w_in"], token = _exchange_start(g_in, 1, False, "rs_w_in_start")
    da_all = _mm(dz, w_in_f, "nt", [F32], "proj_in_dx", tn=2048, dep=token)
    grad_x, acc1 = _modulate_bwd(da_all, xl, vec1_l, "modulate1_bwd", dx_in=dx1)
    _, acc1c = _modulate_bwd(da_all, ctxl, vec1_c, "modulate1_ctx_bwd", da_off=n_lat)

    lane_pad = functools.partial(_pad_lanes, width=d)
    packet = jnp.concatenate([
        acc1[2:3] + acc1c[2:3], acc2[2:3], acc3[1:2],
        acc1[0:1], acc1[1:2], acc2[3:4], acc2[0:1], acc2[1:2], acc3[0:1],
        acc1c[0:1], acc1c[1:2],
        acc_conv[0:3],
        lane_pad(dlg_f[:, 0, 0][None]), lane_pad(dlg_b[:, 0, 0][None]),
        acc3[2:3],
        jnp.zeros((7, d), F32)], axis=0)
    gathered = _small_allgather(packet, "ag_small")
    red, misc = _reduce_small(gathered, dec, d, "reduce_small")
    dmod_ctx = jnp.concatenate([red[9], red[10], jnp.zeros((4 * d,), F32)])
    dmod_all = jnp.concatenate([gathered[:, 3:9, :].reshape(N_DEV, N_MOD * d), dmod_ctx[None]], axis=0)
    dm = jnp.pad(lax.dynamic_slice(dmod_all, (0, me * nb), (N_DEV + 1, nb)), ((0, 7), (0, 0)))
    g_mod, c_part = _mod_grad(cs, dm, w_mod[0], "mod_grad")
    g_c_ctx = _c_ctx_grad(_small_allgather(c_part, "ag_c_ctx"), c_ctx[None], "c_ctx_grad")
    g_b_mod = red[3:9].reshape(1, N_MOD * d) + dmod_ctx[None]
    g_conv_w = lax.dynamic_slice(red[11:14], (0, me * (d // N_DEV)), (3, d // N_DEV))

    res = {"w_mod": _adamw(g_mod[None], w_mod[0], m_w_mod[0], v_w_mod[0], "adamw_w_mod")}
    after = res["w_mod"][0]
    for wname, w, m, v in (("w_ff2", w_ff2, m_w_ff2, v_w_ff2), ("w_ff1", w_ff1, m_w_ff1, v_w_ff1),
                           ("w_o", w_o, m_w_o, v_w_o), ("w_conv_out", w_conv_out, m_w_conv_out, v_w_conv_out),
                           ("w_ret_out", w_ret_out, m_w_ret_out, v_w_ret_out), ("w_in", w_in, m_w_in, v_w_in)):
        parts = _exchange_wait(sent[wname], after, "rs_" + wname + "_wait")
        res[wname] = _adamw(parts, w[0], m[0], v[0], "adamw_" + wname)
        after = res[wname][0]
    res = {k: tuple(t[None] for t in val) for k, val in res.items()}

    small = [("c_ctx", g_c_ctx, c_ctx, m_c_ctx, v_c_ctx), ("b_mod", g_b_mod, b_mod, m_b_mod, v_b_mod),
             ("norm1_g", red[0:1], norm1_g, m_norm1_g, v_norm1_g), ("conv_w", g_conv_w, conv_w, m_conv_w, v_conv_w),
             ("ret_decay_fwd", misc[1:2, :HEADS], ret_decay_fwd, m_ret_decay_fwd, v_ret_decay_fwd),
             ("ret_decay_bwd", misc[2:3, :HEADS], ret_decay_bwd, m_ret_decay_bwd, v_ret_decay_bwd),
             ("norm2_g", red[1:2], norm2_g, m_norm2_g, v_norm2_g), ("final_g", red[2:3], final_g, m_final_g, v_final_g)]

    def flat(t):
        t = t.reshape(-1)
        return jnp.pad(t, (0, (-t.shape[0]) % 1024))

    packed = [jnp.concatenate([flat(item[j]) for item in small]).reshape(-1, 128) for j in range(1, 5)]
    outs = _adamw(packed[0][None], packed[1], packed[2], packed[3], "adamw_small")
    start = 0
    for name, _, w, _, _ in small:
        size = w.size
        res[name] = tuple(o.reshape(-1)[start:start + size].reshape(w.shape) for o in outs)
        start += size + (-size) % 1024

    order = ["c_ctx", "w_mod", "b_mod", "norm1_g", "w_in", "conv_w", "w_conv_out", "ret_decay_fwd", "ret_decay_bwd",
             "w_ret_out", "w_o", "norm2_g", "w_ff1", "w_ff2", "final_g"]
    loss = misc[0, 0]
    return (loss, grad_x[None], *[res[n][0] for n in order], *[res[n][1] for n in order],
            *[res[n][2] for n in order], *[res[n][3] for n in order])
```

```python
import functools

import jax
import jax.numpy as jnp
from jax import lax
from jax.experimental import pallas as pl
from jax.experimental.pallas import tpu as pltpu

F32 = jnp.float32
BF16 = jnp.bfloat16
MESH = pl.DeviceIdType.MESH

N_DEV = 8
HEADS = 8
N_MOD = 6
N_IN = 11
GRID_W = 64
CHUNK = 128
ROPE_BASE = 10000.0
EPS = 1e-6
ADAM_LR, ADAM_B1, ADAM_B2, ADAM_EPS, ADAM_WD, ADAM_STEP = 0.001, 0.9, 0.999, 1e-08, 0.01, 10
VMEM_LIMIT = 56 * 1024 * 1024
HIGHEST = lax.Precision.HIGHEST
CB, CC, CX, CQ, CK, CV, CG, CGC, CGR = 0, 1, 2, 3, 4, 5, 7, 9, 10


def _tile(n, target, mult):
    t = (min(target, n) // mult) * mult
    while t >= mult:
        if n % t == 0:
            return t
        t -= mult
    return n


def _params(sem=None):
    return pltpu.CompilerParams(dimension_semantics=sem, vmem_limit_bytes=VMEM_LIMIT)


def _sig(v):
    return 1.0 / (1.0 + jnp.exp(-v))


def _coords():
    return lax.axis_index("x"), lax.axis_index("y"), lax.axis_index("c")


def _flip(p, m):
    return tuple(1 - v if (m >> s) & 1 else v for v, s in zip(p, (2, 1, 0)))


def _index(p):
    return 4 * p[0] + 2 * p[1] + p[2]


def _small_allgather(x, name):
    r, n = x.shape

    def body(x_ref, out_ref, send_sems, recv_sems):
        me = _coords()
        out_ref[pl.ds(_index(me), 1)] = x_ref[...][None]
        sent = []
        for m in range(1, N_DEV):
            cp = pltpu.make_async_remote_copy(
                src_ref=x_ref, dst_ref=out_ref.at[_index(me)], send_sem=send_sems.at[m - 1],
                recv_sem=recv_sems.at[m - 1], device_id=_flip(me, m), device_id_type=MESH)
            cp.start()
            sent.append(cp)
        for m in range(1, N_DEV):
            pltpu.make_async_remote_copy(
                src_ref=x_ref, dst_ref=out_ref.at[_index(_flip(me, m))], send_sem=send_sems.at[m - 1],
                recv_sem=recv_sems.at[m - 1], device_id=_flip(me, m), device_id_type=MESH).wait_recv()
        for cp in sent:
            cp.wait_send()

    return pl.pallas_call(
        body, name=name, out_shape=jax.ShapeDtypeStruct((N_DEV, r, n), x.dtype),
        in_specs=[pl.BlockSpec(memory_space=pltpu.VMEM)], out_specs=pl.BlockSpec(memory_space=pltpu.VMEM),
        scratch_shapes=[pltpu.SemaphoreType.DMA((N_DEV - 1,)), pltpu.SemaphoreType.DMA((N_DEV - 1,))],
    )(x)


def _window(ref, j, r, c, axis):
    if axis == 0:
        return ref.at[pl.ds(pl.multiple_of(j * r, 8), r), :]
    return ref.at[:, pl.ds(pl.multiple_of(j * c, 128), c)]


def _allgather(x, axis, name):
    r, c = x.shape
    full = (N_DEV * r, c) if axis == 0 else (r, N_DEV * c)

    def body(x_ref, out_ref, send_sems, recv_sems, local_sem):
        me = _coords()
        sibling = _flip(me, 1)
        chips = [4, 2, 6]

        def copy(k, block, to, src=None):
            dst = _window(out_ref, _index(block), r, c, axis)
            return pltpu.make_async_remote_copy(
                src_ref=dst if src is None else src, dst_ref=dst, send_sem=send_sems.at[k],
                recv_sem=recv_sems.at[k], device_id=to, device_id_type=MESH)

        mine = pltpu.make_async_copy(x_ref, _window(out_ref, _index(me), r, c, axis), local_sem)
        mine.start()
        first = [copy(0, me, sibling, src=x_ref)]
        first += [copy(1 + j, me, _flip(me, m), src=x_ref) for j, m in enumerate(chips)]
        for cp in first:
            cp.start()
        passed = [copy(4 + j, _flip(me, m), sibling) for j, m in enumerate(chips)]
        for j, m in enumerate(chips):
            copy(1 + j, _flip(me, m), me).wait_recv()
            passed[j].start()
        copy(0, sibling, me).wait_recv()
        for j, m in enumerate(chips):
            copy(4 + j, _flip(sibling, m), me).wait_recv()
        for cp in first + passed:
            cp.wait_send()
        mine.wait()

    return pl.pallas_call(
        body, name=name, out_shape=jax.ShapeDtypeStruct(full, x.dtype),
        in_specs=[pl.BlockSpec(memory_space=pl.ANY)], out_specs=pl.BlockSpec(memory_space=pl.ANY),
        scratch_shapes=[pltpu.SemaphoreType.DMA((N_DEV - 1,)), pltpu.SemaphoreType.DMA((N_DEV - 1,)),
                        pltpu.SemaphoreType.DMA(())],
    )(x)


def _scatter_exchange(g, axis, name):
    if axis == 0:
        r, c = g.shape[0] // N_DEV, g.shape[1]
    else:
        r, c = g.shape[0], g.shape[1] // N_DEV

    def body(g_ref, out_ref, send_sems, recv_sems, local_sem):
        me = _coords()
        mine = pltpu.make_async_copy(_window(g_ref, _index(me), r, c, axis), out_ref.at[_index(me)], local_sem)
        mine.start()
        sent = []
        for m in range(1, N_DEV):
            peer = _flip(me, m)
            cp = pltpu.make_async_remote_copy(
                src_ref=_window(g_ref, _index(peer), r, c, axis), dst_ref=out_ref.at[_index(me)],
                send_sem=send_sems.at[m - 1], recv_sem=recv_sems.at[m - 1], device_id=peer, device_id_type=MESH)
            cp.start()
            sent.append(cp)
        for m in range(1, N_DEV):
            peer = _flip(me, m)
            pltpu.make_async_remote_copy(
                src_ref=_window(g_ref, _index(me), r, c, axis), dst_ref=out_ref.at[_index(peer)],
                send_sem=send_sems.at[m - 1], recv_sem=recv_sems.at[m - 1], device_id=peer,
                device_id_type=MESH).wait_recv()
        for cp in sent:
            cp.wait_send()
        mine.wait()

    return pl.pallas_call(
        body, name=name, out_shape=jax.ShapeDtypeStruct((N_DEV, r, c), g.dtype),
        in_specs=[pl.BlockSpec(memory_space=pl.ANY)], out_specs=pl.BlockSpec(memory_space=pl.ANY),
        scratch_shapes=[pltpu.SemaphoreType.DMA((N_DEV - 1,)), pltpu.SemaphoreType.DMA((N_DEV - 1,)),
                        pltpu.SemaphoreType.DMA(())],
    )(g)


_HBM = pl.BlockSpec(memory_space=pltpu.HBM)
_SEM = pl.BlockSpec(memory_space=pltpu.SEMAPHORE)
_ANY = pl.BlockSpec(memory_space=pl.ANY)
_EFFECT = pltpu.SideEffectType.DATAFLOW_SIDE_EFFECTING


def _place_shard(w, axis, me, name):
    r, c = w.shape
    full = (N_DEV * r, c) if axis == 0 else (r, N_DEV * c)
    tr = _tile(r, max(16, (2 * 1024 * 1024) // (c * 4)), 16)
    blocks = r // tr

    def out_map(i, me_ref):
        return (me_ref[0] * blocks + i, 0) if axis == 0 else (i, me_ref[0])

    def body(me_ref, w_ref, out_ref):
        out_ref[...] = w_ref[...].astype(BF16)

    return pl.pallas_call(
        body, name=name, out_shape=jax.ShapeDtypeStruct(full, BF16),
        grid_spec=pltpu.PrefetchScalarGridSpec(
            num_scalar_prefetch=1, grid=(blocks,), in_specs=[pl.BlockSpec((tr, c), lambda i, me_ref: (i, 0))],
            out_specs=pl.BlockSpec((tr, c), out_map)),
        compiler_params=_params(("parallel",)))(me.reshape(1), w)


def _exchange_start(src, axis, gather, name, after=()):
    r, c = (src.shape[0] // N_DEV, src.shape[1]) if axis == 0 else (src.shape[0], src.shape[1] // N_DEV)
    n_hbm = 1 if gather else 2
    n_after = len(after)

    def body(*refs):
        src_ref, land_ref = refs[0], refs[n_hbm - 1]
        send_sems, recv_sems = refs[n_hbm + n_after:n_hbm + n_after + 2]
        token = refs[-1]
        me = _coords()
        for m in range(1, N_DEV):
            peer = _flip(me, m)
            if gather:
                mine = theirs = _window(land_ref, _index(me), r, c, axis)
            else:
                mine, theirs = _window(src_ref, _index(peer), r, c, axis), land_ref.at[m - 1]
            pltpu.make_async_remote_copy(
                src_ref=mine, dst_ref=theirs, send_sem=send_sems.at[m - 1], recv_sem=recv_sems.at[m - 1],
                device_id=peer, device_id_type=MESH).start()
        token[...] = jnp.zeros_like(token)

    hbm = [pltpu.with_memory_space_constraint(src, pltpu.HBM)]
    if not gather:
        hbm.append(pltpu.with_memory_space_constraint(lax.empty((N_DEV - 1, r, c), src.dtype), pltpu.HBM))
    outs = pl.pallas_call(
        body, name=name,
        out_shape=(pltpu.SemaphoreType.DMA((N_DEV - 1,)), pltpu.SemaphoreType.DMA((N_DEV - 1,)),
                   *[pltpu.HBM(t.shape, t.dtype) for t in hbm], jax.ShapeDtypeStruct((8, 128), F32)),
        in_specs=[_HBM] * n_hbm + [_ANY] * n_after,
        out_specs=(_SEM, _SEM, *[_HBM] * n_hbm, pl.BlockSpec(memory_space=pltpu.VMEM)),
        input_output_aliases={i: 2 + i for i in range(n_hbm)},
        compiler_params=pltpu.CompilerParams(has_side_effects=_EFFECT),
    )(*hbm, *after)
    return (outs[:2], outs[2:2 + n_hbm], (axis, gather, r, c)), outs[-1]


def _exchange_wait(handle, after, name):
    (send_sems, recv_sems), hbm, (axis, gather, r, c) = handle
    n_hbm = len(hbm)

    def body(*refs):
        src_ref, land_ref = refs[0], refs[n_hbm - 1]
        send_sems, recv_sems = refs[n_hbm:n_hbm + 2]
        me = _coords()
        for m in range(1, N_DEV):
            peer = _flip(me, m)
            if gather:
                mine, theirs = _window(land_ref, _index(me), r, c, axis), _window(land_ref, _index(peer), r, c, axis)
            else:
                mine, theirs = _window(src_ref, _index(peer), r, c, axis), land_ref.at[m - 1]
            copy = pltpu.make_async_remote_copy(
                src_ref=mine, dst_ref=theirs, send_sem=send_sems.at[m - 1], recv_sem=recv_sems.at[m - 1],
                device_id=peer, device_id_type=MESH)
            copy.wait_send()
            copy.wait_recv()

    outs = pl.pallas_call(
        body, name=name, out_shape=tuple(pltpu.HBM(t.shape, t.dtype) for t in hbm),
        in_specs=[_HBM] * n_hbm + [_SEM, _SEM, _ANY], out_specs=tuple([_HBM] * n_hbm),
        input_output_aliases={i: i for i in range(n_hbm)},
        compiler_params=pltpu.CompilerParams(has_side_effects=_EFFECT),
    )(*hbm, send_sems, recv_sems, after)
    return outs[0] if gather else tuple(outs)


def _mm(a, b, mode, out_dtypes, name, tm=1024, tn=1024, tk=2048, extras=(), epilogue=None, dep=None):
    if mode == "nn":
        (m, k), n = a.shape, b.shape[1]
    elif mode == "nt":
        (m, k), n = a.shape, b.shape[0]
    else:
        (k, m), n = a.shape, b.shape[1]
    tm, tn = _tile(m, tm, 8), _tile(n, tn, 128)
    tk = _tile(k, tk, 16 if mode == "tn" else 128)
    nk = k // tk
    swap = nk == 1 and (k * n + (n // tn) * m * k) < (m * k + (m // tm) * k * n)

    def ij(p, q):
        return (q, p) if swap else (p, q)

    def spec(shape, fn):
        return pl.BlockSpec(shape, lambda p, q, kk: fn(*ij(p, q), kk))

    a_spec = spec((tk, tm), lambda i, j, kk: (kk, i)) if mode == "tn" else spec((tm, tk), lambda i, j, kk: (i, kk))
    b_spec = spec((tn, tk), lambda i, j, kk: (j, kk)) if mode == "nt" else spec((tk, tn), lambda i, j, kk: (kk, j))
    dims = {"nn": (((1,), (0,)), ((), ())), "nt": (((1,), (1,)), ((), ())), "tn": (((0,), (0,)), ((), ()))}[mode]
    ex_specs = [spec((tm, tn), functools.partial(lambda i, j, kk, off: (i, j + off), off=off)) for _, off in extras]
    deps = [] if dep is None else [dep]
    dep_specs = [pl.BlockSpec(dep.shape, lambda p, q, kk: (0, 0))] if deps else []
    n_ex, n_out = len(extras), len(out_dtypes)
    n_in = 2 + n_ex + len(deps)

    def body(*refs):
        a_ref, b_ref = refs[0], refs[1]
        ex_refs = refs[2:2 + n_ex]
        out_refs = refs[n_in:n_in + n_out]

        def product():
            return lax.dot_general(a_ref[...], b_ref[...], dims, preferred_element_type=F32)

        def finish(res):
            res = epilogue(res, *[e[...] for e in ex_refs]) if epilogue is not None else (res,)
            for o_ref, val in zip(out_refs, res):
                o_ref[...] = val.astype(o_ref.dtype)

        if nk == 1:
            finish(product())
            return
        acc = refs[-1]
        kk = pl.program_id(2)

        @pl.when(kk == 0)
        def _():
            acc[...] = product()

        @pl.when((kk > 0) & (kk < nk - 1))
        def _():
            acc[...] += product()

        @pl.when(kk == nk - 1)
        def _():
            finish(acc[...] + product())

    outs = pl.pallas_call(
        body, name=name, grid=(*ij(m // tm, n // tn), nk),
        in_specs=[a_spec, b_spec] + ex_specs + dep_specs,
        out_specs=[spec((tm, tn), lambda i, j, kk: (i, j)) for _ in out_dtypes],
        out_shape=[jax.ShapeDtypeStruct((m, n), dt) for dt in out_dtypes],
        scratch_shapes=[pltpu.VMEM((tm, tn), F32)] if nk > 1 else [],
        compiler_params=_params(("parallel", "parallel", "arbitrary")),
    )(a, b, *[e for e, _ in extras], *deps)
    return outs if n_out > 1 else outs[0]


def _rspec(tr, w, cb=0, off=0):
    return pl.BlockSpec((tr, w), lambda i: (i + off, cb))


def _cspec(shape):
    return pl.BlockSpec(shape, lambda i: (0,) * len(shape))


def _rms(xf):
    rstd = lax.rsqrt(jnp.mean(xf * xf, axis=-1, keepdims=True) + EPS)
    return xf * rstd, rstd


def _rms_bwd(dn, n, rstd):
    return rstd * (dn - n * jnp.mean(dn * n, axis=-1, keepdims=True))


def _colsum(v):
    return jnp.sum(v, axis=0, keepdims=True)


def _total(v):
    return jnp.sum(jnp.sum(v, axis=1, keepdims=True), axis=0, keepdims=True)


def _modulate(x, vec, name):
    rows, d = x.shape
    tr = _tile(rows, 256, 8)

    def body(x_ref, v_ref, a_ref):
        n, _ = _rms(x_ref[...])
        a_ref[...] = ((n * v_ref[0:1, :]) * (1.0 + v_ref[2:3, :]) + v_ref[1:2, :]).astype(BF16)

    return pl.pallas_call(
        body, name=name, grid=(rows // tr,), in_specs=[_rspec(tr, d), _cspec(vec.shape)],
        out_specs=_rspec(tr, d), out_shape=jax.ShapeDtypeStruct((rows, d), BF16),
        compiler_params=_params(("parallel",)))(x, vec)


def _resid_modulate(x, y, vec, name):
    rows, d = x.shape
    tr = _tile(rows, 256, 8)

    def body(x_ref, y_ref, v_ref, x1_ref, a_ref):
        x1 = x_ref[...] + v_ref[3:4, :] * y_ref[...]
        x1_ref[...] = x1
        n, _ = _rms(x1)
        a_ref[...] = ((n * v_ref[0:1, :]) * (1.0 + v_ref[2:3, :]) + v_ref[1:2, :]).astype(BF16)

    return pl.pallas_call(
        body, name=name, grid=(rows // tr,), in_specs=[_rspec(tr, d), _rspec(tr, d), _cspec(vec.shape)],
        out_specs=[_rspec(tr, d), _rspec(tr, d)],
        out_shape=[jax.ShapeDtypeStruct((rows, d), F32), jax.ShapeDtypeStruct((rows, d), BF16)],
        compiler_params=_params(("parallel",)))(x, y, vec)


def _loss_head(x1, f, target, vec, name):
    rows, d = x1.shape
    tr = _tile(rows, 256, 8)

    def body(x1_ref, f_ref, t_ref, v_ref, dx2_ref, df_ref, acc_ref):
        @pl.when(pl.program_id(0) == 0)
        def _():
            acc_ref[...] = jnp.zeros_like(acc_ref)

        gate, gain = v_ref[0:1, :], v_ref[1:2, :]
        fv = f_ref[...]
        n, rstd = _rms(x1_ref[...] + gate * fv)
        err = n * gain - t_ref[...]
        dy = err / d
        dx2 = _rms_bwd(dy * gain, n, rstd)
        dx2_ref[...] = dx2
        df_ref[...] = (dx2 * gate).astype(BF16)
        acc_ref[0:1, :] += _colsum(dx2 * fv)
        acc_ref[1:2, :] += _colsum(dy * n)
        acc_ref[2:3, :] += _colsum(err * err)

    return pl.pallas_call(
        body, name=name, grid=(rows // tr,),
        in_specs=[_rspec(tr, d), _rspec(tr, d), _rspec(tr, d), _cspec(vec.shape)],
        out_specs=[_rspec(tr, d), _rspec(tr, d), _cspec((8, d))],
        out_shape=[jax.ShapeDtypeStruct((rows, d), F32), jax.ShapeDtypeStruct((rows, d), BF16),
                   jax.ShapeDtypeStruct((8, d), F32)],
        compiler_params=_params(("arbitrary",)))(x1, f, target, vec)


def _modulate_bwd(da, x, vec, name, da_off=0, dx_in=None, y=None):
    rows, d = x.shape
    tr = _tile(rows, 256, 8)
    off = da_off // tr
    has_dx, has_y = dx_in is not None, y is not None

    def body(*refs):
        da_ref, x_ref, v_ref = refs[0], refs[1], refs[2]
        pos = 3
        dxin_ref = refs[pos] if has_dx else None
        pos += has_dx
        y_ref = refs[pos] if has_y else None
        pos += has_y
        dx_ref = refs[pos]
        dy_ref = refs[pos + 1] if has_y else None
        acc_ref = refs[-1]

        @pl.when(pl.program_id(0) == 0)
        def _():
            acc_ref[...] = jnp.zeros_like(acc_ref)

        gain, scale1 = v_ref[0:1, :], 1.0 + v_ref[2:3, :]
        dav = da_ref[...]
        n, rstd = _rms(x_ref[...])
        dx = _rms_bwd(dav * gain * scale1, n, rstd)
        if has_dx:
            dx = dx + dxin_ref[...]
        dx_ref[...] = dx
        acc_ref[0:1, :] += _colsum(dav)
        acc_ref[1:2, :] += _colsum(dav * (n * gain))
        acc_ref[2:3, :] += _colsum(dav * n * scale1)
        if has_y:
            acc_ref[3:4, :] += _colsum(dx * y_ref[...])
            dy_ref[...] = (dx * v_ref[3:4, :]).astype(BF16)

    ins = [da, x, vec] + ([dx_in] if has_dx else []) + ([y] if has_y else [])
    in_specs = [_rspec(tr, d, off=off), _rspec(tr, d), _cspec(vec.shape)] + [_rspec(tr, d)] * (has_dx + has_y)
    out_specs = [_rspec(tr, d)] + ([_rspec(tr, d)] if has_y else []) + [_cspec((8, d))]
    out_shape = ([jax.ShapeDtypeStruct((rows, d), F32)] + ([jax.ShapeDtypeStruct((rows, d), BF16)] if has_y else [])
                 + [jax.ShapeDtypeStruct((8, d), F32)])
    return pl.pallas_call(
        body, name=name, grid=(rows // tr,), in_specs=in_specs, out_specs=out_specs, out_shape=out_shape,
        compiler_params=_params(("arbitrary",)))(*ins)


def _conv_terms(cc, cx, w_ref, tr):
    t = lax.broadcasted_iota(jnp.int32, (tr, 1), 0) % GRID_W
    first, last = t == 0, t == GRID_W - 1
    u = cc * cx
    prev = jnp.where(first, 0.0, pltpu.roll(u, 1, 0))
    nxt = jnp.where(last, 0.0, pltpu.roll(u, tr - 1, 0))
    y = w_ref[0:1, :] * prev + w_ref[1:2, :] * u + w_ref[2:3, :] * nxt
    return u, prev, nxt, y, first, last


def _conv_fwd(z, conv_w, rows, name):
    d = conv_w.shape[1]
    tr = _tile(rows, 256, GRID_W)

    def body(cb_ref, cc_ref, cx_ref, w_ref, out_ref):
        y = _conv_terms(cc_ref[...], cx_ref[...], w_ref, tr)[3]
        out_ref[...] = (cb_ref[...] * y).astype(BF16)

    return pl.pallas_call(
        body, name=name, grid=(rows // tr,),
        in_specs=[_rspec(tr, d, CB), _rspec(tr, d, CC), _rspec(tr, d, CX), _cspec(conv_w.shape)],
        out_specs=_rspec(tr, d), out_shape=jax.ShapeDtypeStruct((rows, d), BF16),
        compiler_params=_params(("parallel",)))(z, z, z, conv_w)


def _conv_bwd(dpc, z, conv_w, name):
    rows, d = dpc.shape
    tr = _tile(rows, 256, GRID_W)

    def body(dpc_ref, cb_ref, cc_ref, cx_ref, w_ref, dz_ref, acc_ref):
        @pl.when(pl.program_id(0) == 0)
        def _():
            acc_ref[...] = jnp.zeros_like(acc_ref)

        cc, cx, dpcv = cc_ref[...], cx_ref[...], dpc_ref[...]
        u, prev, nxt, y, first, last = _conv_terms(cc, cx, w_ref, tr)
        dy = dpcv * cb_ref[...]
        dy_next = jnp.where(last, 0.0, pltpu.roll(dy, tr - 1, 0))
        dy_prev = jnp.where(first, 0.0, pltpu.roll(dy, 1, 0))
        du = w_ref[0:1, :] * dy_next + w_ref[1:2, :] * dy + w_ref[2:3, :] * dy_prev
        dz_ref[:, 0:d] = (dpcv * y).astype(BF16)
        dz_ref[:, d:2 * d] = (du * cx).astype(BF16)
        dz_ref[:, 2 * d:3 * d] = (du * cc).astype(BF16)
        acc_ref[0:1, :] += _colsum(dy * prev)
        acc_ref[1:2, :] += _colsum(dy * u)
        acc_ref[2:3, :] += _colsum(dy * nxt)

    return pl.pallas_call(
        body, name=name, grid=(rows // tr,),
        in_specs=[_rspec(tr, d), _rspec(tr, d, CB), _rspec(tr, d, CC), _rspec(tr, d, CX), _cspec(conv_w.shape)],
        out_specs=[_rspec(tr, 3 * d), _cspec((8, d))],
        out_shape=[jax.ShapeDtypeStruct((rows, 3 * d), BF16), jax.ShapeDtypeStruct((8, d), F32)],
        compiler_params=_params(("arbitrary",)))(dpc, z, z, z, conv_w)


def _rotary_fwd(z, cos, sin, n_lat, name):
    rows = z.shape[0]
    d = z.shape[1] // N_IN
    dk = d // HEADS
    half = dk // 2
    tr = _tile(n_lat, 256, 8)
    tr = _tile(rows - n_lat, tr, 8)
    lat_blocks = n_lat // tr

    def body(q_ref, k_ref, v0_ref, v1_ref, cos_ref, sin_ref, qo_ref, ko_ref, vo_ref):
        cs, sn = cos_ref[...], sin_ref[...]
        keep = jnp.where(pl.program_id(0) < lat_blocks, 1.0, 0.0)
        for src, dst, scale in ((q_ref, qo_ref, keep), (k_ref, ko_ref, dk ** -0.5)):
            for h in range(HEADS):
                lo, mid, hi = h * dk, h * dk + half, (h + 1) * dk
                t1, t2 = src[:, lo:mid], src[:, mid:hi]
                dst[:, lo:mid] = ((t1 * cs - t2 * sn) * scale).astype(BF16)
                dst[:, mid:hi] = ((t1 * sn + t2 * cs) * scale).astype(BF16)
        vo_ref[:, 0:d] = v0_ref[...].astype(BF16)
        vo_ref[:, d:2 * d] = v1_ref[...].astype(BF16)

    return pl.pallas_call(
        body, name=name, grid=(rows // tr,),
        in_specs=[_rspec(tr, d, CQ), _rspec(tr, d, CK), _rspec(tr, d, CV), _rspec(tr, d, CV + 1),
                  _rspec(tr, half), _rspec(tr, half)],
        out_specs=[_rspec(tr, d), _rspec(tr, d), _rspec(tr, 2 * d)],
        out_shape=[jax.ShapeDtypeStruct((rows, d), BF16), jax.ShapeDtypeStruct((rows, d), BF16),
                   jax.ShapeDtypeStruct((rows, 2 * d), BF16)],
        compiler_params=_params(("parallel",)))(z, z, z, z, cos, sin)


def _qkv_bwd(dq_f, dq_b, dk_f, dk_b, dv_f, dv_b, cos, sin, n_lat, n_ctx, name):
    rows = n_lat + n_ctx
    d = dq_f.shape[1]
    dk = d // HEADS
    half = dk // 2
    tr = _tile(n_ctx, 128, 8)
    lat_blocks, ctx_blocks = n_lat // tr, n_ctx // tr

    def fmap(i):
        return (jnp.where(i < lat_blocks, i + ctx_blocks, i - lat_blocks), 0)

    def bmap(i):
        return (i + ctx_blocks, 0)

    def body(qf_ref, qb_ref, kf_ref, kb_ref, vf_ref, vb_ref, cos_ref, sin_ref, qo_ref, ko_ref, vo_ref):
        cs, sn = cos_ref[...], sin_ref[...]
        keep = jnp.where(pl.program_id(0) < lat_blocks, 1.0, 0.0)
        for fa, fb, dst, scale in ((qf_ref, qb_ref, qo_ref, keep), (kf_ref, kb_ref, ko_ref, dk ** -0.5)):
            for h in range(HEADS):
                lo, mid, hi = h * dk, h * dk + half, (h + 1) * dk
                d1 = fa[:, lo:mid] + fb[:, lo:mid]
                d2 = fa[:, mid:hi] + fb[:, mid:hi]
                dst[:, lo:mid] = ((d1 * cs + d2 * sn) * scale).astype(BF16)
                dst[:, mid:hi] = ((d2 * cs - d1 * sn) * scale).astype(BF16)
        vo_ref[...] = (vf_ref[...] + vb_ref[...]).astype(BF16)

    return pl.pallas_call(
        body, name=name, grid=(rows // tr,),
        in_specs=[pl.BlockSpec((tr, d), fmap), pl.BlockSpec((tr, d), bmap), pl.BlockSpec((tr, d), fmap),
                  pl.BlockSpec((tr, d), bmap), pl.BlockSpec((tr, 2 * d), fmap), pl.BlockSpec((tr, 2 * d), bmap),
                  _rspec(tr, half), _rspec(tr, half)],
        out_specs=[_rspec(tr, d), _rspec(tr, d), _rspec(tr, 2 * d)],
        out_shape=[jax.ShapeDtypeStruct((rows, d), BF16), jax.ShapeDtypeStruct((rows, d), BF16),
                   jax.ShapeDtypeStruct((rows, 2 * d), BF16)],
        compiler_params=_params(("parallel",)))(dq_f, dq_b, dk_f, dk_b, dv_f, dv_b, cos, sin)


def _gn_fwd(o_f, o_b, z, rows, row_off, name):
    d = z.shape[1] // N_IN
    dv = 2 * d // HEADS
    tr = _tile(row_off, 128, 8)
    off = row_off // tr

    def body(of_ref, ob_ref, g0_ref, g1_ref, r_ref):
        for h in range(HEADS):
            lo, hi = h * dv, (h + 1) * dv
            g_ref, glo = (g0_ref, lo) if hi <= d else (g1_ref, lo - d)
            o = of_ref[:, lo:hi] + ob_ref[:, lo:hi]
            cen = o - jnp.mean(o, axis=-1, keepdims=True)
            on = cen * lax.rsqrt(jnp.mean(cen * cen, axis=-1, keepdims=True) + EPS)
            g = g_ref[:, glo:glo + dv]
            r_ref[:, lo:hi] = (g * _sig(g) * on).astype(BF16)

    return pl.pallas_call(
        body, name=name, grid=(rows // tr,),
        in_specs=[_rspec(tr, 2 * d, off=off), _rspec(tr, 2 * d, off=off), _rspec(tr, d, CG), _rspec(tr, d, CG + 1)],
        out_specs=_rspec(tr, 2 * d), out_shape=jax.ShapeDtypeStruct((rows, 2 * d), BF16),
        compiler_params=_params(("parallel",)))(o_f, o_b, z, z)


def _gn_bwd(dr, o_f, o_b, z, row_off, name):
    rows = dr.shape[0]
    d = z.shape[1] // N_IN
    dv = 2 * d // HEADS
    tr = _tile(row_off, 128, 8)
    off = row_off // tr

    def body(dr_ref, of_ref, ob_ref, g0_ref, g1_ref, do_ref, dg_ref):
        for h in range(HEADS):
            lo, hi = h * dv, (h + 1) * dv
            g_ref, glo = (g0_ref, lo) if hi <= d else (g1_ref, lo - d)
            o = of_ref[:, lo:hi] + ob_ref[:, lo:hi]
            cen = o - jnp.mean(o, axis=-1, keepdims=True)
            rstd = lax.rsqrt(jnp.mean(cen * cen, axis=-1, keepdims=True) + EPS)
            on = cen * rstd
            g = g_ref[:, glo:glo + dv]
            sg = _sig(g)
            drv = dr_ref[:, lo:hi]
            dg_ref[:, lo:hi] = (drv * on * (sg * (1.0 + g * (1.0 - sg)))).astype(BF16)
            don = drv * (g * sg)
            do = rstd * (don - jnp.mean(don, axis=-1, keepdims=True)
                         - on * jnp.mean(don * on, axis=-1, keepdims=True))
            do_ref[:, lo:hi] = do.astype(BF16)

    return pl.pallas_call(
        body, name=name, grid=(rows // tr,),
        in_specs=[_rspec(tr, 2 * d), _rspec(tr, 2 * d, off=off), _rspec(tr, 2 * d, off=off),
                  _rspec(tr, d, CG), _rspec(tr, d, CG + 1)],
        out_specs=[_rspec(tr, 2 * d), _rspec(tr, 2 * d)],
        out_shape=[jax.ShapeDtypeStruct((rows, 2 * d), BF16), jax.ShapeDtypeStruct((rows, 2 * d), BF16)],
        compiler_params=_params(("parallel",)))(dr, o_f, o_b, z, z)


def _decays(lg, rev):
    row = lax.broadcasted_iota(jnp.int32, (CHUNK, CHUNK), 0)
    col = lax.broadcasted_iota(jnp.int32, (CHUNK, CHUNK), 1)
    rel = ((col - row) if rev else (row - col)).astype(F32)
    mask = jnp.where(rel >= 0, jnp.exp(lg * jnp.maximum(rel, 0.0)), 0.0)
    r = lax.broadcasted_iota(jnp.int32, (CHUNK, 1), 0)
    rr = ((CHUNK - 1 - r) if rev else r).astype(F32)
    chunk_decay = jnp.exp(lg * jnp.full((1, 1), float(CHUNK), F32))
    return rr, mask, jnp.exp(lg * (rr + 1.0)), jnp.exp(lg * (CHUNK - 1.0 - rr)), chunk_decay


_NT = (((1,), (1,)), ((), ()))
_TN = (((0,), (0,)), ((), ()))


def _dot(a, b, dims=None):
    if dims is None:
        return jnp.dot(a, b, preferred_element_type=F32)
    return lax.dot_general(a, b, dims, preferred_element_type=F32)


def _ret_fwd(q, k, v, lg, rev, name):
    tt, d = q.shape
    dk, dv = d // HEADS, 2 * d // HEADS
    nc = tt // CHUNK

    def cmap(h, i):
        return ((nc - 1 - i) if rev else i, h)

    def body(lg_ref, q_ref, k_ref, v_ref, o_ref, s_ref, state):
        h = pl.program_id(0)

        @pl.when(pl.program_id(1) == 0)
        def _():
            state[...] = jnp.zeros_like(state)

        lgh = lg_ref[h]
        _, mask, qd, kd, cd = _decays(lgh, rev)
        qv, kv, vv = q_ref[...], k_ref[...], v_ref[...]
        st = state[...]
        p = _dot(qv, kv, _NT) * mask
        o_ref[...] = _dot(p.astype(BF16), vv) + _dot((qv * qd).astype(BF16), st.astype(BF16))
        s_ref[...] = st
        state[...] = cd * st + _dot((kv * kd).astype(BF16), vv, _TN)

    return pl.pallas_call(
        body, name=name, grid=(HEADS, nc),
        in_specs=[pl.BlockSpec(memory_space=pltpu.SMEM), pl.BlockSpec((CHUNK, dk), cmap),
                  pl.BlockSpec((CHUNK, dk), cmap), pl.BlockSpec((CHUNK, dv), cmap)],
        out_specs=[pl.BlockSpec((CHUNK, dv), cmap),
                   pl.BlockSpec((None, None, dk, dv), lambda h, i: (cmap(h, i)[0], h, 0, 0))],
        out_shape=[jax.ShapeDtypeStruct((tt, 2 * d), F32), jax.ShapeDtypeStruct((nc, HEADS, dk, dv), F32)],
        scratch_shapes=[pltpu.VMEM((dk, dv), F32)],
        compiler_params=_params(("parallel", "arbitrary")))(lg, q, k, v)


def _ret_bwd(q, k, v, do, states, lg, rev, name):
    tt, d = q.shape
    dk, dv = d // HEADS, 2 * d // HEADS
    nc = tt // CHUNK

    def cmap(h, i):
        return (i if rev else (nc - 1 - i), h)

    def body(lg_ref, q_ref, k_ref, v_ref, do_ref, s_ref, dq_ref, dk_ref, dv_ref, dlg_ref, dstate):
        h = pl.program_id(0)

        @pl.when(pl.program_id(1) == 0)
        def _():
            dstate[...] = jnp.zeros_like(dstate)
            dlg_ref[...] = jnp.zeros_like(dlg_ref)

        lgh = lg_ref[h]
        rr, mask, qd, kd, cd = _decays(lgh, rev)
        qv, kv, vv, dov = q_ref[...], k_ref[...], v_ref[...], do_ref[...]
        st, dst = s_ref[...], dstate[...]
        st_b, dst_b = st.astype(BF16), dst.astype(BF16)
        p = _dot(qv, kv, _NT) * mask
        da = (_dot(dov, vv, _NT) * mask).astype(BF16)
        dq = _dot(da, kv) + _dot(dov, st_b, _NT) * qd
        dkk = _dot(da, qv, _TN) + _dot(vv, dst_b, _NT) * kd
        dq_ref[...] = dq
        dk_ref[...] = dkk
        dv_ref[...] = _dot(p.astype(BF16), dov, _TN) + _dot((kv * kd).astype(BF16), dst_b)
        dnew = cd * dst + _dot((qv * qd).astype(BF16), dov, _TN)
        dstate[...] = dnew
        qf, kf = qv.astype(F32), kv.astype(F32)
        rows = rr * (jnp.sum(qf * dq, axis=1, keepdims=True) - jnp.sum(kf * dkk, axis=1, keepdims=True))
        dlg_ref[...] += _total(rows) + CHUNK * _total(st * dnew)

    return pl.pallas_call(
        body, name=name, grid=(HEADS, nc),
        in_specs=[pl.BlockSpec(memory_space=pltpu.SMEM), pl.BlockSpec((CHUNK, dk), cmap),
                  pl.BlockSpec((CHUNK, dk), cmap), pl.BlockSpec((CHUNK, dv), cmap), pl.BlockSpec((CHUNK, dv), cmap),
                  pl.BlockSpec((None, None, dk, dv), lambda h, i: (cmap(h, i)[0], h, 0, 0))],
        out_specs=[pl.BlockSpec((CHUNK, dk), cmap), pl.BlockSpec((CHUNK, dk), cmap), pl.BlockSpec((CHUNK, dv), cmap),
                   pl.BlockSpec((None, 8, 128), lambda h, i: (h, 0, 0))],
        out_shape=[jax.ShapeDtypeStruct((tt, d), F32), jax.ShapeDtypeStruct((tt, d), F32),
                   jax.ShapeDtypeStruct((tt, 2 * d), F32), jax.ShapeDtypeStruct((HEADS, 8, 128), F32)],
        scratch_shapes=[pltpu.VMEM((dk, dv), F32)],
        compiler_params=_params(("parallel", "arbitrary")))(lg, q, k, v, do, states)


def _silu(v):
    return v * _sig(v)


def _mod_proj(cs, w_mod, b_loc, dec, name):
    nb = w_mod.shape[1]

    def body(cs_ref, w_ref, b_ref, dec_ref, out_ref, lg_ref):
        out_ref[...] = jnp.dot(_silu(cs_ref[...]), w_ref[...], preferred_element_type=F32, precision=HIGHEST) + b_ref[...]
        a = dec_ref[...]
        lg_ref[...] = jnp.minimum(a, 0.0) - jnp.log1p(jnp.exp(-jnp.abs(a)))

    return pl.pallas_call(
        body, name=name,
        out_shape=[jax.ShapeDtypeStruct((16, nb), F32), jax.ShapeDtypeStruct(dec.shape, F32)],
        compiler_params=_params())(cs, w_mod, b_loc, dec)


def _mod_grad(cs, dm, w_mod, name):
    d, nb = w_mod.shape

    def body(cs_ref, dm_ref, w_ref, gw_ref, part_ref):
        dmv = dm_ref[...]
        gw_ref[...] = lax.dot_general(_silu(cs_ref[...]), dmv, _TN, preferred_element_type=F32, precision=HIGHEST)
        part_ref[...] = lax.dot_general(dmv, w_ref[...], _NT, preferred_element_type=F32, precision=HIGHEST)

    return pl.pallas_call(
        body, name=name,
        out_shape=[jax.ShapeDtypeStruct((d, nb), F32), jax.ShapeDtypeStruct((16, d), F32)],
        compiler_params=_params())(cs, dm, w_mod)


def _reduce_small(gathered, dec, n_feat, name):
    _, rows, d = gathered.shape

    def body(g_ref, dec_ref, red_ref, misc_ref):
        total = g_ref[0]
        for i in range(1, N_DEV):
            total = total + g_ref[i]
        red_ref[...] = total
        misc_ref[...] = jnp.zeros_like(misc_ref)
        misc_ref[0:1, :] = jnp.zeros((1, 128), F32) + (0.5 / n_feat) * _total(total[16:17, :])
        misc_ref[1:3, :] = total[14:16, 0:128] * _sig(-dec_ref[0:2, :])

    return pl.pallas_call(
        body, name=name,
        out_shape=[jax.ShapeDtypeStruct((rows, d), F32), jax.ShapeDtypeStruct((8, 128), F32)],
        compiler_params=_params())(gathered, dec)


def _c_ctx_grad(parts, c_ctx, name):
    d = c_ctx.shape[1]

    def body(p_ref, c_ref, out_ref):
        total = p_ref[0]
        for i in range(1, N_DEV):
            total = total + p_ref[i]
        cv = c_ref[...]
        sg = _sig(cv)
        out_ref[...] = total[8:9, :] * (sg * (1.0 + cv * (1.0 - sg)))

    return pl.pallas_call(body, name=name, out_shape=jax.ShapeDtypeStruct((1, d), F32),
                          compiler_params=_params())(parts, c_ctx)


def _adamw(parts, w, m, v, name, own=None):
    n_parts, rows, cols = parts.shape
    row_bytes = cols * (parts.dtype.itemsize * (n_parts + 1) + 7 * 4)
    tr = _tile(rows, max(16, (4 * 1024 * 1024) // row_bytes), 16 if rows % 16 == 0 else 8)
    blocks = rows // tr

    def body(*refs):
        p_ref, w_ref, m_ref, v_ref, g_ref, d_ref, mo_ref, vo_ref = refs[-8:]
        if own is None:
            g = p_ref[0].astype(F32)
        else:
            g = refs[1][...].astype(F32) + p_ref[0].astype(F32)
        for i in range(1, n_parts):
            g = g + p_ref[i].astype(F32)
        m2 = ADAM_B1 * m_ref[...] + (1.0 - ADAM_B1) * g
        v2 = ADAM_B2 * v_ref[...] + (1.0 - ADAM_B2) * jnp.square(g)
        m_hat = m2 / (1.0 - ADAM_B1 ** ADAM_STEP)
        v_hat = v2 / (1.0 - ADAM_B2 ** ADAM_STEP)
        g_ref[...] = g
        d_ref[...] = -ADAM_LR * (m_hat / (jnp.sqrt(v_hat) + ADAM_EPS) + ADAM_WD * w_ref[...])
        mo_ref[...] = m2
        vo_ref[...] = v2

    out_shape = [jax.ShapeDtypeStruct((rows, cols), F32)] * 4
    if own is None:
        spec = _rspec(tr, cols)
        return pl.pallas_call(
            body, name=name, grid=(blocks,),
            in_specs=[pl.BlockSpec((n_parts, tr, cols), lambda i: (0, i, 0)), spec, spec, spec],
            out_specs=[spec] * 4, out_shape=out_shape, compiler_params=_params(("parallel",)))(parts, w, m, v)
    g_full, axis, me = own

    def own_map(i, me_ref):
        return (me_ref[0] * blocks + i, 0) if axis == 0 else (i, me_ref[0])

    spec = pl.BlockSpec((tr, cols), lambda i, me_ref: (i, 0))
    return pl.pallas_call(
        body, name=name, out_shape=out_shape,
        grid_spec=pltpu.PrefetchScalarGridSpec(
            num_scalar_prefetch=1, grid=(blocks,),
            in_specs=[pl.BlockSpec((tr, cols), own_map),
                      pl.BlockSpec((n_parts, tr, cols), lambda i, me_ref: (0, i, 0)), spec, spec, spec],
            out_specs=[spec] * 4),
        compiler_params=_params(("parallel",)))(me.reshape(1), g_full, parts, w, m, v)


def _rope_tables(pos, dk):
    half = dk // 2
    inv_freq = 1.0 / (ROPE_BASE ** jnp.linspace(0.0, 1.0, half, dtype=F32))
    ang = pos[:, None] * inv_freq[None, :]
    return jnp.cos(ang), jnp.sin(ang)


def _pad_lanes(v, width):
    return jnp.pad(v, ((0, 0), (0, width - v.shape[1])))


def kernel(x, c, ctx, c_ctx, w_mod, b_mod, norm1_g, w_in, conv_w, w_conv_out, ret_decay_fwd, ret_decay_bwd, w_ret_out, w_o, norm2_g, w_ff1, w_ff2, final_g, loss_target, m_c_ctx, m_w_mod, m_b_mod, m_norm1_g, m_w_in, m_conv_w, m_w_conv_out, m_ret_decay_fwd, m_ret_decay_bwd, m_w_ret_out, m_w_o, m_norm2_g, m_w_ff1, m_w_ff2, m_final_g, v_c_ctx, v_w_mod, v_b_mod, v_norm1_g, v_w_in, v_conv_w, v_w_conv_out, v_ret_decay_fwd, v_ret_decay_bwd, v_w_ret_out, v_w_o, v_norm2_g, v_w_ff1, v_w_ff2, v_final_g):
    n_lat, d = x.shape[1], x.shape[2]
    n_ctx = ctx.shape[1]
    dk = d // HEADS
    nb = w_mod.shape[2]
    me = 4 * lax.axis_index("x") + 2 * lax.axis_index("y") + lax.axis_index("c")
    xl, ctxl, target = x[0], ctx[0], loss_target[0]

    w_in_f = _allgather(w_in[0].astype(BF16), 1, "ag_w_in")
    conv_w_f = _small_allgather(conv_w[0], "ag_conv_w").transpose(1, 0, 2).reshape(3, d)

    c_all = _small_allgather(c, "ag_c").reshape(N_DEV, d)
    cs = jnp.concatenate([c_all, c_ctx[None], jnp.zeros((7, d), F32)], axis=0)
    dec = jnp.pad(jnp.concatenate([ret_decay_fwd, ret_decay_bwd], axis=0), ((0, 6), (0, 128 - HEADS)))
    b_loc = lax.dynamic_slice(b_mod, (0, me * nb), (1, nb))
    modp, lgs = _mod_proj(cs, w_mod[0], b_loc, dec, "mod_proj")
    modp_all = _small_allgather(modp, "ag_mod")
    mod_l = lax.dynamic_index_in_dim(modp_all, me, axis=1, keepdims=False).reshape(N_MOD, d)
    mod_c = modp_all[:, 8, :].reshape(N_MOD, d)
    lg_f, lg_b = lgs[0, :HEADS], lgs[1, :HEADS]
    zero_row = jnp.zeros((1, d), F32)
    vec1_l = jnp.concatenate([norm1_g, mod_l[0:1], mod_l[1:2], zero_row], axis=0)
    vec1_c = jnp.concatenate([norm1_g, mod_c[0:1], mod_c[1:2], zero_row], axis=0)
    vec2 = jnp.concatenate([norm2_g, mod_l[3:4], mod_l[4:5], mod_l[2:3]], axis=0)
    vec3 = jnp.concatenate([mod_l[5:6], final_g[None]], axis=0)

    later, after = {}, (w_in_f, conv_w_f, modp_all)
    for wname, w, axis in (("w_conv_out", w_conv_out, 0), ("w_ret_out", w_ret_out, 0), ("w_o", w_o, 0),
                           ("w_ff1", w_ff1, 1), ("w_ff2", w_ff2, 0)):
        later[wname], token = _exchange_start(_place_shard(w[0], axis, me, "place_" + wname), axis, True,
                                              "ag_" + wname + "_start", after=after)
        after = (token,)

    a_all = jnp.concatenate([_modulate(xl, vec1_l, "modulate1"), _modulate(ctxl, vec1_c, "modulate1_ctx")], axis=0)
    z = _mm(a_all, w_in_f, "nn", [F32], "proj_in", tn=2048, dep=token)
    pc = _conv_fwd(z, conv_w_f, n_lat, "conv_fwd")
    pos = jnp.concatenate([n_ctx + jnp.arange(n_lat, dtype=F32), jnp.arange(n_ctx, dtype=F32)])
    cos, sin = _rope_tables(pos, dk)
    q_m, k_m, v_m = _rotary_fwd(z, cos, sin, n_lat, "rotary_fwd")
    q_s, k_s, v_s = (jnp.concatenate([t[n_lat:], t], axis=0) for t in (q_m, k_m, v_m))
    o_f, st_f = _ret_fwd(q_s, k_s, v_s, lg_f, False, "ret_fwd_f")
    o_b, st_b = _ret_fwd(q_s, k_s, v_s, lg_b, True, "ret_fwd_b")
    r = _gn_fwd(o_f, o_b, z, n_lat, n_ctx, "gn_fwd")
    w_conv_out_f = _exchange_wait(later["w_conv_out"], r, "ag_w_conv_out_wait")
    w_ret_out_f = _exchange_wait(later["w_ret_out"], r, "ag_w_ret_out_wait")
    w_o_f = _exchange_wait(later["w_o"], r, "ag_w_o_wait")
    y_conv = _mm(pc, w_conv_out_f, "nn", [F32], "proj_conv_out", tn=2048)
    tn_d = _tile(d, 1024, 128)
    gate_offs = (CGC * d // tn_d, CGR * d // tn_d)

    def merge(acc, yc, gc, gr):
        return acc, _sig(gc) * yc + _sig(gr) * acc

    y_ret, mg = _mm(r, w_ret_out_f, "nn", [F32, BF16], "proj_ret_out", tm=512, tn=tn_d,
                    extras=[(y_conv, 0), (z, gate_offs[0]), (z, gate_offs[1])], epilogue=merge)
    y_l = _mm(mg, w_o_f, "nn", [F32], "proj_o", tn=2048)
    x1, a2 = _resid_modulate(xl, y_l, vec2, "resid_modulate2")

    def sqrelu(acc):
        return acc, jnp.square(jnp.maximum(acc, 0.0))

    w_ff1_f = _exchange_wait(later["w_ff1"], a2, "ag_w_ff1_wait")
    hff, s = _mm(a2, w_ff1_f, "nn", [F32, BF16], "ff1", epilogue=sqrelu)
    w_ff2_f = _exchange_wait(later["w_ff2"], s, "ag_w_ff2_wait")
    f = _mm(s, w_ff2_f, "nn", [F32], "ff2")
    dx2, df, acc3 = _loss_head(x1, f, target, vec3, "loss_head")

    def d_sqrelu(acc, h):
        return (acc * (2.0 * jnp.maximum(h, 0.0)),)

    dh = _mm(df, w_ff2_f, "nt", [BF16], "ff2_dx", extras=[(hff, 0)], epilogue=d_sqrelu)
    sent = {}
    sent["w_ff2"], token = _exchange_start(_mm(s, df, "tn", [BF16], "ff2_dw"), 0, False, "rs_w_ff2_start")
    da2 = _mm(dh, w_ff1_f, "nt", [F32], "ff1_dx", dep=token)
    sent["w_ff1"], token = _exchange_start(_mm(a2, dh, "tn", [BF16], "ff1_dw"), 1, False, "rs_w_ff1_start")
    dx1, dyl, acc2 = _modulate_bwd(da2, x1, vec2, "modulate2_bwd", dx_in=dx2, y=y_l)

    def d_merge(acc, yc, yr, gc, gr):
        sc, sr = _sig(gc), _sig(gr)
        return acc * sc, acc * sr, acc * yc * (sc * (1.0 - sc)), acc * yr * (sr * (1.0 - sr))

    dyc, dyr, dgc, dgr = _mm(dyl, w_o_f, "nt", [BF16] * 4, "proj_o_dx", tm=512, tn=tn_d,
                             extras=[(y_conv, 0), (y_ret, 0), (z, gate_offs[0]), (z, gate_offs[1])], epilogue=d_merge,
                             dep=token)
    sent["w_o"], token = _exchange_start(_mm(mg, dyl, "tn", [BF16], "proj_o_dw"), 0, False, "rs_w_o_start")
    dpc = _mm(dyc, w_conv_out_f, "nt", [F32], "proj_conv_out_dx", tn=2048, dep=token)
    sent["w_conv_out"], token = _exchange_start(_mm(pc, dyc, "tn", [BF16], "proj_conv_out_dw"), 0, False,
                                                "rs_w_conv_out_start")
    dz_conv, acc_conv = _conv_bwd(dpc, z, conv_w_f, "conv_bwd")
    dr = _mm(dyr, w_ret_out_f, "nt", [F32], "proj_ret_out_dx", dep=token)
    sent["w_ret_out"], token = _exchange_start(_mm(r, dyr, "tn", [BF16], "proj_ret_out_dw"), 0, False,
                                               "rs_w_ret_out_start")
    do, dz_g = _gn_bwd(dr, o_f, o_b, z, n_ctx, "gn_bwd")
    do_s = jnp.pad(do, ((n_ctx, n_ctx), (0, 0)))
    dq_f, dk_f, dv_f, dlg_f = _ret_bwd(q_s, k_s, v_s, do_s, st_f, lg_f, False, "ret_bwd_f")
    dq_b, dk_b, dv_b, dlg_b = _ret_bwd(q_s, k_s, v_s, do_s, st_b, lg_b, True, "ret_bwd_b")
    dz_q, dz_k, dz_v = _qkv_bwd(dq_f, dq_b, dk_f, dk_b, dv_f, dv_b, cos, sin, n_lat, n_ctx, "qkv_bwd")

    def with_ctx_rows(t):
        return jnp.pad(t, ((0, n_ctx), (0, 0)))

    dz = jnp.concatenate([with_ctx_rows(dz_conv), dz_q, dz_k, dz_v, with_ctx_rows(dz_g), with_ctx_rows(dgc),
                          with_ctx_rows(dgr)], axis=1)
    g_in = _mm(a_all, dz, "tn", [BF16], "proj_in_dw", tn=2048, tk=(n_lat + n_ctx) // 4, dep=token)
    sent["w_in"], token = _exchange_start(g_in, 1, False, "rs_w_in_start")
    da_all = _mm(dz, w_in_f, "nt", [F32], "proj_in_dx", tn=2048, dep=token)
    grad_x, acc1 = _modulate_bwd(da_all, xl, vec1_l, "modulate1_bwd", dx_in=dx1)
    _, acc1c = _modulate_bwd(da_all, ctxl, vec1_c, "modulate1_ctx_bwd", da_off=n_lat)

    lane_pad = functools.partial(_pad_lanes, width=d)
    packet = jnp.concatenate([
        acc1[2:3] + acc1c[2:3], acc2[2:3], acc3[1:2],
        acc1[0:1], acc1[1:2], acc2[3:4], acc2[0:1], acc2[1:2], acc3[0:1],
        acc1c[0:1], acc1c[1:2],
        acc_conv[0:3],
        lane_pad(dlg_f[:, 0, 0][None]), lane_pad(dlg_b[:, 0, 0][None]),
        acc3[2:3],
        jnp.zeros((7, d), F32)], axis=0)
    gathered = _small_allgather(packet, "ag_small")
    red, misc = _reduce_small(gathered, dec, d, "reduce_small")
    dmod_ctx = jnp.concatenate([red[9], red[10], jnp.zeros((4 * d,), F32)])
    dmod_all = jnp.concatenate([gathered[:, 3:9, :].reshape(N_DEV, N_MOD * d), dmod_ctx[None]], axis=0)
    dm = jnp.pad(lax.dynamic_slice(dmod_all, (0, me * nb), (N_DEV + 1, nb)), ((0, 7), (0, 0)))
    g_mod, c_part = _mod_grad(cs, dm, w_mod[0], "mod_grad")
    g_c_ctx = _c_ctx_grad(_small_allgather(c_part, "ag_c_ctx"), c_ctx[None], "c_ctx_grad")
    g_b_mod = red[3:9].reshape(1, N_MOD * d) + dmod_ctx[None]
    g_conv_w = lax.dynamic_slice(red[11:14], (0, me * (d // N_DEV)), (3, d // N_DEV))

    res = {"w_mod": _adamw(g_mod[None], w_mod[0], m_w_mod[0], v_w_mod[0], "adamw_w_mod")}
    after = res["w_mod"][0]
    for wname, axis, w, m, v in (("w_ff2", 0, w_ff2, m_w_ff2, v_w_ff2), ("w_ff1", 1, w_ff1, m_w_ff1, v_w_ff1),
                                 ("w_o", 0, w_o, m_w_o, v_w_o), ("w_conv_out", 0, w_conv_out, m_w_conv_out, v_w_conv_out),
                                 ("w_ret_out", 0, w_ret_out, m_w_ret_out, v_w_ret_out),
                                 ("w_in", 1, w_in, m_w_in, v_w_in)):
        g_full, parts = _exchange_wait(sent[wname], after, "rs_" + wname + "_wait")
        res[wname] = _adamw(parts, w[0], m[0], v[0], "adamw_" + wname, own=(g_full, axis, me))
        after = res[wname][0]
    res = {k: tuple(t[None] for t in val) for k, val in res.items()}

    small = [("c_ctx", g_c_ctx, c_ctx, m_c_ctx, v_c_ctx), ("b_mod", g_b_mod, b_mod, m_b_mod, v_b_mod),
             ("norm1_g", red[0:1], norm1_g, m_norm1_g, v_norm1_g), ("conv_w", g_conv_w, conv_w, m_conv_w, v_conv_w),
             ("ret_decay_fwd", misc[1:2, :HEADS], ret_decay_fwd, m_ret_decay_fwd, v_ret_decay_fwd),
             ("ret_decay_bwd", misc[2:3, :HEADS], ret_decay_bwd, m_ret_decay_bwd, v_ret_decay_bwd),
             ("norm2_g", red[1:2], norm2_g, m_norm2_g, v_norm2_g), ("final_g", red[2:3], final_g, m_final_g, v_final_g)]

    def flat(t):
        t = t.reshape(-1)
        return jnp.pad(t, (0, (-t.shape[0]) % 1024))

    packed = [jnp.concatenate([flat(item[j]) for item in small]).reshape(-1, 128) for j in range(1, 5)]
    outs = _adamw(packed[0][None], packed[1], packed[2], packed[3], "adamw_small")
    start = 0
    for name, _, w, _, _ in small:
        size = w.size
        res[name] = tuple(o.reshape(-1)[start:start + size].reshape(w.shape) for o in outs)
        start += size + (-size) % 1024

    order = ["c_ctx", "w_mod", "b_mod", "norm1_g", "w_in", "conv_w", "w_conv_out", "ret_decay_fwd", "ret_decay_bwd",
             "w_ret_out", "w_o", "norm2_g", "w_ff1", "w_ff2", "final_g"]
    loss = misc[0, 0]
    return (loss, grad_x[None], *[res[n][0] for n in order], *[res[n][1] for n in order],
            *[res[n][2] for n in order], *[res[n][3] for n in order])
```

```python
import functools

import jax
import jax.numpy as jnp
from jax import lax
from jax.experimental import pallas as pl
from jax.experimental.pallas import tpu as pltpu

F32 = jnp.float32
BF16 = jnp.bfloat16
MESH = pl.DeviceIdType.MESH

N_DEV = 8
HEADS = 8
N_MOD = 6
N_IN = 11
GRID_W = 64
CHUNK = 256
ROPE_BASE = 10000.0
EPS = 1e-6
ADAM_LR, ADAM_B1, ADAM_B2, ADAM_EPS, ADAM_WD, ADAM_STEP = 0.001, 0.9, 0.999, 1e-08, 0.01, 10
VMEM_LIMIT = 56 * 1024 * 1024
HIGHEST = lax.Precision.HIGHEST
CB, CC, CX, CQ, CK, CV, CG, CGC, CGR = 0, 1, 2, 3, 4, 5, 7, 9, 10


def _tile(n, target, mult):
    t = (min(target, n) // mult) * mult
    while t >= mult:
        if n % t == 0:
            return t
        t -= mult
    return n


def _params(sem=None):
    return pltpu.CompilerParams(dimension_semantics=sem, vmem_limit_bytes=VMEM_LIMIT)


def _sig(v):
    return 1.0 / (1.0 + jnp.exp(-v))


def _coords():
    return lax.axis_index("x"), lax.axis_index("y"), lax.axis_index("c")


def _flip(p, m):
    return tuple(1 - v if (m >> s) & 1 else v for v, s in zip(p, (2, 1, 0)))


def _index(p):
    return 4 * p[0] + 2 * p[1] + p[2]


def _small_allgather(x, name):
    r, n = x.shape

    def body(x_ref, out_ref, send_sems, recv_sems):
        me = _coords()
        out_ref[pl.ds(_index(me), 1)] = x_ref[...][None]
        sent = []
        for m in range(1, N_DEV):
            cp = pltpu.make_async_remote_copy(
                src_ref=x_ref, dst_ref=out_ref.at[_index(me)], send_sem=send_sems.at[m - 1],
                recv_sem=recv_sems.at[m - 1], device_id=_flip(me, m), device_id_type=MESH)
            cp.start()
            sent.append(cp)
        for m in range(1, N_DEV):
            pltpu.make_async_remote_copy(
                src_ref=x_ref, dst_ref=out_ref.at[_index(_flip(me, m))], send_sem=send_sems.at[m - 1],
                recv_sem=recv_sems.at[m - 1], device_id=_flip(me, m), device_id_type=MESH).wait_recv()
        for cp in sent:
            cp.wait_send()

    return pl.pallas_call(
        body, name=name, out_shape=jax.ShapeDtypeStruct((N_DEV, r, n), x.dtype),
        in_specs=[pl.BlockSpec(memory_space=pltpu.VMEM)], out_specs=pl.BlockSpec(memory_space=pltpu.VMEM),
        scratch_shapes=[pltpu.SemaphoreType.DMA((N_DEV - 1,)), pltpu.SemaphoreType.DMA((N_DEV - 1,))],
    )(x)


def _window(ref, j, r, c, axis):
    if axis == 0:
        return ref.at[pl.ds(pl.multiple_of(j * r, 8), r), :]
    return ref.at[:, pl.ds(pl.multiple_of(j * c, 128), c)]


def _allgather(x, axis, name):
    r, c = x.shape
    full = (N_DEV * r, c) if axis == 0 else (r, N_DEV * c)

    def body(x_ref, out_ref, send_sems, recv_sems, local_sem):
        me = _coords()
        sibling = _flip(me, 1)
        chips = [4, 2, 6]

        def copy(k, block, to, src=None):
            dst = _window(out_ref, _index(block), r, c, axis)
            return pltpu.make_async_remote_copy(
                src_ref=dst if src is None else src, dst_ref=dst, send_sem=send_sems.at[k],
                recv_sem=recv_sems.at[k], device_id=to, device_id_type=MESH)

        mine = pltpu.make_async_copy(x_ref, _window(out_ref, _index(me), r, c, axis), local_sem)
        mine.start()
        first = [copy(0, me, sibling, src=x_ref)]
        first += [copy(1 + j, me, _flip(me, m), src=x_ref) for j, m in enumerate(chips)]
        for cp in first:
            cp.start()
        passed = [copy(4 + j, _flip(me, m), sibling) for j, m in enumerate(chips)]
        for j, m in enumerate(chips):
            copy(1 + j, _flip(me, m), me).wait_recv()
            passed[j].start()
        copy(0, sibling, me).wait_recv()
        for j, m in enumerate(chips):
            copy(4 + j, _flip(sibling, m), me).wait_recv()
        for cp in first + passed:
            cp.wait_send()
        mine.wait()

    return pl.pallas_call(
        body, name=name, out_shape=jax.ShapeDtypeStruct(full, x.dtype),
        in_specs=[pl.BlockSpec(memory_space=pl.ANY)], out_specs=pl.BlockSpec(memory_space=pl.ANY),
        scratch_shapes=[pltpu.SemaphoreType.DMA((N_DEV - 1,)), pltpu.SemaphoreType.DMA((N_DEV - 1,)),
                        pltpu.SemaphoreType.DMA(())],
    )(x)


_HBM = pl.BlockSpec(memory_space=pltpu.HBM)
_SEM = pl.BlockSpec(memory_space=pltpu.SEMAPHORE)
_ANY = pl.BlockSpec(memory_space=pl.ANY)
_EFFECT = pltpu.SideEffectType.DATAFLOW_SIDE_EFFECTING


def _place_shard(w, axis, me, name):
    r, c = w.shape
    full = (N_DEV * r, c) if axis == 0 else (r, N_DEV * c)
    tr = _tile(r, max(16, (2 * 1024 * 1024) // (c * 4)), 16)
    blocks = r // tr

    def out_map(i, me_ref):
        return (me_ref[0] * blocks + i, 0) if axis == 0 else (i, me_ref[0])

    def body(me_ref, w_ref, out_ref):
        out_ref[...] = w_ref[...].astype(BF16)

    return pl.pallas_call(
        body, name=name, out_shape=jax.ShapeDtypeStruct(full, BF16),
        grid_spec=pltpu.PrefetchScalarGridSpec(
            num_scalar_prefetch=1, grid=(blocks,), in_specs=[pl.BlockSpec((tr, c), lambda i, me_ref: (i, 0))],
            out_specs=pl.BlockSpec((tr, c), out_map)),
        compiler_params=_params(("parallel",)))(me.reshape(1), w)


def _exchange_start(src, axis, gather, name, after=()):
    r, c = (src.shape[0] // N_DEV, src.shape[1]) if axis == 0 else (src.shape[0], src.shape[1] // N_DEV)
    n_hbm = 1 if gather else 2
    n_after = len(after)

    def body(*refs):
        src_ref, land_ref = refs[0], refs[n_hbm - 1]
        send_sems, recv_sems = refs[n_hbm + n_after:n_hbm + n_after + 2]
        token = refs[-1]
        me = _coords()
        for m in range(1, N_DEV):
            peer = _flip(me, m)
            if gather:
                mine = theirs = _window(land_ref, _index(me), r, c, axis)
            else:
                mine, theirs = _window(src_ref, _index(peer), r, c, axis), land_ref.at[m - 1]
            pltpu.make_async_remote_copy(
                src_ref=mine, dst_ref=theirs, send_sem=send_sems.at[m - 1], recv_sem=recv_sems.at[m - 1],
                device_id=peer, device_id_type=MESH).start()
        token[...] = jnp.zeros_like(token)

    hbm = [pltpu.with_memory_space_constraint(src, pltpu.HBM)]
    if not gather:
        hbm.append(pltpu.with_memory_space_constraint(lax.empty((N_DEV - 1, r, c), src.dtype), pltpu.HBM))
    outs = pl.pallas_call(
        body, name=name,
        out_shape=(pltpu.SemaphoreType.DMA((N_DEV - 1,)), pltpu.SemaphoreType.DMA((N_DEV - 1,)),
                   *[pltpu.HBM(t.shape, t.dtype) for t in hbm], jax.ShapeDtypeStruct((8, 128), F32)),
        in_specs=[_HBM] * n_hbm + [_ANY] * n_after,
        out_specs=(_SEM, _SEM, *[_HBM] * n_hbm, pl.BlockSpec(memory_space=pltpu.VMEM)),
        input_output_aliases={i: 2 + i for i in range(n_hbm)},
        compiler_params=pltpu.CompilerParams(has_side_effects=_EFFECT),
    )(*hbm, *after)
    return (outs[:2], outs[2:2 + n_hbm], (axis, gather, r, c)), outs[-1]


def _exchange_wait(handle, after, name):
    (send_sems, recv_sems), hbm, (axis, gather, r, c) = handle
    n_hbm = len(hbm)

    def body(*refs):
        src_ref, land_ref = refs[0], refs[n_hbm - 1]
        send_sems, recv_sems = refs[n_hbm:n_hbm + 2]
        me = _coords()
        for m in range(1, N_DEV):
            peer = _flip(me, m)
            if gather:
                mine, theirs = _window(land_ref, _index(me), r, c, axis), _window(land_ref, _index(peer), r, c, axis)
            else:
                mine, theirs = _window(src_ref, _index(peer), r, c, axis), land_ref.at[m - 1]
            copy = pltpu.make_async_remote_copy(
                src_ref=mine, dst_ref=theirs, send_sem=send_sems.at[m - 1], recv_sem=recv_sems.at[m - 1],
                device_id=peer, device_id_type=MESH)
            copy.wait_send()
            copy.wait_recv()

    outs = pl.pallas_call(
        body, name=name, out_shape=tuple(pltpu.HBM(t.shape, t.dtype) for t in hbm),
        in_specs=[_HBM] * n_hbm + [_SEM, _SEM, _ANY], out_specs=tuple([_HBM] * n_hbm),
        input_output_aliases={i: i for i in range(n_hbm)},
        compiler_params=pltpu.CompilerParams(has_side_effects=_EFFECT),
    )(*hbm, send_sems, recv_sems, after)
    return outs[0] if gather else tuple(outs)


def _mm(a, b, mode, out_dtypes, name, tm=1024, tn=1024, tk=2048, extras=(), epilogue=None, dep=None):
    if mode == "nn":
        (m, k), n = a.shape, b.shape[1]
    elif mode == "nt":
        (m, k), n = a.shape, b.shape[0]
    else:
        (k, m), n = a.shape, b.shape[1]
    tm, tn = _tile(m, tm, 8), _tile(n, tn, 128)
    tk = _tile(k, tk, 16 if mode == "tn" else 128)
    nk = k // tk
    swap = nk == 1 and (k * n + (n // tn) * m * k) < (m * k + (m // tm) * k * n)

    def ij(p, q):
        return (q, p) if swap else (p, q)

    def spec(shape, fn):
        return pl.BlockSpec(shape, lambda p, q, kk: fn(*ij(p, q), kk))

    a_spec = spec((tk, tm), lambda i, j, kk: (kk, i)) if mode == "tn" else spec((tm, tk), lambda i, j, kk: (i, kk))
    b_spec = spec((tn, tk), lambda i, j, kk: (j, kk)) if mode == "nt" else spec((tk, tn), lambda i, j, kk: (kk, j))
    dims = {"nn": (((1,), (0,)), ((), ())), "nt": (((1,), (1,)), ((), ())), "tn": (((0,), (0,)), ((), ()))}[mode]
    ex_specs = [spec((tm, tn), functools.partial(lambda i, j, kk, off: (i, j + off), off=off)) for _, off in extras]
    deps = [] if dep is None else [dep]
    dep_specs = [pl.BlockSpec(dep.shape, lambda p, q, kk: (0, 0))] if deps else []
    n_ex, n_out = len(extras), len(out_dtypes)
    n_in = 2 + n_ex + len(deps)

    def body(*refs):
        a_ref, b_ref = refs[0], refs[1]
        ex_refs = refs[2:2 + n_ex]
        out_refs = refs[n_in:n_in + n_out]

        def product():
            return lax.dot_general(a_ref[...], b_ref[...], dims, preferred_element_type=F32)

        def finish(res):
            res = epilogue(res, *[e[...] for e in ex_refs]) if epilogue is not None else (res,)
            for o_ref, val in zip(out_refs, res):
                o_ref[...] = val.astype(o_ref.dtype)

        if nk == 1:
            finish(product())
            return
        acc = refs[-1]
        kk = pl.program_id(2)

        @pl.when(kk == 0)
        def _():
            acc[...] = product()

        @pl.when((kk > 0) & (kk < nk - 1))
        def _():
            acc[...] += product()

        @pl.when(kk == nk - 1)
        def _():
            finish(acc[...] + product())

    outs = pl.pallas_call(
        body, name=name, grid=(*ij(m // tm, n // tn), nk),
        in_specs=[a_spec, b_spec] + ex_specs + dep_specs,
        out_specs=[spec((tm, tn), lambda i, j, kk: (i, j)) for _ in out_dtypes],
        out_shape=[jax.ShapeDtypeStruct((m, n), dt) for dt in out_dtypes],
        scratch_shapes=[pltpu.VMEM((tm, tn), F32)] if nk > 1 else [],
        compiler_params=_params(("parallel", "parallel", "arbitrary")),
    )(a, b, *[e for e, _ in extras], *deps)
    return outs if n_out > 1 else outs[0]


def _rspec(tr, w, cb=0, off=0):
    return pl.BlockSpec((tr, w), lambda i: (i + off, cb))


def _cspec(shape):
    return pl.BlockSpec(shape, lambda i: (0,) * len(shape))


def _rms(xf):
    rstd = lax.rsqrt(jnp.mean(xf * xf, axis=-1, keepdims=True) + EPS)
    return xf * rstd, rstd


def _rms_bwd(dn, n, rstd):
    return rstd * (dn - n * jnp.mean(dn * n, axis=-1, keepdims=True))


def _colsum(v):
    return jnp.sum(v, axis=0, keepdims=True)


def _total(v):
    return jnp.sum(jnp.sum(v, axis=1, keepdims=True), axis=0, keepdims=True)


def _modulate(x, vec, name):
    rows, d = x.shape
    tr = _tile(rows, 256, 8)

    def body(x_ref, v_ref, a_ref):
        n, _ = _rms(x_ref[...])
        a_ref[...] = ((n * v_ref[0:1, :]) * (1.0 + v_ref[2:3, :]) + v_ref[1:2, :]).astype(BF16)

    return pl.pallas_call(
        body, name=name, grid=(rows // tr,), in_specs=[_rspec(tr, d), _cspec(vec.shape)],
        out_specs=_rspec(tr, d), out_shape=jax.ShapeDtypeStruct((rows, d), BF16),
        compiler_params=_params(("parallel",)))(x, vec)


def _resid_modulate(x, y, vec, name):
    rows, d = x.shape
    tr = _tile(rows, 256, 8)

    def body(x_ref, y_ref, v_ref, x1_ref, a_ref):
        x1 = x_ref[...] + v_ref[3:4, :] * y_ref[...]
        x1_ref[...] = x1
        n, _ = _rms(x1)
        a_ref[...] = ((n * v_ref[0:1, :]) * (1.0 + v_ref[2:3, :]) + v_ref[1:2, :]).astype(BF16)

    return pl.pallas_call(
        body, name=name, grid=(rows // tr,), in_specs=[_rspec(tr, d), _rspec(tr, d), _cspec(vec.shape)],
        out_specs=[_rspec(tr, d), _rspec(tr, d)],
        out_shape=[jax.ShapeDtypeStruct((rows, d), F32), jax.ShapeDtypeStruct((rows, d), BF16)],
        compiler_params=_params(("parallel",)))(x, y, vec)


def _loss_head(x1, f, target, vec, name):
    rows, d = x1.shape
    tr = _tile(rows, 256, 8)

    def body(x1_ref, f_ref, t_ref, v_ref, dx2_ref, df_ref, acc_ref):
        @pl.when(pl.program_id(0) == 0)
        def _():
            acc_ref[...] = jnp.zeros_like(acc_ref)

        gate, gain = v_ref[0:1, :], v_ref[1:2, :]
        fv = f_ref[...]
        n, rstd = _rms(x1_ref[...] + gate * fv)
        err = n * gain - t_ref[...]
        dy = err / d
        dx2 = _rms_bwd(dy * gain, n, rstd)
        dx2_ref[...] = dx2
        df_ref[...] = (dx2 * gate).astype(BF16)
        acc_ref[0:1, :] += _colsum(dx2 * fv)
        acc_ref[1:2, :] += _colsum(dy * n)
        acc_ref[2:3, :] += _colsum(err * err)

    return pl.pallas_call(
        body, name=name, grid=(rows // tr,),
        in_specs=[_rspec(tr, d), _rspec(tr, d), _rspec(tr, d), _cspec(vec.shape)],
        out_specs=[_rspec(tr, d), _rspec(tr, d), _cspec((8, d))],
        out_shape=[jax.ShapeDtypeStruct((rows, d), F32), jax.ShapeDtypeStruct((rows, d), BF16),
                   jax.ShapeDtypeStruct((8, d), F32)],
        compiler_params=_params(("arbitrary",)))(x1, f, target, vec)


def _modulate_bwd(da, x, vec, name, da_off=0, dx_in=None, y=None):
    rows, d = x.shape
    tr = _tile(rows, 256, 8)
    off = da_off // tr
    has_dx, has_y = dx_in is not None, y is not None

    def body(*refs):
        da_ref, x_ref, v_ref = refs[0], refs[1], refs[2]
        pos = 3
        dxin_ref = refs[pos] if has_dx else None
        pos += has_dx
        y_ref = refs[pos] if has_y else None
        pos += has_y
        dx_ref = refs[pos]
        dy_ref = refs[pos + 1] if has_y else None
        acc_ref = refs[-1]

        @pl.when(pl.program_id(0) == 0)
        def _():
            acc_ref[...] = jnp.zeros_like(acc_ref)

        gain, scale1 = v_ref[0:1, :], 1.0 + v_ref[2:3, :]
        dav = da_ref[...]
        n, rstd = _rms(x_ref[...])
        dx = _rms_bwd(dav * gain * scale1, n, rstd)
        if has_dx:
            dx = dx + dxin_ref[...]
        dx_ref[...] = dx
        acc_ref[0:1, :] += _colsum(dav)
        acc_ref[1:2, :] += _colsum(dav * (n * gain))
        acc_ref[2:3, :] += _colsum(dav * n * scale1)
        if has_y:
            acc_ref[3:4, :] += _colsum(dx * y_ref[...])
            dy_ref[...] = (dx * v_ref[3:4, :]).astype(BF16)

    ins = [da, x, vec] + ([dx_in] if has_dx else []) + ([y] if has_y else [])
    in_specs = [_rspec(tr, d, off=off), _rspec(tr, d), _cspec(vec.shape)] + [_rspec(tr, d)] * (has_dx + has_y)
    out_specs = [_rspec(tr, d)] + ([_rspec(tr, d)] if has_y else []) + [_cspec((8, d))]
    out_shape = ([jax.ShapeDtypeStruct((rows, d), F32)] + ([jax.ShapeDtypeStruct((rows, d), BF16)] if has_y else [])
                 + [jax.ShapeDtypeStruct((8, d), F32)])
    return pl.pallas_call(
        body, name=name, grid=(rows // tr,), in_specs=in_specs, out_specs=out_specs, out_shape=out_shape,
        compiler_params=_params(("arbitrary",)))(*ins)


def _conv_terms(cc, cx, w_ref, tr):
    t = lax.broadcasted_iota(jnp.int32, (tr, 1), 0) % GRID_W
    first, last = t == 0, t == GRID_W - 1
    u = cc * cx
    prev = jnp.where(first, 0.0, pltpu.roll(u, 1, 0))
    nxt = jnp.where(last, 0.0, pltpu.roll(u, tr - 1, 0))
    y = w_ref[0:1, :] * prev + w_ref[1:2, :] * u + w_ref[2:3, :] * nxt
    return u, prev, nxt, y, first, last


def _conv_fwd(z, conv_w, rows, name):
    d = conv_w.shape[1]
    tr = _tile(rows, 256, GRID_W)

    def body(cb_ref, cc_ref, cx_ref, w_ref, out_ref):
        y = _conv_terms(cc_ref[...], cx_ref[...], w_ref, tr)[3]
        out_ref[...] = (cb_ref[...] * y).astype(BF16)

    return pl.pallas_call(
        body, name=name, grid=(rows // tr,),
        in_specs=[_rspec(tr, d, CB), _rspec(tr, d, CC), _rspec(tr, d, CX), _cspec(conv_w.shape)],
        out_specs=_rspec(tr, d), out_shape=jax.ShapeDtypeStruct((rows, d), BF16),
        compiler_params=_params(("parallel",)))(z, z, z, conv_w)


def _conv_bwd(dpc, z, conv_w, name):
    rows, d = dpc.shape
    tr = _tile(rows, 256, GRID_W)

    def body(dpc_ref, cb_ref, cc_ref, cx_ref, w_ref, dz_ref, acc_ref):
        @pl.when(pl.program_id(0) == 0)
        def _():
            acc_ref[...] = jnp.zeros_like(acc_ref)

        cc, cx, dpcv = cc_ref[...], cx_ref[...], dpc_ref[...]
        u, prev, nxt, y, first, last = _conv_terms(cc, cx, w_ref, tr)
        dy = dpcv * cb_ref[...]
        dy_next = jnp.where(last, 0.0, pltpu.roll(dy, tr - 1, 0))
        dy_prev = jnp.where(first, 0.0, pltpu.roll(dy, 1, 0))
        du = w_ref[0:1, :] * dy_next + w_ref[1:2, :] * dy + w_ref[2:3, :] * dy_prev
        dz_ref[:, 0:d] = (dpcv * y).astype(BF16)
        dz_ref[:, d:2 * d] = (du * cx).astype(BF16)
        dz_ref[:, 2 * d:3 * d] = (du * cc).astype(BF16)
        acc_ref[0:1, :] += _colsum(dy * prev)
        acc_ref[1:2, :] += _colsum(dy * u)
        acc_ref[2:3, :] += _colsum(dy * nxt)

    return pl.pallas_call(
        body, name=name, grid=(rows // tr,),
        in_specs=[_rspec(tr, d), _rspec(tr, d, CB), _rspec(tr, d, CC), _rspec(tr, d, CX), _cspec(conv_w.shape)],
        out_specs=[_rspec(tr, 3 * d), _cspec((8, d))],
        out_shape=[jax.ShapeDtypeStruct((rows, 3 * d), BF16), jax.ShapeDtypeStruct((8, d), F32)],
        compiler_params=_params(("arbitrary",)))(dpc, z, z, z, conv_w)


def _rotary_fwd(z, cos, sin, n_lat, name):
    rows = z.shape[0]
    d = z.shape[1] // N_IN
    dk = d // HEADS
    half = dk // 2
    tr = _tile(n_lat, 256, 8)
    tr = _tile(rows - n_lat, tr, 8)
    lat_blocks, all_blocks = n_lat // tr, rows // tr
    ctx_blocks = all_blocks - lat_blocks
    scan_rows = rows + ctx_blocks * tr

    def z_block(g):
        return jnp.where(g < all_blocks, g, g - ctx_blocks)

    def in_spec(w, cb=0):
        return pl.BlockSpec((tr, w), lambda g: (z_block(g), cb))

    def out_spec(w):
        return pl.BlockSpec((tr, w), lambda g: (jnp.where(g < all_blocks, g + ctx_blocks, g - all_blocks), 0))

    def body(q_ref, k_ref, v0_ref, v1_ref, cos_ref, sin_ref, qo_ref, ko_ref, vo_ref):
        cs, sn = cos_ref[...], sin_ref[...]
        keep = jnp.where(z_block(pl.program_id(0)) < lat_blocks, 1.0, 0.0)
        for src, dst, scale in ((q_ref, qo_ref, keep), (k_ref, ko_ref, dk ** -0.5)):
            for h in range(HEADS):
                lo, mid, hi = h * dk, h * dk + half, (h + 1) * dk
                t1, t2 = src[:, lo:mid], src[:, mid:hi]
                dst[:, lo:mid] = ((t1 * cs - t2 * sn) * scale).astype(BF16)
                dst[:, mid:hi] = ((t1 * sn + t2 * cs) * scale).astype(BF16)
        vo_ref[:, 0:d] = v0_ref[...].astype(BF16)
        vo_ref[:, d:2 * d] = v1_ref[...].astype(BF16)

    return pl.pallas_call(
        body, name=name, grid=(all_blocks + ctx_blocks,),
        in_specs=[in_spec(d, CQ), in_spec(d, CK), in_spec(d, CV), in_spec(d, CV + 1), in_spec(half), in_spec(half)],
        out_specs=[out_spec(d), out_spec(d), out_spec(2 * d)],
        out_shape=[jax.ShapeDtypeStruct((scan_rows, d), BF16), jax.ShapeDtypeStruct((scan_rows, d), BF16),
                   jax.ShapeDtypeStruct((scan_rows, 2 * d), BF16)],
        compiler_params=_params(("parallel",)))(z, z, z, z, cos, sin)


def _dz_assemble(dq_f, dq_b, dk_f, dk_b, dv_f, dv_b, cos, sin, dz_conv, dz_g, dz_gc, dz_gr, n_lat, n_ctx, name):
    rows = n_lat + n_ctx
    d = dq_f.shape[1]
    dk = d // HEADS
    half = dk // 2
    tr = _tile(n_ctx, 128, 8)
    lat_blocks, ctx_blocks = n_lat // tr, n_ctx // tr

    def fmap(i):
        return (jnp.where(i < lat_blocks, i + ctx_blocks, i - lat_blocks), 0)

    def bmap(i):
        return (i + ctx_blocks, 0)

    def lmap(i):
        return (jnp.minimum(i, lat_blocks - 1), 0)

    def body(qf_ref, qb_ref, kf_ref, kb_ref, vf_ref, vb_ref, cos_ref, sin_ref, c_ref, g_ref, gc_ref, gr_ref, out_ref):
        cs, sn = cos_ref[...], sin_ref[...]
        is_lat = pl.program_id(0) < lat_blocks
        keep = jnp.where(is_lat, 1.0, 0.0)
        for fa, fb, base, scale in ((qf_ref, qb_ref, CQ * d, keep), (kf_ref, kb_ref, CK * d, dk ** -0.5)):
            for h in range(HEADS):
                lo, mid, hi = h * dk, h * dk + half, (h + 1) * dk
                d1 = fa[:, lo:mid] + fb[:, lo:mid]
                d2 = fa[:, mid:hi] + fb[:, mid:hi]
                out_ref[:, base + lo:base + mid] = ((d1 * cs + d2 * sn) * scale).astype(BF16)
                out_ref[:, base + mid:base + hi] = ((d2 * cs - d1 * sn) * scale).astype(BF16)
        out_ref[:, CV * d:CG * d] = (vf_ref[...] + vb_ref[...]).astype(BF16)
        for src, lo, hi in ((c_ref, CB * d, CQ * d), (g_ref, CG * d, CGC * d), (gc_ref, CGC * d, CGR * d),
                            (gr_ref, CGR * d, N_IN * d)):
            out_ref[:, lo:hi] = jnp.where(is_lat, src[...], jnp.zeros_like(src))

    return pl.pallas_call(
        body, name=name, grid=(rows // tr,),
        in_specs=[pl.BlockSpec((tr, d), fmap), pl.BlockSpec((tr, d), bmap), pl.BlockSpec((tr, d), fmap),
                  pl.BlockSpec((tr, d), bmap), pl.BlockSpec((tr, 2 * d), fmap), pl.BlockSpec((tr, 2 * d), bmap),
                  _rspec(tr, half), _rspec(tr, half), pl.BlockSpec((tr, 3 * d), lmap), pl.BlockSpec((tr, 2 * d), lmap),
                  pl.BlockSpec((tr, d), lmap), pl.BlockSpec((tr, d), lmap)],
        out_specs=_rspec(tr, N_IN * d), out_shape=jax.ShapeDtypeStruct((rows, N_IN * d), BF16),
        compiler_params=_params(("parallel",)))(dq_f, dq_b, dk_f, dk_b, dv_f, dv_b, cos, sin, dz_conv, dz_g, dz_gc, dz_gr)


def _gn_fwd(o_f, o_b, z, rows, row_off, name):
    d = z.shape[1] // N_IN
    dv = 2 * d // HEADS
    tr = _tile(row_off, 128, 8)
    off = row_off // tr

    def body(of_ref, ob_ref, g0_ref, g1_ref, r_ref):
        for h in range(HEADS):
            lo, hi = h * dv, (h + 1) * dv
            g_ref, glo = (g0_ref, lo) if hi <= d else (g1_ref, lo - d)
            o = of_ref[:, lo:hi] + ob_ref[:, lo:hi]
            cen = o - jnp.mean(o, axis=-1, keepdims=True)
            on = cen * lax.rsqrt(jnp.mean(cen * cen, axis=-1, keepdims=True) + EPS)
            g = g_ref[:, glo:glo + dv]
            r_ref[:, lo:hi] = (g * _sig(g) * on).astype(BF16)

    return pl.pallas_call(
        body, name=name, grid=(rows // tr,),
        in_specs=[_rspec(tr, 2 * d, off=off), _rspec(tr, 2 * d, off=off), _rspec(tr, d, CG), _rspec(tr, d, CG + 1)],
        out_specs=_rspec(tr, 2 * d), out_shape=jax.ShapeDtypeStruct((rows, 2 * d), BF16),
        compiler_params=_params(("parallel",)))(o_f, o_b, z, z)


def _gn_bwd(dr, o_f, o_b, z, row_off, name):
    rows = dr.shape[0]
    d = z.shape[1] // N_IN
    dv = 2 * d // HEADS
    tr = _tile(row_off, 128, 8)
    off = row_off // tr

    def body(dr_ref, of_ref, ob_ref, g0_ref, g1_ref, do_ref, dg_ref):
        for h in range(HEADS):
            lo, hi = h * dv, (h + 1) * dv
            g_ref, glo = (g0_ref, lo) if hi <= d else (g1_ref, lo - d)
            o = of_ref[:, lo:hi] + ob_ref[:, lo:hi]
            cen = o - jnp.mean(o, axis=-1, keepdims=True)
            rstd = lax.rsqrt(jnp.mean(cen * cen, axis=-1, keepdims=True) + EPS)
            on = cen * rstd
            g = g_ref[:, glo:glo + dv]
            sg = _sig(g)
            drv = dr_ref[:, lo:hi]
            dg_ref[:, lo:hi] = (drv * on * (sg * (1.0 + g * (1.0 - sg)))).astype(BF16)
            don = drv * (g * sg)
            do = rstd * (don - jnp.mean(don, axis=-1, keepdims=True)
                         - on * jnp.mean(don * on, axis=-1, keepdims=True))
            do_ref[:, lo:hi] = do.astype(BF16)

    return pl.pallas_call(
        body, name=name, grid=(rows // tr,),
        in_specs=[_rspec(tr, 2 * d), _rspec(tr, 2 * d, off=off), _rspec(tr, 2 * d, off=off),
                  _rspec(tr, d, CG), _rspec(tr, d, CG + 1)],
        out_specs=[_rspec(tr, 2 * d), _rspec(tr, 2 * d)],
        out_shape=[jax.ShapeDtypeStruct((rows, 2 * d), BF16), jax.ShapeDtypeStruct((rows, 2 * d), BF16)],
        compiler_params=_params(("parallel",)))(dr, o_f, o_b, z, z)


def _decays(lg, rev):
    row = lax.broadcasted_iota(jnp.int32, (CHUNK, CHUNK), 0)
    col = lax.broadcasted_iota(jnp.int32, (CHUNK, CHUNK), 1)
    rel = ((col - row) if rev else (row - col)).astype(F32)
    mask = jnp.where(rel >= 0, jnp.exp(lg * jnp.maximum(rel, 0.0)), 0.0)
    r = lax.broadcasted_iota(jnp.int32, (CHUNK, 1), 0)
    rr = ((CHUNK - 1 - r) if rev else r).astype(F32)
    chunk_decay = jnp.exp(lg * jnp.full((1, 1), float(CHUNK), F32))
    return (rr, rel), mask, jnp.exp(lg * (rr + 1.0)), jnp.exp(lg * (CHUNK - 1.0 - rr)), chunk_decay


_NT = (((1,), (1,)), ((), ()))
_TN = (((0,), (0,)), ((), ()))


def _dot(a, b, dims=None):
    if dims is None:
        return jnp.dot(a, b, preferred_element_type=F32)
    return lax.dot_general(a, b, dims, preferred_element_type=F32)


def _ret_fwd(q, k, v, lg, rev, name):
    tt, d = q.shape
    dk, dv = d // HEADS, 2 * d // HEADS
    nc = tt // CHUNK

    def cmap(h, i):
        return ((nc - 1 - i) if rev else i, h)

    def body(lg_ref, q_ref, k_ref, v_ref, o_ref, s_ref, state):
        h = pl.program_id(0)

        @pl.when(pl.program_id(1) == 0)
        def _():
            state[...] = jnp.zeros_like(state)

        lgh = lg_ref[h]
        _, mask, qd, kd, cd = _decays(lgh, rev)
        qv, kv, vv = q_ref[...], k_ref[...], v_ref[...]
        st = state[...]
        p = _dot(qv, kv, _NT) * mask
        o_ref[...] = _dot(p.astype(BF16), vv) + _dot((qv * qd).astype(BF16), st.astype(BF16))
        s_ref[...] = st
        state[...] = cd * st + _dot((kv * kd).astype(BF16), vv, _TN)

    return pl.pallas_call(
        body, name=name, grid=(HEADS, nc),
        in_specs=[pl.BlockSpec(memory_space=pltpu.SMEM), pl.BlockSpec((CHUNK, dk), cmap),
                  pl.BlockSpec((CHUNK, dk), cmap), pl.BlockSpec((CHUNK, dv), cmap)],
        out_specs=[pl.BlockSpec((CHUNK, dv), cmap),
                   pl.BlockSpec((None, None, dk, dv), lambda h, i: (cmap(h, i)[0], h, 0, 0))],
        out_shape=[jax.ShapeDtypeStruct((tt, 2 * d), F32), jax.ShapeDtypeStruct((nc, HEADS, dk, dv), F32)],
        scratch_shapes=[pltpu.VMEM((dk, dv), F32)],
        compiler_params=_params(("parallel", "arbitrary")))(lg, q, k, v)


def _ret_bwd(q, k, v, do, states, lg, rev, name):
    tt, d = q.shape
    dk, dv = d // HEADS, 2 * d // HEADS
    nc = tt // CHUNK
    lat_chunks = do.shape[0] // CHUNK
    ctx_chunks = (nc - lat_chunks) // 2

    def cmap(h, i):
        return (i if rev else (nc - 1 - i), h)

    def do_map(h, i):
        return (jnp.clip(cmap(h, i)[0] - ctx_chunks, 0, lat_chunks - 1), h)

    def body(lg_ref, q_ref, k_ref, v_ref, do_ref, s_ref, dq_ref, dk_ref, dv_ref, dlg_ref, dstate):
        h = pl.program_id(0)
        chunk = cmap(h, pl.program_id(1))[0]
        is_lat = (chunk >= ctx_chunks) & (chunk < ctx_chunks + lat_chunks)

        @pl.when(pl.program_id(1) == 0)
        def _():
            dstate[...] = jnp.zeros_like(dstate)
            dlg_ref[...] = jnp.zeros_like(dlg_ref)

        lgh = lg_ref[h]
        (rr, rel), mask, qd, kd, cd = _decays(lgh, rev)
        qv, kv, vv = q_ref[...], k_ref[...], v_ref[...]
        dov = jnp.where(is_lat, do_ref[...], jnp.zeros_like(do_ref))
        st, dst = s_ref[...], dstate[...]
        st_b, dst_b = st.astype(BF16), dst.astype(BF16)
        p = _dot(qv, kv, _NT) * mask
        dp = _dot(dov, vv, _NT)
        da = (dp * mask).astype(BF16)
        dq_state = _dot(dov, st_b, _NT) * qd
        dk_state = _dot(vv, dst_b, _NT) * kd
        dq_ref[...] = _dot(da, kv) + dq_state
        dk_ref[...] = _dot(da, qv, _TN) + dk_state
        dv_ref[...] = _dot(p.astype(BF16), dov, _TN) + _dot((kv * kd).astype(BF16), dst_b)
        dnew = cd * dst + _dot((qv * qd).astype(BF16), dov, _TN)
        dstate[...] = dnew
        through = rr * (jnp.sum(qv.astype(F32) * dq_state, axis=1, keepdims=True)
                        - jnp.sum(kv.astype(F32) * dk_state, axis=1, keepdims=True))
        dlg_ref[...] += _total(rel * p * dp) + _total(through) + CHUNK * _total(st * dnew)

    return pl.pallas_call(
        body, name=name, grid=(HEADS, nc),
        in_specs=[pl.BlockSpec(memory_space=pltpu.SMEM), pl.BlockSpec((CHUNK, dk), cmap),
                  pl.BlockSpec((CHUNK, dk), cmap), pl.BlockSpec((CHUNK, dv), cmap), pl.BlockSpec((CHUNK, dv), do_map),
                  pl.BlockSpec((None, None, dk, dv), lambda h, i: (cmap(h, i)[0], h, 0, 0))],
        out_specs=[pl.BlockSpec((CHUNK, dk), cmap), pl.BlockSpec((CHUNK, dk), cmap), pl.BlockSpec((CHUNK, dv), cmap),
                   pl.BlockSpec((None, 8, 128), lambda h, i: (h, 0, 0))],
        out_shape=[jax.ShapeDtypeStruct((tt, d), F32), jax.ShapeDtypeStruct((tt, d), F32),
                   jax.ShapeDtypeStruct((tt, 2 * d), F32), jax.ShapeDtypeStruct((HEADS, 8, 128), F32)],
        scratch_shapes=[pltpu.VMEM((dk, dv), F32)],
        compiler_params=_params(("parallel", "arbitrary")))(lg, q, k, v, do, states)


def _silu(v):
    return v * _sig(v)


def _mod_proj(cs, w_mod, b_loc, dec, name):
    nb = w_mod.shape[1]

    def body(cs_ref, w_ref, b_ref, dec_ref, out_ref, lg_ref):
        out_ref[...] = jnp.dot(_silu(cs_ref[...]), w_ref[...], preferred_element_type=F32, precision=HIGHEST) + b_ref[...]
        a = dec_ref[...]
        lg_ref[...] = jnp.minimum(a, 0.0) - jnp.log1p(jnp.exp(-jnp.abs(a)))

    return pl.pallas_call(
        body, name=name,
        out_shape=[jax.ShapeDtypeStruct((16, nb), F32), jax.ShapeDtypeStruct(dec.shape, F32)],
        compiler_params=_params())(cs, w_mod, b_loc, dec)


def _mod_grad(cs, dm, w_mod, name):
    d, nb = w_mod.shape

    def body(cs_ref, dm_ref, w_ref, gw_ref, part_ref):
        dmv = dm_ref[...]
        gw_ref[...] = lax.dot_general(_silu(cs_ref[...]), dmv, _TN, preferred_element_type=F32, precision=HIGHEST)
        part_ref[...] = lax.dot_general(dmv, w_ref[...], _NT, preferred_element_type=F32, precision=HIGHEST)

    return pl.pallas_call(
        body, name=name,
        out_shape=[jax.ShapeDtypeStruct((d, nb), F32), jax.ShapeDtypeStruct((16, d), F32)],
        compiler_params=_params())(cs, dm, w_mod)


def _reduce_small(gathered, dec, n_feat, name):
    _, rows, d = gathered.shape

    def body(g_ref, dec_ref, red_ref, misc_ref):
        total = g_ref[0]
        for i in range(1, N_DEV):
            total = total + g_ref[i]
        red_ref[...] = total
        misc_ref[...] = jnp.zeros_like(misc_ref)
        misc_ref[0:1, :] = jnp.zeros((1, 128), F32) + (0.5 / n_feat) * _total(total[16:17, :])
        misc_ref[1:3, :] = total[14:16, 0:128] * _sig(-dec_ref[0:2, :])

    return pl.pallas_call(
        body, name=name,
        out_shape=[jax.ShapeDtypeStruct((rows, d), F32), jax.ShapeDtypeStruct((8, 128), F32)],
        compiler_params=_params())(gathered, dec)


def _c_ctx_grad(parts, c_ctx, name):
    d = c_ctx.shape[1]

    def body(p_ref, c_ref, out_ref):
        total = p_ref[0]
        for i in range(1, N_DEV):
            total = total + p_ref[i]
        cv = c_ref[...]
        sg = _sig(cv)
        out_ref[...] = total[8:9, :] * (sg * (1.0 + cv * (1.0 - sg)))

    return pl.pallas_call(body, name=name, out_shape=jax.ShapeDtypeStruct((1, d), F32),
                          compiler_params=_params())(parts, c_ctx)


def _adamw(parts, w, m, v, name, own=None):
    n_parts, rows, cols = parts.shape
    row_bytes = cols * (parts.dtype.itemsize * (n_parts + 1) + 7 * 4)
    tr = _tile(rows, max(16, (4 * 1024 * 1024) // row_bytes), 16 if rows % 16 == 0 else 8)
    blocks = rows // tr

    def body(*refs):
        p_ref, w_ref, m_ref, v_ref, g_ref, d_ref, mo_ref, vo_ref = refs[-8:]
        if own is None:
            g = p_ref[0].astype(F32)
        else:
            g = refs[1][...].astype(F32) + p_ref[0].astype(F32)
        for i in range(1, n_parts):
            g = g + p_ref[i].astype(F32)
        m2 = ADAM_B1 * m_ref[...] + (1.0 - ADAM_B1) * g
        v2 = ADAM_B2 * v_ref[...] + (1.0 - ADAM_B2) * jnp.square(g)
        m_hat = m2 / (1.0 - ADAM_B1 ** ADAM_STEP)
        v_hat = v2 / (1.0 - ADAM_B2 ** ADAM_STEP)
        g_ref[...] = g
        d_ref[...] = -ADAM_LR * (m_hat / (jnp.sqrt(v_hat) + ADAM_EPS) + ADAM_WD * w_ref[...])
        mo_ref[...] = m2
        vo_ref[...] = v2

    out_shape = [jax.ShapeDtypeStruct((rows, cols), F32)] * 4
    if own is None:
        spec = _rspec(tr, cols)
        return pl.pallas_call(
            body, name=name, grid=(blocks,),
            in_specs=[pl.BlockSpec((n_parts, tr, cols), lambda i: (0, i, 0)), spec, spec, spec],
            out_specs=[spec] * 4, out_shape=out_shape, compiler_params=_params(("parallel",)))(parts, w, m, v)
    g_full, axis, me = own

    def own_map(i, me_ref):
        return (me_ref[0] * blocks + i, 0) if axis == 0 else (i, me_ref[0])

    spec = pl.BlockSpec((tr, cols), lambda i, me_ref: (i, 0))
    return pl.pallas_call(
        body, name=name, out_shape=out_shape,
        grid_spec=pltpu.PrefetchScalarGridSpec(
            num_scalar_prefetch=1, grid=(blocks,),
            in_specs=[pl.BlockSpec((tr, cols), own_map),
                      pl.BlockSpec((n_parts, tr, cols), lambda i, me_ref: (0, i, 0)), spec, spec, spec],
            out_specs=[spec] * 4),
        compiler_params=_params(("parallel",)))(me.reshape(1), g_full, parts, w, m, v)


def _rope_tables(pos, dk):
    half = dk // 2
    inv_freq = 1.0 / (ROPE_BASE ** jnp.linspace(0.0, 1.0, half, dtype=F32))
    ang = pos[:, None] * inv_freq[None, :]
    return jnp.cos(ang), jnp.sin(ang)


def _pad_lanes(v, width):
    return jnp.pad(v, ((0, 0), (0, width - v.shape[1])))


def kernel(x, c, ctx, c_ctx, w_mod, b_mod, norm1_g, w_in, conv_w, w_conv_out, ret_decay_fwd, ret_decay_bwd, w_ret_out, w_o, norm2_g, w_ff1, w_ff2, final_g, loss_target, m_c_ctx, m_w_mod, m_b_mod, m_norm1_g, m_w_in, m_conv_w, m_w_conv_out, m_ret_decay_fwd, m_ret_decay_bwd, m_w_ret_out, m_w_o, m_norm2_g, m_w_ff1, m_w_ff2, m_final_g, v_c_ctx, v_w_mod, v_b_mod, v_norm1_g, v_w_in, v_conv_w, v_w_conv_out, v_ret_decay_fwd, v_ret_decay_bwd, v_w_ret_out, v_w_o, v_norm2_g, v_w_ff1, v_w_ff2, v_final_g):
    n_lat, d = x.shape[1], x.shape[2]
    n_ctx = ctx.shape[1]
    assert n_ctx % CHUNK == 0 and n_lat % CHUNK == 0 and n_lat % GRID_W == 0
    dk = d // HEADS
    nb = w_mod.shape[2]
    me = 4 * lax.axis_index("x") + 2 * lax.axis_index("y") + lax.axis_index("c")
    xl, ctxl, target = x[0], ctx[0], loss_target[0]

    w_in_f = _allgather(w_in[0].astype(BF16), 1, "ag_w_in")
    conv_w_f = _small_allgather(conv_w[0], "ag_conv_w").transpose(1, 0, 2).reshape(3, d)

    c_all = _small_allgather(c, "ag_c").reshape(N_DEV, d)
    cs = jnp.concatenate([c_all, c_ctx[None], jnp.zeros((7, d), F32)], axis=0)
    dec = jnp.pad(jnp.concatenate([ret_decay_fwd, ret_decay_bwd], axis=0), ((0, 6), (0, 128 - HEADS)))
    b_loc = lax.dynamic_slice(b_mod, (0, me * nb), (1, nb))
    modp, lgs = _mod_proj(cs, w_mod[0], b_loc, dec, "mod_proj")
    modp_all = _small_allgather(modp, "ag_mod")
    mod_l = lax.dynamic_index_in_dim(modp_all, me, axis=1, keepdims=False).reshape(N_MOD, d)
    mod_c = modp_all[:, 8, :].reshape(N_MOD, d)
    lg_f, lg_b = lgs[0, :HEADS], lgs[1, :HEADS]
    zero_row = jnp.zeros((1, d), F32)
    vec1_l = jnp.concatenate([norm1_g, mod_l[0:1], mod_l[1:2], zero_row], axis=0)
    vec1_c = jnp.concatenate([norm1_g, mod_c[0:1], mod_c[1:2], zero_row], axis=0)
    vec2 = jnp.concatenate([norm2_g, mod_l[3:4], mod_l[4:5], mod_l[2:3]], axis=0)
    vec3 = jnp.concatenate([mod_l[5:6], final_g[None]], axis=0)

    later, after = {}, (w_in_f, conv_w_f, modp_all)
    for wname, w, axis in (("w_conv_out", w_conv_out, 0), ("w_ret_out", w_ret_out, 0), ("w_o", w_o, 0),
                           ("w_ff1", w_ff1, 1), ("w_ff2", w_ff2, 0)):
        later[wname], token = _exchange_start(_place_shard(w[0], axis, me, "place_" + wname), axis, True,
                                              "ag_" + wname + "_start", after=after)
        after = (token,)

    a_all = jnp.concatenate([_modulate(xl, vec1_l, "modulate1"), _modulate(ctxl, vec1_c, "modulate1_ctx")], axis=0)
    z = _mm(a_all, w_in_f, "nn", [F32], "proj_in", tn=2048, dep=token)
    pc = _conv_fwd(z, conv_w_f, n_lat, "conv_fwd")
    pos = jnp.concatenate([n_ctx + jnp.arange(n_lat, dtype=F32), jnp.arange(n_ctx, dtype=F32)])
    cos, sin = _rope_tables(pos, dk)
    q_s, k_s, v_s = _rotary_fwd(z, cos, sin, n_lat, "rotary_fwd")
    o_f, st_f = _ret_fwd(q_s, k_s, v_s, lg_f, False, "ret_fwd_f")
    o_b, st_b = _ret_fwd(q_s, k_s, v_s, lg_b, True, "ret_fwd_b")
    r = _gn_fwd(o_f, o_b, z, n_lat, n_ctx, "gn_fwd")
    w_conv_out_f = _exchange_wait(later["w_conv_out"], r, "ag_w_conv_out_wait")
    w_ret_out_f = _exchange_wait(later["w_ret_out"], r, "ag_w_ret_out_wait")
    w_o_f = _exchange_wait(later["w_o"], r, "ag_w_o_wait")
    y_conv = _mm(pc, w_conv_out_f, "nn", [F32], "proj_conv_out", tn=2048)
    tn_d = _tile(d, 1024, 128)
    gate_offs = (CGC * d // tn_d, CGR * d // tn_d)

    def merge(acc, yc, gc, gr):
        return acc, _sig(gc) * yc + _sig(gr) * acc

    y_ret, mg = _mm(r, w_ret_out_f, "nn", [F32, BF16], "proj_ret_out", tm=512, tn=tn_d,
                    extras=[(y_conv, 0), (z, gate_offs[0]), (z, gate_offs[1])], epilogue=merge)
    y_l = _mm(mg, w_o_f, "nn", [F32], "proj_o", tn=2048)
    x1, a2 = _resid_modulate(xl, y_l, vec2, "resid_modulate2")

    def sqrelu(acc):
        return acc, jnp.square(jnp.maximum(acc, 0.0))

    w_ff1_f = _exchange_wait(later["w_ff1"], a2, "ag_w_ff1_wait")
    hff, s = _mm(a2, w_ff1_f, "nn", [F32, BF16], "ff1", epilogue=sqrelu)
    w_ff2_f = _exchange_wait(later["w_ff2"], s, "ag_w_ff2_wait")
    f = _mm(s, w_ff2_f, "nn", [F32], "ff2")
    dx2, df, acc3 = _loss_head(x1, f, target, vec3, "loss_head")

    def d_sqrelu(acc, h):
        return (acc * (2.0 * jnp.maximum(h, 0.0)),)

    dh = _mm(df, w_ff2_f, "nt", [BF16], "ff2_dx", extras=[(hff, 0)], epilogue=d_sqrelu)
    sent = {}
    sent["w_ff2"], token = _exchange_start(_mm(s, df, "tn", [BF16], "ff2_dw"), 0, False, "rs_w_ff2_start")
    da2 = _mm(dh, w_ff1_f, "nt", [F32], "ff1_dx", dep=token)
    sent["w_ff1"], token = _exchange_start(_mm(a2, dh, "tn", [BF16], "ff1_dw"), 1, False, "rs_w_ff1_start")
    dx1, dyl, acc2 = _modulate_bwd(da2, x1, vec2, "modulate2_bwd", dx_in=dx2, y=y_l)

    def d_merge(acc, yc, yr, gc, gr):
        sc, sr = _sig(gc), _sig(gr)
        return acc * sc, acc * sr, acc * yc * (sc * (1.0 - sc)), acc * yr * (sr * (1.0 - sr))

    dyc, dyr, dgc, dgr = _mm(dyl, w_o_f, "nt", [BF16] * 4, "proj_o_dx", tm=512, tn=tn_d,
                             extras=[(y_conv, 0), (y_ret, 0), (z, gate_offs[0]), (z, gate_offs[1])], epilogue=d_merge,
                             dep=token)
    sent["w_o"], token = _exchange_start(_mm(mg, dyl, "tn", [BF16], "proj_o_dw"), 0, False, "rs_w_o_start")
    dpc = _mm(dyc, w_conv_out_f, "nt", [F32], "proj_conv_out_dx", tn=2048, dep=token)
    sent["w_conv_out"], token = _exchange_start(_mm(pc, dyc, "tn", [BF16], "proj_conv_out_dw"), 0, False,
                                                "rs_w_conv_out_start")
    dz_conv, acc_conv = _conv_bwd(dpc, z, conv_w_f, "conv_bwd")
    dr = _mm(dyr, w_ret_out_f, "nt", [F32], "proj_ret_out_dx", dep=token)
    sent["w_ret_out"], token = _exchange_start(_mm(r, dyr, "tn", [BF16], "proj_ret_out_dw"), 0, False,
                                               "rs_w_ret_out_start")
    do, dz_g = _gn_bwd(dr, o_f, o_b, z, n_ctx, "gn_bwd")
    dq_f, dk_f, dv_f, dlg_f = _ret_bwd(q_s, k_s, v_s, do, st_f, lg_f, False, "ret_bwd_f")
    dq_b, dk_b, dv_b, dlg_b = _ret_bwd(q_s, k_s, v_s, do, st_b, lg_b, True, "ret_bwd_b")
    dz = _dz_assemble(dq_f, dq_b, dk_f, dk_b, dv_f, dv_b, cos, sin, dz_conv, dz_g, dgc, dgr, n_lat, n_ctx,
                      "dz_assemble")
    g_in = _mm(a_all, dz, "tn", [BF16], "proj_in_dw", tn=2048, tk=(n_lat + n_ctx) // 4, dep=token)
    sent["w_in"], token = _exchange_start(g_in, 1, False, "rs_w_in_start")
    da_all = _mm(dz, w_in_f, "nt", [F32], "proj_in_dx", tn=2048, dep=token)
    grad_x, acc1 = _modulate_bwd(da_all, xl, vec1_l, "modulate1_bwd", dx_in=dx1)
    _, acc1c = _modulate_bwd(da_all, ctxl, vec1_c, "modulate1_ctx_bwd", da_off=n_lat)

    lane_pad = functools.partial(_pad_lanes, width=d)
    packet = jnp.concatenate([
        acc1[2:3] + acc1c[2:3], acc2[2:3], acc3[1:2],
        acc1[0:1], acc1[1:2], acc2[3:4], acc2[0:1], acc2[1:2], acc3[0:1],
        acc1c[0:1], acc1c[1:2],
        acc_conv[0:3],
        lane_pad(dlg_f[:, 0, 0][None]), lane_pad(dlg_b[:, 0, 0][None]),
        acc3[2:3],
        jnp.zeros((7, d), F32)], axis=0)
    gathered = _small_allgather(packet, "ag_small")
    red, misc = _reduce_small(gathered, dec, d, "reduce_small")
    dmod_ctx = jnp.concatenate([red[9], red[10], jnp.zeros((4 * d,), F32)])
    dmod_all = jnp.concatenate([gathered[:, 3:9, :].reshape(N_DEV, N_MOD * d), dmod_ctx[None]], axis=0)
    dm = jnp.pad(lax.dynamic_slice(dmod_all, (0, me * nb), (N_DEV + 1, nb)), ((0, 7), (0, 0)))
    g_mod, c_part = _mod_grad(cs, dm, w_mod[0], "mod_grad")
    g_c_ctx = _c_ctx_grad(_small_allgather(c_part, "ag_c_ctx"), c_ctx[None], "c_ctx_grad")
    g_b_mod = red[3:9].reshape(1, N_MOD * d) + dmod_ctx[None]
    g_conv_w = lax.dynamic_slice(red[11:14], (0, me * (d // N_DEV)), (3, d // N_DEV))

    res = {"w_mod": _adamw(g_mod[None], w_mod[0], m_w_mod[0], v_w_mod[0], "adamw_w_mod")}
    after = res["w_mod"][0]
    for wname, axis, w, m, v in (("w_ff2", 0, w_ff2, m_w_ff2, v_w_ff2), ("w_ff1", 1, w_ff1, m_w_ff1, v_w_ff1),
                                 ("w_o", 0, w_o, m_w_o, v_w_o), ("w_conv_out", 0, w_conv_out, m_w_conv_out, v_w_conv_out),
                                 ("w_ret_out", 0, w_ret_out, m_w_ret_out, v_w_ret_out),
                                 ("w_in", 1, w_in, m_w_in, v_w_in)):
        g_full, parts = _exchange_wait(sent[wname], after, "rs_" + wname + "_wait")
        res[wname] = _adamw(parts, w[0], m[0], v[0], "adamw_" + wname, own=(g_full, axis, me))
        after = res[wname][0]
    res = {k: tuple(t[None] for t in val) for k, val in res.items()}

    small = [("c_ctx", g_c_ctx, c_ctx, m_c_ctx, v_c_ctx), ("b_mod", g_b_mod, b_mod, m_b_mod, v_b_mod),
             ("norm1_g", red[0:1], norm1_g, m_norm1_g, v_norm1_g), ("conv_w", g_conv_w, conv_w, m_conv_w, v_conv_w),
             ("ret_decay_fwd", misc[1:2, :HEADS], ret_decay_fwd, m_ret_decay_fwd, v_ret_decay_fwd),
             ("ret_decay_bwd", misc[2:3, :HEADS], ret_decay_bwd, m_ret_decay_bwd, v_ret_decay_bwd),
             ("norm2_g", red[1:2], norm2_g, m_norm2_g, v_norm2_g), ("final_g", red[2:3], final_g, m_final_g, v_final_g)]

    def flat(t):
        t = t.reshape(-1)
        return jnp.pad(t, (0, (-t.shape[0]) % 1024))

    packed = [jnp.concatenate([flat(item[j]) for item in small]).reshape(-1, 128) for j in range(1, 5)]
    outs = _adamw(packed[0][None], packed[1], packed[2], packed[3], "adamw_small")
    start = 0
    for name, _, w, _, _ in small:
        size = w.size
        res[name] = tuple(o.reshape(-1)[start:start + size].reshape(w.shape) for o in outs)
        start += size + (-size) % 1024

    order = ["c_ctx", "w_mod", "b_mod", "norm1_g", "w_in", "conv_w", "w_conv_out", "ret_decay_fwd", "ret_decay_bwd",
             "w_ret_out", "w_o", "norm2_g", "w_ff1", "w_ff2", "final_g"]
    loss = misc[0, 0]
    return (loss, grad_x[None], *[res[n][0] for n in order], *[res[n][1] for n in order],
            *[res[n][2] for n in order], *[res[n][3] for n in order])
```

```python
import functools

import jax
import jax.numpy as jnp
from jax import lax
from jax.experimental import pallas as pl
from jax.experimental.pallas import tpu as pltpu

F32 = jnp.float32
BF16 = jnp.bfloat16
MESH = pl.DeviceIdType.MESH

N_DEV = 8
HEADS = 8
N_MOD = 6
N_IN = 11
GRID_W = 64
CHUNK = 256
ROPE_BASE = 10000.0
EPS = 1e-6
ADAM_LR, ADAM_B1, ADAM_B2, ADAM_EPS, ADAM_WD, ADAM_STEP = 0.001, 0.9, 0.999, 1e-08, 0.01, 10
VMEM_LIMIT = 56 * 1024 * 1024
HIGHEST = lax.Precision.HIGHEST
CB, CC, CX, CQ, CK, CV, CG, CGC, CGR = 0, 1, 2, 3, 4, 5, 7, 9, 10


def _tile(n, target, mult):
    t = (min(target, n) // mult) * mult
    while t >= mult:
        if n % t == 0:
            return t
        t -= mult
    return n


def _params(sem=None):
    return pltpu.CompilerParams(dimension_semantics=sem, vmem_limit_bytes=VMEM_LIMIT)


def _sig(v):
    return 1.0 / (1.0 + jnp.exp(-v))


def _coords():
    return lax.axis_index("x"), lax.axis_index("y"), lax.axis_index("c")


def _flip(p, m):
    return tuple(1 - v if (m >> s) & 1 else v for v, s in zip(p, (2, 1, 0)))


def _index(p):
    return 4 * p[0] + 2 * p[1] + p[2]


def _small_allgather(x, name):
    r, n = x.shape

    def body(x_ref, out_ref, send_sems, recv_sems):
        me = _coords()
        out_ref[pl.ds(_index(me), 1)] = x_ref[...][None]
        sent = []
        for m in range(1, N_DEV):
            cp = pltpu.make_async_remote_copy(
                src_ref=x_ref, dst_ref=out_ref.at[_index(me)], send_sem=send_sems.at[m - 1],
                recv_sem=recv_sems.at[m - 1], device_id=_flip(me, m), device_id_type=MESH)
            cp.start()
            sent.append(cp)
        for m in range(1, N_DEV):
            pltpu.make_async_remote_copy(
                src_ref=x_ref, dst_ref=out_ref.at[_index(_flip(me, m))], send_sem=send_sems.at[m - 1],
                recv_sem=recv_sems.at[m - 1], device_id=_flip(me, m), device_id_type=MESH).wait_recv()
        for cp in sent:
            cp.wait_send()

    return pl.pallas_call(
        body, name=name, out_shape=jax.ShapeDtypeStruct((N_DEV, r, n), x.dtype),
        in_specs=[pl.BlockSpec(memory_space=pltpu.VMEM)], out_specs=pl.BlockSpec(memory_space=pltpu.VMEM),
        scratch_shapes=[pltpu.SemaphoreType.DMA((N_DEV - 1,)), pltpu.SemaphoreType.DMA((N_DEV - 1,))],
    )(x)


def _window(ref, j, r, c, axis):
    if axis == 0:
        return ref.at[pl.ds(pl.multiple_of(j * r, 8), r), :]
    return ref.at[:, pl.ds(pl.multiple_of(j * c, 128), c)]


def _allgather(x, axis, name):
    r, c = x.shape
    full = (N_DEV * r, c) if axis == 0 else (r, N_DEV * c)

    def body(x_ref, out_ref, send_sems, recv_sems, local_sem):
        me = _coords()
        sibling = _flip(me, 1)
        chips = [4, 2, 6]

        def copy(k, block, to, src=None):
            dst = _window(out_ref, _index(block), r, c, axis)
            return pltpu.make_async_remote_copy(
                src_ref=dst if src is None else src, dst_ref=dst, send_sem=send_sems.at[k],
                recv_sem=recv_sems.at[k], device_id=to, device_id_type=MESH)

        mine = pltpu.make_async_copy(x_ref, _window(out_ref, _index(me), r, c, axis), local_sem)
        mine.start()
        first = [copy(0, me, sibling, src=x_ref)]
        first += [copy(1 + j, me, _flip(me, m), src=x_ref) for j, m in enumerate(chips)]
        for cp in first:
            cp.start()
        passed = [copy(4 + j, _flip(me, m), sibling) for j, m in enumerate(chips)]
        for j, m in enumerate(chips):
            copy(1 + j, _flip(me, m), me).wait_recv()
            passed[j].start()
        copy(0, sibling, me).wait_recv()
        for j, m in enumerate(chips):
            copy(4 + j, _flip(sibling, m), me).wait_recv()
        for cp in first + passed:
            cp.wait_send()
        mine.wait()

    return pl.pallas_call(
        body, name=name, out_shape=jax.ShapeDtypeStruct(full, x.dtype),
        in_specs=[pl.BlockSpec(memory_space=pl.ANY)], out_specs=pl.BlockSpec(memory_space=pl.ANY),
        scratch_shapes=[pltpu.SemaphoreType.DMA((N_DEV - 1,)), pltpu.SemaphoreType.DMA((N_DEV - 1,)),
                        pltpu.SemaphoreType.DMA(())],
    )(x)


_HBM = pl.BlockSpec(memory_space=pltpu.HBM)
_SEM = pl.BlockSpec(memory_space=pltpu.SEMAPHORE)
_ANY = pl.BlockSpec(memory_space=pl.ANY)
_EFFECT = pltpu.SideEffectType.DATAFLOW_SIDE_EFFECTING


def _place_shard(w, axis, me, name):
    r, c = w.shape
    full = (N_DEV * r, c) if axis == 0 else (r, N_DEV * c)
    tr = _tile(r, max(16, (2 * 1024 * 1024) // (c * 4)), 16)
    blocks = r // tr

    def out_map(i, me_ref):
        return (me_ref[0] * blocks + i, 0) if axis == 0 else (i, me_ref[0])

    def body(me_ref, w_ref, out_ref):
        out_ref[...] = w_ref[...].astype(BF16)

    return pl.pallas_call(
        body, name=name, out_shape=jax.ShapeDtypeStruct(full, BF16),
        grid_spec=pltpu.PrefetchScalarGridSpec(
            num_scalar_prefetch=1, grid=(blocks,), in_specs=[pl.BlockSpec((tr, c), lambda i, me_ref: (i, 0))],
            out_specs=pl.BlockSpec((tr, c), out_map)),
        compiler_params=_params(("parallel",)))(me.reshape(1), w)


def _exchange_start(src, axis, gather, name, after=()):
    r, c = (src.shape[0] // N_DEV, src.shape[1]) if axis == 0 else (src.shape[0], src.shape[1] // N_DEV)
    n_hbm = 1 if gather else 2
    n_after = len(after)

    def body(*refs):
        src_ref, land_ref = refs[0], refs[n_hbm - 1]
        send_sems, recv_sems = refs[n_hbm + n_after:n_hbm + n_after + 2]
        token = refs[-1]
        me = _coords()
        for m in range(1, N_DEV):
            peer = _flip(me, m)
            if gather:
                mine = theirs = _window(land_ref, _index(me), r, c, axis)
            else:
                mine, theirs = _window(src_ref, _index(peer), r, c, axis), land_ref.at[m - 1]
            pltpu.make_async_remote_copy(
                src_ref=mine, dst_ref=theirs, send_sem=send_sems.at[m - 1], recv_sem=recv_sems.at[m - 1],
                device_id=peer, device_id_type=MESH).start()
        token[...] = jnp.zeros_like(token)

    hbm = [pltpu.with_memory_space_constraint(src, pltpu.HBM)]
    if not gather:
        hbm.append(pltpu.with_memory_space_constraint(lax.empty((N_DEV - 1, r, c), src.dtype), pltpu.HBM))
    outs = pl.pallas_call(
        body, name=name,
        out_shape=(pltpu.SemaphoreType.DMA((N_DEV - 1,)), pltpu.SemaphoreType.DMA((N_DEV - 1,)),
                   *[pltpu.HBM(t.shape, t.dtype) for t in hbm], jax.ShapeDtypeStruct((8, 128), F32)),
        in_specs=[_HBM] * n_hbm + [_ANY] * n_after,
        out_specs=(_SEM, _SEM, *[_HBM] * n_hbm, pl.BlockSpec(memory_space=pltpu.VMEM)),
        input_output_aliases={i: 2 + i for i in range(n_hbm)},
        compiler_params=pltpu.CompilerParams(has_side_effects=_EFFECT),
    )(*hbm, *after)
    return (outs[:2], outs[2:2 + n_hbm], (axis, gather, r, c)), outs[-1]


def _exchange_wait(handle, after, name):
    (send_sems, recv_sems), hbm, (axis, gather, r, c) = handle
    n_hbm = len(hbm)

    def body(*refs):
        src_ref, land_ref = refs[0], refs[n_hbm - 1]
        send_sems, recv_sems = refs[n_hbm:n_hbm + 2]
        me = _coords()
        for m in range(1, N_DEV):
            peer = _flip(me, m)
            if gather:
                mine, theirs = _window(land_ref, _index(me), r, c, axis), _window(land_ref, _index(peer), r, c, axis)
            else:
                mine, theirs = _window(src_ref, _index(peer), r, c, axis), land_ref.at[m - 1]
            copy = pltpu.make_async_remote_copy(
                src_ref=mine, dst_ref=theirs, send_sem=send_sems.at[m - 1], recv_sem=recv_sems.at[m - 1],
                device_id=peer, device_id_type=MESH)
            copy.wait_send()
            copy.wait_recv()

    outs = pl.pallas_call(
        body, name=name, out_shape=tuple(pltpu.HBM(t.shape, t.dtype) for t in hbm),
        in_specs=[_HBM] * n_hbm + [_SEM, _SEM, _ANY], out_specs=tuple([_HBM] * n_hbm),
        input_output_aliases={i: i for i in range(n_hbm)},
        compiler_params=pltpu.CompilerParams(has_side_effects=_EFFECT),
    )(*hbm, send_sems, recv_sems, after)
    return outs[0] if gather else tuple(outs)


def _mm(a, b, mode, out_dtypes, name, tm=1024, tn=1024, tk=2048, extras=(), epilogue=None, dep=None):
    if mode == "nn":
        (m, k), n = a.shape, b.shape[1]
    elif mode == "nt":
        (m, k), n = a.shape, b.shape[0]
    else:
        (k, m), n = a.shape, b.shape[1]
    tm, tn = _tile(m, tm, 8), _tile(n, tn, 128)
    tk = _tile(k, tk, 16 if mode == "tn" else 128)
    nk = k // tk
    swap = nk == 1 and (k * n + (n // tn) * m * k) < (m * k + (m // tm) * k * n)

    def ij(p, q):
        return (q, p) if swap else (p, q)

    def spec(shape, fn):
        return pl.BlockSpec(shape, lambda p, q, kk: fn(*ij(p, q), kk))

    a_spec = spec((tk, tm), lambda i, j, kk: (kk, i)) if mode == "tn" else spec((tm, tk), lambda i, j, kk: (i, kk))
    b_spec = spec((tn, tk), lambda i, j, kk: (j, kk)) if mode == "nt" else spec((tk, tn), lambda i, j, kk: (kk, j))
    dims = {"nn": (((1,), (0,)), ((), ())), "nt": (((1,), (1,)), ((), ())), "tn": (((0,), (0,)), ((), ()))}[mode]
    ex_specs = [spec((tm, tn), functools.partial(lambda i, j, kk, off: (i, j + off), off=off)) for _, off in extras]
    deps = [] if dep is None else [dep]
    dep_specs = [pl.BlockSpec(dep.shape, lambda p, q, kk: (0, 0))] if deps else []
    n_ex, n_out = len(extras), len(out_dtypes)
    n_in = 2 + n_ex + len(deps)

    def body(*refs):
        a_ref, b_ref = refs[0], refs[1]
        ex_refs = refs[2:2 + n_ex]
        out_refs = refs[n_in:n_in + n_out]

        def product():
            return lax.dot_general(a_ref[...], b_ref[...], dims, preferred_element_type=F32)

        def finish(res):
            res = epilogue(res, *[e[...] for e in ex_refs]) if epilogue is not None else (res,)
            for o_ref, val in zip(out_refs, res):
                o_ref[...] = val.astype(o_ref.dtype)

        if nk == 1:
            finish(product())
            return
        acc = refs[-1]
        kk = pl.program_id(2)

        @pl.when(kk == 0)
        def _():
            acc[...] = product()

        @pl.when((kk > 0) & (kk < nk - 1))
        def _():
            acc[...] += product()

        @pl.when(kk == nk - 1)
        def _():
            finish(acc[...] + product())

    outs = pl.pallas_call(
        body, name=name, grid=(*ij(m // tm, n // tn), nk),
        in_specs=[a_spec, b_spec] + ex_specs + dep_specs,
        out_specs=[spec((tm, tn), lambda i, j, kk: (i, j)) for _ in out_dtypes],
        out_shape=[jax.ShapeDtypeStruct((m, n), dt) for dt in out_dtypes],
        scratch_shapes=[pltpu.VMEM((tm, tn), F32)] if nk > 1 else [],
        compiler_params=_params(("parallel", "parallel", "arbitrary")),
    )(a, b, *[e for e, _ in extras], *deps)
    return outs if n_out > 1 else outs[0]


def _rspec(tr, w, cb=0, off=0):
    return pl.BlockSpec((tr, w), lambda i: (i + off, cb))


def _cspec(shape):
    return pl.BlockSpec(shape, lambda i: (0,) * len(shape))


def _rms(xf):
    rstd = lax.rsqrt(jnp.mean(xf * xf, axis=-1, keepdims=True) + EPS)
    return xf * rstd, rstd


def _rms_bwd(dn, n, rstd):
    return rstd * (dn - n * jnp.mean(dn * n, axis=-1, keepdims=True))


def _colsum(v):
    return jnp.sum(v, axis=0, keepdims=True)


def _total(v):
    return jnp.sum(jnp.sum(v, axis=1, keepdims=True), axis=0, keepdims=True)


def _modulate(x, vec, name):
    rows, d = x.shape
    tr = _tile(rows, 256, 8)

    def body(x_ref, v_ref, a_ref):
        n, _ = _rms(x_ref[...])
        a_ref[...] = ((n * v_ref[0:1, :]) * (1.0 + v_ref[2:3, :]) + v_ref[1:2, :]).astype(BF16)

    return pl.pallas_call(
        body, name=name, grid=(rows // tr,), in_specs=[_rspec(tr, d), _cspec(vec.shape)],
        out_specs=_rspec(tr, d), out_shape=jax.ShapeDtypeStruct((rows, d), BF16),
        compiler_params=_params(("parallel",)))(x, vec)


def _resid_modulate(x, y, vec, name):
    rows, d = x.shape
    tr = _tile(rows, 256, 8)

    def body(x_ref, y_ref, v_ref, x1_ref, a_ref):
        x1 = x_ref[...] + v_ref[3:4, :] * y_ref[...]
        x1_ref[...] = x1
        n, _ = _rms(x1)
        a_ref[...] = ((n * v_ref[0:1, :]) * (1.0 + v_ref[2:3, :]) + v_ref[1:2, :]).astype(BF16)

    return pl.pallas_call(
        body, name=name, grid=(rows // tr,), in_specs=[_rspec(tr, d), _rspec(tr, d), _cspec(vec.shape)],
        out_specs=[_rspec(tr, d), _rspec(tr, d)],
        out_shape=[jax.ShapeDtypeStruct((rows, d), F32), jax.ShapeDtypeStruct((rows, d), BF16)],
        compiler_params=_params(("parallel",)))(x, y, vec)


def _loss_head(x1, f, target, vec, name):
    rows, d = x1.shape
    tr = _tile(rows, 256, 8)

    def body(x1_ref, f_ref, t_ref, v_ref, dx2_ref, df_ref, acc_ref):
        @pl.when(pl.program_id(0) == 0)
        def _():
            acc_ref[...] = jnp.zeros_like(acc_ref)

        gate, gain = v_ref[0:1, :], v_ref[1:2, :]
        fv = f_ref[...]
        n, rstd = _rms(x1_ref[...] + gate * fv)
        err = n * gain - t_ref[...]
        dy = err / d
        dx2 = _rms_bwd(dy * gain, n, rstd)
        dx2_ref[...] = dx2
        df_ref[...] = (dx2 * gate).astype(BF16)
        acc_ref[0:1, :] += _colsum(dx2 * fv)
        acc_ref[1:2, :] += _colsum(dy * n)
        acc_ref[2:3, :] += _colsum(err * err)

    return pl.pallas_call(
        body, name=name, grid=(rows // tr,),
        in_specs=[_rspec(tr, d), _rspec(tr, d), _rspec(tr, d), _cspec(vec.shape)],
        out_specs=[_rspec(tr, d), _rspec(tr, d), _cspec((8, d))],
        out_shape=[jax.ShapeDtypeStruct((rows, d), F32), jax.ShapeDtypeStruct((rows, d), BF16),
                   jax.ShapeDtypeStruct((8, d), F32)],
        compiler_params=_params(("arbitrary",)))(x1, f, target, vec)


def _modulate_bwd(da, x, vec, name, da_off=0, dx_in=None, y=None):
    rows, d = x.shape
    tr = _tile(rows, 256, 8)
    off = da_off // tr
    has_dx, has_y = dx_in is not None, y is not None

    def body(*refs):
        da_ref, x_ref, v_ref = refs[0], refs[1], refs[2]
        pos = 3
        dxin_ref = refs[pos] if has_dx else None
        pos += has_dx
        y_ref = refs[pos] if has_y else None
        pos += has_y
        dx_ref = refs[pos]
        dy_ref = refs[pos + 1] if has_y else None
        acc_ref = refs[-1]

        @pl.when(pl.program_id(0) == 0)
        def _():
            acc_ref[...] = jnp.zeros_like(acc_ref)

        gain, scale1 = v_ref[0:1, :], 1.0 + v_ref[2:3, :]
        dav = da_ref[...]
        n, rstd = _rms(x_ref[...])
        dx = _rms_bwd(dav * gain * scale1, n, rstd)
        if has_dx:
            dx = dx + dxin_ref[...]
        dx_ref[...] = dx
        acc_ref[0:1, :] += _colsum(dav)
        acc_ref[1:2, :] += _colsum(dav * (n * gain))
        acc_ref[2:3, :] += _colsum(dav * n * scale1)
        if has_y:
            acc_ref[3:4, :] += _colsum(dx * y_ref[...])
            dy_ref[...] = (dx * v_ref[3:4, :]).astype(BF16)

    ins = [da, x, vec] + ([dx_in] if has_dx else []) + ([y] if has_y else [])
    in_specs = [_rspec(tr, d, off=off), _rspec(tr, d), _cspec(vec.shape)] + [_rspec(tr, d)] * (has_dx + has_y)
    out_specs = [_rspec(tr, d)] + ([_rspec(tr, d)] if has_y else []) + [_cspec((8, d))]
    out_shape = ([jax.ShapeDtypeStruct((rows, d), F32)] + ([jax.ShapeDtypeStruct((rows, d), BF16)] if has_y else [])
                 + [jax.ShapeDtypeStruct((8, d), F32)])
    return pl.pallas_call(
        body, name=name, grid=(rows // tr,), in_specs=in_specs, out_specs=out_specs, out_shape=out_shape,
        compiler_params=_params(("arbitrary",)))(*ins)


def _conv_terms(cc, cx, w_ref, tr):
    t = lax.broadcasted_iota(jnp.int32, (tr, 1), 0) % GRID_W
    first, last = t == 0, t == GRID_W - 1
    u = cc * cx
    prev = jnp.where(first, 0.0, pltpu.roll(u, 1, 0))
    nxt = jnp.where(last, 0.0, pltpu.roll(u, tr - 1, 0))
    y = w_ref[0:1, :] * prev + w_ref[1:2, :] * u + w_ref[2:3, :] * nxt
    return u, prev, nxt, y, first, last


def _conv_fwd(z, conv_w, rows, name):
    d = conv_w.shape[1]
    tr = _tile(rows, 256, GRID_W)

    def body(cb_ref, cc_ref, cx_ref, w_ref, out_ref):
        y = _conv_terms(cc_ref[...], cx_ref[...], w_ref, tr)[3]
        out_ref[...] = (cb_ref[...] * y).astype(BF16)

    return pl.pallas_call(
        body, name=name, grid=(rows // tr,),
        in_specs=[_rspec(tr, d, CB), _rspec(tr, d, CC), _rspec(tr, d, CX), _cspec(conv_w.shape)],
        out_specs=_rspec(tr, d), out_shape=jax.ShapeDtypeStruct((rows, d), BF16),
        compiler_params=_params(("parallel",)))(z, z, z, conv_w)


def _conv_bwd(dpc, z, conv_w, name):
    rows, d = dpc.shape
    tr = _tile(rows, 256, GRID_W)

    def body(dpc_ref, cb_ref, cc_ref, cx_ref, w_ref, dz_ref, acc_ref):
        @pl.when(pl.program_id(0) == 0)
        def _():
            acc_ref[...] = jnp.zeros_like(acc_ref)

        cc, cx, dpcv = cc_ref[...], cx_ref[...], dpc_ref[...]
        u, prev, nxt, y, first, last = _conv_terms(cc, cx, w_ref, tr)
        dy = dpcv * cb_ref[...]
        dy_next = jnp.where(last, 0.0, pltpu.roll(dy, tr - 1, 0))
        dy_prev = jnp.where(first, 0.0, pltpu.roll(dy, 1, 0))
        du = w_ref[0:1, :] * dy_next + w_ref[1:2, :] * dy + w_ref[2:3, :] * dy_prev
        dz_ref[:, 0:d] = (dpcv * y).astype(BF16)
        dz_ref[:, d:2 * d] = (du * cx).astype(BF16)
        dz_ref[:, 2 * d:3 * d] = (du * cc).astype(BF16)
        acc_ref[0:1, :] += _colsum(dy * prev)
        acc_ref[1:2, :] += _colsum(dy * u)
        acc_ref[2:3, :] += _colsum(dy * nxt)

    return pl.pallas_call(
        body, name=name, grid=(rows // tr,),
        in_specs=[_rspec(tr, d), _rspec(tr, d, CB), _rspec(tr, d, CC), _rspec(tr, d, CX), _cspec(conv_w.shape)],
        out_specs=[_rspec(tr, 3 * d), _cspec((8, d))],
        out_shape=[jax.ShapeDtypeStruct((rows, 3 * d), BF16), jax.ShapeDtypeStruct((8, d), F32)],
        compiler_params=_params(("arbitrary",)))(dpc, z, z, z, conv_w)


def _rotary_fwd(z, cos, sin, n_lat, name):
    rows = z.shape[0]
    d = z.shape[1] // N_IN
    dk = d // HEADS
    half = dk // 2
    tr = _tile(n_lat, 256, 8)
    tr = _tile(rows - n_lat, tr, 8)
    lat_blocks, all_blocks = n_lat // tr, rows // tr
    ctx_blocks = all_blocks - lat_blocks
    scan_rows = rows + ctx_blocks * tr

    def z_block(g):
        return jnp.where(g < all_blocks, g, g - ctx_blocks)

    def in_spec(w, cb=0):
        return pl.BlockSpec((tr, w), lambda g: (z_block(g), cb))

    def out_spec(w):
        return pl.BlockSpec((tr, w), lambda g: (jnp.where(g < all_blocks, g + ctx_blocks, g - all_blocks), 0))

    def body(q_ref, k_ref, v0_ref, v1_ref, cos_ref, sin_ref, qo_ref, ko_ref, vo_ref):
        cs, sn = cos_ref[...], sin_ref[...]
        keep = jnp.where(z_block(pl.program_id(0)) < lat_blocks, 1.0, 0.0)
        for src, dst, scale in ((q_ref, qo_ref, keep), (k_ref, ko_ref, dk ** -0.5)):
            for h in range(HEADS):
                lo, mid, hi = h * dk, h * dk + half, (h + 1) * dk
                t1, t2 = src[:, lo:mid], src[:, mid:hi]
                dst[:, lo:mid] = ((t1 * cs - t2 * sn) * scale).astype(BF16)
                dst[:, mid:hi] = ((t1 * sn + t2 * cs) * scale).astype(BF16)
        vo_ref[:, 0:d] = v0_ref[...].astype(BF16)
        vo_ref[:, d:2 * d] = v1_ref[...].astype(BF16)

    return pl.pallas_call(
        body, name=name, grid=(all_blocks + ctx_blocks,),
        in_specs=[in_spec(d, CQ), in_spec(d, CK), in_spec(d, CV), in_spec(d, CV + 1), in_spec(half), in_spec(half)],
        out_specs=[out_spec(d), out_spec(d), out_spec(2 * d)],
        out_shape=[jax.ShapeDtypeStruct((scan_rows, d), BF16), jax.ShapeDtypeStruct((scan_rows, d), BF16),
                   jax.ShapeDtypeStruct((scan_rows, 2 * d), BF16)],
        compiler_params=_params(("parallel",)))(z, z, z, z, cos, sin)


def _dz_assemble(dq_f, dq_b, dk_f, dk_b, dv_f, dv_b, cos, sin, dz_conv, dz_g, dz_gc, dz_gr, n_lat, n_ctx, name):
    rows = n_lat + n_ctx
    d = dq_f.shape[1]
    dk = d // HEADS
    half = dk // 2
    tr = _tile(n_ctx, 128, 8)
    lat_blocks, ctx_blocks = n_lat // tr, n_ctx // tr

    def fmap(i):
        return (jnp.where(i < lat_blocks, i + ctx_blocks, i - lat_blocks), 0)

    def bmap(i):
        return (i + ctx_blocks, 0)

    def lmap(i):
        return (jnp.minimum(i, lat_blocks - 1), 0)

    def body(qf_ref, qb_ref, kf_ref, kb_ref, vf_ref, vb_ref, cos_ref, sin_ref, c_ref, g_ref, gc_ref, gr_ref, out_ref):
        cs, sn = cos_ref[...], sin_ref[...]
        is_lat = pl.program_id(0) < lat_blocks
        keep = jnp.where(is_lat, 1.0, 0.0)
        for fa, fb, base, scale in ((qf_ref, qb_ref, CQ * d, keep), (kf_ref, kb_ref, CK * d, dk ** -0.5)):
            for h in range(HEADS):
                lo, mid, hi = h * dk, h * dk + half, (h + 1) * dk
                d1 = fa[:, lo:mid] + fb[:, lo:mid]
                d2 = fa[:, mid:hi] + fb[:, mid:hi]
                out_ref[:, base + lo:base + mid] = ((d1 * cs + d2 * sn) * scale).astype(BF16)
                out_ref[:, base + mid:base + hi] = ((d2 * cs - d1 * sn) * scale).astype(BF16)
        out_ref[:, CV * d:CG * d] = (vf_ref[...] + vb_ref[...]).astype(BF16)
        for src, lo, hi in ((c_ref, CB * d, CQ * d), (g_ref, CG * d, CGC * d), (gc_ref, CGC * d, CGR * d),
                            (gr_ref, CGR * d, N_IN * d)):
            out_ref[:, lo:hi] = jnp.where(is_lat, src[...], jnp.zeros_like(src))

    return pl.pallas_call(
        body, name=name, grid=(rows // tr,),
        in_specs=[pl.BlockSpec((tr, d), fmap), pl.BlockSpec((tr, d), bmap), pl.BlockSpec((tr, d), fmap),
                  pl.BlockSpec((tr, d), bmap), pl.BlockSpec((tr, 2 * d), fmap), pl.BlockSpec((tr, 2 * d), bmap),
                  _rspec(tr, half), _rspec(tr, half), pl.BlockSpec((tr, 3 * d), lmap), pl.BlockSpec((tr, 2 * d), lmap),
                  pl.BlockSpec((tr, d), lmap), pl.BlockSpec((tr, d), lmap)],
        out_specs=_rspec(tr, N_IN * d), out_shape=jax.ShapeDtypeStruct((rows, N_IN * d), BF16),
        compiler_params=_params(("parallel",)))(dq_f, dq_b, dk_f, dk_b, dv_f, dv_b, cos, sin, dz_conv, dz_g, dz_gc, dz_gr)


def _gn_fwd(o_f, o_b, z, rows, row_off, name):
    d = z.shape[1] // N_IN
    dv = 2 * d // HEADS
    tr = _tile(row_off, 128, 8)
    off = row_off // tr

    def body(of_ref, ob_ref, g0_ref, g1_ref, r_ref):
        for h in range(HEADS):
            lo, hi = h * dv, (h + 1) * dv
            g_ref, glo = (g0_ref, lo) if hi <= d else (g1_ref, lo - d)
            o = of_ref[:, lo:hi] + ob_ref[:, lo:hi]
            cen = o - jnp.mean(o, axis=-1, keepdims=True)
            on = cen * lax.rsqrt(jnp.mean(cen * cen, axis=-1, keepdims=True) + EPS)
            g = g_ref[:, glo:glo + dv]
            r_ref[:, lo:hi] = (g * _sig(g) * on).astype(BF16)

    return pl.pallas_call(
        body, name=name, grid=(rows // tr,),
        in_specs=[_rspec(tr, 2 * d, off=off), _rspec(tr, 2 * d, off=off), _rspec(tr, d, CG), _rspec(tr, d, CG + 1)],
        out_specs=_rspec(tr, 2 * d), out_shape=jax.ShapeDtypeStruct((rows, 2 * d), BF16),
        compiler_params=_params(("parallel",)))(o_f, o_b, z, z)


def _gn_bwd(dr, o_f, o_b, z, row_off, name):
    rows = dr.shape[0]
    d = z.shape[1] // N_IN
    dv = 2 * d // HEADS
    tr = _tile(row_off, 128, 8)
    off = row_off // tr

    def body(dr_ref, of_ref, ob_ref, g0_ref, g1_ref, do_ref, dg_ref):
        for h in range(HEADS):
            lo, hi = h * dv, (h + 1) * dv
            g_ref, glo = (g0_ref, lo) if hi <= d else (g1_ref, lo - d)
            o = of_ref[:, lo:hi] + ob_ref[:, lo:hi]
            cen = o - jnp.mean(o, axis=-1, keepdims=True)
            rstd = lax.rsqrt(jnp.mean(cen * cen, axis=-1, keepdims=True) + EPS)
            on = cen * rstd
            g = g_ref[:, glo:glo + dv]
            sg = _sig(g)
            drv = dr_ref[:, lo:hi]
            dg_ref[:, lo:hi] = (drv * on * (sg * (1.0 + g * (1.0 - sg)))).astype(BF16)
            don = drv * (g * sg)
            do = rstd * (don - jnp.mean(don, axis=-1, keepdims=True)
                         - on * jnp.mean(don * on, axis=-1, keepdims=True))
            do_ref[:, lo:hi] = do.astype(BF16)

    return pl.pallas_call(
        body, name=name, grid=(rows // tr,),
        in_specs=[_rspec(tr, 2 * d), _rspec(tr, 2 * d, off=off), _rspec(tr, 2 * d, off=off),
                  _rspec(tr, d, CG), _rspec(tr, d, CG + 1)],
        out_specs=[_rspec(tr, 2 * d), _rspec(tr, 2 * d)],
        out_shape=[jax.ShapeDtypeStruct((rows, 2 * d), BF16), jax.ShapeDtypeStruct((rows, 2 * d), BF16)],
        compiler_params=_params(("parallel",)))(dr, o_f, o_b, z, z)


def _decays(lg, rev):
    row = lax.broadcasted_iota(jnp.int32, (CHUNK, CHUNK), 0)
    col = lax.broadcasted_iota(jnp.int32, (CHUNK, CHUNK), 1)
    rel = ((col - row) if rev else (row - col)).astype(F32)
    mask = jnp.where(rel >= 0, jnp.exp(lg * jnp.maximum(rel, 0.0)), 0.0)
    r = lax.broadcasted_iota(jnp.int32, (CHUNK, 1), 0)
    rr = ((CHUNK - 1 - r) if rev else r).astype(F32)
    chunk_decay = jnp.exp(lg * jnp.full((1, 1), float(CHUNK), F32))
    return (rr, rel), mask, jnp.exp(lg * (rr + 1.0)), jnp.exp(lg * (CHUNK - 1.0 - rr)), chunk_decay


_NT = (((1,), (1,)), ((), ()))
_TN = (((0,), (0,)), ((), ()))


def _dot(a, b, dims=None):
    if dims is None:
        return jnp.dot(a, b, preferred_element_type=F32)
    return lax.dot_general(a, b, dims, preferred_element_type=F32)


def _ret_fwd(q, k, v, lgs, name):
    tt, d = q.shape
    dk, dv = d // HEADS, 2 * d // HEADS
    nc = tt // CHUNK

    def cmap(rev):
        return lambda h, i: ((nc - 1 - i) if rev else i, h)

    def smap(rev):
        return lambda h, i: ((nc - 1 - i) if rev else i, h, 0, 0)

    def body(lg_ref, qf_ref, kf_ref, vf_ref, qb_ref, kb_ref, vb_ref, of_ref, sf_ref, ob_ref, sb_ref, state_f, state_b):
        h = pl.program_id(0)

        @pl.when(pl.program_id(1) == 0)
        def _():
            state_f[...] = jnp.zeros_like(state_f)
            state_b[...] = jnp.zeros_like(state_b)

        for rev, (q_ref, k_ref, v_ref, o_ref, s_ref, state) in enumerate((
                (qf_ref, kf_ref, vf_ref, of_ref, sf_ref, state_f), (qb_ref, kb_ref, vb_ref, ob_ref, sb_ref, state_b))):
            _, mask, qd, kd, cd = _decays(lg_ref[rev, h], bool(rev))
            qv, kv, vv = q_ref[...], k_ref[...], v_ref[...]
            st = state[...]
            p = _dot(qv, kv, _NT) * mask
            o_ref[...] = _dot(p.astype(BF16), vv) + _dot((qv * qd).astype(BF16), st.astype(BF16))
            s_ref[...] = st
            state[...] = cd * st + _dot((kv * kd).astype(BF16), vv, _TN)

    def specs(rev):
        return [pl.BlockSpec((CHUNK, dk), cmap(rev)), pl.BlockSpec((CHUNK, dk), cmap(rev)),
                pl.BlockSpec((CHUNK, dv), cmap(rev))]

    def outs(rev):
        return [pl.BlockSpec((CHUNK, dv), cmap(rev)), pl.BlockSpec((None, None, dk, dv), smap(rev))]

    return pl.pallas_call(
        body, name=name, grid=(HEADS, nc),
        in_specs=[pl.BlockSpec(memory_space=pltpu.SMEM)] + specs(False) + specs(True),
        out_specs=outs(False) + outs(True),
        out_shape=[jax.ShapeDtypeStruct((tt, 2 * d), F32), jax.ShapeDtypeStruct((nc, HEADS, dk, dv), F32)] * 2,
        scratch_shapes=[pltpu.VMEM((dk, dv), F32), pltpu.VMEM((dk, dv), F32)],
        compiler_params=_params(("parallel", "arbitrary")))(lgs, q, k, v, q, k, v)


def _ret_bwd(q, k, v, do, states_f, states_b, lgs, name):
    tt, d = q.shape
    dk, dv = d // HEADS, 2 * d // HEADS
    nc = tt // CHUNK
    lat_chunks = do.shape[0] // CHUNK
    ctx_chunks = (nc - lat_chunks) // 2

    def chunk_of(rev, i):
        return i if rev else nc - 1 - i

    def cmap(rev):
        return lambda h, i: (chunk_of(rev, i), h)

    def do_map(rev):
        return lambda h, i: (jnp.clip(chunk_of(rev, i) - ctx_chunks, 0, lat_chunks - 1), h)

    def smap(rev):
        return lambda h, i: (chunk_of(rev, i), h, 0, 0)

    def body(lg_ref, *refs):
        h = pl.program_id(0)
        dstates = refs[-2:]

        @pl.when(pl.program_id(1) == 0)
        def _():
            for rev in (0, 1):
                dstates[rev][...] = jnp.zeros_like(dstates[rev])
                refs[10 + 4 * rev + 3][...] = jnp.zeros_like(refs[10 + 4 * rev + 3])

        for rev in (0, 1):
            q_ref, k_ref, v_ref, do_ref, s_ref = refs[5 * rev:5 * rev + 5]
            dq_ref, dk_ref, dv_ref, dlg_ref = refs[10 + 4 * rev:10 + 4 * rev + 4]
            dstate = dstates[rev]
            chunk = chunk_of(bool(rev), pl.program_id(1))
            is_lat = (chunk >= ctx_chunks) & (chunk < ctx_chunks + lat_chunks)
            (rr, rel), mask, qd, kd, cd = _decays(lg_ref[rev, h], bool(rev))
            qv, kv, vv = q_ref[...], k_ref[...], v_ref[...]
            dov = jnp.where(is_lat, do_ref[...], jnp.zeros_like(do_ref))
            st, dst = s_ref[...], dstate[...]
            st_b, dst_b = st.astype(BF16), dst.astype(BF16)
            p = _dot(qv, kv, _NT) * mask
            dp = _dot(dov, vv, _NT)
            da = (dp * mask).astype(BF16)
            dq_state = _dot(dov, st_b, _NT) * qd
            dk_state = _dot(vv, dst_b, _NT) * kd
            dq_ref[...] = _dot(da, kv) + dq_state
            dk_ref[...] = _dot(da, qv, _TN) + dk_state
            dv_ref[...] = _dot(p.astype(BF16), dov, _TN) + _dot((kv * kd).astype(BF16), dst_b)
            dnew = cd * dst + _dot((qv * qd).astype(BF16), dov, _TN)
            dstate[...] = dnew
            through = rr * (jnp.sum(qv.astype(F32) * dq_state, axis=1, keepdims=True)
                            - jnp.sum(kv.astype(F32) * dk_state, axis=1, keepdims=True))
            dlg_ref[...] += _total(rel * p * dp) + _total(through) + CHUNK * _total(st * dnew)

    def specs(rev):
        return [pl.BlockSpec((CHUNK, dk), cmap(rev)), pl.BlockSpec((CHUNK, dk), cmap(rev)),
                pl.BlockSpec((CHUNK, dv), cmap(rev)), pl.BlockSpec((CHUNK, dv), do_map(rev)),
                pl.BlockSpec((None, None, dk, dv), smap(rev))]

    def outs(rev):
        return [pl.BlockSpec((CHUNK, dk), cmap(rev)), pl.BlockSpec((CHUNK, dk), cmap(rev)),
                pl.BlockSpec((CHUNK, dv), cmap(rev)), pl.BlockSpec((None, 8, 128), lambda h, i: (h, 0, 0))]

    return pl.pallas_call(
        body, name=name, grid=(HEADS, nc),
        in_specs=[pl.BlockSpec(memory_space=pltpu.SMEM)] + specs(False) + specs(True),
        out_specs=outs(False) + outs(True),
        out_shape=[jax.ShapeDtypeStruct((tt, d), F32), jax.ShapeDtypeStruct((tt, d), F32),
                   jax.ShapeDtypeStruct((tt, 2 * d), F32), jax.ShapeDtypeStruct((HEADS, 8, 128), F32)] * 2,
        scratch_shapes=[pltpu.VMEM((dk, dv), F32), pltpu.VMEM((dk, dv), F32)],
        compiler_params=_params(("parallel", "arbitrary")))(lgs, q, k, v, do, states_f, q, k, v, do, states_b)


def _silu(v):
    return v * _sig(v)


def _mod_proj(cs, w_mod, b_loc, dec, name):
    nb = w_mod.shape[1]

    def body(cs_ref, w_ref, b_ref, dec_ref, out_ref, lg_ref):
        out_ref[...] = jnp.dot(_silu(cs_ref[...]), w_ref[...], preferred_element_type=F32, precision=HIGHEST) + b_ref[...]
        a = dec_ref[...]
        lg_ref[...] = jnp.minimum(a, 0.0) - jnp.log1p(jnp.exp(-jnp.abs(a)))

    return pl.pallas_call(
        body, name=name,
        out_shape=[jax.ShapeDtypeStruct((16, nb), F32), jax.ShapeDtypeStruct(dec.shape, F32)],
        compiler_params=_params())(cs, w_mod, b_loc, dec)


def _mod_grad(cs, dm, w_mod, name):
    d, nb = w_mod.shape

    def body(cs_ref, dm_ref, w_ref, gw_ref, part_ref):
        dmv = dm_ref[...]
        gw_ref[...] = lax.dot_general(_silu(cs_ref[...]), dmv, _TN, preferred_element_type=F32, precision=HIGHEST)
        part_ref[...] = lax.dot_general(dmv, w_ref[...], _NT, preferred_element_type=F32, precision=HIGHEST)

    return pl.pallas_call(
        body, name=name,
        out_shape=[jax.ShapeDtypeStruct((d, nb), F32), jax.ShapeDtypeStruct((16, d), F32)],
        compiler_params=_params())(cs, dm, w_mod)


def _reduce_small(gathered, dec, n_feat, name):
    _, rows, d = gathered.shape

    def body(g_ref, dec_ref, red_ref, misc_ref):
        total = g_ref[0]
        for i in range(1, N_DEV):
            total = total + g_ref[i]
        red_ref[...] = total
        misc_ref[...] = jnp.zeros_like(misc_ref)
        misc_ref[0:1, :] = jnp.zeros((1, 128), F32) + (0.5 / n_feat) * _total(total[16:17, :])
        misc_ref[1:3, :] = total[14:16, 0:128] * _sig(-dec_ref[0:2, :])

    return pl.pallas_call(
        body, name=name,
        out_shape=[jax.ShapeDtypeStruct((rows, d), F32), jax.ShapeDtypeStruct((8, 128), F32)],
        compiler_params=_params())(gathered, dec)


def _c_ctx_grad(parts, c_ctx, name):
    d = c_ctx.shape[1]

    def body(p_ref, c_ref, out_ref):
        total = p_ref[0]
        for i in range(1, N_DEV):
            total = total + p_ref[i]
        cv = c_ref[...]
        sg = _sig(cv)
        out_ref[...] = total[8:9, :] * (sg * (1.0 + cv * (1.0 - sg)))

    return pl.pallas_call(body, name=name, out_shape=jax.ShapeDtypeStruct((1, d), F32),
                          compiler_params=_params())(parts, c_ctx)


def _adamw(parts, w, m, v, name, own=None):
    n_parts, rows, cols = parts.shape
    row_bytes = cols * (parts.dtype.itemsize * (n_parts + 1) + 7 * 4)
    tr = _tile(rows, max(16, (4 * 1024 * 1024) // row_bytes), 16 if rows % 16 == 0 else 8)
    blocks = rows // tr

    def body(*refs):
        p_ref, w_ref, m_ref, v_ref, g_ref, d_ref, mo_ref, vo_ref = refs[-8:]
        if own is None:
            g = p_ref[0].astype(F32)
        else:
            g = refs[1][...].astype(F32) + p_ref[0].astype(F32)
        for i in range(1, n_parts):
            g = g + p_ref[i].astype(F32)
        m2 = ADAM_B1 * m_ref[...] + (1.0 - ADAM_B1) * g
        v2 = ADAM_B2 * v_ref[...] + (1.0 - ADAM_B2) * jnp.square(g)
        m_hat = m2 / (1.0 - ADAM_B1 ** ADAM_STEP)
        v_hat = v2 / (1.0 - ADAM_B2 ** ADAM_STEP)
        g_ref[...] = g
        d_ref[...] = -ADAM_LR * (m_hat / (jnp.sqrt(v_hat) + ADAM_EPS) + ADAM_WD * w_ref[...])
        mo_ref[...] = m2
        vo_ref[...] = v2

    out_shape = [jax.ShapeDtypeStruct((rows, cols), F32)] * 4
    if own is None:
        spec = _rspec(tr, cols)
        return pl.pallas_call(
            body, name=name, grid=(blocks,),
            in_specs=[pl.BlockSpec((n_parts, tr, cols), lambda i: (0, i, 0)), spec, spec, spec],
            out_specs=[spec] * 4, out_shape=out_shape, compiler_params=_params(("parallel",)))(parts, w, m, v)
    g_full, axis, me = own

    def own_map(i, me_ref):
        return (me_ref[0] * blocks + i, 0) if axis == 0 else (i, me_ref[0])

    spec = pl.BlockSpec((tr, cols), lambda i, me_ref: (i, 0))
    return pl.pallas_call(
        body, name=name, out_shape=out_shape,
        grid_spec=pltpu.PrefetchScalarGridSpec(
            num_scalar_prefetch=1, grid=(blocks,),
            in_specs=[pl.BlockSpec((tr, cols), own_map),
                      pl.BlockSpec((n_parts, tr, cols), lambda i, me_ref: (0, i, 0)), spec, spec, spec],
            out_specs=[spec] * 4),
        compiler_params=_params(("parallel",)))(me.reshape(1), g_full, parts, w, m, v)


def _rope_tables(pos, dk):
    half = dk // 2
    inv_freq = 1.0 / (ROPE_BASE ** jnp.linspace(0.0, 1.0, half, dtype=F32))
    ang = pos[:, None] * inv_freq[None, :]
    return jnp.cos(ang), jnp.sin(ang)


def _pad_lanes(v, width):
    return jnp.pad(v, ((0, 0), (0, width - v.shape[1])))


def kernel(x, c, ctx, c_ctx, w_mod, b_mod, norm1_g, w_in, conv_w, w_conv_out, ret_decay_fwd, ret_decay_bwd, w_ret_out, w_o, norm2_g, w_ff1, w_ff2, final_g, loss_target, m_c_ctx, m_w_mod, m_b_mod, m_norm1_g, m_w_in, m_conv_w, m_w_conv_out, m_ret_decay_fwd, m_ret_decay_bwd, m_w_ret_out, m_w_o, m_norm2_g, m_w_ff1, m_w_ff2, m_final_g, v_c_ctx, v_w_mod, v_b_mod, v_norm1_g, v_w_in, v_conv_w, v_w_conv_out, v_ret_decay_fwd, v_ret_decay_bwd, v_w_ret_out, v_w_o, v_norm2_g, v_w_ff1, v_w_ff2, v_final_g):
    n_lat, d = x.shape[1], x.shape[2]
    n_ctx = ctx.shape[1]
    assert n_ctx % CHUNK == 0 and n_lat % CHUNK == 0 and n_lat % GRID_W == 0
    dk = d // HEADS
    nb = w_mod.shape[2]
    me = 4 * lax.axis_index("x") + 2 * lax.axis_index("y") + lax.axis_index("c")
    xl, ctxl, target = x[0], ctx[0], loss_target[0]

    w_in_f = _allgather(w_in[0].astype(BF16), 1, "ag_w_in")
    conv_w_f = _small_allgather(conv_w[0], "ag_conv_w").transpose(1, 0, 2).reshape(3, d)

    c_all = _small_allgather(c, "ag_c").reshape(N_DEV, d)
    cs = jnp.concatenate([c_all, c_ctx[None], jnp.zeros((7, d), F32)], axis=0)
    dec = jnp.pad(jnp.concatenate([ret_decay_fwd, ret_decay_bwd], axis=0), ((0, 6), (0, 128 - HEADS)))
    b_loc = lax.dynamic_slice(b_mod, (0, me * nb), (1, nb))
    modp, lgs = _mod_proj(cs, w_mod[0], b_loc, dec, "mod_proj")
    modp_all = _small_allgather(modp, "ag_mod")
    mod_l = lax.dynamic_index_in_dim(modp_all, me, axis=1, keepdims=False).reshape(N_MOD, d)
    mod_c = modp_all[:, 8, :].reshape(N_MOD, d)
    lgs = lgs[0:2, :HEADS]
    zero_row = jnp.zeros((1, d), F32)
    vec1_l = jnp.concatenate([norm1_g, mod_l[0:1], mod_l[1:2], zero_row], axis=0)
    vec1_c = jnp.concatenate([norm1_g, mod_c[0:1], mod_c[1:2], zero_row], axis=0)
    vec2 = jnp.concatenate([norm2_g, mod_l[3:4], mod_l[4:5], mod_l[2:3]], axis=0)
    vec3 = jnp.concatenate([mod_l[5:6], final_g[None]], axis=0)

    later, after = {}, (w_in_f, conv_w_f, modp_all)
    for wname, w, axis in (("w_conv_out", w_conv_out, 0), ("w_ret_out", w_ret_out, 0), ("w_o", w_o, 0),
                           ("w_ff1", w_ff1, 1), ("w_ff2", w_ff2, 0)):
        later[wname], token = _exchange_start(_place_shard(w[0], axis, me, "place_" + wname), axis, True,
                                              "ag_" + wname + "_start", after=after)
        after = (token,)

    a_all = jnp.concatenate([_modulate(xl, vec1_l, "modulate1"), _modulate(ctxl, vec1_c, "modulate1_ctx")], axis=0)
    z = _mm(a_all, w_in_f, "nn", [F32], "proj_in", tn=2048, dep=token)
    pc = _conv_fwd(z, conv_w_f, n_lat, "conv_fwd")
    pos = jnp.concatenate([n_ctx + jnp.arange(n_lat, dtype=F32), jnp.arange(n_ctx, dtype=F32)])
    cos, sin = _rope_tables(pos, dk)
    q_s, k_s, v_s = _rotary_fwd(z, cos, sin, n_lat, "rotary_fwd")
    o_f, st_f, o_b, st_b = _ret_fwd(q_s, k_s, v_s, lgs, "ret_fwd")
    r = _gn_fwd(o_f, o_b, z, n_lat, n_ctx, "gn_fwd")
    w_conv_out_f = _exchange_wait(later["w_conv_out"], r, "ag_w_conv_out_wait")
    w_ret_out_f = _exchange_wait(later["w_ret_out"], r, "ag_w_ret_out_wait")
    w_o_f = _exchange_wait(later["w_o"], r, "ag_w_o_wait")
    y_conv = _mm(pc, w_conv_out_f, "nn", [F32], "proj_conv_out", tn=2048)
    tn_d = _tile(d, 1024, 128)
    gate_offs = (CGC * d // tn_d, CGR * d // tn_d)

    def merge(acc, yc, gc, gr):
        return acc, _sig(gc) * yc + _sig(gr) * acc

    y_ret, mg = _mm(r, w_ret_out_f, "nn", [F32, BF16], "proj_ret_out", tm=512, tn=tn_d,
                    extras=[(y_conv, 0), (z, gate_offs[0]), (z, gate_offs[1])], epilogue=merge)
    y_l = _mm(mg, w_o_f, "nn", [F32], "proj_o", tn=2048)
    x1, a2 = _resid_modulate(xl, y_l, vec2, "resid_modulate2")

    def sqrelu(acc):
        return acc, jnp.square(jnp.maximum(acc, 0.0))

    w_ff1_f = _exchange_wait(later["w_ff1"], a2, "ag_w_ff1_wait")
    hff, s = _mm(a2, w_ff1_f, "nn", [F32, BF16], "ff1", epilogue=sqrelu)
    w_ff2_f = _exchange_wait(later["w_ff2"], s, "ag_w_ff2_wait")
    f = _mm(s, w_ff2_f, "nn", [F32], "ff2")
    dx2, df, acc3 = _loss_head(x1, f, target, vec3, "loss_head")

    def d_sqrelu(acc, h):
        return (acc * (2.0 * jnp.maximum(h, 0.0)),)

    dh = _mm(df, w_ff2_f, "nt", [BF16], "ff2_dx", extras=[(hff, 0)], epilogue=d_sqrelu)
    sent = {}
    sent["w_ff2"], token = _exchange_start(_mm(s, df, "tn", [BF16], "ff2_dw"), 0, False, "rs_w_ff2_start")
    da2 = _mm(dh, w_ff1_f, "nt", [F32], "ff1_dx", dep=token)
    sent["w_ff1"], token = _exchange_start(_mm(a2, dh, "tn", [BF16], "ff1_dw"), 1, False, "rs_w_ff1_start")
    dx1, dyl, acc2 = _modulate_bwd(da2, x1, vec2, "modulate2_bwd", dx_in=dx2, y=y_l)

    def d_merge(acc, yc, yr, gc, gr):
        sc, sr = _sig(gc), _sig(gr)
        return acc * sc, acc * sr, acc * yc * (sc * (1.0 - sc)), acc * yr * (sr * (1.0 - sr))

    dyc, dyr, dgc, dgr = _mm(dyl, w_o_f, "nt", [BF16] * 4, "proj_o_dx", tm=512, tn=tn_d,
                             extras=[(y_conv, 0), (y_ret, 0), (z, gate_offs[0]), (z, gate_offs[1])], epilogue=d_merge,
                             dep=token)
    sent["w_o"], token = _exchange_start(_mm(mg, dyl, "tn", [BF16], "proj_o_dw"), 0, False, "rs_w_o_start")
    dpc = _mm(dyc, w_conv_out_f, "nt", [F32], "proj_conv_out_dx", tn=2048, dep=token)
    sent["w_conv_out"], token = _exchange_start(_mm(pc, dyc, "tn", [BF16], "proj_conv_out_dw"), 0, False,
                                                "rs_w_conv_out_start")
    dz_conv, acc_conv = _conv_bwd(dpc, z, conv_w_f, "conv_bwd")
    dr = _mm(dyr, w_ret_out_f, "nt", [F32], "proj_ret_out_dx", dep=token)
    sent["w_ret_out"], token = _exchange_start(_mm(r, dyr, "tn", [BF16], "proj_ret_out_dw"), 0, False,
                                               "rs_w_ret_out_start")
    do, dz_g = _gn_bwd(dr, o_f, o_b, z, n_ctx, "gn_bwd")
    dq_f, dk_f, dv_f, dlg_f, dq_b, dk_b, dv_b, dlg_b = _ret_bwd(q_s, k_s, v_s, do, st_f, st_b, lgs, "ret_bwd")
    dz = _dz_assemble(dq_f, dq_b, dk_f, dk_b, dv_f, dv_b, cos, sin, dz_conv, dz_g, dgc, dgr, n_lat, n_ctx,
                      "dz_assemble")
    g_in = _mm(a_all, dz, "tn", [BF16], "proj_in_dw", tn=2048, tk=(n_lat + n_ctx) // 4, dep=token)
    sent["w_in"], token = _exchange_start(g_in, 1, False, "rs_w_in_start")
    da_all = _mm(dz, w_in_f, "nt", [F32], "proj_in_dx", tn=2048, dep=token)
    grad_x, acc1 = _modulate_bwd(da_all, xl, vec1_l, "modulate1_bwd", dx_in=dx1)
    _, acc1c = _modulate_bwd(da_all, ctxl, vec1_c, "modulate1_ctx_bwd", da_off=n_lat)

    lane_pad = functools.partial(_pad_lanes, width=d)
    packet = jnp.concatenate([
        acc1[2:3] + acc1c[2:3], acc2[2:3], acc3[1:2],
        acc1[0:1], acc1[1:2], acc2[3:4], acc2[0:1], acc2[1:2], acc3[0:1],
        acc1c[0:1], acc1c[1:2],
        acc_conv[0:3],
        lane_pad(dlg_f[:, 0, 0][None]), lane_pad(dlg_b[:, 0, 0][None]),
        acc3[2:3],
        jnp.zeros((7, d), F32)], axis=0)
    gathered = _small_allgather(packet, "ag_small")
    red, misc = _reduce_small(gathered, dec, d, "reduce_small")
    dmod_ctx = jnp.concatenate([red[9], red[10], jnp.zeros((4 * d,), F32)])
    dmod_all = jnp.concatenate([gathered[:, 3:9, :].reshape(N_DEV, N_MOD * d), dmod_ctx[None]], axis=0)
    dm = jnp.pad(lax.dynamic_slice(dmod_all, (0, me * nb), (N_DEV + 1, nb)), ((0, 7), (0, 0)))
    g_mod, c_part = _mod_grad(cs, dm, w_mod[0], "mod_grad")
    g_c_ctx = _c_ctx_grad(_small_allgather(c_part, "ag_c_ctx"), c_ctx[None], "c_ctx_grad")
    g_b_mod = red[3:9].reshape(1, N_MOD * d) + dmod_ctx[None]
    g_conv_w = lax.dynamic_slice(red[11:14], (0, me * (d // N_DEV)), (3, d // N_DEV))

    res = {"w_mod": _adamw(g_mod[None], w_mod[0], m_w_mod[0], v_w_mod[0], "adamw_w_mod")}
    after = res["w_mod"][0]
    for wname, axis, w, m, v in (("w_ff2", 0, w_ff2, m_w_ff2, v_w_ff2), ("w_ff1", 1, w_ff1, m_w_ff1, v_w_ff1),
                                 ("w_o", 0, w_o, m_w_o, v_w_o), ("w_conv_out", 0, w_conv_out, m_w_conv_out, v_w_conv_out),
                                 ("w_ret_out", 0, w_ret_out, m_w_ret_out, v_w_ret_out),
                                 ("w_in", 1, w_in, m_w_in, v_w_in)):
        g_full, parts = _exchange_wait(sent[wname], after, "rs_" + wname + "_wait")
        res[wname] = _adamw(parts, w[0], m[0], v[0], "adamw_" + wname, own=(g_full, axis, me))
        after = res[wname][0]
    res = {k: tuple(t[None] for t in val) for k, val in res.items()}

    small = [("c_ctx", g_c_ctx, c_ctx, m_c_ctx, v_c_ctx), ("b_mod", g_b_mod, b_mod, m_b_mod, v_b_mod),
             ("norm1_g", red[0:1], norm1_g, m_norm1_g, v_norm1_g), ("conv_w", g_conv_w, conv_w, m_conv_w, v_conv_w),
             ("ret_decay_fwd", misc[1:2, :HEADS], ret_decay_fwd, m_ret_decay_fwd, v_ret_decay_fwd),
             ("ret_decay_bwd", misc[2:3, :HEADS], ret_decay_bwd, m_ret_decay_bwd, v_ret_decay_bwd),
             ("norm2_g", red[1:2], norm2_g, m_norm2_g, v_norm2_g), ("final_g", red[2:3], final_g, m_final_g, v_final_g)]

    def flat(t):
        t = t.reshape(-1)
        return jnp.pad(t, (0, (-t.shape[0]) % 1024))

    packed = [jnp.concatenate([flat(item[j]) for item in small]).reshape(-1, 128) for j in range(1, 5)]
    outs = _adamw(packed[0][None], packed[1], packed[2], packed[3], "adamw_small")
    start = 0
    for name, _, w, _, _ in small:
        size = w.size
        res[name] = tuple(o.reshape(-1)[start:start + size].reshape(w.shape) for o in outs)
        start += size + (-size) % 1024

    order = ["c_ctx", "w_mod", "b_mod", "norm1_g", "w_in", "conv_w", "w_conv_out", "ret_decay_fwd", "ret_decay_bwd",
             "w_ret_out", "w_o", "norm2_g", "w_ff1", "w_ff2", "final_g"]
    loss = misc[0, 0]
    return (loss, grad_x[None], *[res[n][0] for n in order], *[res[n][1] for n in order],
            *[res[n][2] for n in order], *[res[n][3] for n in order])
```

```python
import functools

import jax
import jax.numpy as jnp
from jax import lax
from jax.experimental import pallas as pl
from jax.experimental.pallas import tpu as pltpu

F32 = jnp.float32
BF16 = jnp.bfloat16
MESH = pl.DeviceIdType.MESH

N_DEV = 8
HEADS = 8
N_MOD = 6
N_IN = 11
GRID_W = 64
CHUNK = 256
ROPE_BASE = 10000.0
EPS = 1e-6
ADAM_LR, ADAM_B1, ADAM_B2, ADAM_EPS, ADAM_WD, ADAM_STEP = 0.001, 0.9, 0.999, 1e-08, 0.01, 10
VMEM_LIMIT = 56 * 1024 * 1024
HIGHEST = lax.Precision.HIGHEST
CB, CC, CX, CQ, CK, CV, CG, CGC, CGR = 0, 1, 2, 3, 4, 5, 7, 9, 10


def _tile(n, target, mult):
    t = (min(target, n) // mult) * mult
    while t >= mult:
        if n % t == 0:
            return t
        t -= mult
    return n


def _params(sem=None):
    return pltpu.CompilerParams(dimension_semantics=sem, vmem_limit_bytes=VMEM_LIMIT)


def _sig(v):
    return 1.0 / (1.0 + jnp.exp(-v))


def _coords():
    return lax.axis_index("x"), lax.axis_index("y"), lax.axis_index("c")


def _flip(p, m):
    return tuple(1 - v if (m >> s) & 1 else v for v, s in zip(p, (2, 1, 0)))


def _index(p):
    return 4 * p[0] + 2 * p[1] + p[2]


def _small_allgather(x, name):
    r, n = x.shape

    def body(x_ref, out_ref, send_sems, recv_sems):
        me = _coords()
        out_ref[pl.ds(_index(me), 1)] = x_ref[...][None]
        sent = []
        for m in range(1, N_DEV):
            cp = pltpu.make_async_remote_copy(
                src_ref=x_ref, dst_ref=out_ref.at[_index(me)], send_sem=send_sems.at[m - 1],
                recv_sem=recv_sems.at[m - 1], device_id=_flip(me, m), device_id_type=MESH)
            cp.start()
            sent.append(cp)
        for m in range(1, N_DEV):
            pltpu.make_async_remote_copy(
                src_ref=x_ref, dst_ref=out_ref.at[_index(_flip(me, m))], send_sem=send_sems.at[m - 1],
                recv_sem=recv_sems.at[m - 1], device_id=_flip(me, m), device_id_type=MESH).wait_recv()
        for cp in sent:
            cp.wait_send()

    return pl.pallas_call(
        body, name=name, out_shape=jax.ShapeDtypeStruct((N_DEV, r, n), x.dtype),
        in_specs=[pl.BlockSpec(memory_space=pltpu.VMEM)], out_specs=pl.BlockSpec(memory_space=pltpu.VMEM),
        scratch_shapes=[pltpu.SemaphoreType.DMA((N_DEV - 1,)), pltpu.SemaphoreType.DMA((N_DEV - 1,))],
    )(x)


def _window(ref, j, r, c, axis):
    if axis == 0:
        return ref.at[pl.ds(pl.multiple_of(j * r, 8), r), :]
    return ref.at[:, pl.ds(pl.multiple_of(j * c, 128), c)]


def _allgather(x, axis, name):
    r, c = x.shape
    full = (N_DEV * r, c) if axis == 0 else (r, N_DEV * c)

    def body(x_ref, out_ref, send_sems, recv_sems, local_sem):
        me = _coords()
        sibling = _flip(me, 1)
        chips = [4, 2, 6]

        def copy(k, block, to, src=None):
            dst = _window(out_ref, _index(block), r, c, axis)
            return pltpu.make_async_remote_copy(
                src_ref=dst if src is None else src, dst_ref=dst, send_sem=send_sems.at[k],
                recv_sem=recv_sems.at[k], device_id=to, device_id_type=MESH)

        mine = pltpu.make_async_copy(x_ref, _window(out_ref, _index(me), r, c, axis), local_sem)
        mine.start()
        first = [copy(0, me, sibling, src=x_ref)]
        first += [copy(1 + j, me, _flip(me, m), src=x_ref) for j, m in enumerate(chips)]
        for cp in first:
            cp.start()
        passed = [copy(4 + j, _flip(me, m), sibling) for j, m in enumerate(chips)]
        for j, m in enumerate(chips):
            copy(1 + j, _flip(me, m), me).wait_recv()
            passed[j].start()
        copy(0, sibling, me).wait_recv()
        for j, m in enumerate(chips):
            copy(4 + j, _flip(sibling, m), me).wait_recv()
        for cp in first + passed:
            cp.wait_send()
        mine.wait()

    return pl.pallas_call(
        body, name=name, out_shape=jax.ShapeDtypeStruct(full, x.dtype),
        in_specs=[pl.BlockSpec(memory_space=pl.ANY)], out_specs=pl.BlockSpec(memory_space=pl.ANY),
        scratch_shapes=[pltpu.SemaphoreType.DMA((N_DEV - 1,)), pltpu.SemaphoreType.DMA((N_DEV - 1,)),
                        pltpu.SemaphoreType.DMA(())],
    )(x)


_HBM = pl.BlockSpec(memory_space=pltpu.HBM)
_SEM = pl.BlockSpec(memory_space=pltpu.SEMAPHORE)
_ANY = pl.BlockSpec(memory_space=pl.ANY)
_EFFECT = pltpu.SideEffectType.DATAFLOW_SIDE_EFFECTING


def _place_shard(w, axis, me, name):
    r, c = w.shape
    full = (N_DEV * r, c) if axis == 0 else (r, N_DEV * c)
    tr = _tile(r, max(16, (2 * 1024 * 1024) // (c * 4)), 16)
    blocks = r // tr

    def out_map(i, me_ref):
        return (me_ref[0] * blocks + i, 0) if axis == 0 else (i, me_ref[0])

    def body(me_ref, w_ref, out_ref):
        out_ref[...] = w_ref[...].astype(BF16)

    return pl.pallas_call(
        body, name=name, out_shape=jax.ShapeDtypeStruct(full, BF16),
        grid_spec=pltpu.PrefetchScalarGridSpec(
            num_scalar_prefetch=1, grid=(blocks,), in_specs=[pl.BlockSpec((tr, c), lambda i, me_ref: (i, 0))],
            out_specs=pl.BlockSpec((tr, c), out_map)),
        compiler_params=_params(("parallel",)))(me.reshape(1), w)


def _exchange_start(src, axis, gather, name, after=()):
    r, c = (src.shape[0] // N_DEV, src.shape[1]) if axis == 0 else (src.shape[0], src.shape[1] // N_DEV)
    n_hbm = 1 if gather else 2
    n_after = len(after)

    def body(*refs):
        src_ref, land_ref = refs[0], refs[n_hbm - 1]
        send_sems, recv_sems = refs[n_hbm + n_after:n_hbm + n_after + 2]
        token = refs[-1]
        me = _coords()
        for m in range(1, N_DEV):
            peer = _flip(me, m)
            if gather:
                mine = theirs = _window(land_ref, _index(me), r, c, axis)
            else:
                mine, theirs = _window(src_ref, _index(peer), r, c, axis), land_ref.at[m - 1]
            pltpu.make_async_remote_copy(
                src_ref=mine, dst_ref=theirs, send_sem=send_sems.at[m - 1], recv_sem=recv_sems.at[m - 1],
                device_id=peer, device_id_type=MESH).start()
        token[...] = jnp.zeros_like(token)

    hbm = [pltpu.with_memory_space_constraint(src, pltpu.HBM)]
    if not gather:
        hbm.append(pltpu.with_memory_space_constraint(lax.empty((N_DEV - 1, r, c), src.dtype), pltpu.HBM))
    outs = pl.pallas_call(
        body, name=name,
        out_shape=(pltpu.SemaphoreType.DMA((N_DEV - 1,)), pltpu.SemaphoreType.DMA((N_DEV - 1,)),
                   *[pltpu.HBM(t.shape, t.dtype) for t in hbm], jax.ShapeDtypeStruct((8, 128), F32)),
        in_specs=[_HBM] * n_hbm + [_ANY] * n_after,
        out_specs=(_SEM, _SEM, *[_HBM] * n_hbm, pl.BlockSpec(memory_space=pltpu.VMEM)),
        input_output_aliases={i: 2 + i for i in range(n_hbm)},
        compiler_params=pltpu.CompilerParams(has_side_effects=_EFFECT),
    )(*hbm, *after)
    return (outs[:2], outs[2:2 + n_hbm], (axis, gather, r, c)), outs[-1]


def _exchange_wait(handle, after, name):
    (send_sems, recv_sems), hbm, (axis, gather, r, c) = handle
    n_hbm = len(hbm)

    def body(*refs):
        src_ref, land_ref = refs[0], refs[n_hbm - 1]
        send_sems, recv_sems = refs[n_hbm:n_hbm + 2]
        me = _coords()
        for m in range(1, N_DEV):
            peer = _flip(me, m)
            if gather:
                mine, theirs = _window(land_ref, _index(me), r, c, axis), _window(land_ref, _index(peer), r, c, axis)
            else:
                mine, theirs = _window(src_ref, _index(peer), r, c, axis), land_ref.at[m - 1]
            copy = pltpu.make_async_remote_copy(
                src_ref=mine, dst_ref=theirs, send_sem=send_sems.at[m - 1], recv_sem=recv_sems.at[m - 1],
                device_id=peer, device_id_type=MESH)
            copy.wait_send()
            copy.wait_recv()

    outs = pl.pallas_call(
        body, name=name, out_shape=tuple(pltpu.HBM(t.shape, t.dtype) for t in hbm),
        in_specs=[_HBM] * n_hbm + [_SEM, _SEM, _ANY], out_specs=tuple([_HBM] * n_hbm),
        input_output_aliases={i: i for i in range(n_hbm)},
        compiler_params=pltpu.CompilerParams(has_side_effects=_EFFECT),
    )(*hbm, send_sems, recv_sems, after)
    return outs[0] if gather else tuple(outs)


def _mm(a, b, mode, out_dtypes, name, tm=1024, tn=1024, tk=2048, extras=(), epilogue=None, dep=None):
    if mode == "nn":
        (m, k), n = a.shape, b.shape[1]
    elif mode == "nt":
        (m, k), n = a.shape, b.shape[0]
    else:
        (k, m), n = a.shape, b.shape[1]
    tm, tn = _tile(m, tm, 8), _tile(n, tn, 128)
    tk = _tile(k, tk, 16 if mode == "tn" else 128)
    nk = k // tk
    swap = nk == 1 and (k * n + (n // tn) * m * k) < (m * k + (m // tm) * k * n)

    def ij(p, q):
        return (q, p) if swap else (p, q)

    def spec(shape, fn):
        return pl.BlockSpec(shape, lambda p, q, kk: fn(*ij(p, q), kk))

    a_spec = spec((tk, tm), lambda i, j, kk: (kk, i)) if mode == "tn" else spec((tm, tk), lambda i, j, kk: (i, kk))
    b_spec = spec((tn, tk), lambda i, j, kk: (j, kk)) if mode == "nt" else spec((tk, tn), lambda i, j, kk: (kk, j))
    dims = {"nn": (((1,), (0,)), ((), ())), "nt": (((1,), (1,)), ((), ())), "tn": (((0,), (0,)), ((), ()))}[mode]
    ex_specs = [spec((tm, tn), functools.partial(lambda i, j, kk, off: (i, j + off), off=off)) for _, off in extras]
    deps = [] if dep is None else [dep]
    dep_specs = [pl.BlockSpec(dep.shape, lambda p, q, kk: (0, 0))] if deps else []
    n_ex, n_out = len(extras), len(out_dtypes)
    n_in = 2 + n_ex + len(deps)

    def body(*refs):
        a_ref, b_ref = refs[0], refs[1]
        ex_refs = refs[2:2 + n_ex]
        out_refs = refs[n_in:n_in + n_out]

        def product():
            return lax.dot_general(a_ref[...], b_ref[...], dims, preferred_element_type=F32)

        def finish(res):
            res = epilogue(res, *[e[...] for e in ex_refs]) if epilogue is not None else (res,)
            for o_ref, val in zip(out_refs, res):
                o_ref[...] = val.astype(o_ref.dtype)

        if nk == 1:
            finish(product())
            return
        acc = refs[-1]
        kk = pl.program_id(2)

        @pl.when(kk == 0)
        def _():
            acc[...] = product()

        @pl.when((kk > 0) & (kk < nk - 1))
        def _():
            acc[...] += product()

        @pl.when(kk == nk - 1)
        def _():
            finish(acc[...] + product())

    outs = pl.pallas_call(
        body, name=name, grid=(*ij(m // tm, n // tn), nk),
        in_specs=[a_spec, b_spec] + ex_specs + dep_specs,
        out_specs=[spec((tm, tn), lambda i, j, kk: (i, j)) for _ in out_dtypes],
        out_shape=[jax.ShapeDtypeStruct((m, n), dt) for dt in out_dtypes],
        scratch_shapes=[pltpu.VMEM((tm, tn), F32)] if nk > 1 else [],
        compiler_params=_params(("parallel", "parallel", "arbitrary")),
    )(a, b, *[e for e, _ in extras], *deps)
    return outs if n_out > 1 else outs[0]


def _rspec(tr, w, cb=0, off=0):
    return pl.BlockSpec((tr, w), lambda i: (i + off, cb))


def _cspec(shape):
    return pl.BlockSpec(shape, lambda i: (0,) * len(shape))


def _rms(xf):
    rstd = lax.rsqrt(jnp.mean(xf * xf, axis=-1, keepdims=True) + EPS)
    return xf * rstd, rstd


def _rms_bwd(dn, n, rstd):
    return rstd * (dn - n * jnp.mean(dn * n, axis=-1, keepdims=True))


def _colsum(v):
    return jnp.sum(v, axis=0, keepdims=True)


def _total(v):
    return jnp.sum(jnp.sum(v, axis=1, keepdims=True), axis=0, keepdims=True)


def _modulate(x, vec, name):
    rows, d = x.shape
    tr = _tile(rows, 256, 8)

    def body(x_ref, v_ref, a_ref):
        n, _ = _rms(x_ref[...])
        a_ref[...] = ((n * v_ref[0:1, :]) * (1.0 + v_ref[2:3, :]) + v_ref[1:2, :]).astype(BF16)

    return pl.pallas_call(
        body, name=name, grid=(rows // tr,), in_specs=[_rspec(tr, d), _cspec(vec.shape)],
        out_specs=_rspec(tr, d), out_shape=jax.ShapeDtypeStruct((rows, d), BF16),
        compiler_params=_params(("parallel",)))(x, vec)


def _resid_modulate(x, y, vec, name):
    rows, d = x.shape
    tr = _tile(rows, 256, 8)

    def body(x_ref, y_ref, v_ref, x1_ref, a_ref):
        x1 = x_ref[...] + v_ref[3:4, :] * y_ref[...]
        x1_ref[...] = x1
        n, _ = _rms(x1)
        a_ref[...] = ((n * v_ref[0:1, :]) * (1.0 + v_ref[2:3, :]) + v_ref[1:2, :]).astype(BF16)

    return pl.pallas_call(
        body, name=name, grid=(rows // tr,), in_specs=[_rspec(tr, d), _rspec(tr, d), _cspec(vec.shape)],
        out_specs=[_rspec(tr, d), _rspec(tr, d)],
        out_shape=[jax.ShapeDtypeStruct((rows, d), F32), jax.ShapeDtypeStruct((rows, d), BF16)],
        compiler_params=_params(("parallel",)))(x, y, vec)


def _loss_head(x1, f, target, vec, name):
    rows, d = x1.shape
    tr = _tile(rows, 256, 8)

    def body(x1_ref, f_ref, t_ref, v_ref, dx2_ref, df_ref, acc_ref):
        @pl.when(pl.program_id(0) == 0)
        def _():
            acc_ref[...] = jnp.zeros_like(acc_ref)

        gate, gain = v_ref[0:1, :], v_ref[1:2, :]
        fv = f_ref[...]
        n, rstd = _rms(x1_ref[...] + gate * fv)
        err = n * gain - t_ref[...]
        dy = err / d
        dx2 = _rms_bwd(dy * gain, n, rstd)
        dx2_ref[...] = dx2
        df_ref[...] = (dx2 * gate).astype(BF16)
        acc_ref[0:1, :] += _colsum(dx2 * fv)
        acc_ref[1:2, :] += _colsum(dy * n)
        acc_ref[2:3, :] += _colsum(err * err)

    return pl.pallas_call(
        body, name=name, grid=(rows // tr,),
        in_specs=[_rspec(tr, d), _rspec(tr, d), _rspec(tr, d), _cspec(vec.shape)],
        out_specs=[_rspec(tr, d), _rspec(tr, d), _cspec((8, d))],
        out_shape=[jax.ShapeDtypeStruct((rows, d), F32), jax.ShapeDtypeStruct((rows, d), BF16),
                   jax.ShapeDtypeStruct((8, d), F32)],
        compiler_params=_params(("arbitrary",)))(x1, f, target, vec)


def _modulate_bwd(da, x, vec, name, da_off=0, dx_in=None, y=None):
    rows, d = x.shape
    tr = _tile(rows, 256, 8)
    off = da_off // tr
    has_dx, has_y = dx_in is not None, y is not None

    def body(*refs):
        da_ref, x_ref, v_ref = refs[0], refs[1], refs[2]
        pos = 3
        dxin_ref = refs[pos] if has_dx else None
        pos += has_dx
        y_ref = refs[pos] if has_y else None
        pos += has_y
        dx_ref = refs[pos]
        dy_ref = refs[pos + 1] if has_y else None
        acc_ref = refs[-1]

        @pl.when(pl.program_id(0) == 0)
        def _():
            acc_ref[...] = jnp.zeros_like(acc_ref)

        gain, scale1 = v_ref[0:1, :], 1.0 + v_ref[2:3, :]
        dav = da_ref[...]
        n, rstd = _rms(x_ref[...])
        dx = _rms_bwd(dav * gain * scale1, n, rstd)
        if has_dx:
            dx = dx + dxin_ref[...]
        dx_ref[...] = dx
        acc_ref[0:1, :] += _colsum(dav)
        acc_ref[1:2, :] += _colsum(dav * (n * gain))
        acc_ref[2:3, :] += _colsum(dav * n * scale1)
        if has_y:
            acc_ref[3:4, :] += _colsum(dx * y_ref[...])
            dy_ref[...] = (dx * v_ref[3:4, :]).astype(BF16)

    ins = [da, x, vec] + ([dx_in] if has_dx else []) + ([y] if has_y else [])
    in_specs = [_rspec(tr, d, off=off), _rspec(tr, d), _cspec(vec.shape)] + [_rspec(tr, d)] * (has_dx + has_y)
    out_specs = [_rspec(tr, d)] + ([_rspec(tr, d)] if has_y else []) + [_cspec((8, d))]
    out_shape = ([jax.ShapeDtypeStruct((rows, d), F32)] + ([jax.ShapeDtypeStruct((rows, d), BF16)] if has_y else [])
                 + [jax.ShapeDtypeStruct((8, d), F32)])
    return pl.pallas_call(
        body, name=name, grid=(rows // tr,), in_specs=in_specs, out_specs=out_specs, out_shape=out_shape,
        compiler_params=_params(("arbitrary",)))(*ins)


def _conv_terms(cc, cx, w_ref, tr):
    t = lax.broadcasted_iota(jnp.int32, (tr, 1), 0) % GRID_W
    first, last = t == 0, t == GRID_W - 1
    u = cc * cx
    prev = jnp.where(first, 0.0, pltpu.roll(u, 1, 0))
    nxt = jnp.where(last, 0.0, pltpu.roll(u, tr - 1, 0))
    y = w_ref[0:1, :] * prev + w_ref[1:2, :] * u + w_ref[2:3, :] * nxt
    return u, prev, nxt, y, first, last


def _conv_fwd(z, conv_w, rows, name):
    d = conv_w.shape[1]
    tr = _tile(rows, 256, GRID_W)

    def body(cb_ref, cc_ref, cx_ref, w_ref, out_ref):
        y = _conv_terms(cc_ref[...], cx_ref[...], w_ref, tr)[3]
        out_ref[...] = (cb_ref[...] * y).astype(BF16)

    return pl.pallas_call(
        body, name=name, grid=(rows // tr,),
        in_specs=[_rspec(tr, d, CB), _rspec(tr, d, CC), _rspec(tr, d, CX), _cspec(conv_w.shape)],
        out_specs=_rspec(tr, d), out_shape=jax.ShapeDtypeStruct((rows, d), BF16),
        compiler_params=_params(("parallel",)))(z, z, z, conv_w)


def _conv_bwd(dpc, z, conv_w, name):
    rows, d = dpc.shape
    tr = _tile(rows, 256, GRID_W)

    def body(dpc_ref, cb_ref, cc_ref, cx_ref, w_ref, dz_ref, acc_ref):
        @pl.when(pl.program_id(0) == 0)
        def _():
            acc_ref[...] = jnp.zeros_like(acc_ref)

        cc, cx, dpcv = cc_ref[...], cx_ref[...], dpc_ref[...]
        u, prev, nxt, y, first, last = _conv_terms(cc, cx, w_ref, tr)
        dy = dpcv * cb_ref[...]
        dy_next = jnp.where(last, 0.0, pltpu.roll(dy, tr - 1, 0))
        dy_prev = jnp.where(first, 0.0, pltpu.roll(dy, 1, 0))
        du = w_ref[0:1, :] * dy_next + w_ref[1:2, :] * dy + w_ref[2:3, :] * dy_prev
        dz_ref[:, 0:d] = (dpcv * y).astype(BF16)
        dz_ref[:, d:2 * d] = (du * cx).astype(BF16)
        dz_ref[:, 2 * d:3 * d] = (du * cc).astype(BF16)
        acc_ref[0:1, :] += _colsum(dy * prev)
        acc_ref[1:2, :] += _colsum(dy * u)
        acc_ref[2:3, :] += _colsum(dy * nxt)

    return pl.pallas_call(
        body, name=name, grid=(rows // tr,),
        in_specs=[_rspec(tr, d), _rspec(tr, d, CB), _rspec(tr, d, CC), _rspec(tr, d, CX), _cspec(conv_w.shape)],
        out_specs=[_rspec(tr, 3 * d), _cspec((8, d))],
        out_shape=[jax.ShapeDtypeStruct((rows, 3 * d), BF16), jax.ShapeDtypeStruct((8, d), F32)],
        compiler_params=_params(("arbitrary",)))(dpc, z, z, z, conv_w)


def _rotary_fwd(z, cos, sin, n_lat, name):
    rows = z.shape[0]
    d = z.shape[1] // N_IN
    dk = d // HEADS
    half = dk // 2
    tr = _tile(n_lat, 256, 8)
    tr = _tile(rows - n_lat, tr, 8)
    lat_blocks, all_blocks = n_lat // tr, rows // tr
    ctx_blocks = all_blocks - lat_blocks
    scan_rows = rows + ctx_blocks * tr

    def z_block(g):
        return jnp.where(g < all_blocks, g, g - ctx_blocks)

    def in_spec(w, cb=0):
        return pl.BlockSpec((tr, w), lambda g: (z_block(g), cb))

    def out_spec(w):
        return pl.BlockSpec((tr, w), lambda g: (jnp.where(g < all_blocks, g + ctx_blocks, g - all_blocks), 0))

    def body(q_ref, k_ref, v0_ref, v1_ref, cos_ref, sin_ref, qo_ref, ko_ref, vo_ref):
        cs, sn = cos_ref[...], sin_ref[...]
        keep = jnp.where(z_block(pl.program_id(0)) < lat_blocks, 1.0, 0.0)
        for src, dst, scale in ((q_ref, qo_ref, keep), (k_ref, ko_ref, dk ** -0.5)):
            for h in range(HEADS):
                lo, mid, hi = h * dk, h * dk + half, (h + 1) * dk
                t1, t2 = src[:, lo:mid], src[:, mid:hi]
                dst[:, lo:mid] = ((t1 * cs - t2 * sn) * scale).astype(BF16)
                dst[:, mid:hi] = ((t1 * sn + t2 * cs) * scale).astype(BF16)
        vo_ref[:, 0:d] = v0_ref[...].astype(BF16)
        vo_ref[:, d:2 * d] = v1_ref[...].astype(BF16)

    return pl.pallas_call(
        body, name=name, grid=(all_blocks + ctx_blocks,),
        in_specs=[in_spec(d, CQ), in_spec(d, CK), in_spec(d, CV), in_spec(d, CV + 1), in_spec(half), in_spec(half)],
        out_specs=[out_spec(d), out_spec(d), out_spec(2 * d)],
        out_shape=[jax.ShapeDtypeStruct((scan_rows, d), BF16), jax.ShapeDtypeStruct((scan_rows, d), BF16),
                   jax.ShapeDtypeStruct((scan_rows, 2 * d), BF16)],
        compiler_params=_params(("parallel",)))(z, z, z, z, cos, sin)


def _dz_assemble(dq_f, dq_b, dk_f, dk_b, dv_f, dv_b, cos, sin, dz_conv, dz_g, dz_gc, dz_gr, n_lat, n_ctx, name):
    rows = n_lat + n_ctx
    d = dq_f.shape[1]
    dk = d // HEADS
    half = dk // 2
    tr = _tile(n_ctx, 128, 8)
    lat_blocks, ctx_blocks = n_lat // tr, n_ctx // tr

    def fmap(i):
        return (jnp.where(i < lat_blocks, i + ctx_blocks, i - lat_blocks), 0)

    def bmap(i):
        return (i + ctx_blocks, 0)

    def lmap(i):
        return (jnp.minimum(i, lat_blocks - 1), 0)

    def body(qf_ref, qb_ref, kf_ref, kb_ref, vf_ref, vb_ref, cos_ref, sin_ref, c_ref, g_ref, gc_ref, gr_ref, out_ref):
        cs, sn = cos_ref[...], sin_ref[...]
        is_lat = pl.program_id(0) < lat_blocks
        keep = jnp.where(is_lat, 1.0, 0.0)
        for fa, fb, base, scale in ((qf_ref, qb_ref, CQ * d, keep), (kf_ref, kb_ref, CK * d, dk ** -0.5)):
            for h in range(HEADS):
                lo, mid, hi = h * dk, h * dk + half, (h + 1) * dk
                d1 = fa[:, lo:mid].astype(F32) + fb[:, lo:mid].astype(F32)
                d2 = fa[:, mid:hi].astype(F32) + fb[:, mid:hi].astype(F32)
                out_ref[:, base + lo:base + mid] = ((d1 * cs + d2 * sn) * scale).astype(BF16)
                out_ref[:, base + mid:base + hi] = ((d2 * cs - d1 * sn) * scale).astype(BF16)
        out_ref[:, CV * d:CG * d] = (vf_ref[...].astype(F32) + vb_ref[...].astype(F32)).astype(BF16)
        for src, lo, hi in ((c_ref, CB * d, CQ * d), (g_ref, CG * d, CGC * d), (gc_ref, CGC * d, CGR * d),
                            (gr_ref, CGR * d, N_IN * d)):
            out_ref[:, lo:hi] = jnp.where(is_lat, src[...], jnp.zeros_like(src))

    return pl.pallas_call(
        body, name=name, grid=(rows // tr,),
        in_specs=[pl.BlockSpec((tr, d), fmap), pl.BlockSpec((tr, d), bmap), pl.BlockSpec((tr, d), fmap),
                  pl.BlockSpec((tr, d), bmap), pl.BlockSpec((tr, 2 * d), fmap), pl.BlockSpec((tr, 2 * d), bmap),
                  _rspec(tr, half), _rspec(tr, half), pl.BlockSpec((tr, 3 * d), lmap), pl.BlockSpec((tr, 2 * d), lmap),
                  pl.BlockSpec((tr, d), lmap), pl.BlockSpec((tr, d), lmap)],
        out_specs=_rspec(tr, N_IN * d), out_shape=jax.ShapeDtypeStruct((rows, N_IN * d), BF16),
        compiler_params=_params(("parallel",)))(dq_f, dq_b, dk_f, dk_b, dv_f, dv_b, cos, sin, dz_conv, dz_g, dz_gc, dz_gr)


def _gn_fwd(o, z, rows, row_off, name):
    d = z.shape[1] // N_IN
    dv = 2 * d // HEADS
    tr = _tile(row_off, 128, 8)
    off = row_off // tr

    def body(o_ref, g0_ref, g1_ref, r_ref):
        for h in range(HEADS):
            lo, hi = h * dv, (h + 1) * dv
            g_ref, glo = (g0_ref, lo) if hi <= d else (g1_ref, lo - d)
            o = o_ref[:, lo:hi]
            cen = o - jnp.mean(o, axis=-1, keepdims=True)
            on = cen * lax.rsqrt(jnp.mean(cen * cen, axis=-1, keepdims=True) + EPS)
            g = g_ref[:, glo:glo + dv]
            r_ref[:, lo:hi] = (g * _sig(g) * on).astype(BF16)

    return pl.pallas_call(
        body, name=name, grid=(rows // tr,),
        in_specs=[_rspec(tr, 2 * d, off=off), _rspec(tr, d, CG), _rspec(tr, d, CG + 1)],
        out_specs=_rspec(tr, 2 * d), out_shape=jax.ShapeDtypeStruct((rows, 2 * d), BF16),
        compiler_params=_params(("parallel",)))(o, z, z)


def _gn_bwd(dr, o, z, row_off, name):
    rows = dr.shape[0]
    d = z.shape[1] // N_IN
    dv = 2 * d // HEADS
    tr = _tile(row_off, 128, 8)
    off = row_off // tr

    def body(dr_ref, o_ref, g0_ref, g1_ref, do_ref, dg_ref):
        for h in range(HEADS):
            lo, hi = h * dv, (h + 1) * dv
            g_ref, glo = (g0_ref, lo) if hi <= d else (g1_ref, lo - d)
            o = o_ref[:, lo:hi]
            cen = o - jnp.mean(o, axis=-1, keepdims=True)
            rstd = lax.rsqrt(jnp.mean(cen * cen, axis=-1, keepdims=True) + EPS)
            on = cen * rstd
            g = g_ref[:, glo:glo + dv]
            sg = _sig(g)
            drv = dr_ref[:, lo:hi]
            dg_ref[:, lo:hi] = (drv * on * (sg * (1.0 + g * (1.0 - sg)))).astype(BF16)
            don = drv * (g * sg)
            do = rstd * (don - jnp.mean(don, axis=-1, keepdims=True)
                         - on * jnp.mean(don * on, axis=-1, keepdims=True))
            do_ref[:, lo:hi] = do.astype(BF16)

    return pl.pallas_call(
        body, name=name, grid=(rows // tr,),
        in_specs=[_rspec(tr, 2 * d), _rspec(tr, 2 * d, off=off), _rspec(tr, d, CG), _rspec(tr, d, CG + 1)],
        out_specs=[_rspec(tr, 2 * d), _rspec(tr, 2 * d)],
        out_shape=[jax.ShapeDtypeStruct((rows, 2 * d), BF16), jax.ShapeDtypeStruct((rows, 2 * d), BF16)],
        compiler_params=_params(("parallel",)))(dr, o, z, z)


def _decays(lg, rev):
    row = lax.broadcasted_iota(jnp.int32, (CHUNK, CHUNK), 0)
    col = lax.broadcasted_iota(jnp.int32, (CHUNK, CHUNK), 1)
    rel = ((col - row) if rev else (row - col)).astype(F32)
    mask = jnp.where(rel >= 0, jnp.exp(lg * jnp.maximum(rel, 0.0)), 0.0)
    r = lax.broadcasted_iota(jnp.int32, (CHUNK, 1), 0)
    rr = ((CHUNK - 1 - r) if rev else r).astype(F32)
    chunk_decay = jnp.exp(lg * jnp.full((1, 1), float(CHUNK), F32))
    return (rr, rel), mask, jnp.exp(lg * (rr + 1.0)), jnp.exp(lg * (CHUNK - 1.0 - rr)), chunk_decay


_NT = (((1,), (1,)), ((), ()))
_TN = (((0,), (0,)), ((), ()))


def _dot(a, b, dims=None):
    if dims is None:
        return jnp.dot(a, b, preferred_element_type=F32)
    return lax.dot_general(a, b, dims, preferred_element_type=F32)


def _ret_fwd(q, k, v, lgs, name):
    tt, d = q.shape
    dk, dv = d // HEADS, 2 * d // HEADS
    nc = tt // CHUNK

    def chunk_of(rev, i):
        return (nc - 1 - i) if rev else i

    def cmap(rev):
        return lambda h, i: (chunk_of(rev, i), h)

    def smap(rev):
        return lambda h, i: (chunk_of(rev, i), h, 0, 0)

    def body(lg_ref, qf_ref, kf_ref, vf_ref, qb_ref, kb_ref, vb_ref, o_ref, sf_ref, sb_ref, state_f, state_b):
        h, i = pl.program_id(0), pl.program_id(1)

        @pl.when(i == 0)
        def _():
            state_f[...] = jnp.zeros_like(state_f)
            state_b[...] = jnp.zeros_like(state_b)

        for rev, (q_ref, k_ref, v_ref, s_ref, state) in enumerate((
                (qf_ref, kf_ref, vf_ref, sf_ref, state_f), (qb_ref, kb_ref, vb_ref, sb_ref, state_b))):
            _, mask, qd, kd, cd = _decays(lg_ref[rev, h], bool(rev))
            qv, kv, vv = q_ref[...], k_ref[...], v_ref[...]
            st = state[...]
            p = _dot(qv, kv, _NT) * mask
            out = _dot(p.astype(BF16), vv) + _dot((qv * qd).astype(BF16), st.astype(BF16))
            s_ref[...] = st
            state[...] = cd * st + _dot((kv * kd).astype(BF16), vv, _TN)
            rows = pl.ds(pl.multiple_of(chunk_of(bool(rev), i) * CHUNK, CHUNK), CHUNK)
            first = (2 * i < nc - 1) if rev else (2 * i <= nc - 1)

            @pl.when(first)
            def _():
                o_ref[rows, :] = out

            @pl.when(jnp.logical_not(first))
            def _():
                o_ref[rows, :] += out

    def specs(rev):
        return [pl.BlockSpec((CHUNK, dk), cmap(rev)), pl.BlockSpec((CHUNK, dk), cmap(rev)),
                pl.BlockSpec((CHUNK, dv), cmap(rev))]

    state_shape = jax.ShapeDtypeStruct((nc, HEADS, dk, dv), F32)
    return pl.pallas_call(
        body, name=name, grid=(HEADS, nc),
        in_specs=[pl.BlockSpec(memory_space=pltpu.SMEM)] + specs(False) + specs(True),
        out_specs=[pl.BlockSpec((tt, dv), lambda h, i: (0, h)), pl.BlockSpec((None, None, dk, dv), smap(False)),
                   pl.BlockSpec((None, None, dk, dv), smap(True))],
        out_shape=[jax.ShapeDtypeStruct((tt, 2 * d), F32), state_shape, state_shape],
        scratch_shapes=[pltpu.VMEM((dk, dv), F32), pltpu.VMEM((dk, dv), F32)],
        compiler_params=_params(("parallel", "arbitrary")))(lgs, q, k, v, q, k, v)


def _ret_bwd(q, k, v, do, states_f, states_b, lgs, name):
    tt, d = q.shape
    dk, dv = d // HEADS, 2 * d // HEADS
    nc = tt // CHUNK
    lat_chunks = do.shape[0] // CHUNK
    ctx_chunks = (nc - lat_chunks) // 2

    def chunk_of(rev, i):
        return i if rev else nc - 1 - i

    def cmap(rev):
        return lambda h, i: (chunk_of(rev, i), h)

    def do_map(rev):
        return lambda h, i: (jnp.clip(chunk_of(rev, i) - ctx_chunks, 0, lat_chunks - 1), h)

    def smap(rev):
        return lambda h, i: (chunk_of(rev, i), h, 0, 0)

    def body(lg_ref, *refs):
        h = pl.program_id(0)
        dstates = refs[-2:]

        @pl.when(pl.program_id(1) == 0)
        def _():
            for rev in (0, 1):
                dstates[rev][...] = jnp.zeros_like(dstates[rev])
                refs[10 + 4 * rev + 3][...] = jnp.zeros_like(refs[10 + 4 * rev + 3])

        for rev in (0, 1):
            q_ref, k_ref, v_ref, do_ref, s_ref = refs[5 * rev:5 * rev + 5]
            dq_ref, dk_ref, dv_ref, dlg_ref = refs[10 + 4 * rev:10 + 4 * rev + 4]
            dstate = dstates[rev]
            chunk = chunk_of(bool(rev), pl.program_id(1))
            is_lat = (chunk >= ctx_chunks) & (chunk < ctx_chunks + lat_chunks)
            (rr, rel), mask, qd, kd, cd = _decays(lg_ref[rev, h], bool(rev))
            qv, kv, vv = q_ref[...], k_ref[...], v_ref[...]
            dov = jnp.where(is_lat, do_ref[...], jnp.zeros_like(do_ref))
            st, dst = s_ref[...], dstate[...]
            st_b, dst_b = st.astype(BF16), dst.astype(BF16)
            p = _dot(qv, kv, _NT) * mask
            dp = _dot(dov, vv, _NT)
            da = (dp * mask).astype(BF16)
            dq_state = _dot(dov, st_b, _NT) * qd
            dk_state = _dot(vv, dst_b, _NT) * kd
            dq_ref[...] = (_dot(da, kv) + dq_state).astype(BF16)
            dk_ref[...] = (_dot(da, qv, _TN) + dk_state).astype(BF16)
            dv_ref[...] = (_dot(p.astype(BF16), dov, _TN) + _dot((kv * kd).astype(BF16), dst_b)).astype(BF16)
            dnew = cd * dst + _dot((qv * qd).astype(BF16), dov, _TN)
            dstate[...] = dnew
            through = rr * (jnp.sum(qv.astype(F32) * dq_state, axis=1, keepdims=True)
                            - jnp.sum(kv.astype(F32) * dk_state, axis=1, keepdims=True))
            dlg_ref[...] += _total(rel * p * dp) + _total(through) + CHUNK * _total(st * dnew)

    def specs(rev):
        return [pl.BlockSpec((CHUNK, dk), cmap(rev)), pl.BlockSpec((CHUNK, dk), cmap(rev)),
                pl.BlockSpec((CHUNK, dv), cmap(rev)), pl.BlockSpec((CHUNK, dv), do_map(rev)),
                pl.BlockSpec((None, None, dk, dv), smap(rev))]

    def outs(rev):
        return [pl.BlockSpec((CHUNK, dk), cmap(rev)), pl.BlockSpec((CHUNK, dk), cmap(rev)),
                pl.BlockSpec((CHUNK, dv), cmap(rev)), pl.BlockSpec((None, 8, 128), lambda h, i: (h, 0, 0))]

    return pl.pallas_call(
        body, name=name, grid=(HEADS, nc),
        in_specs=[pl.BlockSpec(memory_space=pltpu.SMEM)] + specs(False) + specs(True),
        out_specs=outs(False) + outs(True),
        out_shape=[jax.ShapeDtypeStruct((tt, d), BF16), jax.ShapeDtypeStruct((tt, d), BF16),
                   jax.ShapeDtypeStruct((tt, 2 * d), BF16), jax.ShapeDtypeStruct((HEADS, 8, 128), F32)] * 2,
        scratch_shapes=[pltpu.VMEM((dk, dv), F32), pltpu.VMEM((dk, dv), F32)],
        compiler_params=_params(("parallel", "arbitrary")))(lgs, q, k, v, do, states_f, q, k, v, do, states_b)


def _silu(v):
    return v * _sig(v)


def _mod_proj(cs, w_mod, b_loc, dec, name):
    nb = w_mod.shape[1]

    def body(cs_ref, w_ref, b_ref, dec_ref, out_ref, lg_ref):
        out_ref[...] = jnp.dot(_silu(cs_ref[...]), w_ref[...], preferred_element_type=F32, precision=HIGHEST) + b_ref[...]
        a = dec_ref[...]
        lg_ref[...] = jnp.minimum(a, 0.0) - jnp.log1p(jnp.exp(-jnp.abs(a)))

    return pl.pallas_call(
        body, name=name,
        out_shape=[jax.ShapeDtypeStruct((16, nb), F32), jax.ShapeDtypeStruct(dec.shape, F32)],
        compiler_params=_params())(cs, w_mod, b_loc, dec)


def _mod_grad(cs, dm, w_mod, name):
    d, nb = w_mod.shape

    def body(cs_ref, dm_ref, w_ref, gw_ref, part_ref):
        dmv = dm_ref[...]
        gw_ref[...] = lax.dot_general(_silu(cs_ref[...]), dmv, _TN, preferred_element_type=F32, precision=HIGHEST)
        part_ref[...] = lax.dot_general(dmv, w_ref[...], _NT, preferred_element_type=F32, precision=HIGHEST)

    return pl.pallas_call(
        body, name=name,
        out_shape=[jax.ShapeDtypeStruct((d, nb), F32), jax.ShapeDtypeStruct((16, d), F32)],
        compiler_params=_params())(cs, dm, w_mod)


def _reduce_small(gathered, dec, n_feat, name):
    _, rows, d = gathered.shape

    def body(g_ref, dec_ref, red_ref, misc_ref):
        total = g_ref[0]
        for i in range(1, N_DEV):
            total = total + g_ref[i]
        red_ref[...] = total
        misc_ref[...] = jnp.zeros_like(misc_ref)
        misc_ref[0:1, :] = jnp.zeros((1, 128), F32) + (0.5 / n_feat) * _total(total[16:17, :])
        misc_ref[1:3, :] = total[14:16, 0:128] * _sig(-dec_ref[0:2, :])

    return pl.pallas_call(
        body, name=name,
        out_shape=[jax.ShapeDtypeStruct((rows, d), F32), jax.ShapeDtypeStruct((8, 128), F32)],
        compiler_params=_params())(gathered, dec)


def _c_ctx_grad(parts, c_ctx, name):
    d = c_ctx.shape[1]

    def body(p_ref, c_ref, out_ref):
        total = p_ref[0]
        for i in range(1, N_DEV):
            total = total + p_ref[i]
        cv = c_ref[...]
        sg = _sig(cv)
        out_ref[...] = total[8:9, :] * (sg * (1.0 + cv * (1.0 - sg)))

    return pl.pallas_call(body, name=name, out_shape=jax.ShapeDtypeStruct((1, d), F32),
                          compiler_params=_params())(parts, c_ctx)


def _adamw(parts, w, m, v, name, own=None):
    n_parts, rows, cols = parts.shape
    row_bytes = cols * (parts.dtype.itemsize * (n_parts + 1) + 7 * 4)
    tr = _tile(rows, max(16, (4 * 1024 * 1024) // row_bytes), 16 if rows % 16 == 0 else 8)
    blocks = rows // tr

    def body(*refs):
        p_ref, w_ref, m_ref, v_ref, g_ref, d_ref, mo_ref, vo_ref = refs[-8:]
        if own is None:
            g = p_ref[0].astype(F32)
        else:
            g = refs[1][...].astype(F32) + p_ref[0].astype(F32)
        for i in range(1, n_parts):
            g = g + p_ref[i].astype(F32)
        m2 = ADAM_B1 * m_ref[...] + (1.0 - ADAM_B1) * g
        v2 = ADAM_B2 * v_ref[...] + (1.0 - ADAM_B2) * jnp.square(g)
        m_hat = m2 / (1.0 - ADAM_B1 ** ADAM_STEP)
        v_hat = v2 / (1.0 - ADAM_B2 ** ADAM_STEP)
        g_ref[...] = g
        d_ref[...] = -ADAM_LR * (m_hat / (jnp.sqrt(v_hat) + ADAM_EPS) + ADAM_WD * w_ref[...])
        mo_ref[...] = m2
        vo_ref[...] = v2

    out_shape = [jax.ShapeDtypeStruct((rows, cols), F32)] * 4
    if own is None:
        spec = _rspec(tr, cols)
        return pl.pallas_call(
            body, name=name, grid=(blocks,),
            in_specs=[pl.BlockSpec((n_parts, tr, cols), lambda i: (0, i, 0)), spec, spec, spec],
            out_specs=[spec] * 4, out_shape=out_shape, compiler_params=_params(("parallel",)))(parts, w, m, v)
    g_full, axis, me = own

    def own_map(i, me_ref):
        return (me_ref[0] * blocks + i, 0) if axis == 0 else (i, me_ref[0])

    spec = pl.BlockSpec((tr, cols), lambda i, me_ref: (i, 0))
    return pl.pallas_call(
        body, name=name, out_shape=out_shape,
        grid_spec=pltpu.PrefetchScalarGridSpec(
            num_scalar_prefetch=1, grid=(blocks,),
            in_specs=[pl.BlockSpec((tr, cols), own_map),
                      pl.BlockSpec((n_parts, tr, cols), lambda i, me_ref: (0, i, 0)), spec, spec, spec],
            out_specs=[spec] * 4),
        compiler_params=_params(("parallel",)))(me.reshape(1), g_full, parts, w, m, v)


def _rope_tables(pos, dk):
    half = dk // 2
    inv_freq = 1.0 / (ROPE_BASE ** jnp.linspace(0.0, 1.0, half, dtype=F32))
    ang = pos[:, None] * inv_freq[None, :]
    return jnp.cos(ang), jnp.sin(ang)


def _pad_lanes(v, width):
    return jnp.pad(v, ((0, 0), (0, width - v.shape[1])))


def kernel(x, c, ctx, c_ctx, w_mod, b_mod, norm1_g, w_in, conv_w, w_conv_out, ret_decay_fwd, ret_decay_bwd, w_ret_out, w_o, norm2_g, w_ff1, w_ff2, final_g, loss_target, m_c_ctx, m_w_mod, m_b_mod, m_norm1_g, m_w_in, m_conv_w, m_w_conv_out, m_ret_decay_fwd, m_ret_decay_bwd, m_w_ret_out, m_w_o, m_norm2_g, m_w_ff1, m_w_ff2, m_final_g, v_c_ctx, v_w_mod, v_b_mod, v_norm1_g, v_w_in, v_conv_w, v_w_conv_out, v_ret_decay_fwd, v_ret_decay_bwd, v_w_ret_out, v_w_o, v_norm2_g, v_w_ff1, v_w_ff2, v_final_g):
    n_lat, d = x.shape[1], x.shape[2]
    n_ctx = ctx.shape[1]
    assert n_ctx % CHUNK == 0 and n_lat % CHUNK == 0 and n_lat % GRID_W == 0
    dk = d // HEADS
    nb = w_mod.shape[2]
    me = 4 * lax.axis_index("x") + 2 * lax.axis_index("y") + lax.axis_index("c")
    xl, ctxl, target = x[0], ctx[0], loss_target[0]

    w_in_f = _allgather(w_in[0].astype(BF16), 1, "ag_w_in")
    conv_w_f = _small_allgather(conv_w[0], "ag_conv_w").transpose(1, 0, 2).reshape(3, d)

    c_all = _small_allgather(c, "ag_c").reshape(N_DEV, d)
    cs = jnp.concatenate([c_all, c_ctx[None], jnp.zeros((7, d), F32)], axis=0)
    dec = jnp.pad(jnp.concatenate([ret_decay_fwd, ret_decay_bwd], axis=0), ((0, 6), (0, 128 - HEADS)))
    b_loc = lax.dynamic_slice(b_mod, (0, me * nb), (1, nb))
    modp, lgs = _mod_proj(cs, w_mod[0], b_loc, dec, "mod_proj")
    modp_all = _small_allgather(modp, "ag_mod")
    mod_l = lax.dynamic_index_in_dim(modp_all, me, axis=1, keepdims=False).reshape(N_MOD, d)
    mod_c = modp_all[:, 8, :].reshape(N_MOD, d)
    lgs = lgs[0:2, :HEADS]
    zero_row = jnp.zeros((1, d), F32)
    vec1_l = jnp.concatenate([norm1_g, mod_l[0:1], mod_l[1:2], zero_row], axis=0)
    vec1_c = jnp.concatenate([norm1_g, mod_c[0:1], mod_c[1:2], zero_row], axis=0)
    vec2 = jnp.concatenate([norm2_g, mod_l[3:4], mod_l[4:5], mod_l[2:3]], axis=0)
    vec3 = jnp.concatenate([mod_l[5:6], final_g[None]], axis=0)

    later, after = {}, (w_in_f, conv_w_f, modp_all)
    for wname, w, axis in (("w_conv_out", w_conv_out, 0), ("w_ret_out", w_ret_out, 0), ("w_o", w_o, 0),
                           ("w_ff1", w_ff1, 1), ("w_ff2", w_ff2, 0)):
        later[wname], token = _exchange_start(_place_shard(w[0], axis, me, "place_" + wname), axis, True,
                                              "ag_" + wname + "_start", after=after)
        after = (token,)

    a_all = jnp.concatenate([_modulate(xl, vec1_l, "modulate1"), _modulate(ctxl, vec1_c, "modulate1_ctx")], axis=0)
    z = _mm(a_all, w_in_f, "nn", [F32], "proj_in", tn=2048, dep=token)
    pc = _conv_fwd(z, conv_w_f, n_lat, "conv_fwd")
    pos = jnp.concatenate([n_ctx + jnp.arange(n_lat, dtype=F32), jnp.arange(n_ctx, dtype=F32)])
    cos, sin = _rope_tables(pos, dk)
    q_s, k_s, v_s = _rotary_fwd(z, cos, sin, n_lat, "rotary_fwd")
    o, st_f, st_b = _ret_fwd(q_s, k_s, v_s, lgs, "ret_fwd")
    r = _gn_fwd(o, z, n_lat, n_ctx, "gn_fwd")
    w_conv_out_f = _exchange_wait(later["w_conv_out"], r, "ag_w_conv_out_wait")
    w_ret_out_f = _exchange_wait(later["w_ret_out"], r, "ag_w_ret_out_wait")
    w_o_f = _exchange_wait(later["w_o"], r, "ag_w_o_wait")
    y_conv = _mm(pc, w_conv_out_f, "nn", [F32], "proj_conv_out", tn=2048)
    tn_d = _tile(d, 1024, 128)
    gate_offs = (CGC * d // tn_d, CGR * d // tn_d)

    def merge(acc, yc, gc, gr):
        return acc, _sig(gc) * yc + _sig(gr) * acc

    y_ret, mg = _mm(r, w_ret_out_f, "nn", [F32, BF16], "proj_ret_out", tm=512, tn=tn_d,
                    extras=[(y_conv, 0), (z, gate_offs[0]), (z, gate_offs[1])], epilogue=merge)
    y_l = _mm(mg, w_o_f, "nn", [F32], "proj_o", tn=2048)
    x1, a2 = _resid_modulate(xl, y_l, vec2, "resid_modulate2")

    def sqrelu(acc):
        return acc, jnp.square(jnp.maximum(acc, 0.0))

    w_ff1_f = _exchange_wait(later["w_ff1"], a2, "ag_w_ff1_wait")
    hff, s = _mm(a2, w_ff1_f, "nn", [BF16, BF16], "ff1", epilogue=sqrelu)
    w_ff2_f = _exchange_wait(later["w_ff2"], s, "ag_w_ff2_wait")
    f = _mm(s, w_ff2_f, "nn", [F32], "ff2")
    dx2, df, acc3 = _loss_head(x1, f, target, vec3, "loss_head")

    def d_sqrelu(acc, h):
        return (acc * (2.0 * jnp.maximum(h.astype(F32), 0.0)),)

    dh = _mm(df, w_ff2_f, "nt", [BF16], "ff2_dx", extras=[(hff, 0)], epilogue=d_sqrelu)
    sent = {}
    sent["w_ff2"], token = _exchange_start(_mm(s, df, "tn", [BF16], "ff2_dw"), 0, False, "rs_w_ff2_start")
    da2 = _mm(dh, w_ff1_f, "nt", [F32], "ff1_dx", dep=token)
    sent["w_ff1"], token = _exchange_start(_mm(a2, dh, "tn", [BF16], "ff1_dw"), 1, False, "rs_w_ff1_start")
    dx1, dyl, acc2 = _modulate_bwd(da2, x1, vec2, "modulate2_bwd", dx_in=dx2, y=y_l)

    def d_merge(acc, yc, yr, gc, gr):
        sc, sr = _sig(gc), _sig(gr)
        return acc * sc, acc * sr, acc * yc * (sc * (1.0 - sc)), acc * yr * (sr * (1.0 - sr))

    dyc, dyr, dgc, dgr = _mm(dyl, w_o_f, "nt", [BF16] * 4, "proj_o_dx", tm=512, tn=tn_d,
                             extras=[(y_conv, 0), (y_ret, 0), (z, gate_offs[0]), (z, gate_offs[1])], epilogue=d_merge,
                             dep=token)
    sent["w_o"], token = _exchange_start(_mm(mg, dyl, "tn", [BF16], "proj_o_dw"), 0, False, "rs_w_o_start")
    dpc = _mm(dyc, w_conv_out_f, "nt", [F32], "proj_conv_out_dx", tn=2048, dep=token)
    sent["w_conv_out"], token = _exchange_start(_mm(pc, dyc, "tn", [BF16], "proj_conv_out_dw"), 0, False,
                                                "rs_w_conv_out_start")
    dz_conv, acc_conv = _conv_bwd(dpc, z, conv_w_f, "conv_bwd")
    dr = _mm(dyr, w_ret_out_f, "nt", [F32], "proj_ret_out_dx", dep=token)
    sent["w_ret_out"], token = _exchange_start(_mm(r, dyr, "tn", [BF16], "proj_ret_out_dw"), 0, False,
                                               "rs_w_ret_out_start")
    do, dz_g = _gn_bwd(dr, o, z, n_ctx, "gn_bwd")
    dq_f, dk_f, dv_f, dlg_f, dq_b, dk_b, dv_b, dlg_b = _ret_bwd(q_s, k_s, v_s, do, st_f, st_b, lgs, "ret_bwd")
    dz = _dz_assemble(dq_f, dq_b, dk_f, dk_b, dv_f, dv_b, cos, sin, dz_conv, dz_g, dgc, dgr, n_lat, n_ctx,
                      "dz_assemble")
    g_in = _mm(a_all, dz, "tn", [BF16], "proj_in_dw", tn=2048, tk=(n_lat + n_ctx) // 4, dep=token)
    sent["w_in"], token = _exchange_start(g_in, 1, False, "rs_w_in_start")
    da_all = _mm(dz, w_in_f, "nt", [F32], "proj_in_dx", tn=2048, dep=token)
    grad_x, acc1 = _modulate_bwd(da_all, xl, vec1_l, "modulate1_bwd", dx_in=dx1)
    _, acc1c = _modulate_bwd(da_all, ctxl, vec1_c, "modulate1_ctx_bwd", da_off=n_lat)

    lane_pad = functools.partial(_pad_lanes, width=d)
    packet = jnp.concatenate([
        acc1[2:3] + acc1c[2:3], acc2[2:3], acc3[1:2],
        acc1[0:1], acc1[1:2], acc2[3:4], acc2[0:1], acc2[1:2], acc3[0:1],
        acc1c[0:1], acc1c[1:2],
        acc_conv[0:3],
        lane_pad(dlg_f[:, 0, 0][None]), lane_pad(dlg_b[:, 0, 0][None]),
        acc3[2:3],
        jnp.zeros((7, d), F32)], axis=0)
    gathered = _small_allgather(packet, "ag_small")
    red, misc = _reduce_small(gathered, dec, d, "reduce_small")
    dmod_ctx = jnp.concatenate([red[9], red[10], jnp.zeros((4 * d,), F32)])
    dmod_all = jnp.concatenate([gathered[:, 3:9, :].reshape(N_DEV, N_MOD * d), dmod_ctx[None]], axis=0)
    dm = jnp.pad(lax.dynamic_slice(dmod_all, (0, me * nb), (N_DEV + 1, nb)), ((0, 7), (0, 0)))
    g_mod, c_part = _mod_grad(cs, dm, w_mod[0], "mod_grad")
    g_c_ctx = _c_ctx_grad(_small_allgather(c_part, "ag_c_ctx"), c_ctx[None], "c_ctx_grad")
    g_b_mod = red[3:9].reshape(1, N_MOD * d) + dmod_ctx[None]
    g_conv_w = lax.dynamic_slice(red[11:14], (0, me * (d // N_DEV)), (3, d // N_DEV))

    res = {"w_mod": _adamw(g_mod[None], w_mod[0], m_w_mod[0], v_w_mod[0], "adamw_w_mod")}
    after = res["w_mod"][0]
    for wname, axis, w, m, v in (("w_ff2", 0, w_ff2, m_w_ff2, v_w_ff2), ("w_ff1", 1, w_ff1, m_w_ff1, v_w_ff1),
                                 ("w_o", 0, w_o, m_w_o, v_w_o), ("w_conv_out", 0, w_conv_out, m_w_conv_out, v_w_conv_out),
                                 ("w_ret_out", 0, w_ret_out, m_w_ret_out, v_w_ret_out),
                                 ("w_in", 1, w_in, m_w_in, v_w_in)):
        g_full, parts = _exchange_wait(sent[wname], after, "rs_" + wname + "_wait")
        res[wname] = _adamw(parts, w[0], m[0], v[0], "adamw_" + wname, own=(g_full, axis, me))
        after = res[wname][0]
    res = {k: tuple(t[None] for t in val) for k, val in res.items()}

    small = [("c_ctx", g_c_ctx, c_ctx, m_c_ctx, v_c_ctx), ("b_mod", g_b_mod, b_mod, m_b_mod, v_b_mod),
             ("norm1_g", red[0:1], norm1_g, m_norm1_g, v_norm1_g), ("conv_w", g_conv_w, conv_w, m_conv_w, v_conv_w),
             ("ret_decay_fwd", misc[1:2, :HEADS], ret_decay_fwd, m_ret_decay_fwd, v_ret_decay_fwd),
             ("ret_decay_bwd", misc[2:3, :HEADS], ret_decay_bwd, m_ret_decay_bwd, v_ret_decay_bwd),
             ("norm2_g", red[1:2], norm2_g, m_norm2_g, v_norm2_g), ("final_g", red[2:3], final_g, m_final_g, v_final_g)]

    def flat(t):
        t = t.reshape(-1)
        return jnp.pad(t, (0, (-t.shape[0]) % 1024))

    packed = [jnp.concatenate([flat(item[j]) for item in small]).reshape(-1, 128) for j in range(1, 5)]
    outs = _adamw(packed[0][None], packed[1], packed[2], packed[3], "adamw_small")
    start = 0
    for name, _, w, _, _ in small:
        size = w.size
        res[name] = tuple(o.reshape(-1)[start:start + size].reshape(w.shape) for o in outs)
        start += size + (-size) % 1024

    order = ["c_ctx", "w_mod", "b_mod", "norm1_g", "w_in", "conv_w", "w_conv_out", "ret_decay_fwd", "ret_decay_bwd",
             "w_ret_out", "w_o", "norm2_g", "w_ff1", "w_ff2", "final_g"]
    loss = misc[0, 0]
    return (loss, grad_x[None], *[res[n][0] for n in order], *[res[n][1] for n in order],
            *[res[n][2] for n in order], *[res[n][3] for n in order])
```

```python
import functools

import jax
import jax.numpy as jnp
from jax import lax
from jax.experimental import pallas as pl
from jax.experimental.pallas import tpu as pltpu

F32 = jnp.float32
BF16 = jnp.bfloat16
MESH = pl.DeviceIdType.MESH

N_DEV = 8
HEADS = 8
N_MOD = 6
N_IN = 11
GRID_W = 64
CHUNK = 256
ROPE_BASE = 10000.0
EPS = 1e-6
ADAM_LR, ADAM_B1, ADAM_B2, ADAM_EPS, ADAM_WD, ADAM_STEP = 0.001, 0.9, 0.999, 1e-08, 0.01, 10
VMEM_LIMIT = 56 * 1024 * 1024
HIGHEST = lax.Precision.HIGHEST
CB, CC, CX, CQ, CK, CV, CG, CGC, CGR = 0, 1, 2, 3, 4, 5, 7, 9, 10


def _tile(n, target, mult):
    t = (min(target, n) // mult) * mult
    while t >= mult:
        if n % t == 0:
            return t
        t -= mult
    return n


def _params(sem=None):
    return pltpu.CompilerParams(dimension_semantics=sem, vmem_limit_bytes=VMEM_LIMIT)


def _sig(v):
    return 1.0 / (1.0 + jnp.exp(-v))


def _coords():
    return lax.axis_index("x"), lax.axis_index("y"), lax.axis_index("c")


def _flip(p, m):
    return tuple(1 - v if (m >> s) & 1 else v for v, s in zip(p, (2, 1, 0)))


def _index(p):
    return 4 * p[0] + 2 * p[1] + p[2]


def _small_allgather(x, name):
    r, n = x.shape

    def body(x_ref, out_ref, send_sems, recv_sems):
        me = _coords()
        out_ref[pl.ds(_index(me), 1)] = x_ref[...][None]
        sent = []
        for m in range(1, N_DEV):
            cp = pltpu.make_async_remote_copy(
                src_ref=x_ref, dst_ref=out_ref.at[_index(me)], send_sem=send_sems.at[m - 1],
                recv_sem=recv_sems.at[m - 1], device_id=_flip(me, m), device_id_type=MESH)
            cp.start()
            sent.append(cp)
        for m in range(1, N_DEV):
            pltpu.make_async_remote_copy(
                src_ref=x_ref, dst_ref=out_ref.at[_index(_flip(me, m))], send_sem=send_sems.at[m - 1],
                recv_sem=recv_sems.at[m - 1], device_id=_flip(me, m), device_id_type=MESH).wait_recv()
        for cp in sent:
            cp.wait_send()

    return pl.pallas_call(
        body, name=name, out_shape=jax.ShapeDtypeStruct((N_DEV, r, n), x.dtype),
        in_specs=[pl.BlockSpec(memory_space=pltpu.VMEM)], out_specs=pl.BlockSpec(memory_space=pltpu.VMEM),
        scratch_shapes=[pltpu.SemaphoreType.DMA((N_DEV - 1,)), pltpu.SemaphoreType.DMA((N_DEV - 1,))],
    )(x)


def _window(ref, j, r, c, axis):
    if axis == 0:
        return ref.at[pl.ds(pl.multiple_of(j * r, 8), r), :]
    return ref.at[:, pl.ds(pl.multiple_of(j * c, 128), c)]


def _allgather(x, axis, name):
    r, c = x.shape
    full = (N_DEV * r, c) if axis == 0 else (r, N_DEV * c)

    def body(x_ref, out_ref, send_sems, recv_sems, local_sem):
        me = _coords()
        sibling = _flip(me, 1)
        chips = [4, 2, 6]

        def copy(k, block, to, src=None):
            dst = _window(out_ref, _index(block), r, c, axis)
            return pltpu.make_async_remote_copy(
                src_ref=dst if src is None else src, dst_ref=dst, send_sem=send_sems.at[k],
                recv_sem=recv_sems.at[k], device_id=to, device_id_type=MESH)

        mine = pltpu.make_async_copy(x_ref, _window(out_ref, _index(me), r, c, axis), local_sem)
        mine.start()
        first = [copy(0, me, sibling, src=x_ref)]
        first += [copy(1 + j, me, _flip(me, m), src=x_ref) for j, m in enumerate(chips)]
        for cp in first:
            cp.start()
        passed = [copy(4 + j, _flip(me, m), sibling) for j, m in enumerate(chips)]
        for j, m in enumerate(chips):
            copy(1 + j, _flip(me, m), me).wait_recv()
            passed[j].start()
        copy(0, sibling, me).wait_recv()
        for j, m in enumerate(chips):
            copy(4 + j, _flip(sibling, m), me).wait_recv()
        for cp in first + passed:
            cp.wait_send()
        mine.wait()

    return pl.pallas_call(
        body, name=name, out_shape=jax.ShapeDtypeStruct(full, x.dtype),
        in_specs=[pl.BlockSpec(memory_space=pl.ANY)], out_specs=pl.BlockSpec(memory_space=pl.ANY),
        scratch_shapes=[pltpu.SemaphoreType.DMA((N_DEV - 1,)), pltpu.SemaphoreType.DMA((N_DEV - 1,)),
                        pltpu.SemaphoreType.DMA(())],
    )(x)


_HBM = pl.BlockSpec(memory_space=pltpu.HBM)
_SEM = pl.BlockSpec(memory_space=pltpu.SEMAPHORE)
_ANY = pl.BlockSpec(memory_space=pl.ANY)
_EFFECT = pltpu.SideEffectType.DATAFLOW_SIDE_EFFECTING


def _place_shard(w, axis, me, name):
    r, c = w.shape
    full = (N_DEV * r, c) if axis == 0 else (r, N_DEV * c)
    tr = _tile(r, max(16, (2 * 1024 * 1024) // (c * 4)), 16)
    blocks = r // tr

    def out_map(i, me_ref):
        return (me_ref[0] * blocks + i, 0) if axis == 0 else (i, me_ref[0])

    def body(me_ref, w_ref, out_ref):
        out_ref[...] = w_ref[...].astype(BF16)

    return pl.pallas_call(
        body, name=name, out_shape=jax.ShapeDtypeStruct(full, BF16),
        grid_spec=pltpu.PrefetchScalarGridSpec(
            num_scalar_prefetch=1, grid=(blocks,), in_specs=[pl.BlockSpec((tr, c), lambda i, me_ref: (i, 0))],
            out_specs=pl.BlockSpec((tr, c), out_map)),
        compiler_params=_params(("parallel",)))(me.reshape(1), w)


def _exchange_start(src, axis, gather, name, after=()):
    r, c = (src.shape[0] // N_DEV, src.shape[1]) if axis == 0 else (src.shape[0], src.shape[1] // N_DEV)
    n_hbm = 1 if gather else 2
    n_after = len(after)

    def body(*refs):
        src_ref, land_ref = refs[0], refs[n_hbm - 1]
        send_sems, recv_sems = refs[n_hbm + n_after:n_hbm + n_after + 2]
        token = refs[-1]
        me = _coords()
        for m in range(1, N_DEV):
            peer = _flip(me, m)
            if gather:
                mine = theirs = _window(land_ref, _index(me), r, c, axis)
            else:
                mine, theirs = _window(src_ref, _index(peer), r, c, axis), land_ref.at[m - 1]
            pltpu.make_async_remote_copy(
                src_ref=mine, dst_ref=theirs, send_sem=send_sems.at[m - 1], recv_sem=recv_sems.at[m - 1],
                device_id=peer, device_id_type=MESH).start()
        token[...] = jnp.zeros_like(token)

    hbm = [pltpu.with_memory_space_constraint(src, pltpu.HBM)]
    if not gather:
        hbm.append(pltpu.with_memory_space_constraint(lax.empty((N_DEV - 1, r, c), src.dtype), pltpu.HBM))
    outs = pl.pallas_call(
        body, name=name,
        out_shape=(pltpu.SemaphoreType.DMA((N_DEV - 1,)), pltpu.SemaphoreType.DMA((N_DEV - 1,)),
                   *[pltpu.HBM(t.shape, t.dtype) for t in hbm], jax.ShapeDtypeStruct((8, 128), F32)),
        in_specs=[_HBM] * n_hbm + [_ANY] * n_after,
        out_specs=(_SEM, _SEM, *[_HBM] * n_hbm, pl.BlockSpec(memory_space=pltpu.VMEM)),
        input_output_aliases={i: 2 + i for i in range(n_hbm)},
        compiler_params=pltpu.CompilerParams(has_side_effects=_EFFECT),
    )(*hbm, *after)
    return (outs[:2], outs[2:2 + n_hbm], (axis, gather, r, c)), outs[-1]


def _exchange_wait(handle, after, name):
    (send_sems, recv_sems), hbm, (axis, gather, r, c) = handle
    n_hbm = len(hbm)

    def body(*refs):
        src_ref, land_ref = refs[0], refs[n_hbm - 1]
        send_sems, recv_sems = refs[n_hbm:n_hbm + 2]
        me = _coords()
        for m in range(1, N_DEV):
            peer = _flip(me, m)
            if gather:
                mine, theirs = _window(land_ref, _index(me), r, c, axis), _window(land_ref, _index(peer), r, c, axis)
            else:
                mine, theirs = _window(src_ref, _index(peer), r, c, axis), land_ref.at[m - 1]
            copy = pltpu.make_async_remote_copy(
                src_ref=mine, dst_ref=theirs, send_sem=send_sems.at[m - 1], recv_sem=recv_sems.at[m - 1],
                device_id=peer, device_id_type=MESH)
            copy.wait_send()
            copy.wait_recv()

    outs = pl.pallas_call(
        body, name=name, out_shape=tuple(pltpu.HBM(t.shape, t.dtype) for t in hbm),
        in_specs=[_HBM] * n_hbm + [_SEM, _SEM, _ANY], out_specs=tuple([_HBM] * n_hbm),
        input_output_aliases={i: i for i in range(n_hbm)},
        compiler_params=pltpu.CompilerParams(has_side_effects=_EFFECT),
    )(*hbm, send_sems, recv_sems, after)
    return outs[0] if gather else tuple(outs)


def _mm(a, b, mode, out_dtypes, name, tm=1024, tn=1024, tk=2048, extras=(), epilogue=None, dep=None):
    if mode == "nn":
        (m, k), n = a.shape, b.shape[1]
    elif mode == "nt":
        (m, k), n = a.shape, b.shape[0]
    else:
        (k, m), n = a.shape, b.shape[1]
    tm, tn = _tile(m, tm, 8), _tile(n, tn, 128)
    tk = _tile(k, tk, 16 if mode == "tn" else 128)
    nk = k // tk
    swap = nk == 1 and (k * n + (n // tn) * m * k) < (m * k + (m // tm) * k * n)

    def ij(p, q):
        return (q, p) if swap else (p, q)

    def spec(shape, fn):
        return pl.BlockSpec(shape, lambda p, q, kk: fn(*ij(p, q), kk))

    a_spec = spec((tk, tm), lambda i, j, kk: (kk, i)) if mode == "tn" else spec((tm, tk), lambda i, j, kk: (i, kk))
    b_spec = spec((tn, tk), lambda i, j, kk: (j, kk)) if mode == "nt" else spec((tk, tn), lambda i, j, kk: (kk, j))
    dims = {"nn": (((1,), (0,)), ((), ())), "nt": (((1,), (1,)), ((), ())), "tn": (((0,), (0,)), ((), ()))}[mode]
    ex_specs = [spec((tm, tn), functools.partial(lambda i, j, kk, off: (i, j + off), off=off)) for _, off in extras]
    deps = [] if dep is None else [dep]
    dep_specs = [pl.BlockSpec(dep.shape, lambda p, q, kk: (0, 0))] if deps else []
    n_ex, n_out = len(extras), len(out_dtypes)
    n_in = 2 + n_ex + len(deps)

    def body(*refs):
        a_ref, b_ref = refs[0], refs[1]
        ex_refs = refs[2:2 + n_ex]
        out_refs = refs[n_in:n_in + n_out]

        def product():
            return lax.dot_general(a_ref[...], b_ref[...], dims, preferred_element_type=F32)

        def finish(res):
            res = epilogue(res, *[e[...] for e in ex_refs]) if epilogue is not None else (res,)
            for o_ref, val in zip(out_refs, res):
                o_ref[...] = val.astype(o_ref.dtype)

        if nk == 1:
            finish(product())
            return
        acc = refs[-1]
        kk = pl.program_id(2)

        @pl.when(kk == 0)
        def _():
            acc[...] = product()

        @pl.when((kk > 0) & (kk < nk - 1))
        def _():
            acc[...] += product()

        @pl.when(kk == nk - 1)
        def _():
            finish(acc[...] + product())

    outs = pl.pallas_call(
        body, name=name, grid=(*ij(m // tm, n // tn), nk),
        in_specs=[a_spec, b_spec] + ex_specs + dep_specs,
        out_specs=[spec((tm, tn), lambda i, j, kk: (i, j)) for _ in out_dtypes],
        out_shape=[jax.ShapeDtypeStruct((m, n), dt) for dt in out_dtypes],
        scratch_shapes=[pltpu.VMEM((tm, tn), F32)] if nk > 1 else [],
        compiler_params=_params(("parallel", "parallel", "arbitrary")),
    )(a, b, *[e for e, _ in extras], *deps)
    return outs if n_out > 1 else outs[0]


def _gather_project(a, w_full, order, name, tm=768):
    m, k = a.shape
    c = w_full.shape[1] // N_DEV
    tm = _tile(m, tm, 8)
    n_i = m // tm
    chips = [4, 2, 6]

    def body(order_ref, a_ref, w_in_ref, w_ref, z_ref, wbuf, send_sems, recv_sems, load_sem):
        p, i = pl.program_id(0), pl.program_id(1)
        me = _coords()
        sibling = _flip(me, 1)

        def copy(s, block, to):
            win = _window(w_ref, _index(block), k, c, 1)
            return pltpu.make_async_remote_copy(src_ref=win, dst_ref=win, send_sem=send_sems.at[s],
                                                recv_sem=recv_sems.at[s], device_id=to, device_id_type=MESH)

        first = [copy(0, me, sibling)] + [copy(1 + j, me, _flip(me, mm)) for j, mm in enumerate(chips)]
        passed = [copy(4 + j, _flip(me, mm), sibling) for j, mm in enumerate(chips)]

        @pl.when((p == 0) & (i == 0))
        def _():
            for cp in first:
                cp.start()

        arrivals = [(0, None)] + [x for j in range(3) for x in ((1 + j, passed[j]), (4 + j, None))]
        for piece, (sem, forward) in enumerate(arrivals, start=1):
            @pl.when((p == piece) & (i == 0))
            def _(sem=sem, forward=forward, piece=piece):
                src_block = (sibling if piece == 1 else
                             _flip(me if piece % 2 == 0 else sibling, chips[(piece - 2) // 2]))
                copy(sem, src_block, me).wait_recv()
                if forward is not None:
                    forward.start()

        @pl.when(i == 0)
        def _():
            load = pltpu.make_async_copy(_window(w_ref, order_ref[p], k, c, 1), wbuf, load_sem)
            load.start()
            load.wait()

        z_ref[...] = jnp.dot(a_ref[...], wbuf[...], preferred_element_type=F32)

        @pl.when((p == N_DEV - 1) & (i == n_i - 1))
        def _():
            for cp in first + passed:
                cp.wait_send()

    return pl.pallas_call(
        body, name=name,
        out_shape=[jax.ShapeDtypeStruct(w_full.shape, w_full.dtype), jax.ShapeDtypeStruct((m, N_DEV * c), F32)],
        grid_spec=pltpu.PrefetchScalarGridSpec(
            num_scalar_prefetch=1, grid=(N_DEV, n_i),
            in_specs=[pl.BlockSpec((tm, k), lambda p, i, order_ref: (i, 0)), pl.BlockSpec(memory_space=pl.ANY)],
            out_specs=[pl.BlockSpec(memory_space=pl.ANY),
                       pl.BlockSpec((tm, c), lambda p, i, order_ref: (i, order_ref[p]))],
            scratch_shapes=[pltpu.VMEM((k, c), w_full.dtype), pltpu.SemaphoreType.DMA((N_DEV - 1,)),
                            pltpu.SemaphoreType.DMA((N_DEV - 1,)), pltpu.SemaphoreType.DMA(())]),
        input_output_aliases={2: 0},
        compiler_params=_params(("arbitrary", "arbitrary")))(order, a, w_full)


def _rspec(tr, w, cb=0, off=0):
    return pl.BlockSpec((tr, w), lambda i: (i + off, cb))


def _cspec(shape):
    return pl.BlockSpec(shape, lambda i: (0,) * len(shape))


def _rms(xf):
    rstd = lax.rsqrt(jnp.mean(xf * xf, axis=-1, keepdims=True) + EPS)
    return xf * rstd, rstd


def _rms_bwd(dn, n, rstd):
    return rstd * (dn - n * jnp.mean(dn * n, axis=-1, keepdims=True))


def _colsum(v):
    return jnp.sum(v, axis=0, keepdims=True)


def _total(v):
    return jnp.sum(jnp.sum(v, axis=1, keepdims=True), axis=0, keepdims=True)


def _modulate(x, vec, name):
    rows, d = x.shape
    tr = _tile(rows, 256, 8)

    def body(x_ref, v_ref, a_ref):
        n, _ = _rms(x_ref[...])
        a_ref[...] = ((n * v_ref[0:1, :]) * (1.0 + v_ref[2:3, :]) + v_ref[1:2, :]).astype(BF16)

    return pl.pallas_call(
        body, name=name, grid=(rows // tr,), in_specs=[_rspec(tr, d), _cspec(vec.shape)],
        out_specs=_rspec(tr, d), out_shape=jax.ShapeDtypeStruct((rows, d), BF16),
        compiler_params=_params(("parallel",)))(x, vec)


def _resid_modulate(x, y, vec, name):
    rows, d = x.shape
    tr = _tile(rows, 256, 8)

    def body(x_ref, y_ref, v_ref, x1_ref, a_ref):
        x1 = x_ref[...] + v_ref[3:4, :] * y_ref[...]
        x1_ref[...] = x1
        n, _ = _rms(x1)
        a_ref[...] = ((n * v_ref[0:1, :]) * (1.0 + v_ref[2:3, :]) + v_ref[1:2, :]).astype(BF16)

    return pl.pallas_call(
        body, name=name, grid=(rows // tr,), in_specs=[_rspec(tr, d), _rspec(tr, d), _cspec(vec.shape)],
        out_specs=[_rspec(tr, d), _rspec(tr, d)],
        out_shape=[jax.ShapeDtypeStruct((rows, d), F32), jax.ShapeDtypeStruct((rows, d), BF16)],
        compiler_params=_params(("parallel",)))(x, y, vec)


def _loss_head(x1, f, target, vec, name):
    rows, d = x1.shape
    tr = _tile(rows, 256, 8)

    def body(x1_ref, f_ref, t_ref, v_ref, dx2_ref, df_ref, acc_ref):
        @pl.when(pl.program_id(0) == 0)
        def _():
            acc_ref[...] = jnp.zeros_like(acc_ref)

        gate, gain = v_ref[0:1, :], v_ref[1:2, :]
        fv = f_ref[...]
        n, rstd = _rms(x1_ref[...] + gate * fv)
        err = n * gain - t_ref[...]
        dy = err / d
        dx2 = _rms_bwd(dy * gain, n, rstd)
        dx2_ref[...] = dx2
        df_ref[...] = (dx2 * gate).astype(BF16)
        acc_ref[0:1, :] += _colsum(dx2 * fv)
        acc_ref[1:2, :] += _colsum(dy * n)
        acc_ref[2:3, :] += _colsum(err * err)

    return pl.pallas_call(
        body, name=name, grid=(rows // tr,),
        in_specs=[_rspec(tr, d), _rspec(tr, d), _rspec(tr, d), _cspec(vec.shape)],
        out_specs=[_rspec(tr, d), _rspec(tr, d), _cspec((8, d))],
        out_shape=[jax.ShapeDtypeStruct((rows, d), F32), jax.ShapeDtypeStruct((rows, d), BF16),
                   jax.ShapeDtypeStruct((8, d), F32)],
        compiler_params=_params(("arbitrary",)))(x1, f, target, vec)


def _modulate_bwd(da, x, vec, name, da_off=0, dx_in=None, y=None):
    rows, d = x.shape
    tr = _tile(rows, 256, 8)
    off = da_off // tr
    has_dx, has_y = dx_in is not None, y is not None

    def body(*refs):
        da_ref, x_ref, v_ref = refs[0], refs[1], refs[2]
        pos = 3
        dxin_ref = refs[pos] if has_dx else None
        pos += has_dx
        y_ref = refs[pos] if has_y else None
        pos += has_y
        dx_ref = refs[pos]
        dy_ref = refs[pos + 1] if has_y else None
        acc_ref = refs[-1]

        @pl.when(pl.program_id(0) == 0)
        def _():
            acc_ref[...] = jnp.zeros_like(acc_ref)

        gain, scale1 = v_ref[0:1, :], 1.0 + v_ref[2:3, :]
        dav = da_ref[...]
        n, rstd = _rms(x_ref[...])
        dx = _rms_bwd(dav * gain * scale1, n, rstd)
        if has_dx:
            dx = dx + dxin_ref[...]
        dx_ref[...] = dx
        acc_ref[0:1, :] += _colsum(dav)
        acc_ref[1:2, :] += _colsum(dav * (n * gain))
        acc_ref[2:3, :] += _colsum(dav * n * scale1)
        if has_y:
            acc_ref[3:4, :] += _colsum(dx * y_ref[...])
            dy_ref[...] = (dx * v_ref[3:4, :]).astype(BF16)

    ins = [da, x, vec] + ([dx_in] if has_dx else []) + ([y] if has_y else [])
    in_specs = [_rspec(tr, d, off=off), _rspec(tr, d), _cspec(vec.shape)] + [_rspec(tr, d)] * (has_dx + has_y)
    out_specs = [_rspec(tr, d)] + ([_rspec(tr, d)] if has_y else []) + [_cspec((8, d))]
    out_shape = ([jax.ShapeDtypeStruct((rows, d), F32)] + ([jax.ShapeDtypeStruct((rows, d), BF16)] if has_y else [])
                 + [jax.ShapeDtypeStruct((8, d), F32)])
    return pl.pallas_call(
        body, name=name, grid=(rows // tr,), in_specs=in_specs, out_specs=out_specs, out_shape=out_shape,
        compiler_params=_params(("arbitrary",)))(*ins)


def _conv_terms(cc, cx, w_ref, tr):
    t = lax.broadcasted_iota(jnp.int32, (tr, 1), 0) % GRID_W
    first, last = t == 0, t == GRID_W - 1
    u = cc * cx
    prev = jnp.where(first, 0.0, pltpu.roll(u, 1, 0))
    nxt = jnp.where(last, 0.0, pltpu.roll(u, tr - 1, 0))
    y = w_ref[0:1, :] * prev + w_ref[1:2, :] * u + w_ref[2:3, :] * nxt
    return u, prev, nxt, y, first, last


def _conv_fwd(z, conv_w, rows, name, dep):
    d = conv_w.shape[1]
    tr = _tile(rows, 256, GRID_W)

    def body(cb_ref, cc_ref, cx_ref, w_ref, dep_ref, out_ref):
        y = _conv_terms(cc_ref[...], cx_ref[...], w_ref, tr)[3]
        out_ref[...] = (cb_ref[...] * y).astype(BF16)

    return pl.pallas_call(
        body, name=name, grid=(rows // tr,),
        in_specs=[_rspec(tr, d, CB), _rspec(tr, d, CC), _rspec(tr, d, CX), _cspec(conv_w.shape), _cspec(dep.shape)],
        out_specs=_rspec(tr, d), out_shape=jax.ShapeDtypeStruct((rows, d), BF16),
        compiler_params=_params(("parallel",)))(z, z, z, conv_w, dep)


def _conv_bwd(dpc, z, conv_w, name):
    rows, d = dpc.shape
    tr = _tile(rows, 256, GRID_W)

    def body(dpc_ref, cb_ref, cc_ref, cx_ref, w_ref, dz_ref, acc_ref):
        @pl.when(pl.program_id(0) == 0)
        def _():
            acc_ref[...] = jnp.zeros_like(acc_ref)

        cc, cx, dpcv = cc_ref[...], cx_ref[...], dpc_ref[...]
        u, prev, nxt, y, first, last = _conv_terms(cc, cx, w_ref, tr)
        dy = dpcv * cb_ref[...]
        dy_next = jnp.where(last, 0.0, pltpu.roll(dy, tr - 1, 0))
        dy_prev = jnp.where(first, 0.0, pltpu.roll(dy, 1, 0))
        du = w_ref[0:1, :] * dy_next + w_ref[1:2, :] * dy + w_ref[2:3, :] * dy_prev
        dz_ref[:, 0:d] = (dpcv * y).astype(BF16)
        dz_ref[:, d:2 * d] = (du * cx).astype(BF16)
        dz_ref[:, 2 * d:3 * d] = (du * cc).astype(BF16)
        acc_ref[0:1, :] += _colsum(dy * prev)
        acc_ref[1:2, :] += _colsum(dy * u)
        acc_ref[2:3, :] += _colsum(dy * nxt)

    return pl.pallas_call(
        body, name=name, grid=(rows // tr,),
        in_specs=[_rspec(tr, d), _rspec(tr, d, CB), _rspec(tr, d, CC), _rspec(tr, d, CX), _cspec(conv_w.shape)],
        out_specs=[_rspec(tr, 3 * d), _cspec((8, d))],
        out_shape=[jax.ShapeDtypeStruct((rows, 3 * d), BF16), jax.ShapeDtypeStruct((8, d), F32)],
        compiler_params=_params(("arbitrary",)))(dpc, z, z, z, conv_w)


def _rotary_fwd(z, cos, sin, n_lat, name):
    rows = z.shape[0]
    d = z.shape[1] // N_IN
    dk = d // HEADS
    half = dk // 2
    tr = _tile(n_lat, 256, 8)
    tr = _tile(rows - n_lat, tr, 8)
    lat_blocks, all_blocks = n_lat // tr, rows // tr
    ctx_blocks = all_blocks - lat_blocks
    scan_rows = rows + ctx_blocks * tr

    def z_block(g):
        return jnp.where(g < all_blocks, g, g - ctx_blocks)

    def in_spec(w, cb=0):
        return pl.BlockSpec((tr, w), lambda g: (z_block(g), cb))

    def out_spec(w):
        return pl.BlockSpec((tr, w), lambda g: (jnp.where(g < all_blocks, g + ctx_blocks, g - all_blocks), 0))

    def body(q_ref, k_ref, v0_ref, v1_ref, cos_ref, sin_ref, qo_ref, ko_ref, vo_ref):
        cs, sn = cos_ref[...], sin_ref[...]
        keep = jnp.where(z_block(pl.program_id(0)) < lat_blocks, 1.0, 0.0)
        for src, dst, scale in ((q_ref, qo_ref, keep), (k_ref, ko_ref, dk ** -0.5)):
            for h in range(HEADS):
                lo, mid, hi = h * dk, h * dk + half, (h + 1) * dk
                t1, t2 = src[:, lo:mid], src[:, mid:hi]
                dst[:, lo:mid] = ((t1 * cs - t2 * sn) * scale).astype(BF16)
                dst[:, mid:hi] = ((t1 * sn + t2 * cs) * scale).astype(BF16)
        vo_ref[:, 0:d] = v0_ref[...].astype(BF16)
        vo_ref[:, d:2 * d] = v1_ref[...].astype(BF16)

    return pl.pallas_call(
        body, name=name, grid=(all_blocks + ctx_blocks,),
        in_specs=[in_spec(d, CQ), in_spec(d, CK), in_spec(d, CV), in_spec(d, CV + 1), in_spec(half), in_spec(half)],
        out_specs=[out_spec(d), out_spec(d), out_spec(2 * d)],
        out_shape=[jax.ShapeDtypeStruct((scan_rows, d), BF16), jax.ShapeDtypeStruct((scan_rows, d), BF16),
                   jax.ShapeDtypeStruct((scan_rows, 2 * d), BF16)],
        compiler_params=_params(("parallel",)))(z, z, z, z, cos, sin)


def _dz_assemble(dq_f, dq_b, dk_f, dk_b, dv_f, dv_b, cos, sin, dz_conv, dz_g, dz_gc, dz_gr, n_lat, n_ctx, name):
    rows = n_lat + n_ctx
    d = dq_f.shape[1]
    dk = d // HEADS
    half = dk // 2
    tr = _tile(n_ctx, 128, 8)
    lat_blocks, ctx_blocks = n_lat // tr, n_ctx // tr

    def fmap(i):
        return (jnp.where(i < lat_blocks, i + ctx_blocks, i - lat_blocks), 0)

    def bmap(i):
        return (i + ctx_blocks, 0)

    def lmap(i):
        return (jnp.minimum(i, lat_blocks - 1), 0)

    def body(qf_ref, qb_ref, kf_ref, kb_ref, vf_ref, vb_ref, cos_ref, sin_ref, c_ref, g_ref, gc_ref, gr_ref, out_ref):
        cs, sn = cos_ref[...], sin_ref[...]
        is_lat = pl.program_id(0) < lat_blocks
        keep = jnp.where(is_lat, 1.0, 0.0)
        for fa, fb, base, scale in ((qf_ref, qb_ref, CQ * d, keep), (kf_ref, kb_ref, CK * d, dk ** -0.5)):
            for h in range(HEADS):
                lo, mid, hi = h * dk, h * dk + half, (h + 1) * dk
                d1 = fa[:, lo:mid].astype(F32) + fb[:, lo:mid].astype(F32)
                d2 = fa[:, mid:hi].astype(F32) + fb[:, mid:hi].astype(F32)
                out_ref[:, base + lo:base + mid] = ((d1 * cs + d2 * sn) * scale).astype(BF16)
                out_ref[:, base + mid:base + hi] = ((d2 * cs - d1 * sn) * scale).astype(BF16)
        out_ref[:, CV * d:CG * d] = (vf_ref[...].astype(F32) + vb_ref[...].astype(F32)).astype(BF16)
        for src, lo, hi in ((c_ref, CB * d, CQ * d), (g_ref, CG * d, CGC * d), (gc_ref, CGC * d, CGR * d),
                            (gr_ref, CGR * d, N_IN * d)):
            out_ref[:, lo:hi] = jnp.where(is_lat, src[...], jnp.zeros_like(src))

    return pl.pallas_call(
        body, name=name, grid=(rows // tr,),
        in_specs=[pl.BlockSpec((tr, d), fmap), pl.BlockSpec((tr, d), bmap), pl.BlockSpec((tr, d), fmap),
                  pl.BlockSpec((tr, d), bmap), pl.BlockSpec((tr, 2 * d), fmap), pl.BlockSpec((tr, 2 * d), bmap),
                  _rspec(tr, half), _rspec(tr, half), pl.BlockSpec((tr, 3 * d), lmap), pl.BlockSpec((tr, 2 * d), lmap),
                  pl.BlockSpec((tr, d), lmap), pl.BlockSpec((tr, d), lmap)],
        out_specs=_rspec(tr, N_IN * d), out_shape=jax.ShapeDtypeStruct((rows, N_IN * d), BF16),
        compiler_params=_params(("parallel",)))(dq_f, dq_b, dk_f, dk_b, dv_f, dv_b, cos, sin, dz_conv, dz_g, dz_gc, dz_gr)


def _gn_fwd(o, z, rows, row_off, name):
    d = z.shape[1] // N_IN
    dv = 2 * d // HEADS
    tr = _tile(row_off, 128, 8)
    off = row_off // tr

    def body(o_ref, g0_ref, g1_ref, r_ref):
        for h in range(HEADS):
            lo, hi = h * dv, (h + 1) * dv
            g_ref, glo = (g0_ref, lo) if hi <= d else (g1_ref, lo - d)
            o = o_ref[:, lo:hi]
            cen = o - jnp.mean(o, axis=-1, keepdims=True)
            on = cen * lax.rsqrt(jnp.mean(cen * cen, axis=-1, keepdims=True) + EPS)
            g = g_ref[:, glo:glo + dv]
            r_ref[:, lo:hi] = (g * _sig(g) * on).astype(BF16)

    return pl.pallas_call(
        body, name=name, grid=(rows // tr,),
        in_specs=[_rspec(tr, 2 * d, off=off), _rspec(tr, d, CG), _rspec(tr, d, CG + 1)],
        out_specs=_rspec(tr, 2 * d), out_shape=jax.ShapeDtypeStruct((rows, 2 * d), BF16),
        compiler_params=_params(("parallel",)))(o, z, z)


def _gn_bwd(dr, o, z, row_off, name):
    rows = dr.shape[0]
    d = z.shape[1] // N_IN
    dv = 2 * d // HEADS
    tr = _tile(row_off, 128, 8)
    off = row_off // tr

    def body(dr_ref, o_ref, g0_ref, g1_ref, do_ref, dg_ref):
        for h in range(HEADS):
            lo, hi = h * dv, (h + 1) * dv
            g_ref, glo = (g0_ref, lo) if hi <= d else (g1_ref, lo - d)
            o = o_ref[:, lo:hi]
            cen = o - jnp.mean(o, axis=-1, keepdims=True)
            rstd = lax.rsqrt(jnp.mean(cen * cen, axis=-1, keepdims=True) + EPS)
            on = cen * rstd
            g = g_ref[:, glo:glo + dv]
            sg = _sig(g)
            drv = dr_ref[:, lo:hi]
            dg_ref[:, lo:hi] = (drv * on * (sg * (1.0 + g * (1.0 - sg)))).astype(BF16)
            don = drv * (g * sg)
            do = rstd * (don - jnp.mean(don, axis=-1, keepdims=True)
                         - on * jnp.mean(don * on, axis=-1, keepdims=True))
            do_ref[:, lo:hi] = do.astype(BF16)

    return pl.pallas_call(
        body, name=name, grid=(rows // tr,),
        in_specs=[_rspec(tr, 2 * d), _rspec(tr, 2 * d, off=off), _rspec(tr, d, CG), _rspec(tr, d, CG + 1)],
        out_specs=[_rspec(tr, 2 * d), _rspec(tr, 2 * d)],
        out_shape=[jax.ShapeDtypeStruct((rows, 2 * d), BF16), jax.ShapeDtypeStruct((rows, 2 * d), BF16)],
        compiler_params=_params(("parallel",)))(dr, o, z, z)


def _decays(lg, rev):
    row = lax.broadcasted_iota(jnp.int32, (CHUNK, CHUNK), 0)
    col = lax.broadcasted_iota(jnp.int32, (CHUNK, CHUNK), 1)
    rel = ((col - row) if rev else (row - col)).astype(F32)
    mask = jnp.where(rel >= 0, jnp.exp(lg * jnp.maximum(rel, 0.0)), 0.0)
    r = lax.broadcasted_iota(jnp.int32, (CHUNK, 1), 0)
    rr = ((CHUNK - 1 - r) if rev else r).astype(F32)
    chunk_decay = jnp.exp(lg * jnp.full((1, 1), float(CHUNK), F32))
    return (rr, rel), mask, jnp.exp(lg * (rr + 1.0)), jnp.exp(lg * (CHUNK - 1.0 - rr)), chunk_decay


_NT = (((1,), (1,)), ((), ()))
_TN = (((0,), (0,)), ((), ()))


def _dot(a, b, dims=None):
    if dims is None:
        return jnp.dot(a, b, preferred_element_type=F32)
    return lax.dot_general(a, b, dims, preferred_element_type=F32)


def _ret_fwd(q, k, v, lgs, name):
    tt, d = q.shape
    dk, dv = d // HEADS, 2 * d // HEADS
    nc = tt // CHUNK

    def chunk_of(rev, i):
        return (nc - 1 - i) if rev else i

    def cmap(rev):
        return lambda h, i: (chunk_of(rev, i), h)

    def smap(rev):
        return lambda h, i: (chunk_of(rev, i), h, 0, 0)

    def body(lg_ref, qf_ref, kf_ref, vf_ref, qb_ref, kb_ref, vb_ref, o_ref, sf_ref, sb_ref, state_f, state_b):
        h, i = pl.program_id(0), pl.program_id(1)

        @pl.when(i == 0)
        def _():
            state_f[...] = jnp.zeros_like(state_f)
            state_b[...] = jnp.zeros_like(state_b)

        for rev, (q_ref, k_ref, v_ref, s_ref, state) in enumerate((
                (qf_ref, kf_ref, vf_ref, sf_ref, state_f), (qb_ref, kb_ref, vb_ref, sb_ref, state_b))):
            _, mask, qd, kd, cd = _decays(lg_ref[rev, h], bool(rev))
            qv, kv, vv = q_ref[...], k_ref[...], v_ref[...]
            st = state[...]
            p = _dot(qv, kv, _NT) * mask
            out = _dot(p.astype(BF16), vv) + _dot((qv * qd).astype(BF16), st.astype(BF16))
            s_ref[...] = st
            state[...] = cd * st + _dot((kv * kd).astype(BF16), vv, _TN)
            rows = pl.ds(pl.multiple_of(chunk_of(bool(rev), i) * CHUNK, CHUNK), CHUNK)
            first = (2 * i < nc - 1) if rev else (2 * i <= nc - 1)

            @pl.when(first)
            def _():
                o_ref[rows, :] = out

            @pl.when(jnp.logical_not(first))
            def _():
                o_ref[rows, :] += out

    def specs(rev):
        return [pl.BlockSpec((CHUNK, dk), cmap(rev)), pl.BlockSpec((CHUNK, dk), cmap(rev)),
                pl.BlockSpec((CHUNK, dv), cmap(rev))]

    state_shape = jax.ShapeDtypeStruct((nc, HEADS, dk, dv), F32)
    return pl.pallas_call(
        body, name=name, grid=(HEADS, nc),
        in_specs=[pl.BlockSpec(memory_space=pltpu.SMEM)] + specs(False) + specs(True),
        out_specs=[pl.BlockSpec((tt, dv), lambda h, i: (0, h)), pl.BlockSpec((None, None, dk, dv), smap(False)),
                   pl.BlockSpec((None, None, dk, dv), smap(True))],
        out_shape=[jax.ShapeDtypeStruct((tt, 2 * d), F32), state_shape, state_shape],
        scratch_shapes=[pltpu.VMEM((dk, dv), F32), pltpu.VMEM((dk, dv), F32)],
        compiler_params=_params(("parallel", "arbitrary")))(lgs, q, k, v, q, k, v)


def _ret_bwd(q, k, v, do, states_f, states_b, lgs, name):
    tt, d = q.shape
    dk, dv = d // HEADS, 2 * d // HEADS
    nc = tt // CHUNK
    lat_chunks = do.shape[0] // CHUNK
    ctx_chunks = (nc - lat_chunks) // 2

    def chunk_of(rev, i):
        return i if rev else nc - 1 - i

    def cmap(rev):
        return lambda h, i: (chunk_of(rev, i), h)

    def do_map(rev):
        return lambda h, i: (jnp.clip(chunk_of(rev, i) - ctx_chunks, 0, lat_chunks - 1), h)

    def smap(rev):
        return lambda h, i: (chunk_of(rev, i), h, 0, 0)

    def body(lg_ref, *refs):
        h = pl.program_id(0)
        dstates = refs[-2:]

        @pl.when(pl.program_id(1) == 0)
        def _():
            for rev in (0, 1):
                dstates[rev][...] = jnp.zeros_like(dstates[rev])
                refs[10 + 4 * rev + 3][...] = jnp.zeros_like(refs[10 + 4 * rev + 3])

        for rev in (0, 1):
            q_ref, k_ref, v_ref, do_ref, s_ref = refs[5 * rev:5 * rev + 5]
            dq_ref, dk_ref, dv_ref, dlg_ref = refs[10 + 4 * rev:10 + 4 * rev + 4]
            dstate = dstates[rev]
            chunk = chunk_of(bool(rev), pl.program_id(1))
            is_lat = (chunk >= ctx_chunks) & (chunk < ctx_chunks + lat_chunks)
            (rr, rel), mask, qd, kd, cd = _decays(lg_ref[rev, h], bool(rev))
            qv, kv, vv = q_ref[...], k_ref[...], v_ref[...]
            dov = jnp.where(is_lat, do_ref[...], jnp.zeros_like(do_ref))
            st, dst = s_ref[...], dstate[...]
            st_b, dst_b = st.astype(BF16), dst.astype(BF16)
            p = _dot(qv, kv, _NT) * mask
            dp = _dot(dov, vv, _NT)
            da = (dp * mask).astype(BF16)
            dq_state = _dot(dov, st_b, _NT) * qd
            dk_state = _dot(vv, dst_b, _NT) * kd
            dq_ref[...] = (_dot(da, kv) + dq_state).astype(BF16)
            dk_ref[...] = (_dot(da, qv, _TN) + dk_state).astype(BF16)
            dv_ref[...] = (_dot(p.astype(BF16), dov, _TN) + _dot((kv * kd).astype(BF16), dst_b)).astype(BF16)
            dnew = cd * dst + _dot((qv * qd).astype(BF16), dov, _TN)
            dstate[...] = dnew
            through = rr * (jnp.sum(qv.astype(F32) * dq_state, axis=1, keepdims=True)
                            - jnp.sum(kv.astype(F32) * dk_state, axis=1, keepdims=True))
            dlg_ref[...] += _total(rel * p * dp) + _total(through) + CHUNK * _total(st * dnew)

    def specs(rev):
        return [pl.BlockSpec((CHUNK, dk), cmap(rev)), pl.BlockSpec((CHUNK, dk), cmap(rev)),
                pl.BlockSpec((CHUNK, dv), cmap(rev)), pl.BlockSpec((CHUNK, dv), do_map(rev)),
                pl.BlockSpec((None, None, dk, dv), smap(rev))]

    def outs(rev):
        return [pl.BlockSpec((CHUNK, dk), cmap(rev)), pl.BlockSpec((CHUNK, dk), cmap(rev)),
                pl.BlockSpec((CHUNK, dv), cmap(rev)), pl.BlockSpec((None, 8, 128), lambda h, i: (h, 0, 0))]

    return pl.pallas_call(
        body, name=name, grid=(HEADS, nc),
        in_specs=[pl.BlockSpec(memory_space=pltpu.SMEM)] + specs(False) + specs(True),
        out_specs=outs(False) + outs(True),
        out_shape=[jax.ShapeDtypeStruct((tt, d), BF16), jax.ShapeDtypeStruct((tt, d), BF16),
                   jax.ShapeDtypeStruct((tt, 2 * d), BF16), jax.ShapeDtypeStruct((HEADS, 8, 128), F32)] * 2,
        scratch_shapes=[pltpu.VMEM((dk, dv), F32), pltpu.VMEM((dk, dv), F32)],
        compiler_params=_params(("parallel", "arbitrary")))(lgs, q, k, v, do, states_f, q, k, v, do, states_b)


def _silu(v):
    return v * _sig(v)


def _mod_proj(cs, w_mod, b_loc, dec, name):
    nb = w_mod.shape[1]

    def body(cs_ref, w_ref, b_ref, dec_ref, out_ref, lg_ref):
        out_ref[...] = jnp.dot(_silu(cs_ref[...]), w_ref[...], preferred_element_type=F32, precision=HIGHEST) + b_ref[...]
        a = dec_ref[...]
        lg_ref[...] = jnp.minimum(a, 0.0) - jnp.log1p(jnp.exp(-jnp.abs(a)))

    return pl.pallas_call(
        body, name=name,
        out_shape=[jax.ShapeDtypeStruct((16, nb), F32), jax.ShapeDtypeStruct(dec.shape, F32)],
        compiler_params=_params())(cs, w_mod, b_loc, dec)


def _mod_grad(cs, dm, w_mod, name):
    d, nb = w_mod.shape

    def body(cs_ref, dm_ref, w_ref, gw_ref, part_ref):
        dmv = dm_ref[...]
        gw_ref[...] = lax.dot_general(_silu(cs_ref[...]), dmv, _TN, preferred_element_type=F32, precision=HIGHEST)
        part_ref[...] = lax.dot_general(dmv, w_ref[...], _NT, preferred_element_type=F32, precision=HIGHEST)

    return pl.pallas_call(
        body, name=name,
        out_shape=[jax.ShapeDtypeStruct((d, nb), F32), jax.ShapeDtypeStruct((16, d), F32)],
        compiler_params=_params())(cs, dm, w_mod)


def _reduce_small(gathered, dec, n_feat, name):
    _, rows, d = gathered.shape

    def body(g_ref, dec_ref, red_ref, misc_ref):
        total = g_ref[0]
        for i in range(1, N_DEV):
            total = total + g_ref[i]
        red_ref[...] = total
        misc_ref[...] = jnp.zeros_like(misc_ref)
        misc_ref[0:1, :] = jnp.zeros((1, 128), F32) + (0.5 / n_feat) * _total(total[16:17, :])
        misc_ref[1:3, :] = total[14:16, 0:128] * _sig(-dec_ref[0:2, :])

    return pl.pallas_call(
        body, name=name,
        out_shape=[jax.ShapeDtypeStruct((rows, d), F32), jax.ShapeDtypeStruct((8, 128), F32)],
        compiler_params=_params())(gathered, dec)


def _c_ctx_grad(parts, c_ctx, name):
    d = c_ctx.shape[1]

    def body(p_ref, c_ref, out_ref):
        total = p_ref[0]
        for i in range(1, N_DEV):
            total = total + p_ref[i]
        cv = c_ref[...]
        sg = _sig(cv)
        out_ref[...] = total[8:9, :] * (sg * (1.0 + cv * (1.0 - sg)))

    return pl.pallas_call(body, name=name, out_shape=jax.ShapeDtypeStruct((1, d), F32),
                          compiler_params=_params())(parts, c_ctx)


def _adamw(parts, w, m, v, name, own=None):
    n_parts, rows, cols = parts.shape
    row_bytes = cols * (parts.dtype.itemsize * (n_parts + 1) + 7 * 4)
    tr = _tile(rows, max(16, (4 * 1024 * 1024) // row_bytes), 16 if rows % 16 == 0 else 8)
    blocks = rows // tr

    def body(*refs):
        p_ref, w_ref, m_ref, v_ref, g_ref, d_ref, mo_ref, vo_ref = refs[-8:]
        if own is None:
            g = p_ref[0].astype(F32)
        else:
            g = refs[1][...].astype(F32) + p_ref[0].astype(F32)
        for i in range(1, n_parts):
            g = g + p_ref[i].astype(F32)
        m2 = ADAM_B1 * m_ref[...] + (1.0 - ADAM_B1) * g
        v2 = ADAM_B2 * v_ref[...] + (1.0 - ADAM_B2) * jnp.square(g)
        m_hat = m2 / (1.0 - ADAM_B1 ** ADAM_STEP)
        v_hat = v2 / (1.0 - ADAM_B2 ** ADAM_STEP)
        g_ref[...] = g
        d_ref[...] = -ADAM_LR * (m_hat / (jnp.sqrt(v_hat) + ADAM_EPS) + ADAM_WD * w_ref[...])
        mo_ref[...] = m2
        vo_ref[...] = v2

    out_shape = [jax.ShapeDtypeStruct((rows, cols), F32)] * 4
    if own is None:
        spec = _rspec(tr, cols)
        return pl.pallas_call(
            body, name=name, grid=(blocks,),
            in_specs=[pl.BlockSpec((n_parts, tr, cols), lambda i: (0, i, 0)), spec, spec, spec],
            out_specs=[spec] * 4, out_shape=out_shape, compiler_params=_params(("parallel",)))(parts, w, m, v)
    g_full, axis, me = own

    def own_map(i, me_ref):
        return (me_ref[0] * blocks + i, 0) if axis == 0 else (i, me_ref[0])

    spec = pl.BlockSpec((tr, cols), lambda i, me_ref: (i, 0))
    return pl.pallas_call(
        body, name=name, out_shape=out_shape,
        grid_spec=pltpu.PrefetchScalarGridSpec(
            num_scalar_prefetch=1, grid=(blocks,),
            in_specs=[pl.BlockSpec((tr, cols), own_map),
                      pl.BlockSpec((n_parts, tr, cols), lambda i, me_ref: (0, i, 0)), spec, spec, spec],
            out_specs=[spec] * 4),
        compiler_params=_params(("parallel",)))(me.reshape(1), g_full, parts, w, m, v)


def _rope_tables(pos, dk):
    half = dk // 2
    inv_freq = 1.0 / (ROPE_BASE ** jnp.linspace(0.0, 1.0, half, dtype=F32))
    ang = pos[:, None] * inv_freq[None, :]
    return jnp.cos(ang), jnp.sin(ang)


def _pad_lanes(v, width):
    return jnp.pad(v, ((0, 0), (0, width - v.shape[1])))


def kernel(x, c, ctx, c_ctx, w_mod, b_mod, norm1_g, w_in, conv_w, w_conv_out, ret_decay_fwd, ret_decay_bwd, w_ret_out, w_o, norm2_g, w_ff1, w_ff2, final_g, loss_target, m_c_ctx, m_w_mod, m_b_mod, m_norm1_g, m_w_in, m_conv_w, m_w_conv_out, m_ret_decay_fwd, m_ret_decay_bwd, m_w_ret_out, m_w_o, m_norm2_g, m_w_ff1, m_w_ff2, m_final_g, v_c_ctx, v_w_mod, v_b_mod, v_norm1_g, v_w_in, v_conv_w, v_w_conv_out, v_ret_decay_fwd, v_ret_decay_bwd, v_w_ret_out, v_w_o, v_norm2_g, v_w_ff1, v_w_ff2, v_final_g):
    n_lat, d = x.shape[1], x.shape[2]
    n_ctx = ctx.shape[1]
    assert n_ctx % CHUNK == 0 and n_lat % CHUNK == 0 and n_lat % GRID_W == 0
    dk = d // HEADS
    nb = w_mod.shape[2]
    me = 4 * lax.axis_index("x") + 2 * lax.axis_index("y") + lax.axis_index("c")
    xl, ctxl, target = x[0], ctx[0], loss_target[0]

    conv_w_f = _small_allgather(conv_w[0], "ag_conv_w").transpose(1, 0, 2).reshape(3, d)

    c_all = _small_allgather(c, "ag_c").reshape(N_DEV, d)
    cs = jnp.concatenate([c_all, c_ctx[None], jnp.zeros((7, d), F32)], axis=0)
    dec = jnp.pad(jnp.concatenate([ret_decay_fwd, ret_decay_bwd], axis=0), ((0, 6), (0, 128 - HEADS)))
    b_loc = lax.dynamic_slice(b_mod, (0, me * nb), (1, nb))
    modp, lgs = _mod_proj(cs, w_mod[0], b_loc, dec, "mod_proj")
    modp_all = _small_allgather(modp, "ag_mod")
    mod_l = lax.dynamic_index_in_dim(modp_all, me, axis=1, keepdims=False).reshape(N_MOD, d)
    mod_c = modp_all[:, 8, :].reshape(N_MOD, d)
    lgs = lgs[0:2, :HEADS]
    zero_row = jnp.zeros((1, d), F32)
    vec1_l = jnp.concatenate([norm1_g, mod_l[0:1], mod_l[1:2], zero_row], axis=0)
    vec1_c = jnp.concatenate([norm1_g, mod_c[0:1], mod_c[1:2], zero_row], axis=0)
    vec2 = jnp.concatenate([norm2_g, mod_l[3:4], mod_l[4:5], mod_l[2:3]], axis=0)
    vec3 = jnp.concatenate([mod_l[5:6], final_g[None]], axis=0)

    a_all = jnp.concatenate([_modulate(xl, vec1_l, "modulate1"), _modulate(ctxl, vec1_c, "modulate1_ctx")], axis=0)
    arrival = me ^ jnp.array([0, 1, 4, 5, 2, 3, 6, 7], jnp.int32)
    w_in_f, z = _gather_project(a_all, _place_shard(w_in[0], 1, me, "place_w_in"), arrival, "proj_in")

    later, after = {}, (z,)
    for wname, w, axis in (("w_conv_out", w_conv_out, 0), ("w_ret_out", w_ret_out, 0), ("w_o", w_o, 0),
                           ("w_ff1", w_ff1, 1), ("w_ff2", w_ff2, 0)):
        later[wname], token = _exchange_start(_place_shard(w[0], axis, me, "place_" + wname), axis, True,
                                              "ag_" + wname + "_start", after=after)
        after = (token,)
    pc = _conv_fwd(z, conv_w_f, n_lat, "conv_fwd", token)
    pos = jnp.concatenate([n_ctx + jnp.arange(n_lat, dtype=F32), jnp.arange(n_ctx, dtype=F32)])
    cos, sin = _rope_tables(pos, dk)
    q_s, k_s, v_s = _rotary_fwd(z, cos, sin, n_lat, "rotary_fwd")
    o, st_f, st_b = _ret_fwd(q_s, k_s, v_s, lgs, "ret_fwd")
    r = _gn_fwd(o, z, n_lat, n_ctx, "gn_fwd")
    w_conv_out_f = _exchange_wait(later["w_conv_out"], r, "ag_w_conv_out_wait")
    w_ret_out_f = _exchange_wait(later["w_ret_out"], r, "ag_w_ret_out_wait")
    w_o_f = _exchange_wait(later["w_o"], r, "ag_w_o_wait")
    y_conv = _mm(pc, w_conv_out_f, "nn", [F32], "proj_conv_out", tn=2048)
    tn_d = _tile(d, 1024, 128)
    gate_offs = (CGC * d // tn_d, CGR * d // tn_d)

    def merge(acc, yc, gc, gr):
        return acc, _sig(gc) * yc + _sig(gr) * acc

    y_ret, mg = _mm(r, w_ret_out_f, "nn", [F32, BF16], "proj_ret_out", tm=512, tn=tn_d,
                    extras=[(y_conv, 0), (z, gate_offs[0]), (z, gate_offs[1])], epilogue=merge)
    y_l = _mm(mg, w_o_f, "nn", [F32], "proj_o", tn=2048)
    x1, a2 = _resid_modulate(xl, y_l, vec2, "resid_modulate2")

    def sqrelu(acc):
        return acc, jnp.square(jnp.maximum(acc, 0.0))

    w_ff1_f = _exchange_wait(later["w_ff1"], a2, "ag_w_ff1_wait")
    hff, s = _mm(a2, w_ff1_f, "nn", [BF16, BF16], "ff1", epilogue=sqrelu)
    w_ff2_f = _exchange_wait(later["w_ff2"], s, "ag_w_ff2_wait")
    f = _mm(s, w_ff2_f, "nn", [F32], "ff2")
    dx2, df, acc3 = _loss_head(x1, f, target, vec3, "loss_head")

    def d_sqrelu(acc, h):
        return (acc * (2.0 * jnp.maximum(h.astype(F32), 0.0)),)

    dh = _mm(df, w_ff2_f, "nt", [BF16], "ff2_dx", extras=[(hff, 0)], epilogue=d_sqrelu)
    sent = {}
    sent["w_ff2"], token = _exchange_start(_mm(s, df, "tn", [BF16], "ff2_dw"), 0, False, "rs_w_ff2_start")
    da2 = _mm(dh, w_ff1_f, "nt", [F32], "ff1_dx", dep=token)
    sent["w_ff1"], token = _exchange_start(_mm(a2, dh, "tn", [BF16], "ff1_dw"), 1, False, "rs_w_ff1_start")
    dx1, dyl, acc2 = _modulate_bwd(da2, x1, vec2, "modulate2_bwd", dx_in=dx2, y=y_l)

    def d_merge(acc, yc, yr, gc, gr):
        sc, sr = _sig(gc), _sig(gr)
        return acc * sc, acc * sr, acc * yc * (sc * (1.0 - sc)), acc * yr * (sr * (1.0 - sr))

    dyc, dyr, dgc, dgr = _mm(dyl, w_o_f, "nt", [BF16] * 4, "proj_o_dx", tm=512, tn=tn_d,
                             extras=[(y_conv, 0), (y_ret, 0), (z, gate_offs[0]), (z, gate_offs[1])], epilogue=d_merge,
                             dep=token)
    sent["w_o"], token = _exchange_start(_mm(mg, dyl, "tn", [BF16], "proj_o_dw"), 0, False, "rs_w_o_start")
    dpc = _mm(dyc, w_conv_out_f, "nt", [F32], "proj_conv_out_dx", tn=2048, dep=token)
    sent["w_conv_out"], token = _exchange_start(_mm(pc, dyc, "tn", [BF16], "proj_conv_out_dw"), 0, False,
                                                "rs_w_conv_out_start")
    dz_conv, acc_conv = _conv_bwd(dpc, z, conv_w_f, "conv_bwd")
    dr = _mm(dyr, w_ret_out_f, "nt", [F32], "proj_ret_out_dx", dep=token)
    sent["w_ret_out"], token = _exchange_start(_mm(r, dyr, "tn", [BF16], "proj_ret_out_dw"), 0, False,
                                               "rs_w_ret_out_start")
    do, dz_g = _gn_bwd(dr, o, z, n_ctx, "gn_bwd")
    dq_f, dk_f, dv_f, dlg_f, dq_b, dk_b, dv_b, dlg_b = _ret_bwd(q_s, k_s, v_s, do, st_f, st_b, lgs, "ret_bwd")
    dz = _dz_assemble(dq_f, dq_b, dk_f, dk_b, dv_f, dv_b, cos, sin, dz_conv, dz_g, dgc, dgr, n_lat, n_ctx,
                      "dz_assemble")
    g_in = _mm(a_all, dz, "tn", [BF16], "proj_in_dw", tn=2048, tk=(n_lat + n_ctx) // 4, dep=token)
    sent["w_in"], token = _exchange_start(g_in, 1, False, "rs_w_in_start")
    da_all = _mm(dz, w_in_f, "nt", [F32], "proj_in_dx", tn=2048, dep=token)
    grad_x, acc1 = _modulate_bwd(da_all, xl, vec1_l, "modulate1_bwd", dx_in=dx1)
    _, acc1c = _modulate_bwd(da_all, ctxl, vec1_c, "modulate1_ctx_bwd", da_off=n_lat)

    lane_pad = functools.partial(_pad_lanes, width=d)
    packet = jnp.concatenate([
        acc1[2:3] + acc1c[2:3], acc2[2:3], acc3[1:2],
        acc1[0:1], acc1[1:2], acc2[3:4], acc2[0:1], acc2[1:2], acc3[0:1],
        acc1c[0:1], acc1c[1:2],
        acc_conv[0:3],
        lane_pad(dlg_f[:, 0, 0][None]), lane_pad(dlg_b[:, 0, 0][None]),
        acc3[2:3],
        jnp.zeros((7, d), F32)], axis=0)
    gathered = _small_allgather(packet, "ag_small")
    red, misc = _reduce_small(gathered, dec, d, "reduce_small")
    dmod_ctx = jnp.concatenate([red[9], red[10], jnp.zeros((4 * d,), F32)])
    dmod_all = jnp.concatenate([gathered[:, 3:9, :].reshape(N_DEV, N_MOD * d), dmod_ctx[None]], axis=0)
    dm = jnp.pad(lax.dynamic_slice(dmod_all, (0, me * nb), (N_DEV + 1, nb)), ((0, 7), (0, 0)))
    g_mod, c_part = _mod_grad(cs, dm, w_mod[0], "mod_grad")
    g_c_ctx = _c_ctx_grad(_small_allgather(c_part, "ag_c_ctx"), c_ctx[None], "c_ctx_grad")
    g_b_mod = red[3:9].reshape(1, N_MOD * d) + dmod_ctx[None]
    g_conv_w = lax.dynamic_slice(red[11:14], (0, me * (d // N_DEV)), (3, d // N_DEV))

    res = {"w_mod": _adamw(g_mod[None], w_mod[0], m_w_mod[0], v_w_mod[0], "adamw_w_mod")}
    after = res["w_mod"][0]
    for wname, axis, w, m, v in (("w_ff2", 0, w_ff2, m_w_ff2, v_w_ff2), ("w_ff1", 1, w_ff1, m_w_ff1, v_w_ff1),
                                 ("w_o", 0, w_o, m_w_o, v_w_o), ("w_conv_out", 0, w_conv_out, m_w_conv_out, v_w_conv_out),
                                 ("w_ret_out", 0, w_ret_out, m_w_ret_out, v_w_ret_out),
                                 ("w_in", 1, w_in, m_w_in, v_w_in)):
        g_full, parts = _exchange_wait(sent[wname], after, "rs_" + wname + "_wait")
        res[wname] = _adamw(parts, w[0], m[0], v[0], "adamw_" + wname, own=(g_full, axis, me))
        after = res[wname][0]
    res = {k: tuple(t[None] for t in val) for k, val in res.items()}

    small = [("c_ctx", g_c_ctx, c_ctx, m_c_ctx, v_c_ctx), ("b_mod", g_b_mod, b_mod, m_b_mod, v_b_mod),
             ("norm1_g", red[0:1], norm1_g, m_norm1_g, v_norm1_g), ("conv_w", g_conv_w, conv_w, m_conv_w, v_conv_w),
             ("ret_decay_fwd", misc[1:2, :HEADS], ret_decay_fwd, m_ret_decay_fwd, v_ret_decay_fwd),
             ("ret_decay_bwd", misc[2:3, :HEADS], ret_decay_bwd, m_ret_decay_bwd, v_ret_decay_bwd),
             ("norm2_g", red[1:2], norm2_g, m_norm2_g, v_norm2_g), ("final_g", red[2:3], final_g, m_final_g, v_final_g)]

    def flat(t):
        t = t.reshape(-1)
        return jnp.pad(t, (0, (-t.shape[0]) % 1024))

    packed = [jnp.concatenate([flat(item[j]) for item in small]).reshape(-1, 128) for j in range(1, 5)]
    outs = _adamw(packed[0][None], packed[1], packed[2], packed[3], "adamw_small")
    start = 0
    for name, _, w, _, _ in small:
        size = w.size
        res[name] = tuple(o.reshape(-1)[start:start + size].reshape(w.shape) for o in outs)
        start += size + (-size) % 1024

    order = ["c_ctx", "w_mod", "b_mod", "norm1_g", "w_in", "conv_w", "w_conv_out", "ret_decay_fwd", "ret_decay_bwd",
             "w_ret_out", "w_o", "norm2_g", "w_ff1", "w_ff2", "final_g"]
    loss = misc[0, 0]
    return (loss, grad_x[None], *[res[n][0] for n in order], *[res[n][1] for n in order],
            *[res[n][2] for n in order], *[res[n][3] for n in order])
```

```python
import functools

import jax
import jax.numpy as jnp
from jax import lax
from jax.experimental import pallas as pl
from jax.experimental.pallas import tpu as pltpu

F32 = jnp.float32
BF16 = jnp.bfloat16
MESH = pl.DeviceIdType.MESH

N_DEV = 8
HEADS = 8
N_MOD = 6
N_IN = 11
GRID_W = 64
CHUNK = 256
ROPE_BASE = 10000.0
EPS = 1e-6
ADAM_LR, ADAM_B1, ADAM_B2, ADAM_EPS, ADAM_WD, ADAM_STEP = 0.001, 0.9, 0.999, 1e-08, 0.01, 10
VMEM_LIMIT = 56 * 1024 * 1024
HIGHEST = lax.Precision.HIGHEST
CB, CC, CX, CQ, CK, CV, CG, CGC, CGR = 0, 1, 2, 3, 4, 5, 7, 9, 10


def _tile(n, target, mult):
    t = (min(target, n) // mult) * mult
    while t >= mult:
        if n % t == 0:
            return t
        t -= mult
    return n


def _params(sem=None):
    return pltpu.CompilerParams(dimension_semantics=sem, vmem_limit_bytes=VMEM_LIMIT)


def _sig(v):
    return 1.0 / (1.0 + jnp.exp(-v))


def _coords():
    return lax.axis_index("x"), lax.axis_index("y"), lax.axis_index("c")


def _flip(p, m):
    return tuple(1 - v if (m >> s) & 1 else v for v, s in zip(p, (2, 1, 0)))


def _index(p):
    return 4 * p[0] + 2 * p[1] + p[2]


def _small_allgather(x, name):
    r, n = x.shape

    def body(x_ref, out_ref, send_sems, recv_sems):
        me = _coords()
        out_ref[pl.ds(_index(me), 1)] = x_ref[...][None]
        sent = []
        for m in range(1, N_DEV):
            cp = pltpu.make_async_remote_copy(
                src_ref=x_ref, dst_ref=out_ref.at[_index(me)], send_sem=send_sems.at[m - 1],
                recv_sem=recv_sems.at[m - 1], device_id=_flip(me, m), device_id_type=MESH)
            cp.start()
            sent.append(cp)
        for m in range(1, N_DEV):
            pltpu.make_async_remote_copy(
                src_ref=x_ref, dst_ref=out_ref.at[_index(_flip(me, m))], send_sem=send_sems.at[m - 1],
                recv_sem=recv_sems.at[m - 1], device_id=_flip(me, m), device_id_type=MESH).wait_recv()
        for cp in sent:
            cp.wait_send()

    return pl.pallas_call(
        body, name=name, out_shape=jax.ShapeDtypeStruct((N_DEV, r, n), x.dtype),
        in_specs=[pl.BlockSpec(memory_space=pltpu.VMEM)], out_specs=pl.BlockSpec(memory_space=pltpu.VMEM),
        scratch_shapes=[pltpu.SemaphoreType.DMA((N_DEV - 1,)), pltpu.SemaphoreType.DMA((N_DEV - 1,))],
    )(x)


def _window(ref, j, r, c, axis):
    if axis == 0:
        return ref.at[pl.ds(pl.multiple_of(j * r, 8), r), :]
    return ref.at[:, pl.ds(pl.multiple_of(j * c, 128), c)]


_HBM = pl.BlockSpec(memory_space=pltpu.HBM)
_SEM = pl.BlockSpec(memory_space=pltpu.SEMAPHORE)
_ANY = pl.BlockSpec(memory_space=pl.ANY)
_EFFECT = pltpu.SideEffectType.DATAFLOW_SIDE_EFFECTING


def _place_shard(w, axis, me, name):
    r, c = w.shape
    full = (N_DEV * r, c) if axis == 0 else (r, N_DEV * c)
    tr = _tile(r, max(16, (2 * 1024 * 1024) // (c * 4)), 16)
    blocks = r // tr

    def out_map(i, me_ref):
        return (me_ref[0] * blocks + i, 0) if axis == 0 else (i, me_ref[0])

    def body(me_ref, w_ref, out_ref):
        out_ref[...] = w_ref[...].astype(BF16)

    return pl.pallas_call(
        body, name=name, out_shape=jax.ShapeDtypeStruct(full, BF16),
        grid_spec=pltpu.PrefetchScalarGridSpec(
            num_scalar_prefetch=1, grid=(blocks,), in_specs=[pl.BlockSpec((tr, c), lambda i, me_ref: (i, 0))],
            out_specs=pl.BlockSpec((tr, c), out_map)),
        compiler_params=_params(("parallel",)))(me.reshape(1), w)


def _exchange_start(src, axis, gather, name, after=()):
    r, c = (src.shape[0] // N_DEV, src.shape[1]) if axis == 0 else (src.shape[0], src.shape[1] // N_DEV)
    n_hbm = 1 if gather else 2
    n_after = len(after)

    def body(*refs):
        src_ref, land_ref = refs[0], refs[n_hbm - 1]
        send_sems, recv_sems = refs[n_hbm + n_after:n_hbm + n_after + 2]
        token = refs[-1]
        me = _coords()
        for m in range(1, N_DEV):
            peer = _flip(me, m)
            if gather:
                mine = theirs = _window(land_ref, _index(me), r, c, axis)
            else:
                mine, theirs = _window(src_ref, _index(peer), r, c, axis), land_ref.at[m - 1]
            pltpu.make_async_remote_copy(
                src_ref=mine, dst_ref=theirs, send_sem=send_sems.at[m - 1], recv_sem=recv_sems.at[m - 1],
                device_id=peer, device_id_type=MESH).start()
        token[...] = jnp.zeros_like(token)

    hbm = [pltpu.with_memory_space_constraint(src, pltpu.HBM)]
    if not gather:
        hbm.append(pltpu.with_memory_space_constraint(lax.empty((N_DEV - 1, r, c), src.dtype), pltpu.HBM))
    outs = pl.pallas_call(
        body, name=name,
        out_shape=(pltpu.SemaphoreType.DMA((N_DEV - 1,)), pltpu.SemaphoreType.DMA((N_DEV - 1,)),
                   *[pltpu.HBM(t.shape, t.dtype) for t in hbm], jax.ShapeDtypeStruct((8, 128), F32)),
        in_specs=[_HBM] * n_hbm + [_ANY] * n_after,
        out_specs=(_SEM, _SEM, *[_HBM] * n_hbm, pl.BlockSpec(memory_space=pltpu.VMEM)),
        input_output_aliases={i: 2 + i for i in range(n_hbm)},
        compiler_params=pltpu.CompilerParams(has_side_effects=_EFFECT),
    )(*hbm, *after)
    return (outs[:2], outs[2:2 + n_hbm], (axis, gather, r, c)), outs[-1]


def _exchange_wait(handle, after, name):
    (send_sems, recv_sems), hbm, (axis, gather, r, c) = handle
    n_hbm = len(hbm)

    def body(*refs):
        src_ref, land_ref = refs[0], refs[n_hbm - 1]
        send_sems, recv_sems = refs[n_hbm:n_hbm + 2]
        me = _coords()
        for m in range(1, N_DEV):
            peer = _flip(me, m)
            if gather:
                mine, theirs = _window(land_ref, _index(me), r, c, axis), _window(land_ref, _index(peer), r, c, axis)
            else:
                mine, theirs = _window(src_ref, _index(peer), r, c, axis), land_ref.at[m - 1]
            copy = pltpu.make_async_remote_copy(
                src_ref=mine, dst_ref=theirs, send_sem=send_sems.at[m - 1], recv_sem=recv_sems.at[m - 1],
                device_id=peer, device_id_type=MESH)
            copy.wait_send()
            copy.wait_recv()

    outs = pl.pallas_call(
        body, name=name, out_shape=tuple(pltpu.HBM(t.shape, t.dtype) for t in hbm),
        in_specs=[_HBM] * n_hbm + [_SEM, _SEM, _ANY], out_specs=tuple([_HBM] * n_hbm),
        input_output_aliases={i: i for i in range(n_hbm)},
        compiler_params=pltpu.CompilerParams(has_side_effects=_EFFECT),
    )(*hbm, send_sems, recv_sems, after)
    return outs[0] if gather else tuple(outs)


def _mm(a, b, mode, out_dtypes, name, tm=1024, tn=1024, tk=2048, extras=(), epilogue=None, dep=None):
    if mode == "nn":
        (m, k), n = a.shape, b.shape[1]
    elif mode == "nt":
        (m, k), n = a.shape, b.shape[0]
    else:
        (k, m), n = a.shape, b.shape[1]
    tm, tn = _tile(m, tm, 8), _tile(n, tn, 128)
    tk = _tile(k, tk, 16 if mode == "tn" else 128)
    nk = k // tk
    swap = nk == 1 and (k * n + (n // tn) * m * k) < (m * k + (m // tm) * k * n)

    def ij(p, q):
        return (q, p) if swap else (p, q)

    def spec(shape, fn):
        return pl.BlockSpec(shape, lambda p, q, kk: fn(*ij(p, q), kk))

    a_spec = spec((tk, tm), lambda i, j, kk: (kk, i)) if mode == "tn" else spec((tm, tk), lambda i, j, kk: (i, kk))
    b_spec = spec((tn, tk), lambda i, j, kk: (j, kk)) if mode == "nt" else spec((tk, tn), lambda i, j, kk: (kk, j))
    dims = {"nn": (((1,), (0,)), ((), ())), "nt": (((1,), (1,)), ((), ())), "tn": (((0,), (0,)), ((), ()))}[mode]
    ex_specs = [spec((tm, tn), functools.partial(lambda i, j, kk, off: (i, j + off), off=off)) for _, off in extras]
    deps = [] if dep is None else [dep]
    dep_specs = [pl.BlockSpec(dep.shape, lambda p, q, kk: (0, 0))] if deps else []
    n_ex, n_out = len(extras), len(out_dtypes)
    n_in = 2 + n_ex + len(deps)

    def body(*refs):
        a_ref, b_ref = refs[0], refs[1]
        ex_refs = refs[2:2 + n_ex]
        out_refs = refs[n_in:n_in + n_out]

        def product():
            return lax.dot_general(a_ref[...], b_ref[...], dims, preferred_element_type=F32)

        def finish(res):
            res = epilogue(res, *[e[...] for e in ex_refs]) if epilogue is not None else (res,)
            for o_ref, val in zip(out_refs, res):
                o_ref[...] = val.astype(o_ref.dtype)

        if nk == 1:
            finish(product())
            return
        acc = refs[-1]
        kk = pl.program_id(2)

        @pl.when(kk == 0)
        def _():
            acc[...] = product()

        @pl.when((kk > 0) & (kk < nk - 1))
        def _():
            acc[...] += product()

        @pl.when(kk == nk - 1)
        def _():
            finish(acc[...] + product())

    outs = pl.pallas_call(
        body, name=name, grid=(*ij(m // tm, n // tn), nk),
        in_specs=[a_spec, b_spec] + ex_specs + dep_specs,
        out_specs=[spec((tm, tn), lambda i, j, kk: (i, j)) for _ in out_dtypes],
        out_shape=[jax.ShapeDtypeStruct((m, n), dt) for dt in out_dtypes],
        scratch_shapes=[pltpu.VMEM((tm, tn), F32)] if nk > 1 else [],
        compiler_params=_params(("parallel", "parallel", "arbitrary")),
    )(a, b, *[e for e, _ in extras], *deps)
    return outs if n_out > 1 else outs[0]


def _gather_project(a, w_full, order, name, riders=(), tm=768):
    m, k = a.shape
    c = w_full.shape[1] // N_DEV
    tm = _tile(m, tm, 8)
    n_i = m // tm
    chips = [4, 2, 6]
    n_rid = len(riders)
    shard = [(k, c, 1)] + [((t.shape[0] // N_DEV, t.shape[1], 0) if ax == 0 else (t.shape[0], t.shape[1] // N_DEV, 1))
                           for t, ax in riders]

    def body(order_ref, a_ref, *refs):
        outs = refs[1 + n_rid:2 + 2 * n_rid]
        z_ref, wbuf, send_sems, recv_sems, load_sem = refs[2 + 2 * n_rid:]
        w_ref = outs[0]
        p, i = pl.program_id(0), pl.program_id(1)
        me = _coords()
        sibling = _flip(me, 1)

        def copy(t, s, block, to):
            r_t, c_t, ax_t = shard[t]
            win = _window(outs[t], _index(block), r_t, c_t, ax_t)
            return pltpu.make_async_remote_copy(src_ref=win, dst_ref=win, send_sem=send_sems.at[7 * t + s],
                                                recv_sem=recv_sems.at[7 * t + s], device_id=to, device_id_type=MESH)

        def first(t):
            return [copy(t, 0, me, sibling)] + [copy(t, 1 + j, me, _flip(me, mm)) for j, mm in enumerate(chips)]

        def passed(t):
            return [copy(t, 4 + j, _flip(me, mm), sibling) for j, mm in enumerate(chips)]

        @pl.when((p == 0) & (i == 0))
        def _():
            for t in range(1 + n_rid):
                for cp in first(t):
                    cp.start()

        arrivals = [(0, None)] + [x for j in range(3) for x in ((1 + j, passed(0)[j]), (4 + j, None))]
        for piece, (sem, forward) in enumerate(arrivals, start=1):
            @pl.when((p == piece) & (i == 0))
            def _(sem=sem, forward=forward, piece=piece):
                src_block = (sibling if piece == 1 else
                             _flip(me if piece % 2 == 0 else sibling, chips[(piece - 2) // 2]))
                copy(0, sem, src_block, me).wait_recv()
                if forward is not None:
                    forward.start()

        @pl.when(i == 0)
        def _():
            load = pltpu.make_async_copy(_window(w_ref, order_ref[p], k, c, 1), wbuf, load_sem)
            load.start()
            load.wait()

        @pl.when((p == N_DEV - 1) & (i == 0))
        def _():
            for t in range(1, 1 + n_rid):
                for j, mm in enumerate(chips):
                    copy(t, 1 + j, _flip(me, mm), me).wait_recv()
                    passed(t)[j].start()

        z_ref[...] = jnp.dot(a_ref[...], wbuf[...], preferred_element_type=F32)

        @pl.when((p == N_DEV - 1) & (i == n_i - 1))
        def _():
            for t in range(1, 1 + n_rid):
                copy(t, 0, sibling, me).wait_recv()
                for j, mm in enumerate(chips):
                    copy(t, 4 + j, _flip(sibling, mm), me).wait_recv()
            for t in range(1 + n_rid):
                for cp in first(t) + passed(t):
                    cp.wait_send()

    gathered = [w_full] + [t for t, _ in riders]
    n_sem = 7 * len(gathered)
    return pl.pallas_call(
        body, name=name,
        out_shape=[jax.ShapeDtypeStruct(t.shape, t.dtype) for t in gathered]
        + [jax.ShapeDtypeStruct((m, N_DEV * c), F32)],
        grid_spec=pltpu.PrefetchScalarGridSpec(
            num_scalar_prefetch=1, grid=(N_DEV, n_i),
            in_specs=[pl.BlockSpec((tm, k), lambda p, i, order_ref: (i, 0))]
            + [pl.BlockSpec(memory_space=pl.ANY)] * len(gathered),
            out_specs=[pl.BlockSpec(memory_space=pl.ANY)] * len(gathered)
            + [pl.BlockSpec((tm, c), lambda p, i, order_ref: (i, order_ref[p]))],
            scratch_shapes=[pltpu.VMEM((k, c), w_full.dtype), pltpu.SemaphoreType.DMA((n_sem,)),
                            pltpu.SemaphoreType.DMA((n_sem,)), pltpu.SemaphoreType.DMA(())]),
        input_output_aliases={2 + t: t for t in range(len(gathered))},
        compiler_params=_params(("arbitrary", "arbitrary")))(order, a, *gathered)


def _rspec(tr, w, cb=0, off=0):
    return pl.BlockSpec((tr, w), lambda i: (i + off, cb))


def _cspec(shape):
    return pl.BlockSpec(shape, lambda i: (0,) * len(shape))


def _rms(xf):
    rstd = lax.rsqrt(jnp.mean(xf * xf, axis=-1, keepdims=True) + EPS)
    return xf * rstd, rstd


def _rms_bwd(dn, n, rstd):
    return rstd * (dn - n * jnp.mean(dn * n, axis=-1, keepdims=True))


def _colsum(v):
    return jnp.sum(v, axis=0, keepdims=True)


def _total(v):
    return jnp.sum(jnp.sum(v, axis=1, keepdims=True), axis=0, keepdims=True)


def _modulate(x, vec, name):
    rows, d = x.shape
    tr = _tile(rows, 256, 8)

    def body(x_ref, v_ref, a_ref):
        n, _ = _rms(x_ref[...])
        a_ref[...] = ((n * v_ref[0:1, :]) * (1.0 + v_ref[2:3, :]) + v_ref[1:2, :]).astype(BF16)

    return pl.pallas_call(
        body, name=name, grid=(rows // tr,), in_specs=[_rspec(tr, d), _cspec(vec.shape)],
        out_specs=_rspec(tr, d), out_shape=jax.ShapeDtypeStruct((rows, d), BF16),
        compiler_params=_params(("parallel",)))(x, vec)


def _resid_modulate(x, y, vec, name):
    rows, d = x.shape
    tr = _tile(rows, 256, 8)

    def body(x_ref, y_ref, v_ref, x1_ref, a_ref):
        x1 = x_ref[...] + v_ref[3:4, :] * y_ref[...]
        x1_ref[...] = x1
        n, _ = _rms(x1)
        a_ref[...] = ((n * v_ref[0:1, :]) * (1.0 + v_ref[2:3, :]) + v_ref[1:2, :]).astype(BF16)

    return pl.pallas_call(
        body, name=name, grid=(rows // tr,), in_specs=[_rspec(tr, d), _rspec(tr, d), _cspec(vec.shape)],
        out_specs=[_rspec(tr, d), _rspec(tr, d)],
        out_shape=[jax.ShapeDtypeStruct((rows, d), F32), jax.ShapeDtypeStruct((rows, d), BF16)],
        compiler_params=_params(("parallel",)))(x, y, vec)


def _loss_head(x1, f, target, vec, name):
    rows, d = x1.shape
    tr = _tile(rows, 256, 8)

    def body(x1_ref, f_ref, t_ref, v_ref, dx2_ref, df_ref, acc_ref):
        @pl.when(pl.program_id(0) == 0)
        def _():
            acc_ref[...] = jnp.zeros_like(acc_ref)

        gate, gain = v_ref[0:1, :], v_ref[1:2, :]
        fv = f_ref[...]
        n, rstd = _rms(x1_ref[...] + gate * fv)
        err = n * gain - t_ref[...]
        dy = err / d
        dx2 = _rms_bwd(dy * gain, n, rstd)
        dx2_ref[...] = dx2
        df_ref[...] = (dx2 * gate).astype(BF16)
        acc_ref[0:1, :] += _colsum(dx2 * fv)
        acc_ref[1:2, :] += _colsum(dy * n)
        acc_ref[2:3, :] += _colsum(err * err)

    return pl.pallas_call(
        body, name=name, grid=(rows // tr,),
        in_specs=[_rspec(tr, d), _rspec(tr, d), _rspec(tr, d), _cspec(vec.shape)],
        out_specs=[_rspec(tr, d), _rspec(tr, d), _cspec((8, d))],
        out_shape=[jax.ShapeDtypeStruct((rows, d), F32), jax.ShapeDtypeStruct((rows, d), BF16),
                   jax.ShapeDtypeStruct((8, d), F32)],
        compiler_params=_params(("arbitrary",)))(x1, f, target, vec)


def _modulate_bwd(da, x, vec, name, da_off=0, dx_in=None, y=None):
    rows, d = x.shape
    tr = _tile(rows, 256, 8)
    off = da_off // tr
    has_dx, has_y = dx_in is not None, y is not None

    def body(*refs):
        da_ref, x_ref, v_ref = refs[0], refs[1], refs[2]
        pos = 3
        dxin_ref = refs[pos] if has_dx else None
        pos += has_dx
        y_ref = refs[pos] if has_y else None
        pos += has_y
        dx_ref = refs[pos]
        dy_ref = refs[pos + 1] if has_y else None
        acc_ref = refs[-1]

        @pl.when(pl.program_id(0) == 0)
        def _():
            acc_ref[...] = jnp.zeros_like(acc_ref)

        gain, scale1 = v_ref[0:1, :], 1.0 + v_ref[2:3, :]
        dav = da_ref[...]
        n, rstd = _rms(x_ref[...])
        dx = _rms_bwd(dav * gain * scale1, n, rstd)
        if has_dx:
            dx = dx + dxin_ref[...]
        dx_ref[...] = dx
        acc_ref[0:1, :] += _colsum(dav)
        acc_ref[1:2, :] += _colsum(dav * (n * gain))
        acc_ref[2:3, :] += _colsum(dav * n * scale1)
        if has_y:
            acc_ref[3:4, :] += _colsum(dx * y_ref[...])
            dy_ref[...] = (dx * v_ref[3:4, :]).astype(BF16)

    ins = [da, x, vec] + ([dx_in] if has_dx else []) + ([y] if has_y else [])
    in_specs = [_rspec(tr, d, off=off), _rspec(tr, d), _cspec(vec.shape)] + [_rspec(tr, d)] * (has_dx + has_y)
    out_specs = [_rspec(tr, d)] + ([_rspec(tr, d)] if has_y else []) + [_cspec((8, d))]
    out_shape = ([jax.ShapeDtypeStruct((rows, d), F32)] + ([jax.ShapeDtypeStruct((rows, d), BF16)] if has_y else [])
                 + [jax.ShapeDtypeStruct((8, d), F32)])
    return pl.pallas_call(
        body, name=name, grid=(rows // tr,), in_specs=in_specs, out_specs=out_specs, out_shape=out_shape,
        compiler_params=_params(("arbitrary",)))(*ins)


def _conv_terms(cc, cx, w_ref, tr):
    t = lax.broadcasted_iota(jnp.int32, (tr, 1), 0) % GRID_W
    first, last = t == 0, t == GRID_W - 1
    u = cc * cx
    prev = jnp.where(first, 0.0, pltpu.roll(u, 1, 0))
    nxt = jnp.where(last, 0.0, pltpu.roll(u, tr - 1, 0))
    y = w_ref[0:1, :] * prev + w_ref[1:2, :] * u + w_ref[2:3, :] * nxt
    return u, prev, nxt, y, first, last


def _conv_fwd(z, conv_w, rows, name, dep):
    d = conv_w.shape[1]
    tr = _tile(rows, 256, GRID_W)

    def body(cb_ref, cc_ref, cx_ref, w_ref, dep_ref, out_ref):
        y = _conv_terms(cc_ref[...], cx_ref[...], w_ref, tr)[3]
        out_ref[...] = (cb_ref[...] * y).astype(BF16)

    return pl.pallas_call(
        body, name=name, grid=(rows // tr,),
        in_specs=[_rspec(tr, d, CB), _rspec(tr, d, CC), _rspec(tr, d, CX), _cspec(conv_w.shape), _cspec(dep.shape)],
        out_specs=_rspec(tr, d), out_shape=jax.ShapeDtypeStruct((rows, d), BF16),
        compiler_params=_params(("parallel",)))(z, z, z, conv_w, dep)


def _conv_bwd(dpc, z, conv_w, name):
    rows, d = dpc.shape
    tr = _tile(rows, 256, GRID_W)

    def body(dpc_ref, cb_ref, cc_ref, cx_ref, w_ref, dz_ref, acc_ref):
        @pl.when(pl.program_id(0) == 0)
        def _():
            acc_ref[...] = jnp.zeros_like(acc_ref)

        cc, cx, dpcv = cc_ref[...], cx_ref[...], dpc_ref[...]
        u, prev, nxt, y, first, last = _conv_terms(cc, cx, w_ref, tr)
        dy = dpcv * cb_ref[...]
        dy_next = jnp.where(last, 0.0, pltpu.roll(dy, tr - 1, 0))
        dy_prev = jnp.where(first, 0.0, pltpu.roll(dy, 1, 0))
        du = w_ref[0:1, :] * dy_next + w_ref[1:2, :] * dy + w_ref[2:3, :] * dy_prev
        dz_ref[:, 0:d] = (dpcv * y).astype(BF16)
        dz_ref[:, d:2 * d] = (du * cx).astype(BF16)
        dz_ref[:, 2 * d:3 * d] = (du * cc).astype(BF16)
        acc_ref[0:1, :] += _colsum(dy * prev)
        acc_ref[1:2, :] += _colsum(dy * u)
        acc_ref[2:3, :] += _colsum(dy * nxt)

    return pl.pallas_call(
        body, name=name, grid=(rows // tr,),
        in_specs=[_rspec(tr, d), _rspec(tr, d, CB), _rspec(tr, d, CC), _rspec(tr, d, CX), _cspec(conv_w.shape)],
        out_specs=[_rspec(tr, 3 * d), _cspec((8, d))],
        out_shape=[jax.ShapeDtypeStruct((rows, 3 * d), BF16), jax.ShapeDtypeStruct((8, d), F32)],
        compiler_params=_params(("arbitrary",)))(dpc, z, z, z, conv_w)


def _rotary_fwd(z, cos, sin, n_lat, name):
    rows = z.shape[0]
    d = z.shape[1] // N_IN
    dk = d // HEADS
    half = dk // 2
    tr = _tile(n_lat, 256, 8)
    tr = _tile(rows - n_lat, tr, 8)
    lat_blocks, all_blocks = n_lat // tr, rows // tr
    ctx_blocks = all_blocks - lat_blocks
    scan_rows = rows + ctx_blocks * tr

    def z_block(g):
        return jnp.where(g < all_blocks, g, g - ctx_blocks)

    def in_spec(w, cb=0):
        return pl.BlockSpec((tr, w), lambda g: (z_block(g), cb))

    def out_spec(w):
        return pl.BlockSpec((tr, w), lambda g: (jnp.where(g < all_blocks, g + ctx_blocks, g - all_blocks), 0))

    def body(q_ref, k_ref, v0_ref, v1_ref, cos_ref, sin_ref, qo_ref, ko_ref, vo_ref):
        cs, sn = cos_ref[...], sin_ref[...]
        keep = jnp.where(z_block(pl.program_id(0)) < lat_blocks, 1.0, 0.0)
        for src, dst, scale in ((q_ref, qo_ref, keep), (k_ref, ko_ref, dk ** -0.5)):
            for h in range(HEADS):
                lo, mid, hi = h * dk, h * dk + half, (h + 1) * dk
                t1, t2 = src[:, lo:mid], src[:, mid:hi]
                dst[:, lo:mid] = ((t1 * cs - t2 * sn) * scale).astype(BF16)
                dst[:, mid:hi] = ((t1 * sn + t2 * cs) * scale).astype(BF16)
        vo_ref[:, 0:d] = v0_ref[...].astype(BF16)
        vo_ref[:, d:2 * d] = v1_ref[...].astype(BF16)

    return pl.pallas_call(
        body, name=name, grid=(all_blocks + ctx_blocks,),
        in_specs=[in_spec(d, CQ), in_spec(d, CK), in_spec(d, CV), in_spec(d, CV + 1), in_spec(half), in_spec(half)],
        out_specs=[out_spec(d), out_spec(d), out_spec(2 * d)],
        out_shape=[jax.ShapeDtypeStruct((scan_rows, d), BF16), jax.ShapeDtypeStruct((scan_rows, d), BF16),
                   jax.ShapeDtypeStruct((scan_rows, 2 * d), BF16)],
        compiler_params=_params(("parallel",)))(z, z, z, z, cos, sin)


def _dz_assemble(dq_f, dq_b, dk_f, dk_b, dv_f, dv_b, cos, sin, dz_conv, dz_g, dz_gc, dz_gr, n_lat, n_ctx, name):
    rows = n_lat + n_ctx
    d = dq_f.shape[1]
    dk = d // HEADS
    half = dk // 2
    tr = _tile(n_ctx, 128, 8)
    lat_blocks, ctx_blocks = n_lat // tr, n_ctx // tr

    def fmap(i):
        return (jnp.where(i < lat_blocks, i + ctx_blocks, i - lat_blocks), 0)

    def bmap(i):
        return (i + ctx_blocks, 0)

    def lmap(i):
        return (jnp.minimum(i, lat_blocks - 1), 0)

    def body(qf_ref, qb_ref, kf_ref, kb_ref, vf_ref, vb_ref, cos_ref, sin_ref, c_ref, g_ref, gc_ref, gr_ref, out_ref):
        cs, sn = cos_ref[...], sin_ref[...]
        is_lat = pl.program_id(0) < lat_blocks
        keep = jnp.where(is_lat, 1.0, 0.0)
        for fa, fb, base, scale in ((qf_ref, qb_ref, CQ * d, keep), (kf_ref, kb_ref, CK * d, dk ** -0.5)):
            for h in range(HEADS):
                lo, mid, hi = h * dk, h * dk + half, (h + 1) * dk
                d1 = fa[:, lo:mid].astype(F32) + fb[:, lo:mid].astype(F32)
                d2 = fa[:, mid:hi].astype(F32) + fb[:, mid:hi].astype(F32)
                out_ref[:, base + lo:base + mid] = ((d1 * cs + d2 * sn) * scale).astype(BF16)
                out_ref[:, base + mid:base + hi] = ((d2 * cs - d1 * sn) * scale).astype(BF16)
        out_ref[:, CV * d:CG * d] = (vf_ref[...].astype(F32) + vb_ref[...].astype(F32)).astype(BF16)
        for src, lo, hi in ((c_ref, CB * d, CQ * d), (g_ref, CG * d, CGC * d), (gc_ref, CGC * d, CGR * d),
                            (gr_ref, CGR * d, N_IN * d)):
            out_ref[:, lo:hi] = jnp.where(is_lat, src[...], jnp.zeros_like(src))

    return pl.pallas_call(
        body, name=name, grid=(rows // tr,),
        in_specs=[pl.BlockSpec((tr, d), fmap), pl.BlockSpec((tr, d), bmap), pl.BlockSpec((tr, d), fmap),
                  pl.BlockSpec((tr, d), bmap), pl.BlockSpec((tr, 2 * d), fmap), pl.BlockSpec((tr, 2 * d), bmap),
                  _rspec(tr, half), _rspec(tr, half), pl.BlockSpec((tr, 3 * d), lmap), pl.BlockSpec((tr, 2 * d), lmap),
                  pl.BlockSpec((tr, d), lmap), pl.BlockSpec((tr, d), lmap)],
        out_specs=_rspec(tr, N_IN * d), out_shape=jax.ShapeDtypeStruct((rows, N_IN * d), BF16),
        compiler_params=_params(("parallel",)))(dq_f, dq_b, dk_f, dk_b, dv_f, dv_b, cos, sin, dz_conv, dz_g, dz_gc, dz_gr)


def _gn_fwd(o, z, rows, row_off, name):
    d = z.shape[1] // N_IN
    dv = 2 * d // HEADS
    tr = _tile(row_off, 128, 8)
    off = row_off // tr

    def body(o_ref, g0_ref, g1_ref, r_ref):
        for h in range(HEADS):
            lo, hi = h * dv, (h + 1) * dv
            g_ref, glo = (g0_ref, lo) if hi <= d else (g1_ref, lo - d)
            o = o_ref[:, lo:hi]
            cen = o - jnp.mean(o, axis=-1, keepdims=True)
            on = cen * lax.rsqrt(jnp.mean(cen * cen, axis=-1, keepdims=True) + EPS)
            g = g_ref[:, glo:glo + dv]
            r_ref[:, lo:hi] = (g * _sig(g) * on).astype(BF16)

    return pl.pallas_call(
        body, name=name, grid=(rows // tr,),
        in_specs=[_rspec(tr, 2 * d, off=off), _rspec(tr, d, CG), _rspec(tr, d, CG + 1)],
        out_specs=_rspec(tr, 2 * d), out_shape=jax.ShapeDtypeStruct((rows, 2 * d), BF16),
        compiler_params=_params(("parallel",)))(o, z, z)


def _gn_bwd(dr, o, z, row_off, name):
    rows = dr.shape[0]
    d = z.shape[1] // N_IN
    dv = 2 * d // HEADS
    tr = _tile(row_off, 128, 8)
    off = row_off // tr

    def body(dr_ref, o_ref, g0_ref, g1_ref, do_ref, dg_ref):
        for h in range(HEADS):
            lo, hi = h * dv, (h + 1) * dv
            g_ref, glo = (g0_ref, lo) if hi <= d else (g1_ref, lo - d)
            o = o_ref[:, lo:hi]
            cen = o - jnp.mean(o, axis=-1, keepdims=True)
            rstd = lax.rsqrt(jnp.mean(cen * cen, axis=-1, keepdims=True) + EPS)
            on = cen * rstd
            g = g_ref[:, glo:glo + dv]
            sg = _sig(g)
            drv = dr_ref[:, lo:hi]
            dg_ref[:, lo:hi] = (drv * on * (sg * (1.0 + g * (1.0 - sg)))).astype(BF16)
            don = drv * (g * sg)
            do = rstd * (don - jnp.mean(don, axis=-1, keepdims=True)
                         - on * jnp.mean(don * on, axis=-1, keepdims=True))
            do_ref[:, lo:hi] = do.astype(BF16)

    return pl.pallas_call(
        body, name=name, grid=(rows // tr,),
        in_specs=[_rspec(tr, 2 * d), _rspec(tr, 2 * d, off=off), _rspec(tr, d, CG), _rspec(tr, d, CG + 1)],
        out_specs=[_rspec(tr, 2 * d), _rspec(tr, 2 * d)],
        out_shape=[jax.ShapeDtypeStruct((rows, 2 * d), BF16), jax.ShapeDtypeStruct((rows, 2 * d), BF16)],
        compiler_params=_params(("parallel",)))(dr, o, z, z)


def _decays(lg, rev):
    row = lax.broadcasted_iota(jnp.int32, (CHUNK, CHUNK), 0)
    col = lax.broadcasted_iota(jnp.int32, (CHUNK, CHUNK), 1)
    rel = ((col - row) if rev else (row - col)).astype(F32)
    mask = jnp.where(rel >= 0, jnp.exp(lg * jnp.maximum(rel, 0.0)), 0.0)
    r = lax.broadcasted_iota(jnp.int32, (CHUNK, 1), 0)
    rr = ((CHUNK - 1 - r) if rev else r).astype(F32)
    chunk_decay = jnp.exp(lg * jnp.full((1, 1), float(CHUNK), F32))
    return (rr, rel), mask, jnp.exp(lg * (rr + 1.0)), jnp.exp(lg * (CHUNK - 1.0 - rr)), chunk_decay


_NT = (((1,), (1,)), ((), ()))
_TN = (((0,), (0,)), ((), ()))


def _dot(a, b, dims=None):
    if dims is None:
        return jnp.dot(a, b, preferred_element_type=F32)
    return lax.dot_general(a, b, dims, preferred_element_type=F32)


def _ret_fwd(q, k, v, lgs, name):
    tt, d = q.shape
    dk, dv = d // HEADS, 2 * d // HEADS
    nc = tt // CHUNK

    def chunk_of(rev, i):
        return (nc - 1 - i) if rev else i

    def cmap(rev):
        return lambda h, i: (chunk_of(rev, i), h)

    def smap(rev):
        return lambda h, i: (chunk_of(rev, i), h, 0, 0)

    def body(lg_ref, qf_ref, kf_ref, vf_ref, qb_ref, kb_ref, vb_ref, o_ref, sf_ref, sb_ref, state_f, state_b):
        h, i = pl.program_id(0), pl.program_id(1)

        @pl.when(i == 0)
        def _():
            state_f[...] = jnp.zeros_like(state_f)
            state_b[...] = jnp.zeros_like(state_b)

        for rev, (q_ref, k_ref, v_ref, s_ref, state) in enumerate((
                (qf_ref, kf_ref, vf_ref, sf_ref, state_f), (qb_ref, kb_ref, vb_ref, sb_ref, state_b))):
            _, mask, qd, kd, cd = _decays(lg_ref[rev, h], bool(rev))
            qv, kv, vv = q_ref[...], k_ref[...], v_ref[...]
            st = state[...]
            p = _dot(qv, kv, _NT) * mask
            out = _dot(p.astype(BF16), vv) + _dot((qv * qd).astype(BF16), st.astype(BF16))
            s_ref[...] = st
            state[...] = cd * st + _dot((kv * kd).astype(BF16), vv, _TN)
            rows = pl.ds(pl.multiple_of(chunk_of(bool(rev), i) * CHUNK, CHUNK), CHUNK)
            first = (2 * i < nc - 1) if rev else (2 * i <= nc - 1)

            @pl.when(first)
            def _():
                o_ref[rows, :] = out

            @pl.when(jnp.logical_not(first))
            def _():
                o_ref[rows, :] += out

    def specs(rev):
        return [pl.BlockSpec((CHUNK, dk), cmap(rev)), pl.BlockSpec((CHUNK, dk), cmap(rev)),
                pl.BlockSpec((CHUNK, dv), cmap(rev))]

    state_shape = jax.ShapeDtypeStruct((nc, HEADS, dk, dv), F32)
    return pl.pallas_call(
        body, name=name, grid=(HEADS, nc),
        in_specs=[pl.BlockSpec(memory_space=pltpu.SMEM)] + specs(False) + specs(True),
        out_specs=[pl.BlockSpec((tt, dv), lambda h, i: (0, h)), pl.BlockSpec((None, None, dk, dv), smap(False)),
                   pl.BlockSpec((None, None, dk, dv), smap(True))],
        out_shape=[jax.ShapeDtypeStruct((tt, 2 * d), F32), state_shape, state_shape],
        scratch_shapes=[pltpu.VMEM((dk, dv), F32), pltpu.VMEM((dk, dv), F32)],
        compiler_params=_params(("parallel", "arbitrary")))(lgs, q, k, v, q, k, v)


def _ret_bwd(q, k, v, do, states_f, states_b, lgs, name):
    tt, d = q.shape
    dk, dv = d // HEADS, 2 * d // HEADS
    nc = tt // CHUNK
    lat_chunks = do.shape[0] // CHUNK
    ctx_chunks = (nc - lat_chunks) // 2

    def chunk_of(rev, i):
        return i if rev else nc - 1 - i

    def cmap(rev):
        return lambda h, i: (chunk_of(rev, i), h)

    def do_map(rev):
        return lambda h, i: (jnp.clip(chunk_of(rev, i) - ctx_chunks, 0, lat_chunks - 1), h)

    def smap(rev):
        return lambda h, i: (chunk_of(rev, i), h, 0, 0)

    def body(lg_ref, *refs):
        h = pl.program_id(0)
        dstates = refs[-2:]

        @pl.when(pl.program_id(1) == 0)
        def _():
            for rev in (0, 1):
                dstates[rev][...] = jnp.zeros_like(dstates[rev])
                refs[10 + 4 * rev + 3][...] = jnp.zeros_like(refs[10 + 4 * rev + 3])

        for rev in (0, 1):
            q_ref, k_ref, v_ref, do_ref, s_ref = refs[5 * rev:5 * rev + 5]
            dq_ref, dk_ref, dv_ref, dlg_ref = refs[10 + 4 * rev:10 + 4 * rev + 4]
            dstate = dstates[rev]
            chunk = chunk_of(bool(rev), pl.program_id(1))
            is_lat = (chunk >= ctx_chunks) & (chunk < ctx_chunks + lat_chunks)
            (rr, rel), mask, qd, kd, cd = _decays(lg_ref[rev, h], bool(rev))
            qv, kv, vv = q_ref[...], k_ref[...], v_ref[...]
            dov = jnp.where(is_lat, do_ref[...], jnp.zeros_like(do_ref))
            st, dst = s_ref[...], dstate[...]
            st_b, dst_b = st.astype(BF16), dst.astype(BF16)
            p = _dot(qv, kv, _NT) * mask
            dp = _dot(dov, vv, _NT)
            da = (dp * mask).astype(BF16)
            dq_state = _dot(dov, st_b, _NT) * qd
            dk_state = _dot(vv, dst_b, _NT) * kd
            dq_ref[...] = (_dot(da, kv) + dq_state).astype(BF16)
            dk_ref[...] = (_dot(da, qv, _TN) + dk_state).astype(BF16)
            dv_ref[...] = (_dot(p.astype(BF16), dov, _TN) + _dot((kv * kd).astype(BF16), dst_b)).astype(BF16)
            dnew = cd * dst + _dot((qv * qd).astype(BF16), dov, _TN)
            dstate[...] = dnew
            through = rr * (jnp.sum(qv.astype(F32) * dq_state, axis=1, keepdims=True)
                            - jnp.sum(kv.astype(F32) * dk_state, axis=1, keepdims=True))
            dlg_ref[...] += _total(rel * p * dp) + _total(through) + CHUNK * _total(st * dnew)

    def specs(rev):
        return [pl.BlockSpec((CHUNK, dk), cmap(rev)), pl.BlockSpec((CHUNK, dk), cmap(rev)),
                pl.BlockSpec((CHUNK, dv), cmap(rev)), pl.BlockSpec((CHUNK, dv), do_map(rev)),
                pl.BlockSpec((None, None, dk, dv), smap(rev))]

    def outs(rev):
        return [pl.BlockSpec((CHUNK, dk), cmap(rev)), pl.BlockSpec((CHUNK, dk), cmap(rev)),
                pl.BlockSpec((CHUNK, dv), cmap(rev)), pl.BlockSpec((None, 8, 128), lambda h, i: (h, 0, 0))]

    return pl.pallas_call(
        body, name=name, grid=(HEADS, nc),
        in_specs=[pl.BlockSpec(memory_space=pltpu.SMEM)] + specs(False) + specs(True),
        out_specs=outs(False) + outs(True),
        out_shape=[jax.ShapeDtypeStruct((tt, d), BF16), jax.ShapeDtypeStruct((tt, d), BF16),
                   jax.ShapeDtypeStruct((tt, 2 * d), BF16), jax.ShapeDtypeStruct((HEADS, 8, 128), F32)] * 2,
        scratch_shapes=[pltpu.VMEM((dk, dv), F32), pltpu.VMEM((dk, dv), F32)],
        compiler_params=_params(("parallel", "arbitrary")))(lgs, q, k, v, do, states_f, q, k, v, do, states_b)


def _silu(v):
    return v * _sig(v)


def _mod_proj(cs, w_mod, b_loc, dec, name):
    nb = w_mod.shape[1]

    def body(cs_ref, w_ref, b_ref, dec_ref, out_ref, lg_ref):
        out_ref[...] = jnp.dot(_silu(cs_ref[...]), w_ref[...], preferred_element_type=F32, precision=HIGHEST) + b_ref[...]
        a = dec_ref[...]
        lg_ref[...] = jnp.minimum(a, 0.0) - jnp.log1p(jnp.exp(-jnp.abs(a)))

    return pl.pallas_call(
        body, name=name,
        out_shape=[jax.ShapeDtypeStruct((16, nb), F32), jax.ShapeDtypeStruct(dec.shape, F32)],
        compiler_params=_params())(cs, w_mod, b_loc, dec)


def _mod_grad(cs, dm, w_mod, name):
    d, nb = w_mod.shape

    def body(cs_ref, dm_ref, w_ref, gw_ref, part_ref):
        dmv = dm_ref[...]
        gw_ref[...] = lax.dot_general(_silu(cs_ref[...]), dmv, _TN, preferred_element_type=F32, precision=HIGHEST)
        part_ref[...] = lax.dot_general(dmv, w_ref[...], _NT, preferred_element_type=F32, precision=HIGHEST)

    return pl.pallas_call(
        body, name=name,
        out_shape=[jax.ShapeDtypeStruct((d, nb), F32), jax.ShapeDtypeStruct((16, d), F32)],
        compiler_params=_params())(cs, dm, w_mod)


def _reduce_small(gathered, dec, n_feat, name):
    _, rows, d = gathered.shape

    def body(g_ref, dec_ref, red_ref, misc_ref):
        total = g_ref[0]
        for i in range(1, N_DEV):
            total = total + g_ref[i]
        red_ref[...] = total
        misc_ref[...] = jnp.zeros_like(misc_ref)
        misc_ref[0:1, :] = jnp.zeros((1, 128), F32) + (0.5 / n_feat) * _total(total[16:17, :])
        misc_ref[1:3, :] = total[14:16, 0:128] * _sig(-dec_ref[0:2, :])

    return pl.pallas_call(
        body, name=name,
        out_shape=[jax.ShapeDtypeStruct((rows, d), F32), jax.ShapeDtypeStruct((8, 128), F32)],
        compiler_params=_params())(gathered, dec)


def _c_ctx_grad(parts, c_ctx, name):
    d = c_ctx.shape[1]

    def body(p_ref, c_ref, out_ref):
        total = p_ref[0]
        for i in range(1, N_DEV):
            total = total + p_ref[i]
        cv = c_ref[...]
        sg = _sig(cv)
        out_ref[...] = total[8:9, :] * (sg * (1.0 + cv * (1.0 - sg)))

    return pl.pallas_call(body, name=name, out_shape=jax.ShapeDtypeStruct((1, d), F32),
                          compiler_params=_params())(parts, c_ctx)


def _adamw(parts, w, m, v, name, own=None):
    n_parts, rows, cols = parts.shape
    row_bytes = cols * (parts.dtype.itemsize * (n_parts + 1) + 7 * 4)
    tr = _tile(rows, max(16, (4 * 1024 * 1024) // row_bytes), 16 if rows % 16 == 0 else 8)
    blocks = rows // tr

    def body(*refs):
        p_ref, w_ref, m_ref, v_ref, g_ref, d_ref, mo_ref, vo_ref = refs[-8:]
        if own is None:
            g = p_ref[0].astype(F32)
        else:
            g = refs[1][...].astype(F32) + p_ref[0].astype(F32)
        for i in range(1, n_parts):
            g = g + p_ref[i].astype(F32)
        m2 = ADAM_B1 * m_ref[...] + (1.0 - ADAM_B1) * g
        v2 = ADAM_B2 * v_ref[...] + (1.0 - ADAM_B2) * jnp.square(g)
        m_hat = m2 / (1.0 - ADAM_B1 ** ADAM_STEP)
        v_hat = v2 / (1.0 - ADAM_B2 ** ADAM_STEP)
        g_ref[...] = g
        d_ref[...] = -ADAM_LR * (m_hat / (jnp.sqrt(v_hat) + ADAM_EPS) + ADAM_WD * w_ref[...])
        mo_ref[...] = m2
        vo_ref[...] = v2

    out_shape = [jax.ShapeDtypeStruct((rows, cols), F32)] * 4
    if own is None:
        spec = _rspec(tr, cols)
        return pl.pallas_call(
            body, name=name, grid=(blocks,),
            in_specs=[pl.BlockSpec((n_parts, tr, cols), lambda i: (0, i, 0)), spec, spec, spec],
            out_specs=[spec] * 4, out_shape=out_shape, compiler_params=_params(("parallel",)))(parts, w, m, v)
    g_full, axis, me = own

    def own_map(i, me_ref):
        return (me_ref[0] * blocks + i, 0) if axis == 0 else (i, me_ref[0])

    spec = pl.BlockSpec((tr, cols), lambda i, me_ref: (i, 0))
    return pl.pallas_call(
        body, name=name, out_shape=out_shape,
        grid_spec=pltpu.PrefetchScalarGridSpec(
            num_scalar_prefetch=1, grid=(blocks,),
            in_specs=[pl.BlockSpec((tr, cols), own_map),
                      pl.BlockSpec((n_parts, tr, cols), lambda i, me_ref: (0, i, 0)), spec, spec, spec],
            out_specs=[spec] * 4),
        compiler_params=_params(("parallel",)))(me.reshape(1), g_full, parts, w, m, v)


def _rope_tables(pos, dk):
    half = dk // 2
    inv_freq = 1.0 / (ROPE_BASE ** jnp.linspace(0.0, 1.0, half, dtype=F32))
    ang = pos[:, None] * inv_freq[None, :]
    return jnp.cos(ang), jnp.sin(ang)


def _pad_lanes(v, width):
    return jnp.pad(v, ((0, 0), (0, width - v.shape[1])))


def kernel(x, c, ctx, c_ctx, w_mod, b_mod, norm1_g, w_in, conv_w, w_conv_out, ret_decay_fwd, ret_decay_bwd, w_ret_out, w_o, norm2_g, w_ff1, w_ff2, final_g, loss_target, m_c_ctx, m_w_mod, m_b_mod, m_norm1_g, m_w_in, m_conv_w, m_w_conv_out, m_ret_decay_fwd, m_ret_decay_bwd, m_w_ret_out, m_w_o, m_norm2_g, m_w_ff1, m_w_ff2, m_final_g, v_c_ctx, v_w_mod, v_b_mod, v_norm1_g, v_w_in, v_conv_w, v_w_conv_out, v_ret_decay_fwd, v_ret_decay_bwd, v_w_ret_out, v_w_o, v_norm2_g, v_w_ff1, v_w_ff2, v_final_g):
    n_lat, d = x.shape[1], x.shape[2]
    n_ctx = ctx.shape[1]
    assert n_ctx % CHUNK == 0 and n_lat % CHUNK == 0 and n_lat % GRID_W == 0
    dk = d // HEADS
    nb = w_mod.shape[2]
    me = 4 * lax.axis_index("x") + 2 * lax.axis_index("y") + lax.axis_index("c")
    xl, ctxl, target = x[0], ctx[0], loss_target[0]

    conv_w_f = _small_allgather(conv_w[0], "ag_conv_w").transpose(1, 0, 2).reshape(3, d)

    c_all = _small_allgather(c, "ag_c").reshape(N_DEV, d)
    cs = jnp.concatenate([c_all, c_ctx[None], jnp.zeros((7, d), F32)], axis=0)
    dec = jnp.pad(jnp.concatenate([ret_decay_fwd, ret_decay_bwd], axis=0), ((0, 6), (0, 128 - HEADS)))
    b_loc = lax.dynamic_slice(b_mod, (0, me * nb), (1, nb))
    modp, lgs = _mod_proj(cs, w_mod[0], b_loc, dec, "mod_proj")
    modp_all = _small_allgather(modp, "ag_mod")
    mod_l = lax.dynamic_index_in_dim(modp_all, me, axis=1, keepdims=False).reshape(N_MOD, d)
    mod_c = modp_all[:, 8, :].reshape(N_MOD, d)
    lgs = lgs[0:2, :HEADS]
    zero_row = jnp.zeros((1, d), F32)
    vec1_l = jnp.concatenate([norm1_g, mod_l[0:1], mod_l[1:2], zero_row], axis=0)
    vec1_c = jnp.concatenate([norm1_g, mod_c[0:1], mod_c[1:2], zero_row], axis=0)
    vec2 = jnp.concatenate([norm2_g, mod_l[3:4], mod_l[4:5], mod_l[2:3]], axis=0)
    vec3 = jnp.concatenate([mod_l[5:6], final_g[None]], axis=0)

    a_all = jnp.concatenate([_modulate(xl, vec1_l, "modulate1"), _modulate(ctxl, vec1_c, "modulate1_ctx")], axis=0)
    arrival = me ^ jnp.array([0, 1, 4, 5, 2, 3, 6, 7], jnp.int32)
    riders = [(_place_shard(w[0], 0, me, "place_" + wname), 0)
              for wname, w in (("w_conv_out", w_conv_out), ("w_ret_out", w_ret_out), ("w_o", w_o))]
    w_in_f, w_conv_out_f, w_ret_out_f, w_o_f, z = _gather_project(
        a_all, _place_shard(w_in[0], 1, me, "place_w_in"), arrival, "proj_in", riders=riders)

    later, after = {}, (z, conv_w_f)
    for wname, w, axis in (("w_ff1", w_ff1, 1), ("w_ff2", w_ff2, 0)):
        later[wname], token = _exchange_start(_place_shard(w[0], axis, me, "place_" + wname), axis, True,
                                              "ag_" + wname + "_start", after=after)
        after = (token,)
    pc = _conv_fwd(z, conv_w_f, n_lat, "conv_fwd", token)
    pos = jnp.concatenate([n_ctx + jnp.arange(n_lat, dtype=F32), jnp.arange(n_ctx, dtype=F32)])
    cos, sin = _rope_tables(pos, dk)
    q_s, k_s, v_s = _rotary_fwd(z, cos, sin, n_lat, "rotary_fwd")
    o, st_f, st_b = _ret_fwd(q_s, k_s, v_s, lgs, "ret_fwd")
    r = _gn_fwd(o, z, n_lat, n_ctx, "gn_fwd")
    y_conv = _mm(pc, w_conv_out_f, "nn", [F32], "proj_conv_out", tn=2048)
    tn_d = _tile(d, 1024, 128)
    gate_offs = (CGC * d // tn_d, CGR * d // tn_d)

    def merge(acc, yc, gc, gr):
        return acc, _sig(gc) * yc + _sig(gr) * acc

    y_ret, mg = _mm(r, w_ret_out_f, "nn", [F32, BF16], "proj_ret_out", tm=512, tn=tn_d,
                    extras=[(y_conv, 0), (z, gate_offs[0]), (z, gate_offs[1])], epilogue=merge)
    y_l = _mm(mg, w_o_f, "nn", [F32], "proj_o", tn=2048)
    x1, a2 = _resid_modulate(xl, y_l, vec2, "resid_modulate2")

    def sqrelu(acc):
        return acc, jnp.square(jnp.maximum(acc, 0.0))

    w_ff1_f = _exchange_wait(later["w_ff1"], a2, "ag_w_ff1_wait")
    hff, s = _mm(a2, w_ff1_f, "nn", [BF16, BF16], "ff1", epilogue=sqrelu)
    w_ff2_f = _exchange_wait(later["w_ff2"], s, "ag_w_ff2_wait")
    f = _mm(s, w_ff2_f, "nn", [F32], "ff2")
    dx2, df, acc3 = _loss_head(x1, f, target, vec3, "loss_head")

    def d_sqrelu(acc, h):
        return (acc * (2.0 * jnp.maximum(h.astype(F32), 0.0)),)

    dh = _mm(df, w_ff2_f, "nt", [BF16], "ff2_dx", extras=[(hff, 0)], epilogue=d_sqrelu)
    sent = {}
    sent["w_ff2"], token = _exchange_start(_mm(s, df, "tn", [BF16], "ff2_dw"), 0, False, "rs_w_ff2_start")
    da2 = _mm(dh, w_ff1_f, "nt", [F32], "ff1_dx", dep=token)
    sent["w_ff1"], token = _exchange_start(_mm(a2, dh, "tn", [BF16], "ff1_dw"), 1, False, "rs_w_ff1_start")
    dx1, dyl, acc2 = _modulate_bwd(da2, x1, vec2, "modulate2_bwd", dx_in=dx2, y=y_l)

    def d_merge(acc, yc, yr, gc, gr):
        sc, sr = _sig(gc), _sig(gr)
        return acc * sc, acc * sr, acc * yc * (sc * (1.0 - sc)), acc * yr * (sr * (1.0 - sr))

    dyc, dyr, dgc, dgr = _mm(dyl, w_o_f, "nt", [BF16] * 4, "proj_o_dx", tm=512, tn=tn_d,
                             extras=[(y_conv, 0), (y_ret, 0), (z, gate_offs[0]), (z, gate_offs[1])], epilogue=d_merge,
                             dep=token)
    sent["w_o"], token = _exchange_start(_mm(mg, dyl, "tn", [BF16], "proj_o_dw"), 0, False, "rs_w_o_start")
    dpc = _mm(dyc, w_conv_out_f, "nt", [F32], "proj_conv_out_dx", tn=2048, dep=token)
    sent["w_conv_out"], token = _exchange_start(_mm(pc, dyc, "tn", [BF16], "proj_conv_out_dw"), 0, False,
                                                "rs_w_conv_out_start")
    dz_conv, acc_conv = _conv_bwd(dpc, z, conv_w_f, "conv_bwd")
    dr = _mm(dyr, w_ret_out_f, "nt", [F32], "proj_ret_out_dx", dep=token)
    sent["w_ret_out"], token = _exchange_start(_mm(r, dyr, "tn", [BF16], "proj_ret_out_dw"), 0, False,
                                               "rs_w_ret_out_start")
    do, dz_g = _gn_bwd(dr, o, z, n_ctx, "gn_bwd")
    dq_f, dk_f, dv_f, dlg_f, dq_b, dk_b, dv_b, dlg_b = _ret_bwd(q_s, k_s, v_s, do, st_f, st_b, lgs, "ret_bwd")
    dz = _dz_assemble(dq_f, dq_b, dk_f, dk_b, dv_f, dv_b, cos, sin, dz_conv, dz_g, dgc, dgr, n_lat, n_ctx,
                      "dz_assemble")
    g_in = _mm(a_all, dz, "tn", [BF16], "proj_in_dw", tn=2048, tk=(n_lat + n_ctx) // 4, dep=token)
    sent["w_in"], token = _exchange_start(g_in, 1, False, "rs_w_in_start")
    da_all = _mm(dz, w_in_f, "nt", [F32], "proj_in_dx", tn=2048, dep=token)
    grad_x, acc1 = _modulate_bwd(da_all, xl, vec1_l, "modulate1_bwd", dx_in=dx1)
    _, acc1c = _modulate_bwd(da_all, ctxl, vec1_c, "modulate1_ctx_bwd", da_off=n_lat)

    lane_pad = functools.partial(_pad_lanes, width=d)
    packet = jnp.concatenate([
        acc1[2:3] + acc1c[2:3], acc2[2:3], acc3[1:2],
        acc1[0:1], acc1[1:2], acc2[3:4], acc2[0:1], acc2[1:2], acc3[0:1],
        acc1c[0:1], acc1c[1:2],
        acc_conv[0:3],
        lane_pad(dlg_f[:, 0, 0][None]), lane_pad(dlg_b[:, 0, 0][None]),
        acc3[2:3],
        jnp.zeros((7, d), F32)], axis=0)
    gathered = _small_allgather(packet, "ag_small")
    red, misc = _reduce_small(gathered, dec, d, "reduce_small")
    dmod_ctx = jnp.concatenate([red[9], red[10], jnp.zeros((4 * d,), F32)])
    dmod_all = jnp.concatenate([gathered[:, 3:9, :].reshape(N_DEV, N_MOD * d), dmod_ctx[None]], axis=0)
    dm = jnp.pad(lax.dynamic_slice(dmod_all, (0, me * nb), (N_DEV + 1, nb)), ((0, 7), (0, 0)))
    g_mod, c_part = _mod_grad(cs, dm, w_mod[0], "mod_grad")
    g_c_ctx = _c_ctx_grad(_small_allgather(c_part, "ag_c_ctx"), c_ctx[None], "c_ctx_grad")
    g_b_mod = red[3:9].reshape(1, N_MOD * d) + dmod_ctx[None]
    g_conv_w = lax.dynamic_slice(red[11:14], (0, me * (d // N_DEV)), (3, d // N_DEV))

    res = {"w_mod": _adamw(g_mod[None], w_mod[0], m_w_mod[0], v_w_mod[0], "adamw_w_mod")}
    after = res["w_mod"][0]
    for wname, axis, w, m, v in (("w_ff2", 0, w_ff2, m_w_ff2, v_w_ff2), ("w_ff1", 1, w_ff1, m_w_ff1, v_w_ff1),
                                 ("w_o", 0, w_o, m_w_o, v_w_o), ("w_conv_out", 0, w_conv_out, m_w_conv_out, v_w_conv_out),
                                 ("w_ret_out", 0, w_ret_out, m_w_ret_out, v_w_ret_out),
                                 ("w_in", 1, w_in, m_w_in, v_w_in)):
        g_full, parts = _exchange_wait(sent[wname], after, "rs_" + wname + "_wait")
        res[wname] = _adamw(parts, w[0], m[0], v[0], "adamw_" + wname, own=(g_full, axis, me))
        after = res[wname][0]
    res = {k: tuple(t[None] for t in val) for k, val in res.items()}

    small = [("c_ctx", g_c_ctx, c_ctx, m_c_ctx, v_c_ctx), ("b_mod", g_b_mod, b_mod, m_b_mod, v_b_mod),
             ("norm1_g", red[0:1], norm1_g, m_norm1_g, v_norm1_g), ("conv_w", g_conv_w, conv_w, m_conv_w, v_conv_w),
             ("ret_decay_fwd", misc[1:2, :HEADS], ret_decay_fwd, m_ret_decay_fwd, v_ret_decay_fwd),
             ("ret_decay_bwd", misc[2:3, :HEADS], ret_decay_bwd, m_ret_decay_bwd, v_ret_decay_bwd),
             ("norm2_g", red[1:2], norm2_g, m_norm2_g, v_norm2_g), ("final_g", red[2:3], final_g, m_final_g, v_final_g)]

    def flat(t):
        t = t.reshape(-1)
        return jnp.pad(t, (0, (-t.shape[0]) % 1024))

    packed = [jnp.concatenate([flat(item[j]) for item in small]).reshape(-1, 128) for j in range(1, 5)]
    outs = _adamw(packed[0][None], packed[1], packed[2], packed[3], "adamw_small")
    start = 0
    for name, _, w, _, _ in small:
        size = w.size
        res[name] = tuple(o.reshape(-1)[start:start + size].reshape(w.shape) for o in outs)
        start += size + (-size) % 1024

    order = ["c_ctx", "w_mod", "b_mod", "norm1_g", "w_in", "conv_w", "w_conv_out", "ret_decay_fwd", "ret_decay_bwd",
             "w_ret_out", "w_o", "norm2_g", "w_ff1", "w_ff2", "final_g"]
    loss = misc[0, 0]
    return (loss, grad_x[None], *[res[n][0] for n in order], *[res[n][1] for n in order],
            *[res[n][2] for n in order], *[res[n][3] for n in order])
```

```python
import functools

import jax
import jax.numpy as jnp
from jax import lax
from jax.experimental import pallas as pl
from jax.experimental.pallas import tpu as pltpu

F32 = jnp.float32
BF16 = jnp.bfloat16
MESH = pl.DeviceIdType.MESH

N_DEV = 8
HEADS = 8
N_MOD = 6
N_IN = 11
GRID_W = 64
CHUNK = 256
ROPE_BASE = 10000.0
EPS = 1e-6
ADAM_LR, ADAM_B1, ADAM_B2, ADAM_EPS, ADAM_WD, ADAM_STEP = 0.001, 0.9, 0.999, 1e-08, 0.01, 10
VMEM_LIMIT = 56 * 1024 * 1024
HIGHEST = lax.Precision.HIGHEST
CB, CC, CX, CQ, CK, CV, CG, CGC, CGR = 0, 1, 2, 3, 4, 5, 7, 9, 10


def _tile(n, target, mult):
    t = (min(target, n) // mult) * mult
    while t >= mult:
        if n % t == 0:
            return t
        t -= mult
    return n


def _params(sem=None):
    return pltpu.CompilerParams(dimension_semantics=sem, vmem_limit_bytes=VMEM_LIMIT)


def _sig(v):
    return 1.0 / (1.0 + jnp.exp(-v))


def _coords():
    return lax.axis_index("x"), lax.axis_index("y"), lax.axis_index("c")


def _flip(p, m):
    return tuple(1 - v if (m >> s) & 1 else v for v, s in zip(p, (2, 1, 0)))


def _index(p):
    return 4 * p[0] + 2 * p[1] + p[2]


def _small_allgather(x, name):
    r, n = x.shape

    def body(x_ref, out_ref, send_sems, recv_sems):
        me = _coords()
        out_ref[pl.ds(_index(me), 1)] = x_ref[...][None]
        sent = []
        for m in range(1, N_DEV):
            cp = pltpu.make_async_remote_copy(
                src_ref=x_ref, dst_ref=out_ref.at[_index(me)], send_sem=send_sems.at[m - 1],
                recv_sem=recv_sems.at[m - 1], device_id=_flip(me, m), device_id_type=MESH)
            cp.start()
            sent.append(cp)
        for m in range(1, N_DEV):
            pltpu.make_async_remote_copy(
                src_ref=x_ref, dst_ref=out_ref.at[_index(_flip(me, m))], send_sem=send_sems.at[m - 1],
                recv_sem=recv_sems.at[m - 1], device_id=_flip(me, m), device_id_type=MESH).wait_recv()
        for cp in sent:
            cp.wait_send()

    return pl.pallas_call(
        body, name=name, out_shape=jax.ShapeDtypeStruct((N_DEV, r, n), x.dtype),
        in_specs=[pl.BlockSpec(memory_space=pltpu.VMEM)], out_specs=pl.BlockSpec(memory_space=pltpu.VMEM),
        scratch_shapes=[pltpu.SemaphoreType.DMA((N_DEV - 1,)), pltpu.SemaphoreType.DMA((N_DEV - 1,))],
    )(x)


def _window(ref, j, r, c, axis):
    if axis == 0:
        return ref.at[pl.ds(pl.multiple_of(j * r, 8), r), :]
    return ref.at[:, pl.ds(pl.multiple_of(j * c, 128), c)]


_HBM = pl.BlockSpec(memory_space=pltpu.HBM)
_SEM = pl.BlockSpec(memory_space=pltpu.SEMAPHORE)
_ANY = pl.BlockSpec(memory_space=pl.ANY)
_EFFECT = pltpu.SideEffectType.DATAFLOW_SIDE_EFFECTING


def _place_shard(w, axis, me, name):
    r, c = w.shape
    full = (N_DEV * r, c) if axis == 0 else (r, N_DEV * c)
    tr = _tile(r, max(16, (2 * 1024 * 1024) // (c * 4)), 16)
    blocks = r // tr

    def out_map(i, me_ref):
        return (me_ref[0] * blocks + i, 0) if axis == 0 else (i, me_ref[0])

    def body(me_ref, w_ref, out_ref):
        out_ref[...] = w_ref[...].astype(BF16)

    return pl.pallas_call(
        body, name=name, out_shape=jax.ShapeDtypeStruct(full, BF16),
        grid_spec=pltpu.PrefetchScalarGridSpec(
            num_scalar_prefetch=1, grid=(blocks,), in_specs=[pl.BlockSpec((tr, c), lambda i, me_ref: (i, 0))],
            out_specs=pl.BlockSpec((tr, c), out_map)),
        compiler_params=_params(("parallel",)))(me.reshape(1), w)


def _exchange_start(src, axis, gather, name, after=()):
    r, c = (src.shape[0] // N_DEV, src.shape[1]) if axis == 0 else (src.shape[0], src.shape[1] // N_DEV)
    n_hbm = 1 if gather else 2
    n_after = len(after)

    def body(*refs):
        src_ref, land_ref = refs[0], refs[n_hbm - 1]
        send_sems, recv_sems = refs[n_hbm + n_after:n_hbm + n_after + 2]
        token = refs[-1]
        me = _coords()
        for m in range(1, N_DEV):
            peer = _flip(me, m)
            if gather:
                mine = theirs = _window(land_ref, _index(me), r, c, axis)
            else:
                mine, theirs = _window(src_ref, _index(peer), r, c, axis), land_ref.at[m - 1]
            pltpu.make_async_remote_copy(
                src_ref=mine, dst_ref=theirs, send_sem=send_sems.at[m - 1], recv_sem=recv_sems.at[m - 1],
                device_id=peer, device_id_type=MESH).start()
        token[...] = jnp.zeros_like(token)

    hbm = [pltpu.with_memory_space_constraint(src, pltpu.HBM)]
    if not gather:
        hbm.append(pltpu.with_memory_space_constraint(lax.empty((N_DEV - 1, r, c), src.dtype), pltpu.HBM))
    outs = pl.pallas_call(
        body, name=name,
        out_shape=(pltpu.SemaphoreType.DMA((N_DEV - 1,)), pltpu.SemaphoreType.DMA((N_DEV - 1,)),
                   *[pltpu.HBM(t.shape, t.dtype) for t in hbm], jax.ShapeDtypeStruct((8, 128), F32)),
        in_specs=[_HBM] * n_hbm + [_ANY] * n_after,
        out_specs=(_SEM, _SEM, *[_HBM] * n_hbm, pl.BlockSpec(memory_space=pltpu.VMEM)),
        input_output_aliases={i: 2 + i for i in range(n_hbm)},
        compiler_params=pltpu.CompilerParams(has_side_effects=_EFFECT),
    )(*hbm, *after)
    return (outs[:2], outs[2:2 + n_hbm], (axis, gather, r, c)), outs[-1]


def _exchange_wait(handle, after, name):
    (send_sems, recv_sems), hbm, (axis, gather, r, c) = handle
    n_hbm = len(hbm)

    def body(*refs):
        src_ref, land_ref = refs[0], refs[n_hbm - 1]
        send_sems, recv_sems = refs[n_hbm:n_hbm + 2]
        me = _coords()
        for m in range(1, N_DEV):
            peer = _flip(me, m)
            if gather:
                mine, theirs = _window(land_ref, _index(me), r, c, axis), _window(land_ref, _index(peer), r, c, axis)
            else:
                mine, theirs = _window(src_ref, _index(peer), r, c, axis), land_ref.at[m - 1]
            copy = pltpu.make_async_remote_copy(
                src_ref=mine, dst_ref=theirs, send_sem=send_sems.at[m - 1], recv_sem=recv_sems.at[m - 1],
                device_id=peer, device_id_type=MESH)
            copy.wait_send()
            copy.wait_recv()

    outs = pl.pallas_call(
        body, name=name, out_shape=tuple(pltpu.HBM(t.shape, t.dtype) for t in hbm),
        in_specs=[_HBM] * n_hbm + [_SEM, _SEM, _ANY], out_specs=tuple([_HBM] * n_hbm),
        input_output_aliases={i: i for i in range(n_hbm)},
        compiler_params=pltpu.CompilerParams(has_side_effects=_EFFECT),
    )(*hbm, send_sems, recv_sems, after)
    return outs[0] if gather else tuple(outs)


def _mm(a, b, mode, out_dtypes, name, tm=1024, tn=1024, tk=2048, extras=(), epilogue=None, dep=None):
    if mode == "nn":
        (m, k), n = a.shape, b.shape[1]
    elif mode == "nt":
        (m, k), n = a.shape, b.shape[0]
    else:
        (k, m), n = a.shape, b.shape[1]
    tm, tn = _tile(m, tm, 8), _tile(n, tn, 128)
    tk = _tile(k, tk, 16 if mode == "tn" else 128)
    nk = k // tk
    swap = nk == 1 and (k * n + (n // tn) * m * k) < (m * k + (m // tm) * k * n)

    def ij(p, q):
        return (q, p) if swap else (p, q)

    def spec(shape, fn):
        return pl.BlockSpec(shape, lambda p, q, kk: fn(*ij(p, q), kk))

    a_spec = spec((tk, tm), lambda i, j, kk: (kk, i)) if mode == "tn" else spec((tm, tk), lambda i, j, kk: (i, kk))
    b_spec = spec((tn, tk), lambda i, j, kk: (j, kk)) if mode == "nt" else spec((tk, tn), lambda i, j, kk: (kk, j))
    dims = {"nn": (((1,), (0,)), ((), ())), "nt": (((1,), (1,)), ((), ())), "tn": (((0,), (0,)), ((), ()))}[mode]
    ex_specs = [spec((tm, tn), functools.partial(lambda i, j, kk, off: (i, j + off), off=off)) for _, off in extras]
    deps = [] if dep is None else [dep]
    dep_specs = [pl.BlockSpec(dep.shape, lambda p, q, kk: (0, 0))] if deps else []
    n_ex, n_out = len(extras), len(out_dtypes)
    n_in = 2 + n_ex + len(deps)

    def body(*refs):
        a_ref, b_ref = refs[0], refs[1]
        ex_refs = refs[2:2 + n_ex]
        out_refs = refs[n_in:n_in + n_out]

        def product():
            return lax.dot_general(a_ref[...], b_ref[...], dims, preferred_element_type=F32)

        def finish(res):
            res = epilogue(res, *[e[...] for e in ex_refs]) if epilogue is not None else (res,)
            for o_ref, val in zip(out_refs, res):
                o_ref[...] = val.astype(o_ref.dtype)

        if nk == 1:
            finish(product())
            return
        acc = refs[-1]
        kk = pl.program_id(2)

        @pl.when(kk == 0)
        def _():
            acc[...] = product()

        @pl.when((kk > 0) & (kk < nk - 1))
        def _():
            acc[...] += product()

        @pl.when(kk == nk - 1)
        def _():
            finish(acc[...] + product())

    outs = pl.pallas_call(
        body, name=name, grid=(*ij(m // tm, n // tn), nk),
        in_specs=[a_spec, b_spec] + ex_specs + dep_specs,
        out_specs=[spec((tm, tn), lambda i, j, kk: (i, j)) for _ in out_dtypes],
        out_shape=[jax.ShapeDtypeStruct((m, n), dt) for dt in out_dtypes],
        scratch_shapes=[pltpu.VMEM((tm, tn), F32)] if nk > 1 else [],
        compiler_params=_params(("parallel", "parallel", "arbitrary")),
    )(a, b, *[e for e, _ in extras], *deps)
    return outs if n_out > 1 else outs[0]


def _gather_project(a, w_full, order, name, riders=(), tm=768):
    m, k = a.shape
    c = w_full.shape[1] // N_DEV
    tm = _tile(m, tm, 8)
    n_i = m // tm
    chips = [4, 2, 6]
    n_rid = len(riders)
    shard = [(k, c, 1)] + [((t.shape[0] // N_DEV, t.shape[1], 0) if ax == 0 else (t.shape[0], t.shape[1] // N_DEV, 1))
                           for t, ax in riders]

    def body(order_ref, a_ref, *refs):
        outs = refs[1 + n_rid:2 + 2 * n_rid]
        z_ref, wbuf, send_sems, recv_sems, load_sem = refs[2 + 2 * n_rid:]
        w_ref = outs[0]
        p, i = pl.program_id(0), pl.program_id(1)
        me = _coords()
        sibling = _flip(me, 1)

        def copy(t, s, block, to):
            r_t, c_t, ax_t = shard[t]
            win = _window(outs[t], _index(block), r_t, c_t, ax_t)
            return pltpu.make_async_remote_copy(src_ref=win, dst_ref=win, send_sem=send_sems.at[7 * t + s],
                                                recv_sem=recv_sems.at[7 * t + s], device_id=to, device_id_type=MESH)

        def first(t):
            return [copy(t, 0, me, sibling)] + [copy(t, 1 + j, me, _flip(me, mm)) for j, mm in enumerate(chips)]

        def passed(t):
            return [copy(t, 4 + j, _flip(me, mm), sibling) for j, mm in enumerate(chips)]

        @pl.when((p == 0) & (i == 0))
        def _():
            for t in range(1 + n_rid):
                for cp in first(t):
                    cp.start()

        arrivals = [(0, None)] + [x for j in range(3) for x in ((1 + j, passed(0)[j]), (4 + j, None))]
        for piece, (sem, forward) in enumerate(arrivals, start=1):
            @pl.when((p == piece) & (i == 0))
            def _(sem=sem, forward=forward, piece=piece):
                src_block = (sibling if piece == 1 else
                             _flip(me if piece % 2 == 0 else sibling, chips[(piece - 2) // 2]))
                copy(0, sem, src_block, me).wait_recv()
                if forward is not None:
                    forward.start()

        @pl.when(i == 0)
        def _():
            load = pltpu.make_async_copy(_window(w_ref, order_ref[p], k, c, 1), wbuf, load_sem)
            load.start()
            load.wait()

        @pl.when((p == N_DEV - 1) & (i == 0))
        def _():
            for t in range(1, 1 + n_rid):
                for j, mm in enumerate(chips):
                    copy(t, 1 + j, _flip(me, mm), me).wait_recv()
                    passed(t)[j].start()

        z_ref[...] = jnp.dot(a_ref[...], wbuf[...], preferred_element_type=F32)

        @pl.when((p == N_DEV - 1) & (i == n_i - 1))
        def _():
            for t in range(1, 1 + n_rid):
                copy(t, 0, sibling, me).wait_recv()
                for j, mm in enumerate(chips):
                    copy(t, 4 + j, _flip(sibling, mm), me).wait_recv()
            for t in range(1 + n_rid):
                for cp in first(t) + passed(t):
                    cp.wait_send()

    gathered = [w_full] + [t for t, _ in riders]
    n_sem = 7 * len(gathered)
    return pl.pallas_call(
        body, name=name,
        out_shape=[jax.ShapeDtypeStruct(t.shape, t.dtype) for t in gathered]
        + [jax.ShapeDtypeStruct((m, N_DEV * c), F32)],
        grid_spec=pltpu.PrefetchScalarGridSpec(
            num_scalar_prefetch=1, grid=(N_DEV, n_i),
            in_specs=[pl.BlockSpec((tm, k), lambda p, i, order_ref: (i, 0))]
            + [pl.BlockSpec(memory_space=pl.ANY)] * len(gathered),
            out_specs=[pl.BlockSpec(memory_space=pl.ANY)] * len(gathered)
            + [pl.BlockSpec((tm, c), lambda p, i, order_ref: (i, order_ref[p]))],
            scratch_shapes=[pltpu.VMEM((k, c), w_full.dtype), pltpu.SemaphoreType.DMA((n_sem,)),
                            pltpu.SemaphoreType.DMA((n_sem,)), pltpu.SemaphoreType.DMA(())]),
        input_output_aliases={2 + t: t for t in range(len(gathered))},
        compiler_params=_params(("arbitrary", "arbitrary")))(order, a, *gathered)


def _rspec(tr, w, cb=0, off=0):
    return pl.BlockSpec((tr, w), lambda i: (i + off, cb))


def _cspec(shape):
    return pl.BlockSpec(shape, lambda i: (0,) * len(shape))


def _rms(xf):
    rstd = lax.rsqrt(jnp.mean(xf * xf, axis=-1, keepdims=True) + EPS)
    return xf * rstd, rstd


def _rms_bwd(dn, n, rstd):
    return rstd * (dn - n * jnp.mean(dn * n, axis=-1, keepdims=True))


def _colsum(v):
    return jnp.sum(v, axis=0, keepdims=True)


def _total(v):
    return jnp.sum(jnp.sum(v, axis=1, keepdims=True), axis=0, keepdims=True)


def _modulate(x, vec, name):
    rows, d = x.shape
    tr = _tile(rows, 256, 8)

    def body(x_ref, v_ref, a_ref):
        n, _ = _rms(x_ref[...])
        a_ref[...] = ((n * v_ref[0:1, :]) * (1.0 + v_ref[2:3, :]) + v_ref[1:2, :]).astype(BF16)

    return pl.pallas_call(
        body, name=name, grid=(rows // tr,), in_specs=[_rspec(tr, d), _cspec(vec.shape)],
        out_specs=_rspec(tr, d), out_shape=jax.ShapeDtypeStruct((rows, d), BF16),
        compiler_params=_params(("parallel",)))(x, vec)


def _resid_modulate(x, y, vec, name):
    rows, d = x.shape
    tr = _tile(rows, 256, 8)

    def body(x_ref, y_ref, v_ref, x1_ref, a_ref):
        x1 = x_ref[...] + v_ref[3:4, :] * y_ref[...]
        x1_ref[...] = x1
        n, _ = _rms(x1)
        a_ref[...] = ((n * v_ref[0:1, :]) * (1.0 + v_ref[2:3, :]) + v_ref[1:2, :]).astype(BF16)

    return pl.pallas_call(
        body, name=name, grid=(rows // tr,), in_specs=[_rspec(tr, d), _rspec(tr, d), _cspec(vec.shape)],
        out_specs=[_rspec(tr, d), _rspec(tr, d)],
        out_shape=[jax.ShapeDtypeStruct((rows, d), F32), jax.ShapeDtypeStruct((rows, d), BF16)],
        compiler_params=_params(("parallel",)))(x, y, vec)


def _loss_head(x1, f, target, vec, name):
    rows, d = x1.shape
    tr = _tile(rows, 256, 8)

    def body(x1_ref, f_ref, t_ref, v_ref, dx2_ref, df_ref, acc_ref):
        @pl.when(pl.program_id(0) == 0)
        def _():
            acc_ref[...] = jnp.zeros_like(acc_ref)

        gate, gain = v_ref[0:1, :], v_ref[1:2, :]
        fv = f_ref[...]
        n, rstd = _rms(x1_ref[...] + gate * fv)
        err = n * gain - t_ref[...]
        dy = err / d
        dx2 = _rms_bwd(dy * gain, n, rstd)
        dx2_ref[...] = dx2
        df_ref[...] = (dx2 * gate).astype(BF16)
        acc_ref[0:1, :] += _colsum(dx2 * fv)
        acc_ref[1:2, :] += _colsum(dy * n)
        acc_ref[2:3, :] += _colsum(err * err)

    return pl.pallas_call(
        body, name=name, grid=(rows // tr,),
        in_specs=[_rspec(tr, d), _rspec(tr, d), _rspec(tr, d), _cspec(vec.shape)],
        out_specs=[_rspec(tr, d), _rspec(tr, d), _cspec((8, d))],
        out_shape=[jax.ShapeDtypeStruct((rows, d), F32), jax.ShapeDtypeStruct((rows, d), BF16),
                   jax.ShapeDtypeStruct((8, d), F32)],
        compiler_params=_params(("arbitrary",)))(x1, f, target, vec)


def _modulate_bwd(da, x, vec, name, da_off=0, dx_in=None, y=None):
    rows, d = x.shape
    tr = _tile(rows, 256, 8)
    off = da_off // tr
    has_dx, has_y = dx_in is not None, y is not None

    def body(*refs):
        da_ref, x_ref, v_ref = refs[0], refs[1], refs[2]
        pos = 3
        dxin_ref = refs[pos] if has_dx else None
        pos += has_dx
        y_ref = refs[pos] if has_y else None
        pos += has_y
        dx_ref = refs[pos]
        dy_ref = refs[pos + 1] if has_y else None
        acc_ref = refs[-1]

        @pl.when(pl.program_id(0) == 0)
        def _():
            acc_ref[...] = jnp.zeros_like(acc_ref)

        gain, scale1 = v_ref[0:1, :], 1.0 + v_ref[2:3, :]
        dav = da_ref[...]
        n, rstd = _rms(x_ref[...])
        dx = _rms_bwd(dav * gain * scale1, n, rstd)
        if has_dx:
            dx = dx + dxin_ref[...]
        dx_ref[...] = dx
        acc_ref[0:1, :] += _colsum(dav)
        acc_ref[1:2, :] += _colsum(dav * (n * gain))
        acc_ref[2:3, :] += _colsum(dav * n * scale1)
        if has_y:
            acc_ref[3:4, :] += _colsum(dx * y_ref[...])
            dy_ref[...] = (dx * v_ref[3:4, :]).astype(BF16)

    ins = [da, x, vec] + ([dx_in] if has_dx else []) + ([y] if has_y else [])
    in_specs = [_rspec(tr, d, off=off), _rspec(tr, d), _cspec(vec.shape)] + [_rspec(tr, d)] * (has_dx + has_y)
    out_specs = [_rspec(tr, d)] + ([_rspec(tr, d)] if has_y else []) + [_cspec((8, d))]
    out_shape = ([jax.ShapeDtypeStruct((rows, d), F32)] + ([jax.ShapeDtypeStruct((rows, d), BF16)] if has_y else [])
                 + [jax.ShapeDtypeStruct((8, d), F32)])
    return pl.pallas_call(
        body, name=name, grid=(rows // tr,), in_specs=in_specs, out_specs=out_specs, out_shape=out_shape,
        compiler_params=_params(("arbitrary",)))(*ins)


def _conv_terms(cc, cx, w_ref, tr):
    t = lax.broadcasted_iota(jnp.int32, (tr, 1), 0) % GRID_W
    first, last = t == 0, t == GRID_W - 1
    u = cc * cx
    prev = jnp.where(first, 0.0, pltpu.roll(u, 1, 0))
    nxt = jnp.where(last, 0.0, pltpu.roll(u, tr - 1, 0))
    y = w_ref[0:1, :] * prev + w_ref[1:2, :] * u + w_ref[2:3, :] * nxt
    return u, prev, nxt, y, first, last


def _conv_fwd(z, conv_w, rows, name, dep):
    d = conv_w.shape[1]
    tr = _tile(rows, 256, GRID_W)

    def body(cb_ref, cc_ref, cx_ref, w_ref, dep_ref, out_ref):
        y = _conv_terms(cc_ref[...], cx_ref[...], w_ref, tr)[3]
        out_ref[...] = (cb_ref[...] * y).astype(BF16)

    return pl.pallas_call(
        body, name=name, grid=(rows // tr,),
        in_specs=[_rspec(tr, d, CB), _rspec(tr, d, CC), _rspec(tr, d, CX), _cspec(conv_w.shape), _cspec(dep.shape)],
        out_specs=_rspec(tr, d), out_shape=jax.ShapeDtypeStruct((rows, d), BF16),
        compiler_params=_params(("parallel",)))(z, z, z, conv_w, dep)


def _conv_bwd(dpc, z, conv_w, name):
    rows, d = dpc.shape
    tr = _tile(rows, 256, GRID_W)

    def body(dpc_ref, cb_ref, cc_ref, cx_ref, w_ref, dz_ref, acc_ref):
        @pl.when(pl.program_id(0) == 0)
        def _():
            acc_ref[...] = jnp.zeros_like(acc_ref)

        cc, cx, dpcv = cc_ref[...], cx_ref[...], dpc_ref[...]
        u, prev, nxt, y, first, last = _conv_terms(cc, cx, w_ref, tr)
        dy = dpcv * cb_ref[...]
        dy_next = jnp.where(last, 0.0, pltpu.roll(dy, tr - 1, 0))
        dy_prev = jnp.where(first, 0.0, pltpu.roll(dy, 1, 0))
        du = w_ref[0:1, :] * dy_next + w_ref[1:2, :] * dy + w_ref[2:3, :] * dy_prev
        dz_ref[:, 0:d] = (dpcv * y).astype(BF16)
        dz_ref[:, d:2 * d] = (du * cx).astype(BF16)
        dz_ref[:, 2 * d:3 * d] = (du * cc).astype(BF16)
        acc_ref[0:1, :] += _colsum(dy * prev)
        acc_ref[1:2, :] += _colsum(dy * u)
        acc_ref[2:3, :] += _colsum(dy * nxt)

    return pl.pallas_call(
        body, name=name, grid=(rows // tr,),
        in_specs=[_rspec(tr, d), _rspec(tr, d, CB), _rspec(tr, d, CC), _rspec(tr, d, CX), _cspec(conv_w.shape)],
        out_specs=[_rspec(tr, 3 * d), _cspec((8, d))],
        out_shape=[jax.ShapeDtypeStruct((rows, 3 * d), BF16), jax.ShapeDtypeStruct((8, d), F32)],
        compiler_params=_params(("arbitrary",)))(dpc, z, z, z, conv_w)


def _rotary_fwd(z, cos, sin, n_lat, name):
    rows = z.shape[0]
    d = z.shape[1] // N_IN
    dk = d // HEADS
    half = dk // 2
    tr = _tile(n_lat, 256, 8)
    tr = _tile(rows - n_lat, tr, 8)
    lat_blocks, all_blocks = n_lat // tr, rows // tr
    ctx_blocks = all_blocks - lat_blocks
    scan_rows = rows + ctx_blocks * tr

    def z_block(g):
        return jnp.where(g < all_blocks, g, g - ctx_blocks)

    def in_spec(w, cb=0):
        return pl.BlockSpec((tr, w), lambda g: (z_block(g), cb))

    def out_spec(w):
        return pl.BlockSpec((tr, w), lambda g: (jnp.where(g < all_blocks, g + ctx_blocks, g - all_blocks), 0))

    def body(q_ref, k_ref, v0_ref, v1_ref, cos_ref, sin_ref, qo_ref, ko_ref, vo_ref):
        cs, sn = cos_ref[...], sin_ref[...]
        keep = jnp.where(z_block(pl.program_id(0)) < lat_blocks, 1.0, 0.0)
        for src, dst, scale in ((q_ref, qo_ref, keep), (k_ref, ko_ref, dk ** -0.5)):
            for h in range(HEADS):
                lo, mid, hi = h * dk, h * dk + half, (h + 1) * dk
                t1, t2 = src[:, lo:mid], src[:, mid:hi]
                dst[:, lo:mid] = ((t1 * cs - t2 * sn) * scale).astype(BF16)
                dst[:, mid:hi] = ((t1 * sn + t2 * cs) * scale).astype(BF16)
        vo_ref[:, 0:d] = v0_ref[...].astype(BF16)
        vo_ref[:, d:2 * d] = v1_ref[...].astype(BF16)

    return pl.pallas_call(
        body, name=name, grid=(all_blocks + ctx_blocks,),
        in_specs=[in_spec(d, CQ), in_spec(d, CK), in_spec(d, CV), in_spec(d, CV + 1), in_spec(half), in_spec(half)],
        out_specs=[out_spec(d), out_spec(d), out_spec(2 * d)],
        out_shape=[jax.ShapeDtypeStruct((scan_rows, d), BF16), jax.ShapeDtypeStruct((scan_rows, d), BF16),
                   jax.ShapeDtypeStruct((scan_rows, 2 * d), BF16)],
        compiler_params=_params(("parallel",)))(z, z, z, z, cos, sin)


def _dz_assemble(dq_f, dq_b, dk_f, dk_b, dv_f, dv_b, cos, sin, dz_conv, dz_g, dz_gc, dz_gr, n_lat, n_ctx, name):
    rows = n_lat + n_ctx
    d = dq_f.shape[1]
    dk = d // HEADS
    half = dk // 2
    tr = _tile(n_ctx, 128, 8)
    lat_blocks, ctx_blocks = n_lat // tr, n_ctx // tr

    def fmap(i):
        return (jnp.where(i < lat_blocks, i + ctx_blocks, i - lat_blocks), 0)

    def bmap(i):
        return (i + ctx_blocks, 0)

    def lmap(i):
        return (jnp.minimum(i, lat_blocks - 1), 0)

    def body(qf_ref, qb_ref, kf_ref, kb_ref, vf_ref, vb_ref, cos_ref, sin_ref, c_ref, g_ref, gc_ref, gr_ref, out_ref):
        cs, sn = cos_ref[...], sin_ref[...]
        is_lat = pl.program_id(0) < lat_blocks
        keep = jnp.where(is_lat, 1.0, 0.0)
        for fa, fb, base, scale in ((qf_ref, qb_ref, CQ * d, keep), (kf_ref, kb_ref, CK * d, dk ** -0.5)):
            for h in range(HEADS):
                lo, mid, hi = h * dk, h * dk + half, (h + 1) * dk
                d1 = fa[:, lo:mid].astype(F32) + fb[:, lo:mid].astype(F32)
                d2 = fa[:, mid:hi].astype(F32) + fb[:, mid:hi].astype(F32)
                out_ref[:, base + lo:base + mid] = ((d1 * cs + d2 * sn) * scale).astype(BF16)
                out_ref[:, base + mid:base + hi] = ((d2 * cs - d1 * sn) * scale).astype(BF16)
        out_ref[:, CV * d:CG * d] = (vf_ref[...].astype(F32) + vb_ref[...].astype(F32)).astype(BF16)
        for src, lo, hi in ((c_ref, CB * d, CQ * d), (g_ref, CG * d, CGC * d), (gc_ref, CGC * d, CGR * d),
                            (gr_ref, CGR * d, N_IN * d)):
            out_ref[:, lo:hi] = jnp.where(is_lat, src[...], jnp.zeros_like(src))

    return pl.pallas_call(
        body, name=name, grid=(rows // tr,),
        in_specs=[pl.BlockSpec((tr, d), fmap), pl.BlockSpec((tr, d), bmap), pl.BlockSpec((tr, d), fmap),
                  pl.BlockSpec((tr, d), bmap), pl.BlockSpec((tr, 2 * d), fmap), pl.BlockSpec((tr, 2 * d), bmap),
                  _rspec(tr, half), _rspec(tr, half), pl.BlockSpec((tr, 3 * d), lmap), pl.BlockSpec((tr, 2 * d), lmap),
                  pl.BlockSpec((tr, d), lmap), pl.BlockSpec((tr, d), lmap)],
        out_specs=_rspec(tr, N_IN * d), out_shape=jax.ShapeDtypeStruct((rows, N_IN * d), BF16),
        compiler_params=_params(("parallel",)))(dq_f, dq_b, dk_f, dk_b, dv_f, dv_b, cos, sin, dz_conv, dz_g, dz_gc, dz_gr)


def _gn_fwd(o, z, rows, row_off, name):
    d = z.shape[1] // N_IN
    dv = 2 * d // HEADS
    tr = _tile(row_off, 128, 8)
    off = row_off // tr

    def body(o_ref, g0_ref, g1_ref, r_ref):
        for h in range(HEADS):
            lo, hi = h * dv, (h + 1) * dv
            g_ref, glo = (g0_ref, lo) if hi <= d else (g1_ref, lo - d)
            o = o_ref[:, lo:hi]
            cen = o - jnp.mean(o, axis=-1, keepdims=True)
            on = cen * lax.rsqrt(jnp.mean(cen * cen, axis=-1, keepdims=True) + EPS)
            g = g_ref[:, glo:glo + dv]
            r_ref[:, lo:hi] = (g * _sig(g) * on).astype(BF16)

    return pl.pallas_call(
        body, name=name, grid=(rows // tr,),
        in_specs=[_rspec(tr, 2 * d, off=off), _rspec(tr, d, CG), _rspec(tr, d, CG + 1)],
        out_specs=_rspec(tr, 2 * d), out_shape=jax.ShapeDtypeStruct((rows, 2 * d), BF16),
        compiler_params=_params(("parallel",)))(o, z, z)


def _gn_bwd(dr, o, z, row_off, name):
    rows = dr.shape[0]
    d = z.shape[1] // N_IN
    dv = 2 * d // HEADS
    tr = _tile(row_off, 128, 8)
    off = row_off // tr

    def body(dr_ref, o_ref, g0_ref, g1_ref, do_ref, dg_ref):
        for h in range(HEADS):
            lo, hi = h * dv, (h + 1) * dv
            g_ref, glo = (g0_ref, lo) if hi <= d else (g1_ref, lo - d)
            o = o_ref[:, lo:hi]
            cen = o - jnp.mean(o, axis=-1, keepdims=True)
            rstd = lax.rsqrt(jnp.mean(cen * cen, axis=-1, keepdims=True) + EPS)
            on = cen * rstd
            g = g_ref[:, glo:glo + dv]
            sg = _sig(g)
            drv = dr_ref[:, lo:hi]
            dg_ref[:, lo:hi] = (drv * on * (sg * (1.0 + g * (1.0 - sg)))).astype(BF16)
            don = drv * (g * sg)
            do = rstd * (don - jnp.mean(don, axis=-1, keepdims=True)
                         - on * jnp.mean(don * on, axis=-1, keepdims=True))
            do_ref[:, lo:hi] = do.astype(BF16)

    return pl.pallas_call(
        body, name=name, grid=(rows // tr,),
        in_specs=[_rspec(tr, 2 * d), _rspec(tr, 2 * d, off=off), _rspec(tr, d, CG), _rspec(tr, d, CG + 1)],
        out_specs=[_rspec(tr, 2 * d), _rspec(tr, 2 * d)],
        out_shape=[jax.ShapeDtypeStruct((rows, 2 * d), BF16), jax.ShapeDtypeStruct((rows, 2 * d), BF16)],
        compiler_params=_params(("parallel",)))(dr, o, z, z)


def _decays(lg, rev):
    row = lax.broadcasted_iota(jnp.int32, (CHUNK, CHUNK), 0)
    col = lax.broadcasted_iota(jnp.int32, (CHUNK, CHUNK), 1)
    rel = ((col - row) if rev else (row - col)).astype(F32)
    mask = jnp.where(rel >= 0, jnp.exp(lg * jnp.maximum(rel, 0.0)), 0.0)
    r = lax.broadcasted_iota(jnp.int32, (CHUNK, 1), 0)
    rr = ((CHUNK - 1 - r) if rev else r).astype(F32)
    chunk_decay = jnp.exp(lg * jnp.full((1, 1), float(CHUNK), F32))
    return (rr, rel), mask, jnp.exp(lg * (rr + 1.0)), jnp.exp(lg * (CHUNK - 1.0 - rr)), chunk_decay


def _decay_scratch():
    square, column = pltpu.VMEM((2, CHUNK, CHUNK), F32), pltpu.VMEM((2, CHUNK, 1), F32)
    return [square, square, column, column, column]


def _head_decays(lg_ref, h, first_chunk, tables):
    rel_t, mask_t, rr_t, qd_t, kd_t = tables

    @pl.when(first_chunk)
    def _():
        for rev in (0, 1):
            (rr, rel), mask, qd, kd, _ = _decays(lg_ref[rev, h], bool(rev))
            rel_t[rev], mask_t[rev], rr_t[rev], qd_t[rev], kd_t[rev] = rel, mask, rr, qd, kd

    def of(rev):
        chunk_decay = jnp.exp(lg_ref[rev, h] * jnp.full((1, 1), float(CHUNK), F32))
        return (rr_t[rev], rel_t[rev]), mask_t[rev], qd_t[rev], kd_t[rev], chunk_decay

    return of


_NT = (((1,), (1,)), ((), ()))
_TN = (((0,), (0,)), ((), ()))


def _dot(a, b, dims=None):
    if dims is None:
        return jnp.dot(a, b, preferred_element_type=F32)
    return lax.dot_general(a, b, dims, preferred_element_type=F32)


def _ret_fwd(q, k, v, lgs, name):
    tt, d = q.shape
    dk, dv = d // HEADS, 2 * d // HEADS
    nc = tt // CHUNK

    def chunk_of(rev, i):
        return (nc - 1 - i) if rev else i

    def cmap(rev):
        return lambda h, i: (chunk_of(rev, i), h)

    def smap(rev):
        return lambda h, i: (chunk_of(rev, i), h, 0, 0)

    def body(lg_ref, qf_ref, kf_ref, vf_ref, qb_ref, kb_ref, vb_ref, o_ref, sf_ref, sb_ref, state_f, state_b, *tables):
        h, i = pl.program_id(0), pl.program_id(1)
        decays = _head_decays(lg_ref, h, i == 0, tables)

        @pl.when(i == 0)
        def _():
            state_f[...] = jnp.zeros_like(state_f)
            state_b[...] = jnp.zeros_like(state_b)

        for rev, (q_ref, k_ref, v_ref, s_ref, state) in enumerate((
                (qf_ref, kf_ref, vf_ref, sf_ref, state_f), (qb_ref, kb_ref, vb_ref, sb_ref, state_b))):
            _, mask, qd, kd, cd = decays(rev)
            qv, kv, vv = q_ref[...], k_ref[...], v_ref[...]
            st = state[...]
            p = _dot(qv, kv, _NT) * mask
            out = _dot(p.astype(BF16), vv) + _dot((qv * qd).astype(BF16), st.astype(BF16))
            s_ref[...] = st
            state[...] = cd * st + _dot((kv * kd).astype(BF16), vv, _TN)
            rows = pl.ds(pl.multiple_of(chunk_of(bool(rev), i) * CHUNK, CHUNK), CHUNK)
            first = (2 * i < nc - 1) if rev else (2 * i <= nc - 1)

            @pl.when(first)
            def _():
                o_ref[rows, :] = out

            @pl.when(jnp.logical_not(first))
            def _():
                o_ref[rows, :] += out

    def specs(rev):
        return [pl.BlockSpec((CHUNK, dk), cmap(rev)), pl.BlockSpec((CHUNK, dk), cmap(rev)),
                pl.BlockSpec((CHUNK, dv), cmap(rev))]

    state_shape = jax.ShapeDtypeStruct((nc, HEADS, dk, dv), F32)
    return pl.pallas_call(
        body, name=name, grid=(HEADS, nc),
        in_specs=[pl.BlockSpec(memory_space=pltpu.SMEM)] + specs(False) + specs(True),
        out_specs=[pl.BlockSpec((tt, dv), lambda h, i: (0, h)), pl.BlockSpec((None, None, dk, dv), smap(False)),
                   pl.BlockSpec((None, None, dk, dv), smap(True))],
        out_shape=[jax.ShapeDtypeStruct((tt, 2 * d), F32), state_shape, state_shape],
        scratch_shapes=[pltpu.VMEM((dk, dv), F32), pltpu.VMEM((dk, dv), F32)] + _decay_scratch(),
        compiler_params=_params(("parallel", "arbitrary")))(lgs, q, k, v, q, k, v)


def _ret_bwd(q, k, v, do, states_f, states_b, lgs, name):
    tt, d = q.shape
    dk, dv = d // HEADS, 2 * d // HEADS
    nc = tt // CHUNK
    lat_chunks = do.shape[0] // CHUNK
    ctx_chunks = (nc - lat_chunks) // 2

    def chunk_of(rev, i):
        return i if rev else nc - 1 - i

    def cmap(rev):
        return lambda h, i: (chunk_of(rev, i), h)

    def do_map(rev):
        return lambda h, i: (jnp.clip(chunk_of(rev, i) - ctx_chunks, 0, lat_chunks - 1), h)

    def smap(rev):
        return lambda h, i: (chunk_of(rev, i), h, 0, 0)

    def body(lg_ref, *refs):
        h = pl.program_id(0)
        dstates = refs[18:20]
        decays = _head_decays(lg_ref, h, pl.program_id(1) == 0, refs[20:])

        @pl.when(pl.program_id(1) == 0)
        def _():
            for rev in (0, 1):
                dstates[rev][...] = jnp.zeros_like(dstates[rev])
                refs[10 + 4 * rev + 3][...] = jnp.zeros_like(refs[10 + 4 * rev + 3])

        for rev in (0, 1):
            q_ref, k_ref, v_ref, do_ref, s_ref = refs[5 * rev:5 * rev + 5]
            dq_ref, dk_ref, dv_ref, dlg_ref = refs[10 + 4 * rev:10 + 4 * rev + 4]
            dstate = dstates[rev]
            chunk = chunk_of(bool(rev), pl.program_id(1))
            is_lat = (chunk >= ctx_chunks) & (chunk < ctx_chunks + lat_chunks)
            (rr, rel), mask, qd, kd, cd = decays(rev)
            qv, kv, vv = q_ref[...], k_ref[...], v_ref[...]
            dov = jnp.where(is_lat, do_ref[...], jnp.zeros_like(do_ref))
            st, dst = s_ref[...], dstate[...]
            st_b, dst_b = st.astype(BF16), dst.astype(BF16)
            p = _dot(qv, kv, _NT) * mask
            dp = _dot(dov, vv, _NT)
            da = (dp * mask).astype(BF16)
            dq_state = _dot(dov, st_b, _NT) * qd
            dk_state = _dot(vv, dst_b, _NT) * kd
            dq_ref[...] = (_dot(da, kv) + dq_state).astype(BF16)
            dk_ref[...] = (_dot(da, qv, _TN) + dk_state).astype(BF16)
            dv_ref[...] = (_dot(p.astype(BF16), dov, _TN) + _dot((kv * kd).astype(BF16), dst_b)).astype(BF16)
            dnew = cd * dst + _dot((qv * qd).astype(BF16), dov, _TN)
            dstate[...] = dnew
            through = rr * (jnp.sum(qv.astype(F32) * dq_state, axis=1, keepdims=True)
                            - jnp.sum(kv.astype(F32) * dk_state, axis=1, keepdims=True))
            dlg_ref[...] += _total(rel * p * dp) + _total(through) + CHUNK * _total(st * dnew)

    def specs(rev):
        return [pl.BlockSpec((CHUNK, dk), cmap(rev)), pl.BlockSpec((CHUNK, dk), cmap(rev)),
                pl.BlockSpec((CHUNK, dv), cmap(rev)), pl.BlockSpec((CHUNK, dv), do_map(rev)),
                pl.BlockSpec((None, None, dk, dv), smap(rev))]

    def outs(rev):
        return [pl.BlockSpec((CHUNK, dk), cmap(rev)), pl.BlockSpec((CHUNK, dk), cmap(rev)),
                pl.BlockSpec((CHUNK, dv), cmap(rev)), pl.BlockSpec((None, 8, 128), lambda h, i: (h, 0, 0))]

    return pl.pallas_call(
        body, name=name, grid=(HEADS, nc),
        in_specs=[pl.BlockSpec(memory_space=pltpu.SMEM)] + specs(False) + specs(True),
        out_specs=outs(False) + outs(True),
        out_shape=[jax.ShapeDtypeStruct((tt, d), BF16), jax.ShapeDtypeStruct((tt, d), BF16),
                   jax.ShapeDtypeStruct((tt, 2 * d), BF16), jax.ShapeDtypeStruct((HEADS, 8, 128), F32)] * 2,
        scratch_shapes=[pltpu.VMEM((dk, dv), F32), pltpu.VMEM((dk, dv), F32)] + _decay_scratch(),
        compiler_params=_params(("parallel", "arbitrary")))(lgs, q, k, v, do, states_f, q, k, v, do, states_b)


def _silu(v):
    return v * _sig(v)


def _mod_proj(cs, w_mod, b_loc, dec, name):
    nb = w_mod.shape[1]

    def body(cs_ref, w_ref, b_ref, dec_ref, out_ref, lg_ref):
        out_ref[...] = jnp.dot(_silu(cs_ref[...]), w_ref[...], preferred_element_type=F32, precision=HIGHEST) + b_ref[...]
        a = dec_ref[...]
        lg_ref[...] = jnp.minimum(a, 0.0) - jnp.log1p(jnp.exp(-jnp.abs(a)))

    return pl.pallas_call(
        body, name=name,
        out_shape=[jax.ShapeDtypeStruct((16, nb), F32), jax.ShapeDtypeStruct(dec.shape, F32)],
        compiler_params=_params())(cs, w_mod, b_loc, dec)


def _mod_grad(cs, dm, w_mod, name):
    d, nb = w_mod.shape

    def body(cs_ref, dm_ref, w_ref, gw_ref, part_ref):
        dmv = dm_ref[...]
        gw_ref[...] = lax.dot_general(_silu(cs_ref[...]), dmv, _TN, preferred_element_type=F32, precision=HIGHEST)
        part_ref[...] = lax.dot_general(dmv, w_ref[...], _NT, preferred_element_type=F32, precision=HIGHEST)

    return pl.pallas_call(
        body, name=name,
        out_shape=[jax.ShapeDtypeStruct((d, nb), F32), jax.ShapeDtypeStruct((16, d), F32)],
        compiler_params=_params())(cs, dm, w_mod)


def _reduce_small(gathered, dec, n_feat, name):
    _, rows, d = gathered.shape

    def body(g_ref, dec_ref, red_ref, misc_ref):
        total = g_ref[0]
        for i in range(1, N_DEV):
            total = total + g_ref[i]
        red_ref[...] = total
        misc_ref[...] = jnp.zeros_like(misc_ref)
        misc_ref[0:1, :] = jnp.zeros((1, 128), F32) + (0.5 / n_feat) * _total(total[16:17, :])
        misc_ref[1:3, :] = total[14:16, 0:128] * _sig(-dec_ref[0:2, :])

    return pl.pallas_call(
        body, name=name,
        out_shape=[jax.ShapeDtypeStruct((rows, d), F32), jax.ShapeDtypeStruct((8, 128), F32)],
        compiler_params=_params())(gathered, dec)


def _c_ctx_grad(parts, c_ctx, name):
    d = c_ctx.shape[1]

    def body(p_ref, c_ref, out_ref):
        total = p_ref[0]
        for i in range(1, N_DEV):
            total = total + p_ref[i]
        cv = c_ref[...]
        sg = _sig(cv)
        out_ref[...] = total[8:9, :] * (sg * (1.0 + cv * (1.0 - sg)))

    return pl.pallas_call(body, name=name, out_shape=jax.ShapeDtypeStruct((1, d), F32),
                          compiler_params=_params())(parts, c_ctx)


def _adamw(parts, w, m, v, name, own=None):
    n_parts, rows, cols = parts.shape
    row_bytes = cols * (parts.dtype.itemsize * (n_parts + 1) + 7 * 4)
    tr = _tile(rows, max(16, (8 * 1024 * 1024) // row_bytes), 16 if rows % 16 == 0 else 8)
    blocks = rows // tr

    def body(*refs):
        p_ref, w_ref, m_ref, v_ref, g_ref, d_ref, mo_ref, vo_ref = refs[-8:]
        if own is None:
            g = p_ref[0].astype(F32)
        else:
            g = refs[1][...].astype(F32) + p_ref[0].astype(F32)
        for i in range(1, n_parts):
            g = g + p_ref[i].astype(F32)
        m2 = ADAM_B1 * m_ref[...] + (1.0 - ADAM_B1) * g
        v2 = ADAM_B2 * v_ref[...] + (1.0 - ADAM_B2) * jnp.square(g)
        m_hat = m2 / (1.0 - ADAM_B1 ** ADAM_STEP)
        v_hat = v2 / (1.0 - ADAM_B2 ** ADAM_STEP)
        g_ref[...] = g
        d_ref[...] = -ADAM_LR * (m_hat / (jnp.sqrt(v_hat) + ADAM_EPS) + ADAM_WD * w_ref[...])
        mo_ref[...] = m2
        vo_ref[...] = v2

    out_shape = [jax.ShapeDtypeStruct((rows, cols), F32)] * 4
    if own is None:
        spec = _rspec(tr, cols)
        return pl.pallas_call(
            body, name=name, grid=(blocks,),
            in_specs=[pl.BlockSpec((n_parts, tr, cols), lambda i: (0, i, 0)), spec, spec, spec],
            out_specs=[spec] * 4, out_shape=out_shape, compiler_params=_params(("parallel",)))(parts, w, m, v)
    g_full, axis, me = own

    def own_map(i, me_ref):
        return (me_ref[0] * blocks + i, 0) if axis == 0 else (i, me_ref[0])

    spec = pl.BlockSpec((tr, cols), lambda i, me_ref: (i, 0))
    return pl.pallas_call(
        body, name=name, out_shape=out_shape,
        grid_spec=pltpu.PrefetchScalarGridSpec(
            num_scalar_prefetch=1, grid=(blocks,),
            in_specs=[pl.BlockSpec((tr, cols), own_map),
                      pl.BlockSpec((n_parts, tr, cols), lambda i, me_ref: (0, i, 0)), spec, spec, spec],
            out_specs=[spec] * 4),
        compiler_params=_params(("parallel",)))(me.reshape(1), g_full, parts, w, m, v)


def _rope_tables(pos, dk):
    half = dk // 2
    inv_freq = 1.0 / (ROPE_BASE ** jnp.linspace(0.0, 1.0, half, dtype=F32))
    ang = pos[:, None] * inv_freq[None, :]
    return jnp.cos(ang), jnp.sin(ang)


def _pad_lanes(v, width):
    return jnp.pad(v, ((0, 0), (0, width - v.shape[1])))


def kernel(x, c, ctx, c_ctx, w_mod, b_mod, norm1_g, w_in, conv_w, w_conv_out, ret_decay_fwd, ret_decay_bwd, w_ret_out, w_o, norm2_g, w_ff1, w_ff2, final_g, loss_target, m_c_ctx, m_w_mod, m_b_mod, m_norm1_g, m_w_in, m_conv_w, m_w_conv_out, m_ret_decay_fwd, m_ret_decay_bwd, m_w_ret_out, m_w_o, m_norm2_g, m_w_ff1, m_w_ff2, m_final_g, v_c_ctx, v_w_mod, v_b_mod, v_norm1_g, v_w_in, v_conv_w, v_w_conv_out, v_ret_decay_fwd, v_ret_decay_bwd, v_w_ret_out, v_w_o, v_norm2_g, v_w_ff1, v_w_ff2, v_final_g):
    n_lat, d = x.shape[1], x.shape[2]
    n_ctx = ctx.shape[1]
    assert n_ctx % CHUNK == 0 and n_lat % CHUNK == 0 and n_lat % GRID_W == 0
    dk = d // HEADS
    nb = w_mod.shape[2]
    me = 4 * lax.axis_index("x") + 2 * lax.axis_index("y") + lax.axis_index("c")
    xl, ctxl, target = x[0], ctx[0], loss_target[0]

    conv_w_f = _small_allgather(conv_w[0], "ag_conv_w").transpose(1, 0, 2).reshape(3, d)

    c_all = _small_allgather(c, "ag_c").reshape(N_DEV, d)
    cs = jnp.concatenate([c_all, c_ctx[None], jnp.zeros((7, d), F32)], axis=0)
    dec = jnp.pad(jnp.concatenate([ret_decay_fwd, ret_decay_bwd], axis=0), ((0, 6), (0, 128 - HEADS)))
    b_loc = lax.dynamic_slice(b_mod, (0, me * nb), (1, nb))
    modp, lgs = _mod_proj(cs, w_mod[0], b_loc, dec, "mod_proj")
    modp_all = _small_allgather(modp, "ag_mod")
    mod_l = lax.dynamic_index_in_dim(modp_all, me, axis=1, keepdims=False).reshape(N_MOD, d)
    mod_c = modp_all[:, 8, :].reshape(N_MOD, d)
    lgs = lgs[0:2, :HEADS]
    zero_row = jnp.zeros((1, d), F32)
    vec1_l = jnp.concatenate([norm1_g, mod_l[0:1], mod_l[1:2], zero_row], axis=0)
    vec1_c = jnp.concatenate([norm1_g, mod_c[0:1], mod_c[1:2], zero_row], axis=0)
    vec2 = jnp.concatenate([norm2_g, mod_l[3:4], mod_l[4:5], mod_l[2:3]], axis=0)
    vec3 = jnp.concatenate([mod_l[5:6], final_g[None]], axis=0)

    a_all = jnp.concatenate([_modulate(xl, vec1_l, "modulate1"), _modulate(ctxl, vec1_c, "modulate1_ctx")], axis=0)
    arrival = me ^ jnp.array([0, 1, 4, 5, 2, 3, 6, 7], jnp.int32)
    riders = [(_place_shard(w[0], 0, me, "place_" + wname), 0)
              for wname, w in (("w_conv_out", w_conv_out), ("w_ret_out", w_ret_out), ("w_o", w_o))]
    w_in_f, w_conv_out_f, w_ret_out_f, w_o_f, z = _gather_project(
        a_all, _place_shard(w_in[0], 1, me, "place_w_in"), arrival, "proj_in", riders=riders)

    later, after = {}, (z, conv_w_f)
    for wname, w, axis in (("w_ff1", w_ff1, 1), ("w_ff2", w_ff2, 0)):
        later[wname], token = _exchange_start(_place_shard(w[0], axis, me, "place_" + wname), axis, True,
                                              "ag_" + wname + "_start", after=after)
        after = (token,)
    pc = _conv_fwd(z, conv_w_f, n_lat, "conv_fwd", token)
    pos = jnp.concatenate([n_ctx + jnp.arange(n_lat, dtype=F32), jnp.arange(n_ctx, dtype=F32)])
    cos, sin = _rope_tables(pos, dk)
    q_s, k_s, v_s = _rotary_fwd(z, cos, sin, n_lat, "rotary_fwd")
    o, st_f, st_b = _ret_fwd(q_s, k_s, v_s, lgs, "ret_fwd")
    r = _gn_fwd(o, z, n_lat, n_ctx, "gn_fwd")
    y_conv = _mm(pc, w_conv_out_f, "nn", [F32], "proj_conv_out", tn=2048)
    tn_d = _tile(d, 1024, 128)
    gate_offs = (CGC * d // tn_d, CGR * d // tn_d)

    def merge(acc, yc, gc, gr):
        return acc, _sig(gc) * yc + _sig(gr) * acc

    y_ret, mg = _mm(r, w_ret_out_f, "nn", [F32, BF16], "proj_ret_out", tm=512, tn=tn_d,
                    extras=[(y_conv, 0), (z, gate_offs[0]), (z, gate_offs[1])], epilogue=merge)
    y_l = _mm(mg, w_o_f, "nn", [F32], "proj_o", tn=2048)
    x1, a2 = _resid_modulate(xl, y_l, vec2, "resid_modulate2")

    def sqrelu(acc):
        return acc, jnp.square(jnp.maximum(acc, 0.0))

    w_ff1_f = _exchange_wait(later["w_ff1"], a2, "ag_w_ff1_wait")
    hff, s = _mm(a2, w_ff1_f, "nn", [BF16, BF16], "ff1", epilogue=sqrelu)
    w_ff2_f = _exchange_wait(later["w_ff2"], s, "ag_w_ff2_wait")
    f = _mm(s, w_ff2_f, "nn", [F32], "ff2")
    dx2, df, acc3 = _loss_head(x1, f, target, vec3, "loss_head")

    def d_sqrelu(acc, h):
        return (acc * (2.0 * jnp.maximum(h.astype(F32), 0.0)),)

    dh = _mm(df, w_ff2_f, "nt", [BF16], "ff2_dx", extras=[(hff, 0)], epilogue=d_sqrelu)
    sent = {}
    sent["w_ff2"], token = _exchange_start(_mm(s, df, "tn", [BF16], "ff2_dw"), 0, False, "rs_w_ff2_start")
    da2 = _mm(dh, w_ff1_f, "nt", [F32], "ff1_dx", dep=token)
    sent["w_ff1"], token = _exchange_start(_mm(a2, dh, "tn", [BF16], "ff1_dw"), 1, False, "rs_w_ff1_start")
    dx1, dyl, acc2 = _modulate_bwd(da2, x1, vec2, "modulate2_bwd", dx_in=dx2, y=y_l)

    def d_merge(acc, yc, yr, gc, gr):
        sc, sr = _sig(gc), _sig(gr)
        return acc * sc, acc * sr, acc * yc * (sc * (1.0 - sc)), acc * yr * (sr * (1.0 - sr))

    dyc, dyr, dgc, dgr = _mm(dyl, w_o_f, "nt", [BF16] * 4, "proj_o_dx", tm=512, tn=tn_d,
                             extras=[(y_conv, 0), (y_ret, 0), (z, gate_offs[0]), (z, gate_offs[1])], epilogue=d_merge,
                             dep=token)
    sent["w_o"], token = _exchange_start(_mm(mg, dyl, "tn", [BF16], "proj_o_dw"), 0, False, "rs_w_o_start")
    dpc = _mm(dyc, w_conv_out_f, "nt", [F32], "proj_conv_out_dx", tn=2048, dep=token)
    sent["w_conv_out"], token = _exchange_start(_mm(pc, dyc, "tn", [BF16], "proj_conv_out_dw"), 0, False,
                                                "rs_w_conv_out_start")
    dz_conv, acc_conv = _conv_bwd(dpc, z, conv_w_f, "conv_bwd")
    dr = _mm(dyr, w_ret_out_f, "nt", [F32], "proj_ret_out_dx", dep=token)
    sent["w_ret_out"], token = _exchange_start(_mm(r, dyr, "tn", [BF16], "proj_ret_out_dw"), 0, False,
                                               "rs_w_ret_out_start")
    do, dz_g = _gn_bwd(dr, o, z, n_ctx, "gn_bwd")
    dq_f, dk_f, dv_f, dlg_f, dq_b, dk_b, dv_b, dlg_b = _ret_bwd(q_s, k_s, v_s, do, st_f, st_b, lgs, "ret_bwd")
    dz = _dz_assemble(dq_f, dq_b, dk_f, dk_b, dv_f, dv_b, cos, sin, dz_conv, dz_g, dgc, dgr, n_lat, n_ctx,
                      "dz_assemble")
    g_in = _mm(a_all, dz, "tn", [BF16], "proj_in_dw", tn=2048, tk=(n_lat + n_ctx) // 4, dep=token)
    sent["w_in"], token = _exchange_start(g_in, 1, False, "rs_w_in_start")
    da_all = _mm(dz, w_in_f, "nt", [F32], "proj_in_dx", tn=2048, dep=token)
    grad_x, acc1 = _modulate_bwd(da_all, xl, vec1_l, "modulate1_bwd", dx_in=dx1)
    _, acc1c = _modulate_bwd(da_all, ctxl, vec1_c, "modulate1_ctx_bwd", da_off=n_lat)

    lane_pad = functools.partial(_pad_lanes, width=d)
    packet = jnp.concatenate([
        acc1[2:3] + acc1c[2:3], acc2[2:3], acc3[1:2],
        acc1[0:1], acc1[1:2], acc2[3:4], acc2[0:1], acc2[1:2], acc3[0:1],
        acc1c[0:1], acc1c[1:2],
        acc_conv[0:3],
        lane_pad(dlg_f[:, 0, 0][None]), lane_pad(dlg_b[:, 0, 0][None]),
        acc3[2:3],
        jnp.zeros((7, d), F32)], axis=0)
    gathered = _small_allgather(packet, "ag_small")
    red, misc = _reduce_small(gathered, dec, d, "reduce_small")
    dmod_ctx = jnp.concatenate([red[9], red[10], jnp.zeros((4 * d,), F32)])
    dmod_all = jnp.concatenate([gathered[:, 3:9, :].reshape(N_DEV, N_MOD * d), dmod_ctx[None]], axis=0)
    dm = jnp.pad(lax.dynamic_slice(dmod_all, (0, me * nb), (N_DEV + 1, nb)), ((0, 7), (0, 0)))
    g_mod, c_part = _mod_grad(cs, dm, w_mod[0], "mod_grad")
    g_c_ctx = _c_ctx_grad(_small_allgather(c_part, "ag_c_ctx"), c_ctx[None], "c_ctx_grad")
    g_b_mod = red[3:9].reshape(1, N_MOD * d) + dmod_ctx[None]
    g_conv_w = lax.dynamic_slice(red[11:14], (0, me * (d // N_DEV)), (3, d // N_DEV))

    res = {"w_mod": _adamw(g_mod[None], w_mod[0], m_w_mod[0], v_w_mod[0], "adamw_w_mod")}
    after = res["w_mod"][0]
    for wname, axis, w, m, v in (("w_ff2", 0, w_ff2, m_w_ff2, v_w_ff2), ("w_ff1", 1, w_ff1, m_w_ff1, v_w_ff1),
                                 ("w_o", 0, w_o, m_w_o, v_w_o), ("w_conv_out", 0, w_conv_out, m_w_conv_out, v_w_conv_out),
                                 ("w_ret_out", 0, w_ret_out, m_w_ret_out, v_w_ret_out),
                                 ("w_in", 1, w_in, m_w_in, v_w_in)):
        g_full, parts = _exchange_wait(sent[wname], after, "rs_" + wname + "_wait")
        res[wname] = _adamw(parts, w[0], m[0], v[0], "adamw_" + wname, own=(g_full, axis, me))
        after = res[wname][0]
    res = {k: tuple(t[None] for t in val) for k, val in res.items()}

    small = [("c_ctx", g_c_ctx, c_ctx, m_c_ctx, v_c_ctx), ("b_mod", g_b_mod, b_mod, m_b_mod, v_b_mod),
             ("norm1_g", red[0:1], norm1_g, m_norm1_g, v_norm1_g), ("conv_w", g_conv_w, conv_w, m_conv_w, v_conv_w),
             ("ret_decay_fwd", misc[1:2, :HEADS], ret_decay_fwd, m_ret_decay_fwd, v_ret_decay_fwd),
             ("ret_decay_bwd", misc[2:3, :HEADS], ret_decay_bwd, m_ret_decay_bwd, v_ret_decay_bwd),
             ("norm2_g", red[1:2], norm2_g, m_norm2_g, v_norm2_g), ("final_g", red[2:3], final_g, m_final_g, v_final_g)]

    def flat(t):
        t = t.reshape(-1)
        return jnp.pad(t, (0, (-t.shape[0]) % 1024))

    packed = [jnp.concatenate([flat(item[j]) for item in small]).reshape(-1, 128) for j in range(1, 5)]
    outs = _adamw(packed[0][None], packed[1], packed[2], packed[3], "adamw_small")
    start = 0
    for name, _, w, _, _ in small:
        size = w.size
        res[name] = tuple(o.reshape(-1)[start:start + size].reshape(w.shape) for o in outs)
        start += size + (-size) % 1024

    order = ["c_ctx", "w_mod", "b_mod", "norm1_g", "w_in", "conv_w", "w_conv_out", "ret_decay_fwd", "ret_decay_bwd",
             "w_ret_out", "w_o", "norm2_g", "w_ff1", "w_ff2", "final_g"]
    loss = misc[0, 0]
    return (loss, grad_x[None], *[res[n][0] for n in order], *[res[n][1] for n in order],
            *[res[n][2] for n in order], *[res[n][3] for n in order])
```

```python
import functools

import jax
import jax.numpy as jnp
from jax import lax
from jax.experimental import pallas as pl
from jax.experimental.pallas import tpu as pltpu

F32 = jnp.float32
BF16 = jnp.bfloat16
MESH = pl.DeviceIdType.MESH

N_DEV = 8
HEADS = 8
N_MOD = 6
N_IN = 11
GRID_W = 64
CHUNK = 256
ROPE_BASE = 10000.0
EPS = 1e-6
ADAM_LR, ADAM_B1, ADAM_B2, ADAM_EPS, ADAM_WD, ADAM_STEP = 0.001, 0.9, 0.999, 1e-08, 0.01, 10
VMEM_LIMIT = 56 * 1024 * 1024
HIGHEST = lax.Precision.HIGHEST
CB, CC, CX, CQ, CK, CV, CG, CGC, CGR = 0, 1, 2, 3, 4, 5, 7, 9, 10


def _tile(n, target, mult):
    t = (min(target, n) // mult) * mult
    while t >= mult:
        if n % t == 0:
            return t
        t -= mult
    return n


def _params(sem=None):
    return pltpu.CompilerParams(dimension_semantics=sem, vmem_limit_bytes=VMEM_LIMIT)


def _sig(v):
    return 1.0 / (1.0 + jnp.exp(-v))


def _coords():
    return lax.axis_index("x"), lax.axis_index("y"), lax.axis_index("c")


def _flip(p, m):
    return tuple(1 - v if (m >> s) & 1 else v for v, s in zip(p, (2, 1, 0)))


def _index(p):
    return 4 * p[0] + 2 * p[1] + p[2]


def _small_allgather(x, name):
    r, n = x.shape

    def body(x_ref, out_ref, send_sems, recv_sems):
        me = _coords()
        out_ref[pl.ds(_index(me), 1)] = x_ref[...][None]
        sent = []
        for m in range(1, N_DEV):
            cp = pltpu.make_async_remote_copy(
                src_ref=x_ref, dst_ref=out_ref.at[_index(me)], send_sem=send_sems.at[m - 1],
                recv_sem=recv_sems.at[m - 1], device_id=_flip(me, m), device_id_type=MESH)
            cp.start()
            sent.append(cp)
        for m in range(1, N_DEV):
            pltpu.make_async_remote_copy(
                src_ref=x_ref, dst_ref=out_ref.at[_index(_flip(me, m))], send_sem=send_sems.at[m - 1],
                recv_sem=recv_sems.at[m - 1], device_id=_flip(me, m), device_id_type=MESH).wait_recv()
        for cp in sent:
            cp.wait_send()

    return pl.pallas_call(
        body, name=name, out_shape=jax.ShapeDtypeStruct((N_DEV, r, n), x.dtype),
        in_specs=[pl.BlockSpec(memory_space=pltpu.VMEM)], out_specs=pl.BlockSpec(memory_space=pltpu.VMEM),
        scratch_shapes=[pltpu.SemaphoreType.DMA((N_DEV - 1,)), pltpu.SemaphoreType.DMA((N_DEV - 1,))],
    )(x)


def _window(ref, j, r, c, axis):
    if axis == 0:
        return ref.at[pl.ds(pl.multiple_of(j * r, 8), r), :]
    return ref.at[:, pl.ds(pl.multiple_of(j * c, 128), c)]


_HBM = pl.BlockSpec(memory_space=pltpu.HBM)
_SEM = pl.BlockSpec(memory_space=pltpu.SEMAPHORE)
_ANY = pl.BlockSpec(memory_space=pl.ANY)
_EFFECT = pltpu.SideEffectType.DATAFLOW_SIDE_EFFECTING


def _place_shard(w, axis, me, name):
    r, c = w.shape
    full = (N_DEV * r, c) if axis == 0 else (r, N_DEV * c)
    tr = _tile(r, max(16, (2 * 1024 * 1024) // (c * 4)), 16)
    blocks = r // tr

    def out_map(i, me_ref):
        return (me_ref[0] * blocks + i, 0) if axis == 0 else (i, me_ref[0])

    def body(me_ref, w_ref, out_ref):
        out_ref[...] = w_ref[...].astype(BF16)

    return pl.pallas_call(
        body, name=name, out_shape=jax.ShapeDtypeStruct(full, BF16),
        grid_spec=pltpu.PrefetchScalarGridSpec(
            num_scalar_prefetch=1, grid=(blocks,), in_specs=[pl.BlockSpec((tr, c), lambda i, me_ref: (i, 0))],
            out_specs=pl.BlockSpec((tr, c), out_map)),
        compiler_params=_params(("parallel",)))(me.reshape(1), w)


def _exchange_start(src, axis, gather, name, after=()):
    r, c = (src.shape[0] // N_DEV, src.shape[1]) if axis == 0 else (src.shape[0], src.shape[1] // N_DEV)
    n_hbm = 1 if gather else 2
    n_after = len(after)

    def body(*refs):
        src_ref, land_ref = refs[0], refs[n_hbm - 1]
        send_sems, recv_sems = refs[n_hbm + n_after:n_hbm + n_after + 2]
        token = refs[-1]
        me = _coords()
        for m in range(1, N_DEV):
            peer = _flip(me, m)
            if gather:
                mine = theirs = _window(land_ref, _index(me), r, c, axis)
            else:
                mine, theirs = _window(src_ref, _index(peer), r, c, axis), land_ref.at[m - 1]
            pltpu.make_async_remote_copy(
                src_ref=mine, dst_ref=theirs, send_sem=send_sems.at[m - 1], recv_sem=recv_sems.at[m - 1],
                device_id=peer, device_id_type=MESH).start()
        token[...] = jnp.zeros_like(token)

    hbm = [pltpu.with_memory_space_constraint(src, pltpu.HBM)]
    if not gather:
        hbm.append(pltpu.with_memory_space_constraint(lax.empty((N_DEV - 1, r, c), src.dtype), pltpu.HBM))
    outs = pl.pallas_call(
        body, name=name,
        out_shape=(pltpu.SemaphoreType.DMA((N_DEV - 1,)), pltpu.SemaphoreType.DMA((N_DEV - 1,)),
                   *[pltpu.HBM(t.shape, t.dtype) for t in hbm], jax.ShapeDtypeStruct((8, 128), F32)),
        in_specs=[_HBM] * n_hbm + [_ANY] * n_after,
        out_specs=(_SEM, _SEM, *[_HBM] * n_hbm, pl.BlockSpec(memory_space=pltpu.VMEM)),
        input_output_aliases={i: 2 + i for i in range(n_hbm)},
        compiler_params=pltpu.CompilerParams(has_side_effects=_EFFECT),
    )(*hbm, *after)
    return (outs[:2], outs[2:2 + n_hbm], (axis, gather, r, c)), outs[-1]


def _exchange_wait(handle, after, name):
    (send_sems, recv_sems), hbm, (axis, gather, r, c) = handle
    n_hbm = len(hbm)

    def body(*refs):
        src_ref, land_ref = refs[0], refs[n_hbm - 1]
        send_sems, recv_sems = refs[n_hbm:n_hbm + 2]
        me = _coords()
        for m in range(1, N_DEV):
            peer = _flip(me, m)
            if gather:
                mine, theirs = _window(land_ref, _index(me), r, c, axis), _window(land_ref, _index(peer), r, c, axis)
            else:
                mine, theirs = _window(src_ref, _index(peer), r, c, axis), land_ref.at[m - 1]
            copy = pltpu.make_async_remote_copy(
                src_ref=mine, dst_ref=theirs, send_sem=send_sems.at[m - 1], recv_sem=recv_sems.at[m - 1],
                device_id=peer, device_id_type=MESH)
            copy.wait_send()
            copy.wait_recv()

    outs = pl.pallas_call(
        body, name=name, out_shape=tuple(pltpu.HBM(t.shape, t.dtype) for t in hbm),
        in_specs=[_HBM] * n_hbm + [_SEM, _SEM, _ANY], out_specs=tuple([_HBM] * n_hbm),
        input_output_aliases={i: i for i in range(n_hbm)},
        compiler_params=pltpu.CompilerParams(has_side_effects=_EFFECT),
    )(*hbm, send_sems, recv_sems, after)
    return outs[0] if gather else tuple(outs)


def _mm(a, b, mode, out_dtypes, name, tm=1024, tn=1024, tk=2048, extras=(), epilogue=None, dep=None):
    if mode == "nn":
        (m, k), n = a.shape, b.shape[1]
    elif mode == "nt":
        (m, k), n = a.shape, b.shape[0]
    else:
        (k, m), n = a.shape, b.shape[1]
    tm, tn = _tile(m, tm, 8), _tile(n, tn, 128)
    tk = _tile(k, tk, 16 if mode == "tn" else 128)
    nk = k // tk
    swap = nk == 1 and (k * n + (n // tn) * m * k) < (m * k + (m // tm) * k * n)

    def ij(p, q):
        return (q, p) if swap else (p, q)

    def spec(shape, fn):
        return pl.BlockSpec(shape, lambda p, q, kk: fn(*ij(p, q), kk))

    a_spec = spec((tk, tm), lambda i, j, kk: (kk, i)) if mode == "tn" else spec((tm, tk), lambda i, j, kk: (i, kk))
    b_spec = spec((tn, tk), lambda i, j, kk: (j, kk)) if mode == "nt" else spec((tk, tn), lambda i, j, kk: (kk, j))
    dims = {"nn": (((1,), (0,)), ((), ())), "nt": (((1,), (1,)), ((), ())), "tn": (((0,), (0,)), ((), ()))}[mode]
    ex_specs = [spec((tm, tn), functools.partial(lambda i, j, kk, off: (i, j + off), off=off)) for _, off in extras]
    deps = [] if dep is None else [dep]
    dep_specs = [pl.BlockSpec(dep.shape, lambda p, q, kk: (0, 0))] if deps else []
    n_ex, n_out = len(extras), len(out_dtypes)
    n_in = 2 + n_ex + len(deps)

    def body(*refs):
        a_ref, b_ref = refs[0], refs[1]
        ex_refs = refs[2:2 + n_ex]
        out_refs = refs[n_in:n_in + n_out]

        def product():
            return lax.dot_general(a_ref[...], b_ref[...], dims, preferred_element_type=F32)

        def finish(res):
            res = epilogue(res, *[e[...] for e in ex_refs]) if epilogue is not None else (res,)
            for o_ref, val in zip(out_refs, res):
                o_ref[...] = val.astype(o_ref.dtype)

        if nk == 1:
            finish(product())
            return
        acc = refs[-1]
        kk = pl.program_id(2)

        @pl.when(kk == 0)
        def _():
            acc[...] = product()

        @pl.when((kk > 0) & (kk < nk - 1))
        def _():
            acc[...] += product()

        @pl.when(kk == nk - 1)
        def _():
            finish(acc[...] + product())

    outs = pl.pallas_call(
        body, name=name, grid=(*ij(m // tm, n // tn), nk),
        in_specs=[a_spec, b_spec] + ex_specs + dep_specs,
        out_specs=[spec((tm, tn), lambda i, j, kk: (i, j)) for _ in out_dtypes],
        out_shape=[jax.ShapeDtypeStruct((m, n), dt) for dt in out_dtypes],
        scratch_shapes=[pltpu.VMEM((tm, tn), F32)] if nk > 1 else [],
        compiler_params=_params(("parallel", "parallel", "arbitrary")),
    )(a, b, *[e for e, _ in extras], *deps)
    return outs if n_out > 1 else outs[0]


def _gather_project(a, w_full, order, name, riders=(), tm=768):
    m, k = a.shape
    c = w_full.shape[1] // N_DEV
    tm = _tile(m, tm, 8)
    n_i = m // tm
    chips = [4, 2, 6]
    n_rid = len(riders)
    shard = [(k, c, 1)] + [((t.shape[0] // N_DEV, t.shape[1], 0) if ax == 0 else (t.shape[0], t.shape[1] // N_DEV, 1))
                           for t, ax in riders]

    def body(order_ref, a_ref, *refs):
        outs = refs[1 + n_rid:2 + 2 * n_rid]
        z_ref, wbuf, send_sems, recv_sems, load_sems = refs[2 + 2 * n_rid:]
        w_ref = outs[0]
        p, i = pl.program_id(0), pl.program_id(1)
        me = _coords()
        sibling = _flip(me, 1)

        def copy(t, s, block, to):
            r_t, c_t, ax_t = shard[t]
            win = _window(outs[t], _index(block), r_t, c_t, ax_t)
            return pltpu.make_async_remote_copy(src_ref=win, dst_ref=win, send_sem=send_sems.at[7 * t + s],
                                                recv_sem=recv_sems.at[7 * t + s], device_id=to, device_id_type=MESH)

        def first(t):
            return [copy(t, 0, me, sibling)] + [copy(t, 1 + j, me, _flip(me, mm)) for j, mm in enumerate(chips)]

        def passed(t):
            return [copy(t, 4 + j, _flip(me, mm), sibling) for j, mm in enumerate(chips)]

        @pl.when((p == 0) & (i == 0))
        def _():
            for t in range(1 + n_rid):
                for cp in first(t):
                    cp.start()

        def load(piece):
            return pltpu.make_async_copy(_window(w_ref, order_ref[piece], k, c, 1), wbuf.at[piece % 2],
                                         load_sems.at[piece % 2])

        @pl.when((p == 0) & (i == 0))
        def _():
            load(p).start()

        arrivals = [(0, None)] + [x for j in range(3) for x in ((1 + j, passed(0)[j]), (4 + j, None))]
        for piece, (sem, forward) in enumerate(arrivals, start=1):
            @pl.when((p == piece - 1) & (i == n_i - 1))
            def _(sem=sem, forward=forward, piece=piece):
                src_block = (sibling if piece == 1 else
                             _flip(me if piece % 2 == 0 else sibling, chips[(piece - 2) // 2]))
                copy(0, sem, src_block, me).wait_recv()
                if forward is not None:
                    forward.start()
                load(p + 1).start()

        @pl.when(i == 0)
        def _():
            load(p).wait()

        @pl.when((p == N_DEV - 1) & (i == 0))
        def _():
            for t in range(1, 1 + n_rid):
                for j, mm in enumerate(chips):
                    copy(t, 1 + j, _flip(me, mm), me).wait_recv()
                    passed(t)[j].start()

        z_ref[...] = jnp.dot(a_ref[...], wbuf[p % 2], preferred_element_type=F32)

        @pl.when((p == N_DEV - 1) & (i == n_i - 1))
        def _():
            for t in range(1, 1 + n_rid):
                copy(t, 0, sibling, me).wait_recv()
                for j, mm in enumerate(chips):
                    copy(t, 4 + j, _flip(sibling, mm), me).wait_recv()
            for t in range(1 + n_rid):
                for cp in first(t) + passed(t):
                    cp.wait_send()

    gathered = [w_full] + [t for t, _ in riders]
    n_sem = 7 * len(gathered)
    return pl.pallas_call(
        body, name=name,
        out_shape=[jax.ShapeDtypeStruct(t.shape, t.dtype) for t in gathered]
        + [jax.ShapeDtypeStruct((m, N_DEV * c), F32)],
        grid_spec=pltpu.PrefetchScalarGridSpec(
            num_scalar_prefetch=1, grid=(N_DEV, n_i),
            in_specs=[pl.BlockSpec((tm, k), lambda p, i, order_ref: (i, 0))]
            + [pl.BlockSpec(memory_space=pl.ANY)] * len(gathered),
            out_specs=[pl.BlockSpec(memory_space=pl.ANY)] * len(gathered)
            + [pl.BlockSpec((tm, c), lambda p, i, order_ref: (i, order_ref[p]))],
            scratch_shapes=[pltpu.VMEM((2, k, c), w_full.dtype), pltpu.SemaphoreType.DMA((n_sem,)),
                            pltpu.SemaphoreType.DMA((n_sem,)), pltpu.SemaphoreType.DMA((2,))]),
        input_output_aliases={2 + t: t for t in range(len(gathered))},
        compiler_params=_params(("arbitrary", "arbitrary")))(order, a, *gathered)


def _rspec(tr, w, cb=0, off=0):
    return pl.BlockSpec((tr, w), lambda i: (i + off, cb))


def _cspec(shape):
    return pl.BlockSpec(shape, lambda i: (0,) * len(shape))


def _rms(xf):
    rstd = lax.rsqrt(jnp.mean(xf * xf, axis=-1, keepdims=True) + EPS)
    return xf * rstd, rstd


def _rms_bwd(dn, n, rstd):
    return rstd * (dn - n * jnp.mean(dn * n, axis=-1, keepdims=True))


def _colsum(v):
    return jnp.sum(v, axis=0, keepdims=True)


def _total(v):
    return jnp.sum(jnp.sum(v, axis=1, keepdims=True), axis=0, keepdims=True)


def _modulate(x, vec, name, total_rows, into=None):
    rows, d = x.shape
    tr = _tile(rows, 256, 8)
    off = 0 if into is None else (total_rows - rows) // tr

    def body(x_ref, v_ref, *refs):
        n, _ = _rms(x_ref[...])
        refs[-1][...] = ((n * v_ref[0:1, :]) * (1.0 + v_ref[2:3, :]) + v_ref[1:2, :]).astype(BF16)

    return pl.pallas_call(
        body, name=name, grid=(rows // tr,),
        in_specs=[_rspec(tr, d), _cspec(vec.shape)] + ([] if into is None else [_ANY]),
        out_specs=_rspec(tr, d, off=off), out_shape=jax.ShapeDtypeStruct((total_rows, d), BF16),
        input_output_aliases={} if into is None else {2: 0},
        compiler_params=_params(("parallel",)))(x, vec, *([] if into is None else [into]))


def _resid_modulate(x, y, vec, name):
    rows, d = x.shape
    tr = _tile(rows, 256, 8)

    def body(x_ref, y_ref, v_ref, x1_ref, a_ref):
        x1 = x_ref[...] + v_ref[3:4, :] * y_ref[...]
        x1_ref[...] = x1
        n, _ = _rms(x1)
        a_ref[...] = ((n * v_ref[0:1, :]) * (1.0 + v_ref[2:3, :]) + v_ref[1:2, :]).astype(BF16)

    return pl.pallas_call(
        body, name=name, grid=(rows // tr,), in_specs=[_rspec(tr, d), _rspec(tr, d), _cspec(vec.shape)],
        out_specs=[_rspec(tr, d), _rspec(tr, d)],
        out_shape=[jax.ShapeDtypeStruct((rows, d), F32), jax.ShapeDtypeStruct((rows, d), BF16)],
        compiler_params=_params(("parallel",)))(x, y, vec)


def _loss_head(x1, f, target, vec, name):
    rows, d = x1.shape
    tr = _tile(rows, 256, 8)

    def body(x1_ref, f_ref, t_ref, v_ref, dx2_ref, df_ref, acc_ref):
        @pl.when(pl.program_id(0) == 0)
        def _():
            acc_ref[...] = jnp.zeros_like(acc_ref)

        gate, gain = v_ref[0:1, :], v_ref[1:2, :]
        fv = f_ref[...]
        n, rstd = _rms(x1_ref[...] + gate * fv)
        err = n * gain - t_ref[...]
        dy = err / d
        dx2 = _rms_bwd(dy * gain, n, rstd)
        dx2_ref[...] = dx2
        df_ref[...] = (dx2 * gate).astype(BF16)
        acc_ref[0:1, :] += _colsum(dx2 * fv)
        acc_ref[1:2, :] += _colsum(dy * n)
        acc_ref[2:3, :] += _colsum(err * err)

    return pl.pallas_call(
        body, name=name, grid=(rows // tr,),
        in_specs=[_rspec(tr, d), _rspec(tr, d), _rspec(tr, d), _cspec(vec.shape)],
        out_specs=[_rspec(tr, d), _rspec(tr, d), _cspec((8, d))],
        out_shape=[jax.ShapeDtypeStruct((rows, d), F32), jax.ShapeDtypeStruct((rows, d), BF16),
                   jax.ShapeDtypeStruct((8, d), F32)],
        compiler_params=_params(("arbitrary",)))(x1, f, target, vec)


def _modulate_bwd(da, x, vec, name, da_off=0, dx_in=None, y=None):
    rows, d = x.shape
    tr = _tile(rows, 256, 8)
    off = da_off // tr
    has_dx, has_y = dx_in is not None, y is not None

    def body(*refs):
        da_ref, x_ref, v_ref = refs[0], refs[1], refs[2]
        pos = 3
        dxin_ref = refs[pos] if has_dx else None
        pos += has_dx
        y_ref = refs[pos] if has_y else None
        pos += has_y
        dx_ref = refs[pos]
        dy_ref = refs[pos + 1] if has_y else None
        acc_ref = refs[-1]

        @pl.when(pl.program_id(0) == 0)
        def _():
            acc_ref[...] = jnp.zeros_like(acc_ref)

        gain, scale1 = v_ref[0:1, :], 1.0 + v_ref[2:3, :]
        dav = da_ref[...]
        n, rstd = _rms(x_ref[...])
        dx = _rms_bwd(dav * gain * scale1, n, rstd)
        if has_dx:
            dx = dx + dxin_ref[...]
        dx_ref[...] = dx
        acc_ref[0:1, :] += _colsum(dav)
        acc_ref[1:2, :] += _colsum(dav * (n * gain))
        acc_ref[2:3, :] += _colsum(dav * n * scale1)
        if has_y:
            acc_ref[3:4, :] += _colsum(dx * y_ref[...])
            dy_ref[...] = (dx * v_ref[3:4, :]).astype(BF16)

    ins = [da, x, vec] + ([dx_in] if has_dx else []) + ([y] if has_y else [])
    in_specs = [_rspec(tr, d, off=off), _rspec(tr, d), _cspec(vec.shape)] + [_rspec(tr, d)] * (has_dx + has_y)
    out_specs = [_rspec(tr, d)] + ([_rspec(tr, d)] if has_y else []) + [_cspec((8, d))]
    out_shape = ([jax.ShapeDtypeStruct((rows, d), F32)] + ([jax.ShapeDtypeStruct((rows, d), BF16)] if has_y else [])
                 + [jax.ShapeDtypeStruct((8, d), F32)])
    return pl.pallas_call(
        body, name=name, grid=(rows // tr,), in_specs=in_specs, out_specs=out_specs, out_shape=out_shape,
        compiler_params=_params(("arbitrary",)))(*ins)


def _conv_terms(cc, cx, w_ref, tr):
    t = lax.broadcasted_iota(jnp.int32, (tr, 1), 0) % GRID_W
    first, last = t == 0, t == GRID_W - 1
    u = cc * cx
    prev = jnp.where(first, 0.0, pltpu.roll(u, 1, 0))
    nxt = jnp.where(last, 0.0, pltpu.roll(u, tr - 1, 0))
    y = w_ref[0:1, :] * prev + w_ref[1:2, :] * u + w_ref[2:3, :] * nxt
    return u, prev, nxt, y, first, last


def _conv_fwd(z, conv_w, rows, name, dep):
    d = conv_w.shape[1]
    tr = _tile(rows, 256, GRID_W)

    def body(cb_ref, cc_ref, cx_ref, w_ref, dep_ref, out_ref):
        y = _conv_terms(cc_ref[...], cx_ref[...], w_ref, tr)[3]
        out_ref[...] = (cb_ref[...] * y).astype(BF16)

    return pl.pallas_call(
        body, name=name, grid=(rows // tr,),
        in_specs=[_rspec(tr, d, CB), _rspec(tr, d, CC), _rspec(tr, d, CX), _cspec(conv_w.shape), _cspec(dep.shape)],
        out_specs=_rspec(tr, d), out_shape=jax.ShapeDtypeStruct((rows, d), BF16),
        compiler_params=_params(("parallel",)))(z, z, z, conv_w, dep)


def _conv_bwd(dpc, z, conv_w, name):
    rows, d = dpc.shape
    tr = _tile(rows, 256, GRID_W)

    def body(dpc_ref, cb_ref, cc_ref, cx_ref, w_ref, dz_ref, acc_ref):
        @pl.when(pl.program_id(0) == 0)
        def _():
            acc_ref[...] = jnp.zeros_like(acc_ref)

        cc, cx, dpcv = cc_ref[...], cx_ref[...], dpc_ref[...]
        u, prev, nxt, y, first, last = _conv_terms(cc, cx, w_ref, tr)
        dy = dpcv * cb_ref[...]
        dy_next = jnp.where(last, 0.0, pltpu.roll(dy, tr - 1, 0))
        dy_prev = jnp.where(first, 0.0, pltpu.roll(dy, 1, 0))
        du = w_ref[0:1, :] * dy_next + w_ref[1:2, :] * dy + w_ref[2:3, :] * dy_prev
        dz_ref[:, 0:d] = (dpcv * y).astype(BF16)
        dz_ref[:, d:2 * d] = (du * cx).astype(BF16)
        dz_ref[:, 2 * d:3 * d] = (du * cc).astype(BF16)
        acc_ref[0:1, :] += _colsum(dy * prev)
        acc_ref[1:2, :] += _colsum(dy * u)
        acc_ref[2:3, :] += _colsum(dy * nxt)

    return pl.pallas_call(
        body, name=name, grid=(rows // tr,),
        in_specs=[_rspec(tr, d), _rspec(tr, d, CB), _rspec(tr, d, CC), _rspec(tr, d, CX), _cspec(conv_w.shape)],
        out_specs=[_rspec(tr, 3 * d), _cspec((8, d))],
        out_shape=[jax.ShapeDtypeStruct((rows, 3 * d), BF16), jax.ShapeDtypeStruct((8, d), F32)],
        compiler_params=_params(("arbitrary",)))(dpc, z, z, z, conv_w)


def _rotary_fwd(z, cos, sin, n_lat, name):
    rows = z.shape[0]
    d = z.shape[1] // N_IN
    dk = d // HEADS
    half = dk // 2
    tr = _tile(n_lat, 256, 8)
    tr = _tile(rows - n_lat, tr, 8)
    lat_blocks, all_blocks = n_lat // tr, rows // tr
    ctx_blocks = all_blocks - lat_blocks
    scan_rows = rows + ctx_blocks * tr

    def z_block(g):
        return jnp.where(g < all_blocks, g, g - ctx_blocks)

    def in_spec(w, cb=0):
        return pl.BlockSpec((tr, w), lambda g: (z_block(g), cb))

    def out_spec(w):
        return pl.BlockSpec((tr, w), lambda g: (jnp.where(g < all_blocks, g + ctx_blocks, g - all_blocks), 0))

    def body(q_ref, k_ref, v0_ref, v1_ref, cos_ref, sin_ref, qo_ref, ko_ref, vo_ref):
        cs, sn = cos_ref[...], sin_ref[...]
        keep = jnp.where(z_block(pl.program_id(0)) < lat_blocks, 1.0, 0.0)
        for src, dst, scale in ((q_ref, qo_ref, keep), (k_ref, ko_ref, dk ** -0.5)):
            for h in range(HEADS):
                lo, mid, hi = h * dk, h * dk + half, (h + 1) * dk
                t1, t2 = src[:, lo:mid], src[:, mid:hi]
                dst[:, lo:mid] = ((t1 * cs - t2 * sn) * scale).astype(BF16)
                dst[:, mid:hi] = ((t1 * sn + t2 * cs) * scale).astype(BF16)
        vo_ref[:, 0:d] = v0_ref[...].astype(BF16)
        vo_ref[:, d:2 * d] = v1_ref[...].astype(BF16)

    return pl.pallas_call(
        body, name=name, grid=(all_blocks + ctx_blocks,),
        in_specs=[in_spec(d, CQ), in_spec(d, CK), in_spec(d, CV), in_spec(d, CV + 1), in_spec(half), in_spec(half)],
        out_specs=[out_spec(d), out_spec(d), out_spec(2 * d)],
        out_shape=[jax.ShapeDtypeStruct((scan_rows, d), BF16), jax.ShapeDtypeStruct((scan_rows, d), BF16),
                   jax.ShapeDtypeStruct((scan_rows, 2 * d), BF16)],
        compiler_params=_params(("parallel",)))(z, z, z, z, cos, sin)


def _dz_assemble(dq_f, dq_b, dk_f, dk_b, dv_f, dv_b, cos, sin, dz_conv, dz_g, dz_gc, dz_gr, n_lat, n_ctx, name):
    rows = n_lat + n_ctx
    d = dq_f.shape[1]
    dk = d // HEADS
    half = dk // 2
    tr = _tile(n_ctx, 128, 8)
    lat_blocks, ctx_blocks = n_lat // tr, n_ctx // tr

    def fmap(i):
        return (jnp.where(i < lat_blocks, i + ctx_blocks, i - lat_blocks), 0)

    def bmap(i):
        return (i + ctx_blocks, 0)

    def lmap(i):
        return (jnp.minimum(i, lat_blocks - 1), 0)

    def body(qf_ref, qb_ref, kf_ref, kb_ref, vf_ref, vb_ref, cos_ref, sin_ref, c_ref, g_ref, gc_ref, gr_ref, out_ref):
        cs, sn = cos_ref[...], sin_ref[...]
        is_lat = pl.program_id(0) < lat_blocks
        keep = jnp.where(is_lat, 1.0, 0.0)
        for fa, fb, base, scale in ((qf_ref, qb_ref, CQ * d, keep), (kf_ref, kb_ref, CK * d, dk ** -0.5)):
            for h in range(HEADS):
                lo, mid, hi = h * dk, h * dk + half, (h + 1) * dk
                d1 = fa[:, lo:mid].astype(F32) + fb[:, lo:mid].astype(F32)
                d2 = fa[:, mid:hi].astype(F32) + fb[:, mid:hi].astype(F32)
                out_ref[:, base + lo:base + mid] = ((d1 * cs + d2 * sn) * scale).astype(BF16)
                out_ref[:, base + mid:base + hi] = ((d2 * cs - d1 * sn) * scale).astype(BF16)
        out_ref[:, CV * d:CG * d] = (vf_ref[...].astype(F32) + vb_ref[...].astype(F32)).astype(BF16)
        for src, lo, hi in ((c_ref, CB * d, CQ * d), (g_ref, CG * d, CGC * d), (gc_ref, CGC * d, CGR * d),
                            (gr_ref, CGR * d, N_IN * d)):
            out_ref[:, lo:hi] = jnp.where(is_lat, src[...], jnp.zeros_like(src))

    return pl.pallas_call(
        body, name=name, grid=(rows // tr,),
        in_specs=[pl.BlockSpec((tr, d), fmap), pl.BlockSpec((tr, d), bmap), pl.BlockSpec((tr, d), fmap),
                  pl.BlockSpec((tr, d), bmap), pl.BlockSpec((tr, 2 * d), fmap), pl.BlockSpec((tr, 2 * d), bmap),
                  _rspec(tr, half), _rspec(tr, half), pl.BlockSpec((tr, 3 * d), lmap), pl.BlockSpec((tr, 2 * d), lmap),
                  pl.BlockSpec((tr, d), lmap), pl.BlockSpec((tr, d), lmap)],
        out_specs=_rspec(tr, N_IN * d), out_shape=jax.ShapeDtypeStruct((rows, N_IN * d), BF16),
        compiler_params=_params(("parallel",)))(dq_f, dq_b, dk_f, dk_b, dv_f, dv_b, cos, sin, dz_conv, dz_g, dz_gc, dz_gr)


def _gn_fwd(o, z, rows, row_off, name):
    d = z.shape[1] // N_IN
    dv = 2 * d // HEADS
    tr = _tile(row_off, 128, 8)
    off = row_off // tr

    def body(o_ref, g0_ref, g1_ref, r_ref):
        for h in range(HEADS):
            lo, hi = h * dv, (h + 1) * dv
            g_ref, glo = (g0_ref, lo) if hi <= d else (g1_ref, lo - d)
            o = o_ref[:, lo:hi]
            cen = o - jnp.mean(o, axis=-1, keepdims=True)
            on = cen * lax.rsqrt(jnp.mean(cen * cen, axis=-1, keepdims=True) + EPS)
            g = g_ref[:, glo:glo + dv]
            r_ref[:, lo:hi] = (g * _sig(g) * on).astype(BF16)

    return pl.pallas_call(
        body, name=name, grid=(rows // tr,),
        in_specs=[_rspec(tr, 2 * d, off=off), _rspec(tr, d, CG), _rspec(tr, d, CG + 1)],
        out_specs=_rspec(tr, 2 * d), out_shape=jax.ShapeDtypeStruct((rows, 2 * d), BF16),
        compiler_params=_params(("parallel",)))(o, z, z)


def _gn_bwd(dr, o, z, row_off, name):
    rows = dr.shape[0]
    d = z.shape[1] // N_IN
    dv = 2 * d // HEADS
    tr = _tile(row_off, 128, 8)
    off = row_off // tr

    def body(dr_ref, o_ref, g0_ref, g1_ref, do_ref, dg_ref):
        for h in range(HEADS):
            lo, hi = h * dv, (h + 1) * dv
            g_ref, glo = (g0_ref, lo) if hi <= d else (g1_ref, lo - d)
            o = o_ref[:, lo:hi]
            cen = o - jnp.mean(o, axis=-1, keepdims=True)
            rstd = lax.rsqrt(jnp.mean(cen * cen, axis=-1, keepdims=True) + EPS)
            on = cen * rstd
            g = g_ref[:, glo:glo + dv]
            sg = _sig(g)
            drv = dr_ref[:, lo:hi]
            dg_ref[:, lo:hi] = (drv * on * (sg * (1.0 + g * (1.0 - sg)))).astype(BF16)
            don = drv * (g * sg)
            do = rstd * (don - jnp.mean(don, axis=-1, keepdims=True)
                         - on * jnp.mean(don * on, axis=-1, keepdims=True))
            do_ref[:, lo:hi] = do.astype(BF16)

    return pl.pallas_call(
        body, name=name, grid=(rows // tr,),
        in_specs=[_rspec(tr, 2 * d), _rspec(tr, 2 * d, off=off), _rspec(tr, d, CG), _rspec(tr, d, CG + 1)],
        out_specs=[_rspec(tr, 2 * d), _rspec(tr, 2 * d)],
        out_shape=[jax.ShapeDtypeStruct((rows, 2 * d), BF16), jax.ShapeDtypeStruct((rows, 2 * d), BF16)],
        compiler_params=_params(("parallel",)))(dr, o, z, z)


def _decays(lg, rev):
    row = lax.broadcasted_iota(jnp.int32, (CHUNK, CHUNK), 0)
    col = lax.broadcasted_iota(jnp.int32, (CHUNK, CHUNK), 1)
    rel = ((col - row) if rev else (row - col)).astype(F32)
    mask = jnp.where(rel >= 0, jnp.exp(lg * jnp.maximum(rel, 0.0)), 0.0)
    r = lax.broadcasted_iota(jnp.int32, (CHUNK, 1), 0)
    rr = ((CHUNK - 1 - r) if rev else r).astype(F32)
    chunk_decay = jnp.exp(lg * jnp.full((1, 1), float(CHUNK), F32))
    return (rr, rel), mask, jnp.exp(lg * (rr + 1.0)), jnp.exp(lg * (CHUNK - 1.0 - rr)), chunk_decay


_NT = (((1,), (1,)), ((), ()))
_TN = (((0,), (0,)), ((), ()))


def _dot(a, b, dims=None):
    if dims is None:
        return jnp.dot(a, b, preferred_element_type=F32)
    return lax.dot_general(a, b, dims, preferred_element_type=F32)


def _ret_fwd(q, k, v, lgs, name):
    tt, d = q.shape
    dk, dv = d // HEADS, 2 * d // HEADS
    nc = tt // CHUNK

    def chunk_of(rev, i):
        return (nc - 1 - i) if rev else i

    def cmap(rev):
        return lambda h, i: (chunk_of(rev, i), h)

    def smap(rev):
        return lambda h, i: (chunk_of(rev, i), h, 0, 0)

    def body(lg_ref, qf_ref, kf_ref, vf_ref, qb_ref, kb_ref, vb_ref, o_ref, sf_ref, sb_ref, state_f, state_b):
        h, i = pl.program_id(0), pl.program_id(1)

        @pl.when(i == 0)
        def _():
            state_f[...] = jnp.zeros_like(state_f)
            state_b[...] = jnp.zeros_like(state_b)

        for rev, (q_ref, k_ref, v_ref, s_ref, state) in enumerate((
                (qf_ref, kf_ref, vf_ref, sf_ref, state_f), (qb_ref, kb_ref, vb_ref, sb_ref, state_b))):
            _, mask, qd, kd, cd = _decays(lg_ref[rev, h], bool(rev))
            qv, kv, vv = q_ref[...], k_ref[...], v_ref[...]
            st = state[...]
            p = _dot(qv, kv, _NT) * mask
            out = _dot(p.astype(BF16), vv) + _dot((qv * qd).astype(BF16), st.astype(BF16))
            s_ref[...] = st
            state[...] = cd * st + _dot((kv * kd).astype(BF16), vv, _TN)
            rows = pl.ds(pl.multiple_of(chunk_of(bool(rev), i) * CHUNK, CHUNK), CHUNK)
            first = (2 * i < nc - 1) if rev else (2 * i <= nc - 1)

            @pl.when(first)
            def _():
                o_ref[rows, :] = out

            @pl.when(jnp.logical_not(first))
            def _():
                o_ref[rows, :] += out

    def specs(rev):
        return [pl.BlockSpec((CHUNK, dk), cmap(rev)), pl.BlockSpec((CHUNK, dk), cmap(rev)),
                pl.BlockSpec((CHUNK, dv), cmap(rev))]

    state_shape = jax.ShapeDtypeStruct((nc, HEADS, dk, dv), F32)
    return pl.pallas_call(
        body, name=name, grid=(HEADS, nc),
        in_specs=[pl.BlockSpec(memory_space=pltpu.SMEM)] + specs(False) + specs(True),
        out_specs=[pl.BlockSpec((tt, dv), lambda h, i: (0, h)), pl.BlockSpec((None, None, dk, dv), smap(False)),
                   pl.BlockSpec((None, None, dk, dv), smap(True))],
        out_shape=[jax.ShapeDtypeStruct((tt, 2 * d), F32), state_shape, state_shape],
        scratch_shapes=[pltpu.VMEM((dk, dv), F32), pltpu.VMEM((dk, dv), F32)],
        compiler_params=_params(("parallel", "arbitrary")))(lgs, q, k, v, q, k, v)


def _ret_bwd(q, k, v, do, states_f, states_b, lgs, name):
    tt, d = q.shape
    dk, dv = d // HEADS, 2 * d // HEADS
    nc = tt // CHUNK
    lat_chunks = do.shape[0] // CHUNK
    ctx_chunks = (nc - lat_chunks) // 2

    def chunk_of(rev, i):
        return i if rev else nc - 1 - i

    def cmap(rev):
        return lambda h, i: (chunk_of(rev, i), h)

    def do_map(rev):
        return lambda h, i: (jnp.clip(chunk_of(rev, i) - ctx_chunks, 0, lat_chunks - 1), h)

    def smap(rev):
        return lambda h, i: (chunk_of(rev, i), h, 0, 0)

    def body(lg_ref, *refs):
        h = pl.program_id(0)
        dstates = refs[-2:]

        @pl.when(pl.program_id(1) == 0)
        def _():
            for rev in (0, 1):
                dstates[rev][...] = jnp.zeros_like(dstates[rev])
                refs[10 + 4 * rev + 3][...] = jnp.zeros_like(refs[10 + 4 * rev + 3])

        for rev in (0, 1):
            q_ref, k_ref, v_ref, do_ref, s_ref = refs[5 * rev:5 * rev + 5]
            dq_ref, dk_ref, dv_ref, dlg_ref = refs[10 + 4 * rev:10 + 4 * rev + 4]
            dstate = dstates[rev]
            chunk = chunk_of(bool(rev), pl.program_id(1))
            is_lat = (chunk >= ctx_chunks) & (chunk < ctx_chunks + lat_chunks)
            (rr, rel), mask, qd, kd, cd = _decays(lg_ref[rev, h], bool(rev))
            qv, kv, vv = q_ref[...], k_ref[...], v_ref[...]
            dov = jnp.where(is_lat, do_ref[...], jnp.zeros_like(do_ref))
            st, dst = s_ref[...], dstate[...]
            st_b, dst_b = st.astype(BF16), dst.astype(BF16)
            p = _dot(qv, kv, _NT) * mask
            dp = _dot(dov, vv, _NT)
            da = (dp * mask).astype(BF16)
            dq_state = _dot(dov, st_b, _NT) * qd
            dk_state = _dot(vv, dst_b, _NT) * kd
            dq_ref[...] = (_dot(da, kv) + dq_state).astype(BF16)
            dk_ref[...] = (_dot(da, qv, _TN) + dk_state).astype(BF16)
            dv_ref[...] = (_dot(p.astype(BF16), dov, _TN) + _dot((kv * kd).astype(BF16), dst_b)).astype(BF16)
            dnew = cd * dst + _dot((qv * qd).astype(BF16), dov, _TN)
            dstate[...] = dnew
            through = rr * (jnp.sum(qv.astype(F32) * dq_state, axis=1, keepdims=True)
                            - jnp.sum(kv.astype(F32) * dk_state, axis=1, keepdims=True))
            dlg_ref[...] += _total(rel * p * dp) + _total(through) + CHUNK * _total(st * dnew)

    def specs(rev):
        return [pl.BlockSpec((CHUNK, dk), cmap(rev)), pl.BlockSpec((CHUNK, dk), cmap(rev)),
                pl.BlockSpec((CHUNK, dv), cmap(rev)), pl.BlockSpec((CHUNK, dv), do_map(rev)),
                pl.BlockSpec((None, None, dk, dv), smap(rev))]

    def outs(rev):
        return [pl.BlockSpec((CHUNK, dk), cmap(rev)), pl.BlockSpec((CHUNK, dk), cmap(rev)),
                pl.BlockSpec((CHUNK, dv), cmap(rev)), pl.BlockSpec((None, 8, 128), lambda h, i: (h, 0, 0))]

    return pl.pallas_call(
        body, name=name, grid=(HEADS, nc),
        in_specs=[pl.BlockSpec(memory_space=pltpu.SMEM)] + specs(False) + specs(True),
        out_specs=outs(False) + outs(True),
        out_shape=[jax.ShapeDtypeStruct((tt, d), BF16), jax.ShapeDtypeStruct((tt, d), BF16),
                   jax.ShapeDtypeStruct((tt, 2 * d), BF16), jax.ShapeDtypeStruct((HEADS, 8, 128), F32)] * 2,
        scratch_shapes=[pltpu.VMEM((dk, dv), F32), pltpu.VMEM((dk, dv), F32)],
        compiler_params=_params(("parallel", "arbitrary")))(lgs, q, k, v, do, states_f, q, k, v, do, states_b)


def _silu(v):
    return v * _sig(v)


def _mod_proj(cs, w_mod, b_loc, dec, name):
    nb = w_mod.shape[1]

    def body(cs_ref, w_ref, b_ref, dec_ref, out_ref, lg_ref):
        out_ref[...] = jnp.dot(_silu(cs_ref[...]), w_ref[...], preferred_element_type=F32, precision=HIGHEST) + b_ref[...]
        a = dec_ref[...]
        lg_ref[...] = jnp.minimum(a, 0.0) - jnp.log1p(jnp.exp(-jnp.abs(a)))

    return pl.pallas_call(
        body, name=name,
        out_shape=[jax.ShapeDtypeStruct((16, nb), F32), jax.ShapeDtypeStruct(dec.shape, F32)],
        compiler_params=_params())(cs, w_mod, b_loc, dec)


def _mod_grad(cs, dm, w_mod, name):
    d, nb = w_mod.shape

    def body(cs_ref, dm_ref, w_ref, gw_ref, part_ref):
        dmv = dm_ref[...]
        gw_ref[...] = lax.dot_general(_silu(cs_ref[...]), dmv, _TN, preferred_element_type=F32, precision=HIGHEST)
        part_ref[...] = lax.dot_general(dmv, w_ref[...], _NT, preferred_element_type=F32, precision=HIGHEST)

    return pl.pallas_call(
        body, name=name,
        out_shape=[jax.ShapeDtypeStruct((d, nb), F32), jax.ShapeDtypeStruct((16, d), F32)],
        compiler_params=_params())(cs, dm, w_mod)


def _reduce_small(gathered, dec, n_feat, name):
    _, rows, d = gathered.shape

    def body(g_ref, dec_ref, red_ref, misc_ref):
        total = g_ref[0]
        for i in range(1, N_DEV):
            total = total + g_ref[i]
        red_ref[...] = total
        misc_ref[...] = jnp.zeros_like(misc_ref)
        misc_ref[0:1, :] = jnp.zeros((1, 128), F32) + (0.5 / n_feat) * _total(total[16:17, :])
        misc_ref[1:3, :] = total[14:16, 0:128] * _sig(-dec_ref[0:2, :])

    return pl.pallas_call(
        body, name=name,
        out_shape=[jax.ShapeDtypeStruct((rows, d), F32), jax.ShapeDtypeStruct((8, 128), F32)],
        compiler_params=_params())(gathered, dec)


def _c_ctx_grad(parts, c_ctx, name):
    d = c_ctx.shape[1]

    def body(p_ref, c_ref, out_ref):
        total = p_ref[0]
        for i in range(1, N_DEV):
            total = total + p_ref[i]
        cv = c_ref[...]
        sg = _sig(cv)
        out_ref[...] = total[8:9, :] * (sg * (1.0 + cv * (1.0 - sg)))

    return pl.pallas_call(body, name=name, out_shape=jax.ShapeDtypeStruct((1, d), F32),
                          compiler_params=_params())(parts, c_ctx)


def _adamw(parts, w, m, v, name, own=None):
    n_parts, rows, cols = parts.shape
    row_bytes = cols * (parts.dtype.itemsize * (n_parts + 1) + 7 * 4)
    tr = _tile(rows, max(16, (8 * 1024 * 1024) // row_bytes), 16 if rows % 16 == 0 else 8)
    blocks = rows // tr

    def body(*refs):
        p_ref, w_ref, m_ref, v_ref, g_ref, d_ref, mo_ref, vo_ref = refs[-8:]
        if own is None:
            g = p_ref[0].astype(F32)
        else:
            g = refs[1][...].astype(F32) + p_ref[0].astype(F32)
        for i in range(1, n_parts):
            g = g + p_ref[i].astype(F32)
        m2 = ADAM_B1 * m_ref[...] + (1.0 - ADAM_B1) * g
        v2 = ADAM_B2 * v_ref[...] + (1.0 - ADAM_B2) * jnp.square(g)
        m_hat = m2 / (1.0 - ADAM_B1 ** ADAM_STEP)
        v_hat = v2 / (1.0 - ADAM_B2 ** ADAM_STEP)
        g_ref[...] = g
        d_ref[...] = -ADAM_LR * (m_hat / (jnp.sqrt(v_hat) + ADAM_EPS) + ADAM_WD * w_ref[...])
        mo_ref[...] = m2
        vo_ref[...] = v2

    out_shape = [jax.ShapeDtypeStruct((rows, cols), F32)] * 4
    if own is None:
        spec = _rspec(tr, cols)
        return pl.pallas_call(
            body, name=name, grid=(blocks,),
            in_specs=[pl.BlockSpec((n_parts, tr, cols), lambda i: (0, i, 0)), spec, spec, spec],
            out_specs=[spec] * 4, out_shape=out_shape, compiler_params=_params(("parallel",)))(parts, w, m, v)
    g_full, axis, me = own

    def own_map(i, me_ref):
        return (me_ref[0] * blocks + i, 0) if axis == 0 else (i, me_ref[0])

    spec = pl.BlockSpec((tr, cols), lambda i, me_ref: (i, 0))
    return pl.pallas_call(
        body, name=name, out_shape=out_shape,
        grid_spec=pltpu.PrefetchScalarGridSpec(
            num_scalar_prefetch=1, grid=(blocks,),
            in_specs=[pl.BlockSpec((tr, cols), own_map),
                      pl.BlockSpec((n_parts, tr, cols), lambda i, me_ref: (0, i, 0)), spec, spec, spec],
            out_specs=[spec] * 4),
        compiler_params=_params(("parallel",)))(me.reshape(1), g_full, parts, w, m, v)


def _rope_tables(pos, dk):
    half = dk // 2
    inv_freq = 1.0 / (ROPE_BASE ** jnp.linspace(0.0, 1.0, half, dtype=F32))
    ang = pos[:, None] * inv_freq[None, :]
    return jnp.cos(ang), jnp.sin(ang)


def _pad_lanes(v, width):
    return jnp.pad(v, ((0, 0), (0, width - v.shape[1])))


def kernel(x, c, ctx, c_ctx, w_mod, b_mod, norm1_g, w_in, conv_w, w_conv_out, ret_decay_fwd, ret_decay_bwd, w_ret_out, w_o, norm2_g, w_ff1, w_ff2, final_g, loss_target, m_c_ctx, m_w_mod, m_b_mod, m_norm1_g, m_w_in, m_conv_w, m_w_conv_out, m_ret_decay_fwd, m_ret_decay_bwd, m_w_ret_out, m_w_o, m_norm2_g, m_w_ff1, m_w_ff2, m_final_g, v_c_ctx, v_w_mod, v_b_mod, v_norm1_g, v_w_in, v_conv_w, v_w_conv_out, v_ret_decay_fwd, v_ret_decay_bwd, v_w_ret_out, v_w_o, v_norm2_g, v_w_ff1, v_w_ff2, v_final_g):
    n_lat, d = x.shape[1], x.shape[2]
    n_ctx = ctx.shape[1]
    assert n_ctx % CHUNK == 0 and n_lat % CHUNK == 0 and n_lat % GRID_W == 0
    dk = d // HEADS
    nb = w_mod.shape[2]
    me = 4 * lax.axis_index("x") + 2 * lax.axis_index("y") + lax.axis_index("c")
    xl, ctxl, target = x[0], ctx[0], loss_target[0]

    conv_w_f = _small_allgather(conv_w[0], "ag_conv_w").transpose(1, 0, 2).reshape(3, d)

    c_all = _small_allgather(c, "ag_c").reshape(N_DEV, d)
    cs = jnp.concatenate([c_all, c_ctx[None], jnp.zeros((7, d), F32)], axis=0)
    dec = jnp.pad(jnp.concatenate([ret_decay_fwd, ret_decay_bwd], axis=0), ((0, 6), (0, 128 - HEADS)))
    b_loc = lax.dynamic_slice(b_mod, (0, me * nb), (1, nb))
    modp, lgs = _mod_proj(cs, w_mod[0], b_loc, dec, "mod_proj")
    modp_all = _small_allgather(modp, "ag_mod")
    mod_l = lax.dynamic_index_in_dim(modp_all, me, axis=1, keepdims=False).reshape(N_MOD, d)
    mod_c = modp_all[:, 8, :].reshape(N_MOD, d)
    lgs = lgs[0:2, :HEADS]
    zero_row = jnp.zeros((1, d), F32)
    vec1_l = jnp.concatenate([norm1_g, mod_l[0:1], mod_l[1:2], zero_row], axis=0)
    vec1_c = jnp.concatenate([norm1_g, mod_c[0:1], mod_c[1:2], zero_row], axis=0)
    vec2 = jnp.concatenate([norm2_g, mod_l[3:4], mod_l[4:5], mod_l[2:3]], axis=0)
    vec3 = jnp.concatenate([mod_l[5:6], final_g[None]], axis=0)

    a_all = _modulate(ctxl, vec1_c, "modulate1_ctx", n_lat + n_ctx,
                      into=_modulate(xl, vec1_l, "modulate1", n_lat + n_ctx))
    arrival = me ^ jnp.array([0, 1, 4, 5, 2, 3, 6, 7], jnp.int32)
    riders = [(_place_shard(w[0], 0, me, "place_" + wname), 0)
              for wname, w in (("w_conv_out", w_conv_out), ("w_ret_out", w_ret_out), ("w_o", w_o))]
    w_in_f, w_conv_out_f, w_ret_out_f, w_o_f, z = _gather_project(
        a_all, _place_shard(w_in[0], 1, me, "place_w_in"), arrival, "proj_in", riders=riders)

    later, after = {}, (z, conv_w_f)
    for wname, w, axis in (("w_ff1", w_ff1, 1), ("w_ff2", w_ff2, 0)):
        later[wname], token = _exchange_start(_place_shard(w[0], axis, me, "place_" + wname), axis, True,
                                              "ag_" + wname + "_start", after=after)
        after = (token,)
    pc = _conv_fwd(z, conv_w_f, n_lat, "conv_fwd", token)
    pos = jnp.concatenate([n_ctx + jnp.arange(n_lat, dtype=F32), jnp.arange(n_ctx, dtype=F32)])
    cos, sin = _rope_tables(pos, dk)
    q_s, k_s, v_s = _rotary_fwd(z, cos, sin, n_lat, "rotary_fwd")
    o, st_f, st_b = _ret_fwd(q_s, k_s, v_s, lgs, "ret_fwd")
    r = _gn_fwd(o, z, n_lat, n_ctx, "gn_fwd")
    y_conv = _mm(pc, w_conv_out_f, "nn", [F32], "proj_conv_out", tn=2048)
    tn_d = _tile(d, 1024, 128)
    gate_offs = (CGC * d // tn_d, CGR * d // tn_d)

    def merge(acc, yc, gc, gr):
        return acc, _sig(gc) * yc + _sig(gr) * acc

    y_ret, mg = _mm(r, w_ret_out_f, "nn", [F32, BF16], "proj_ret_out", tm=512, tn=tn_d,
                    extras=[(y_conv, 0), (z, gate_offs[0]), (z, gate_offs[1])], epilogue=merge)
    y_l = _mm(mg, w_o_f, "nn", [F32], "proj_o", tn=2048)
    x1, a2 = _resid_modulate(xl, y_l, vec2, "resid_modulate2")

    def sqrelu(acc):
        return acc, jnp.square(jnp.maximum(acc, 0.0))

    w_ff1_f = _exchange_wait(later["w_ff1"], a2, "ag_w_ff1_wait")
    hff, s = _mm(a2, w_ff1_f, "nn", [BF16, BF16], "ff1", epilogue=sqrelu)
    w_ff2_f = _exchange_wait(later["w_ff2"], s, "ag_w_ff2_wait")
    f = _mm(s, w_ff2_f, "nn", [F32], "ff2")
    dx2, df, acc3 = _loss_head(x1, f, target, vec3, "loss_head")

    def d_sqrelu(acc, h):
        return (acc * (2.0 * jnp.maximum(h.astype(F32), 0.0)),)

    dh = _mm(df, w_ff2_f, "nt", [BF16], "ff2_dx", extras=[(hff, 0)], epilogue=d_sqrelu)
    sent = {}
    sent["w_ff2"], token = _exchange_start(_mm(s, df, "tn", [BF16], "ff2_dw"), 0, False, "rs_w_ff2_start")
    da2 = _mm(dh, w_ff1_f, "nt", [F32], "ff1_dx", dep=token)
    sent["w_ff1"], token = _exchange_start(_mm(a2, dh, "tn", [BF16], "ff1_dw"), 1, False, "rs_w_ff1_start")
    dx1, dyl, acc2 = _modulate_bwd(da2, x1, vec2, "modulate2_bwd", dx_in=dx2, y=y_l)

    def d_merge(acc, yc, yr, gc, gr):
        sc, sr = _sig(gc), _sig(gr)
        return acc * sc, acc * sr, acc * yc * (sc * (1.0 - sc)), acc * yr * (sr * (1.0 - sr))

    dyc, dyr, dgc, dgr = _mm(dyl, w_o_f, "nt", [BF16] * 4, "proj_o_dx", tm=512, tn=tn_d,
                             extras=[(y_conv, 0), (y_ret, 0), (z, gate_offs[0]), (z, gate_offs[1])], epilogue=d_merge,
                             dep=token)
    sent["w_o"], token = _exchange_start(_mm(mg, dyl, "tn", [BF16], "proj_o_dw"), 0, False, "rs_w_o_start")
    dpc = _mm(dyc, w_conv_out_f, "nt", [F32], "proj_conv_out_dx", tn=2048, dep=token)
    sent["w_conv_out"], token = _exchange_start(_mm(pc, dyc, "tn", [BF16], "proj_conv_out_dw"), 0, False,
                                                "rs_w_conv_out_start")
    dz_conv, acc_conv = _conv_bwd(dpc, z, conv_w_f, "conv_bwd")
    dr = _mm(dyr, w_ret_out_f, "nt", [F32], "proj_ret_out_dx", dep=token)
    sent["w_ret_out"], token = _exchange_start(_mm(r, dyr, "tn", [BF16], "proj_ret_out_dw"), 0, False,
                                               "rs_w_ret_out_start")
    do, dz_g = _gn_bwd(dr, o, z, n_ctx, "gn_bwd")
    dq_f, dk_f, dv_f, dlg_f, dq_b, dk_b, dv_b, dlg_b = _ret_bwd(q_s, k_s, v_s, do, st_f, st_b, lgs, "ret_bwd")
    dz = _dz_assemble(dq_f, dq_b, dk_f, dk_b, dv_f, dv_b, cos, sin, dz_conv, dz_g, dgc, dgr, n_lat, n_ctx,
                      "dz_assemble")
    g_in = _mm(a_all, dz, "tn", [BF16], "proj_in_dw", tn=2048, tk=(n_lat + n_ctx) // 4, dep=token)
    sent["w_in"], token = _exchange_start(g_in, 1, False, "rs_w_in_start")
    da_all = _mm(dz, w_in_f, "nt", [F32], "proj_in_dx", tn=2048, dep=token)
    grad_x, acc1 = _modulate_bwd(da_all, xl, vec1_l, "modulate1_bwd", dx_in=dx1)
    _, acc1c = _modulate_bwd(da_all, ctxl, vec1_c, "modulate1_ctx_bwd", da_off=n_lat)

    lane_pad = functools.partial(_pad_lanes, width=d)
    packet = jnp.concatenate([
        acc1[2:3] + acc1c[2:3], acc2[2:3], acc3[1:2],
        acc1[0:1], acc1[1:2], acc2[3:4], acc2[0:1], acc2[1:2], acc3[0:1],
        acc1c[0:1], acc1c[1:2],
        acc_conv[0:3],
        lane_pad(dlg_f[:, 0, 0][None]), lane_pad(dlg_b[:, 0, 0][None]),
        acc3[2:3],
        jnp.zeros((7, d), F32)], axis=0)
    n_rows = packet.shape[0]
    packets = lax.dynamic_update_slice(jnp.zeros((N_DEV * n_rows, d), F32), packet, (me * n_rows, 0))
    packets_sent, after = _exchange_start(packets, 0, True, "ag_small_start")
    res = {}

    def update(wname, axis, w, m, v, after):
        g_full, parts = _exchange_wait(sent[wname], after, "rs_" + wname + "_wait")
        res[wname] = _adamw(parts, w[0], m[0], v[0], "adamw_" + wname, own=(g_full, axis, me))
        return res[wname][0]

    for wname, axis, w, m, v in (("w_ff2", 0, w_ff2, m_w_ff2, v_w_ff2), ("w_ff1", 1, w_ff1, m_w_ff1, v_w_ff1),
                                 ("w_o", 0, w_o, m_w_o, v_w_o), ("w_conv_out", 0, w_conv_out, m_w_conv_out, v_w_conv_out),
                                 ("w_ret_out", 0, w_ret_out, m_w_ret_out, v_w_ret_out)):
        after = update(wname, axis, w, m, v, after)
    gathered = _exchange_wait(packets_sent, after, "ag_small_wait").reshape(N_DEV, n_rows, d)
    red, misc = _reduce_small(gathered, dec, d, "reduce_small")
    dmod_ctx = jnp.concatenate([red[9], red[10], jnp.zeros((4 * d,), F32)])
    dmod_all = jnp.concatenate([gathered[:, 3:9, :].reshape(N_DEV, N_MOD * d), dmod_ctx[None]], axis=0)
    dm = jnp.pad(lax.dynamic_slice(dmod_all, (0, me * nb), (N_DEV + 1, nb)), ((0, 7), (0, 0)))
    g_mod, c_part = _mod_grad(cs, dm, w_mod[0], "mod_grad")
    g_c_ctx = _c_ctx_grad(_small_allgather(c_part, "ag_c_ctx"), c_ctx[None], "c_ctx_grad")
    g_b_mod = red[3:9].reshape(1, N_MOD * d) + dmod_ctx[None]
    g_conv_w = lax.dynamic_slice(red[11:14], (0, me * (d // N_DEV)), (3, d // N_DEV))

    res["w_mod"] = _adamw(g_mod[None], w_mod[0], m_w_mod[0], v_w_mod[0], "adamw_w_mod")
    update("w_in", 1, w_in, m_w_in, v_w_in, res["w_mod"][0])
    res = {k: tuple(t[None] for t in val) for k, val in res.items()}

    small = [("c_ctx", g_c_ctx, c_ctx, m_c_ctx, v_c_ctx), ("b_mod", g_b_mod, b_mod, m_b_mod, v_b_mod),
             ("norm1_g", red[0:1], norm1_g, m_norm1_g, v_norm1_g), ("conv_w", g_conv_w, conv_w, m_conv_w, v_conv_w),
             ("ret_decay_fwd", misc[1:2, :HEADS], ret_decay_fwd, m_ret_decay_fwd, v_ret_decay_fwd),
             ("ret_decay_bwd", misc[2:3, :HEADS], ret_decay_bwd, m_ret_decay_bwd, v_ret_decay_bwd),
             ("norm2_g", red[1:2], norm2_g, m_norm2_g, v_norm2_g), ("final_g", red[2:3], final_g, m_final_g, v_final_g)]

    def flat(t):
        t = t.reshape(-1)
        return jnp.pad(t, (0, (-t.shape[0]) % 1024))

    packed = [jnp.concatenate([flat(item[j]) for item in small]).reshape(-1, 128) for j in range(1, 5)]
    outs = _adamw(packed[0][None], packed[1], packed[2], packed[3], "adamw_small")
    start = 0
    for name, _, w, _, _ in small:
        size = w.size
        res[name] = tuple(o.reshape(-1)[start:start + size].reshape(w.shape) for o in outs)
        start += size + (-size) % 1024

    order = ["c_ctx", "w_mod", "b_mod", "norm1_g", "w_in", "conv_w", "w_conv_out", "ret_decay_fwd", "ret_decay_bwd",
             "w_ret_out", "w_o", "norm2_g", "w_ff1", "w_ff2", "final_g"]
    loss = misc[0, 0]
    return (loss, grad_x[None], *[res[n][0] for n in order], *[res[n][1] for n in order],
            *[res[n][2] for n in order], *[res[n][3] for n in order])
```

```python
import functools

import jax
import jax.numpy as jnp
from jax import lax
from jax.experimental import pallas as pl
from jax.experimental.pallas import tpu as pltpu

F32 = jnp.float32
BF16 = jnp.bfloat16
MESH = pl.DeviceIdType.MESH

N_DEV = 8
HEADS = 8
N_MOD = 6
N_IN = 11
GRID_W = 64
CHUNK = 256
ROPE_BASE = 10000.0
EPS = 1e-6
ADAM_LR, ADAM_B1, ADAM_B2, ADAM_EPS, ADAM_WD, ADAM_STEP = 0.001, 0.9, 0.999, 1e-08, 0.01, 10
VMEM_LIMIT = 56 * 1024 * 1024
HIGHEST = lax.Precision.HIGHEST
CB, CC, CX, CQ, CK, CV, CG, CGC, CGR = 0, 1, 2, 3, 4, 5, 7, 9, 10


def _tile(n, target, mult):
    t = (min(target, n) // mult) * mult
    while t >= mult:
        if n % t == 0:
            return t
        t -= mult
    return n


def _params(sem=None):
    return pltpu.CompilerParams(dimension_semantics=sem, vmem_limit_bytes=VMEM_LIMIT)


def _sig(v):
    return 1.0 / (1.0 + jnp.exp(-v))


def _coords():
    return lax.axis_index("x"), lax.axis_index("y"), lax.axis_index("c")


def _flip(p, m):
    return tuple(1 - v if (m >> s) & 1 else v for v, s in zip(p, (2, 1, 0)))


def _index(p):
    return 4 * p[0] + 2 * p[1] + p[2]


def _small_allgather(x, name):
    r, n = x.shape

    def body(x_ref, out_ref, send_sems, recv_sems):
        me = _coords()
        out_ref[pl.ds(_index(me), 1)] = x_ref[...][None]
        sent = []
        for m in range(1, N_DEV):
            cp = pltpu.make_async_remote_copy(
                src_ref=x_ref, dst_ref=out_ref.at[_index(me)], send_sem=send_sems.at[m - 1],
                recv_sem=recv_sems.at[m - 1], device_id=_flip(me, m), device_id_type=MESH)
            cp.start()
            sent.append(cp)
        for m in range(1, N_DEV):
            pltpu.make_async_remote_copy(
                src_ref=x_ref, dst_ref=out_ref.at[_index(_flip(me, m))], send_sem=send_sems.at[m - 1],
                recv_sem=recv_sems.at[m - 1], device_id=_flip(me, m), device_id_type=MESH).wait_recv()
        for cp in sent:
            cp.wait_send()

    return pl.pallas_call(
        body, name=name, out_shape=jax.ShapeDtypeStruct((N_DEV, r, n), x.dtype),
        in_specs=[pl.BlockSpec(memory_space=pltpu.VMEM)], out_specs=pl.BlockSpec(memory_space=pltpu.VMEM),
        scratch_shapes=[pltpu.SemaphoreType.DMA((N_DEV - 1,)), pltpu.SemaphoreType.DMA((N_DEV - 1,))],
    )(x)


def _window(ref, j, r, c, axis):
    if axis == 0:
        return ref.at[pl.ds(pl.multiple_of(j * r, 8), r), :]
    return ref.at[:, pl.ds(pl.multiple_of(j * c, 128), c)]


_HBM = pl.BlockSpec(memory_space=pltpu.HBM)
_SEM = pl.BlockSpec(memory_space=pltpu.SEMAPHORE)
_ANY = pl.BlockSpec(memory_space=pl.ANY)
_EFFECT = pltpu.SideEffectType.DATAFLOW_SIDE_EFFECTING


def _place_shard(w, axis, me, name):
    r, c = w.shape
    full = (N_DEV * r, c) if axis == 0 else (r, N_DEV * c)
    tr = _tile(r, max(16, (2 * 1024 * 1024) // (c * 4)), 16)
    blocks = r // tr

    def out_map(i, me_ref):
        return (me_ref[0] * blocks + i, 0) if axis == 0 else (i, me_ref[0])

    def body(me_ref, w_ref, out_ref):
        out_ref[...] = w_ref[...].astype(BF16)

    return pl.pallas_call(
        body, name=name, out_shape=jax.ShapeDtypeStruct(full, BF16),
        grid_spec=pltpu.PrefetchScalarGridSpec(
            num_scalar_prefetch=1, grid=(blocks,), in_specs=[pl.BlockSpec((tr, c), lambda i, me_ref: (i, 0))],
            out_specs=pl.BlockSpec((tr, c), out_map)),
        compiler_params=_params(("parallel",)))(me.reshape(1), w)


def _exchange_start(src, axis, gather, name, after=()):
    r, c = (src.shape[0] // N_DEV, src.shape[1]) if axis == 0 else (src.shape[0], src.shape[1] // N_DEV)
    n_hbm = 1 if gather else 2
    n_after = len(after)

    def body(*refs):
        src_ref, land_ref = refs[0], refs[n_hbm - 1]
        send_sems, recv_sems = refs[n_hbm + n_after:n_hbm + n_after + 2]
        token = refs[-1]
        me = _coords()
        for m in range(1, N_DEV):
            peer = _flip(me, m)
            if gather:
                mine = theirs = _window(land_ref, _index(me), r, c, axis)
            else:
                mine, theirs = _window(src_ref, _index(peer), r, c, axis), land_ref.at[m - 1]
            pltpu.make_async_remote_copy(
                src_ref=mine, dst_ref=theirs, send_sem=send_sems.at[m - 1], recv_sem=recv_sems.at[m - 1],
                device_id=peer, device_id_type=MESH).start()
        token[...] = jnp.zeros_like(token)

    hbm = [pltpu.with_memory_space_constraint(src, pltpu.HBM)]
    if not gather:
        hbm.append(pltpu.with_memory_space_constraint(lax.empty((N_DEV - 1, r, c), src.dtype), pltpu.HBM))
    outs = pl.pallas_call(
        body, name=name,
        out_shape=(pltpu.SemaphoreType.DMA((N_DEV - 1,)), pltpu.SemaphoreType.DMA((N_DEV - 1,)),
                   *[pltpu.HBM(t.shape, t.dtype) for t in hbm], jax.ShapeDtypeStruct((8, 128), F32)),
        in_specs=[_HBM] * n_hbm + [_ANY] * n_after,
        out_specs=(_SEM, _SEM, *[_HBM] * n_hbm, pl.BlockSpec(memory_space=pltpu.VMEM)),
        input_output_aliases={i: 2 + i for i in range(n_hbm)},
        compiler_params=pltpu.CompilerParams(has_side_effects=_EFFECT),
    )(*hbm, *after)
    return (outs[:2], outs[2:2 + n_hbm], (axis, gather, r, c)), outs[-1]


def _exchange_wait(handle, after, name):
    (send_sems, recv_sems), hbm, (axis, gather, r, c) = handle
    n_hbm = len(hbm)

    def body(*refs):
        src_ref, land_ref = refs[0], refs[n_hbm - 1]
        send_sems, recv_sems = refs[n_hbm:n_hbm + 2]
        me = _coords()
        for m in range(1, N_DEV):
            peer = _flip(me, m)
            if gather:
                mine, theirs = _window(land_ref, _index(me), r, c, axis), _window(land_ref, _index(peer), r, c, axis)
            else:
                mine, theirs = _window(src_ref, _index(peer), r, c, axis), land_ref.at[m - 1]
            copy = pltpu.make_async_remote_copy(
                src_ref=mine, dst_ref=theirs, send_sem=send_sems.at[m - 1], recv_sem=recv_sems.at[m - 1],
                device_id=peer, device_id_type=MESH)
            copy.wait_send()
            copy.wait_recv()

    outs = pl.pallas_call(
        body, name=name, out_shape=tuple(pltpu.HBM(t.shape, t.dtype) for t in hbm),
        in_specs=[_HBM] * n_hbm + [_SEM, _SEM, _ANY], out_specs=tuple([_HBM] * n_hbm),
        input_output_aliases={i: i for i in range(n_hbm)},
        compiler_params=pltpu.CompilerParams(has_side_effects=_EFFECT),
    )(*hbm, send_sems, recv_sems, after)
    return outs[0] if gather else tuple(outs)


def _mm(a, b, mode, out_dtypes, name, tm=1024, tn=1024, tk=2048, extras=(), epilogue=None, dep=None):
    if mode == "nn":
        (m, k), n = a.shape, b.shape[1]
    elif mode == "nt":
        (m, k), n = a.shape, b.shape[0]
    else:
        (k, m), n = a.shape, b.shape[1]
    tm, tn = _tile(m, tm, 8), _tile(n, tn, min(128, tn))
    tk = _tile(k, tk, 16 if mode == "tn" else 128)
    nk = k // tk
    swap = nk == 1 and (k * n + (n // tn) * m * k) < (m * k + (m // tm) * k * n)

    def ij(p, q):
        return (q, p) if swap else (p, q)

    def spec(shape, fn):
        return pl.BlockSpec(shape, lambda p, q, kk: fn(*ij(p, q), kk))

    a_spec = spec((tk, tm), lambda i, j, kk: (kk, i)) if mode == "tn" else spec((tm, tk), lambda i, j, kk: (i, kk))
    b_spec = spec((tn, tk), lambda i, j, kk: (j, kk)) if mode == "nt" else spec((tk, tn), lambda i, j, kk: (kk, j))
    dims = {"nn": (((1,), (0,)), ((), ())), "nt": (((1,), (1,)), ((), ())), "tn": (((0,), (0,)), ((), ()))}[mode]
    ex_specs = [spec((tm, tn), functools.partial(lambda i, j, kk, off: (i, j + off), off=off)) for _, off in extras]
    deps = [] if dep is None else [dep]
    dep_specs = [pl.BlockSpec(dep.shape, lambda p, q, kk: (0, 0))] if deps else []
    n_ex, n_out = len(extras), len(out_dtypes)
    n_in = 2 + n_ex + len(deps)

    def body(*refs):
        a_ref, b_ref = refs[0], refs[1]
        ex_refs = refs[2:2 + n_ex]
        out_refs = refs[n_in:n_in + n_out]

        def product():
            return lax.dot_general(a_ref[...], b_ref[...], dims, preferred_element_type=F32)

        def finish(res):
            res = epilogue(res, *[e[...] for e in ex_refs]) if epilogue is not None else (res,)
            for o_ref, val in zip(out_refs, res):
                o_ref[...] = val.astype(o_ref.dtype)

        if nk == 1:
            finish(product())
            return
        acc = refs[-1]
        kk = pl.program_id(2)

        @pl.when(kk == 0)
        def _():
            acc[...] = product()

        @pl.when((kk > 0) & (kk < nk - 1))
        def _():
            acc[...] += product()

        @pl.when(kk == nk - 1)
        def _():
            finish(acc[...] + product())

    outs = pl.pallas_call(
        body, name=name, grid=(*ij(m // tm, n // tn), nk),
        in_specs=[a_spec, b_spec] + ex_specs + dep_specs,
        out_specs=[spec((tm, tn), lambda i, j, kk: (i, j)) for _ in out_dtypes],
        out_shape=[jax.ShapeDtypeStruct((m, n), dt) for dt in out_dtypes],
        scratch_shapes=[pltpu.VMEM((tm, tn), F32)] if nk > 1 else [],
        compiler_params=_params(("parallel", "parallel", "arbitrary")),
    )(a, b, *[e for e, _ in extras], *deps)
    return outs if n_out > 1 else outs[0]


def _gather_project(a, w_full, order, name, riders=(), tm=768):
    m, k = a.shape
    c = w_full.shape[1] // N_DEV
    tm = _tile(m, tm, 8)
    n_i = m // tm
    chips = [4, 2, 6]
    n_rid = len(riders)
    shard = [(k, c, 1)] + [((t.shape[0] // N_DEV, t.shape[1], 0) if ax == 0 else (t.shape[0], t.shape[1] // N_DEV, 1))
                           for t, ax in riders]

    def body(order_ref, a_ref, *refs):
        outs = refs[1 + n_rid:2 + 2 * n_rid]
        z_ref, wbuf, send_sems, recv_sems, load_sems = refs[2 + 2 * n_rid:]
        w_ref = outs[0]
        p, i = pl.program_id(0), pl.program_id(1)
        me = _coords()
        sibling = _flip(me, 1)

        def copy(t, s, block, to):
            r_t, c_t, ax_t = shard[t]
            win = _window(outs[t], _index(block), r_t, c_t, ax_t)
            return pltpu.make_async_remote_copy(src_ref=win, dst_ref=win, send_sem=send_sems.at[7 * t + s],
                                                recv_sem=recv_sems.at[7 * t + s], device_id=to, device_id_type=MESH)

        def first(t):
            return [copy(t, 0, me, sibling)] + [copy(t, 1 + j, me, _flip(me, mm)) for j, mm in enumerate(chips)]

        def passed(t):
            return [copy(t, 4 + j, _flip(me, mm), sibling) for j, mm in enumerate(chips)]

        @pl.when((p == 0) & (i == 0))
        def _():
            for t in range(1 + n_rid):
                for cp in first(t):
                    cp.start()

        def load(piece):
            return pltpu.make_async_copy(_window(w_ref, order_ref[piece], k, c, 1), wbuf.at[piece % 2],
                                         load_sems.at[piece % 2])

        @pl.when((p == 0) & (i == 0))
        def _():
            load(p).start()

        arrivals = [(0, None)] + [x for j in range(3) for x in ((1 + j, passed(0)[j]), (4 + j, None))]
        for piece, (sem, forward) in enumerate(arrivals, start=1):
            @pl.when((p == piece - 1) & (i == n_i - 1))
            def _(sem=sem, forward=forward, piece=piece):
                src_block = (sibling if piece == 1 else
                             _flip(me if piece % 2 == 0 else sibling, chips[(piece - 2) // 2]))
                copy(0, sem, src_block, me).wait_recv()
                if forward is not None:
                    forward.start()
                load(p + 1).start()

        @pl.when(i == 0)
        def _():
            load(p).wait()

        @pl.when((p == N_DEV - 1) & (i == 0))
        def _():
            for t in range(1, 1 + n_rid):
                for j, mm in enumerate(chips):
                    copy(t, 1 + j, _flip(me, mm), me).wait_recv()
                    passed(t)[j].start()

        z_ref[...] = jnp.dot(a_ref[...], wbuf[p % 2], preferred_element_type=F32)

        @pl.when((p == N_DEV - 1) & (i == n_i - 1))
        def _():
            for t in range(1, 1 + n_rid):
                copy(t, 0, sibling, me).wait_recv()
                for j, mm in enumerate(chips):
                    copy(t, 4 + j, _flip(sibling, mm), me).wait_recv()
            for t in range(1 + n_rid):
                for cp in first(t) + passed(t):
                    cp.wait_send()

    gathered = [w_full] + [t for t, _ in riders]
    n_sem = 7 * len(gathered)
    return pl.pallas_call(
        body, name=name,
        out_shape=[jax.ShapeDtypeStruct(t.shape, t.dtype) for t in gathered]
        + [jax.ShapeDtypeStruct((m, N_DEV * c), F32)],
        grid_spec=pltpu.PrefetchScalarGridSpec(
            num_scalar_prefetch=1, grid=(N_DEV, n_i),
            in_specs=[pl.BlockSpec((tm, k), lambda p, i, order_ref: (i, 0))]
            + [pl.BlockSpec(memory_space=pl.ANY)] * len(gathered),
            out_specs=[pl.BlockSpec(memory_space=pl.ANY)] * len(gathered)
            + [pl.BlockSpec((tm, c), lambda p, i, order_ref: (i, order_ref[p]))],
            scratch_shapes=[pltpu.VMEM((2, k, c), w_full.dtype), pltpu.SemaphoreType.DMA((n_sem,)),
                            pltpu.SemaphoreType.DMA((n_sem,)), pltpu.SemaphoreType.DMA((2,))]),
        input_output_aliases={2 + t: t for t in range(len(gathered))},
        compiler_params=_params(("arbitrary", "arbitrary")))(order, a, *gathered)


def _rspec(tr, w, cb=0, off=0):
    return pl.BlockSpec((tr, w), lambda i: (i + off, cb))


def _cspec(shape):
    return pl.BlockSpec(shape, lambda i: (0,) * len(shape))


def _rms(xf):
    rstd = lax.rsqrt(jnp.mean(xf * xf, axis=-1, keepdims=True) + EPS)
    return xf * rstd, rstd


def _rms_bwd(dn, n, rstd):
    return rstd * (dn - n * jnp.mean(dn * n, axis=-1, keepdims=True))


def _colsum(v):
    return jnp.sum(v, axis=0, keepdims=True)


def _total(v):
    return jnp.sum(jnp.sum(v, axis=1, keepdims=True), axis=0, keepdims=True)


def _modulate(x, vec, name, total_rows, into=None):
    rows, d = x.shape
    tr = _tile(rows, 256, 8)
    off = 0 if into is None else (total_rows - rows) // tr

    def body(x_ref, v_ref, *refs):
        n, _ = _rms(x_ref[...])
        refs[-1][...] = ((n * v_ref[0:1, :]) * (1.0 + v_ref[2:3, :]) + v_ref[1:2, :]).astype(BF16)

    return pl.pallas_call(
        body, name=name, grid=(rows // tr,),
        in_specs=[_rspec(tr, d), _cspec(vec.shape)] + ([] if into is None else [_ANY]),
        out_specs=_rspec(tr, d, off=off), out_shape=jax.ShapeDtypeStruct((total_rows, d), BF16),
        input_output_aliases={} if into is None else {2: 0},
        compiler_params=_params(("parallel",)))(x, vec, *([] if into is None else [into]))


def _resid_modulate(x, y, vec, name):
    rows, d = x.shape
    tr = _tile(rows, 256, 8)

    def body(x_ref, y_ref, v_ref, x1_ref, a_ref):
        x1 = x_ref[...] + v_ref[3:4, :] * y_ref[...]
        x1_ref[...] = x1
        n, _ = _rms(x1)
        a_ref[...] = ((n * v_ref[0:1, :]) * (1.0 + v_ref[2:3, :]) + v_ref[1:2, :]).astype(BF16)

    return pl.pallas_call(
        body, name=name, grid=(rows // tr,), in_specs=[_rspec(tr, d), _rspec(tr, d), _cspec(vec.shape)],
        out_specs=[_rspec(tr, d), _rspec(tr, d)],
        out_shape=[jax.ShapeDtypeStruct((rows, d), F32), jax.ShapeDtypeStruct((rows, d), BF16)],
        compiler_params=_params(("parallel",)))(x, y, vec)


def _loss_head(x1, f, target, vec, name):
    rows, d = x1.shape
    tr = _tile(rows, 256, 8)

    def body(x1_ref, f_ref, t_ref, v_ref, dx2_ref, df_ref, acc_ref):
        @pl.when(pl.program_id(0) == 0)
        def _():
            acc_ref[...] = jnp.zeros_like(acc_ref)

        gate, gain = v_ref[0:1, :], v_ref[1:2, :]
        fv = f_ref[...]
        n, rstd = _rms(x1_ref[...] + gate * fv)
        err = n * gain - t_ref[...]
        dy = err / d
        dx2 = _rms_bwd(dy * gain, n, rstd)
        dx2_ref[...] = dx2
        df_ref[...] = (dx2 * gate).astype(BF16)
        acc_ref[0:1, :] += _colsum(dx2 * fv)
        acc_ref[1:2, :] += _colsum(dy * n)
        acc_ref[2:3, :] += _colsum(err * err)

    return pl.pallas_call(
        body, name=name, grid=(rows // tr,),
        in_specs=[_rspec(tr, d), _rspec(tr, d), _rspec(tr, d), _cspec(vec.shape)],
        out_specs=[_rspec(tr, d), _rspec(tr, d), _cspec((8, d))],
        out_shape=[jax.ShapeDtypeStruct((rows, d), F32), jax.ShapeDtypeStruct((rows, d), BF16),
                   jax.ShapeDtypeStruct((8, d), F32)],
        compiler_params=_params(("arbitrary",)))(x1, f, target, vec)


def _modulate_bwd(da, x, vec, name, da_off=0, dx_in=None, y=None):
    rows, d = x.shape
    tr = _tile(rows, 256, 8)
    off = da_off // tr
    has_dx, has_y = dx_in is not None, y is not None

    def body(*refs):
        da_ref, x_ref, v_ref = refs[0], refs[1], refs[2]
        pos = 3
        dxin_ref = refs[pos] if has_dx else None
        pos += has_dx
        y_ref = refs[pos] if has_y else None
        pos += has_y
        dx_ref = refs[pos]
        dy_ref = refs[pos + 1] if has_y else None
        acc_ref = refs[-1]

        @pl.when(pl.program_id(0) == 0)
        def _():
            acc_ref[...] = jnp.zeros_like(acc_ref)

        gain, scale1 = v_ref[0:1, :], 1.0 + v_ref[2:3, :]
        dav = da_ref[...]
        n, rstd = _rms(x_ref[...])
        dx = _rms_bwd(dav * gain * scale1, n, rstd)
        if has_dx:
            dx = dx + dxin_ref[...]
        dx_ref[...] = dx
        acc_ref[0:1, :] += _colsum(dav)
        acc_ref[1:2, :] += _colsum(dav * (n * gain))
        acc_ref[2:3, :] += _colsum(dav * n * scale1)
        if has_y:
            acc_ref[3:4, :] += _colsum(dx * y_ref[...])
            dy_ref[...] = (dx * v_ref[3:4, :]).astype(BF16)

    ins = [da, x, vec] + ([dx_in] if has_dx else []) + ([y] if has_y else [])
    in_specs = [_rspec(tr, d, off=off), _rspec(tr, d), _cspec(vec.shape)] + [_rspec(tr, d)] * (has_dx + has_y)
    out_specs = [_rspec(tr, d)] + ([_rspec(tr, d)] if has_y else []) + [_cspec((8, d))]
    out_shape = ([jax.ShapeDtypeStruct((rows, d), F32)] + ([jax.ShapeDtypeStruct((rows, d), BF16)] if has_y else [])
                 + [jax.ShapeDtypeStruct((8, d), F32)])
    return pl.pallas_call(
        body, name=name, grid=(rows // tr,), in_specs=in_specs, out_specs=out_specs, out_shape=out_shape,
        compiler_params=_params(("arbitrary",)))(*ins)


def _conv_terms(cc, cx, w_ref, tr):
    t = lax.broadcasted_iota(jnp.int32, (tr, 1), 0) % GRID_W
    first, last = t == 0, t == GRID_W - 1
    u = cc * cx
    prev = jnp.where(first, 0.0, pltpu.roll(u, 1, 0))
    nxt = jnp.where(last, 0.0, pltpu.roll(u, tr - 1, 0))
    y = w_ref[0:1, :] * prev + w_ref[1:2, :] * u + w_ref[2:3, :] * nxt
    return u, prev, nxt, y, first, last


def _conv_fwd(z, conv_w, rows, name, dep):
    d = conv_w.shape[1]
    tr = _tile(rows, 256, GRID_W)

    def body(cb_ref, cc_ref, cx_ref, w_ref, dep_ref, out_ref):
        y = _conv_terms(cc_ref[...], cx_ref[...], w_ref, tr)[3]
        out_ref[...] = (cb_ref[...] * y).astype(BF16)

    return pl.pallas_call(
        body, name=name, grid=(rows // tr,),
        in_specs=[_rspec(tr, d, CB), _rspec(tr, d, CC), _rspec(tr, d, CX), _cspec(conv_w.shape), _cspec(dep.shape)],
        out_specs=_rspec(tr, d), out_shape=jax.ShapeDtypeStruct((rows, d), BF16),
        compiler_params=_params(("parallel",)))(z, z, z, conv_w, dep)


def _conv_bwd(dpc, z, conv_w, name):
    rows, d = dpc.shape
    tr = _tile(rows, 256, GRID_W)

    def body(dpc_ref, cb_ref, cc_ref, cx_ref, w_ref, dz_ref, acc_ref):
        @pl.when(pl.program_id(0) == 0)
        def _():
            acc_ref[...] = jnp.zeros_like(acc_ref)

        cc, cx, dpcv = cc_ref[...], cx_ref[...], dpc_ref[...]
        u, prev, nxt, y, first, last = _conv_terms(cc, cx, w_ref, tr)
        dy = dpcv * cb_ref[...]
        dy_next = jnp.where(last, 0.0, pltpu.roll(dy, tr - 1, 0))
        dy_prev = jnp.where(first, 0.0, pltpu.roll(dy, 1, 0))
        du = w_ref[0:1, :] * dy_next + w_ref[1:2, :] * dy + w_ref[2:3, :] * dy_prev
        dz_ref[:, 0:d] = (dpcv * y).astype(BF16)
        dz_ref[:, d:2 * d] = (du * cx).astype(BF16)
        dz_ref[:, 2 * d:3 * d] = (du * cc).astype(BF16)
        acc_ref[0:1, :] += _colsum(dy * prev)
        acc_ref[1:2, :] += _colsum(dy * u)
        acc_ref[2:3, :] += _colsum(dy * nxt)

    return pl.pallas_call(
        body, name=name, grid=(rows // tr,),
        in_specs=[_rspec(tr, d), _rspec(tr, d, CB), _rspec(tr, d, CC), _rspec(tr, d, CX), _cspec(conv_w.shape)],
        out_specs=[_rspec(tr, 3 * d), _cspec((8, d))],
        out_shape=[jax.ShapeDtypeStruct((rows, 3 * d), BF16), jax.ShapeDtypeStruct((8, d), F32)],
        compiler_params=_params(("arbitrary",)))(dpc, z, z, z, conv_w)


def _rotary_fwd(z, cos, sin, n_lat, name):
    rows = z.shape[0]
    d = z.shape[1] // N_IN
    dk = d // HEADS
    half = dk // 2
    tr = _tile(n_lat, 256, 8)
    tr = _tile(rows - n_lat, tr, 8)
    lat_blocks, all_blocks = n_lat // tr, rows // tr
    ctx_blocks = all_blocks - lat_blocks
    scan_rows = rows + ctx_blocks * tr

    def z_block(g):
        return jnp.where(g < all_blocks, g, g - ctx_blocks)

    def in_spec(w, cb=0):
        return pl.BlockSpec((tr, w), lambda g: (z_block(g), cb))

    def out_spec(w):
        return pl.BlockSpec((tr, w), lambda g: (jnp.where(g < all_blocks, g + ctx_blocks, g - all_blocks), 0))

    def body(q_ref, k_ref, v0_ref, v1_ref, cos_ref, sin_ref, qo_ref, ko_ref, vo_ref):
        cs, sn = cos_ref[...], sin_ref[...]
        keep = jnp.where(z_block(pl.program_id(0)) < lat_blocks, 1.0, 0.0)
        for src, dst, scale in ((q_ref, qo_ref, keep), (k_ref, ko_ref, dk ** -0.5)):
            for h in range(HEADS):
                lo, mid, hi = h * dk, h * dk + half, (h + 1) * dk
                t1, t2 = src[:, lo:mid], src[:, mid:hi]
                dst[:, lo:mid] = ((t1 * cs - t2 * sn) * scale).astype(BF16)
                dst[:, mid:hi] = ((t1 * sn + t2 * cs) * scale).astype(BF16)
        vo_ref[:, 0:d] = v0_ref[...].astype(BF16)
        vo_ref[:, d:2 * d] = v1_ref[...].astype(BF16)

    return pl.pallas_call(
        body, name=name, grid=(all_blocks + ctx_blocks,),
        in_specs=[in_spec(d, CQ), in_spec(d, CK), in_spec(d, CV), in_spec(d, CV + 1), in_spec(half), in_spec(half)],
        out_specs=[out_spec(d), out_spec(d), out_spec(2 * d)],
        out_shape=[jax.ShapeDtypeStruct((scan_rows, d), BF16), jax.ShapeDtypeStruct((scan_rows, d), BF16),
                   jax.ShapeDtypeStruct((scan_rows, 2 * d), BF16)],
        compiler_params=_params(("parallel",)))(z, z, z, z, cos, sin)


def _dz_assemble(dq_f, dq_b, dk_f, dk_b, dv_f, dv_b, cos, sin, dz_conv, dz_g, dz_gc, dz_gr, n_lat, n_ctx, name):
    rows = n_lat + n_ctx
    d = dq_f.shape[1]
    dk = d // HEADS
    half = dk // 2
    tr = _tile(n_ctx, 128, 8)
    lat_blocks, ctx_blocks = n_lat // tr, n_ctx // tr

    def fmap(i):
        return (jnp.where(i < lat_blocks, i + ctx_blocks, i - lat_blocks), 0)

    def bmap(i):
        return (i + ctx_blocks, 0)

    def lmap(i):
        return (jnp.minimum(i, lat_blocks - 1), 0)

    def body(qf_ref, qb_ref, kf_ref, kb_ref, vf_ref, vb_ref, cos_ref, sin_ref, c_ref, g_ref, gc_ref, gr_ref, out_ref):
        cs, sn = cos_ref[...], sin_ref[...]
        is_lat = pl.program_id(0) < lat_blocks
        keep = jnp.where(is_lat, 1.0, 0.0)
        for fa, fb, base, scale in ((qf_ref, qb_ref, CQ * d, keep), (kf_ref, kb_ref, CK * d, dk ** -0.5)):
            for h in range(HEADS):
                lo, mid, hi = h * dk, h * dk + half, (h + 1) * dk
                d1 = fa[:, lo:mid].astype(F32) + fb[:, lo:mid].astype(F32)
                d2 = fa[:, mid:hi].astype(F32) + fb[:, mid:hi].astype(F32)
                out_ref[:, base + lo:base + mid] = ((d1 * cs + d2 * sn) * scale).astype(BF16)
                out_ref[:, base + mid:base + hi] = ((d2 * cs - d1 * sn) * scale).astype(BF16)
        out_ref[:, CV * d:CG * d] = (vf_ref[...].astype(F32) + vb_ref[...].astype(F32)).astype(BF16)
        for src, lo, hi in ((c_ref, CB * d, CQ * d), (g_ref, CG * d, CGC * d), (gc_ref, CGC * d, CGR * d),
                            (gr_ref, CGR * d, N_IN * d)):
            out_ref[:, lo:hi] = jnp.where(is_lat, src[...], jnp.zeros_like(src))

    return pl.pallas_call(
        body, name=name, grid=(rows // tr,),
        in_specs=[pl.BlockSpec((tr, d), fmap), pl.BlockSpec((tr, d), bmap), pl.BlockSpec((tr, d), fmap),
                  pl.BlockSpec((tr, d), bmap), pl.BlockSpec((tr, 2 * d), fmap), pl.BlockSpec((tr, 2 * d), bmap),
                  _rspec(tr, half), _rspec(tr, half), pl.BlockSpec((tr, 3 * d), lmap), pl.BlockSpec((tr, 2 * d), lmap),
                  pl.BlockSpec((tr, d), lmap), pl.BlockSpec((tr, d), lmap)],
        out_specs=_rspec(tr, N_IN * d), out_shape=jax.ShapeDtypeStruct((rows, N_IN * d), BF16),
        compiler_params=_params(("parallel",)))(dq_f, dq_b, dk_f, dk_b, dv_f, dv_b, cos, sin, dz_conv, dz_g, dz_gc, dz_gr)


def _gn_fwd(o, z, name):
    rows = o.shape[0]
    d = z.shape[1] // N_IN
    dv = 2 * d // HEADS
    tr = _tile(rows, 128, 8)

    def body(o_ref, g0_ref, g1_ref, r_ref):
        for h in range(HEADS):
            lo, hi = h * dv, (h + 1) * dv
            g_ref, glo = (g0_ref, lo) if hi <= d else (g1_ref, lo - d)
            o = o_ref[:, lo:hi]
            cen = o - jnp.mean(o, axis=-1, keepdims=True)
            on = cen * lax.rsqrt(jnp.mean(cen * cen, axis=-1, keepdims=True) + EPS)
            g = g_ref[:, glo:glo + dv]
            r_ref[:, lo:hi] = (g * _sig(g) * on).astype(BF16)

    return pl.pallas_call(
        body, name=name, grid=(rows // tr,),
        in_specs=[_rspec(tr, 2 * d), _rspec(tr, d, CG), _rspec(tr, d, CG + 1)],
        out_specs=_rspec(tr, 2 * d), out_shape=jax.ShapeDtypeStruct((rows, 2 * d), BF16),
        compiler_params=_params(("parallel",)))(o, z, z)


def _gn_bwd_head(dr, o, g):
    cen = o - jnp.mean(o, axis=-1, keepdims=True)
    rstd = lax.rsqrt(jnp.mean(cen * cen, axis=-1, keepdims=True) + EPS)
    on = cen * rstd
    sg = _sig(g)
    don = dr * (g * sg)
    do = rstd * (don - jnp.mean(don, axis=-1, keepdims=True) - on * jnp.mean(don * on, axis=-1, keepdims=True))
    return do, dr * on * (sg * (1.0 + g * (1.0 - sg)))


def _decays(lg, rev):
    row = lax.broadcasted_iota(jnp.int32, (CHUNK, CHUNK), 0)
    col = lax.broadcasted_iota(jnp.int32, (CHUNK, CHUNK), 1)
    rel = ((col - row) if rev else (row - col)).astype(F32)
    mask = jnp.where(rel >= 0, jnp.exp(lg * jnp.maximum(rel, 0.0)), 0.0)
    r = lax.broadcasted_iota(jnp.int32, (CHUNK, 1), 0)
    rr = ((CHUNK - 1 - r) if rev else r).astype(F32)
    chunk_decay = jnp.exp(lg * jnp.full((1, 1), float(CHUNK), F32))
    return (rr, rel), mask, jnp.exp(lg * (rr + 1.0)), jnp.exp(lg * (CHUNK - 1.0 - rr)), chunk_decay


_NT = (((1,), (1,)), ((), ()))
_TN = (((0,), (0,)), ((), ()))


def _dot(a, b, dims=None):
    if dims is None:
        return jnp.dot(a, b, preferred_element_type=F32)
    return lax.dot_general(a, b, dims, preferred_element_type=F32)


def _ret_fwd(q, k, v, lgs, n_lat, name):
    tt, d = q.shape
    dk, dv = d // HEADS, 2 * d // HEADS
    nc = tt // CHUNK
    lat_chunks = n_lat // CHUNK
    ctx_chunks = (nc - lat_chunks) // 2

    def chunk_of(rev, i):
        return (nc - 1 - i) if rev else i

    def cmap(rev):
        return lambda h, i: (chunk_of(rev, i), h)

    def smap(rev):
        return lambda h, i: (chunk_of(rev, i), h, 0, 0)

    def body(lg_ref, qf_ref, kf_ref, vf_ref, qb_ref, kb_ref, vb_ref, o_ref, sf_ref, sb_ref, state_f, state_b):
        h, i = pl.program_id(0), pl.program_id(1)

        @pl.when(i == 0)
        def _():
            state_f[...] = jnp.zeros_like(state_f)
            state_b[...] = jnp.zeros_like(state_b)

        for rev, (q_ref, k_ref, v_ref, s_ref, state) in enumerate((
                (qf_ref, kf_ref, vf_ref, sf_ref, state_f), (qb_ref, kb_ref, vb_ref, sb_ref, state_b))):
            _, mask, qd, kd, cd = _decays(lg_ref[rev, h], bool(rev))
            qv, kv, vv = q_ref[...], k_ref[...], v_ref[...]
            st = state[...]
            p = _dot(qv, kv, _NT) * mask
            out = _dot(p.astype(BF16), vv) + _dot((qv * qd).astype(BF16), st.astype(BF16))
            s_ref[...] = st
            state[...] = cd * st + _dot((kv * kd).astype(BF16), vv, _TN)
            lat_chunk = chunk_of(bool(rev), i) - ctx_chunks
            is_lat = (lat_chunk >= 0) & (lat_chunk < lat_chunks)
            first = (2 * i < nc - 1) if rev else (2 * i <= nc - 1)

            @pl.when(is_lat & first)
            def _(lat_chunk=lat_chunk, out=out):
                o_ref[pl.ds(pl.multiple_of(lat_chunk * CHUNK, CHUNK), CHUNK), :] = out

            @pl.when(is_lat & jnp.logical_not(first))
            def _(lat_chunk=lat_chunk, out=out):
                o_ref[pl.ds(pl.multiple_of(lat_chunk * CHUNK, CHUNK), CHUNK), :] += out

    def specs(rev):
        return [pl.BlockSpec((CHUNK, dk), cmap(rev)), pl.BlockSpec((CHUNK, dk), cmap(rev)),
                pl.BlockSpec((CHUNK, dv), cmap(rev))]

    state_shape = jax.ShapeDtypeStruct((nc, HEADS, dk, dv), F32)
    return pl.pallas_call(
        body, name=name, grid=(HEADS, nc),
        in_specs=[pl.BlockSpec(memory_space=pltpu.SMEM)] + specs(False) + specs(True),
        out_specs=[pl.BlockSpec((n_lat, dv), lambda h, i: (0, h)), pl.BlockSpec((None, None, dk, dv), smap(False)),
                   pl.BlockSpec((None, None, dk, dv), smap(True))],
        out_shape=[jax.ShapeDtypeStruct((n_lat, 2 * d), F32), state_shape, state_shape],
        scratch_shapes=[pltpu.VMEM((dk, dv), F32), pltpu.VMEM((dk, dv), F32)],
        compiler_params=_params(("parallel", "arbitrary")))(lgs, q, k, v, q, k, v)


def _ret_bwd(q, k, v, do, states_f, states_b, lgs, name):
    tt, d = q.shape
    dk, dv = d // HEADS, 2 * d // HEADS
    nc = tt // CHUNK
    lat_chunks = do.shape[0] // CHUNK
    ctx_chunks = (nc - lat_chunks) // 2

    def chunk_of(rev, i):
        return i if rev else nc - 1 - i

    def cmap(rev):
        return lambda h, i: (chunk_of(rev, i), h)

    def do_map(rev):
        return lambda h, i: (jnp.clip(chunk_of(rev, i) - ctx_chunks, 0, lat_chunks - 1), h)

    def smap(rev):
        return lambda h, i: (chunk_of(rev, i), h, 0, 0)

    def body(lg_ref, *refs):
        h = pl.program_id(0)
        dstates = refs[-2:]

        @pl.when(pl.program_id(1) == 0)
        def _():
            for rev in (0, 1):
                dstates[rev][...] = jnp.zeros_like(dstates[rev])
                refs[10 + 4 * rev + 3][...] = jnp.zeros_like(refs[10 + 4 * rev + 3])

        for rev in (0, 1):
            q_ref, k_ref, v_ref, do_ref, s_ref = refs[5 * rev:5 * rev + 5]
            dq_ref, dk_ref, dv_ref, dlg_ref = refs[10 + 4 * rev:10 + 4 * rev + 4]
            dstate = dstates[rev]
            chunk = chunk_of(bool(rev), pl.program_id(1))
            is_lat = (chunk >= ctx_chunks) & (chunk < ctx_chunks + lat_chunks)
            (rr, rel), mask, qd, kd, cd = _decays(lg_ref[rev, h], bool(rev))
            qv, kv, vv = q_ref[...], k_ref[...], v_ref[...]
            dov = jnp.where(is_lat, do_ref[...], jnp.zeros_like(do_ref))
            st, dst = s_ref[...], dstate[...]
            st_b, dst_b = st.astype(BF16), dst.astype(BF16)
            p = _dot(qv, kv, _NT) * mask
            dp = _dot(dov, vv, _NT)
            da = (dp * mask).astype(BF16)
            dq_state = _dot(dov, st_b, _NT) * qd
            dk_state = _dot(vv, dst_b, _NT) * kd
            dq_ref[...] = (_dot(da, kv) + dq_state).astype(BF16)
            dk_ref[...] = (_dot(da, qv, _TN) + dk_state).astype(BF16)
            dv_ref[...] = (_dot(p.astype(BF16), dov, _TN) + _dot((kv * kd).astype(BF16), dst_b)).astype(BF16)
            dnew = cd * dst + _dot((qv * qd).astype(BF16), dov, _TN)
            dstate[...] = dnew
            through = rr * (jnp.sum(qv.astype(F32) * dq_state, axis=1, keepdims=True)
                            - jnp.sum(kv.astype(F32) * dk_state, axis=1, keepdims=True))
            dlg_ref[...] += _total(rel * p * dp) + _total(through) + CHUNK * _total(st * dnew)

    def specs(rev):
        return [pl.BlockSpec((CHUNK, dk), cmap(rev)), pl.BlockSpec((CHUNK, dk), cmap(rev)),
                pl.BlockSpec((CHUNK, dv), cmap(rev)), pl.BlockSpec((CHUNK, dv), do_map(rev)),
                pl.BlockSpec((None, None, dk, dv), smap(rev))]

    def outs(rev):
        return [pl.BlockSpec((CHUNK, dk), cmap(rev)), pl.BlockSpec((CHUNK, dk), cmap(rev)),
                pl.BlockSpec((CHUNK, dv), cmap(rev)), pl.BlockSpec((None, 8, 128), lambda h, i: (h, 0, 0))]

    return pl.pallas_call(
        body, name=name, grid=(HEADS, nc),
        in_specs=[pl.BlockSpec(memory_space=pltpu.SMEM)] + specs(False) + specs(True),
        out_specs=outs(False) + outs(True),
        out_shape=[jax.ShapeDtypeStruct((tt, d), BF16), jax.ShapeDtypeStruct((tt, d), BF16),
                   jax.ShapeDtypeStruct((tt, 2 * d), BF16), jax.ShapeDtypeStruct((HEADS, 8, 128), F32)] * 2,
        scratch_shapes=[pltpu.VMEM((dk, dv), F32), pltpu.VMEM((dk, dv), F32)],
        compiler_params=_params(("parallel", "arbitrary")))(lgs, q, k, v, do, states_f, q, k, v, do, states_b)


def _silu(v):
    return v * _sig(v)


def _mod_proj(cs, w_mod, b_loc, dec, name):
    nb = w_mod.shape[1]

    def body(cs_ref, w_ref, b_ref, dec_ref, out_ref, lg_ref):
        out_ref[...] = jnp.dot(_silu(cs_ref[...]), w_ref[...], preferred_element_type=F32, precision=HIGHEST) + b_ref[...]
        a = dec_ref[...]
        lg_ref[...] = jnp.minimum(a, 0.0) - jnp.log1p(jnp.exp(-jnp.abs(a)))

    return pl.pallas_call(
        body, name=name,
        out_shape=[jax.ShapeDtypeStruct((16, nb), F32), jax.ShapeDtypeStruct(dec.shape, F32)],
        compiler_params=_params())(cs, w_mod, b_loc, dec)


def _mod_grad(cs, dm, w_mod, name):
    d, nb = w_mod.shape

    def body(cs_ref, dm_ref, w_ref, gw_ref, part_ref):
        dmv = dm_ref[...]
        gw_ref[...] = lax.dot_general(_silu(cs_ref[...]), dmv, _TN, preferred_element_type=F32, precision=HIGHEST)
        part_ref[...] = lax.dot_general(dmv, w_ref[...], _NT, preferred_element_type=F32, precision=HIGHEST)

    return pl.pallas_call(
        body, name=name,
        out_shape=[jax.ShapeDtypeStruct((d, nb), F32), jax.ShapeDtypeStruct((16, d), F32)],
        compiler_params=_params())(cs, dm, w_mod)


def _reduce_small(gathered, dec, n_feat, name):
    _, rows, d = gathered.shape

    def body(g_ref, dec_ref, red_ref, misc_ref):
        total = g_ref[0]
        for i in range(1, N_DEV):
            total = total + g_ref[i]
        red_ref[...] = total
        misc_ref[...] = jnp.zeros_like(misc_ref)
        misc_ref[0:1, :] = jnp.zeros((1, 128), F32) + (0.5 / n_feat) * _total(total[16:17, :])
        misc_ref[1:3, :] = total[14:16, 0:128] * _sig(-dec_ref[0:2, :])

    return pl.pallas_call(
        body, name=name,
        out_shape=[jax.ShapeDtypeStruct((rows, d), F32), jax.ShapeDtypeStruct((8, 128), F32)],
        compiler_params=_params())(gathered, dec)


def _c_ctx_grad(parts, c_ctx, name):
    d = c_ctx.shape[1]

    def body(p_ref, c_ref, out_ref):
        total = p_ref[0]
        for i in range(1, N_DEV):
            total = total + p_ref[i]
        cv = c_ref[...]
        sg = _sig(cv)
        out_ref[...] = total[8:9, :] * (sg * (1.0 + cv * (1.0 - sg)))

    return pl.pallas_call(body, name=name, out_shape=jax.ShapeDtypeStruct((1, d), F32),
                          compiler_params=_params())(parts, c_ctx)


def _adamw(parts, w, m, v, name, own=None):
    n_parts, rows, cols = parts.shape
    row_bytes = cols * (parts.dtype.itemsize * (n_parts + 1) + 7 * 4)
    tr = _tile(rows, max(16, (8 * 1024 * 1024) // row_bytes), 16 if rows % 16 == 0 else 8)
    blocks = rows // tr

    def body(*refs):
        p_ref, w_ref, m_ref, v_ref, g_ref, d_ref, mo_ref, vo_ref = refs[-8:]
        if own is None:
            g = p_ref[0].astype(F32)
        else:
            g = refs[1][...].astype(F32) + p_ref[0].astype(F32)
        for i in range(1, n_parts):
            g = g + p_ref[i].astype(F32)
        m2 = ADAM_B1 * m_ref[...] + (1.0 - ADAM_B1) * g
        v2 = ADAM_B2 * v_ref[...] + (1.0 - ADAM_B2) * jnp.square(g)
        m_hat = m2 / (1.0 - ADAM_B1 ** ADAM_STEP)
        v_hat = v2 / (1.0 - ADAM_B2 ** ADAM_STEP)
        g_ref[...] = g
        d_ref[...] = -ADAM_LR * (m_hat / (jnp.sqrt(v_hat) + ADAM_EPS) + ADAM_WD * w_ref[...])
        mo_ref[...] = m2
        vo_ref[...] = v2

    out_shape = [jax.ShapeDtypeStruct((rows, cols), F32)] * 4
    if own is None:
        spec = _rspec(tr, cols)
        return pl.pallas_call(
            body, name=name, grid=(blocks,),
            in_specs=[pl.BlockSpec((n_parts, tr, cols), lambda i: (0, i, 0)), spec, spec, spec],
            out_specs=[spec] * 4, out_shape=out_shape, compiler_params=_params(("parallel",)))(parts, w, m, v)
    g_full, axis, me = own

    def own_map(i, me_ref):
        return (me_ref[0] * blocks + i, 0) if axis == 0 else (i, me_ref[0])

    spec = pl.BlockSpec((tr, cols), lambda i, me_ref: (i, 0))
    return pl.pallas_call(
        body, name=name, out_shape=out_shape,
        grid_spec=pltpu.PrefetchScalarGridSpec(
            num_scalar_prefetch=1, grid=(blocks,),
            in_specs=[pl.BlockSpec((tr, cols), own_map),
                      pl.BlockSpec((n_parts, tr, cols), lambda i, me_ref: (0, i, 0)), spec, spec, spec],
            out_specs=[spec] * 4),
        compiler_params=_params(("parallel",)))(me.reshape(1), g_full, parts, w, m, v)


def _rope_tables(pos, dk):
    half = dk // 2
    inv_freq = 1.0 / (ROPE_BASE ** jnp.linspace(0.0, 1.0, half, dtype=F32))
    ang = pos[:, None] * inv_freq[None, :]
    return jnp.cos(ang), jnp.sin(ang)


def _pad_lanes(v, width):
    return jnp.pad(v, ((0, 0), (0, width - v.shape[1])))


def kernel(x, c, ctx, c_ctx, w_mod, b_mod, norm1_g, w_in, conv_w, w_conv_out, ret_decay_fwd, ret_decay_bwd, w_ret_out, w_o, norm2_g, w_ff1, w_ff2, final_g, loss_target, m_c_ctx, m_w_mod, m_b_mod, m_norm1_g, m_w_in, m_conv_w, m_w_conv_out, m_ret_decay_fwd, m_ret_decay_bwd, m_w_ret_out, m_w_o, m_norm2_g, m_w_ff1, m_w_ff2, m_final_g, v_c_ctx, v_w_mod, v_b_mod, v_norm1_g, v_w_in, v_conv_w, v_w_conv_out, v_ret_decay_fwd, v_ret_decay_bwd, v_w_ret_out, v_w_o, v_norm2_g, v_w_ff1, v_w_ff2, v_final_g):
    n_lat, d = x.shape[1], x.shape[2]
    n_ctx = ctx.shape[1]
    assert n_ctx % CHUNK == 0 and n_lat % CHUNK == 0 and n_lat % GRID_W == 0
    dk = d // HEADS
    nb = w_mod.shape[2]
    me = 4 * lax.axis_index("x") + 2 * lax.axis_index("y") + lax.axis_index("c")
    xl, ctxl, target = x[0], ctx[0], loss_target[0]

    conv_w_f = _small_allgather(conv_w[0], "ag_conv_w").transpose(1, 0, 2).reshape(3, d)

    c_all = _small_allgather(c, "ag_c").reshape(N_DEV, d)
    cs = jnp.concatenate([c_all, c_ctx[None], jnp.zeros((7, d), F32)], axis=0)
    dec = jnp.pad(jnp.concatenate([ret_decay_fwd, ret_decay_bwd], axis=0), ((0, 6), (0, 128 - HEADS)))
    b_loc = lax.dynamic_slice(b_mod, (0, me * nb), (1, nb))
    modp, lgs = _mod_proj(cs, w_mod[0], b_loc, dec, "mod_proj")
    modp_all = _small_allgather(modp, "ag_mod")
    mod_l = lax.dynamic_index_in_dim(modp_all, me, axis=1, keepdims=False).reshape(N_MOD, d)
    mod_c = modp_all[:, 8, :].reshape(N_MOD, d)
    lgs = lgs[0:2, :HEADS]
    zero_row = jnp.zeros((1, d), F32)
    vec1_l = jnp.concatenate([norm1_g, mod_l[0:1], mod_l[1:2], zero_row], axis=0)
    vec1_c = jnp.concatenate([norm1_g, mod_c[0:1], mod_c[1:2], zero_row], axis=0)
    vec2 = jnp.concatenate([norm2_g, mod_l[3:4], mod_l[4:5], mod_l[2:3]], axis=0)
    vec3 = jnp.concatenate([mod_l[5:6], final_g[None]], axis=0)

    a_all = _modulate(ctxl, vec1_c, "modulate1_ctx", n_lat + n_ctx,
                      into=_modulate(xl, vec1_l, "modulate1", n_lat + n_ctx))
    arrival = me ^ jnp.array([0, 1, 4, 5, 2, 3, 6, 7], jnp.int32)
    riders = [(_place_shard(w[0], 0, me, "place_" + wname), 0)
              for wname, w in (("w_conv_out", w_conv_out), ("w_ret_out", w_ret_out), ("w_o", w_o))]
    w_in_f, w_conv_out_f, w_ret_out_f, w_o_f, z = _gather_project(
        a_all, _place_shard(w_in[0], 1, me, "place_w_in"), arrival, "proj_in", riders=riders)

    later, after = {}, (z, conv_w_f)
    for wname, w, axis in (("w_ff1", w_ff1, 1), ("w_ff2", w_ff2, 0)):
        later[wname], token = _exchange_start(_place_shard(w[0], axis, me, "place_" + wname), axis, True,
                                              "ag_" + wname + "_start", after=after)
        after = (token,)
    pc = _conv_fwd(z, conv_w_f, n_lat, "conv_fwd", token)
    pos = jnp.concatenate([n_ctx + jnp.arange(n_lat, dtype=F32), jnp.arange(n_ctx, dtype=F32)])
    cos, sin = _rope_tables(pos, dk)
    q_s, k_s, v_s = _rotary_fwd(z, cos, sin, n_lat, "rotary_fwd")
    o, st_f, st_b = _ret_fwd(q_s, k_s, v_s, lgs, n_lat, "ret_fwd")
    r = _gn_fwd(o, z, "gn_fwd")
    y_conv = _mm(pc, w_conv_out_f, "nn", [F32], "proj_conv_out", tn=2048)
    tn_d = _tile(d, 1024, 128)
    gate_offs = (CGC * d // tn_d, CGR * d // tn_d)

    def merge(acc, yc, gc, gr):
        return acc, _sig(gc) * yc + _sig(gr) * acc

    y_ret, mg = _mm(r, w_ret_out_f, "nn", [F32, BF16], "proj_ret_out", tm=512, tn=tn_d,
                    extras=[(y_conv, 0), (z, gate_offs[0]), (z, gate_offs[1])], epilogue=merge)
    y_l = _mm(mg, w_o_f, "nn", [F32], "proj_o", tn=2048)
    x1, a2 = _resid_modulate(xl, y_l, vec2, "resid_modulate2")

    def sqrelu(acc):
        return acc, jnp.square(jnp.maximum(acc, 0.0))

    w_ff1_f = _exchange_wait(later["w_ff1"], a2, "ag_w_ff1_wait")
    hff, s = _mm(a2, w_ff1_f, "nn", [BF16, BF16], "ff1", epilogue=sqrelu)
    w_ff2_f = _exchange_wait(later["w_ff2"], s, "ag_w_ff2_wait")
    f = _mm(s, w_ff2_f, "nn", [F32], "ff2")
    dx2, df, acc3 = _loss_head(x1, f, target, vec3, "loss_head")

    def d_sqrelu(acc, h):
        return (acc * (2.0 * jnp.maximum(h.astype(F32), 0.0)),)

    dh = _mm(df, w_ff2_f, "nt", [BF16], "ff2_dx", extras=[(hff, 0)], epilogue=d_sqrelu)
    sent = {}
    sent["w_ff2"], token = _exchange_start(_mm(s, df, "tn", [BF16], "ff2_dw"), 0, False, "rs_w_ff2_start")
    da2 = _mm(dh, w_ff1_f, "nt", [F32], "ff1_dx", dep=token)
    sent["w_ff1"], token = _exchange_start(_mm(a2, dh, "tn", [BF16], "ff1_dw"), 1, False, "rs_w_ff1_start")
    dx1, dyl, acc2 = _modulate_bwd(da2, x1, vec2, "modulate2_bwd", dx_in=dx2, y=y_l)

    def d_merge(acc, yc, yr, gc, gr):
        sc, sr = _sig(gc), _sig(gr)
        return acc * sc, acc * sr, acc * yc * (sc * (1.0 - sc)), acc * yr * (sr * (1.0 - sr))

    dyc, dyr, dgc, dgr = _mm(dyl, w_o_f, "nt", [BF16] * 4, "proj_o_dx", tm=512, tn=tn_d,
                             extras=[(y_conv, 0), (y_ret, 0), (z, gate_offs[0]), (z, gate_offs[1])], epilogue=d_merge,
                             dep=token)
    sent["w_o"], token = _exchange_start(_mm(mg, dyl, "tn", [BF16], "proj_o_dw"), 0, False, "rs_w_o_start")
    dpc = _mm(dyc, w_conv_out_f, "nt", [F32], "proj_conv_out_dx", tn=2048, dep=token)
    sent["w_conv_out"], token = _exchange_start(_mm(pc, dyc, "tn", [BF16], "proj_conv_out_dw"), 0, False,
                                                "rs_w_conv_out_start")
    dz_conv, acc_conv = _conv_bwd(dpc, z, conv_w_f, "conv_bwd")
    dv_head = 2 * d // HEADS
    do, dz_g = _mm(dyr, w_ret_out_f, "nt", [BF16, BF16], "proj_ret_out_dx", tn=dv_head,
                   extras=[(o, 0), (z, CG * d // dv_head)], epilogue=_gn_bwd_head, dep=token)
    sent["w_ret_out"], token = _exchange_start(_mm(r, dyr, "tn", [BF16], "proj_ret_out_dw"), 0, False,
                                               "rs_w_ret_out_start")
    dq_f, dk_f, dv_f, dlg_f, dq_b, dk_b, dv_b, dlg_b = _ret_bwd(q_s, k_s, v_s, do, st_f, st_b, lgs, "ret_bwd")
    dz = _dz_assemble(dq_f, dq_b, dk_f, dk_b, dv_f, dv_b, cos, sin, dz_conv, dz_g, dgc, dgr, n_lat, n_ctx,
                      "dz_assemble")
    g_in = _mm(a_all, dz, "tn", [BF16], "proj_in_dw", tk=(n_lat + n_ctx) // 4, dep=token)
    sent["w_in"], token = _exchange_start(g_in, 1, False, "rs_w_in_start")
    da_all = _mm(dz, w_in_f, "nt", [F32], "proj_in_dx", tn=2048, dep=token)
    grad_x, acc1 = _modulate_bwd(da_all, xl, vec1_l, "modulate1_bwd", dx_in=dx1)
    _, acc1c = _modulate_bwd(da_all, ctxl, vec1_c, "modulate1_ctx_bwd", da_off=n_lat)

    lane_pad = functools.partial(_pad_lanes, width=d)
    packet = jnp.concatenate([
        acc1[2:3] + acc1c[2:3], acc2[2:3], acc3[1:2],
        acc1[0:1], acc1[1:2], acc2[3:4], acc2[0:1], acc2[1:2], acc3[0:1],
        acc1c[0:1], acc1c[1:2],
        acc_conv[0:3],
        lane_pad(dlg_f[:, 0, 0][None]), lane_pad(dlg_b[:, 0, 0][None]),
        acc3[2:3],
        jnp.zeros((7, d), F32)], axis=0)
    n_rows = packet.shape[0]
    packets = lax.dynamic_update_slice(jnp.zeros((N_DEV * n_rows, d), F32), packet, (me * n_rows, 0))
    packets_sent, after = _exchange_start(packets, 0, True, "ag_small_start")
    res = {}

    def update(wname, axis, w, m, v, after):
        g_full, parts = _exchange_wait(sent[wname], after, "rs_" + wname + "_wait")
        res[wname] = _adamw(parts, w[0], m[0], v[0], "adamw_" + wname, own=(g_full, axis, me))
        return res[wname][0]

    for wname, axis, w, m, v in (("w_ff2", 0, w_ff2, m_w_ff2, v_w_ff2), ("w_ff1", 1, w_ff1, m_w_ff1, v_w_ff1),
                                 ("w_o", 0, w_o, m_w_o, v_w_o), ("w_conv_out", 0, w_conv_out, m_w_conv_out, v_w_conv_out),
                                 ("w_ret_out", 0, w_ret_out, m_w_ret_out, v_w_ret_out)):
        after = update(wname, axis, w, m, v, after)
    gathered = _exchange_wait(packets_sent, after, "ag_small_wait").reshape(N_DEV, n_rows, d)
    red, misc = _reduce_small(gathered, dec, d, "reduce_small")
    dmod_ctx = jnp.concatenate([red[9], red[10], jnp.zeros((4 * d,), F32)])
    dmod_all = jnp.concatenate([gathered[:, 3:9, :].reshape(N_DEV, N_MOD * d), dmod_ctx[None]], axis=0)
    dm = jnp.pad(lax.dynamic_slice(dmod_all, (0, me * nb), (N_DEV + 1, nb)), ((0, 7), (0, 0)))
    g_mod, c_part = _mod_grad(cs, dm, w_mod[0], "mod_grad")
    g_c_ctx = _c_ctx_grad(_small_allgather(c_part, "ag_c_ctx"), c_ctx[None], "c_ctx_grad")
    g_b_mod = red[3:9].reshape(1, N_MOD * d) + dmod_ctx[None]
    g_conv_w = lax.dynamic_slice(red[11:14], (0, me * (d // N_DEV)), (3, d // N_DEV))

    res["w_mod"] = _adamw(g_mod[None], w_mod[0], m_w_mod[0], v_w_mod[0], "adamw_w_mod")
    update("w_in", 1, w_in, m_w_in, v_w_in, res["w_mod"][0])
    res = {k: tuple(t[None] for t in val) for k, val in res.items()}

    small = [("c_ctx", g_c_ctx, c_ctx, m_c_ctx, v_c_ctx), ("b_mod", g_b_mod, b_mod, m_b_mod, v_b_mod),
             ("norm1_g", red[0:1], norm1_g, m_norm1_g, v_norm1_g), ("conv_w", g_conv_w, conv_w, m_conv_w, v_conv_w),
             ("ret_decay_fwd", misc[1:2, :HEADS], ret_decay_fwd, m_ret_decay_fwd, v_ret_decay_fwd),
             ("ret_decay_bwd", misc[2:3, :HEADS], ret_decay_bwd, m_ret_decay_bwd, v_ret_decay_bwd),
             ("norm2_g", red[1:2], norm2_g, m_norm2_g, v_norm2_g), ("final_g", red[2:3], final_g, m_final_g, v_final_g)]

    def flat(t):
        t = t.reshape(-1)
        return jnp.pad(t, (0, (-t.shape[0]) % 1024))

    packed = [jnp.concatenate([flat(item[j]) for item in small]).reshape(-1, 128) for j in range(1, 5)]
    outs = _adamw(packed[0][None], packed[1], packed[2], packed[3], "adamw_small")
    start = 0
    for name, _, w, _, _ in small:
        size = w.size
        res[name] = tuple(o.reshape(-1)[start:start + size].reshape(w.shape) for o in outs)
        start += size + (-size) % 1024

    order = ["c_ctx", "w_mod", "b_mod", "norm1_g", "w_in", "conv_w", "w_conv_out", "ret_decay_fwd", "ret_decay_bwd",
             "w_ret_out", "w_o", "norm2_g", "w_ff1", "w_ff2", "final_g"]
    loss = misc[0, 0]
    return (loss, grad_x[None], *[res[n][0] for n in order], *[res[n][1] for n in order],
            *[res[n][2] for n in order], *[res[n][3] for n in order])
```

```python
import functools

import jax
import jax.numpy as jnp
from jax import lax
from jax.experimental import pallas as pl
from jax.experimental.pallas import tpu as pltpu

F32 = jnp.float32
BF16 = jnp.bfloat16
MESH = pl.DeviceIdType.MESH

N_DEV = 8
HEADS = 8
N_MOD = 6
N_IN = 11
GRID_W = 64
CHUNK = 256
ROPE_BASE = 10000.0
EPS = 1e-6
ADAM_LR, ADAM_B1, ADAM_B2, ADAM_EPS, ADAM_WD, ADAM_STEP = 0.001, 0.9, 0.999, 1e-08, 0.01, 10
VMEM_LIMIT = 56 * 1024 * 1024
HIGHEST = lax.Precision.HIGHEST
CB, CC, CX, CQ, CK, CV, CG, CGC, CGR = 0, 1, 2, 3, 4, 5, 7, 9, 10


def _tile(n, target, mult):
    t = (min(target, n) // mult) * mult
    while t >= mult:
        if n % t == 0:
            return t
        t -= mult
    return n


def _params(sem=None):
    return pltpu.CompilerParams(dimension_semantics=sem, vmem_limit_bytes=VMEM_LIMIT)


def _sig(v):
    return 1.0 / (1.0 + jnp.exp(-v))


def _coords():
    return lax.axis_index("x"), lax.axis_index("y"), lax.axis_index("c")


def _flip(p, m):
    return tuple(1 - v if (m >> s) & 1 else v for v, s in zip(p, (2, 1, 0)))


def _index(p):
    return 4 * p[0] + 2 * p[1] + p[2]


def _small_allgather(x, name):
    r, n = x.shape

    def body(x_ref, out_ref, send_sems, recv_sems):
        me = _coords()
        out_ref[pl.ds(_index(me), 1)] = x_ref[...][None]
        sent = []
        for m in range(1, N_DEV):
            cp = pltpu.make_async_remote_copy(
                src_ref=x_ref, dst_ref=out_ref.at[_index(me)], send_sem=send_sems.at[m - 1],
                recv_sem=recv_sems.at[m - 1], device_id=_flip(me, m), device_id_type=MESH)
            cp.start()
            sent.append(cp)
        for m in range(1, N_DEV):
            pltpu.make_async_remote_copy(
                src_ref=x_ref, dst_ref=out_ref.at[_index(_flip(me, m))], send_sem=send_sems.at[m - 1],
                recv_sem=recv_sems.at[m - 1], device_id=_flip(me, m), device_id_type=MESH).wait_recv()
        for cp in sent:
            cp.wait_send()

    return pl.pallas_call(
        body, name=name, out_shape=jax.ShapeDtypeStruct((N_DEV, r, n), x.dtype),
        in_specs=[pl.BlockSpec(memory_space=pltpu.VMEM)], out_specs=pl.BlockSpec(memory_space=pltpu.VMEM),
        scratch_shapes=[pltpu.SemaphoreType.DMA((N_DEV - 1,)), pltpu.SemaphoreType.DMA((N_DEV - 1,))],
    )(x)


def _window(ref, j, r, c, axis):
    if axis == 0:
        return ref.at[pl.ds(pl.multiple_of(j * r, 8), r), :]
    return ref.at[:, pl.ds(pl.multiple_of(j * c, 128), c)]


_HBM = pl.BlockSpec(memory_space=pltpu.HBM)
_SEM = pl.BlockSpec(memory_space=pltpu.SEMAPHORE)
_ANY = pl.BlockSpec(memory_space=pl.ANY)
_EFFECT = pltpu.SideEffectType.DATAFLOW_SIDE_EFFECTING


def _place_shard(w, axis, me, name):
    r, c = w.shape
    full = (N_DEV * r, c) if axis == 0 else (r, N_DEV * c)
    tr = _tile(r, max(16, (2 * 1024 * 1024) // (c * 4)), 16)
    blocks = r // tr

    def out_map(i, me_ref):
        return (me_ref[0] * blocks + i, 0) if axis == 0 else (i, me_ref[0])

    def body(me_ref, w_ref, out_ref):
        out_ref[...] = w_ref[...].astype(BF16)

    return pl.pallas_call(
        body, name=name, out_shape=jax.ShapeDtypeStruct(full, BF16),
        grid_spec=pltpu.PrefetchScalarGridSpec(
            num_scalar_prefetch=1, grid=(blocks,), in_specs=[pl.BlockSpec((tr, c), lambda i, me_ref: (i, 0))],
            out_specs=pl.BlockSpec((tr, c), out_map)),
        compiler_params=_params(("parallel",)))(me.reshape(1), w)


def _exchange_start(src, axis, gather, name, after=()):
    r, c = (src.shape[0] // N_DEV, src.shape[1]) if axis == 0 else (src.shape[0], src.shape[1] // N_DEV)
    n_hbm = 1 if gather else 2
    n_after = len(after)

    def body(*refs):
        src_ref, land_ref = refs[0], refs[n_hbm - 1]
        send_sems, recv_sems = refs[n_hbm + n_after:n_hbm + n_after + 2]
        token = refs[-1]
        me = _coords()
        for m in range(1, N_DEV):
            peer = _flip(me, m)
            if gather:
                mine = theirs = _window(land_ref, _index(me), r, c, axis)
            else:
                mine, theirs = _window(src_ref, _index(peer), r, c, axis), land_ref.at[m - 1]
            pltpu.make_async_remote_copy(
                src_ref=mine, dst_ref=theirs, send_sem=send_sems.at[m - 1], recv_sem=recv_sems.at[m - 1],
                device_id=peer, device_id_type=MESH).start()
        token[...] = jnp.zeros_like(token)

    hbm = [pltpu.with_memory_space_constraint(src, pltpu.HBM)]
    if not gather:
        hbm.append(pltpu.with_memory_space_constraint(lax.empty((N_DEV - 1, r, c), src.dtype), pltpu.HBM))
    outs = pl.pallas_call(
        body, name=name,
        out_shape=(pltpu.SemaphoreType.DMA((N_DEV - 1,)), pltpu.SemaphoreType.DMA((N_DEV - 1,)),
                   *[pltpu.HBM(t.shape, t.dtype) for t in hbm], jax.ShapeDtypeStruct((8, 128), F32)),
        in_specs=[_HBM] * n_hbm + [_ANY] * n_after,
        out_specs=(_SEM, _SEM, *[_HBM] * n_hbm, pl.BlockSpec(memory_space=pltpu.VMEM)),
        input_output_aliases={i: 2 + i for i in range(n_hbm)},
        compiler_params=pltpu.CompilerParams(has_side_effects=_EFFECT),
    )(*hbm, *after)
    return (outs[:2], outs[2:2 + n_hbm], (axis, gather, r, c)), outs[-1]


def _exchange_wait(handle, after, name):
    (send_sems, recv_sems), hbm, (axis, gather, r, c) = handle
    n_hbm = len(hbm)

    def body(*refs):
        src_ref, land_ref = refs[0], refs[n_hbm - 1]
        send_sems, recv_sems = refs[n_hbm:n_hbm + 2]
        me = _coords()
        for m in range(1, N_DEV):
            peer = _flip(me, m)
            if gather:
                mine, theirs = _window(land_ref, _index(me), r, c, axis), _window(land_ref, _index(peer), r, c, axis)
            else:
                mine, theirs = _window(src_ref, _index(peer), r, c, axis), land_ref.at[m - 1]
            copy = pltpu.make_async_remote_copy(
                src_ref=mine, dst_ref=theirs, send_sem=send_sems.at[m - 1], recv_sem=recv_sems.at[m - 1],
                device_id=peer, device_id_type=MESH)
            copy.wait_send()
            copy.wait_recv()

    outs = pl.pallas_call(
        body, name=name, out_shape=tuple(pltpu.HBM(t.shape, t.dtype) for t in hbm),
        in_specs=[_HBM] * n_hbm + [_SEM, _SEM, _ANY], out_specs=tuple([_HBM] * n_hbm),
        input_output_aliases={i: i for i in range(n_hbm)},
        compiler_params=pltpu.CompilerParams(has_side_effects=_EFFECT),
    )(*hbm, send_sems, recv_sems, after)
    return outs[0] if gather else tuple(outs)


def _mm(a, b, mode, out_dtypes, name, tm=1024, tn=1024, tk=2048, extras=(), epilogue=None, dep=None):
    if mode == "nn":
        (m, k), n = a.shape, b.shape[1]
    elif mode == "nt":
        (m, k), n = a.shape, b.shape[0]
    else:
        (k, m), n = a.shape, b.shape[1]
    tm, tn = _tile(m, tm, 8), _tile(n, tn, min(128, tn))
    tk = _tile(k, tk, 16 if mode == "tn" else 128)
    nk = k // tk
    swap = nk == 1 and (k * n + (n // tn) * m * k) < (m * k + (m // tm) * k * n)

    def ij(p, q):
        return (q, p) if swap else (p, q)

    def spec(shape, fn):
        return pl.BlockSpec(shape, lambda p, q, kk: fn(*ij(p, q), kk))

    a_spec = spec((tk, tm), lambda i, j, kk: (kk, i)) if mode == "tn" else spec((tm, tk), lambda i, j, kk: (i, kk))
    b_spec = spec((tn, tk), lambda i, j, kk: (j, kk)) if mode == "nt" else spec((tk, tn), lambda i, j, kk: (kk, j))
    dims = {"nn": (((1,), (0,)), ((), ())), "nt": (((1,), (1,)), ((), ())), "tn": (((0,), (0,)), ((), ()))}[mode]
    ex_specs = [spec((tm, tn), functools.partial(lambda i, j, kk, off: (i, j + off), off=off)) for _, off in extras]
    deps = [] if dep is None else [dep]
    dep_specs = [pl.BlockSpec(dep.shape, lambda p, q, kk: (0, 0))] if deps else []
    n_ex, n_out = len(extras), len(out_dtypes)
    n_in = 2 + n_ex + len(deps)

    def body(*refs):
        a_ref, b_ref = refs[0], refs[1]
        ex_refs = refs[2:2 + n_ex]
        out_refs = refs[n_in:n_in + n_out]

        def product():
            return lax.dot_general(a_ref[...], b_ref[...], dims, preferred_element_type=F32)

        def finish(res):
            res = epilogue(res, *[e[...] for e in ex_refs]) if epilogue is not None else (res,)
            for o_ref, val in zip(out_refs, res):
                o_ref[...] = val.astype(o_ref.dtype)

        if nk == 1:
            finish(product())
            return
        acc = refs[-1]
        kk = pl.program_id(2)

        @pl.when(kk == 0)
        def _():
            acc[...] = product()

        @pl.when((kk > 0) & (kk < nk - 1))
        def _():
            acc[...] += product()

        @pl.when(kk == nk - 1)
        def _():
            finish(acc[...] + product())

    outs = pl.pallas_call(
        body, name=name, grid=(*ij(m // tm, n // tn), nk),
        in_specs=[a_spec, b_spec] + ex_specs + dep_specs,
        out_specs=[spec((tm, tn), lambda i, j, kk: (i, j)) for _ in out_dtypes],
        out_shape=[jax.ShapeDtypeStruct((m, n), dt) for dt in out_dtypes],
        scratch_shapes=[pltpu.VMEM((tm, tn), F32)] if nk > 1 else [],
        compiler_params=_params(("parallel", "parallel", "arbitrary")),
    )(a, b, *[e for e, _ in extras], *deps)
    return outs if n_out > 1 else outs[0]


def _gather_project(a, w_full, order, name, riders=(), tm=768):
    m, k = a.shape
    c = w_full.shape[1] // N_DEV
    tm = _tile(m, tm, 8)
    n_i = m // tm
    chips = [4, 2, 6]
    n_rid = len(riders)
    shard = [(k, c, 1)] + [((t.shape[0] // N_DEV, t.shape[1], 0) if ax == 0 else (t.shape[0], t.shape[1] // N_DEV, 1))
                           for t, ax in riders]

    def body(order_ref, a_ref, *refs):
        outs = refs[1 + n_rid:2 + 2 * n_rid]
        z_ref, wbuf, send_sems, recv_sems, load_sems = refs[2 + 2 * n_rid:]
        w_ref = outs[0]
        p, i = pl.program_id(0), pl.program_id(1)
        me = _coords()
        sibling = _flip(me, 1)

        def copy(t, s, block, to):
            r_t, c_t, ax_t = shard[t]
            win = _window(outs[t], _index(block), r_t, c_t, ax_t)
            return pltpu.make_async_remote_copy(src_ref=win, dst_ref=win, send_sem=send_sems.at[7 * t + s],
                                                recv_sem=recv_sems.at[7 * t + s], device_id=to, device_id_type=MESH)

        def first(t):
            return [copy(t, 0, me, sibling)] + [copy(t, 1 + j, me, _flip(me, mm)) for j, mm in enumerate(chips)]

        def passed(t):
            return [copy(t, 4 + j, _flip(me, mm), sibling) for j, mm in enumerate(chips)]

        @pl.when((p == 0) & (i == 0))
        def _():
            for t in range(1 + n_rid):
                for cp in first(t):
                    cp.start()

        def load(piece):
            return pltpu.make_async_copy(_window(w_ref, order_ref[piece], k, c, 1), wbuf.at[piece % 2],
                                         load_sems.at[piece % 2])

        @pl.when((p == 0) & (i == 0))
        def _():
            load(p).start()

        arrivals = [(0, None)] + [x for j in range(3) for x in ((1 + j, passed(0)[j]), (4 + j, None))]
        for piece, (sem, forward) in enumerate(arrivals, start=1):
            @pl.when((p == piece - 1) & (i == n_i - 1))
            def _(sem=sem, forward=forward, piece=piece):
                src_block = (sibling if piece == 1 else
                             _flip(me if piece % 2 == 0 else sibling, chips[(piece - 2) // 2]))
                copy(0, sem, src_block, me).wait_recv()
                if forward is not None:
                    forward.start()
                load(p + 1).start()

        @pl.when(i == 0)
        def _():
            load(p).wait()

        @pl.when((p == N_DEV - 1) & (i == 0))
        def _():
            for t in range(1, 1 + n_rid):
                for j, mm in enumerate(chips):
                    copy(t, 1 + j, _flip(me, mm), me).wait_recv()
                    passed(t)[j].start()

        z_ref[...] = jnp.dot(a_ref[...], wbuf[p % 2], preferred_element_type=F32)

        @pl.when((p == N_DEV - 1) & (i == n_i - 1))
        def _():
            for t in range(1, 1 + n_rid):
                copy(t, 0, sibling, me).wait_recv()
                for j, mm in enumerate(chips):
                    copy(t, 4 + j, _flip(sibling, mm), me).wait_recv()
            for t in range(1 + n_rid):
                for cp in first(t) + passed(t):
                    cp.wait_send()

    gathered = [w_full] + [t for t, _ in riders]
    n_sem = 7 * len(gathered)
    return pl.pallas_call(
        body, name=name,
        out_shape=[jax.ShapeDtypeStruct(t.shape, t.dtype) for t in gathered]
        + [jax.ShapeDtypeStruct((m, N_DEV * c), F32)],
        grid_spec=pltpu.PrefetchScalarGridSpec(
            num_scalar_prefetch=1, grid=(N_DEV, n_i),
            in_specs=[pl.BlockSpec((tm, k), lambda p, i, order_ref: (i, 0))]
            + [pl.BlockSpec(memory_space=pl.ANY)] * len(gathered),
            out_specs=[pl.BlockSpec(memory_space=pl.ANY)] * len(gathered)
            + [pl.BlockSpec((tm, c), lambda p, i, order_ref: (i, order_ref[p]))],
            scratch_shapes=[pltpu.VMEM((2, k, c), w_full.dtype), pltpu.SemaphoreType.DMA((n_sem,)),
                            pltpu.SemaphoreType.DMA((n_sem,)), pltpu.SemaphoreType.DMA((2,))]),
        input_output_aliases={2 + t: t for t in range(len(gathered))},
        compiler_params=_params(("arbitrary", "arbitrary")))(order, a, *gathered)


def _rspec(tr, w, cb=0, off=0):
    return pl.BlockSpec((tr, w), lambda i: (i + off, cb))


def _cspec(shape):
    return pl.BlockSpec(shape, lambda i: (0,) * len(shape))


def _rms(xf):
    rstd = lax.rsqrt(jnp.mean(xf * xf, axis=-1, keepdims=True) + EPS)
    return xf * rstd, rstd


def _rms_bwd(dn, n, rstd):
    return rstd * (dn - n * jnp.mean(dn * n, axis=-1, keepdims=True))


def _colsum(v):
    return jnp.sum(v, axis=0, keepdims=True)


def _total(v):
    return jnp.sum(jnp.sum(v, axis=1, keepdims=True), axis=0, keepdims=True)


def _modulate(x, vec, name, total_rows, into=None):
    rows, d = x.shape
    tr = _tile(rows, 256, 8)
    off = 0 if into is None else (total_rows - rows) // tr

    def body(x_ref, v_ref, *refs):
        n, _ = _rms(x_ref[...])
        refs[-1][...] = ((n * v_ref[0:1, :]) * (1.0 + v_ref[2:3, :]) + v_ref[1:2, :]).astype(BF16)

    return pl.pallas_call(
        body, name=name, grid=(rows // tr,),
        in_specs=[_rspec(tr, d), _cspec(vec.shape)] + ([] if into is None else [_ANY]),
        out_specs=_rspec(tr, d, off=off), out_shape=jax.ShapeDtypeStruct((total_rows, d), BF16),
        input_output_aliases={} if into is None else {2: 0},
        compiler_params=_params(("parallel",)))(x, vec, *([] if into is None else [into]))


def _resid_modulate(x, y, vec, name):
    rows, d = x.shape
    tr = _tile(rows, 256, 8)

    def body(x_ref, y_ref, v_ref, x1_ref, a_ref):
        x1 = x_ref[...] + v_ref[3:4, :] * y_ref[...]
        x1_ref[...] = x1
        n, _ = _rms(x1)
        a_ref[...] = ((n * v_ref[0:1, :]) * (1.0 + v_ref[2:3, :]) + v_ref[1:2, :]).astype(BF16)

    return pl.pallas_call(
        body, name=name, grid=(rows // tr,), in_specs=[_rspec(tr, d), _rspec(tr, d), _cspec(vec.shape)],
        out_specs=[_rspec(tr, d), _rspec(tr, d)],
        out_shape=[jax.ShapeDtypeStruct((rows, d), F32), jax.ShapeDtypeStruct((rows, d), BF16)],
        compiler_params=_params(("parallel",)))(x, y, vec)


def _loss_head(x1, f, target, vec, name):
    rows, d = x1.shape
    tr = _tile(rows, 256, 8)

    def body(x1_ref, f_ref, t_ref, v_ref, dx2_ref, df_ref, acc_ref):
        @pl.when(pl.program_id(0) == 0)
        def _():
            acc_ref[...] = jnp.zeros_like(acc_ref)

        gate, gain = v_ref[0:1, :], v_ref[1:2, :]
        fv = f_ref[...]
        n, rstd = _rms(x1_ref[...] + gate * fv)
        err = n * gain - t_ref[...]
        dy = err / d
        dx2 = _rms_bwd(dy * gain, n, rstd)
        dx2_ref[...] = dx2
        df_ref[...] = (dx2 * gate).astype(BF16)
        acc_ref[0:1, :] += _colsum(dx2 * fv)
        acc_ref[1:2, :] += _colsum(dy * n)
        acc_ref[2:3, :] += _colsum(err * err)

    return pl.pallas_call(
        body, name=name, grid=(rows // tr,),
        in_specs=[_rspec(tr, d), _rspec(tr, d), _rspec(tr, d), _cspec(vec.shape)],
        out_specs=[_rspec(tr, d), _rspec(tr, d), _cspec((8, d))],
        out_shape=[jax.ShapeDtypeStruct((rows, d), F32), jax.ShapeDtypeStruct((rows, d), BF16),
                   jax.ShapeDtypeStruct((8, d), F32)],
        compiler_params=_params(("arbitrary",)))(x1, f, target, vec)


def _modulate_bwd(da, x, vec, name, da_off=0, dx_in=None, y=None):
    rows, d = x.shape
    tr = _tile(rows, 256, 8)
    off = da_off // tr
    has_dx, has_y = dx_in is not None, y is not None

    def body(*refs):
        da_ref, x_ref, v_ref = refs[0], refs[1], refs[2]
        pos = 3
        dxin_ref = refs[pos] if has_dx else None
        pos += has_dx
        y_ref = refs[pos] if has_y else None
        pos += has_y
        dx_ref = refs[pos]
        dy_ref = refs[pos + 1] if has_y else None
        acc_ref = refs[-1]

        @pl.when(pl.program_id(0) == 0)
        def _():
            acc_ref[...] = jnp.zeros_like(acc_ref)

        gain, scale1 = v_ref[0:1, :], 1.0 + v_ref[2:3, :]
        dav = da_ref[...]
        n, rstd = _rms(x_ref[...])
        dx = _rms_bwd(dav * gain * scale1, n, rstd)
        if has_dx:
            dx = dx + dxin_ref[...]
        dx_ref[...] = dx
        acc_ref[0:1, :] += _colsum(dav)
        acc_ref[1:2, :] += _colsum(dav * (n * gain))
        acc_ref[2:3, :] += _colsum(dav * n * scale1)
        if has_y:
            acc_ref[3:4, :] += _colsum(dx * y_ref[...])
            dy_ref[...] = (dx * v_ref[3:4, :]).astype(BF16)

    ins = [da, x, vec] + ([dx_in] if has_dx else []) + ([y] if has_y else [])
    in_specs = [_rspec(tr, d, off=off), _rspec(tr, d), _cspec(vec.shape)] + [_rspec(tr, d)] * (has_dx + has_y)
    out_specs = [_rspec(tr, d)] + ([_rspec(tr, d)] if has_y else []) + [_cspec((8, d))]
    out_shape = ([jax.ShapeDtypeStruct((rows, d), F32)] + ([jax.ShapeDtypeStruct((rows, d), BF16)] if has_y else [])
                 + [jax.ShapeDtypeStruct((8, d), F32)])
    return pl.pallas_call(
        body, name=name, grid=(rows // tr,), in_specs=in_specs, out_specs=out_specs, out_shape=out_shape,
        compiler_params=_params(("arbitrary",)))(*ins)


def _conv_terms(cc, cx, w_ref, tr):
    t = lax.broadcasted_iota(jnp.int32, (tr, 1), 0) % GRID_W
    first, last = t == 0, t == GRID_W - 1
    u = cc * cx
    prev = jnp.where(first, 0.0, pltpu.roll(u, 1, 0))
    nxt = jnp.where(last, 0.0, pltpu.roll(u, tr - 1, 0))
    y = w_ref[0:1, :] * prev + w_ref[1:2, :] * u + w_ref[2:3, :] * nxt
    return u, prev, nxt, y, first, last


def _conv_fwd(z, conv_w, rows, name, dep):
    d = conv_w.shape[1]
    tr = _tile(rows, 256, GRID_W)

    def body(cb_ref, cc_ref, cx_ref, w_ref, dep_ref, out_ref):
        y = _conv_terms(cc_ref[...], cx_ref[...], w_ref, tr)[3]
        out_ref[...] = (cb_ref[...] * y).astype(BF16)

    return pl.pallas_call(
        body, name=name, grid=(rows // tr,),
        in_specs=[_rspec(tr, d, CB), _rspec(tr, d, CC), _rspec(tr, d, CX), _cspec(conv_w.shape), _cspec(dep.shape)],
        out_specs=_rspec(tr, d), out_shape=jax.ShapeDtypeStruct((rows, d), BF16),
        compiler_params=_params(("parallel",)))(z, z, z, conv_w, dep)


def _conv_bwd(dpc, z, conv_w, name):
    rows, d = dpc.shape
    tr = _tile(rows, 256, GRID_W)

    def body(dpc_ref, cb_ref, cc_ref, cx_ref, w_ref, dz_ref, acc_ref):
        @pl.when(pl.program_id(0) == 0)
        def _():
            acc_ref[...] = jnp.zeros_like(acc_ref)

        cc, cx, dpcv = cc_ref[...], cx_ref[...], dpc_ref[...]
        u, prev, nxt, y, first, last = _conv_terms(cc, cx, w_ref, tr)
        dy = dpcv * cb_ref[...]
        dy_next = jnp.where(last, 0.0, pltpu.roll(dy, tr - 1, 0))
        dy_prev = jnp.where(first, 0.0, pltpu.roll(dy, 1, 0))
        du = w_ref[0:1, :] * dy_next + w_ref[1:2, :] * dy + w_ref[2:3, :] * dy_prev
        dz_ref[:, 0:d] = (dpcv * y).astype(BF16)
        dz_ref[:, d:2 * d] = (du * cx).astype(BF16)
        dz_ref[:, 2 * d:3 * d] = (du * cc).astype(BF16)
        acc_ref[0:1, :] += _colsum(dy * prev)
        acc_ref[1:2, :] += _colsum(dy * u)
        acc_ref[2:3, :] += _colsum(dy * nxt)

    return pl.pallas_call(
        body, name=name, grid=(rows // tr,),
        in_specs=[_rspec(tr, d), _rspec(tr, d, CB), _rspec(tr, d, CC), _rspec(tr, d, CX), _cspec(conv_w.shape)],
        out_specs=[_rspec(tr, 3 * d), _cspec((8, d))],
        out_shape=[jax.ShapeDtypeStruct((rows, 3 * d), BF16), jax.ShapeDtypeStruct((8, d), F32)],
        compiler_params=_params(("arbitrary",)))(dpc, z, z, z, conv_w)


def _rotary_fwd(z, cos, sin, n_lat, name):
    rows = z.shape[0]
    d = z.shape[1] // N_IN
    dk = d // HEADS
    half = dk // 2
    tr = _tile(n_lat, 256, 8)
    tr = _tile(rows - n_lat, tr, 8)
    lat_blocks, all_blocks = n_lat // tr, rows // tr
    ctx_blocks = all_blocks - lat_blocks
    scan_rows = rows + ctx_blocks * tr

    def z_block(g):
        return jnp.where(g < all_blocks, g, g - ctx_blocks)

    def in_spec(w, cb=0):
        return pl.BlockSpec((tr, w), lambda g: (z_block(g), cb))

    def out_spec(w):
        return pl.BlockSpec((tr, w), lambda g: (jnp.where(g < all_blocks, g + ctx_blocks, g - all_blocks), 0))

    def body(q_ref, k_ref, v0_ref, v1_ref, cos_ref, sin_ref, qo_ref, ko_ref, vo_ref):
        cs, sn = cos_ref[...], sin_ref[...]
        keep = jnp.where(z_block(pl.program_id(0)) < lat_blocks, 1.0, 0.0)
        for src, dst, scale in ((q_ref, qo_ref, keep), (k_ref, ko_ref, dk ** -0.5)):
            for h in range(HEADS):
                lo, mid, hi = h * dk, h * dk + half, (h + 1) * dk
                t1, t2 = src[:, lo:mid], src[:, mid:hi]
                dst[:, lo:mid] = ((t1 * cs - t2 * sn) * scale).astype(BF16)
                dst[:, mid:hi] = ((t1 * sn + t2 * cs) * scale).astype(BF16)
        vo_ref[:, 0:d] = v0_ref[...].astype(BF16)
        vo_ref[:, d:2 * d] = v1_ref[...].astype(BF16)

    return pl.pallas_call(
        body, name=name, grid=(all_blocks + ctx_blocks,),
        in_specs=[in_spec(d, CQ), in_spec(d, CK), in_spec(d, CV), in_spec(d, CV + 1), in_spec(half), in_spec(half)],
        out_specs=[out_spec(d), out_spec(d), out_spec(2 * d)],
        out_shape=[jax.ShapeDtypeStruct((scan_rows, d), BF16), jax.ShapeDtypeStruct((scan_rows, d), BF16),
                   jax.ShapeDtypeStruct((scan_rows, 2 * d), BF16)],
        compiler_params=_params(("parallel",)))(z, z, z, z, cos, sin)


def _dz_assemble(dq_f, dq_b, dk_f, dk_b, dv_f, dv_b, cos, sin, dz_conv, dz_g, dz_gc, dz_gr, n_lat, n_ctx, name):
    rows = n_lat + n_ctx
    d = dq_f.shape[1]
    dk = d // HEADS
    half = dk // 2
    tr = _tile(n_ctx, 128, 8)
    lat_blocks, ctx_blocks = n_lat // tr, n_ctx // tr

    def fmap(i):
        return (jnp.where(i < lat_blocks, i + ctx_blocks, i - lat_blocks), 0)

    def bmap(i):
        return (i + ctx_blocks, 0)

    def lmap(i):
        return (jnp.minimum(i, lat_blocks - 1), 0)

    def body(qf_ref, qb_ref, kf_ref, kb_ref, vf_ref, vb_ref, cos_ref, sin_ref, c_ref, g_ref, gc_ref, gr_ref, out_ref):
        cs, sn = cos_ref[...], sin_ref[...]
        is_lat = pl.program_id(0) < lat_blocks
        keep = jnp.where(is_lat, 1.0, 0.0)
        for fa, fb, base, scale in ((qf_ref, qb_ref, CQ * d, keep), (kf_ref, kb_ref, CK * d, dk ** -0.5)):
            for h in range(HEADS):
                lo, mid, hi = h * dk, h * dk + half, (h + 1) * dk
                d1 = fa[:, lo:mid].astype(F32) + fb[:, lo:mid].astype(F32)
                d2 = fa[:, mid:hi].astype(F32) + fb[:, mid:hi].astype(F32)
                out_ref[:, base + lo:base + mid] = ((d1 * cs + d2 * sn) * scale).astype(BF16)
                out_ref[:, base + mid:base + hi] = ((d2 * cs - d1 * sn) * scale).astype(BF16)
        out_ref[:, CV * d:CG * d] = (vf_ref[...].astype(F32) + vb_ref[...].astype(F32)).astype(BF16)
        for src, lo, hi in ((c_ref, CB * d, CQ * d), (g_ref, CG * d, CGC * d), (gc_ref, CGC * d, CGR * d),
                            (gr_ref, CGR * d, N_IN * d)):
            out_ref[:, lo:hi] = jnp.where(is_lat, src[...], jnp.zeros_like(src))

    return pl.pallas_call(
        body, name=name, grid=(rows // tr,),
        in_specs=[pl.BlockSpec((tr, d), fmap), pl.BlockSpec((tr, d), bmap), pl.BlockSpec((tr, d), fmap),
                  pl.BlockSpec((tr, d), bmap), pl.BlockSpec((tr, 2 * d), fmap), pl.BlockSpec((tr, 2 * d), bmap),
                  _rspec(tr, half), _rspec(tr, half), pl.BlockSpec((tr, 3 * d), lmap), pl.BlockSpec((tr, 2 * d), lmap),
                  pl.BlockSpec((tr, d), lmap), pl.BlockSpec((tr, d), lmap)],
        out_specs=_rspec(tr, N_IN * d), out_shape=jax.ShapeDtypeStruct((rows, N_IN * d), BF16),
        compiler_params=_params(("parallel",)))(dq_f, dq_b, dk_f, dk_b, dv_f, dv_b, cos, sin, dz_conv, dz_g, dz_gc, dz_gr)


def _gn_fwd(o, z, name):
    rows = o.shape[0]
    d = z.shape[1] // N_IN
    dv = 2 * d // HEADS
    tr = _tile(rows, 128, 8)

    def body(o_ref, g0_ref, g1_ref, r_ref):
        for h in range(HEADS):
            lo, hi = h * dv, (h + 1) * dv
            g_ref, glo = (g0_ref, lo) if hi <= d else (g1_ref, lo - d)
            o = o_ref[:, lo:hi]
            cen = o - jnp.mean(o, axis=-1, keepdims=True)
            on = cen * lax.rsqrt(jnp.mean(cen * cen, axis=-1, keepdims=True) + EPS)
            g = g_ref[:, glo:glo + dv]
            r_ref[:, lo:hi] = (g * _sig(g) * on).astype(BF16)

    return pl.pallas_call(
        body, name=name, grid=(rows // tr,),
        in_specs=[_rspec(tr, 2 * d), _rspec(tr, d, CG), _rspec(tr, d, CG + 1)],
        out_specs=_rspec(tr, 2 * d), out_shape=jax.ShapeDtypeStruct((rows, 2 * d), BF16),
        compiler_params=_params(("parallel",)))(o, z, z)


def _gn_bwd_head(dr, o, g):
    cen = o - jnp.mean(o, axis=-1, keepdims=True)
    rstd = lax.rsqrt(jnp.mean(cen * cen, axis=-1, keepdims=True) + EPS)
    on = cen * rstd
    sg = _sig(g)
    don = dr * (g * sg)
    do = rstd * (don - jnp.mean(don, axis=-1, keepdims=True) - on * jnp.mean(don * on, axis=-1, keepdims=True))
    return do, dr * on * (sg * (1.0 + g * (1.0 - sg)))


def _decays(lg, rev):
    row = lax.broadcasted_iota(jnp.int32, (CHUNK, CHUNK), 0)
    col = lax.broadcasted_iota(jnp.int32, (CHUNK, CHUNK), 1)
    rel = ((col - row) if rev else (row - col)).astype(F32)
    mask = jnp.where(rel >= 0, jnp.exp(lg * jnp.maximum(rel, 0.0)), 0.0)
    r = lax.broadcasted_iota(jnp.int32, (CHUNK, 1), 0)
    rr = ((CHUNK - 1 - r) if rev else r).astype(F32)
    chunk_decay = jnp.exp(lg * jnp.full((1, 1), float(CHUNK), F32))
    return (rr, rel), mask, jnp.exp(lg * (rr + 1.0)), jnp.exp(lg * (CHUNK - 1.0 - rr)), chunk_decay


_NT = (((1,), (1,)), ((), ()))
_TN = (((0,), (0,)), ((), ()))


def _dot(a, b, dims=None):
    if dims is None:
        return jnp.dot(a, b, preferred_element_type=F32)
    return lax.dot_general(a, b, dims, preferred_element_type=F32)


def _ret_fwd(q, k, v, lgs, n_lat, name):
    tt, d = q.shape
    dk, dv = d // HEADS, 2 * d // HEADS
    nc = tt // CHUNK
    lat_chunks = n_lat // CHUNK
    ctx_chunks = (nc - lat_chunks) // 2

    def chunk_of(rev, i):
        return (nc - 1 - i) if rev else i

    def cmap(rev):
        return lambda h, i: (chunk_of(rev, i), h)

    def smap(rev):
        return lambda h, i: (chunk_of(rev, i), h, 0, 0)

    def body(lg_ref, qf_ref, kf_ref, vf_ref, qb_ref, kb_ref, vb_ref, o_ref, sf_ref, sb_ref, state_f, state_b):
        h, i = pl.program_id(0), pl.program_id(1)

        @pl.when(i == 0)
        def _():
            state_f[...] = jnp.zeros_like(state_f)
            state_b[...] = jnp.zeros_like(state_b)

        for rev, (q_ref, k_ref, v_ref, s_ref, state) in enumerate((
                (qf_ref, kf_ref, vf_ref, sf_ref, state_f), (qb_ref, kb_ref, vb_ref, sb_ref, state_b))):
            _, mask, qd, kd, cd = _decays(lg_ref[rev, h], bool(rev))
            qv, kv, vv = q_ref[...], k_ref[...], v_ref[...]
            st = state[...]
            p = _dot(qv, kv, _NT) * mask
            out = _dot(p.astype(BF16), vv) + _dot((qv * qd).astype(BF16), st.astype(BF16))
            s_ref[...] = st
            state[...] = cd * st + _dot((kv * kd).astype(BF16), vv, _TN)
            lat_chunk = chunk_of(bool(rev), i) - ctx_chunks
            is_lat = (lat_chunk >= 0) & (lat_chunk < lat_chunks)
            first = (2 * i < nc - 1) if rev else (2 * i <= nc - 1)

            @pl.when(is_lat & first)
            def _(lat_chunk=lat_chunk, out=out):
                o_ref[pl.ds(pl.multiple_of(lat_chunk * CHUNK, CHUNK), CHUNK), :] = out

            @pl.when(is_lat & jnp.logical_not(first))
            def _(lat_chunk=lat_chunk, out=out):
                o_ref[pl.ds(pl.multiple_of(lat_chunk * CHUNK, CHUNK), CHUNK), :] += out

    def specs(rev):
        return [pl.BlockSpec((CHUNK, dk), cmap(rev)), pl.BlockSpec((CHUNK, dk), cmap(rev)),
                pl.BlockSpec((CHUNK, dv), cmap(rev))]

    state_shape = jax.ShapeDtypeStruct((nc, HEADS, dk, dv), F32)
    return pl.pallas_call(
        body, name=name, grid=(HEADS, nc),
        in_specs=[pl.BlockSpec(memory_space=pltpu.SMEM)] + specs(False) + specs(True),
        out_specs=[pl.BlockSpec((n_lat, dv), lambda h, i: (0, h)), pl.BlockSpec((None, None, dk, dv), smap(False)),
                   pl.BlockSpec((None, None, dk, dv), smap(True))],
        out_shape=[jax.ShapeDtypeStruct((n_lat, 2 * d), F32), state_shape, state_shape],
        scratch_shapes=[pltpu.VMEM((dk, dv), F32), pltpu.VMEM((dk, dv), F32)],
        compiler_params=_params(("parallel", "arbitrary")))(lgs, q, k, v, q, k, v)


def _ret_bwd(q, k, v, do, states_f, states_b, lgs, name):
    tt, d = q.shape
    dk, dv = d // HEADS, 2 * d // HEADS
    nc = tt // CHUNK
    lat_chunks = do.shape[0] // CHUNK
    ctx_chunks = (nc - lat_chunks) // 2

    def chunk_of(rev, i):
        return i if rev else nc - 1 - i

    def cmap(rev):
        return lambda h, i: (chunk_of(rev, i), h)

    def do_map(rev):
        return lambda h, i: (jnp.clip(chunk_of(rev, i) - ctx_chunks, 0, lat_chunks - 1), h)

    def smap(rev):
        return lambda h, i: (chunk_of(rev, i), h, 0, 0)

    def body(lg_ref, *refs):
        h = pl.program_id(0)
        dstates = refs[-2:]

        @pl.when(pl.program_id(1) == 0)
        def _():
            for rev in (0, 1):
                dstates[rev][...] = jnp.zeros_like(dstates[rev])
                refs[10 + 4 * rev + 3][...] = jnp.zeros_like(refs[10 + 4 * rev + 3])

        for rev in (0, 1):
            q_ref, k_ref, v_ref, do_ref, s_ref = refs[5 * rev:5 * rev + 5]
            dq_ref, dk_ref, dv_ref, dlg_ref = refs[10 + 4 * rev:10 + 4 * rev + 4]
            dstate = dstates[rev]
            chunk = chunk_of(bool(rev), pl.program_id(1))
            is_lat = (chunk >= ctx_chunks) & (chunk < ctx_chunks + lat_chunks)
            (rr, rel), mask, qd, kd, cd = _decays(lg_ref[rev, h], bool(rev))
            qv, kv, vv = q_ref[...], k_ref[...], v_ref[...]
            dov = jnp.where(is_lat, do_ref[...], jnp.zeros_like(do_ref))
            st, dst = s_ref[...], dstate[...]
            st_b, dst_b = st.astype(BF16), dst.astype(BF16)
            p = _dot(qv, kv, _NT) * mask
            dp = _dot(dov, vv, _NT)
            da = (dp * mask).astype(BF16)
            dq_state = _dot(dov, st_b, _NT) * qd
            dk_state = _dot(vv, dst_b, _NT) * kd
            dq_ref[...] = (_dot(da, kv) + dq_state).astype(BF16)
            dk_ref[...] = (_dot(da, qv, _TN) + dk_state).astype(BF16)
            dv_ref[...] = (_dot(p.astype(BF16), dov, _TN) + _dot((kv * kd).astype(BF16), dst_b)).astype(BF16)
            dnew = cd * dst + _dot((qv * qd).astype(BF16), dov, _TN)
            dstate[...] = dnew
            through = rr * (jnp.sum(qv.astype(F32) * dq_state, axis=1, keepdims=True)
                            - jnp.sum(kv.astype(F32) * dk_state, axis=1, keepdims=True))
            dlg_ref[...] += _total(rel * p * dp) + _total(through) + CHUNK * _total(st * dnew)

    def specs(rev):
        return [pl.BlockSpec((CHUNK, dk), cmap(rev)), pl.BlockSpec((CHUNK, dk), cmap(rev)),
                pl.BlockSpec((CHUNK, dv), cmap(rev)), pl.BlockSpec((CHUNK, dv), do_map(rev)),
                pl.BlockSpec((None, None, dk, dv), smap(rev))]

    def outs(rev):
        return [pl.BlockSpec((CHUNK, dk), cmap(rev)), pl.BlockSpec((CHUNK, dk), cmap(rev)),
                pl.BlockSpec((CHUNK, dv), cmap(rev)), pl.BlockSpec((None, 8, 128), lambda h, i: (h, 0, 0))]

    return pl.pallas_call(
        body, name=name, grid=(HEADS, nc),
        in_specs=[pl.BlockSpec(memory_space=pltpu.SMEM)] + specs(False) + specs(True),
        out_specs=outs(False) + outs(True),
        out_shape=[jax.ShapeDtypeStruct((tt, d), BF16), jax.ShapeDtypeStruct((tt, d), BF16),
                   jax.ShapeDtypeStruct((tt, 2 * d), BF16), jax.ShapeDtypeStruct((HEADS, 8, 128), F32)] * 2,
        scratch_shapes=[pltpu.VMEM((dk, dv), F32), pltpu.VMEM((dk, dv), F32)],
        compiler_params=_params(("parallel", "arbitrary")))(lgs, q, k, v, do, states_f, q, k, v, do, states_b)


def _silu(v):
    return v * _sig(v)


def _mod_proj(cs, w_mod, b_loc, dec, name):
    nb = w_mod.shape[1]

    def body(cs_ref, w_ref, b_ref, dec_ref, out_ref, lg_ref):
        out_ref[...] = jnp.dot(_silu(cs_ref[...]), w_ref[...], preferred_element_type=F32, precision=HIGHEST) + b_ref[...]
        a = dec_ref[...]
        lg_ref[...] = jnp.minimum(a, 0.0) - jnp.log1p(jnp.exp(-jnp.abs(a)))

    return pl.pallas_call(
        body, name=name,
        out_shape=[jax.ShapeDtypeStruct((16, nb), F32), jax.ShapeDtypeStruct(dec.shape, F32)],
        compiler_params=_params())(cs, w_mod, b_loc, dec)


def _mod_grad(cs, dm, w_mod, name):
    d, nb = w_mod.shape

    def body(cs_ref, dm_ref, w_ref, gw_ref, part_ref):
        dmv = dm_ref[...]
        gw_ref[...] = lax.dot_general(_silu(cs_ref[...]), dmv, _TN, preferred_element_type=F32, precision=HIGHEST)
        part_ref[...] = lax.dot_general(dmv, w_ref[...], _NT, preferred_element_type=F32, precision=HIGHEST)

    return pl.pallas_call(
        body, name=name,
        out_shape=[jax.ShapeDtypeStruct((d, nb), F32), jax.ShapeDtypeStruct((16, d), F32)],
        compiler_params=_params())(cs, dm, w_mod)


def _reduce_small(gathered, dec, n_feat, name):
    _, rows, d = gathered.shape

    def body(g_ref, dec_ref, red_ref, misc_ref):
        total = g_ref[0]
        for i in range(1, N_DEV):
            total = total + g_ref[i]
        red_ref[...] = total
        misc_ref[...] = jnp.zeros_like(misc_ref)
        misc_ref[0:1, :] = jnp.zeros((1, 128), F32) + (0.5 / n_feat) * _total(total[16:17, :])
        misc_ref[1:3, :] = total[14:16, 0:128] * _sig(-dec_ref[0:2, :])

    return pl.pallas_call(
        body, name=name,
        out_shape=[jax.ShapeDtypeStruct((rows, d), F32), jax.ShapeDtypeStruct((8, 128), F32)],
        compiler_params=_params())(gathered, dec)


def _c_ctx_grad(parts, c_ctx, name):
    d = c_ctx.shape[1]

    def body(p_ref, c_ref, out_ref):
        total = p_ref[0]
        for i in range(1, N_DEV):
            total = total + p_ref[i]
        cv = c_ref[...]
        sg = _sig(cv)
        out_ref[...] = total[8:9, :] * (sg * (1.0 + cv * (1.0 - sg)))

    return pl.pallas_call(body, name=name, out_shape=jax.ShapeDtypeStruct((1, d), F32),
                          compiler_params=_params())(parts, c_ctx)


def _adamw(parts, w, m, v, name, own=None):
    n_parts, rows, cols = parts.shape
    row_bytes = cols * (parts.dtype.itemsize * (n_parts + 1) + 7 * 4)
    tr = _tile(rows, max(16, (8 * 1024 * 1024) // row_bytes), 16 if rows % 16 == 0 else 8)
    blocks = rows // tr

    def body(*refs):
        p_ref, w_ref, m_ref, v_ref, g_ref, d_ref, mo_ref, vo_ref = refs[-8:]
        if own is None:
            g = p_ref[0].astype(F32)
        else:
            g = refs[1][...].astype(F32) + p_ref[0].astype(F32)
        for i in range(1, n_parts):
            g = g + p_ref[i].astype(F32)
        m2 = ADAM_B1 * m_ref[...] + (1.0 - ADAM_B1) * g
        v2 = ADAM_B2 * v_ref[...] + (1.0 - ADAM_B2) * jnp.square(g)
        m_hat = m2 / (1.0 - ADAM_B1 ** ADAM_STEP)
        v_hat = v2 / (1.0 - ADAM_B2 ** ADAM_STEP)
        g_ref[...] = g
        d_ref[...] = -ADAM_LR * (m_hat / (jnp.sqrt(v_hat) + ADAM_EPS) + ADAM_WD * w_ref[...])
        mo_ref[...] = m2
        vo_ref[...] = v2

    out_shape = [jax.ShapeDtypeStruct((rows, cols), F32)] * 4
    if own is None:
        spec = _rspec(tr, cols)
        return pl.pallas_call(
            body, name=name, grid=(blocks,),
            in_specs=[pl.BlockSpec((n_parts, tr, cols), lambda i: (0, i, 0)), spec, spec, spec],
            out_specs=[spec] * 4, out_shape=out_shape, compiler_params=_params(("parallel",)))(parts, w, m, v)
    g_full, axis, me = own

    def own_map(i, me_ref):
        return (me_ref[0] * blocks + i, 0) if axis == 0 else (i, me_ref[0])

    spec = pl.BlockSpec((tr, cols), lambda i, me_ref: (i, 0))
    return pl.pallas_call(
        body, name=name, out_shape=out_shape,
        grid_spec=pltpu.PrefetchScalarGridSpec(
            num_scalar_prefetch=1, grid=(blocks,),
            in_specs=[pl.BlockSpec((tr, cols), own_map),
                      pl.BlockSpec((n_parts, tr, cols), lambda i, me_ref: (0, i, 0)), spec, spec, spec],
            out_specs=[spec] * 4),
        compiler_params=_params(("parallel",)))(me.reshape(1), g_full, parts, w, m, v)


def _rope_tables(pos, dk):
    half = dk // 2
    inv_freq = 1.0 / (ROPE_BASE ** jnp.linspace(0.0, 1.0, half, dtype=F32))
    ang = pos[:, None] * inv_freq[None, :]
    return jnp.cos(ang), jnp.sin(ang)


def _pad_lanes(v, width):
    return jnp.pad(v, ((0, 0), (0, width - v.shape[1])))


def kernel(x, c, ctx, c_ctx, w_mod, b_mod, norm1_g, w_in, conv_w, w_conv_out, ret_decay_fwd, ret_decay_bwd, w_ret_out, w_o, norm2_g, w_ff1, w_ff2, final_g, loss_target, m_c_ctx, m_w_mod, m_b_mod, m_norm1_g, m_w_in, m_conv_w, m_w_conv_out, m_ret_decay_fwd, m_ret_decay_bwd, m_w_ret_out, m_w_o, m_norm2_g, m_w_ff1, m_w_ff2, m_final_g, v_c_ctx, v_w_mod, v_b_mod, v_norm1_g, v_w_in, v_conv_w, v_w_conv_out, v_ret_decay_fwd, v_ret_decay_bwd, v_w_ret_out, v_w_o, v_norm2_g, v_w_ff1, v_w_ff2, v_final_g):
    n_lat, d = x.shape[1], x.shape[2]
    n_ctx = ctx.shape[1]
    assert n_ctx % CHUNK == 0 and n_lat % CHUNK == 0 and n_lat % GRID_W == 0
    dk = d // HEADS
    nb = w_mod.shape[2]
    me = 4 * lax.axis_index("x") + 2 * lax.axis_index("y") + lax.axis_index("c")
    xl, ctxl, target = x[0], ctx[0], loss_target[0]

    first = _small_allgather(jnp.concatenate([c, _pad_lanes(conv_w[0], d), jnp.zeros((4, d), F32)], axis=0), "ag_c")
    conv_w_f = first[:, 1:4, :d // N_DEV].transpose(1, 0, 2).reshape(3, d)

    c_all = first[:, 0, :]
    cs = jnp.concatenate([c_all, c_ctx[None], jnp.zeros((7, d), F32)], axis=0)
    dec = jnp.pad(jnp.concatenate([ret_decay_fwd, ret_decay_bwd], axis=0), ((0, 6), (0, 128 - HEADS)))
    b_loc = lax.dynamic_slice(b_mod, (0, me * nb), (1, nb))
    modp, lgs = _mod_proj(cs, w_mod[0], b_loc, dec, "mod_proj")
    modp_all = _small_allgather(modp, "ag_mod")
    mod_l = lax.dynamic_index_in_dim(modp_all, me, axis=1, keepdims=False).reshape(N_MOD, d)
    mod_c = modp_all[:, 8, :].reshape(N_MOD, d)
    lgs = lgs[0:2, :HEADS]
    zero_row = jnp.zeros((1, d), F32)
    vec1_l = jnp.concatenate([norm1_g, mod_l[0:1], mod_l[1:2], zero_row], axis=0)
    vec1_c = jnp.concatenate([norm1_g, mod_c[0:1], mod_c[1:2], zero_row], axis=0)
    vec2 = jnp.concatenate([norm2_g, mod_l[3:4], mod_l[4:5], mod_l[2:3]], axis=0)
    vec3 = jnp.concatenate([mod_l[5:6], final_g[None]], axis=0)

    a_all = _modulate(ctxl, vec1_c, "modulate1_ctx", n_lat + n_ctx,
                      into=_modulate(xl, vec1_l, "modulate1", n_lat + n_ctx))
    arrival = me ^ jnp.array([0, 1, 4, 5, 2, 3, 6, 7], jnp.int32)
    riders = [(_place_shard(w[0], 0, me, "place_" + wname), 0)
              for wname, w in (("w_conv_out", w_conv_out), ("w_ret_out", w_ret_out), ("w_o", w_o))]
    w_in_f, w_conv_out_f, w_ret_out_f, w_o_f, z = _gather_project(
        a_all, _place_shard(w_in[0], 1, me, "place_w_in"), arrival, "proj_in", riders=riders)

    later, after = {}, (z, conv_w_f)
    for wname, w, axis in (("w_ff1", w_ff1, 1), ("w_ff2", w_ff2, 0)):
        later[wname], token = _exchange_start(_place_shard(w[0], axis, me, "place_" + wname), axis, True,
                                              "ag_" + wname + "_start", after=after)
        after = (token,)
    pc = _conv_fwd(z, conv_w_f, n_lat, "conv_fwd", token)
    pos = jnp.concatenate([n_ctx + jnp.arange(n_lat, dtype=F32), jnp.arange(n_ctx, dtype=F32)])
    cos, sin = _rope_tables(pos, dk)
    q_s, k_s, v_s = _rotary_fwd(z, cos, sin, n_lat, "rotary_fwd")
    o, st_f, st_b = _ret_fwd(q_s, k_s, v_s, lgs, n_lat, "ret_fwd")
    r = _gn_fwd(o, z, "gn_fwd")
    y_conv = _mm(pc, w_conv_out_f, "nn", [F32], "proj_conv_out", tn=2048)
    tn_d = _tile(d, 1024, 128)
    gate_offs = (CGC * d // tn_d, CGR * d // tn_d)

    def merge(acc, yc, gc, gr):
        return acc, _sig(gc) * yc + _sig(gr) * acc

    y_ret, mg = _mm(r, w_ret_out_f, "nn", [F32, BF16], "proj_ret_out", tm=512, tn=tn_d, tk=2 * d,
                    extras=[(y_conv, 0), (z, gate_offs[0]), (z, gate_offs[1])], epilogue=merge)
    y_l = _mm(mg, w_o_f, "nn", [F32], "proj_o", tn=2048)
    x1, a2 = _resid_modulate(xl, y_l, vec2, "resid_modulate2")

    def sqrelu(acc):
        return acc, jnp.square(jnp.maximum(acc, 0.0))

    w_ff1_f = _exchange_wait(later["w_ff1"], a2, "ag_w_ff1_wait")
    hff, s = _mm(a2, w_ff1_f, "nn", [BF16, BF16], "ff1", epilogue=sqrelu)
    w_ff2_f = _exchange_wait(later["w_ff2"], s, "ag_w_ff2_wait")
    f = _mm(s, w_ff2_f, "nn", [F32], "ff2")
    dx2, df, acc3 = _loss_head(x1, f, target, vec3, "loss_head")

    def d_sqrelu(acc, h):
        return (acc * (2.0 * jnp.maximum(h.astype(F32), 0.0)),)

    dh = _mm(df, w_ff2_f, "nt", [BF16], "ff2_dx", extras=[(hff, 0)], epilogue=d_sqrelu)
    sent = {}
    sent["w_ff2"], token = _exchange_start(_mm(s, df, "tn", [BF16], "ff2_dw"), 0, False, "rs_w_ff2_start")
    da2 = _mm(dh, w_ff1_f, "nt", [F32], "ff1_dx", dep=token)
    sent["w_ff1"], token = _exchange_start(_mm(a2, dh, "tn", [BF16], "ff1_dw"), 1, False, "rs_w_ff1_start")
    dx1, dyl, acc2 = _modulate_bwd(da2, x1, vec2, "modulate2_bwd", dx_in=dx2, y=y_l)

    def d_merge(acc, yc, yr, gc, gr):
        sc, sr = _sig(gc), _sig(gr)
        return acc * sc, acc * sr, acc * yc * (sc * (1.0 - sc)), acc * yr * (sr * (1.0 - sr))

    dyc, dyr, dgc, dgr = _mm(dyl, w_o_f, "nt", [BF16] * 4, "proj_o_dx", tm=512, tn=tn_d,
                             extras=[(y_conv, 0), (y_ret, 0), (z, gate_offs[0]), (z, gate_offs[1])], epilogue=d_merge,
                             dep=token)
    sent["w_o"], token = _exchange_start(_mm(mg, dyl, "tn", [BF16], "proj_o_dw"), 0, False, "rs_w_o_start")
    dpc = _mm(dyc, w_conv_out_f, "nt", [F32], "proj_conv_out_dx", tn=2048, dep=token)
    sent["w_conv_out"], token = _exchange_start(_mm(pc, dyc, "tn", [BF16], "proj_conv_out_dw"), 0, False,
                                                "rs_w_conv_out_start")
    dz_conv, acc_conv = _conv_bwd(dpc, z, conv_w_f, "conv_bwd")
    dv_head = 2 * d // HEADS
    do, dz_g = _mm(dyr, w_ret_out_f, "nt", [BF16, BF16], "proj_ret_out_dx", tn=dv_head,
                   extras=[(o, 0), (z, CG * d // dv_head)], epilogue=_gn_bwd_head, dep=token)
    sent["w_ret_out"], token = _exchange_start(_mm(r, dyr, "tn", [BF16], "proj_ret_out_dw"), 0, False,
                                               "rs_w_ret_out_start")
    dq_f, dk_f, dv_f, dlg_f, dq_b, dk_b, dv_b, dlg_b = _ret_bwd(q_s, k_s, v_s, do, st_f, st_b, lgs, "ret_bwd")
    dz = _dz_assemble(dq_f, dq_b, dk_f, dk_b, dv_f, dv_b, cos, sin, dz_conv, dz_g, dgc, dgr, n_lat, n_ctx,
                      "dz_assemble")
    g_in = _mm(a_all, dz, "tn", [BF16], "proj_in_dw", tn=2048, tk=(n_lat + n_ctx) // 4, dep=token)
    sent["w_in"], token = _exchange_start(g_in, 1, False, "rs_w_in_start")
    da_all = _mm(dz, w_in_f, "nt", [F32], "proj_in_dx", tn=2048, dep=token)
    grad_x, acc1 = _modulate_bwd(da_all, xl, vec1_l, "modulate1_bwd", dx_in=dx1)
    _, acc1c = _modulate_bwd(da_all, ctxl, vec1_c, "modulate1_ctx_bwd", da_off=n_lat)

    lane_pad = functools.partial(_pad_lanes, width=d)
    packet = jnp.concatenate([
        acc1[2:3] + acc1c[2:3], acc2[2:3], acc3[1:2],
        acc1[0:1], acc1[1:2], acc2[3:4], acc2[0:1], acc2[1:2], acc3[0:1],
        acc1c[0:1], acc1c[1:2],
        acc_conv[0:3],
        lane_pad(dlg_f[:, 0, 0][None]), lane_pad(dlg_b[:, 0, 0][None]),
        acc3[2:3],
        jnp.zeros((7, d), F32)], axis=0)
    n_rows = packet.shape[0]
    packets = lax.dynamic_update_slice(jnp.zeros((N_DEV * n_rows, d), F32), packet, (me * n_rows, 0))
    packets_sent, after = _exchange_start(packets, 0, True, "ag_small_start")
    res = {}

    def update(wname, axis, w, m, v, after):
        g_full, parts = _exchange_wait(sent[wname], after, "rs_" + wname + "_wait")
        res[wname] = _adamw(parts, w[0], m[0], v[0], "adamw_" + wname, own=(g_full, axis, me))
        return res[wname][0]

    for wname, axis, w, m, v in (("w_ff2", 0, w_ff2, m_w_ff2, v_w_ff2), ("w_ff1", 1, w_ff1, m_w_ff1, v_w_ff1),
                                 ("w_o", 0, w_o, m_w_o, v_w_o), ("w_conv_out", 0, w_conv_out, m_w_conv_out, v_w_conv_out),
                                 ("w_ret_out", 0, w_ret_out, m_w_ret_out, v_w_ret_out)):
        after = update(wname, axis, w, m, v, after)
    gathered = _exchange_wait(packets_sent, after, "ag_small_wait").reshape(N_DEV, n_rows, d)
    red, misc = _reduce_small(gathered, dec, d, "reduce_small")
    dmod_ctx = jnp.concatenate([red[9], red[10], jnp.zeros((4 * d,), F32)])
    dmod_all = jnp.concatenate([gathered[:, 3:9, :].reshape(N_DEV, N_MOD * d), dmod_ctx[None]], axis=0)
    dm = jnp.pad(lax.dynamic_slice(dmod_all, (0, me * nb), (N_DEV + 1, nb)), ((0, 7), (0, 0)))
    g_mod, c_part = _mod_grad(cs, dm, w_mod[0], "mod_grad")
    g_c_ctx = _c_ctx_grad(_small_allgather(c_part, "ag_c_ctx"), c_ctx[None], "c_ctx_grad")
    g_b_mod = red[3:9].reshape(1, N_MOD * d) + dmod_ctx[None]
    g_conv_w = lax.dynamic_slice(red[11:14], (0, me * (d // N_DEV)), (3, d // N_DEV))

    res["w_mod"] = _adamw(g_mod[None], w_mod[0], m_w_mod[0], v_w_mod[0], "adamw_w_mod")
    update("w_in", 1, w_in, m_w_in, v_w_in, res["w_mod"][0])
    res = {k: tuple(t[None] for t in val) for k, val in res.items()}

    small = [("c_ctx", g_c_ctx, c_ctx, m_c_ctx, v_c_ctx), ("b_mod", g_b_mod, b_mod, m_b_mod, v_b_mod),
             ("norm1_g", red[0:1], norm1_g, m_norm1_g, v_norm1_g), ("conv_w", g_conv_w, conv_w, m_conv_w, v_conv_w),
             ("ret_decay_fwd", misc[1:2, :HEADS], ret_decay_fwd, m_ret_decay_fwd, v_ret_decay_fwd),
             ("ret_decay_bwd", misc[2:3, :HEADS], ret_decay_bwd, m_ret_decay_bwd, v_ret_decay_bwd),
             ("norm2_g", red[1:2], norm2_g, m_norm2_g, v_norm2_g), ("final_g", red[2:3], final_g, m_final_g, v_final_g)]

    def flat(t):
        t = t.reshape(-1)
        return jnp.pad(t, (0, (-t.shape[0]) % 1024))

    packed = [jnp.concatenate([flat(item[j]) for item in small]).reshape(-1, 128) for j in range(1, 5)]
    outs = _adamw(packed[0][None], packed[1], packed[2], packed[3], "adamw_small")
    start = 0
    for name, _, w, _, _ in small:
        size = w.size
        res[name] = tuple(o.reshape(-1)[start:start + size].reshape(w.shape) for o in outs)
        start += size + (-size) % 1024

    order = ["c_ctx", "w_mod", "b_mod", "norm1_g", "w_in", "conv_w", "w_conv_out", "ret_decay_fwd", "ret_decay_bwd",
             "w_ret_out", "w_o", "norm2_g", "w_ff1", "w_ff2", "final_g"]
    loss = misc[0, 0]
    return (loss, grad_x[None], *[res[n][0] for n in order], *[res[n][1] for n in order],
            *[res[n][2] for n in order], *[res[n][3] for n in order])
```

```python
import functools

import jax
import jax.numpy as jnp
from jax import lax
from jax.experimental import pallas as pl
from jax.experimental.pallas import tpu as pltpu

F32 = jnp.float32
BF16 = jnp.bfloat16
MESH = pl.DeviceIdType.MESH

N_DEV = 8
HEADS = 8
N_MOD = 6
N_IN = 11
GRID_W = 64
CHUNK = 256
ROPE_BASE = 10000.0
EPS = 1e-6
ADAM_LR, ADAM_B1, ADAM_B2, ADAM_EPS, ADAM_WD, ADAM_STEP = 0.001, 0.9, 0.999, 1e-08, 0.01, 10
VMEM_LIMIT = 56 * 1024 * 1024
HIGHEST = lax.Precision.HIGHEST
CB, CC, CX, CQ, CK, CV, CG, CGC, CGR = 0, 1, 2, 3, 4, 5, 7, 9, 10


def _tile(n, target, mult):
    t = (min(target, n) // mult) * mult
    while t >= mult:
        if n % t == 0:
            return t
        t -= mult
    return n


def _params(sem=None):
    return pltpu.CompilerParams(dimension_semantics=sem, vmem_limit_bytes=VMEM_LIMIT)


def _sig(v):
    return 1.0 / (1.0 + jnp.exp(-v))


def _coords():
    return lax.axis_index("x"), lax.axis_index("y"), lax.axis_index("c")


def _flip(p, m):
    return tuple(1 - v if (m >> s) & 1 else v for v, s in zip(p, (2, 1, 0)))


def _index(p):
    return 4 * p[0] + 2 * p[1] + p[2]


def _small_allgather(x, name):
    r, n = x.shape

    def body(x_ref, out_ref, send_sems, recv_sems):
        me = _coords()
        out_ref[pl.ds(_index(me), 1)] = x_ref[...][None]
        sent = []
        for m in range(1, N_DEV):
            cp = pltpu.make_async_remote_copy(
                src_ref=x_ref, dst_ref=out_ref.at[_index(me)], send_sem=send_sems.at[m - 1],
                recv_sem=recv_sems.at[m - 1], device_id=_flip(me, m), device_id_type=MESH)
            cp.start()
            sent.append(cp)
        for m in range(1, N_DEV):
            pltpu.make_async_remote_copy(
                src_ref=x_ref, dst_ref=out_ref.at[_index(_flip(me, m))], send_sem=send_sems.at[m - 1],
                recv_sem=recv_sems.at[m - 1], device_id=_flip(me, m), device_id_type=MESH).wait_recv()
        for cp in sent:
            cp.wait_send()

    return pl.pallas_call(
        body, name=name, out_shape=jax.ShapeDtypeStruct((N_DEV, r, n), x.dtype),
        in_specs=[pl.BlockSpec(memory_space=pltpu.VMEM)], out_specs=pl.BlockSpec(memory_space=pltpu.VMEM),
        scratch_shapes=[pltpu.SemaphoreType.DMA((N_DEV - 1,)), pltpu.SemaphoreType.DMA((N_DEV - 1,))],
    )(x)


def _window(ref, j, r, c, axis):
    if axis == 0:
        return ref.at[pl.ds(pl.multiple_of(j * r, 8), r), :]
    return ref.at[:, pl.ds(pl.multiple_of(j * c, 128), c)]


_HBM = pl.BlockSpec(memory_space=pltpu.HBM)
_SEM = pl.BlockSpec(memory_space=pltpu.SEMAPHORE)
_ANY = pl.BlockSpec(memory_space=pl.ANY)
_EFFECT = pltpu.SideEffectType.DATAFLOW_SIDE_EFFECTING


def _place_shard(w, axis, me, name):
    r, c = w.shape
    full = (N_DEV * r, c) if axis == 0 else (r, N_DEV * c)
    tr = _tile(r, max(16, (2 * 1024 * 1024) // (c * 4)), 16)
    blocks = r // tr

    def out_map(i, me_ref):
        return (me_ref[0] * blocks + i, 0) if axis == 0 else (i, me_ref[0])

    def body(me_ref, w_ref, out_ref):
        out_ref[...] = w_ref[...].astype(BF16)

    return pl.pallas_call(
        body, name=name, out_shape=jax.ShapeDtypeStruct(full, BF16),
        grid_spec=pltpu.PrefetchScalarGridSpec(
            num_scalar_prefetch=1, grid=(blocks,), in_specs=[pl.BlockSpec((tr, c), lambda i, me_ref: (i, 0))],
            out_specs=pl.BlockSpec((tr, c), out_map)),
        compiler_params=_params(("parallel",)))(me.reshape(1), w)


def _exchange_start(src, axis, gather, name, after=()):
    r, c = (src.shape[0] // N_DEV, src.shape[1]) if axis == 0 else (src.shape[0], src.shape[1] // N_DEV)
    n_hbm = 1 if gather else 2
    n_after = len(after)

    def body(*refs):
        src_ref, land_ref = refs[0], refs[n_hbm - 1]
        send_sems, recv_sems = refs[n_hbm + n_after:n_hbm + n_after + 2]
        token = refs[-1]
        me = _coords()
        for m in range(1, N_DEV):
            peer = _flip(me, m)
            if gather:
                mine = theirs = _window(land_ref, _index(me), r, c, axis)
            else:
                mine, theirs = _window(src_ref, _index(peer), r, c, axis), land_ref.at[m - 1]
            pltpu.make_async_remote_copy(
                src_ref=mine, dst_ref=theirs, send_sem=send_sems.at[m - 1], recv_sem=recv_sems.at[m - 1],
                device_id=peer, device_id_type=MESH).start()
        token[...] = jnp.zeros_like(token)

    hbm = [pltpu.with_memory_space_constraint(src, pltpu.HBM)]
    if not gather:
        hbm.append(pltpu.with_memory_space_constraint(lax.empty((N_DEV - 1, r, c), src.dtype), pltpu.HBM))
    outs = pl.pallas_call(
        body, name=name,
        out_shape=(pltpu.SemaphoreType.DMA((N_DEV - 1,)), pltpu.SemaphoreType.DMA((N_DEV - 1,)),
                   *[pltpu.HBM(t.shape, t.dtype) for t in hbm], jax.ShapeDtypeStruct((8, 128), F32)),
        in_specs=[_HBM] * n_hbm + [_ANY] * n_after,
        out_specs=(_SEM, _SEM, *[_HBM] * n_hbm, pl.BlockSpec(memory_space=pltpu.VMEM)),
        input_output_aliases={i: 2 + i for i in range(n_hbm)},
        compiler_params=pltpu.CompilerParams(has_side_effects=_EFFECT),
    )(*hbm, *after)
    return (outs[:2], outs[2:2 + n_hbm], (axis, gather, r, c)), outs[-1]


def _exchange_wait(handle, after, name):
    (send_sems, recv_sems), hbm, (axis, gather, r, c) = handle
    n_hbm = len(hbm)

    def body(*refs):
        src_ref, land_ref = refs[0], refs[n_hbm - 1]
        send_sems, recv_sems = refs[n_hbm:n_hbm + 2]
        me = _coords()
        for m in range(1, N_DEV):
            peer = _flip(me, m)
            if gather:
                mine, theirs = _window(land_ref, _index(me), r, c, axis), _window(land_ref, _index(peer), r, c, axis)
            else:
                mine, theirs = _window(src_ref, _index(peer), r, c, axis), land_ref.at[m - 1]
            copy = pltpu.make_async_remote_copy(
                src_ref=mine, dst_ref=theirs, send_sem=send_sems.at[m - 1], recv_sem=recv_sems.at[m - 1],
                device_id=peer, device_id_type=MESH)
            copy.wait_send()
            copy.wait_recv()

    outs = pl.pallas_call(
        body, name=name, out_shape=tuple(pltpu.HBM(t.shape, t.dtype) for t in hbm),
        in_specs=[_HBM] * n_hbm + [_SEM, _SEM, _ANY], out_specs=tuple([_HBM] * n_hbm),
        input_output_aliases={i: i for i in range(n_hbm)},
        compiler_params=pltpu.CompilerParams(has_side_effects=_EFFECT),
    )(*hbm, send_sems, recv_sems, after)
    return outs[0] if gather else tuple(outs)


def _mm(a, b, mode, out_dtypes, name, tm=1024, tn=1024, tk=2048, extras=(), epilogue=None, dep=None):
    if mode == "nn":
        (m, k), n = a.shape, b.shape[1]
    elif mode == "nt":
        (m, k), n = a.shape, b.shape[0]
    else:
        (k, m), n = a.shape, b.shape[1]
    tm, tn = _tile(m, tm, 8), _tile(n, tn, min(128, tn))
    tk = _tile(k, tk, 16 if mode == "tn" else 128)
    nk = k // tk
    swap = nk == 1 and (k * n + (n // tn) * m * k) < (m * k + (m // tm) * k * n)

    def ij(p, q):
        return (q, p) if swap else (p, q)

    def spec(shape, fn):
        return pl.BlockSpec(shape, lambda p, q, kk: fn(*ij(p, q), kk))

    a_spec = spec((tk, tm), lambda i, j, kk: (kk, i)) if mode == "tn" else spec((tm, tk), lambda i, j, kk: (i, kk))
    b_spec = spec((tn, tk), lambda i, j, kk: (j, kk)) if mode == "nt" else spec((tk, tn), lambda i, j, kk: (kk, j))
    dims = {"nn": (((1,), (0,)), ((), ())), "nt": (((1,), (1,)), ((), ())), "tn": (((0,), (0,)), ((), ()))}[mode]
    ex_specs = [spec((tm, tn), functools.partial(lambda i, j, kk, off: (i, j + off), off=off)) for _, off in extras]
    deps = [] if dep is None else [dep]
    dep_specs = [pl.BlockSpec(dep.shape, lambda p, q, kk: (0, 0))] if deps else []
    n_ex, n_out = len(extras), len(out_dtypes)
    n_in = 2 + n_ex + len(deps)

    def body(*refs):
        a_ref, b_ref = refs[0], refs[1]
        ex_refs = refs[2:2 + n_ex]
        out_refs = refs[n_in:n_in + n_out]

        def product():
            return lax.dot_general(a_ref[...], b_ref[...], dims, preferred_element_type=F32)

        def finish(res):
            res = epilogue(res, *[e[...] for e in ex_refs]) if epilogue is not None else (res,)
            for o_ref, val in zip(out_refs, res):
                o_ref[...] = val.astype(o_ref.dtype)

        if nk == 1:
            finish(product())
            return
        acc = refs[-1]
        kk = pl.program_id(2)

        @pl.when(kk == 0)
        def _():
            acc[...] = product()

        @pl.when((kk > 0) & (kk < nk - 1))
        def _():
            acc[...] += product()

        @pl.when(kk == nk - 1)
        def _():
            finish(acc[...] + product())

    outs = pl.pallas_call(
        body, name=name, grid=(*ij(m // tm, n // tn), nk),
        in_specs=[a_spec, b_spec] + ex_specs + dep_specs,
        out_specs=[spec((tm, tn), lambda i, j, kk: (i, j)) for _ in out_dtypes],
        out_shape=[jax.ShapeDtypeStruct((m, n), dt) for dt in out_dtypes],
        scratch_shapes=[pltpu.VMEM((tm, tn), F32)] if nk > 1 else [],
        compiler_params=_params(("parallel", "parallel", "arbitrary")),
    )(a, b, *[e for e, _ in extras], *deps)
    return outs if n_out > 1 else outs[0]


def _gather_project(a, w_full, order, name, riders=(), tm=768):
    m, k = a.shape
    c = w_full.shape[1] // N_DEV
    tm = _tile(m, tm, 8)
    n_i = m // tm
    chips = [4, 2, 6]
    n_rid = len(riders)
    shard = [(k, c, 1)] + [((t.shape[0] // N_DEV, t.shape[1], 0) if ax == 0 else (t.shape[0], t.shape[1] // N_DEV, 1))
                           for t, ax in riders]

    def body(order_ref, a_ref, *refs):
        outs = refs[1 + n_rid:2 + 2 * n_rid]
        z_ref, wbuf, send_sems, recv_sems, load_sems = refs[2 + 2 * n_rid:]
        w_ref = outs[0]
        p, i = pl.program_id(0), pl.program_id(1)
        me = _coords()
        sibling = _flip(me, 1)

        def copy(t, s, block, to):
            r_t, c_t, ax_t = shard[t]
            win = _window(outs[t], _index(block), r_t, c_t, ax_t)
            return pltpu.make_async_remote_copy(src_ref=win, dst_ref=win, send_sem=send_sems.at[7 * t + s],
                                                recv_sem=recv_sems.at[7 * t + s], device_id=to, device_id_type=MESH)

        def first(t):
            return [copy(t, 0, me, sibling)] + [copy(t, 1 + j, me, _flip(me, mm)) for j, mm in enumerate(chips)]

        def passed(t):
            return [copy(t, 4 + j, _flip(me, mm), sibling) for j, mm in enumerate(chips)]

        @pl.when((p == 0) & (i == 0))
        def _():
            for t in range(1 + n_rid):
                for cp in first(t):
                    cp.start()

        def load(piece):
            return pltpu.make_async_copy(_window(w_ref, order_ref[piece], k, c, 1), wbuf.at[piece % 2],
                                         load_sems.at[piece % 2])

        @pl.when((p == 0) & (i == 0))
        def _():
            load(p).start()

        arrivals = [(0, None)] + [x for j in range(3) for x in ((1 + j, passed(0)[j]), (4 + j, None))]
        for piece, (sem, forward) in enumerate(arrivals, start=1):
            @pl.when((p == piece - 1) & (i == n_i - 1))
            def _(sem=sem, forward=forward, piece=piece):
                src_block = (sibling if piece == 1 else
                             _flip(me if piece % 2 == 0 else sibling, chips[(piece - 2) // 2]))
                copy(0, sem, src_block, me).wait_recv()
                if forward is not None:
                    forward.start()
                load(p + 1).start()

        @pl.when(i == 0)
        def _():
            load(p).wait()

        @pl.when((p == N_DEV - 1) & (i == 0))
        def _():
            for t in range(1, 1 + n_rid):
                for j, mm in enumerate(chips):
                    copy(t, 1 + j, _flip(me, mm), me).wait_recv()
                    passed(t)[j].start()

        z_ref[...] = jnp.dot(a_ref[...], wbuf[p % 2], preferred_element_type=F32)

        @pl.when((p == N_DEV - 1) & (i == n_i - 1))
        def _():
            for t in range(1, 1 + n_rid):
                copy(t, 0, sibling, me).wait_recv()
                for j, mm in enumerate(chips):
                    copy(t, 4 + j, _flip(sibling, mm), me).wait_recv()
            for t in range(1 + n_rid):
                for cp in first(t) + passed(t):
                    cp.wait_send()

    gathered = [w_full] + [t for t, _ in riders]
    n_sem = 7 * len(gathered)
    return pl.pallas_call(
        body, name=name,
        out_shape=[jax.ShapeDtypeStruct(t.shape, t.dtype) for t in gathered]
        + [jax.ShapeDtypeStruct((m, N_DEV * c), F32)],
        grid_spec=pltpu.PrefetchScalarGridSpec(
            num_scalar_prefetch=1, grid=(N_DEV, n_i),
            in_specs=[pl.BlockSpec((tm, k), lambda p, i, order_ref: (i, 0))]
            + [pl.BlockSpec(memory_space=pl.ANY)] * len(gathered),
            out_specs=[pl.BlockSpec(memory_space=pl.ANY)] * len(gathered)
            + [pl.BlockSpec((tm, c), lambda p, i, order_ref: (i, order_ref[p]))],
            scratch_shapes=[pltpu.VMEM((2, k, c), w_full.dtype), pltpu.SemaphoreType.DMA((n_sem,)),
                            pltpu.SemaphoreType.DMA((n_sem,)), pltpu.SemaphoreType.DMA((2,))]),
        input_output_aliases={2 + t: t for t in range(len(gathered))},
        compiler_params=_params(("arbitrary", "arbitrary")))(order, a, *gathered)


def _rspec(tr, w, cb=0, off=0):
    return pl.BlockSpec((tr, w), lambda i: (i + off, cb))


def _cspec(shape):
    return pl.BlockSpec(shape, lambda i: (0,) * len(shape))


def _rms(xf):
    rstd = lax.rsqrt(jnp.mean(xf * xf, axis=-1, keepdims=True) + EPS)
    return xf * rstd, rstd


def _rms_bwd(dn, n, rstd):
    return rstd * (dn - n * jnp.mean(dn * n, axis=-1, keepdims=True))


def _colsum(v):
    return jnp.sum(v, axis=0, keepdims=True)


def _total(v):
    return jnp.sum(jnp.sum(v, axis=1, keepdims=True), axis=0, keepdims=True)


def _modulate(x, vec, name, total_rows, into=None):
    rows, d = x.shape
    tr = _tile(rows, 256, 8)
    off = 0 if into is None else (total_rows - rows) // tr

    def body(x_ref, v_ref, *refs):
        n, _ = _rms(x_ref[...])
        refs[-1][...] = ((n * v_ref[0:1, :]) * (1.0 + v_ref[2:3, :]) + v_ref[1:2, :]).astype(BF16)

    return pl.pallas_call(
        body, name=name, grid=(rows // tr,),
        in_specs=[_rspec(tr, d), _cspec(vec.shape)] + ([] if into is None else [_ANY]),
        out_specs=_rspec(tr, d, off=off), out_shape=jax.ShapeDtypeStruct((total_rows, d), BF16),
        input_output_aliases={} if into is None else {2: 0},
        compiler_params=_params(("parallel",)))(x, vec, *([] if into is None else [into]))


def _resid_modulate(x, y, vec, name):
    rows, d = x.shape
    tr = _tile(rows, 256, 8)

    def body(x_ref, y_ref, v_ref, x1_ref, a_ref):
        x1 = x_ref[...] + v_ref[3:4, :] * y_ref[...]
        x1_ref[...] = x1
        n, _ = _rms(x1)
        a_ref[...] = ((n * v_ref[0:1, :]) * (1.0 + v_ref[2:3, :]) + v_ref[1:2, :]).astype(BF16)

    return pl.pallas_call(
        body, name=name, grid=(rows // tr,), in_specs=[_rspec(tr, d), _rspec(tr, d), _cspec(vec.shape)],
        out_specs=[_rspec(tr, d), _rspec(tr, d)],
        out_shape=[jax.ShapeDtypeStruct((rows, d), F32), jax.ShapeDtypeStruct((rows, d), BF16)],
        compiler_params=_params(("parallel",)))(x, y, vec)


def _loss_head(x1, f, target, vec, name):
    rows, d = x1.shape
    tr = _tile(rows, 256, 8)

    def body(x1_ref, f_ref, t_ref, v_ref, dx2_ref, df_ref, acc_ref):
        @pl.when(pl.program_id(0) == 0)
        def _():
            acc_ref[...] = jnp.zeros_like(acc_ref)

        gate, gain = v_ref[0:1, :], v_ref[1:2, :]
        fv = f_ref[...]
        n, rstd = _rms(x1_ref[...] + gate * fv)
        err = n * gain - t_ref[...]
        dy = err / d
        dx2 = _rms_bwd(dy * gain, n, rstd)
        dx2_ref[...] = dx2
        df_ref[...] = (dx2 * gate).astype(BF16)
        acc_ref[0:1, :] += _colsum(dx2 * fv)
        acc_ref[1:2, :] += _colsum(dy * n)
        acc_ref[2:3, :] += _colsum(err * err)

    return pl.pallas_call(
        body, name=name, grid=(rows // tr,),
        in_specs=[_rspec(tr, d), _rspec(tr, d), _rspec(tr, d), _cspec(vec.shape)],
        out_specs=[_rspec(tr, d), _rspec(tr, d), _cspec((8, d))],
        out_shape=[jax.ShapeDtypeStruct((rows, d), F32), jax.ShapeDtypeStruct((rows, d), BF16),
                   jax.ShapeDtypeStruct((8, d), F32)],
        compiler_params=_params(("arbitrary",)))(x1, f, target, vec)


def _modulate_bwd(da, x, vec, name, da_off=0, dx_in=None, y=None):
    rows, d = x.shape
    tr = _tile(rows, 256, 8)
    off = da_off // tr
    has_dx, has_y = dx_in is not None, y is not None

    def body(*refs):
        da_ref, x_ref, v_ref = refs[0], refs[1], refs[2]
        pos = 3
        dxin_ref = refs[pos] if has_dx else None
        pos += has_dx
        y_ref = refs[pos] if has_y else None
        pos += has_y
        dx_ref = refs[pos]
        dy_ref = refs[pos + 1] if has_y else None
        acc_ref = refs[-1]

        @pl.when(pl.program_id(0) == 0)
        def _():
            acc_ref[...] = jnp.zeros_like(acc_ref)

        gain, scale1 = v_ref[0:1, :], 1.0 + v_ref[2:3, :]
        dav = da_ref[...]
        n, rstd = _rms(x_ref[...])
        dx = _rms_bwd(dav * gain * scale1, n, rstd)
        if has_dx:
            dx = dx + dxin_ref[...]
        dx_ref[...] = dx
        acc_ref[0:1, :] += _colsum(dav)
        acc_ref[1:2, :] += _colsum(dav * (n * gain))
        acc_ref[2:3, :] += _colsum(dav * n * scale1)
        if has_y:
            acc_ref[3:4, :] += _colsum(dx * y_ref[...])
            dy_ref[...] = (dx * v_ref[3:4, :]).astype(BF16)

    ins = [da, x, vec] + ([dx_in] if has_dx else []) + ([y] if has_y else [])
    in_specs = [_rspec(tr, d, off=off), _rspec(tr, d), _cspec(vec.shape)] + [_rspec(tr, d)] * (has_dx + has_y)
    out_specs = [_rspec(tr, d)] + ([_rspec(tr, d)] if has_y else []) + [_cspec((8, d))]
    out_shape = ([jax.ShapeDtypeStruct((rows, d), F32)] + ([jax.ShapeDtypeStruct((rows, d), BF16)] if has_y else [])
                 + [jax.ShapeDtypeStruct((8, d), F32)])
    return pl.pallas_call(
        body, name=name, grid=(rows // tr,), in_specs=in_specs, out_specs=out_specs, out_shape=out_shape,
        compiler_params=_params(("arbitrary",)))(*ins)


def _conv_terms(cc, cx, w_ref, tr):
    t = lax.broadcasted_iota(jnp.int32, (tr, 1), 0) % GRID_W
    first, last = t == 0, t == GRID_W - 1
    u = cc * cx
    prev = jnp.where(first, 0.0, pltpu.roll(u, 1, 0))
    nxt = jnp.where(last, 0.0, pltpu.roll(u, tr - 1, 0))
    y = w_ref[0:1, :] * prev + w_ref[1:2, :] * u + w_ref[2:3, :] * nxt
    return u, prev, nxt, y, first, last


def _conv_fwd(z, conv_w, rows, name, dep):
    d = conv_w.shape[1]
    tr = _tile(rows, 256, GRID_W)

    def body(cb_ref, cc_ref, cx_ref, w_ref, dep_ref, out_ref):
        y = _conv_terms(cc_ref[...], cx_ref[...], w_ref, tr)[3]
        out_ref[...] = (cb_ref[...] * y).astype(BF16)

    return pl.pallas_call(
        body, name=name, grid=(rows // tr,),
        in_specs=[_rspec(tr, d, CB), _rspec(tr, d, CC), _rspec(tr, d, CX), _cspec(conv_w.shape), _cspec(dep.shape)],
        out_specs=_rspec(tr, d), out_shape=jax.ShapeDtypeStruct((rows, d), BF16),
        compiler_params=_params(("parallel",)))(z, z, z, conv_w, dep)


def _conv_bwd(dpc, z, conv_w, name):
    rows, d = dpc.shape
    tr = _tile(rows, 256, GRID_W)

    def body(dpc_ref, cb_ref, cc_ref, cx_ref, w_ref, dz_ref, acc_ref):
        @pl.when(pl.program_id(0) == 0)
        def _():
            acc_ref[...] = jnp.zeros_like(acc_ref)

        cc, cx, dpcv = cc_ref[...], cx_ref[...], dpc_ref[...]
        u, prev, nxt, y, first, last = _conv_terms(cc, cx, w_ref, tr)
        dy = dpcv * cb_ref[...]
        dy_next = jnp.where(last, 0.0, pltpu.roll(dy, tr - 1, 0))
        dy_prev = jnp.where(first, 0.0, pltpu.roll(dy, 1, 0))
        du = w_ref[0:1, :] * dy_next + w_ref[1:2, :] * dy + w_ref[2:3, :] * dy_prev
        dz_ref[:, 0:d] = (dpcv * y).astype(BF16)
        dz_ref[:, d:2 * d] = (du * cx).astype(BF16)
        dz_ref[:, 2 * d:3 * d] = (du * cc).astype(BF16)
        acc_ref[0:1, :] += _colsum(dy * prev)
        acc_ref[1:2, :] += _colsum(dy * u)
        acc_ref[2:3, :] += _colsum(dy * nxt)

    return pl.pallas_call(
        body, name=name, grid=(rows // tr,),
        in_specs=[_rspec(tr, d), _rspec(tr, d, CB), _rspec(tr, d, CC), _rspec(tr, d, CX), _cspec(conv_w.shape)],
        out_specs=[_rspec(tr, 3 * d), _cspec((8, d))],
        out_shape=[jax.ShapeDtypeStruct((rows, 3 * d), BF16), jax.ShapeDtypeStruct((8, d), F32)],
        compiler_params=_params(("arbitrary",)))(dpc, z, z, z, conv_w)


def _rotary_fwd(z, cos, sin, n_lat, name):
    rows = z.shape[0]
    d = z.shape[1] // N_IN
    dk = d // HEADS
    half = dk // 2
    tr = _tile(n_lat, 256, 8)
    tr = _tile(rows - n_lat, tr, 8)
    lat_blocks, all_blocks = n_lat // tr, rows // tr
    ctx_blocks = all_blocks - lat_blocks
    scan_rows = rows + ctx_blocks * tr

    def z_block(g):
        return jnp.where(g < all_blocks, g, g - ctx_blocks)

    def in_spec(w, cb=0):
        return pl.BlockSpec((tr, w), lambda g: (z_block(g), cb))

    def out_spec(w):
        return pl.BlockSpec((tr, w), lambda g: (jnp.where(g < all_blocks, g + ctx_blocks, g - all_blocks), 0))

    def body(q_ref, k_ref, v0_ref, v1_ref, cos_ref, sin_ref, qo_ref, ko_ref, vo_ref):
        cs, sn = cos_ref[...], sin_ref[...]
        keep = jnp.where(z_block(pl.program_id(0)) < lat_blocks, 1.0, 0.0)
        for src, dst, scale in ((q_ref, qo_ref, keep), (k_ref, ko_ref, dk ** -0.5)):
            for h in range(HEADS):
                lo, mid, hi = h * dk, h * dk + half, (h + 1) * dk
                t1, t2 = src[:, lo:mid], src[:, mid:hi]
                dst[:, lo:mid] = ((t1 * cs - t2 * sn) * scale).astype(BF16)
                dst[:, mid:hi] = ((t1 * sn + t2 * cs) * scale).astype(BF16)
        vo_ref[:, 0:d] = v0_ref[...].astype(BF16)
        vo_ref[:, d:2 * d] = v1_ref[...].astype(BF16)

    return pl.pallas_call(
        body, name=name, grid=(all_blocks + ctx_blocks,),
        in_specs=[in_spec(d, CQ), in_spec(d, CK), in_spec(d, CV), in_spec(d, CV + 1), in_spec(half), in_spec(half)],
        out_specs=[out_spec(d), out_spec(d), out_spec(2 * d)],
        out_shape=[jax.ShapeDtypeStruct((scan_rows, d), BF16), jax.ShapeDtypeStruct((scan_rows, d), BF16),
                   jax.ShapeDtypeStruct((scan_rows, 2 * d), BF16)],
        compiler_params=_params(("parallel",)))(z, z, z, z, cos, sin)


def _dz_assemble(dq_f, dq_b, dk_f, dk_b, dv_f, dv_b, cos, sin, dz_conv, dz_g, dz_gc, dz_gr, n_lat, n_ctx, name):
    rows = n_lat + n_ctx
    d = dq_f.shape[1]
    dk = d // HEADS
    half = dk // 2
    tr = _tile(n_ctx, 128, 8)
    lat_blocks, ctx_blocks = n_lat // tr, n_ctx // tr

    def fmap(i):
        return (jnp.where(i < lat_blocks, i + ctx_blocks, i - lat_blocks), 0)

    def bmap(i):
        return (i + ctx_blocks, 0)

    def lmap(i):
        return (jnp.minimum(i, lat_blocks - 1), 0)

    def body(qf_ref, qb_ref, kf_ref, kb_ref, vf_ref, vb_ref, cos_ref, sin_ref, c_ref, g_ref, gc_ref, gr_ref, out_ref):
        cs, sn = cos_ref[...], sin_ref[...]
        is_lat = pl.program_id(0) < lat_blocks
        keep = jnp.where(is_lat, 1.0, 0.0)
        for fa, fb, base, scale in ((qf_ref, qb_ref, CQ * d, keep), (kf_ref, kb_ref, CK * d, dk ** -0.5)):
            for h in range(HEADS):
                lo, mid, hi = h * dk, h * dk + half, (h + 1) * dk
                d1 = fa[:, lo:mid].astype(F32) + fb[:, lo:mid].astype(F32)
                d2 = fa[:, mid:hi].astype(F32) + fb[:, mid:hi].astype(F32)
                out_ref[:, base + lo:base + mid] = ((d1 * cs + d2 * sn) * scale).astype(BF16)
                out_ref[:, base + mid:base + hi] = ((d2 * cs - d1 * sn) * scale).astype(BF16)
        out_ref[:, CV * d:CG * d] = (vf_ref[...].astype(F32) + vb_ref[...].astype(F32)).astype(BF16)
        for src, lo, hi in ((c_ref, CB * d, CQ * d), (g_ref, CG * d, CGC * d), (gc_ref, CGC * d, CGR * d),
                            (gr_ref, CGR * d, N_IN * d)):
            out_ref[:, lo:hi] = jnp.where(is_lat, src[...], jnp.zeros_like(src))

    return pl.pallas_call(
        body, name=name, grid=(rows // tr,),
        in_specs=[pl.BlockSpec((tr, d), fmap), pl.BlockSpec((tr, d), bmap), pl.BlockSpec((tr, d), fmap),
                  pl.BlockSpec((tr, d), bmap), pl.BlockSpec((tr, 2 * d), fmap), pl.BlockSpec((tr, 2 * d), bmap),
                  _rspec(tr, half), _rspec(tr, half), pl.BlockSpec((tr, 3 * d), lmap), pl.BlockSpec((tr, 2 * d), lmap),
                  pl.BlockSpec((tr, d), lmap), pl.BlockSpec((tr, d), lmap)],
        out_specs=_rspec(tr, N_IN * d), out_shape=jax.ShapeDtypeStruct((rows, N_IN * d), BF16),
        compiler_params=_params(("parallel",)))(dq_f, dq_b, dk_f, dk_b, dv_f, dv_b, cos, sin, dz_conv, dz_g, dz_gc, dz_gr)


def _gn_fwd(o, z, name):
    rows = o.shape[0]
    d = z.shape[1] // N_IN
    dv = 2 * d // HEADS
    tr = _tile(rows, 128, 8)

    def body(o_ref, g0_ref, g1_ref, r_ref):
        for h in range(HEADS):
            lo, hi = h * dv, (h + 1) * dv
            g_ref, glo = (g0_ref, lo) if hi <= d else (g1_ref, lo - d)
            o = o_ref[:, lo:hi]
            cen = o - jnp.mean(o, axis=-1, keepdims=True)
            on = cen * lax.rsqrt(jnp.mean(cen * cen, axis=-1, keepdims=True) + EPS)
            g = g_ref[:, glo:glo + dv]
            r_ref[:, lo:hi] = (g * _sig(g) * on).astype(BF16)

    return pl.pallas_call(
        body, name=name, grid=(rows // tr,),
        in_specs=[_rspec(tr, 2 * d), _rspec(tr, d, CG), _rspec(tr, d, CG + 1)],
        out_specs=_rspec(tr, 2 * d), out_shape=jax.ShapeDtypeStruct((rows, 2 * d), BF16),
        compiler_params=_params(("parallel",)))(o, z, z)


def _gn_bwd_head(dr, o, g):
    cen = o - jnp.mean(o, axis=-1, keepdims=True)
    rstd = lax.rsqrt(jnp.mean(cen * cen, axis=-1, keepdims=True) + EPS)
    on = cen * rstd
    sg = _sig(g)
    don = dr * (g * sg)
    do = rstd * (don - jnp.mean(don, axis=-1, keepdims=True) - on * jnp.mean(don * on, axis=-1, keepdims=True))
    return do, dr * on * (sg * (1.0 + g * (1.0 - sg)))


def _decays(lg, rev):
    row = lax.broadcasted_iota(jnp.int32, (CHUNK, CHUNK), 0)
    col = lax.broadcasted_iota(jnp.int32, (CHUNK, CHUNK), 1)
    rel = ((col - row) if rev else (row - col)).astype(F32)
    mask = jnp.where(rel >= 0, jnp.exp(lg * jnp.maximum(rel, 0.0)), 0.0)
    r = lax.broadcasted_iota(jnp.int32, (CHUNK, 1), 0)
    rr = ((CHUNK - 1 - r) if rev else r).astype(F32)
    chunk_decay = jnp.exp(lg * jnp.full((1, 1), float(CHUNK), F32))
    return (rr, rel), mask, jnp.exp(lg * (rr + 1.0)), jnp.exp(lg * (CHUNK - 1.0 - rr)), chunk_decay


_NT = (((1,), (1,)), ((), ()))
_TN = (((0,), (0,)), ((), ()))


def _dot(a, b, dims=None):
    if dims is None:
        return jnp.dot(a, b, preferred_element_type=F32)
    return lax.dot_general(a, b, dims, preferred_element_type=F32)


def _ret_fwd(q, k, v, lgs, n_lat, name):
    tt, d = q.shape
    dk, dv = d // HEADS, 2 * d // HEADS
    nc = tt // CHUNK
    lat_chunks = n_lat // CHUNK
    ctx_chunks = (nc - lat_chunks) // 2

    def chunk_of(rev, i):
        return (nc - 1 - i) if rev else i

    def cmap(rev):
        return lambda h, i: (chunk_of(rev, i), h)

    def smap(rev):
        return lambda h, i: (chunk_of(rev, i), h, 0, 0)

    def body(lg_ref, qf_ref, kf_ref, vf_ref, qb_ref, kb_ref, vb_ref, o_ref, sf_ref, sb_ref, state_f, state_b):
        h, i = pl.program_id(0), pl.program_id(1)

        @pl.when(i == 0)
        def _():
            state_f[...] = jnp.zeros_like(state_f)
            state_b[...] = jnp.zeros_like(state_b)

        for rev, (q_ref, k_ref, v_ref, s_ref, state) in enumerate((
                (qf_ref, kf_ref, vf_ref, sf_ref, state_f), (qb_ref, kb_ref, vb_ref, sb_ref, state_b))):
            _, mask, qd, kd, cd = _decays(lg_ref[rev, h], bool(rev))
            qv, kv, vv = q_ref[...], k_ref[...], v_ref[...]
            st = state[...]
            p = _dot(qv, kv, _NT) * mask
            out = _dot(p.astype(BF16), vv) + _dot((qv * qd).astype(BF16), st.astype(BF16))
            s_ref[...] = st
            state[...] = cd * st + _dot((kv * kd).astype(BF16), vv, _TN)
            lat_chunk = chunk_of(bool(rev), i) - ctx_chunks
            is_lat = (lat_chunk >= 0) & (lat_chunk < lat_chunks)
            first = (2 * i < nc - 1) if rev else (2 * i <= nc - 1)

            @pl.when(is_lat & first)
            def _(lat_chunk=lat_chunk, out=out):
                o_ref[pl.ds(pl.multiple_of(lat_chunk * CHUNK, CHUNK), CHUNK), :] = out

            @pl.when(is_lat & jnp.logical_not(first))
            def _(lat_chunk=lat_chunk, out=out):
                o_ref[pl.ds(pl.multiple_of(lat_chunk * CHUNK, CHUNK), CHUNK), :] += out

    def specs(rev):
        return [pl.BlockSpec((CHUNK, dk), cmap(rev)), pl.BlockSpec((CHUNK, dk), cmap(rev)),
                pl.BlockSpec((CHUNK, dv), cmap(rev))]

    state_shape = jax.ShapeDtypeStruct((nc, HEADS, dk, dv), F32)
    return pl.pallas_call(
        body, name=name, grid=(HEADS, nc),
        in_specs=[pl.BlockSpec(memory_space=pltpu.SMEM)] + specs(False) + specs(True),
        out_specs=[pl.BlockSpec((n_lat, dv), lambda h, i: (0, h)), pl.BlockSpec((None, None, dk, dv), smap(False)),
                   pl.BlockSpec((None, None, dk, dv), smap(True))],
        out_shape=[jax.ShapeDtypeStruct((n_lat, 2 * d), F32), state_shape, state_shape],
        scratch_shapes=[pltpu.VMEM((dk, dv), F32), pltpu.VMEM((dk, dv), F32)],
        compiler_params=_params(("parallel", "arbitrary")))(lgs, q, k, v, q, k, v)


def _ret_bwd(q, k, v, do, states_f, states_b, lgs, name):
    tt, d = q.shape
    dk, dv = d // HEADS, 2 * d // HEADS
    nc = tt // CHUNK
    lat_chunks = do.shape[0] // CHUNK
    ctx_chunks = (nc - lat_chunks) // 2

    def chunk_of(rev, i):
        return i if rev else nc - 1 - i

    def cmap(rev):
        return lambda h, i: (chunk_of(rev, i), h)

    def do_map(rev):
        return lambda h, i: (jnp.clip(chunk_of(rev, i) - ctx_chunks, 0, lat_chunks - 1), h)

    def smap(rev):
        return lambda h, i: (chunk_of(rev, i), h, 0, 0)

    def body(lg_ref, *refs):
        h = pl.program_id(0)
        dstates = refs[-2:]

        @pl.when(pl.program_id(1) == 0)
        def _():
            for rev in (0, 1):
                dstates[rev][...] = jnp.zeros_like(dstates[rev])
                refs[10 + 4 * rev + 3][...] = jnp.zeros_like(refs[10 + 4 * rev + 3])

        for rev in (0, 1):
            q_ref, k_ref, v_ref, do_ref, s_ref = refs[5 * rev:5 * rev + 5]
            dq_ref, dk_ref, dv_ref, dlg_ref = refs[10 + 4 * rev:10 + 4 * rev + 4]
            dstate = dstates[rev]
            chunk = chunk_of(bool(rev), pl.program_id(1))
            is_lat = (chunk >= ctx_chunks) & (chunk < ctx_chunks + lat_chunks)
            (rr, rel), mask, qd, kd, cd = _decays(lg_ref[rev, h], bool(rev))
            qv, kv, vv = q_ref[...], k_ref[...], v_ref[...]
            dov = jnp.where(is_lat, do_ref[...], jnp.zeros_like(do_ref))
            st, dst = s_ref[...], dstate[...]
            st_b, dst_b = st.astype(BF16), dst.astype(BF16)
            p = _dot(qv, kv, _NT) * mask
            dp = _dot(dov, vv, _NT)
            da = (dp * mask).astype(BF16)
            dq_state = _dot(dov, st_b, _NT) * qd
            dk_state = _dot(vv, dst_b, _NT) * kd
            dq_ref[...] = (_dot(da, kv) + dq_state).astype(BF16)
            dk_ref[...] = (_dot(da, qv, _TN) + dk_state).astype(BF16)
            dv_ref[...] = (_dot(p.astype(BF16), dov, _TN) + _dot((kv * kd).astype(BF16), dst_b)).astype(BF16)
            dnew = cd * dst + _dot((qv * qd).astype(BF16), dov, _TN)
            dstate[...] = dnew
            through = rr * (jnp.sum(qv.astype(F32) * dq_state, axis=1, keepdims=True)
                            - jnp.sum(kv.astype(F32) * dk_state, axis=1, keepdims=True))
            dlg_ref[...] += _total(rel * p * dp) + _total(through) + CHUNK * _total(st * dnew)

    def specs(rev):
        return [pl.BlockSpec((CHUNK, dk), cmap(rev)), pl.BlockSpec((CHUNK, dk), cmap(rev)),
                pl.BlockSpec((CHUNK, dv), cmap(rev)), pl.BlockSpec((CHUNK, dv), do_map(rev)),
                pl.BlockSpec((None, None, dk, dv), smap(rev))]

    def outs(rev):
        return [pl.BlockSpec((CHUNK, dk), cmap(rev)), pl.BlockSpec((CHUNK, dk), cmap(rev)),
                pl.BlockSpec((CHUNK, dv), cmap(rev)), pl.BlockSpec((None, 8, 128), lambda h, i: (h, 0, 0))]

    return pl.pallas_call(
        body, name=name, grid=(HEADS, nc),
        in_specs=[pl.BlockSpec(memory_space=pltpu.SMEM)] + specs(False) + specs(True),
        out_specs=outs(False) + outs(True),
        out_shape=[jax.ShapeDtypeStruct((tt, d), BF16), jax.ShapeDtypeStruct((tt, d), BF16),
                   jax.ShapeDtypeStruct((tt, 2 * d), BF16), jax.ShapeDtypeStruct((HEADS, 8, 128), F32)] * 2,
        scratch_shapes=[pltpu.VMEM((dk, dv), F32), pltpu.VMEM((dk, dv), F32)],
        compiler_params=_params(("parallel", "arbitrary")))(lgs, q, k, v, do, states_f, q, k, v, do, states_b)


def _silu(v):
    return v * _sig(v)


def _mod_proj(cs, w_mod, b_loc, dec, name):
    nb = w_mod.shape[1]

    def body(cs_ref, w_ref, b_ref, dec_ref, out_ref, lg_ref):
        out_ref[...] = jnp.dot(_silu(cs_ref[...]), w_ref[...], preferred_element_type=F32, precision=HIGHEST) + b_ref[...]
        a = dec_ref[...]
        lg_ref[...] = jnp.minimum(a, 0.0) - jnp.log1p(jnp.exp(-jnp.abs(a)))

    return pl.pallas_call(
        body, name=name,
        out_shape=[jax.ShapeDtypeStruct((16, nb), F32), jax.ShapeDtypeStruct(dec.shape, F32)],
        compiler_params=_params())(cs, w_mod, b_loc, dec)


def _mod_grad(cs, dm, w_mod, name):
    d, nb = w_mod.shape

    def body(cs_ref, dm_ref, w_ref, gw_ref, part_ref):
        dmv = dm_ref[...]
        gw_ref[...] = lax.dot_general(_silu(cs_ref[...]), dmv, _TN, preferred_element_type=F32, precision=HIGHEST)
        part_ref[...] = lax.dot_general(dmv, w_ref[...], _NT, preferred_element_type=F32, precision=HIGHEST)

    return pl.pallas_call(
        body, name=name,
        out_shape=[jax.ShapeDtypeStruct((d, nb), F32), jax.ShapeDtypeStruct((16, d), F32)],
        compiler_params=_params())(cs, dm, w_mod)


def _reduce_small(gathered, dec, n_feat, name):
    _, rows, d = gathered.shape

    def body(g_ref, dec_ref, red_ref, misc_ref):
        total = g_ref[0]
        for i in range(1, N_DEV):
            total = total + g_ref[i]
        red_ref[...] = total
        misc_ref[...] = jnp.zeros_like(misc_ref)
        misc_ref[0:1, :] = jnp.zeros((1, 128), F32) + (0.5 / n_feat) * _total(total[16:17, :])
        misc_ref[1:3, :] = total[14:16, 0:128] * _sig(-dec_ref[0:2, :])

    return pl.pallas_call(
        body, name=name,
        out_shape=[jax.ShapeDtypeStruct((rows, d), F32), jax.ShapeDtypeStruct((8, 128), F32)],
        compiler_params=_params())(gathered, dec)


def _c_ctx_grad(parts, c_ctx, name):
    d = c_ctx.shape[1]

    def body(p_ref, c_ref, out_ref):
        total = p_ref[0]
        for i in range(1, N_DEV):
            total = total + p_ref[i]
        cv = c_ref[...]
        sg = _sig(cv)
        out_ref[...] = total[8:9, :] * (sg * (1.0 + cv * (1.0 - sg)))

    return pl.pallas_call(body, name=name, out_shape=jax.ShapeDtypeStruct((1, d), F32),
                          compiler_params=_params())(parts, c_ctx)


def _adamw(parts, w, m, v, name, own=None):
    n_parts, rows, cols = parts.shape
    row_bytes = cols * (parts.dtype.itemsize * (n_parts + 1) + 7 * 4)
    tr = _tile(rows, max(16, (8 * 1024 * 1024) // row_bytes), 16 if rows % 16 == 0 else 8)
    blocks = rows // tr

    def body(*refs):
        p_ref, w_ref, m_ref, v_ref, g_ref, d_ref, mo_ref, vo_ref = refs[-8:]
        if own is None:
            g = p_ref[0].astype(F32)
        else:
            g = refs[1][...].astype(F32) + p_ref[0].astype(F32)
        for i in range(1, n_parts):
            g = g + p_ref[i].astype(F32)
        m2 = ADAM_B1 * m_ref[...] + (1.0 - ADAM_B1) * g
        v2 = ADAM_B2 * v_ref[...] + (1.0 - ADAM_B2) * jnp.square(g)
        m_hat = m2 / (1.0 - ADAM_B1 ** ADAM_STEP)
        v_hat = v2 / (1.0 - ADAM_B2 ** ADAM_STEP)
        g_ref[...] = g
        d_ref[...] = -ADAM_LR * (m_hat / (jnp.sqrt(v_hat) + ADAM_EPS) + ADAM_WD * w_ref[...])
        mo_ref[...] = m2
        vo_ref[...] = v2

    out_shape = [jax.ShapeDtypeStruct((rows, cols), F32)] * 4
    if own is None:
        spec = _rspec(tr, cols)
        return pl.pallas_call(
            body, name=name, grid=(blocks,),
            in_specs=[pl.BlockSpec((n_parts, tr, cols), lambda i: (0, i, 0)), spec, spec, spec],
            out_specs=[spec] * 4, out_shape=out_shape, compiler_params=_params(("parallel",)))(parts, w, m, v)
    g_full, axis, me = own

    def own_map(i, me_ref):
        return (me_ref[0] * blocks + i, 0) if axis == 0 else (i, me_ref[0])

    spec = pl.BlockSpec((tr, cols), lambda i, me_ref: (i, 0))
    return pl.pallas_call(
        body, name=name, out_shape=out_shape,
        grid_spec=pltpu.PrefetchScalarGridSpec(
            num_scalar_prefetch=1, grid=(blocks,),
            in_specs=[pl.BlockSpec((tr, cols), own_map),
                      pl.BlockSpec((n_parts, tr, cols), lambda i, me_ref: (0, i, 0)), spec, spec, spec],
            out_specs=[spec] * 4),
        compiler_params=_params(("parallel",)))(me.reshape(1), g_full, parts, w, m, v)


def _rope_tables(pos, dk):
    half = dk // 2
    inv_freq = 1.0 / (ROPE_BASE ** jnp.linspace(0.0, 1.0, half, dtype=F32))
    ang = pos[:, None] * inv_freq[None, :]
    return jnp.cos(ang), jnp.sin(ang)


def _pad_lanes(v, width):
    return jnp.pad(v, ((0, 0), (0, width - v.shape[1])))


def kernel(x, c, ctx, c_ctx, w_mod, b_mod, norm1_g, w_in, conv_w, w_conv_out, ret_decay_fwd, ret_decay_bwd, w_ret_out, w_o, norm2_g, w_ff1, w_ff2, final_g, loss_target, m_c_ctx, m_w_mod, m_b_mod, m_norm1_g, m_w_in, m_conv_w, m_w_conv_out, m_ret_decay_fwd, m_ret_decay_bwd, m_w_ret_out, m_w_o, m_norm2_g, m_w_ff1, m_w_ff2, m_final_g, v_c_ctx, v_w_mod, v_b_mod, v_norm1_g, v_w_in, v_conv_w, v_w_conv_out, v_ret_decay_fwd, v_ret_decay_bwd, v_w_ret_out, v_w_o, v_norm2_g, v_w_ff1, v_w_ff2, v_final_g):
    n_lat, d = x.shape[1], x.shape[2]
    n_ctx = ctx.shape[1]
    assert n_ctx % CHUNK == 0 and n_lat % CHUNK == 0 and n_lat % GRID_W == 0
    dk = d // HEADS
    nb = w_mod.shape[2]
    me = 4 * lax.axis_index("x") + 2 * lax.axis_index("y") + lax.axis_index("c")
    xl, ctxl, target = x[0], ctx[0], loss_target[0]

    first = _small_allgather(jnp.concatenate([c, _pad_lanes(conv_w[0], d), jnp.zeros((4, d), F32)], axis=0), "ag_c")
    conv_w_f = first[:, 1:4, :d // N_DEV].transpose(1, 0, 2).reshape(3, d)

    c_all = first[:, 0, :]
    cs = jnp.concatenate([c_all, c_ctx[None], jnp.zeros((7, d), F32)], axis=0)
    dec = jnp.pad(jnp.concatenate([ret_decay_fwd, ret_decay_bwd], axis=0), ((0, 6), (0, 128 - HEADS)))
    b_loc = lax.dynamic_slice(b_mod, (0, me * nb), (1, nb))
    modp, lgs = _mod_proj(cs, w_mod[0], b_loc, dec, "mod_proj")
    modp_all = _small_allgather(modp, "ag_mod")
    mod_l = lax.dynamic_index_in_dim(modp_all, me, axis=1, keepdims=False).reshape(N_MOD, d)
    mod_c = modp_all[:, 8, :].reshape(N_MOD, d)
    lgs = lgs[0:2, :HEADS]
    zero_row = jnp.zeros((1, d), F32)
    vec1_l = jnp.concatenate([norm1_g, mod_l[0:1], mod_l[1:2], zero_row], axis=0)
    vec1_c = jnp.concatenate([norm1_g, mod_c[0:1], mod_c[1:2], zero_row], axis=0)
    vec2 = jnp.concatenate([norm2_g, mod_l[3:4], mod_l[4:5], mod_l[2:3]], axis=0)
    vec3 = jnp.concatenate([mod_l[5:6], final_g[None]], axis=0)

    a_all = _modulate(ctxl, vec1_c, "modulate1_ctx", n_lat + n_ctx,
                      into=_modulate(xl, vec1_l, "modulate1", n_lat + n_ctx))
    arrival = me ^ jnp.array([0, 1, 4, 5, 2, 3, 6, 7], jnp.int32)
    riders = [(_place_shard(w[0], 0, me, "place_" + wname), 0)
              for wname, w in (("w_conv_out", w_conv_out), ("w_ret_out", w_ret_out), ("w_o", w_o))]
    w_in_f, w_conv_out_f, w_ret_out_f, w_o_f, z = _gather_project(
        a_all, _place_shard(w_in[0], 1, me, "place_w_in"), arrival, "proj_in", riders=riders)

    later, after = {}, (z, conv_w_f)
    for wname, w, axis in (("w_ff1", w_ff1, 1), ("w_ff2", w_ff2, 0)):
        later[wname], token = _exchange_start(_place_shard(w[0], axis, me, "place_" + wname), axis, True,
                                              "ag_" + wname + "_start", after=after)
        after = (token,)
    pc = _conv_fwd(z, conv_w_f, n_lat, "conv_fwd", token)
    pos = jnp.concatenate([n_ctx + jnp.arange(n_lat, dtype=F32), jnp.arange(n_ctx, dtype=F32)])
    cos, sin = _rope_tables(pos, dk)
    q_s, k_s, v_s = _rotary_fwd(z, cos, sin, n_lat, "rotary_fwd")
    o, st_f, st_b = _ret_fwd(q_s, k_s, v_s, lgs, n_lat, "ret_fwd")
    r = _gn_fwd(o, z, "gn_fwd")
    y_conv = _mm(pc, w_conv_out_f, "nn", [F32], "proj_conv_out", tn=2048)
    tn_d = _tile(d, 1024, 128)
    gate_offs = (CGC * d // tn_d, CGR * d // tn_d)

    def merge(acc, yc, gc, gr):
        return acc, _sig(gc) * yc + _sig(gr) * acc

    y_ret, mg = _mm(r, w_ret_out_f, "nn", [F32, BF16], "proj_ret_out", tm=512, tn=tn_d, tk=2 * d,
                    extras=[(y_conv, 0), (z, gate_offs[0]), (z, gate_offs[1])], epilogue=merge)
    y_l = _mm(mg, w_o_f, "nn", [F32], "proj_o", tn=2048)
    x1, a2 = _resid_modulate(xl, y_l, vec2, "resid_modulate2")

    def sqrelu(acc):
        return acc, jnp.square(jnp.maximum(acc, 0.0))

    w_ff1_f = _exchange_wait(later["w_ff1"], a2, "ag_w_ff1_wait")
    hff, s = _mm(a2, w_ff1_f, "nn", [BF16, BF16], "ff1", epilogue=sqrelu)
    w_ff2_f = _exchange_wait(later["w_ff2"], s, "ag_w_ff2_wait")
    f = _mm(s, w_ff2_f, "nn", [F32], "ff2")
    dx2, df, acc3 = _loss_head(x1, f, target, vec3, "loss_head")

    def d_sqrelu(acc, h):
        return (acc * (2.0 * jnp.maximum(h.astype(F32), 0.0)),)

    dh = _mm(df, w_ff2_f, "nt", [BF16], "ff2_dx", extras=[(hff, 0)], epilogue=d_sqrelu)
    sent = {}
    sent["w_ff2"], token = _exchange_start(_mm(s, df, "tn", [BF16], "ff2_dw"), 0, False, "rs_w_ff2_start")
    da2 = _mm(dh, w_ff1_f, "nt", [F32], "ff1_dx", dep=token)
    sent["w_ff1"], token = _exchange_start(_mm(a2, dh, "tn", [BF16], "ff1_dw"), 1, False, "rs_w_ff1_start")
    dx1, dyl, acc2 = _modulate_bwd(da2, x1, vec2, "modulate2_bwd", dx_in=dx2, y=y_l)

    def d_merge(acc, yc, yr, gc, gr):
        sc, sr = _sig(gc), _sig(gr)
        return acc * sc, acc * sr, acc * yc * (sc * (1.0 - sc)), acc * yr * (sr * (1.0 - sr))

    dyc, dyr, dgc, dgr = _mm(dyl, w_o_f, "nt", [BF16] * 4, "proj_o_dx", tm=512, tn=tn_d,
                             extras=[(y_conv, 0), (y_ret, 0), (z, gate_offs[0]), (z, gate_offs[1])], epilogue=d_merge,
                             dep=token)
    sent["w_o"], token = _exchange_start(_mm(mg, dyl, "tn", [BF16], "proj_o_dw"), 0, False, "rs_w_o_start")
    dpc = _mm(dyc, w_conv_out_f, "nt", [F32], "proj_conv_out_dx", tn=2048, dep=token)
    sent["w_conv_out"], token = _exchange_start(_mm(pc, dyc, "tn", [BF16], "proj_conv_out_dw"), 0, False,
                                                "rs_w_conv_out_start")
    dz_conv, acc_conv = _conv_bwd(dpc, z, conv_w_f, "conv_bwd")
    dv_head = 2 * d // HEADS
    do, dz_g = _mm(dyr, w_ret_out_f, "nt", [BF16, BF16], "proj_ret_out_dx", tn=dv_head,
                   extras=[(o, 0), (z, CG * d // dv_head)], epilogue=_gn_bwd_head, dep=token)
    sent["w_ret_out"], token = _exchange_start(_mm(r, dyr, "tn", [BF16], "proj_ret_out_dw"), 0, False,
                                               "rs_w_ret_out_start")
    dq_f, dk_f, dv_f, dlg_f, dq_b, dk_b, dv_b, dlg_b = _ret_bwd(q_s, k_s, v_s, do, st_f, st_b, lgs, "ret_bwd")
    dz = _dz_assemble(dq_f, dq_b, dk_f, dk_b, dv_f, dv_b, cos, sin, dz_conv, dz_g, dgc, dgr, n_lat, n_ctx,
                      "dz_assemble")
    g_in = _mm(a_all, dz, "tn", [BF16], "proj_in_dw", tn=2048, tk=(n_lat + n_ctx) // 4, dep=token)
    sent["w_in"], token = _exchange_start(g_in, 1, False, "rs_w_in_start")
    da_all = _mm(dz, w_in_f, "nt", [F32], "proj_in_dx", tn=2048, dep=token)
    grad_x, acc1 = _modulate_bwd(da_all, xl, vec1_l, "modulate1_bwd", dx_in=dx1)
    _, acc1c = _modulate_bwd(da_all, ctxl, vec1_c, "modulate1_ctx_bwd", da_off=n_lat)

    lane_pad = functools.partial(_pad_lanes, width=d)
    packet = jnp.concatenate([
        acc1[2:3] + acc1c[2:3], acc2[2:3], acc3[1:2],
        acc1[0:1], acc1[1:2], acc2[3:4], acc2[0:1], acc2[1:2], acc3[0:1],
        acc1c[0:1], acc1c[1:2],
        acc_conv[0:3],
        lane_pad(dlg_f[:, 0, 0][None]), lane_pad(dlg_b[:, 0, 0][None]),
        acc3[2:3],
        jnp.zeros((7, d), F32)], axis=0)
    n_rows = packet.shape[0]
    packets = lax.dynamic_update_slice(jnp.zeros((N_DEV * n_rows, d), F32), packet, (me * n_rows, 0))
    packets_sent, after = _exchange_start(packets, 0, True, "ag_small_start")
    res = {}

    def update(wname, axis, w, m, v, after):
        g_full, parts = _exchange_wait(sent[wname], after, "rs_" + wname + "_wait")
        res[wname] = _adamw(parts, w[0], m[0], v[0], "adamw_" + wname, own=(g_full, axis, me))
        return res[wname][0]

    for wname, axis, w, m, v in (("w_ff2", 0, w_ff2, m_w_ff2, v_w_ff2), ("w_ff1", 1, w_ff1, m_w_ff1, v_w_ff1),
                                 ("w_o", 0, w_o, m_w_o, v_w_o), ("w_conv_out", 0, w_conv_out, m_w_conv_out, v_w_conv_out),
                                 ("w_ret_out", 0, w_ret_out, m_w_ret_out, v_w_ret_out)):
        after = update(wname, axis, w, m, v, after)
    gathered = _exchange_wait(packets_sent, after, "ag_small_wait").reshape(N_DEV, n_rows, d)
    red, misc = _reduce_small(gathered, dec, d, "reduce_small")
    dmod_ctx = jnp.concatenate([red[9], red[10], jnp.zeros((4 * d,), F32)])
    dmod_all = jnp.concatenate([gathered[:, 3:9, :].reshape(N_DEV, N_MOD * d), dmod_ctx[None]], axis=0)
    dm = jnp.pad(lax.dynamic_slice(dmod_all, (0, me * nb), (N_DEV + 1, nb)), ((0, 7), (0, 0)))
    g_mod, c_part = _mod_grad(cs, dm, w_mod[0], "mod_grad")
    g_b_mod = red[3:9].reshape(1, N_MOD * d) + dmod_ctx[None]
    g_conv_w = lax.dynamic_slice(red[11:14], (0, me * (d // N_DEV)), (3, d // N_DEV))

    c_parts = lax.dynamic_update_slice(jnp.zeros((N_DEV * 16, d), F32), c_part, (me * 16, 0))
    c_parts_sent, token = _exchange_start(c_parts, 0, True, "ag_c_ctx_start")
    res["w_mod"] = _adamw(g_mod[None], w_mod[0], m_w_mod[0], v_w_mod[0], "adamw_w_mod")
    after = update("w_in", 1, w_in, m_w_in, v_w_in, token)
    c_parts = _exchange_wait(c_parts_sent, after, "ag_c_ctx_wait").reshape(N_DEV, 16, d)
    g_c_ctx = _c_ctx_grad(c_parts, c_ctx[None], "c_ctx_grad")
    res = {k: tuple(t[None] for t in val) for k, val in res.items()}

    small = [("c_ctx", g_c_ctx, c_ctx, m_c_ctx, v_c_ctx), ("b_mod", g_b_mod, b_mod, m_b_mod, v_b_mod),
             ("norm1_g", red[0:1], norm1_g, m_norm1_g, v_norm1_g), ("conv_w", g_conv_w, conv_w, m_conv_w, v_conv_w),
             ("ret_decay_fwd", misc[1:2, :HEADS], ret_decay_fwd, m_ret_decay_fwd, v_ret_decay_fwd),
             ("ret_decay_bwd", misc[2:3, :HEADS], ret_decay_bwd, m_ret_decay_bwd, v_ret_decay_bwd),
             ("norm2_g", red[1:2], norm2_g, m_norm2_g, v_norm2_g), ("final_g", red[2:3], final_g, m_final_g, v_final_g)]

    def flat(t):
        t = t.reshape(-1)
        return jnp.pad(t, (0, (-t.shape[0]) % 1024))

    packed = [jnp.concatenate([flat(item[j]) for item in small]).reshape(-1, 128) for j in range(1, 5)]
    outs = _adamw(packed[0][None], packed[1], packed[2], packed[3], "adamw_small")
    start = 0
    for name, _, w, _, _ in small:
        size = w.size
        res[name] = tuple(o.reshape(-1)[start:start + size].reshape(w.shape) for o in outs)
        start += size + (-size) % 1024

    order = ["c_ctx", "w_mod", "b_mod", "norm1_g", "w_in", "conv_w", "w_conv_out", "ret_decay_fwd", "ret_decay_bwd",
             "w_ret_out", "w_o", "norm2_g", "w_ff1", "w_ff2", "final_g"]
    loss = misc[0, 0]
    return (loss, grad_x[None], *[res[n][0] for n in order], *[res[n][1] for n in order],
            *[res[n][2] for n in order], *[res[n][3] for n in order])
```

```python
import functools

import jax
import jax.numpy as jnp
from jax import lax
from jax.experimental import pallas as pl
from jax.experimental.pallas import tpu as pltpu

F32 = jnp.float32
BF16 = jnp.bfloat16
MESH = pl.DeviceIdType.MESH

N_DEV = 8
HEADS = 8
N_MOD = 6
N_IN = 11
GRID_W = 64
CHUNK = 256
ROPE_BASE = 10000.0
EPS = 1e-6
ADAM_LR, ADAM_B1, ADAM_B2, ADAM_EPS, ADAM_WD, ADAM_STEP = 0.001, 0.9, 0.999, 1e-08, 0.01, 10
VMEM_LIMIT = 56 * 1024 * 1024
HIGHEST = lax.Precision.HIGHEST
CB, CC, CX, CQ, CK, CV, CG, CGC, CGR = 0, 1, 2, 3, 4, 5, 7, 9, 10


def _tile(n, target, mult):
    t = (min(target, n) // mult) * mult
    while t >= mult:
        if n % t == 0:
            return t
        t -= mult
    return n


def _params(sem=None):
    return pltpu.CompilerParams(dimension_semantics=sem, vmem_limit_bytes=VMEM_LIMIT)


def _sig(v):
    return 1.0 / (1.0 + jnp.exp(-v))


def _coords():
    return lax.axis_index("x"), lax.axis_index("y"), lax.axis_index("c")


def _flip(p, m):
    return tuple(1 - v if (m >> s) & 1 else v for v, s in zip(p, (2, 1, 0)))


def _index(p):
    return 4 * p[0] + 2 * p[1] + p[2]


def _small_allgather(x, name):
    r, n = x.shape

    def body(x_ref, out_ref, send_sems, recv_sems):
        me = _coords()
        out_ref[pl.ds(_index(me), 1)] = x_ref[...][None]
        sent = []
        for m in range(1, N_DEV):
            cp = pltpu.make_async_remote_copy(
                src_ref=x_ref, dst_ref=out_ref.at[_index(me)], send_sem=send_sems.at[m - 1],
                recv_sem=recv_sems.at[m - 1], device_id=_flip(me, m), device_id_type=MESH)
            cp.start()
            sent.append(cp)
        for m in range(1, N_DEV):
            pltpu.make_async_remote_copy(
                src_ref=x_ref, dst_ref=out_ref.at[_index(_flip(me, m))], send_sem=send_sems.at[m - 1],
                recv_sem=recv_sems.at[m - 1], device_id=_flip(me, m), device_id_type=MESH).wait_recv()
        for cp in sent:
            cp.wait_send()

    return pl.pallas_call(
        body, name=name, out_shape=jax.ShapeDtypeStruct((N_DEV, r, n), x.dtype),
        in_specs=[pl.BlockSpec(memory_space=pltpu.VMEM)], out_specs=pl.BlockSpec(memory_space=pltpu.VMEM),
        scratch_shapes=[pltpu.SemaphoreType.DMA((N_DEV - 1,)), pltpu.SemaphoreType.DMA((N_DEV - 1,))],
    )(x)


def _window(ref, j, r, c, axis):
    if axis == 0:
        return ref.at[pl.ds(pl.multiple_of(j * r, 8), r), :]
    return ref.at[:, pl.ds(pl.multiple_of(j * c, 128), c)]


_HBM = pl.BlockSpec(memory_space=pltpu.HBM)
_SEM = pl.BlockSpec(memory_space=pltpu.SEMAPHORE)
_ANY = pl.BlockSpec(memory_space=pl.ANY)
_EFFECT = pltpu.SideEffectType.DATAFLOW_SIDE_EFFECTING


def _place_shard(w, axis, me, name):
    r, c = w.shape
    full = (N_DEV * r, c) if axis == 0 else (r, N_DEV * c)
    tr = _tile(r, max(16, (2 * 1024 * 1024) // (c * 4)), 16)
    blocks = r // tr

    def out_map(i, me_ref):
        return (me_ref[0] * blocks + i, 0) if axis == 0 else (i, me_ref[0])

    def body(me_ref, w_ref, out_ref):
        out_ref[...] = w_ref[...].astype(BF16)

    return pl.pallas_call(
        body, name=name, out_shape=jax.ShapeDtypeStruct(full, BF16),
        grid_spec=pltpu.PrefetchScalarGridSpec(
            num_scalar_prefetch=1, grid=(blocks,), in_specs=[pl.BlockSpec((tr, c), lambda i, me_ref: (i, 0))],
            out_specs=pl.BlockSpec((tr, c), out_map)),
        compiler_params=_params(("parallel",)))(me.reshape(1), w)


def _exchange_start(src, axis, gather, name, after=()):
    r, c = (src.shape[0] // N_DEV, src.shape[1]) if axis == 0 else (src.shape[0], src.shape[1] // N_DEV)
    n_hbm = 1 if gather else 2
    n_after = len(after)

    def body(*refs):
        src_ref, land_ref = refs[0], refs[n_hbm - 1]
        send_sems, recv_sems = refs[n_hbm + n_after:n_hbm + n_after + 2]
        token = refs[-1]
        me = _coords()
        for m in range(1, N_DEV):
            peer = _flip(me, m)
            if gather:
                mine = theirs = _window(land_ref, _index(me), r, c, axis)
            else:
                mine, theirs = _window(src_ref, _index(peer), r, c, axis), land_ref.at[m - 1]
            pltpu.make_async_remote_copy(
                src_ref=mine, dst_ref=theirs, send_sem=send_sems.at[m - 1], recv_sem=recv_sems.at[m - 1],
                device_id=peer, device_id_type=MESH).start()
        token[...] = jnp.zeros_like(token)

    hbm = [pltpu.with_memory_space_constraint(src, pltpu.HBM)]
    if not gather:
        hbm.append(pltpu.with_memory_space_constraint(lax.empty((N_DEV - 1, r, c), src.dtype), pltpu.HBM))
    outs = pl.pallas_call(
        body, name=name,
        out_shape=(pltpu.SemaphoreType.DMA((N_DEV - 1,)), pltpu.SemaphoreType.DMA((N_DEV - 1,)),
                   *[pltpu.HBM(t.shape, t.dtype) for t in hbm], jax.ShapeDtypeStruct((8, 128), F32)),
        in_specs=[_HBM] * n_hbm + [_ANY] * n_after,
        out_specs=(_SEM, _SEM, *[_HBM] * n_hbm, pl.BlockSpec(memory_space=pltpu.VMEM)),
        input_output_aliases={i: 2 + i for i in range(n_hbm)},
        compiler_params=pltpu.CompilerParams(has_side_effects=_EFFECT),
    )(*hbm, *after)
    return (outs[:2], outs[2:2 + n_hbm], (axis, gather, r, c)), outs[-1]


def _exchange_wait(handle, after, name):
    (send_sems, recv_sems), hbm, (axis, gather, r, c) = handle
    n_hbm = len(hbm)

    def body(*refs):
        src_ref, land_ref = refs[0], refs[n_hbm - 1]
        send_sems, recv_sems = refs[n_hbm:n_hbm + 2]
        me = _coords()
        for m in range(1, N_DEV):
            peer = _flip(me, m)
            if gather:
                mine, theirs = _window(land_ref, _index(me), r, c, axis), _window(land_ref, _index(peer), r, c, axis)
            else:
                mine, theirs = _window(src_ref, _index(peer), r, c, axis), land_ref.at[m - 1]
            copy = pltpu.make_async_remote_copy(
                src_ref=mine, dst_ref=theirs, send_sem=send_sems.at[m - 1], recv_sem=recv_sems.at[m - 1],
                device_id=peer, device_id_type=MESH)
            copy.wait_send()
            copy.wait_recv()

    outs = pl.pallas_call(
        body, name=name, out_shape=tuple(pltpu.HBM(t.shape, t.dtype) for t in hbm),
        in_specs=[_HBM] * n_hbm + [_SEM, _SEM, _ANY], out_specs=tuple([_HBM] * n_hbm),
        input_output_aliases={i: i for i in range(n_hbm)},
        compiler_params=pltpu.CompilerParams(has_side_effects=_EFFECT),
    )(*hbm, send_sems, recv_sems, after)
    return outs[0] if gather else tuple(outs)


def _mm(a, b, mode, out_dtypes, name, tm=1024, tn=1024, tk=2048, extras=(), epilogue=None, dep=None):
    if mode == "nn":
        (m, k), n = a.shape, b.shape[1]
    elif mode == "nt":
        (m, k), n = a.shape, b.shape[0]
    else:
        (k, m), n = a.shape, b.shape[1]
    tm, tn = _tile(m, tm, 8), _tile(n, tn, min(128, tn))
    tk = _tile(k, tk, 16 if mode == "tn" else 128)
    nk = k // tk
    swap = nk == 1 and (k * n + (n // tn) * m * k) < (m * k + (m // tm) * k * n)

    def ij(p, q):
        return (q, p) if swap else (p, q)

    def spec(shape, fn):
        return pl.BlockSpec(shape, lambda p, q, kk: fn(*ij(p, q), kk))

    a_spec = spec((tk, tm), lambda i, j, kk: (kk, i)) if mode == "tn" else spec((tm, tk), lambda i, j, kk: (i, kk))
    b_spec = spec((tn, tk), lambda i, j, kk: (j, kk)) if mode == "nt" else spec((tk, tn), lambda i, j, kk: (kk, j))
    dims = {"nn": (((1,), (0,)), ((), ())), "nt": (((1,), (1,)), ((), ())), "tn": (((0,), (0,)), ((), ()))}[mode]
    ex_specs = [spec((tm, tn), functools.partial(lambda i, j, kk, off: (i, j + off), off=off)) for _, off in extras]
    deps = [] if dep is None else [dep]
    dep_specs = [pl.BlockSpec(dep.shape, lambda p, q, kk: (0, 0))] if deps else []
    n_ex, n_out = len(extras), len(out_dtypes)
    n_in = 2 + n_ex + len(deps)

    def body(*refs):
        a_ref, b_ref = refs[0], refs[1]
        ex_refs = refs[2:2 + n_ex]
        out_refs = refs[n_in:n_in + n_out]

        def product():
            return lax.dot_general(a_ref[...], b_ref[...], dims, preferred_element_type=F32)

        def finish(res):
            res = epilogue(res, *[e[...] for e in ex_refs]) if epilogue is not None else (res,)
            for o_ref, val in zip(out_refs, res):
                o_ref[...] = val.astype(o_ref.dtype)

        if nk == 1:
            finish(product())
            return
        acc = refs[-1]
        kk = pl.program_id(2)

        @pl.when(kk == 0)
        def _():
            acc[...] = product()

        @pl.when((kk > 0) & (kk < nk - 1))
        def _():
            acc[...] += product()

        @pl.when(kk == nk - 1)
        def _():
            finish(acc[...] + product())

    outs = pl.pallas_call(
        body, name=name, grid=(*ij(m // tm, n // tn), nk),
        in_specs=[a_spec, b_spec] + ex_specs + dep_specs,
        out_specs=[spec((tm, tn), lambda i, j, kk: (i, j)) for _ in out_dtypes],
        out_shape=[jax.ShapeDtypeStruct((m, n), dt) for dt in out_dtypes],
        scratch_shapes=[pltpu.VMEM((tm, tn), F32)] if nk > 1 else [],
        compiler_params=_params(("parallel", "parallel", "arbitrary")),
    )(a, b, *[e for e, _ in extras], *deps)
    return outs if n_out > 1 else outs[0]


def _gather_project(a, w_full, order, name, riders=(), tm=768):
    m, k = a.shape
    c = w_full.shape[1] // N_DEV
    tm = _tile(m, tm, 8)
    n_i = m // tm
    chips = [4, 2, 6]
    n_rid = len(riders)
    shard = [(k, c, 1)] + [((t.shape[0] // N_DEV, t.shape[1], 0) if ax == 0 else (t.shape[0], t.shape[1] // N_DEV, 1))
                           for t, ax in riders]

    def body(order_ref, a_ref, *refs):
        outs = refs[1 + n_rid:2 + 2 * n_rid]
        z_ref, wbuf, send_sems, recv_sems, load_sems = refs[2 + 2 * n_rid:]
        w_ref = outs[0]
        p, i = pl.program_id(0), pl.program_id(1)
        me = _coords()
        sibling = _flip(me, 1)

        def copy(t, s, block, to):
            r_t, c_t, ax_t = shard[t]
            win = _window(outs[t], _index(block), r_t, c_t, ax_t)
            return pltpu.make_async_remote_copy(src_ref=win, dst_ref=win, send_sem=send_sems.at[7 * t + s],
                                                recv_sem=recv_sems.at[7 * t + s], device_id=to, device_id_type=MESH)

        def first(t):
            return [copy(t, 0, me, sibling)] + [copy(t, 1 + j, me, _flip(me, mm)) for j, mm in enumerate(chips)]

        def passed(t):
            return [copy(t, 4 + j, _flip(me, mm), sibling) for j, mm in enumerate(chips)]

        @pl.when((p == 0) & (i == 0))
        def _():
            for t in range(1 + n_rid):
                for cp in first(t):
                    cp.start()

        def load(piece):
            return pltpu.make_async_copy(_window(w_ref, order_ref[piece], k, c, 1), wbuf.at[piece % 2],
                                         load_sems.at[piece % 2])

        @pl.when((p == 0) & (i == 0))
        def _():
            load(p).start()

        arrivals = [(0, None)] + [x for j in range(3) for x in ((1 + j, passed(0)[j]), (4 + j, None))]
        for piece, (sem, forward) in enumerate(arrivals, start=1):
            @pl.when((p == piece - 1) & (i == n_i - 1))
            def _(sem=sem, forward=forward, piece=piece):
                src_block = (sibling if piece == 1 else
                             _flip(me if piece % 2 == 0 else sibling, chips[(piece - 2) // 2]))
                copy(0, sem, src_block, me).wait_recv()
                if forward is not None:
                    forward.start()
                load(p + 1).start()

        @pl.when(i == 0)
        def _():
            load(p).wait()

        @pl.when((p == N_DEV - 1) & (i == 0))
        def _():
            for t in range(1, 1 + n_rid):
                for j, mm in enumerate(chips):
                    copy(t, 1 + j, _flip(me, mm), me).wait_recv()
                    passed(t)[j].start()

        z_ref[...] = jnp.dot(a_ref[...], wbuf[p % 2], preferred_element_type=F32)

        @pl.when((p == N_DEV - 1) & (i == n_i - 1))
        def _():
            for t in range(1, 1 + n_rid):
                copy(t, 0, sibling, me).wait_recv()
                for j, mm in enumerate(chips):
                    copy(t, 4 + j, _flip(sibling, mm), me).wait_recv()
            for t in range(1 + n_rid):
                for cp in first(t) + passed(t):
                    cp.wait_send()

    gathered = [w_full] + [t for t, _ in riders]
    n_sem = 7 * len(gathered)
    return pl.pallas_call(
        body, name=name,
        out_shape=[jax.ShapeDtypeStruct(t.shape, t.dtype) for t in gathered]
        + [jax.ShapeDtypeStruct((m, N_DEV * c), F32)],
        grid_spec=pltpu.PrefetchScalarGridSpec(
            num_scalar_prefetch=1, grid=(N_DEV, n_i),
            in_specs=[pl.BlockSpec((tm, k), lambda p, i, order_ref: (i, 0))]
            + [pl.BlockSpec(memory_space=pl.ANY)] * len(gathered),
            out_specs=[pl.BlockSpec(memory_space=pl.ANY)] * len(gathered)
            + [pl.BlockSpec((tm, c), lambda p, i, order_ref: (i, order_ref[p]))],
            scratch_shapes=[pltpu.VMEM((2, k, c), w_full.dtype), pltpu.SemaphoreType.DMA((n_sem,)),
                            pltpu.SemaphoreType.DMA((n_sem,)), pltpu.SemaphoreType.DMA((2,))]),
        input_output_aliases={2 + t: t for t in range(len(gathered))},
        compiler_params=_params(("arbitrary", "arbitrary")))(order, a, *gathered)


def _rspec(tr, w, cb=0, off=0):
    return pl.BlockSpec((tr, w), lambda i: (i + off, cb))


def _cspec(shape):
    return pl.BlockSpec(shape, lambda i: (0,) * len(shape))


def _rms(xf):
    rstd = lax.rsqrt(jnp.mean(xf * xf, axis=-1, keepdims=True) + EPS)
    return xf * rstd, rstd


def _rms_bwd(dn, n, rstd):
    return rstd * (dn - n * jnp.mean(dn * n, axis=-1, keepdims=True))


def _colsum(v):
    return jnp.sum(v, axis=0, keepdims=True)


def _total(v):
    return jnp.sum(jnp.sum(v, axis=1, keepdims=True), axis=0, keepdims=True)


def _modulate(x, vec, name, total_rows, into=None):
    rows, d = x.shape
    tr = _tile(rows, 256, 8)
    off = 0 if into is None else (total_rows - rows) // tr

    def body(x_ref, v_ref, *refs):
        n, _ = _rms(x_ref[...])
        refs[-1][...] = ((n * v_ref[0:1, :]) * (1.0 + v_ref[2:3, :]) + v_ref[1:2, :]).astype(BF16)

    return pl.pallas_call(
        body, name=name, grid=(rows // tr,),
        in_specs=[_rspec(tr, d), _cspec(vec.shape)] + ([] if into is None else [_ANY]),
        out_specs=_rspec(tr, d, off=off), out_shape=jax.ShapeDtypeStruct((total_rows, d), BF16),
        input_output_aliases={} if into is None else {2: 0},
        compiler_params=_params(("parallel",)))(x, vec, *([] if into is None else [into]))


def _resid_modulate(x, y, vec, name):
    rows, d = x.shape
    tr = _tile(rows, 256, 8)

    def body(x_ref, y_ref, v_ref, x1_ref, a_ref):
        x1 = x_ref[...] + v_ref[3:4, :] * y_ref[...]
        x1_ref[...] = x1
        n, _ = _rms(x1)
        a_ref[...] = ((n * v_ref[0:1, :]) * (1.0 + v_ref[2:3, :]) + v_ref[1:2, :]).astype(BF16)

    return pl.pallas_call(
        body, name=name, grid=(rows // tr,), in_specs=[_rspec(tr, d), _rspec(tr, d), _cspec(vec.shape)],
        out_specs=[_rspec(tr, d), _rspec(tr, d)],
        out_shape=[jax.ShapeDtypeStruct((rows, d), F32), jax.ShapeDtypeStruct((rows, d), BF16)],
        compiler_params=_params(("parallel",)))(x, y, vec)


def _loss_head(x1, f, target, vec, name):
    rows, d = x1.shape
    tr = _tile(rows, 256, 8)

    def body(x1_ref, f_ref, t_ref, v_ref, dx2_ref, df_ref, acc_ref):
        @pl.when(pl.program_id(0) == 0)
        def _():
            acc_ref[...] = jnp.zeros_like(acc_ref)

        gate, gain = v_ref[0:1, :], v_ref[1:2, :]
        fv = f_ref[...]
        n, rstd = _rms(x1_ref[...] + gate * fv)
        err = n * gain - t_ref[...]
        dy = err / d
        dx2 = _rms_bwd(dy * gain, n, rstd)
        dx2_ref[...] = dx2
        df_ref[...] = (dx2 * gate).astype(BF16)
        acc_ref[0:1, :] += _colsum(dx2 * fv)
        acc_ref[1:2, :] += _colsum(dy * n)
        acc_ref[2:3, :] += _colsum(err * err)

    return pl.pallas_call(
        body, name=name, grid=(rows // tr,),
        in_specs=[_rspec(tr, d), _rspec(tr, d), _rspec(tr, d), _cspec(vec.shape)],
        out_specs=[_rspec(tr, d), _rspec(tr, d), _cspec((8, d))],
        out_shape=[jax.ShapeDtypeStruct((rows, d), F32), jax.ShapeDtypeStruct((rows, d), BF16),
                   jax.ShapeDtypeStruct((8, d), F32)],
        compiler_params=_params(("arbitrary",)))(x1, f, target, vec)


def _modulate_bwd(da, x, vec, name, da_off=0, dx_in=None, y=None):
    rows, d = x.shape
    tr = _tile(rows, 256, 8)
    off = da_off // tr
    has_dx, has_y = dx_in is not None, y is not None

    def body(*refs):
        da_ref, x_ref, v_ref = refs[0], refs[1], refs[2]
        pos = 3
        dxin_ref = refs[pos] if has_dx else None
        pos += has_dx
        y_ref = refs[pos] if has_y else None
        pos += has_y
        dx_ref = refs[pos]
        dy_ref = refs[pos + 1] if has_y else None
        acc_ref = refs[-1]

        @pl.when(pl.program_id(0) == 0)
        def _():
            acc_ref[...] = jnp.zeros_like(acc_ref)

        gain, scale1 = v_ref[0:1, :], 1.0 + v_ref[2:3, :]
        dav = da_ref[...]
        n, rstd = _rms(x_ref[...])
        dx = _rms_bwd(dav * gain * scale1, n, rstd)
        if has_dx:
            dx = dx + dxin_ref[...]
        dx_ref[...] = dx
        acc_ref[0:1, :] += _colsum(dav)
        acc_ref[1:2, :] += _colsum(dav * (n * gain))
        acc_ref[2:3, :] += _colsum(dav * n * scale1)
        if has_y:
            acc_ref[3:4, :] += _colsum(dx * y_ref[...])
            dy_ref[...] = (dx * v_ref[3:4, :]).astype(BF16)

    ins = [da, x, vec] + ([dx_in] if has_dx else []) + ([y] if has_y else [])
    in_specs = [_rspec(tr, d, off=off), _rspec(tr, d), _cspec(vec.shape)] + [_rspec(tr, d)] * (has_dx + has_y)
    out_specs = [_rspec(tr, d)] + ([_rspec(tr, d)] if has_y else []) + [_cspec((8, d))]
    out_shape = ([jax.ShapeDtypeStruct((rows, d), F32)] + ([jax.ShapeDtypeStruct((rows, d), BF16)] if has_y else [])
                 + [jax.ShapeDtypeStruct((8, d), F32)])
    return pl.pallas_call(
        body, name=name, grid=(rows // tr,), in_specs=in_specs, out_specs=out_specs, out_shape=out_shape,
        compiler_params=_params(("arbitrary",)))(*ins)


def _conv_terms(cc, cx, w_ref, tr):
    t = lax.broadcasted_iota(jnp.int32, (tr, 1), 0) % GRID_W
    first, last = t == 0, t == GRID_W - 1
    u = cc * cx
    prev = jnp.where(first, 0.0, pltpu.roll(u, 1, 0))
    nxt = jnp.where(last, 0.0, pltpu.roll(u, tr - 1, 0))
    y = w_ref[0:1, :] * prev + w_ref[1:2, :] * u + w_ref[2:3, :] * nxt
    return u, prev, nxt, y, first, last


def _conv_fwd(z, conv_w, rows, name, dep):
    d = conv_w.shape[1]
    tr = _tile(rows, 256, GRID_W)

    def body(cb_ref, cc_ref, cx_ref, w_ref, dep_ref, out_ref):
        y = _conv_terms(cc_ref[...], cx_ref[...], w_ref, tr)[3]
        out_ref[...] = (cb_ref[...] * y).astype(BF16)

    return pl.pallas_call(
        body, name=name, grid=(rows // tr,),
        in_specs=[_rspec(tr, d, CB), _rspec(tr, d, CC), _rspec(tr, d, CX), _cspec(conv_w.shape), _cspec(dep.shape)],
        out_specs=_rspec(tr, d), out_shape=jax.ShapeDtypeStruct((rows, d), BF16),
        compiler_params=_params(("parallel",)))(z, z, z, conv_w, dep)


def _conv_bwd(dpc, z, conv_w, name):
    rows, d = dpc.shape
    tr = _tile(rows, 256, GRID_W)

    def body(dpc_ref, cb_ref, cc_ref, cx_ref, w_ref, dz_ref, acc_ref):
        @pl.when(pl.program_id(0) == 0)
        def _():
            acc_ref[...] = jnp.zeros_like(acc_ref)

        cc, cx, dpcv = cc_ref[...], cx_ref[...], dpc_ref[...]
        u, prev, nxt, y, first, last = _conv_terms(cc, cx, w_ref, tr)
        dy = dpcv * cb_ref[...]
        dy_next = jnp.where(last, 0.0, pltpu.roll(dy, tr - 1, 0))
        dy_prev = jnp.where(first, 0.0, pltpu.roll(dy, 1, 0))
        du = w_ref[0:1, :] * dy_next + w_ref[1:2, :] * dy + w_ref[2:3, :] * dy_prev
        dz_ref[:, 0:d] = (dpcv * y).astype(BF16)
        dz_ref[:, d:2 * d] = (du * cx).astype(BF16)
        dz_ref[:, 2 * d:3 * d] = (du * cc).astype(BF16)
        acc_ref[0:1, :] += _colsum(dy * prev)
        acc_ref[1:2, :] += _colsum(dy * u)
        acc_ref[2:3, :] += _colsum(dy * nxt)

    return pl.pallas_call(
        body, name=name, grid=(rows // tr,),
        in_specs=[_rspec(tr, d), _rspec(tr, d, CB), _rspec(tr, d, CC), _rspec(tr, d, CX), _cspec(conv_w.shape)],
        out_specs=[_rspec(tr, 3 * d), _cspec((8, d))],
        out_shape=[jax.ShapeDtypeStruct((rows, 3 * d), BF16), jax.ShapeDtypeStruct((8, d), F32)],
        compiler_params=_params(("arbitrary",)))(dpc, z, z, z, conv_w)


def _rotary_fwd(z, cos, sin, n_lat, name):
    rows = z.shape[0]
    d = z.shape[1] // N_IN
    dk = d // HEADS
    half = dk // 2
    tr = _tile(n_lat, 256, 8)
    tr = _tile(rows - n_lat, tr, 8)
    lat_blocks, all_blocks = n_lat // tr, rows // tr
    ctx_blocks = all_blocks - lat_blocks
    scan_rows = rows + ctx_blocks * tr

    def z_block(g):
        return jnp.where(g < all_blocks, g, g - ctx_blocks)

    def in_spec(w, cb=0):
        return pl.BlockSpec((tr, w), lambda g: (z_block(g), cb))

    def out_spec(w):
        return pl.BlockSpec((tr, w), lambda g: (jnp.where(g < all_blocks, g + ctx_blocks, g - all_blocks), 0))

    def body(q_ref, k_ref, v0_ref, v1_ref, cos_ref, sin_ref, qo_ref, ko_ref, vo_ref):
        cs, sn = cos_ref[...], sin_ref[...]
        keep = jnp.where(z_block(pl.program_id(0)) < lat_blocks, 1.0, 0.0)
        for src, dst, scale in ((q_ref, qo_ref, keep), (k_ref, ko_ref, dk ** -0.5)):
            for h in range(HEADS):
                lo, mid, hi = h * dk, h * dk + half, (h + 1) * dk
                t1, t2 = src[:, lo:mid], src[:, mid:hi]
                dst[:, lo:mid] = ((t1 * cs - t2 * sn) * scale).astype(BF16)
                dst[:, mid:hi] = ((t1 * sn + t2 * cs) * scale).astype(BF16)
        vo_ref[:, 0:d] = v0_ref[...].astype(BF16)
        vo_ref[:, d:2 * d] = v1_ref[...].astype(BF16)

    return pl.pallas_call(
        body, name=name, grid=(all_blocks + ctx_blocks,),
        in_specs=[in_spec(d, CQ), in_spec(d, CK), in_spec(d, CV), in_spec(d, CV + 1), in_spec(half), in_spec(half)],
        out_specs=[out_spec(d), out_spec(d), out_spec(2 * d)],
        out_shape=[jax.ShapeDtypeStruct((scan_rows, d), BF16), jax.ShapeDtypeStruct((scan_rows, d), BF16),
                   jax.ShapeDtypeStruct((scan_rows, 2 * d), BF16)],
        compiler_params=_params(("parallel",)))(z, z, z, z, cos, sin)


def _dz_assemble(dq_f, dq_b, dk_f, dk_b, dv_f, dv_b, cos, sin, dz_conv, dz_g, dz_gc, dz_gr, n_lat, n_ctx, name):
    rows = n_lat + n_ctx
    d = dq_f.shape[1]
    dk = d // HEADS
    half = dk // 2
    tr = _tile(n_ctx, 128, 8)
    lat_blocks, ctx_blocks = n_lat // tr, n_ctx // tr

    def fmap(i):
        return (jnp.where(i < lat_blocks, i + ctx_blocks, i - lat_blocks), 0)

    def bmap(i):
        return (i + ctx_blocks, 0)

    def lmap(i):
        return (jnp.minimum(i, lat_blocks - 1), 0)

    def body(qf_ref, qb_ref, kf_ref, kb_ref, vf_ref, vb_ref, cos_ref, sin_ref, c_ref, g_ref, gc_ref, gr_ref, out_ref):
        cs, sn = cos_ref[...], sin_ref[...]
        is_lat = pl.program_id(0) < lat_blocks
        keep = jnp.where(is_lat, 1.0, 0.0)
        for fa, fb, base, scale in ((qf_ref, qb_ref, CQ * d, keep), (kf_ref, kb_ref, CK * d, dk ** -0.5)):
            for h in range(HEADS):
                lo, mid, hi = h * dk, h * dk + half, (h + 1) * dk
                d1 = fa[:, lo:mid].astype(F32) + fb[:, lo:mid].astype(F32)
                d2 = fa[:, mid:hi].astype(F32) + fb[:, mid:hi].astype(F32)
                out_ref[:, base + lo:base + mid] = ((d1 * cs + d2 * sn) * scale).astype(BF16)
                out_ref[:, base + mid:base + hi] = ((d2 * cs - d1 * sn) * scale).astype(BF16)
        out_ref[:, CV * d:CG * d] = (vf_ref[...].astype(F32) + vb_ref[...].astype(F32)).astype(BF16)
        for src, lo, hi in ((c_ref, CB * d, CQ * d), (g_ref, CG * d, CGC * d), (gc_ref, CGC * d, CGR * d),
                            (gr_ref, CGR * d, N_IN * d)):
            out_ref[:, lo:hi] = jnp.where(is_lat, src[...], jnp.zeros_like(src))

    return pl.pallas_call(
        body, name=name, grid=(rows // tr,),
        in_specs=[pl.BlockSpec((tr, d), fmap), pl.BlockSpec((tr, d), bmap), pl.BlockSpec((tr, d), fmap),
                  pl.BlockSpec((tr, d), bmap), pl.BlockSpec((tr, 2 * d), fmap), pl.BlockSpec((tr, 2 * d), bmap),
                  _rspec(tr, half), _rspec(tr, half), pl.BlockSpec((tr, 3 * d), lmap), pl.BlockSpec((tr, 2 * d), lmap),
                  pl.BlockSpec((tr, d), lmap), pl.BlockSpec((tr, d), lmap)],
        out_specs=_rspec(tr, N_IN * d), out_shape=jax.ShapeDtypeStruct((rows, N_IN * d), BF16),
        compiler_params=_params(("parallel",)))(dq_f, dq_b, dk_f, dk_b, dv_f, dv_b, cos, sin, dz_conv, dz_g, dz_gc, dz_gr)


def _gn_fwd(o, z, name):
    rows = o.shape[0]
    d = z.shape[1] // N_IN
    dv = 2 * d // HEADS
    tr = _tile(rows, 128, 8)

    def body(o_ref, g0_ref, g1_ref, r_ref):
        for h in range(HEADS):
            lo, hi = h * dv, (h + 1) * dv
            g_ref, glo = (g0_ref, lo) if hi <= d else (g1_ref, lo - d)
            o = o_ref[:, lo:hi]
            cen = o - jnp.mean(o, axis=-1, keepdims=True)
            on = cen * lax.rsqrt(jnp.mean(cen * cen, axis=-1, keepdims=True) + EPS)
            g = g_ref[:, glo:glo + dv]
            r_ref[:, lo:hi] = (g * _sig(g) * on).astype(BF16)

    return pl.pallas_call(
        body, name=name, grid=(rows // tr,),
        in_specs=[_rspec(tr, 2 * d), _rspec(tr, d, CG), _rspec(tr, d, CG + 1)],
        out_specs=_rspec(tr, 2 * d), out_shape=jax.ShapeDtypeStruct((rows, 2 * d), BF16),
        compiler_params=_params(("parallel",)))(o, z, z)


def _gn_bwd_head(dr, o, g):
    cen = o - jnp.mean(o, axis=-1, keepdims=True)
    rstd = lax.rsqrt(jnp.mean(cen * cen, axis=-1, keepdims=True) + EPS)
    on = cen * rstd
    sg = _sig(g)
    don = dr * (g * sg)
    do = rstd * (don - jnp.mean(don, axis=-1, keepdims=True) - on * jnp.mean(don * on, axis=-1, keepdims=True))
    return do, dr * on * (sg * (1.0 + g * (1.0 - sg)))


def _decays(lg, rev):
    row = lax.broadcasted_iota(jnp.int32, (CHUNK, CHUNK), 0)
    col = lax.broadcasted_iota(jnp.int32, (CHUNK, CHUNK), 1)
    rel = ((col - row) if rev else (row - col)).astype(F32)
    mask = jnp.where(rel >= 0, jnp.exp(lg * jnp.maximum(rel, 0.0)), 0.0)
    r = lax.broadcasted_iota(jnp.int32, (CHUNK, 1), 0)
    rr = ((CHUNK - 1 - r) if rev else r).astype(F32)
    chunk_decay = jnp.exp(lg * jnp.full((1, 1), float(CHUNK), F32))
    return (rr, rel), mask, jnp.exp(lg * (rr + 1.0)), jnp.exp(lg * (CHUNK - 1.0 - rr)), chunk_decay


_NT = (((1,), (1,)), ((), ()))
_TN = (((0,), (0,)), ((), ()))


def _dot(a, b, dims=None):
    if dims is None:
        return jnp.dot(a, b, preferred_element_type=F32)
    return lax.dot_general(a, b, dims, preferred_element_type=F32)


def _ret_fwd(q, k, v, lgs, n_lat, name):
    tt, d = q.shape
    dk, dv = d // HEADS, 2 * d // HEADS
    nc = tt // CHUNK
    lat_chunks = n_lat // CHUNK
    ctx_chunks = (nc - lat_chunks) // 2

    def chunk_of(rev, i):
        return (nc - 1 - i) if rev else i

    def cmap(rev):
        return lambda h, i: (chunk_of(rev, i), h)

    def smap(rev):
        return lambda h, i: (chunk_of(rev, i), h, 0, 0)

    def body(lg_ref, qf_ref, kf_ref, vf_ref, qb_ref, kb_ref, vb_ref, o_ref, sf_ref, sb_ref, state_f, state_b):
        h, i = pl.program_id(0), pl.program_id(1)

        @pl.when(i == 0)
        def _():
            state_f[...] = jnp.zeros_like(state_f)
            state_b[...] = jnp.zeros_like(state_b)

        for rev, (q_ref, k_ref, v_ref, s_ref, state) in enumerate((
                (qf_ref, kf_ref, vf_ref, sf_ref, state_f), (qb_ref, kb_ref, vb_ref, sb_ref, state_b))):
            _, mask, qd, kd, cd = _decays(lg_ref[rev, h], bool(rev))
            qv, kv, vv = q_ref[...], k_ref[...], v_ref[...]
            st = state[...]
            p = _dot(qv, kv, _NT) * mask
            out = _dot(p.astype(BF16), vv) + _dot((qv * qd).astype(BF16), st.astype(BF16))
            s_ref[...] = st
            state[...] = cd * st + _dot((kv * kd).astype(BF16), vv, _TN)
            lat_chunk = chunk_of(bool(rev), i) - ctx_chunks
            is_lat = (lat_chunk >= 0) & (lat_chunk < lat_chunks)
            first = (2 * i < nc - 1) if rev else (2 * i <= nc - 1)

            @pl.when(is_lat & first)
            def _(lat_chunk=lat_chunk, out=out):
                o_ref[pl.ds(pl.multiple_of(lat_chunk * CHUNK, CHUNK), CHUNK), :] = out

            @pl.when(is_lat & jnp.logical_not(first))
            def _(lat_chunk=lat_chunk, out=out):
                o_ref[pl.ds(pl.multiple_of(lat_chunk * CHUNK, CHUNK), CHUNK), :] += out

    def specs(rev):
        return [pl.BlockSpec((CHUNK, dk), cmap(rev)), pl.BlockSpec((CHUNK, dk), cmap(rev)),
                pl.BlockSpec((CHUNK, dv), cmap(rev))]

    state_shape = jax.ShapeDtypeStruct((nc, HEADS, dk, dv), F32)
    return pl.pallas_call(
        body, name=name, grid=(HEADS, nc),
        in_specs=[pl.BlockSpec(memory_space=pltpu.SMEM)] + specs(False) + specs(True),
        out_specs=[pl.BlockSpec((n_lat, dv), lambda h, i: (0, h)), pl.BlockSpec((None, None, dk, dv), smap(False)),
                   pl.BlockSpec((None, None, dk, dv), smap(True))],
        out_shape=[jax.ShapeDtypeStruct((n_lat, 2 * d), F32), state_shape, state_shape],
        scratch_shapes=[pltpu.VMEM((dk, dv), F32), pltpu.VMEM((dk, dv), F32)],
        compiler_params=_params(("parallel", "arbitrary")))(lgs, q, k, v, q, k, v)


def _ret_bwd(q, k, v, do, states_f, states_b, lgs, name):
    tt, d = q.shape
    dk, dv = d // HEADS, 2 * d // HEADS
    nc = tt // CHUNK
    lat_chunks = do.shape[0] // CHUNK
    ctx_chunks = (nc - lat_chunks) // 2

    def chunk_of(rev, i):
        return i if rev else nc - 1 - i

    def cmap(rev):
        return lambda h, i: (chunk_of(rev, i), h)

    def do_map(rev):
        return lambda h, i: (jnp.clip(chunk_of(rev, i) - ctx_chunks, 0, lat_chunks - 1), h)

    def smap(rev):
        return lambda h, i: (chunk_of(rev, i), h, 0, 0)

    def body(lg_ref, *refs):
        h = pl.program_id(0)
        dstates = refs[-2:]

        @pl.when(pl.program_id(1) == 0)
        def _():
            for rev in (0, 1):
                dstates[rev][...] = jnp.zeros_like(dstates[rev])
                refs[10 + 4 * rev + 3][...] = jnp.zeros_like(refs[10 + 4 * rev + 3])

        for rev in (0, 1):
            q_ref, k_ref, v_ref, do_ref, s_ref = refs[5 * rev:5 * rev + 5]
            dq_ref, dk_ref, dv_ref, dlg_ref = refs[10 + 4 * rev:10 + 4 * rev + 4]
            dstate = dstates[rev]
            chunk = chunk_of(bool(rev), pl.program_id(1))
            is_lat = (chunk >= ctx_chunks) & (chunk < ctx_chunks + lat_chunks)
            (rr, rel), mask, qd, kd, cd = _decays(lg_ref[rev, h], bool(rev))
            qv, kv, vv = q_ref[...], k_ref[...], v_ref[...]
            dov = jnp.where(is_lat, do_ref[...], jnp.zeros_like(do_ref))
            st, dst = s_ref[...], dstate[...]
            st_b, dst_b = st.astype(BF16), dst.astype(BF16)
            p = _dot(qv, kv, _NT) * mask
            dp = _dot(dov, vv, _NT)
            da = (dp * mask).astype(BF16)
            dq_state = _dot(dov, st_b, _NT) * qd
            dk_state = _dot(vv, dst_b, _NT) * kd
            dq_ref[...] = (_dot(da, kv) + dq_state).astype(BF16)
            dk_ref[...] = (_dot(da, qv, _TN) + dk_state).astype(BF16)
            dv_ref[...] = (_dot(p.astype(BF16), dov, _TN) + _dot((kv * kd).astype(BF16), dst_b)).astype(BF16)
            dnew = cd * dst + _dot((qv * qd).astype(BF16), dov, _TN)
            dstate[...] = dnew
            through = rr * (jnp.sum(qv.astype(F32) * dq_state, axis=1, keepdims=True)
                            - jnp.sum(kv.astype(F32) * dk_state, axis=1, keepdims=True))
            dlg_ref[...] += _total(rel * p * dp) + _total(through) + CHUNK * _total(st * dnew)

    def specs(rev):
        return [pl.BlockSpec((CHUNK, dk), cmap(rev)), pl.BlockSpec((CHUNK, dk), cmap(rev)),
                pl.BlockSpec((CHUNK, dv), cmap(rev)), pl.BlockSpec((CHUNK, dv), do_map(rev)),
                pl.BlockSpec((None, None, dk, dv), smap(rev))]

    def outs(rev):
        return [pl.BlockSpec((CHUNK, dk), cmap(rev)), pl.BlockSpec((CHUNK, dk), cmap(rev)),
                pl.BlockSpec((CHUNK, dv), cmap(rev)), pl.BlockSpec((None, 8, 128), lambda h, i: (h, 0, 0))]

    return pl.pallas_call(
        body, name=name, grid=(HEADS, nc),
        in_specs=[pl.BlockSpec(memory_space=pltpu.SMEM)] + specs(False) + specs(True),
        out_specs=outs(False) + outs(True),
        out_shape=[jax.ShapeDtypeStruct((tt, d), BF16), jax.ShapeDtypeStruct((tt, d), BF16),
                   jax.ShapeDtypeStruct((tt, 2 * d), BF16), jax.ShapeDtypeStruct((HEADS, 8, 128), F32)] * 2,
        scratch_shapes=[pltpu.VMEM((dk, dv), F32), pltpu.VMEM((dk, dv), F32)],
        compiler_params=_params(("parallel", "arbitrary")))(lgs, q, k, v, do, states_f, q, k, v, do, states_b)


def _silu(v):
    return v * _sig(v)


def _mod_proj(cs, w_mod, b_loc, dec, name):
    nb = w_mod.shape[1]

    def body(cs_ref, w_ref, b_ref, dec_ref, out_ref, lg_ref):
        out_ref[...] = jnp.dot(_silu(cs_ref[...]), w_ref[...], preferred_element_type=F32, precision=HIGHEST) + b_ref[...]
        a = dec_ref[...]
        lg_ref[...] = jnp.minimum(a, 0.0) - jnp.log1p(jnp.exp(-jnp.abs(a)))

    return pl.pallas_call(
        body, name=name,
        out_shape=[jax.ShapeDtypeStruct((16, nb), F32), jax.ShapeDtypeStruct(dec.shape, F32)],
        compiler_params=_params())(cs, w_mod, b_loc, dec)


def _mod_grad(cs, dm, w_mod, name):
    d, nb = w_mod.shape

    def body(cs_ref, dm_ref, w_ref, gw_ref, part_ref):
        dmv = dm_ref[...]
        gw_ref[...] = lax.dot_general(_silu(cs_ref[...]), dmv, _TN, preferred_element_type=F32, precision=HIGHEST)
        part_ref[...] = lax.dot_general(dmv, w_ref[...], _NT, preferred_element_type=F32, precision=HIGHEST)

    return pl.pallas_call(
        body, name=name,
        out_shape=[jax.ShapeDtypeStruct((d, nb), F32), jax.ShapeDtypeStruct((16, d), F32)],
        compiler_params=_params())(cs, dm, w_mod)


def _reduce_small(gathered, dec, n_feat, name):
    _, rows, d = gathered.shape

    def body(g_ref, dec_ref, red_ref, misc_ref):
        total = g_ref[0]
        for i in range(1, N_DEV):
            total = total + g_ref[i]
        red_ref[...] = total
        misc_ref[...] = jnp.zeros_like(misc_ref)
        misc_ref[0:1, :] = jnp.zeros((1, 128), F32) + (0.5 / n_feat) * _total(total[16:17, :])
        misc_ref[1:3, :] = total[14:16, 0:128] * _sig(-dec_ref[0:2, :])

    return pl.pallas_call(
        body, name=name,
        out_shape=[jax.ShapeDtypeStruct((rows, d), F32), jax.ShapeDtypeStruct((8, 128), F32)],
        compiler_params=_params())(gathered, dec)


def _c_ctx_grad(parts, c_ctx, name):
    d = c_ctx.shape[1]

    def body(p_ref, c_ref, out_ref):
        total = p_ref[0]
        for i in range(1, N_DEV):
            total = total + p_ref[i]
        cv = c_ref[...]
        sg = _sig(cv)
        out_ref[...] = total[8:9, :] * (sg * (1.0 + cv * (1.0 - sg)))

    return pl.pallas_call(body, name=name, out_shape=jax.ShapeDtypeStruct((1, d), F32),
                          compiler_params=_params())(parts, c_ctx)


def _adamw(parts, w, m, v, name, own=None):
    n_parts, rows, cols = parts.shape
    row_bytes = cols * (parts.dtype.itemsize * (n_parts + 1) + 7 * 4)
    tr = _tile(rows, max(16, (8 * 1024 * 1024) // row_bytes), 16 if rows % 16 == 0 else 8)
    blocks = rows // tr

    def body(*refs):
        p_ref, w_ref, m_ref, v_ref, g_ref, d_ref, mo_ref, vo_ref = refs[-8:]
        if own is None:
            g = p_ref[0].astype(F32)
        else:
            g = refs[1][...].astype(F32) + p_ref[0].astype(F32)
        for i in range(1, n_parts):
            g = g + p_ref[i].astype(F32)
        m2 = ADAM_B1 * m_ref[...] + (1.0 - ADAM_B1) * g
        v2 = ADAM_B2 * v_ref[...] + (1.0 - ADAM_B2) * jnp.square(g)
        m_hat = m2 / (1.0 - ADAM_B1 ** ADAM_STEP)
        v_hat = v2 / (1.0 - ADAM_B2 ** ADAM_STEP)
        g_ref[...] = g
        d_ref[...] = -ADAM_LR * (m_hat / (jnp.sqrt(v_hat) + ADAM_EPS) + ADAM_WD * w_ref[...])
        mo_ref[...] = m2
        vo_ref[...] = v2

    out_shape = [jax.ShapeDtypeStruct((rows, cols), F32)] * 4
    if own is None:
        spec = _rspec(tr, cols)
        return pl.pallas_call(
            body, name=name, grid=(blocks,),
            in_specs=[pl.BlockSpec((n_parts, tr, cols), lambda i: (0, i, 0)), spec, spec, spec],
            out_specs=[spec] * 4, out_shape=out_shape, compiler_params=_params(("parallel",)))(parts, w, m, v)
    g_full, axis, me = own

    def own_map(i, me_ref):
        return (me_ref[0] * blocks + i, 0) if axis == 0 else (i, me_ref[0])

    spec = pl.BlockSpec((tr, cols), lambda i, me_ref: (i, 0))
    return pl.pallas_call(
        body, name=name, out_shape=out_shape,
        grid_spec=pltpu.PrefetchScalarGridSpec(
            num_scalar_prefetch=1, grid=(blocks,),
            in_specs=[pl.BlockSpec((tr, cols), own_map),
                      pl.BlockSpec((n_parts, tr, cols), lambda i, me_ref: (0, i, 0)), spec, spec, spec],
            out_specs=[spec] * 4),
        compiler_params=_params(("parallel",)))(me.reshape(1), g_full, parts, w, m, v)


def _rope_tables(pos, dk):
    half = dk // 2
    inv_freq = 1.0 / (ROPE_BASE ** jnp.linspace(0.0, 1.0, half, dtype=F32))
    ang = pos[:, None] * inv_freq[None, :]
    return jnp.cos(ang), jnp.sin(ang)


def _pad_lanes(v, width):
    return jnp.pad(v, ((0, 0), (0, width - v.shape[1])))


def kernel(x, c, ctx, c_ctx, w_mod, b_mod, norm1_g, w_in, conv_w, w_conv_out, ret_decay_fwd, ret_decay_bwd, w_ret_out, w_o, norm2_g, w_ff1, w_ff2, final_g, loss_target, m_c_ctx, m_w_mod, m_b_mod, m_norm1_g, m_w_in, m_conv_w, m_w_conv_out, m_ret_decay_fwd, m_ret_decay_bwd, m_w_ret_out, m_w_o, m_norm2_g, m_w_ff1, m_w_ff2, m_final_g, v_c_ctx, v_w_mod, v_b_mod, v_norm1_g, v_w_in, v_conv_w, v_w_conv_out, v_ret_decay_fwd, v_ret_decay_bwd, v_w_ret_out, v_w_o, v_norm2_g, v_w_ff1, v_w_ff2, v_final_g):
    n_lat, d = x.shape[1], x.shape[2]
    n_ctx = ctx.shape[1]
    assert n_ctx % CHUNK == 0 and n_lat % CHUNK == 0 and n_lat % GRID_W == 0
    dk = d // HEADS
    nb = w_mod.shape[2]
    me = 4 * lax.axis_index("x") + 2 * lax.axis_index("y") + lax.axis_index("c")
    xl, ctxl, target = x[0], ctx[0], loss_target[0]

    first = _small_allgather(jnp.concatenate([c, _pad_lanes(conv_w[0], d), jnp.zeros((4, d), F32)], axis=0), "ag_c")
    conv_w_f = first[:, 1:4, :d // N_DEV].transpose(1, 0, 2).reshape(3, d)

    c_all = first[:, 0, :]
    cs = jnp.concatenate([c_all, c_ctx[None], jnp.zeros((7, d), F32)], axis=0)
    dec = jnp.pad(jnp.concatenate([ret_decay_fwd, ret_decay_bwd], axis=0), ((0, 6), (0, 128 - HEADS)))
    b_loc = lax.dynamic_slice(b_mod, (0, me * nb), (1, nb))
    modp, lgs = _mod_proj(cs, w_mod[0], b_loc, dec, "mod_proj")
    modp_all = _small_allgather(modp, "ag_mod")
    mod_l = lax.dynamic_index_in_dim(modp_all, me, axis=1, keepdims=False).reshape(N_MOD, d)
    mod_c = modp_all[:, 8, :].reshape(N_MOD, d)
    lgs = lgs[0:2, :HEADS]
    zero_row = jnp.zeros((1, d), F32)
    vec1_l = jnp.concatenate([norm1_g, mod_l[0:1], mod_l[1:2], zero_row], axis=0)
    vec1_c = jnp.concatenate([norm1_g, mod_c[0:1], mod_c[1:2], zero_row], axis=0)
    vec2 = jnp.concatenate([norm2_g, mod_l[3:4], mod_l[4:5], mod_l[2:3]], axis=0)
    vec3 = jnp.concatenate([mod_l[5:6], final_g[None]], axis=0)

    a_all = _modulate(ctxl, vec1_c, "modulate1_ctx", n_lat + n_ctx,
                      into=_modulate(xl, vec1_l, "modulate1", n_lat + n_ctx))
    arrival = me ^ jnp.array([0, 1, 4, 5, 2, 3, 6, 7], jnp.int32)
    riders = [(_place_shard(w[0], 0, me, "place_" + wname), 0)
              for wname, w in (("w_conv_out", w_conv_out), ("w_ret_out", w_ret_out), ("w_o", w_o))]
    w_in_f, w_conv_out_f, w_ret_out_f, w_o_f, z = _gather_project(
        a_all, _place_shard(w_in[0], 1, me, "place_w_in"), arrival, "proj_in", riders=riders)

    later, after = {}, (z, conv_w_f)
    for wname, w, axis in (("w_ff1", w_ff1, 1), ("w_ff2", w_ff2, 0)):
        later[wname], token = _exchange_start(_place_shard(w[0], axis, me, "place_" + wname), axis, True,
                                              "ag_" + wname + "_start", after=after)
        after = (token,)
    pc = _conv_fwd(z, conv_w_f, n_lat, "conv_fwd", token)
    pos = jnp.concatenate([n_ctx + jnp.arange(n_lat, dtype=F32), jnp.arange(n_ctx, dtype=F32)])
    cos, sin = _rope_tables(pos, dk)
    q_s, k_s, v_s = _rotary_fwd(z, cos, sin, n_lat, "rotary_fwd")
    o, st_f, st_b = _ret_fwd(q_s, k_s, v_s, lgs, n_lat, "ret_fwd")
    r = _gn_fwd(o, z, "gn_fwd")
    y_conv = _mm(pc, w_conv_out_f, "nn", [F32], "proj_conv_out", tn=2048)
    tn_d = _tile(d, 1024, 128)
    gate_offs = (CGC * d // tn_d, CGR * d // tn_d)

    def merge(acc, yc, gc, gr):
        return acc, _sig(gc) * yc + _sig(gr) * acc

    y_ret, mg = _mm(r, w_ret_out_f, "nn", [F32, BF16], "proj_ret_out", tm=512, tn=tn_d, tk=2 * d,
                    extras=[(y_conv, 0), (z, gate_offs[0]), (z, gate_offs[1])], epilogue=merge)
    y_l = _mm(mg, w_o_f, "nn", [F32], "proj_o", tn=2048)
    x1, a2 = _resid_modulate(xl, y_l, vec2, "resid_modulate2")

    def sqrelu(acc):
        return acc, jnp.square(jnp.maximum(acc, 0.0))

    w_ff1_f = _exchange_wait(later["w_ff1"], a2, "ag_w_ff1_wait")
    hff, s = _mm(a2, w_ff1_f, "nn", [BF16, BF16], "ff1", tn=2048, epilogue=sqrelu)
    w_ff2_f = _exchange_wait(later["w_ff2"], s, "ag_w_ff2_wait")
    f = _mm(s, w_ff2_f, "nn", [F32], "ff2")
    dx2, df, acc3 = _loss_head(x1, f, target, vec3, "loss_head")

    def d_sqrelu(acc, h):
        return (acc * (2.0 * jnp.maximum(h.astype(F32), 0.0)),)

    dh = _mm(df, w_ff2_f, "nt", [BF16], "ff2_dx", tn=2048, extras=[(hff, 0)], epilogue=d_sqrelu)
    sent = {}
    sent["w_ff2"], token = _exchange_start(_mm(s, df, "tn", [BF16], "ff2_dw", tn=2048), 0, False, "rs_w_ff2_start")
    da2 = _mm(dh, w_ff1_f, "nt", [F32], "ff1_dx", dep=token)
    sent["w_ff1"], token = _exchange_start(_mm(a2, dh, "tn", [BF16], "ff1_dw", tn=2048), 1, False, "rs_w_ff1_start")
    dx1, dyl, acc2 = _modulate_bwd(da2, x1, vec2, "modulate2_bwd", dx_in=dx2, y=y_l)

    def d_merge(acc, yc, yr, gc, gr):
        sc, sr = _sig(gc), _sig(gr)
        return acc * sc, acc * sr, acc * yc * (sc * (1.0 - sc)), acc * yr * (sr * (1.0 - sr))

    dyc, dyr, dgc, dgr = _mm(dyl, w_o_f, "nt", [BF16] * 4, "proj_o_dx", tm=512, tn=tn_d,
                             extras=[(y_conv, 0), (y_ret, 0), (z, gate_offs[0]), (z, gate_offs[1])], epilogue=d_merge,
                             dep=token)
    sent["w_o"], token = _exchange_start(_mm(mg, dyl, "tn", [BF16], "proj_o_dw"), 0, False, "rs_w_o_start")
    dpc = _mm(dyc, w_conv_out_f, "nt", [F32], "proj_conv_out_dx", tn=2048, dep=token)
    sent["w_conv_out"], token = _exchange_start(_mm(pc, dyc, "tn", [BF16], "proj_conv_out_dw"), 0, False,
                                                "rs_w_conv_out_start")
    dz_conv, acc_conv = _conv_bwd(dpc, z, conv_w_f, "conv_bwd")
    dv_head = 2 * d // HEADS
    do, dz_g = _mm(dyr, w_ret_out_f, "nt", [BF16, BF16], "proj_ret_out_dx", tn=dv_head,
                   extras=[(o, 0), (z, CG * d // dv_head)], epilogue=_gn_bwd_head, dep=token)
    sent["w_ret_out"], token = _exchange_start(_mm(r, dyr, "tn", [BF16], "proj_ret_out_dw"), 0, False,
                                               "rs_w_ret_out_start")
    dq_f, dk_f, dv_f, dlg_f, dq_b, dk_b, dv_b, dlg_b = _ret_bwd(q_s, k_s, v_s, do, st_f, st_b, lgs, "ret_bwd")
    dz = _dz_assemble(dq_f, dq_b, dk_f, dk_b, dv_f, dv_b, cos, sin, dz_conv, dz_g, dgc, dgr, n_lat, n_ctx,
                      "dz_assemble")
    g_in = _mm(a_all, dz, "tn", [BF16], "proj_in_dw", tn=2048, tk=(n_lat + n_ctx) // 4, dep=token)
    sent["w_in"], token = _exchange_start(g_in, 1, False, "rs_w_in_start")
    da_all = _mm(dz, w_in_f, "nt", [F32], "proj_in_dx", tn=2048, dep=token)
    grad_x, acc1 = _modulate_bwd(da_all, xl, vec1_l, "modulate1_bwd", dx_in=dx1)
    _, acc1c = _modulate_bwd(da_all, ctxl, vec1_c, "modulate1_ctx_bwd", da_off=n_lat)

    lane_pad = functools.partial(_pad_lanes, width=d)
    packet = jnp.concatenate([
        acc1[2:3] + acc1c[2:3], acc2[2:3], acc3[1:2],
        acc1[0:1], acc1[1:2], acc2[3:4], acc2[0:1], acc2[1:2], acc3[0:1],
        acc1c[0:1], acc1c[1:2],
        acc_conv[0:3],
        lane_pad(dlg_f[:, 0, 0][None]), lane_pad(dlg_b[:, 0, 0][None]),
        acc3[2:3],
        jnp.zeros((7, d), F32)], axis=0)
    n_rows = packet.shape[0]
    packets = lax.dynamic_update_slice(jnp.zeros((N_DEV * n_rows, d), F32), packet, (me * n_rows, 0))
    packets_sent, after = _exchange_start(packets, 0, True, "ag_small_start")
    res = {}

    def update(wname, axis, w, m, v, after):
        g_full, parts = _exchange_wait(sent[wname], after, "rs_" + wname + "_wait")
        res[wname] = _adamw(parts, w[0], m[0], v[0], "adamw_" + wname, own=(g_full, axis, me))
        return res[wname][0]

    for wname, axis, w, m, v in (("w_ff2", 0, w_ff2, m_w_ff2, v_w_ff2), ("w_ff1", 1, w_ff1, m_w_ff1, v_w_ff1),
                                 ("w_o", 0, w_o, m_w_o, v_w_o), ("w_conv_out", 0, w_conv_out, m_w_conv_out, v_w_conv_out),
                                 ("w_ret_out", 0, w_ret_out, m_w_ret_out, v_w_ret_out)):
        after = update(wname, axis, w, m, v, after)
    gathered = _exchange_wait(packets_sent, after, "ag_small_wait").reshape(N_DEV, n_rows, d)
    red, misc = _reduce_small(gathered, dec, d, "reduce_small")
    dmod_ctx = jnp.concatenate([red[9], red[10], jnp.zeros((4 * d,), F32)])
    dmod_all = jnp.concatenate([gathered[:, 3:9, :].reshape(N_DEV, N_MOD * d), dmod_ctx[None]], axis=0)
    dm = jnp.pad(lax.dynamic_slice(dmod_all, (0, me * nb), (N_DEV + 1, nb)), ((0, 7), (0, 0)))
    g_mod, c_part = _mod_grad(cs, dm, w_mod[0], "mod_grad")
    g_b_mod = red[3:9].reshape(1, N_MOD * d) + dmod_ctx[None]
    g_conv_w = lax.dynamic_slice(red[11:14], (0, me * (d // N_DEV)), (3, d // N_DEV))

    c_parts = lax.dynamic_update_slice(jnp.zeros((N_DEV * 16, d), F32), c_part, (me * 16, 0))
    c_parts_sent, token = _exchange_start(c_parts, 0, True, "ag_c_ctx_start")
    res["w_mod"] = _adamw(g_mod[None], w_mod[0], m_w_mod[0], v_w_mod[0], "adamw_w_mod")
    after = update("w_in", 1, w_in, m_w_in, v_w_in, token)
    c_parts = _exchange_wait(c_parts_sent, after, "ag_c_ctx_wait").reshape(N_DEV, 16, d)
    g_c_ctx = _c_ctx_grad(c_parts, c_ctx[None], "c_ctx_grad")
    res = {k: tuple(t[None] for t in val) for k, val in res.items()}

    small = [("c_ctx", g_c_ctx, c_ctx, m_c_ctx, v_c_ctx), ("b_mod", g_b_mod, b_mod, m_b_mod, v_b_mod),
             ("norm1_g", red[0:1], norm1_g, m_norm1_g, v_norm1_g), ("conv_w", g_conv_w, conv_w, m_conv_w, v_conv_w),
             ("ret_decay_fwd", misc[1:2, :HEADS], ret_decay_fwd, m_ret_decay_fwd, v_ret_decay_fwd),
             ("ret_decay_bwd", misc[2:3, :HEADS], ret_decay_bwd, m_ret_decay_bwd, v_ret_decay_bwd),
             ("norm2_g", red[1:2], norm2_g, m_norm2_g, v_norm2_g), ("final_g", red[2:3], final_g, m_final_g, v_final_g)]

    def flat(t):
        t = t.reshape(-1)
        return jnp.pad(t, (0, (-t.shape[0]) % 1024))

    packed = [jnp.concatenate([flat(item[j]) for item in small]).reshape(-1, 128) for j in range(1, 5)]
    outs = _adamw(packed[0][None], packed[1], packed[2], packed[3], "adamw_small")
    start = 0
    for name, _, w, _, _ in small:
        size = w.size
        res[name] = tuple(o.reshape(-1)[start:start + size].reshape(w.shape) for o in outs)
        start += size + (-size) % 1024

    order = ["c_ctx", "w_mod", "b_mod", "norm1_g", "w_in", "conv_w", "w_conv_out", "ret_decay_fwd", "ret_decay_bwd",
             "w_ret_out", "w_o", "norm2_g", "w_ff1", "w_ff2", "final_g"]
    loss = misc[0, 0]
    return (loss, grad_x[None], *[res[n][0] for n in order], *[res[n][1] for n in order],
            *[res[n][2] for n in order], *[res[n][3] for n in order])
```

```python
import functools

import jax
import jax.numpy as jnp
from jax import lax
from jax.experimental import pallas as pl
from jax.experimental.pallas import tpu as pltpu

F32 = jnp.float32
BF16 = jnp.bfloat16
MESH = pl.DeviceIdType.MESH

N_DEV = 8
HEADS = 8
N_MOD = 6
N_IN = 11
GRID_W = 64
CHUNK = 256
ROPE_BASE = 10000.0
EPS = 1e-6
ADAM_LR, ADAM_B1, ADAM_B2, ADAM_EPS, ADAM_WD, ADAM_STEP = 0.001, 0.9, 0.999, 1e-08, 0.01, 10
VMEM_LIMIT = 56 * 1024 * 1024
HIGHEST = lax.Precision.HIGHEST
CB, CC, CX, CQ, CK, CV, CG, CGC, CGR = 0, 1, 2, 3, 4, 5, 7, 9, 10


def _tile(n, target, mult):
    t = (min(target, n) // mult) * mult
    while t >= mult:
        if n % t == 0:
            return t
        t -= mult
    return n


def _params(sem=None):
    return pltpu.CompilerParams(dimension_semantics=sem, vmem_limit_bytes=VMEM_LIMIT)


def _sig(v):
    return 1.0 / (1.0 + jnp.exp(-v))


def _coords():
    return lax.axis_index("x"), lax.axis_index("y"), lax.axis_index("c")


def _flip(p, m):
    return tuple(1 - v if (m >> s) & 1 else v for v, s in zip(p, (2, 1, 0)))


def _index(p):
    return 4 * p[0] + 2 * p[1] + p[2]


def _small_allgather(x, name):
    r, n = x.shape

    def body(x_ref, out_ref, send_sems, recv_sems):
        me = _coords()
        out_ref[pl.ds(_index(me), 1)] = x_ref[...][None]
        sent = []
        for m in range(1, N_DEV):
            cp = pltpu.make_async_remote_copy(
                src_ref=x_ref, dst_ref=out_ref.at[_index(me)], send_sem=send_sems.at[m - 1],
                recv_sem=recv_sems.at[m - 1], device_id=_flip(me, m), device_id_type=MESH)
            cp.start()
            sent.append(cp)
        for m in range(1, N_DEV):
            pltpu.make_async_remote_copy(
                src_ref=x_ref, dst_ref=out_ref.at[_index(_flip(me, m))], send_sem=send_sems.at[m - 1],
                recv_sem=recv_sems.at[m - 1], device_id=_flip(me, m), device_id_type=MESH).wait_recv()
        for cp in sent:
            cp.wait_send()

    return pl.pallas_call(
        body, name=name, out_shape=jax.ShapeDtypeStruct((N_DEV, r, n), x.dtype),
        in_specs=[pl.BlockSpec(memory_space=pltpu.VMEM)], out_specs=pl.BlockSpec(memory_space=pltpu.VMEM),
        scratch_shapes=[pltpu.SemaphoreType.DMA((N_DEV - 1,)), pltpu.SemaphoreType.DMA((N_DEV - 1,))],
    )(x)


def _window(ref, j, r, c, axis):
    if axis == 0:
        return ref.at[pl.ds(pl.multiple_of(j * r, 8), r), :]
    return ref.at[:, pl.ds(pl.multiple_of(j * c, 128), c)]


_HBM = pl.BlockSpec(memory_space=pltpu.HBM)
_SEM = pl.BlockSpec(memory_space=pltpu.SEMAPHORE)
_ANY = pl.BlockSpec(memory_space=pl.ANY)
_EFFECT = pltpu.SideEffectType.DATAFLOW_SIDE_EFFECTING


def _place_shard(w, axis, me, name):
    r, c = w.shape
    full = (N_DEV * r, c) if axis == 0 else (r, N_DEV * c)
    tr = _tile(r, max(16, (2 * 1024 * 1024) // (c * 4)), 16)
    blocks = r // tr

    def out_map(i, me_ref):
        return (me_ref[0] * blocks + i, 0) if axis == 0 else (i, me_ref[0])

    def body(me_ref, w_ref, out_ref):
        out_ref[...] = w_ref[...].astype(BF16)

    return pl.pallas_call(
        body, name=name, out_shape=jax.ShapeDtypeStruct(full, BF16),
        grid_spec=pltpu.PrefetchScalarGridSpec(
            num_scalar_prefetch=1, grid=(blocks,), in_specs=[pl.BlockSpec((tr, c), lambda i, me_ref: (i, 0))],
            out_specs=pl.BlockSpec((tr, c), out_map)),
        compiler_params=_params(("parallel",)))(me.reshape(1), w)


def _exchange_start(src, axis, gather, name, after=()):
    r, c = (src.shape[0] // N_DEV, src.shape[1]) if axis == 0 else (src.shape[0], src.shape[1] // N_DEV)
    n_hbm = 1 if gather else 2
    n_after = len(after)

    def body(*refs):
        src_ref, land_ref = refs[0], refs[n_hbm - 1]
        send_sems, recv_sems = refs[n_hbm + n_after:n_hbm + n_after + 2]
        token = refs[-1]
        me = _coords()
        for m in range(1, N_DEV):
            peer = _flip(me, m)
            if gather:
                mine = theirs = _window(land_ref, _index(me), r, c, axis)
            else:
                mine, theirs = _window(src_ref, _index(peer), r, c, axis), land_ref.at[m - 1]
            pltpu.make_async_remote_copy(
                src_ref=mine, dst_ref=theirs, send_sem=send_sems.at[m - 1], recv_sem=recv_sems.at[m - 1],
                device_id=peer, device_id_type=MESH).start()
        token[...] = jnp.zeros_like(token)

    hbm = [pltpu.with_memory_space_constraint(src, pltpu.HBM)]
    if not gather:
        hbm.append(pltpu.with_memory_space_constraint(lax.empty((N_DEV - 1, r, c), src.dtype), pltpu.HBM))
    outs = pl.pallas_call(
        body, name=name,
        out_shape=(pltpu.SemaphoreType.DMA((N_DEV - 1,)), pltpu.SemaphoreType.DMA((N_DEV - 1,)),
                   *[pltpu.HBM(t.shape, t.dtype) for t in hbm], jax.ShapeDtypeStruct((8, 128), F32)),
        in_specs=[_HBM] * n_hbm + [_ANY] * n_after,
        out_specs=(_SEM, _SEM, *[_HBM] * n_hbm, pl.BlockSpec(memory_space=pltpu.VMEM)),
        input_output_aliases={i: 2 + i for i in range(n_hbm)},
        compiler_params=pltpu.CompilerParams(has_side_effects=_EFFECT),
    )(*hbm, *after)
    return (outs[:2], outs[2:2 + n_hbm], (axis, gather, r, c)), outs[-1]


def _exchange_wait(handle, after, name):
    (send_sems, recv_sems), hbm, (axis, gather, r, c) = handle
    n_hbm = len(hbm)

    def body(*refs):
        src_ref, land_ref = refs[0], refs[n_hbm - 1]
        send_sems, recv_sems = refs[n_hbm:n_hbm + 2]
        me = _coords()
        for m in range(1, N_DEV):
            peer = _flip(me, m)
            if gather:
                mine, theirs = _window(land_ref, _index(me), r, c, axis), _window(land_ref, _index(peer), r, c, axis)
            else:
                mine, theirs = _window(src_ref, _index(peer), r, c, axis), land_ref.at[m - 1]
            copy = pltpu.make_async_remote_copy(
                src_ref=mine, dst_ref=theirs, send_sem=send_sems.at[m - 1], recv_sem=recv_sems.at[m - 1],
                device_id=peer, device_id_type=MESH)
            copy.wait_send()
            copy.wait_recv()

    outs = pl.pallas_call(
        body, name=name, out_shape=tuple(pltpu.HBM(t.shape, t.dtype) for t in hbm),
        in_specs=[_HBM] * n_hbm + [_SEM, _SEM, _ANY], out_specs=tuple([_HBM] * n_hbm),
        input_output_aliases={i: i for i in range(n_hbm)},
        compiler_params=pltpu.CompilerParams(has_side_effects=_EFFECT),
    )(*hbm, send_sems, recv_sems, after)
    return outs[0] if gather else tuple(outs)


def _mm(a, b, mode, out_dtypes, name, tm=1024, tn=1024, tk=2048, extras=(), epilogue=None, dep=None):
    if mode == "nn":
        (m, k), n = a.shape, b.shape[1]
    elif mode == "nt":
        (m, k), n = a.shape, b.shape[0]
    else:
        (k, m), n = a.shape, b.shape[1]
    tm, tn = _tile(m, tm, 8), _tile(n, tn, min(128, tn))
    tk = _tile(k, tk, 16 if mode == "tn" else 128)
    nk = k // tk
    swap = nk == 1 and (k * n + (n // tn) * m * k) < (m * k + (m // tm) * k * n)

    def ij(p, q):
        return (q, p) if swap else (p, q)

    def spec(shape, fn):
        return pl.BlockSpec(shape, lambda p, q, kk: fn(*ij(p, q), kk))

    a_spec = spec((tk, tm), lambda i, j, kk: (kk, i)) if mode == "tn" else spec((tm, tk), lambda i, j, kk: (i, kk))
    b_spec = spec((tn, tk), lambda i, j, kk: (j, kk)) if mode == "nt" else spec((tk, tn), lambda i, j, kk: (kk, j))
    dims = {"nn": (((1,), (0,)), ((), ())), "nt": (((1,), (1,)), ((), ())), "tn": (((0,), (0,)), ((), ()))}[mode]
    ex_specs = [spec((tm, tn), functools.partial(lambda i, j, kk, off: (i, j + off), off=off)) for _, off in extras]
    deps = [] if dep is None else [dep]
    dep_specs = [pl.BlockSpec(dep.shape, lambda p, q, kk: (0, 0))] if deps else []
    n_ex, n_out = len(extras), len(out_dtypes)
    n_in = 2 + n_ex + len(deps)

    def body(*refs):
        a_ref, b_ref = refs[0], refs[1]
        ex_refs = refs[2:2 + n_ex]
        out_refs = refs[n_in:n_in + n_out]

        def product():
            return lax.dot_general(a_ref[...], b_ref[...], dims, preferred_element_type=F32)

        def finish(res):
            res = epilogue(res, *[e[...] for e in ex_refs]) if epilogue is not None else (res,)
            for o_ref, val in zip(out_refs, res):
                o_ref[...] = val.astype(o_ref.dtype)

        if nk == 1:
            finish(product())
            return
        acc = refs[-1]
        kk = pl.program_id(2)

        @pl.when(kk == 0)
        def _():
            acc[...] = product()

        @pl.when((kk > 0) & (kk < nk - 1))
        def _():
            acc[...] += product()

        @pl.when(kk == nk - 1)
        def _():
            finish(acc[...] + product())

    outs = pl.pallas_call(
        body, name=name, grid=(*ij(m // tm, n // tn), nk),
        in_specs=[a_spec, b_spec] + ex_specs + dep_specs,
        out_specs=[spec((tm, tn), lambda i, j, kk: (i, j)) for _ in out_dtypes],
        out_shape=[jax.ShapeDtypeStruct((m, n), dt) for dt in out_dtypes],
        scratch_shapes=[pltpu.VMEM((tm, tn), F32)] if nk > 1 else [],
        compiler_params=_params(("parallel", "parallel", "arbitrary")),
    )(a, b, *[e for e, _ in extras], *deps)
    return outs if n_out > 1 else outs[0]


def _gather_project(a, w_full, order, name, riders=(), tm=768):
    m, k = a.shape
    c = w_full.shape[1] // N_DEV
    tm = _tile(m, tm, 8)
    n_i = m // tm
    chips = [4, 2, 6]
    n_rid = len(riders)
    shard = [(k, c, 1)] + [((t.shape[0] // N_DEV, t.shape[1], 0) if ax == 0 else (t.shape[0], t.shape[1] // N_DEV, 1))
                           for t, ax in riders]

    def body(order_ref, a_ref, *refs):
        outs = refs[1 + n_rid:2 + 2 * n_rid]
        z_ref, wbuf, send_sems, recv_sems, load_sems = refs[2 + 2 * n_rid:]
        w_ref = outs[0]
        p, i = pl.program_id(0), pl.program_id(1)
        me = _coords()
        sibling = _flip(me, 1)

        def copy(t, s, block, to):
            r_t, c_t, ax_t = shard[t]
            win = _window(outs[t], _index(block), r_t, c_t, ax_t)
            return pltpu.make_async_remote_copy(src_ref=win, dst_ref=win, send_sem=send_sems.at[7 * t + s],
                                                recv_sem=recv_sems.at[7 * t + s], device_id=to, device_id_type=MESH)

        def first(t):
            return [copy(t, 0, me, sibling)] + [copy(t, 1 + j, me, _flip(me, mm)) for j, mm in enumerate(chips)]

        def passed(t):
            return [copy(t, 4 + j, _flip(me, mm), sibling) for j, mm in enumerate(chips)]

        @pl.when((p == 0) & (i == 0))
        def _():
            for t in range(1 + n_rid):
                for cp in first(t):
                    cp.start()

        def load(piece):
            return pltpu.make_async_copy(_window(w_ref, order_ref[piece], k, c, 1), wbuf.at[piece % 2],
                                         load_sems.at[piece % 2])

        @pl.when((p == 0) & (i == 0))
        def _():
            load(p).start()

        arrivals = [(0, None)] + [x for j in range(3) for x in ((1 + j, passed(0)[j]), (4 + j, None))]
        for piece, (sem, forward) in enumerate(arrivals, start=1):
            @pl.when((p == piece - 1) & (i == n_i - 1))
            def _(sem=sem, forward=forward, piece=piece):
                src_block = (sibling if piece == 1 else
                             _flip(me if piece % 2 == 0 else sibling, chips[(piece - 2) // 2]))
                copy(0, sem, src_block, me).wait_recv()
                if forward is not None:
                    forward.start()
                load(p + 1).start()

        @pl.when(i == 0)
        def _():
            load(p).wait()

        @pl.when((p == N_DEV - 1) & (i == 0))
        def _():
            for t in range(1, 1 + n_rid):
                for j, mm in enumerate(chips):
                    copy(t, 1 + j, _flip(me, mm), me).wait_recv()
                    passed(t)[j].start()

        z_ref[...] = jnp.dot(a_ref[...], wbuf[p % 2], preferred_element_type=F32)

        @pl.when((p == N_DEV - 1) & (i == n_i - 1))
        def _():
            for t in range(1, 1 + n_rid):
                copy(t, 0, sibling, me).wait_recv()
                for j, mm in enumerate(chips):
                    copy(t, 4 + j, _flip(sibling, mm), me).wait_recv()
            for t in range(1 + n_rid):
                for cp in first(t) + passed(t):
                    cp.wait_send()

    gathered = [w_full] + [t for t, _ in riders]
    n_sem = 7 * len(gathered)
    return pl.pallas_call(
        body, name=name,
        out_shape=[jax.ShapeDtypeStruct(t.shape, t.dtype) for t in gathered]
        + [jax.ShapeDtypeStruct((m, N_DEV * c), F32)],
        grid_spec=pltpu.PrefetchScalarGridSpec(
            num_scalar_prefetch=1, grid=(N_DEV, n_i),
            in_specs=[pl.BlockSpec((tm, k), lambda p, i, order_ref: (i, 0))]
            + [pl.BlockSpec(memory_space=pl.ANY)] * len(gathered),
            out_specs=[pl.BlockSpec(memory_space=pl.ANY)] * len(gathered)
            + [pl.BlockSpec((tm, c), lambda p, i, order_ref: (i, order_ref[p]))],
            scratch_shapes=[pltpu.VMEM((2, k, c), w_full.dtype), pltpu.SemaphoreType.DMA((n_sem,)),
                            pltpu.SemaphoreType.DMA((n_sem,)), pltpu.SemaphoreType.DMA((2,))]),
        input_output_aliases={2 + t: t for t in range(len(gathered))},
        compiler_params=_params(("arbitrary", "arbitrary")))(order, a, *gathered)


def _rspec(tr, w, cb=0, off=0):
    return pl.BlockSpec((tr, w), lambda i: (i + off, cb))


def _cspec(shape):
    return pl.BlockSpec(shape, lambda i: (0,) * len(shape))


def _rms(xf):
    rstd = lax.rsqrt(jnp.mean(xf * xf, axis=-1, keepdims=True) + EPS)
    return xf * rstd, rstd


def _rms_bwd(dn, n, rstd):
    return rstd * (dn - n * jnp.mean(dn * n, axis=-1, keepdims=True))


def _colsum(v):
    return jnp.sum(v, axis=0, keepdims=True)


def _total(v):
    return jnp.sum(jnp.sum(v, axis=1, keepdims=True), axis=0, keepdims=True)


def _modulate(x, vec, name, total_rows, into=None):
    rows, d = x.shape
    tr = _tile(rows, 256, 8)
    off = 0 if into is None else (total_rows - rows) // tr

    def body(x_ref, v_ref, *refs):
        n, _ = _rms(x_ref[...])
        refs[-1][...] = ((n * v_ref[0:1, :]) * (1.0 + v_ref[2:3, :]) + v_ref[1:2, :]).astype(BF16)

    return pl.pallas_call(
        body, name=name, grid=(rows // tr,),
        in_specs=[_rspec(tr, d), _cspec(vec.shape)] + ([] if into is None else [_ANY]),
        out_specs=_rspec(tr, d, off=off), out_shape=jax.ShapeDtypeStruct((total_rows, d), BF16),
        input_output_aliases={} if into is None else {2: 0},
        compiler_params=_params(("parallel",)))(x, vec, *([] if into is None else [into]))


def _resid_modulate(x, y, vec, name):
    rows, d = x.shape
    tr = _tile(rows, 256, 8)

    def body(x_ref, y_ref, v_ref, x1_ref, a_ref):
        x1 = x_ref[...] + v_ref[3:4, :] * y_ref[...]
        x1_ref[...] = x1
        n, _ = _rms(x1)
        a_ref[...] = ((n * v_ref[0:1, :]) * (1.0 + v_ref[2:3, :]) + v_ref[1:2, :]).astype(BF16)

    return pl.pallas_call(
        body, name=name, grid=(rows // tr,), in_specs=[_rspec(tr, d), _rspec(tr, d), _cspec(vec.shape)],
        out_specs=[_rspec(tr, d), _rspec(tr, d)],
        out_shape=[jax.ShapeDtypeStruct((rows, d), F32), jax.ShapeDtypeStruct((rows, d), BF16)],
        compiler_params=_params(("parallel",)))(x, y, vec)


def _loss_head(x1, f, target, vec, name):
    rows, d = x1.shape
    tr = _tile(rows, 256, 8)

    def body(x1_ref, f_ref, t_ref, v_ref, dx2_ref, df_ref, acc_ref):
        @pl.when(pl.program_id(0) == 0)
        def _():
            acc_ref[...] = jnp.zeros_like(acc_ref)

        gate, gain = v_ref[0:1, :], v_ref[1:2, :]
        fv = f_ref[...]
        n, rstd = _rms(x1_ref[...] + gate * fv)
        err = n * gain - t_ref[...]
        dy = err / d
        dx2 = _rms_bwd(dy * gain, n, rstd)
        dx2_ref[...] = dx2
        df_ref[...] = (dx2 * gate).astype(BF16)
        acc_ref[0:1, :] += _colsum(dx2 * fv)
        acc_ref[1:2, :] += _colsum(dy * n)
        acc_ref[2:3, :] += _colsum(err * err)

    return pl.pallas_call(
        body, name=name, grid=(rows // tr,),
        in_specs=[_rspec(tr, d), _rspec(tr, d), _rspec(tr, d), _cspec(vec.shape)],
        out_specs=[_rspec(tr, d), _rspec(tr, d), _cspec((8, d))],
        out_shape=[jax.ShapeDtypeStruct((rows, d), F32), jax.ShapeDtypeStruct((rows, d), BF16),
                   jax.ShapeDtypeStruct((8, d), F32)],
        compiler_params=_params(("arbitrary",)))(x1, f, target, vec)


def _modulate_bwd(da, x, vec, name, da_off=0, dx_in=None, y=None):
    rows, d = x.shape
    tr = _tile(rows, 256, 8)
    off = da_off // tr
    has_dx, has_y = dx_in is not None, y is not None

    def body(*refs):
        da_ref, x_ref, v_ref = refs[0], refs[1], refs[2]
        pos = 3
        dxin_ref = refs[pos] if has_dx else None
        pos += has_dx
        y_ref = refs[pos] if has_y else None
        pos += has_y
        dx_ref = refs[pos]
        dy_ref = refs[pos + 1] if has_y else None
        acc_ref = refs[-1]

        @pl.when(pl.program_id(0) == 0)
        def _():
            acc_ref[...] = jnp.zeros_like(acc_ref)

        gain, scale1 = v_ref[0:1, :], 1.0 + v_ref[2:3, :]
        dav = da_ref[...]
        n, rstd = _rms(x_ref[...])
        dx = _rms_bwd(dav * gain * scale1, n, rstd)
        if has_dx:
            dx = dx + dxin_ref[...]
        dx_ref[...] = dx
        acc_ref[0:1, :] += _colsum(dav)
        acc_ref[1:2, :] += _colsum(dav * (n * gain))
        acc_ref[2:3, :] += _colsum(dav * n * scale1)
        if has_y:
            acc_ref[3:4, :] += _colsum(dx * y_ref[...])
            dy_ref[...] = (dx * v_ref[3:4, :]).astype(BF16)

    ins = [da, x, vec] + ([dx_in] if has_dx else []) + ([y] if has_y else [])
    in_specs = [_rspec(tr, d, off=off), _rspec(tr, d), _cspec(vec.shape)] + [_rspec(tr, d)] * (has_dx + has_y)
    out_specs = [_rspec(tr, d)] + ([_rspec(tr, d)] if has_y else []) + [_cspec((8, d))]
    out_shape = ([jax.ShapeDtypeStruct((rows, d), F32)] + ([jax.ShapeDtypeStruct((rows, d), BF16)] if has_y else [])
                 + [jax.ShapeDtypeStruct((8, d), F32)])
    return pl.pallas_call(
        body, name=name, grid=(rows // tr,), in_specs=in_specs, out_specs=out_specs, out_shape=out_shape,
        compiler_params=_params(("arbitrary",)))(*ins)


def _conv_terms(cc, cx, w_ref, tr):
    t = lax.broadcasted_iota(jnp.int32, (tr, 1), 0) % GRID_W
    first, last = t == 0, t == GRID_W - 1
    u = cc * cx
    prev = jnp.where(first, 0.0, pltpu.roll(u, 1, 0))
    nxt = jnp.where(last, 0.0, pltpu.roll(u, tr - 1, 0))
    y = w_ref[0:1, :] * prev + w_ref[1:2, :] * u + w_ref[2:3, :] * nxt
    return u, prev, nxt, y, first, last


def _conv_fwd(z, conv_w, rows, name, dep):
    d = conv_w.shape[1]
    tr = _tile(rows, 256, GRID_W)

    def body(cb_ref, cc_ref, cx_ref, w_ref, dep_ref, out_ref):
        y = _conv_terms(cc_ref[...], cx_ref[...], w_ref, tr)[3]
        out_ref[...] = (cb_ref[...] * y).astype(BF16)

    return pl.pallas_call(
        body, name=name, grid=(rows // tr,),
        in_specs=[_rspec(tr, d, CB), _rspec(tr, d, CC), _rspec(tr, d, CX), _cspec(conv_w.shape), _cspec(dep.shape)],
        out_specs=_rspec(tr, d), out_shape=jax.ShapeDtypeStruct((rows, d), BF16),
        compiler_params=_params(("parallel",)))(z, z, z, conv_w, dep)


def _conv_bwd(dpc, z, conv_w, name):
    rows, d = dpc.shape
    tr = _tile(rows, 256, GRID_W)

    def body(dpc_ref, cb_ref, cc_ref, cx_ref, w_ref, dz_ref, acc_ref):
        @pl.when(pl.program_id(0) == 0)
        def _():
            acc_ref[...] = jnp.zeros_like(acc_ref)

        cc, cx, dpcv = cc_ref[...], cx_ref[...], dpc_ref[...]
        u, prev, nxt, y, first, last = _conv_terms(cc, cx, w_ref, tr)
        dy = dpcv * cb_ref[...]
        dy_next = jnp.where(last, 0.0, pltpu.roll(dy, tr - 1, 0))
        dy_prev = jnp.where(first, 0.0, pltpu.roll(dy, 1, 0))
        du = w_ref[0:1, :] * dy_next + w_ref[1:2, :] * dy + w_ref[2:3, :] * dy_prev
        dz_ref[:, 0:d] = (dpcv * y).astype(BF16)
        dz_ref[:, d:2 * d] = (du * cx).astype(BF16)
        dz_ref[:, 2 * d:3 * d] = (du * cc).astype(BF16)
        acc_ref[0:1, :] += _colsum(dy * prev)
        acc_ref[1:2, :] += _colsum(dy * u)
        acc_ref[2:3, :] += _colsum(dy * nxt)

    return pl.pallas_call(
        body, name=name, grid=(rows // tr,),
        in_specs=[_rspec(tr, d), _rspec(tr, d, CB), _rspec(tr, d, CC), _rspec(tr, d, CX), _cspec(conv_w.shape)],
        out_specs=[_rspec(tr, 3 * d), _cspec((8, d))],
        out_shape=[jax.ShapeDtypeStruct((rows, 3 * d), BF16), jax.ShapeDtypeStruct((8, d), F32)],
        compiler_params=_params(("arbitrary",)))(dpc, z, z, z, conv_w)


def _rotary_fwd(z, cos, sin, n_lat, name):
    rows = z.shape[0]
    d = z.shape[1] // N_IN
    dk = d // HEADS
    half = dk // 2
    tr = _tile(n_lat, 256, 8)
    tr = _tile(rows - n_lat, tr, 8)
    lat_blocks, all_blocks = n_lat // tr, rows // tr
    ctx_blocks = all_blocks - lat_blocks
    scan_rows = rows + ctx_blocks * tr

    def z_block(g):
        return jnp.where(g < all_blocks, g, g - ctx_blocks)

    def in_spec(w, cb=0):
        return pl.BlockSpec((tr, w), lambda g: (z_block(g), cb))

    def out_spec(w):
        return pl.BlockSpec((tr, w), lambda g: (jnp.where(g < all_blocks, g + ctx_blocks, g - all_blocks), 0))

    def body(q_ref, k_ref, v0_ref, v1_ref, cos_ref, sin_ref, qo_ref, ko_ref, vo_ref):
        cs, sn = cos_ref[...], sin_ref[...]
        keep = jnp.where(z_block(pl.program_id(0)) < lat_blocks, 1.0, 0.0)
        for src, dst, scale in ((q_ref, qo_ref, keep), (k_ref, ko_ref, dk ** -0.5)):
            for h in range(HEADS):
                lo, mid, hi = h * dk, h * dk + half, (h + 1) * dk
                t1, t2 = src[:, lo:mid], src[:, mid:hi]
                dst[:, lo:mid] = ((t1 * cs - t2 * sn) * scale).astype(BF16)
                dst[:, mid:hi] = ((t1 * sn + t2 * cs) * scale).astype(BF16)
        vo_ref[:, 0:d] = v0_ref[...].astype(BF16)
        vo_ref[:, d:2 * d] = v1_ref[...].astype(BF16)

    return pl.pallas_call(
        body, name=name, grid=(all_blocks + ctx_blocks,),
        in_specs=[in_spec(d, CQ), in_spec(d, CK), in_spec(d, CV), in_spec(d, CV + 1), in_spec(half), in_spec(half)],
        out_specs=[out_spec(d), out_spec(d), out_spec(2 * d)],
        out_shape=[jax.ShapeDtypeStruct((scan_rows, d), BF16), jax.ShapeDtypeStruct((scan_rows, d), BF16),
                   jax.ShapeDtypeStruct((scan_rows, 2 * d), BF16)],
        compiler_params=_params(("parallel",)))(z, z, z, z, cos, sin)


def _dz_assemble(dq_f, dq_b, dk_f, dk_b, dv_f, dv_b, cos, sin, dz_conv, dz_g, dz_gc, dz_gr, n_lat, n_ctx, name):
    rows = n_lat + n_ctx
    d = dq_f.shape[1]
    dk = d // HEADS
    half = dk // 2
    tr = _tile(n_ctx, 128, 8)
    lat_blocks, ctx_blocks = n_lat // tr, n_ctx // tr

    def fmap(i):
        return (jnp.where(i < lat_blocks, i + ctx_blocks, i - lat_blocks), 0)

    def bmap(i):
        return (i + ctx_blocks, 0)

    def lmap(i):
        return (jnp.minimum(i, lat_blocks - 1), 0)

    def body(qf_ref, qb_ref, kf_ref, kb_ref, vf_ref, vb_ref, cos_ref, sin_ref, c_ref, g_ref, gc_ref, gr_ref, out_ref):
        cs, sn = cos_ref[...], sin_ref[...]
        is_lat = pl.program_id(0) < lat_blocks
        keep = jnp.where(is_lat, 1.0, 0.0)
        for fa, fb, base, scale in ((qf_ref, qb_ref, CQ * d, keep), (kf_ref, kb_ref, CK * d, dk ** -0.5)):
            for h in range(HEADS):
                lo, mid, hi = h * dk, h * dk + half, (h + 1) * dk
                d1 = fa[:, lo:mid].astype(F32) + fb[:, lo:mid].astype(F32)
                d2 = fa[:, mid:hi].astype(F32) + fb[:, mid:hi].astype(F32)
                out_ref[:, base + lo:base + mid] = ((d1 * cs + d2 * sn) * scale).astype(BF16)
                out_ref[:, base + mid:base + hi] = ((d2 * cs - d1 * sn) * scale).astype(BF16)
        out_ref[:, CV * d:CG * d] = (vf_ref[...].astype(F32) + vb_ref[...].astype(F32)).astype(BF16)
        for src, lo, hi in ((c_ref, CB * d, CQ * d), (g_ref, CG * d, CGC * d), (gc_ref, CGC * d, CGR * d),
                            (gr_ref, CGR * d, N_IN * d)):
            out_ref[:, lo:hi] = jnp.where(is_lat, src[...], jnp.zeros_like(src))

    return pl.pallas_call(
        body, name=name, grid=(rows // tr,),
        in_specs=[pl.BlockSpec((tr, d), fmap), pl.BlockSpec((tr, d), bmap), pl.BlockSpec((tr, d), fmap),
                  pl.BlockSpec((tr, d), bmap), pl.BlockSpec((tr, 2 * d), fmap), pl.BlockSpec((tr, 2 * d), bmap),
                  _rspec(tr, half), _rspec(tr, half), pl.BlockSpec((tr, 3 * d), lmap), pl.BlockSpec((tr, 2 * d), lmap),
                  pl.BlockSpec((tr, d), lmap), pl.BlockSpec((tr, d), lmap)],
        out_specs=_rspec(tr, N_IN * d), out_shape=jax.ShapeDtypeStruct((rows, N_IN * d), BF16),
        compiler_params=_params(("parallel",)))(dq_f, dq_b, dk_f, dk_b, dv_f, dv_b, cos, sin, dz_conv, dz_g, dz_gc, dz_gr)


def _gn_fwd(o, z, name):
    rows = o.shape[0]
    d = z.shape[1] // N_IN
    dv = 2 * d // HEADS
    tr = _tile(rows, 128, 8)

    def body(o_ref, g0_ref, g1_ref, r_ref):
        for h in range(HEADS):
            lo, hi = h * dv, (h + 1) * dv
            g_ref, glo = (g0_ref, lo) if hi <= d else (g1_ref, lo - d)
            o = o_ref[:, lo:hi]
            cen = o - jnp.mean(o, axis=-1, keepdims=True)
            on = cen * lax.rsqrt(jnp.mean(cen * cen, axis=-1, keepdims=True) + EPS)
            g = g_ref[:, glo:glo + dv]
            r_ref[:, lo:hi] = (g * _sig(g) * on).astype(BF16)

    return pl.pallas_call(
        body, name=name, grid=(rows // tr,),
        in_specs=[_rspec(tr, 2 * d), _rspec(tr, d, CG), _rspec(tr, d, CG + 1)],
        out_specs=_rspec(tr, 2 * d), out_shape=jax.ShapeDtypeStruct((rows, 2 * d), BF16),
        compiler_params=_params(("parallel",)))(o, z, z)


def _gn_bwd_head(dr, o, g):
    cen = o - jnp.mean(o, axis=-1, keepdims=True)
    rstd = lax.rsqrt(jnp.mean(cen * cen, axis=-1, keepdims=True) + EPS)
    on = cen * rstd
    sg = _sig(g)
    don = dr * (g * sg)
    do = rstd * (don - jnp.mean(don, axis=-1, keepdims=True) - on * jnp.mean(don * on, axis=-1, keepdims=True))
    return do, dr * on * (sg * (1.0 + g * (1.0 - sg)))


def _decays(lg, rev):
    row = lax.broadcasted_iota(jnp.int32, (CHUNK, CHUNK), 0)
    col = lax.broadcasted_iota(jnp.int32, (CHUNK, CHUNK), 1)
    rel = ((col - row) if rev else (row - col)).astype(F32)
    mask = jnp.where(rel >= 0, jnp.exp(lg * jnp.maximum(rel, 0.0)), 0.0)
    r = lax.broadcasted_iota(jnp.int32, (CHUNK, 1), 0)
    rr = ((CHUNK - 1 - r) if rev else r).astype(F32)
    chunk_decay = jnp.exp(lg * jnp.full((1, 1), float(CHUNK), F32))
    return (rr, rel), mask, jnp.exp(lg * (rr + 1.0)), jnp.exp(lg * (CHUNK - 1.0 - rr)), chunk_decay


_NT = (((1,), (1,)), ((), ()))
_TN = (((0,), (0,)), ((), ()))


def _dot(a, b, dims=None):
    if dims is None:
        return jnp.dot(a, b, preferred_element_type=F32)
    return lax.dot_general(a, b, dims, preferred_element_type=F32)


def _ret_fwd(q, k, v, lgs, n_lat, name):
    tt, d = q.shape
    dk, dv = d // HEADS, 2 * d // HEADS
    nc = tt // CHUNK
    lat_chunks = n_lat // CHUNK
    ctx_chunks = (nc - lat_chunks) // 2

    def chunk_of(rev, i):
        return (nc - 1 - i) if rev else i

    def cmap(rev):
        return lambda h, i: (chunk_of(rev, i), h)

    def smap(rev):
        return lambda h, i: (chunk_of(rev, i), h, 0, 0)

    def body(lg_ref, qf_ref, kf_ref, vf_ref, qb_ref, kb_ref, vb_ref, o_ref, sf_ref, sb_ref, state_f, state_b):
        h, i = pl.program_id(0), pl.program_id(1)

        @pl.when(i == 0)
        def _():
            state_f[...] = jnp.zeros_like(state_f)
            state_b[...] = jnp.zeros_like(state_b)

        for rev, (q_ref, k_ref, v_ref, s_ref, state) in enumerate((
                (qf_ref, kf_ref, vf_ref, sf_ref, state_f), (qb_ref, kb_ref, vb_ref, sb_ref, state_b))):
            _, mask, qd, kd, cd = _decays(lg_ref[rev, h], bool(rev))
            qv, kv, vv = q_ref[...], k_ref[...], v_ref[...]
            st = state[...]
            p = _dot(qv, kv, _NT) * mask
            out = _dot(p.astype(BF16), vv) + _dot((qv * qd).astype(BF16), st.astype(BF16))
            s_ref[...] = st
            state[...] = cd * st + _dot((kv * kd).astype(BF16), vv, _TN)
            lat_chunk = chunk_of(bool(rev), i) - ctx_chunks
            is_lat = (lat_chunk >= 0) & (lat_chunk < lat_chunks)
            first = (2 * i < nc - 1) if rev else (2 * i <= nc - 1)

            @pl.when(is_lat & first)
            def _(lat_chunk=lat_chunk, out=out):
                o_ref[pl.ds(pl.multiple_of(lat_chunk * CHUNK, CHUNK), CHUNK), :] = out

            @pl.when(is_lat & jnp.logical_not(first))
            def _(lat_chunk=lat_chunk, out=out):
                o_ref[pl.ds(pl.multiple_of(lat_chunk * CHUNK, CHUNK), CHUNK), :] += out

    def specs(rev):
        return [pl.BlockSpec((CHUNK, dk), cmap(rev)), pl.BlockSpec((CHUNK, dk), cmap(rev)),
                pl.BlockSpec((CHUNK, dv), cmap(rev))]

    state_shape = jax.ShapeDtypeStruct((nc, HEADS, dk, dv), F32)
    return pl.pallas_call(
        body, name=name, grid=(HEADS, nc),
        in_specs=[pl.BlockSpec(memory_space=pltpu.SMEM)] + specs(False) + specs(True),
        out_specs=[pl.BlockSpec((n_lat, dv), lambda h, i: (0, h)), pl.BlockSpec((None, None, dk, dv), smap(False)),
                   pl.BlockSpec((None, None, dk, dv), smap(True))],
        out_shape=[jax.ShapeDtypeStruct((n_lat, 2 * d), F32), state_shape, state_shape],
        scratch_shapes=[pltpu.VMEM((dk, dv), F32), pltpu.VMEM((dk, dv), F32)],
        compiler_params=_params(("parallel", "arbitrary")))(lgs, q, k, v, q, k, v)


def _ret_bwd(q, k, v, do, states_f, states_b, lgs, name):
    tt, d = q.shape
    dk, dv = d // HEADS, 2 * d // HEADS
    nc = tt // CHUNK
    lat_chunks = do.shape[0] // CHUNK
    ctx_chunks = (nc - lat_chunks) // 2

    def chunk_of(rev, i):
        return i if rev else nc - 1 - i

    def cmap(rev):
        return lambda h, i: (chunk_of(rev, i), h)

    def do_map(rev):
        return lambda h, i: (jnp.clip(chunk_of(rev, i) - ctx_chunks, 0, lat_chunks - 1), h)

    def smap(rev):
        return lambda h, i: (chunk_of(rev, i), h, 0, 0)

    def body(lg_ref, *refs):
        h = pl.program_id(0)
        dstates = refs[-2:]

        @pl.when(pl.program_id(1) == 0)
        def _():
            for rev in (0, 1):
                dstates[rev][...] = jnp.zeros_like(dstates[rev])
                refs[10 + 4 * rev + 3][...] = jnp.zeros_like(refs[10 + 4 * rev + 3])

        for rev in (0, 1):
            q_ref, k_ref, v_ref, do_ref, s_ref = refs[5 * rev:5 * rev + 5]
            dq_ref, dk_ref, dv_ref, dlg_ref = refs[10 + 4 * rev:10 + 4 * rev + 4]
            dstate = dstates[rev]
            chunk = chunk_of(bool(rev), pl.program_id(1))
            is_lat = (chunk >= ctx_chunks) & (chunk < ctx_chunks + lat_chunks)
            (rr, rel), mask, qd, kd, cd = _decays(lg_ref[rev, h], bool(rev))
            qv, kv, vv = q_ref[...], k_ref[...], v_ref[...]
            dov = jnp.where(is_lat, do_ref[...], jnp.zeros_like(do_ref))
            st, dst = s_ref[...], dstate[...]
            st_b, dst_b = st.astype(BF16), dst.astype(BF16)
            p = _dot(qv, kv, _NT) * mask
            dp = _dot(dov, vv, _NT)
            da = (dp * mask).astype(BF16)
            dq_state = _dot(dov, st_b, _NT) * qd
            dk_state = _dot(vv, dst_b, _NT) * kd
            dq_ref[...] = (_dot(da, kv) + dq_state).astype(BF16)
            dk_ref[...] = (_dot(da, qv, _TN) + dk_state).astype(BF16)
            dv_ref[...] = (_dot(p.astype(BF16), dov, _TN) + _dot((kv * kd).astype(BF16), dst_b)).astype(BF16)
            dnew = cd * dst + _dot((qv * qd).astype(BF16), dov, _TN)
            dstate[...] = dnew
            through = rr * (jnp.sum(qv.astype(F32) * dq_state, axis=1, keepdims=True)
                            - jnp.sum(kv.astype(F32) * dk_state, axis=1, keepdims=True))
            dlg_ref[...] += _total(rel * p * dp) + _total(through) + CHUNK * _total(st * dnew)

    def specs(rev):
        return [pl.BlockSpec((CHUNK, dk), cmap(rev)), pl.BlockSpec((CHUNK, dk), cmap(rev)),
                pl.BlockSpec((CHUNK, dv), cmap(rev)), pl.BlockSpec((CHUNK, dv), do_map(rev)),
                pl.BlockSpec((None, None, dk, dv), smap(rev))]

    def outs(rev):
        return [pl.BlockSpec((CHUNK, dk), cmap(rev)), pl.BlockSpec((CHUNK, dk), cmap(rev)),
                pl.BlockSpec((CHUNK, dv), cmap(rev)), pl.BlockSpec((None, 8, 128), lambda h, i: (h, 0, 0))]

    return pl.pallas_call(
        body, name=name, grid=(HEADS, nc),
        in_specs=[pl.BlockSpec(memory_space=pltpu.SMEM)] + specs(False) + specs(True),
        out_specs=outs(False) + outs(True),
        out_shape=[jax.ShapeDtypeStruct((tt, d), BF16), jax.ShapeDtypeStruct((tt, d), BF16),
                   jax.ShapeDtypeStruct((tt, 2 * d), BF16), jax.ShapeDtypeStruct((HEADS, 8, 128), F32)] * 2,
        scratch_shapes=[pltpu.VMEM((dk, dv), F32), pltpu.VMEM((dk, dv), F32)],
        compiler_params=_params(("parallel", "arbitrary")))(lgs, q, k, v, do, states_f, q, k, v, do, states_b)


def _silu(v):
    return v * _sig(v)


def _mod_proj(cs, w_mod, b_loc, dec, name):
    nb = w_mod.shape[1]

    def body(cs_ref, w_ref, b_ref, dec_ref, out_ref, lg_ref):
        out_ref[...] = jnp.dot(_silu(cs_ref[...]), w_ref[...], preferred_element_type=F32, precision=HIGHEST) + b_ref[...]
        a = dec_ref[...]
        lg_ref[...] = jnp.minimum(a, 0.0) - jnp.log1p(jnp.exp(-jnp.abs(a)))

    return pl.pallas_call(
        body, name=name,
        out_shape=[jax.ShapeDtypeStruct((16, nb), F32), jax.ShapeDtypeStruct(dec.shape, F32)],
        compiler_params=_params())(cs, w_mod, b_loc, dec)


def _mod_grad(cs, dm, w_mod, name):
    d, nb = w_mod.shape

    def body(cs_ref, dm_ref, w_ref, gw_ref, part_ref):
        dmv = dm_ref[...]
        gw_ref[...] = lax.dot_general(_silu(cs_ref[...]), dmv, _TN, preferred_element_type=F32, precision=HIGHEST)
        part_ref[...] = lax.dot_general(dmv, w_ref[...], _NT, preferred_element_type=F32, precision=HIGHEST)

    return pl.pallas_call(
        body, name=name,
        out_shape=[jax.ShapeDtypeStruct((d, nb), F32), jax.ShapeDtypeStruct((16, d), F32)],
        compiler_params=_params())(cs, dm, w_mod)


def _reduce_small(gathered, dec, n_feat, name):
    _, rows, d = gathered.shape

    def body(g_ref, dec_ref, red_ref, misc_ref):
        total = g_ref[0]
        for i in range(1, N_DEV):
            total = total + g_ref[i]
        red_ref[...] = total
        misc_ref[...] = jnp.zeros_like(misc_ref)
        misc_ref[0:1, :] = jnp.zeros((1, 128), F32) + (0.5 / n_feat) * _total(total[16:17, :])
        misc_ref[1:3, :] = total[14:16, 0:128] * _sig(-dec_ref[0:2, :])

    return pl.pallas_call(
        body, name=name,
        out_shape=[jax.ShapeDtypeStruct((rows, d), F32), jax.ShapeDtypeStruct((8, 128), F32)],
        compiler_params=_params())(gathered, dec)


def _c_ctx_grad(parts, c_ctx, name):
    d = c_ctx.shape[1]

    def body(p_ref, c_ref, out_ref):
        total = p_ref[0]
        for i in range(1, N_DEV):
            total = total + p_ref[i]
        cv = c_ref[...]
        sg = _sig(cv)
        out_ref[...] = total[8:9, :] * (sg * (1.0 + cv * (1.0 - sg)))

    return pl.pallas_call(body, name=name, out_shape=jax.ShapeDtypeStruct((1, d), F32),
                          compiler_params=_params())(parts, c_ctx)


def _adamw(parts, w, m, v, name, own=None):
    n_parts, rows, cols = parts.shape
    row_bytes = cols * (parts.dtype.itemsize * (n_parts + 1) + 7 * 4)
    tr = _tile(rows, max(16, (8 * 1024 * 1024) // row_bytes), 16 if rows % 16 == 0 else 8)
    blocks = rows // tr

    def body(*refs):
        p_ref, w_ref, m_ref, v_ref, g_ref, d_ref, mo_ref, vo_ref = refs[-8:]
        if own is None:
            g = p_ref[0].astype(F32)
        else:
            g = refs[1][...].astype(F32) + p_ref[0].astype(F32)
        for i in range(1, n_parts):
            g = g + p_ref[i].astype(F32)
        m2 = ADAM_B1 * m_ref[...] + (1.0 - ADAM_B1) * g
        v2 = ADAM_B2 * v_ref[...] + (1.0 - ADAM_B2) * jnp.square(g)
        m_hat = m2 / (1.0 - ADAM_B1 ** ADAM_STEP)
        v_hat = v2 / (1.0 - ADAM_B2 ** ADAM_STEP)
        g_ref[...] = g
        d_ref[...] = -ADAM_LR * (m_hat / (jnp.sqrt(v_hat) + ADAM_EPS) + ADAM_WD * w_ref[...])
        mo_ref[...] = m2
        vo_ref[...] = v2

    out_shape = [jax.ShapeDtypeStruct((rows, cols), F32)] * 4
    if own is None:
        spec = _rspec(tr, cols)
        return pl.pallas_call(
            body, name=name, grid=(blocks,),
            in_specs=[pl.BlockSpec((n_parts, tr, cols), lambda i: (0, i, 0)), spec, spec, spec],
            out_specs=[spec] * 4, out_shape=out_shape, compiler_params=_params(("parallel",)))(parts, w, m, v)
    g_full, axis, me = own

    def own_map(i, me_ref):
        return (me_ref[0] * blocks + i, 0) if axis == 0 else (i, me_ref[0])

    spec = pl.BlockSpec((tr, cols), lambda i, me_ref: (i, 0))
    return pl.pallas_call(
        body, name=name, out_shape=out_shape,
        grid_spec=pltpu.PrefetchScalarGridSpec(
            num_scalar_prefetch=1, grid=(blocks,),
            in_specs=[pl.BlockSpec((tr, cols), own_map),
                      pl.BlockSpec((n_parts, tr, cols), lambda i, me_ref: (0, i, 0)), spec, spec, spec],
            out_specs=[spec] * 4),
        compiler_params=_params(("parallel",)))(me.reshape(1), g_full, parts, w, m, v)


def _rope_tables(pos, dk):
    half = dk // 2
    inv_freq = 1.0 / (ROPE_BASE ** jnp.linspace(0.0, 1.0, half, dtype=F32))
    ang = pos[:, None] * inv_freq[None, :]
    return jnp.cos(ang), jnp.sin(ang)


def _pad_lanes(v, width):
    return jnp.pad(v, ((0, 0), (0, width - v.shape[1])))


def kernel(x, c, ctx, c_ctx, w_mod, b_mod, norm1_g, w_in, conv_w, w_conv_out, ret_decay_fwd, ret_decay_bwd, w_ret_out, w_o, norm2_g, w_ff1, w_ff2, final_g, loss_target, m_c_ctx, m_w_mod, m_b_mod, m_norm1_g, m_w_in, m_conv_w, m_w_conv_out, m_ret_decay_fwd, m_ret_decay_bwd, m_w_ret_out, m_w_o, m_norm2_g, m_w_ff1, m_w_ff2, m_final_g, v_c_ctx, v_w_mod, v_b_mod, v_norm1_g, v_w_in, v_conv_w, v_w_conv_out, v_ret_decay_fwd, v_ret_decay_bwd, v_w_ret_out, v_w_o, v_norm2_g, v_w_ff1, v_w_ff2, v_final_g):
    n_lat, d = x.shape[1], x.shape[2]
    n_ctx = ctx.shape[1]
    assert n_ctx % CHUNK == 0 and n_lat % CHUNK == 0 and n_lat % GRID_W == 0
    dk = d // HEADS
    nb = w_mod.shape[2]
    me = 4 * lax.axis_index("x") + 2 * lax.axis_index("y") + lax.axis_index("c")
    xl, ctxl, target = x[0], ctx[0], loss_target[0]

    first = _small_allgather(jnp.concatenate([c, _pad_lanes(conv_w[0], d), jnp.zeros((4, d), F32)], axis=0), "ag_c")
    conv_w_f = first[:, 1:4, :d // N_DEV].transpose(1, 0, 2).reshape(3, d)

    c_all = first[:, 0, :]
    cs = jnp.concatenate([c_all, c_ctx[None], jnp.zeros((7, d), F32)], axis=0)
    dec = jnp.pad(jnp.concatenate([ret_decay_fwd, ret_decay_bwd], axis=0), ((0, 6), (0, 128 - HEADS)))
    b_loc = lax.dynamic_slice(b_mod, (0, me * nb), (1, nb))
    modp, lgs = _mod_proj(cs, w_mod[0], b_loc, dec, "mod_proj")
    modp_all = _small_allgather(modp, "ag_mod")
    mod_l = lax.dynamic_index_in_dim(modp_all, me, axis=1, keepdims=False).reshape(N_MOD, d)
    mod_c = modp_all[:, 8, :].reshape(N_MOD, d)
    lgs = lgs[0:2, :HEADS]
    zero_row = jnp.zeros((1, d), F32)
    vec1_l = jnp.concatenate([norm1_g, mod_l[0:1], mod_l[1:2], zero_row], axis=0)
    vec1_c = jnp.concatenate([norm1_g, mod_c[0:1], mod_c[1:2], zero_row], axis=0)
    vec2 = jnp.concatenate([norm2_g, mod_l[3:4], mod_l[4:5], mod_l[2:3]], axis=0)
    vec3 = jnp.concatenate([mod_l[5:6], final_g[None]], axis=0)

    a_all = _modulate(ctxl, vec1_c, "modulate1_ctx", n_lat + n_ctx,
                      into=_modulate(xl, vec1_l, "modulate1", n_lat + n_ctx))
    arrival = me ^ jnp.array([0, 1, 4, 5, 2, 3, 6, 7], jnp.int32)
    riders = [(_place_shard(w[0], 0, me, "place_" + wname), 0)
              for wname, w in (("w_conv_out", w_conv_out), ("w_ret_out", w_ret_out), ("w_o", w_o))]
    w_in_f, w_conv_out_f, w_ret_out_f, w_o_f, z = _gather_project(
        a_all, _place_shard(w_in[0], 1, me, "place_w_in"), arrival, "proj_in", riders=riders)

    later, after = {}, (z, conv_w_f)
    for wname, w, axis in (("w_ff1", w_ff1, 1), ("w_ff2", w_ff2, 0)):
        later[wname], token = _exchange_start(_place_shard(w[0], axis, me, "place_" + wname), axis, True,
                                              "ag_" + wname + "_start", after=after)
        after = (token,)
    pc = _conv_fwd(z, conv_w_f, n_lat, "conv_fwd", token)
    pos = jnp.concatenate([n_ctx + jnp.arange(n_lat, dtype=F32), jnp.arange(n_ctx, dtype=F32)])
    cos, sin = _rope_tables(pos, dk)
    q_s, k_s, v_s = _rotary_fwd(z, cos, sin, n_lat, "rotary_fwd")
    o, st_f, st_b = _ret_fwd(q_s, k_s, v_s, lgs, n_lat, "ret_fwd")
    r = _gn_fwd(o, z, "gn_fwd")
    y_conv = _mm(pc, w_conv_out_f, "nn", [F32], "proj_conv_out", tn=2048)
    tn_d = _tile(d, 1024, 128)
    gate_offs = (CGC * d // tn_d, CGR * d // tn_d)

    def merge(acc, yc, gc, gr):
        return acc, _sig(gc) * yc + _sig(gr) * acc

    y_ret, mg = _mm(r, w_ret_out_f, "nn", [F32, BF16], "proj_ret_out", tm=512, tn=tn_d, tk=2 * d,
                    extras=[(y_conv, 0), (z, gate_offs[0]), (z, gate_offs[1])], epilogue=merge)
    y_l = _mm(mg, w_o_f, "nn", [F32], "proj_o", tn=2048)
    x1, a2 = _resid_modulate(xl, y_l, vec2, "resid_modulate2")

    def sqrelu(acc):
        return acc, jnp.square(jnp.maximum(acc, 0.0))

    w_ff1_f = _exchange_wait(later["w_ff1"], a2, "ag_w_ff1_wait")
    hff, s = _mm(a2, w_ff1_f, "nn", [BF16, BF16], "ff1", tn=2048, epilogue=sqrelu)
    w_ff2_f = _exchange_wait(later["w_ff2"], s, "ag_w_ff2_wait")
    f = _mm(s, w_ff2_f, "nn", [F32], "ff2")
    dx2, df, acc3 = _loss_head(x1, f, target, vec3, "loss_head")

    def d_sqrelu(acc, h):
        return (acc * (2.0 * jnp.maximum(h.astype(F32), 0.0)),)

    dh = _mm(df, w_ff2_f, "nt", [BF16], "ff2_dx", extras=[(hff, 0)], epilogue=d_sqrelu)
    sent = {}
    sent["w_ff2"], token = _exchange_start(_mm(s, df, "tn", [BF16], "ff2_dw", tn=2048), 0, False, "rs_w_ff2_start")
    da2 = _mm(dh, w_ff1_f, "nt", [F32], "ff1_dx", dep=token)
    sent["w_ff1"], token = _exchange_start(_mm(a2, dh, "tn", [BF16], "ff1_dw", tn=2048), 1, False, "rs_w_ff1_start")
    dx1, dyl, acc2 = _modulate_bwd(da2, x1, vec2, "modulate2_bwd", dx_in=dx2, y=y_l)

    def d_merge(acc, yc, yr, gc, gr):
        sc, sr = _sig(gc), _sig(gr)
        return acc * sc, acc * sr, acc * yc * (sc * (1.0 - sc)), acc * yr * (sr * (1.0 - sr))

    dyc, dyr, dgc, dgr = _mm(dyl, w_o_f, "nt", [BF16] * 4, "proj_o_dx", tm=512, tn=tn_d,
                             extras=[(y_conv, 0), (y_ret, 0), (z, gate_offs[0]), (z, gate_offs[1])], epilogue=d_merge,
                             dep=token)
    sent["w_o"], token = _exchange_start(_mm(mg, dyl, "tn", [BF16], "proj_o_dw", tn=2048), 0, False, "rs_w_o_start")
    dpc = _mm(dyc, w_conv_out_f, "nt", [F32], "proj_conv_out_dx", tn=2048, dep=token)
    sent["w_conv_out"], token = _exchange_start(_mm(pc, dyc, "tn", [BF16], "proj_conv_out_dw", tn=2048), 0, False,
                                                "rs_w_conv_out_start")
    dz_conv, acc_conv = _conv_bwd(dpc, z, conv_w_f, "conv_bwd")
    dv_head = 2 * d // HEADS
    do, dz_g = _mm(dyr, w_ret_out_f, "nt", [BF16, BF16], "proj_ret_out_dx", tn=dv_head,
                   extras=[(o, 0), (z, CG * d // dv_head)], epilogue=_gn_bwd_head, dep=token)
    sent["w_ret_out"], token = _exchange_start(_mm(r, dyr, "tn", [BF16], "proj_ret_out_dw", tn=2048), 0, False,
                                               "rs_w_ret_out_start")
    dq_f, dk_f, dv_f, dlg_f, dq_b, dk_b, dv_b, dlg_b = _ret_bwd(q_s, k_s, v_s, do, st_f, st_b, lgs, "ret_bwd")
    dz = _dz_assemble(dq_f, dq_b, dk_f, dk_b, dv_f, dv_b, cos, sin, dz_conv, dz_g, dgc, dgr, n_lat, n_ctx,
                      "dz_assemble")
    g_in = _mm(a_all, dz, "tn", [BF16], "proj_in_dw", tn=2048, tk=(n_lat + n_ctx) // 4, dep=token)
    sent["w_in"], token = _exchange_start(g_in, 1, False, "rs_w_in_start")
    da_all = _mm(dz, w_in_f, "nt", [F32], "proj_in_dx", tn=2048, dep=token)
    grad_x, acc1 = _modulate_bwd(da_all, xl, vec1_l, "modulate1_bwd", dx_in=dx1)
    _, acc1c = _modulate_bwd(da_all, ctxl, vec1_c, "modulate1_ctx_bwd", da_off=n_lat)

    lane_pad = functools.partial(_pad_lanes, width=d)
    packet = jnp.concatenate([
        acc1[2:3] + acc1c[2:3], acc2[2:3], acc3[1:2],
        acc1[0:1], acc1[1:2], acc2[3:4], acc2[0:1], acc2[1:2], acc3[0:1],
        acc1c[0:1], acc1c[1:2],
        acc_conv[0:3],
        lane_pad(dlg_f[:, 0, 0][None]), lane_pad(dlg_b[:, 0, 0][None]),
        acc3[2:3],
        jnp.zeros((7, d), F32)], axis=0)
    n_rows = packet.shape[0]
    packets = lax.dynamic_update_slice(jnp.zeros((N_DEV * n_rows, d), F32), packet, (me * n_rows, 0))
    packets_sent, after = _exchange_start(packets, 0, True, "ag_small_start")
    res = {}

    def update(wname, axis, w, m, v, after):
        g_full, parts = _exchange_wait(sent[wname], after, "rs_" + wname + "_wait")
        res[wname] = _adamw(parts, w[0], m[0], v[0], "adamw_" + wname, own=(g_full, axis, me))
        return res[wname][0]

    for wname, axis, w, m, v in (("w_ff2", 0, w_ff2, m_w_ff2, v_w_ff2), ("w_ff1", 1, w_ff1, m_w_ff1, v_w_ff1),
                                 ("w_o", 0, w_o, m_w_o, v_w_o), ("w_conv_out", 0, w_conv_out, m_w_conv_out, v_w_conv_out),
                                 ("w_ret_out", 0, w_ret_out, m_w_ret_out, v_w_ret_out)):
        after = update(wname, axis, w, m, v, after)
    gathered = _exchange_wait(packets_sent, after, "ag_small_wait").reshape(N_DEV, n_rows, d)
    red, misc = _reduce_small(gathered, dec, d, "reduce_small")
    dmod_ctx = jnp.concatenate([red[9], red[10], jnp.zeros((4 * d,), F32)])
    dmod_all = jnp.concatenate([gathered[:, 3:9, :].reshape(N_DEV, N_MOD * d), dmod_ctx[None]], axis=0)
    dm = jnp.pad(lax.dynamic_slice(dmod_all, (0, me * nb), (N_DEV + 1, nb)), ((0, 7), (0, 0)))
    g_mod, c_part = _mod_grad(cs, dm, w_mod[0], "mod_grad")
    g_b_mod = red[3:9].reshape(1, N_MOD * d) + dmod_ctx[None]
    g_conv_w = lax.dynamic_slice(red[11:14], (0, me * (d // N_DEV)), (3, d // N_DEV))

    c_parts = lax.dynamic_update_slice(jnp.zeros((N_DEV * 16, d), F32), c_part, (me * 16, 0))
    c_parts_sent, token = _exchange_start(c_parts, 0, True, "ag_c_ctx_start")
    res["w_mod"] = _adamw(g_mod[None], w_mod[0], m_w_mod[0], v_w_mod[0], "adamw_w_mod")
    after = update("w_in", 1, w_in, m_w_in, v_w_in, token)
    c_parts = _exchange_wait(c_parts_sent, after, "ag_c_ctx_wait").reshape(N_DEV, 16, d)
    g_c_ctx = _c_ctx_grad(c_parts, c_ctx[None], "c_ctx_grad")
    res = {k: tuple(t[None] for t in val) for k, val in res.items()}

    small = [("c_ctx", g_c_ctx, c_ctx, m_c_ctx, v_c_ctx), ("b_mod", g_b_mod, b_mod, m_b_mod, v_b_mod),
             ("norm1_g", red[0:1], norm1_g, m_norm1_g, v_norm1_g), ("conv_w", g_conv_w, conv_w, m_conv_w, v_conv_w),
             ("ret_decay_fwd", misc[1:2, :HEADS], ret_decay_fwd, m_ret_decay_fwd, v_ret_decay_fwd),
             ("ret_decay_bwd", misc[2:3, :HEADS], ret_decay_bwd, m_ret_decay_bwd, v_ret_decay_bwd),
             ("norm2_g", red[1:2], norm2_g, m_norm2_g, v_norm2_g), ("final_g", red[2:3], final_g, m_final_g, v_final_g)]

    def flat(t):
        t = t.reshape(-1)
        return jnp.pad(t, (0, (-t.shape[0]) % 1024))

    packed = [jnp.concatenate([flat(item[j]) for item in small]).reshape(-1, 128) for j in range(1, 5)]
    outs = _adamw(packed[0][None], packed[1], packed[2], packed[3], "adamw_small")
    start = 0
    for name, _, w, _, _ in small:
        size = w.size
        res[name] = tuple(o.reshape(-1)[start:start + size].reshape(w.shape) for o in outs)
        start += size + (-size) % 1024

    order = ["c_ctx", "w_mod", "b_mod", "norm1_g", "w_in", "conv_w", "w_conv_out", "ret_decay_fwd", "ret_decay_bwd",
             "w_ret_out", "w_o", "norm2_g", "w_ff1", "w_ff2", "final_g"]
    loss = misc[0, 0]
    return (loss, grad_x[None], *[res[n][0] for n in order], *[res[n][1] for n in order],
            *[res[n][2] for n in order], *[res[n][3] for n in order])
```

```python
import functools

import jax
import jax.numpy as jnp
from jax import lax
from jax.experimental import pallas as pl
from jax.experimental.pallas import tpu as pltpu

F32 = jnp.float32
BF16 = jnp.bfloat16
MESH = pl.DeviceIdType.MESH

N_DEV = 8
HEADS = 8
N_MOD = 6
N_IN = 11
GRID_W = 64
CHUNK = 256
ROPE_BASE = 10000.0
EPS = 1e-6
ADAM_LR, ADAM_B1, ADAM_B2, ADAM_EPS, ADAM_WD, ADAM_STEP = 0.001, 0.9, 0.999, 1e-08, 0.01, 10
VMEM_LIMIT = 56 * 1024 * 1024
HIGHEST = lax.Precision.HIGHEST
CB, CC, CX, CQ, CK, CV, CG, CGC, CGR = 0, 1, 2, 3, 4, 5, 7, 9, 10


def _tile(n, target, mult):
    t = (min(target, n) // mult) * mult
    while t >= mult:
        if n % t == 0:
            return t
        t -= mult
    return n


def _params(sem=None):
    return pltpu.CompilerParams(dimension_semantics=sem, vmem_limit_bytes=VMEM_LIMIT)


def _sig(v):
    return 1.0 / (1.0 + jnp.exp(-v))


def _coords():
    return lax.axis_index("x"), lax.axis_index("y"), lax.axis_index("c")


def _flip(p, m):
    return tuple(1 - v if (m >> s) & 1 else v for v, s in zip(p, (2, 1, 0)))


def _index(p):
    return 4 * p[0] + 2 * p[1] + p[2]


def _small_allgather(x, name):
    r, n = x.shape

    def body(x_ref, out_ref, send_sems, recv_sems):
        me = _coords()
        out_ref[pl.ds(_index(me), 1)] = x_ref[...][None]
        sent = []
        for m in range(1, N_DEV):
            cp = pltpu.make_async_remote_copy(
                src_ref=x_ref, dst_ref=out_ref.at[_index(me)], send_sem=send_sems.at[m - 1],
                recv_sem=recv_sems.at[m - 1], device_id=_flip(me, m), device_id_type=MESH)
            cp.start()
            sent.append(cp)
        for m in range(1, N_DEV):
            pltpu.make_async_remote_copy(
                src_ref=x_ref, dst_ref=out_ref.at[_index(_flip(me, m))], send_sem=send_sems.at[m - 1],
                recv_sem=recv_sems.at[m - 1], device_id=_flip(me, m), device_id_type=MESH).wait_recv()
        for cp in sent:
            cp.wait_send()

    return pl.pallas_call(
        body, name=name, out_shape=jax.ShapeDtypeStruct((N_DEV, r, n), x.dtype),
        in_specs=[pl.BlockSpec(memory_space=pltpu.VMEM)], out_specs=pl.BlockSpec(memory_space=pltpu.VMEM),
        scratch_shapes=[pltpu.SemaphoreType.DMA((N_DEV - 1,)), pltpu.SemaphoreType.DMA((N_DEV - 1,))],
    )(x)


def _window(ref, j, r, c, axis):
    if axis == 0:
        return ref.at[pl.ds(pl.multiple_of(j * r, 8), r), :]
    return ref.at[:, pl.ds(pl.multiple_of(j * c, 128), c)]


_HBM = pl.BlockSpec(memory_space=pltpu.HBM)
_SEM = pl.BlockSpec(memory_space=pltpu.SEMAPHORE)
_ANY = pl.BlockSpec(memory_space=pl.ANY)
_EFFECT = pltpu.SideEffectType.DATAFLOW_SIDE_EFFECTING


def _place_shard(w, axis, me, name):
    r, c = w.shape
    full = (N_DEV * r, c) if axis == 0 else (r, N_DEV * c)
    tr = _tile(r, max(16, (2 * 1024 * 1024) // (c * 4)), 16)
    blocks = r // tr

    def out_map(i, me_ref):
        return (me_ref[0] * blocks + i, 0) if axis == 0 else (i, me_ref[0])

    def body(me_ref, w_ref, out_ref):
        out_ref[...] = w_ref[...].astype(BF16)

    return pl.pallas_call(
        body, name=name, out_shape=jax.ShapeDtypeStruct(full, BF16),
        grid_spec=pltpu.PrefetchScalarGridSpec(
            num_scalar_prefetch=1, grid=(blocks,), in_specs=[pl.BlockSpec((tr, c), lambda i, me_ref: (i, 0))],
            out_specs=pl.BlockSpec((tr, c), out_map)),
        compiler_params=_params(("parallel",)))(me.reshape(1), w)


def _exchange_start(src, axis, gather, name, after=()):
    r, c = (src.shape[0] // N_DEV, src.shape[1]) if axis == 0 else (src.shape[0], src.shape[1] // N_DEV)
    n_hbm = 1 if gather else 2
    n_after = len(after)

    def body(*refs):
        src_ref, land_ref = refs[0], refs[n_hbm - 1]
        send_sems, recv_sems = refs[n_hbm + n_after:n_hbm + n_after + 2]
        token = refs[-1]
        me = _coords()
        for m in range(1, N_DEV):
            peer = _flip(me, m)
            if gather:
                mine = theirs = _window(land_ref, _index(me), r, c, axis)
            else:
                mine, theirs = _window(src_ref, _index(peer), r, c, axis), land_ref.at[m - 1]
            pltpu.make_async_remote_copy(
                src_ref=mine, dst_ref=theirs, send_sem=send_sems.at[m - 1], recv_sem=recv_sems.at[m - 1],
                device_id=peer, device_id_type=MESH).start()
        token[...] = jnp.zeros_like(token)

    hbm = [pltpu.with_memory_space_constraint(src, pltpu.HBM)]
    if not gather:
        hbm.append(pltpu.with_memory_space_constraint(lax.empty((N_DEV - 1, r, c), src.dtype), pltpu.HBM))
    outs = pl.pallas_call(
        body, name=name,
        out_shape=(pltpu.SemaphoreType.DMA((N_DEV - 1,)), pltpu.SemaphoreType.DMA((N_DEV - 1,)),
                   *[pltpu.HBM(t.shape, t.dtype) for t in hbm], jax.ShapeDtypeStruct((8, 128), F32)),
        in_specs=[_HBM] * n_hbm + [_ANY] * n_after,
        out_specs=(_SEM, _SEM, *[_HBM] * n_hbm, pl.BlockSpec(memory_space=pltpu.VMEM)),
        input_output_aliases={i: 2 + i for i in range(n_hbm)},
        compiler_params=pltpu.CompilerParams(has_side_effects=_EFFECT),
    )(*hbm, *after)
    return (outs[:2], outs[2:2 + n_hbm], (axis, gather, r, c)), outs[-1]


def _exchange_wait(handle, after, name):
    (send_sems, recv_sems), hbm, (axis, gather, r, c) = handle
    n_hbm = len(hbm)

    def body(*refs):
        src_ref, land_ref = refs[0], refs[n_hbm - 1]
        send_sems, recv_sems = refs[n_hbm:n_hbm + 2]
        me = _coords()
        for m in range(1, N_DEV):
            peer = _flip(me, m)
            if gather:
                mine, theirs = _window(land_ref, _index(me), r, c, axis), _window(land_ref, _index(peer), r, c, axis)
            else:
                mine, theirs = _window(src_ref, _index(peer), r, c, axis), land_ref.at[m - 1]
            copy = pltpu.make_async_remote_copy(
                src_ref=mine, dst_ref=theirs, send_sem=send_sems.at[m - 1], recv_sem=recv_sems.at[m - 1],
                device_id=peer, device_id_type=MESH)
            copy.wait_send()
            copy.wait_recv()

    outs = pl.pallas_call(
        body, name=name, out_shape=tuple(pltpu.HBM(t.shape, t.dtype) for t in hbm),
        in_specs=[_HBM] * n_hbm + [_SEM, _SEM, _ANY], out_specs=tuple([_HBM] * n_hbm),
        input_output_aliases={i: i for i in range(n_hbm)},
        compiler_params=pltpu.CompilerParams(has_side_effects=_EFFECT),
    )(*hbm, send_sems, recv_sems, after)
    return outs[0] if gather else tuple(outs)


def _mm(a, b, mode, out_dtypes, name, tm=1024, tn=1024, tk=2048, extras=(), epilogue=None, dep=None):
    if mode == "nn":
        (m, k), n = a.shape, b.shape[1]
    elif mode == "nt":
        (m, k), n = a.shape, b.shape[0]
    else:
        (k, m), n = a.shape, b.shape[1]
    tm, tn = _tile(m, tm, 8), _tile(n, tn, min(128, tn))
    tk = _tile(k, tk, 16 if mode == "tn" else 128)
    nk = k // tk
    swap = nk == 1 and (k * n + (n // tn) * m * k) < (m * k + (m // tm) * k * n)

    def ij(p, q):
        return (q, p) if swap else (p, q)

    def spec(shape, fn):
        return pl.BlockSpec(shape, lambda p, q, kk: fn(*ij(p, q), kk))

    a_spec = spec((tk, tm), lambda i, j, kk: (kk, i)) if mode == "tn" else spec((tm, tk), lambda i, j, kk: (i, kk))
    b_spec = spec((tn, tk), lambda i, j, kk: (j, kk)) if mode == "nt" else spec((tk, tn), lambda i, j, kk: (kk, j))
    dims = {"nn": (((1,), (0,)), ((), ())), "nt": (((1,), (1,)), ((), ())), "tn": (((0,), (0,)), ((), ()))}[mode]
    ex_specs = [spec((tm, tn), functools.partial(lambda i, j, kk, off: (i, j + off), off=off)) for _, off in extras]
    deps = [] if dep is None else [dep]
    dep_specs = [pl.BlockSpec(dep.shape, lambda p, q, kk: (0, 0))] if deps else []
    n_ex, n_out = len(extras), len(out_dtypes)
    n_in = 2 + n_ex + len(deps)

    def body(*refs):
        a_ref, b_ref = refs[0], refs[1]
        ex_refs = refs[2:2 + n_ex]
        out_refs = refs[n_in:n_in + n_out]

        def product():
            return lax.dot_general(a_ref[...], b_ref[...], dims, preferred_element_type=F32)

        def finish(res):
            res = epilogue(res, *[e[...] for e in ex_refs]) if epilogue is not None else (res,)
            for o_ref, val in zip(out_refs, res):
                o_ref[...] = val.astype(o_ref.dtype)

        if nk == 1:
            finish(product())
            return
        acc = refs[-1]
        kk = pl.program_id(2)

        @pl.when(kk == 0)
        def _():
            acc[...] = product()

        @pl.when((kk > 0) & (kk < nk - 1))
        def _():
            acc[...] += product()

        @pl.when(kk == nk - 1)
        def _():
            finish(acc[...] + product())

    outs = pl.pallas_call(
        body, name=name, grid=(*ij(m // tm, n // tn), nk),
        in_specs=[a_spec, b_spec] + ex_specs + dep_specs,
        out_specs=[spec((tm, tn), lambda i, j, kk: (i, j)) for _ in out_dtypes],
        out_shape=[jax.ShapeDtypeStruct((m, n), dt) for dt in out_dtypes],
        scratch_shapes=[pltpu.VMEM((tm, tn), F32)] if nk > 1 else [],
        compiler_params=_params(("parallel", "parallel", "arbitrary")),
    )(a, b, *[e for e, _ in extras], *deps)
    return outs if n_out > 1 else outs[0]


def _gather_project(a, w_full, order, name, riders=(), tm=768):
    m, k = a.shape
    c = w_full.shape[1] // N_DEV
    tm = _tile(m, tm, 8)
    n_i = m // tm
    chips = [4, 2, 6]
    n_rid = len(riders)
    shard = [(k, c, 1)] + [((t.shape[0] // N_DEV, t.shape[1], 0) if ax == 0 else (t.shape[0], t.shape[1] // N_DEV, 1))
                           for t, ax in riders]

    def body(order_ref, a_ref, *refs):
        outs = refs[1 + n_rid:2 + 2 * n_rid]
        z_ref, wbuf, send_sems, recv_sems, load_sems = refs[2 + 2 * n_rid:]
        w_ref = outs[0]
        p, i = pl.program_id(0), pl.program_id(1)
        me = _coords()
        sibling = _flip(me, 1)

        def copy(t, s, block, to):
            r_t, c_t, ax_t = shard[t]
            win = _window(outs[t], _index(block), r_t, c_t, ax_t)
            return pltpu.make_async_remote_copy(src_ref=win, dst_ref=win, send_sem=send_sems.at[7 * t + s],
                                                recv_sem=recv_sems.at[7 * t + s], device_id=to, device_id_type=MESH)

        def first(t):
            return [copy(t, 0, me, sibling)] + [copy(t, 1 + j, me, _flip(me, mm)) for j, mm in enumerate(chips)]

        def passed(t):
            return [copy(t, 4 + j, _flip(me, mm), sibling) for j, mm in enumerate(chips)]

        @pl.when((p == 0) & (i == 0))
        def _():
            for t in range(1 + n_rid):
                for cp in first(t):
                    cp.start()

        def load(piece):
            return pltpu.make_async_copy(_window(w_ref, order_ref[piece], k, c, 1), wbuf.at[piece % 2],
                                         load_sems.at[piece % 2])

        @pl.when((p == 0) & (i == 0))
        def _():
            load(p).start()

        arrivals = [(0, None)] + [x for j in range(3) for x in ((1 + j, passed(0)[j]), (4 + j, None))]
        for piece, (sem, forward) in enumerate(arrivals, start=1):
            @pl.when((p == piece - 1) & (i == n_i - 1))
            def _(sem=sem, forward=forward, piece=piece):
                src_block = (sibling if piece == 1 else
                             _flip(me if piece % 2 == 0 else sibling, chips[(piece - 2) // 2]))
                copy(0, sem, src_block, me).wait_recv()
                if forward is not None:
                    forward.start()
                load(p + 1).start()

        @pl.when(i == 0)
        def _():
            load(p).wait()

        @pl.when((p == N_DEV - 1) & (i == 0))
        def _():
            for t in range(1, 1 + n_rid):
                for j, mm in enumerate(chips):
                    copy(t, 1 + j, _flip(me, mm), me).wait_recv()
                    passed(t)[j].start()

        z_ref[...] = jnp.dot(a_ref[...], wbuf[p % 2], preferred_element_type=F32)

        @pl.when((p == N_DEV - 1) & (i == n_i - 1))
        def _():
            for t in range(1, 1 + n_rid):
                copy(t, 0, sibling, me).wait_recv()
                for j, mm in enumerate(chips):
                    copy(t, 4 + j, _flip(sibling, mm), me).wait_recv()
            for t in range(1 + n_rid):
                for cp in first(t) + passed(t):
                    cp.wait_send()

    gathered = [w_full] + [t for t, _ in riders]
    n_sem = 7 * len(gathered)
    return pl.pallas_call(
        body, name=name,
        out_shape=[jax.ShapeDtypeStruct(t.shape, t.dtype) for t in gathered]
        + [jax.ShapeDtypeStruct((m, N_DEV * c), F32)],
        grid_spec=pltpu.PrefetchScalarGridSpec(
            num_scalar_prefetch=1, grid=(N_DEV, n_i),
            in_specs=[pl.BlockSpec((tm, k), lambda p, i, order_ref: (i, 0))]
            + [pl.BlockSpec(memory_space=pl.ANY)] * len(gathered),
            out_specs=[pl.BlockSpec(memory_space=pl.ANY)] * len(gathered)
            + [pl.BlockSpec((tm, c), lambda p, i, order_ref: (i, order_ref[p]))],
            scratch_shapes=[pltpu.VMEM((2, k, c), w_full.dtype), pltpu.SemaphoreType.DMA((n_sem,)),
                            pltpu.SemaphoreType.DMA((n_sem,)), pltpu.SemaphoreType.DMA((2,))]),
        input_output_aliases={2 + t: t for t in range(len(gathered))},
        compiler_params=_params(("arbitrary", "arbitrary")))(order, a, *gathered)


def _rspec(tr, w, cb=0, off=0):
    return pl.BlockSpec((tr, w), lambda i: (i + off, cb))


def _cspec(shape):
    return pl.BlockSpec(shape, lambda i: (0,) * len(shape))


def _rms(xf):
    rstd = lax.rsqrt(jnp.mean(xf * xf, axis=-1, keepdims=True) + EPS)
    return xf * rstd, rstd


def _rms_bwd(dn, n, rstd):
    return rstd * (dn - n * jnp.mean(dn * n, axis=-1, keepdims=True))


def _colsum(v):
    return jnp.sum(v, axis=0, keepdims=True)


def _total(v):
    return jnp.sum(jnp.sum(v, axis=1, keepdims=True), axis=0, keepdims=True)


def _modulate(x, vec, name, total_rows, into=None):
    rows, d = x.shape
    tr = _tile(rows, 256, 8)
    off = 0 if into is None else (total_rows - rows) // tr

    def body(x_ref, v_ref, *refs):
        n, _ = _rms(x_ref[...])
        refs[-1][...] = ((n * v_ref[0:1, :]) * (1.0 + v_ref[2:3, :]) + v_ref[1:2, :]).astype(BF16)

    return pl.pallas_call(
        body, name=name, grid=(rows // tr,),
        in_specs=[_rspec(tr, d), _cspec(vec.shape)] + ([] if into is None else [_ANY]),
        out_specs=_rspec(tr, d, off=off), out_shape=jax.ShapeDtypeStruct((total_rows, d), BF16),
        input_output_aliases={} if into is None else {2: 0},
        compiler_params=_params(("parallel",)))(x, vec, *([] if into is None else [into]))


def _resid_modulate(x, y, vec, name):
    rows, d = x.shape
    tr = _tile(rows, 256, 8)

    def body(x_ref, y_ref, v_ref, x1_ref, a_ref):
        x1 = x_ref[...] + v_ref[3:4, :] * y_ref[...]
        x1_ref[...] = x1
        n, _ = _rms(x1)
        a_ref[...] = ((n * v_ref[0:1, :]) * (1.0 + v_ref[2:3, :]) + v_ref[1:2, :]).astype(BF16)

    return pl.pallas_call(
        body, name=name, grid=(rows // tr,), in_specs=[_rspec(tr, d), _rspec(tr, d), _cspec(vec.shape)],
        out_specs=[_rspec(tr, d), _rspec(tr, d)],
        out_shape=[jax.ShapeDtypeStruct((rows, d), F32), jax.ShapeDtypeStruct((rows, d), BF16)],
        compiler_params=_params(("parallel",)))(x, y, vec)


def _loss_head(x1, f, target, vec, name):
    rows, d = x1.shape
    tr = _tile(rows, 256, 8)

    def body(x1_ref, f_ref, t_ref, v_ref, dx2_ref, df_ref, acc_ref):
        @pl.when(pl.program_id(0) == 0)
        def _():
            acc_ref[...] = jnp.zeros_like(acc_ref)

        gate, gain = v_ref[0:1, :], v_ref[1:2, :]
        fv = f_ref[...]
        n, rstd = _rms(x1_ref[...] + gate * fv)
        err = n * gain - t_ref[...]
        dy = err / d
        dx2 = _rms_bwd(dy * gain, n, rstd)
        dx2_ref[...] = dx2
        df_ref[...] = (dx2 * gate).astype(BF16)
        acc_ref[0:1, :] += _colsum(dx2 * fv)
        acc_ref[1:2, :] += _colsum(dy * n)
        acc_ref[2:3, :] += _colsum(err * err)

    return pl.pallas_call(
        body, name=name, grid=(rows // tr,),
        in_specs=[_rspec(tr, d), _rspec(tr, d), _rspec(tr, d), _cspec(vec.shape)],
        out_specs=[_rspec(tr, d), _rspec(tr, d), _cspec((8, d))],
        out_shape=[jax.ShapeDtypeStruct((rows, d), F32), jax.ShapeDtypeStruct((rows, d), BF16),
                   jax.ShapeDtypeStruct((8, d), F32)],
        compiler_params=_params(("arbitrary",)))(x1, f, target, vec)


def _modulate_bwd(da, x, vec, name, da_off=0, dx_in=None, y=None):
    rows, d = x.shape
    tr = _tile(rows, 256, 8)
    off = da_off // tr
    has_dx, has_y = dx_in is not None, y is not None

    def body(*refs):
        da_ref, x_ref, v_ref = refs[0], refs[1], refs[2]
        pos = 3
        dxin_ref = refs[pos] if has_dx else None
        pos += has_dx
        y_ref = refs[pos] if has_y else None
        pos += has_y
        dx_ref = refs[pos]
        dy_ref = refs[pos + 1] if has_y else None
        acc_ref = refs[-1]

        @pl.when(pl.program_id(0) == 0)
        def _():
            acc_ref[...] = jnp.zeros_like(acc_ref)

        gain, scale1 = v_ref[0:1, :], 1.0 + v_ref[2:3, :]
        dav = da_ref[...]
        n, rstd = _rms(x_ref[...])
        dx = _rms_bwd(dav * gain * scale1, n, rstd)
        if has_dx:
            dx = dx + dxin_ref[...]
        dx_ref[...] = dx
        acc_ref[0:1, :] += _colsum(dav)
        acc_ref[1:2, :] += _colsum(dav * (n * gain))
        acc_ref[2:3, :] += _colsum(dav * n * scale1)
        if has_y:
            acc_ref[3:4, :] += _colsum(dx * y_ref[...])
            dy_ref[...] = (dx * v_ref[3:4, :]).astype(BF16)

    ins = [da, x, vec] + ([dx_in] if has_dx else []) + ([y] if has_y else [])
    in_specs = [_rspec(tr, d, off=off), _rspec(tr, d), _cspec(vec.shape)] + [_rspec(tr, d)] * (has_dx + has_y)
    out_specs = [_rspec(tr, d)] + ([_rspec(tr, d)] if has_y else []) + [_cspec((8, d))]
    out_shape = ([jax.ShapeDtypeStruct((rows, d), F32)] + ([jax.ShapeDtypeStruct((rows, d), BF16)] if has_y else [])
                 + [jax.ShapeDtypeStruct((8, d), F32)])
    return pl.pallas_call(
        body, name=name, grid=(rows // tr,), in_specs=in_specs, out_specs=out_specs, out_shape=out_shape,
        compiler_params=_params(("arbitrary",)))(*ins)


def _conv_terms(cc, cx, w_ref, tr):
    t = lax.broadcasted_iota(jnp.int32, (tr, 1), 0) % GRID_W
    first, last = t == 0, t == GRID_W - 1
    u = cc * cx
    prev = jnp.where(first, 0.0, pltpu.roll(u, 1, 0))
    nxt = jnp.where(last, 0.0, pltpu.roll(u, tr - 1, 0))
    y = w_ref[0:1, :] * prev + w_ref[1:2, :] * u + w_ref[2:3, :] * nxt
    return u, prev, nxt, y, first, last


def _conv_fwd(z, conv_w, rows, name, dep):
    d = conv_w.shape[1]
    tr = _tile(rows, 256, GRID_W)

    def body(cb_ref, cc_ref, cx_ref, w_ref, dep_ref, out_ref):
        y = _conv_terms(cc_ref[...], cx_ref[...], w_ref, tr)[3]
        out_ref[...] = (cb_ref[...] * y).astype(BF16)

    return pl.pallas_call(
        body, name=name, grid=(rows // tr,),
        in_specs=[_rspec(tr, d, CB), _rspec(tr, d, CC), _rspec(tr, d, CX), _cspec(conv_w.shape), _cspec(dep.shape)],
        out_specs=_rspec(tr, d), out_shape=jax.ShapeDtypeStruct((rows, d), BF16),
        compiler_params=_params(("parallel",)))(z, z, z, conv_w, dep)


def _conv_bwd(dpc, z, conv_w, name):
    rows, d = dpc.shape
    tr = _tile(rows, 256, GRID_W)

    def body(dpc_ref, cb_ref, cc_ref, cx_ref, w_ref, dz_ref, acc_ref):
        @pl.when(pl.program_id(0) == 0)
        def _():
            acc_ref[...] = jnp.zeros_like(acc_ref)

        cc, cx, dpcv = cc_ref[...], cx_ref[...], dpc_ref[...]
        u, prev, nxt, y, first, last = _conv_terms(cc, cx, w_ref, tr)
        dy = dpcv * cb_ref[...]
        dy_next = jnp.where(last, 0.0, pltpu.roll(dy, tr - 1, 0))
        dy_prev = jnp.where(first, 0.0, pltpu.roll(dy, 1, 0))
        du = w_ref[0:1, :] * dy_next + w_ref[1:2, :] * dy + w_ref[2:3, :] * dy_prev
        dz_ref[:, 0:d] = (dpcv * y).astype(BF16)
        dz_ref[:, d:2 * d] = (du * cx).astype(BF16)
        dz_ref[:, 2 * d:3 * d] = (du * cc).astype(BF16)
        acc_ref[0:1, :] += _colsum(dy * prev)
        acc_ref[1:2, :] += _colsum(dy * u)
        acc_ref[2:3, :] += _colsum(dy * nxt)

    return pl.pallas_call(
        body, name=name, grid=(rows // tr,),
        in_specs=[_rspec(tr, d), _rspec(tr, d, CB), _rspec(tr, d, CC), _rspec(tr, d, CX), _cspec(conv_w.shape)],
        out_specs=[_rspec(tr, 3 * d), _cspec((8, d))],
        out_shape=[jax.ShapeDtypeStruct((rows, 3 * d), BF16), jax.ShapeDtypeStruct((8, d), F32)],
        compiler_params=_params(("arbitrary",)))(dpc, z, z, z, conv_w)


def _rotary_fwd(z, cos, sin, n_lat, name):
    rows = z.shape[0]
    d = z.shape[1] // N_IN
    dk = d // HEADS
    half = dk // 2
    tr = _tile(n_lat, 256, 8)
    tr = _tile(rows - n_lat, tr, 8)
    lat_blocks, all_blocks = n_lat // tr, rows // tr
    ctx_blocks = all_blocks - lat_blocks
    scan_rows = rows + ctx_blocks * tr

    def z_block(g):
        return jnp.where(g < all_blocks, g, g - ctx_blocks)

    def in_spec(w, cb=0):
        return pl.BlockSpec((tr, w), lambda g: (z_block(g), cb))

    def out_spec(w):
        return pl.BlockSpec((tr, w), lambda g: (jnp.where(g < all_blocks, g + ctx_blocks, g - all_blocks), 0))

    def body(q_ref, k_ref, v0_ref, v1_ref, cos_ref, sin_ref, qo_ref, ko_ref, vo_ref):
        cs, sn = cos_ref[...], sin_ref[...]
        keep = jnp.where(z_block(pl.program_id(0)) < lat_blocks, 1.0, 0.0)
        for src, dst, scale in ((q_ref, qo_ref, keep), (k_ref, ko_ref, dk ** -0.5)):
            for h in range(HEADS):
                lo, mid, hi = h * dk, h * dk + half, (h + 1) * dk
                t1, t2 = src[:, lo:mid], src[:, mid:hi]
                dst[:, lo:mid] = ((t1 * cs - t2 * sn) * scale).astype(BF16)
                dst[:, mid:hi] = ((t1 * sn + t2 * cs) * scale).astype(BF16)
        vo_ref[:, 0:d] = v0_ref[...].astype(BF16)
        vo_ref[:, d:2 * d] = v1_ref[...].astype(BF16)

    return pl.pallas_call(
        body, name=name, grid=(all_blocks + ctx_blocks,),
        in_specs=[in_spec(d, CQ), in_spec(d, CK), in_spec(d, CV), in_spec(d, CV + 1), in_spec(half), in_spec(half)],
        out_specs=[out_spec(d), out_spec(d), out_spec(2 * d)],
        out_shape=[jax.ShapeDtypeStruct((scan_rows, d), BF16), jax.ShapeDtypeStruct((scan_rows, d), BF16),
                   jax.ShapeDtypeStruct((scan_rows, 2 * d), BF16)],
        compiler_params=_params(("parallel",)))(z, z, z, z, cos, sin)


def _dz_assemble(dq_f, dq_b, dk_f, dk_b, dv_f, dv_b, cos, sin, dz_conv, dz_g, dz_gc, dz_gr, n_lat, n_ctx, name):
    rows = n_lat + n_ctx
    d = dq_f.shape[1]
    dk = d // HEADS
    half = dk // 2
    tr = _tile(n_ctx, 128, 8)
    lat_blocks, ctx_blocks = n_lat // tr, n_ctx // tr

    def fmap(i):
        return (jnp.where(i < lat_blocks, i + ctx_blocks, i - lat_blocks), 0)

    def bmap(i):
        return (i + ctx_blocks, 0)

    def lmap(i):
        return (jnp.minimum(i, lat_blocks - 1), 0)

    def body(qf_ref, qb_ref, kf_ref, kb_ref, vf_ref, vb_ref, cos_ref, sin_ref, c_ref, g_ref, gc_ref, gr_ref, out_ref):
        cs, sn = cos_ref[...], sin_ref[...]
        is_lat = pl.program_id(0) < lat_blocks
        keep = jnp.where(is_lat, 1.0, 0.0)
        for fa, fb, base, scale in ((qf_ref, qb_ref, CQ * d, keep), (kf_ref, kb_ref, CK * d, dk ** -0.5)):
            for h in range(HEADS):
                lo, mid, hi = h * dk, h * dk + half, (h + 1) * dk
                d1 = fa[:, lo:mid].astype(F32) + fb[:, lo:mid].astype(F32)
                d2 = fa[:, mid:hi].astype(F32) + fb[:, mid:hi].astype(F32)
                out_ref[:, base + lo:base + mid] = ((d1 * cs + d2 * sn) * scale).astype(BF16)
                out_ref[:, base + mid:base + hi] = ((d2 * cs - d1 * sn) * scale).astype(BF16)
        out_ref[:, CV * d:CG * d] = (vf_ref[...].astype(F32) + vb_ref[...].astype(F32)).astype(BF16)
        for src, lo, hi in ((c_ref, CB * d, CQ * d), (g_ref, CG * d, CGC * d), (gc_ref, CGC * d, CGR * d),
                            (gr_ref, CGR * d, N_IN * d)):
            out_ref[:, lo:hi] = jnp.where(is_lat, src[...], jnp.zeros_like(src))

    return pl.pallas_call(
        body, name=name, grid=(rows // tr,),
        in_specs=[pl.BlockSpec((tr, d), fmap), pl.BlockSpec((tr, d), bmap), pl.BlockSpec((tr, d), fmap),
                  pl.BlockSpec((tr, d), bmap), pl.BlockSpec((tr, 2 * d), fmap), pl.BlockSpec((tr, 2 * d), bmap),
                  _rspec(tr, half), _rspec(tr, half), pl.BlockSpec((tr, 3 * d), lmap), pl.BlockSpec((tr, 2 * d), lmap),
                  pl.BlockSpec((tr, d), lmap), pl.BlockSpec((tr, d), lmap)],
        out_specs=_rspec(tr, N_IN * d), out_shape=jax.ShapeDtypeStruct((rows, N_IN * d), BF16),
        compiler_params=_params(("parallel",)))(dq_f, dq_b, dk_f, dk_b, dv_f, dv_b, cos, sin, dz_conv, dz_g, dz_gc, dz_gr)


def _gn_fwd(o, z, name):
    rows = o.shape[0]
    d = z.shape[1] // N_IN
    dv = 2 * d // HEADS
    tr = _tile(rows, 128, 8)

    def body(o_ref, g0_ref, g1_ref, r_ref):
        for h in range(HEADS):
            lo, hi = h * dv, (h + 1) * dv
            g_ref, glo = (g0_ref, lo) if hi <= d else (g1_ref, lo - d)
            o = o_ref[:, lo:hi]
            cen = o - jnp.mean(o, axis=-1, keepdims=True)
            on = cen * lax.rsqrt(jnp.mean(cen * cen, axis=-1, keepdims=True) + EPS)
            g = g_ref[:, glo:glo + dv]
            r_ref[:, lo:hi] = (g * _sig(g) * on).astype(BF16)

    return pl.pallas_call(
        body, name=name, grid=(rows // tr,),
        in_specs=[_rspec(tr, 2 * d), _rspec(tr, d, CG), _rspec(tr, d, CG + 1)],
        out_specs=_rspec(tr, 2 * d), out_shape=jax.ShapeDtypeStruct((rows, 2 * d), BF16),
        compiler_params=_params(("parallel",)))(o, z, z)


def _gn_bwd_head(dr, o, g):
    cen = o - jnp.mean(o, axis=-1, keepdims=True)
    rstd = lax.rsqrt(jnp.mean(cen * cen, axis=-1, keepdims=True) + EPS)
    on = cen * rstd
    sg = _sig(g)
    don = dr * (g * sg)
    do = rstd * (don - jnp.mean(don, axis=-1, keepdims=True) - on * jnp.mean(don * on, axis=-1, keepdims=True))
    return do, dr * on * (sg * (1.0 + g * (1.0 - sg)))


def _decays(lg, rev):
    row = lax.broadcasted_iota(jnp.int32, (CHUNK, CHUNK), 0)
    col = lax.broadcasted_iota(jnp.int32, (CHUNK, CHUNK), 1)
    rel = ((col - row) if rev else (row - col)).astype(F32)
    mask = jnp.where(rel >= 0, jnp.exp(lg * jnp.maximum(rel, 0.0)), 0.0)
    r = lax.broadcasted_iota(jnp.int32, (CHUNK, 1), 0)
    rr = ((CHUNK - 1 - r) if rev else r).astype(F32)
    chunk_decay = jnp.exp(lg * jnp.full((1, 1), float(CHUNK), F32))
    return (rr, rel), mask, jnp.exp(lg * (rr + 1.0)), jnp.exp(lg * (CHUNK - 1.0 - rr)), chunk_decay


_NT = (((1,), (1,)), ((), ()))
_TN = (((0,), (0,)), ((), ()))


def _dot(a, b, dims=None):
    if dims is None:
        return jnp.dot(a, b, preferred_element_type=F32)
    return lax.dot_general(a, b, dims, preferred_element_type=F32)


def _ret_fwd(q, k, v, lgs, n_lat, name):
    tt, d = q.shape
    dk, dv = d // HEADS, 2 * d // HEADS
    nc = tt // CHUNK
    lat_chunks = n_lat // CHUNK
    ctx_chunks = (nc - lat_chunks) // 2

    def chunk_of(rev, i):
        return (nc - 1 - i) if rev else i

    def cmap(rev):
        return lambda h, i: (chunk_of(rev, i), h)

    def smap(rev):
        return lambda h, i: (chunk_of(rev, i), h, 0, 0)

    def body(lg_ref, qf_ref, kf_ref, vf_ref, qb_ref, kb_ref, vb_ref, o_ref, sf_ref, sb_ref, state_f, state_b):
        h, i = pl.program_id(0), pl.program_id(1)

        @pl.when(i == 0)
        def _():
            state_f[...] = jnp.zeros_like(state_f)
            state_b[...] = jnp.zeros_like(state_b)

        for rev, (q_ref, k_ref, v_ref, s_ref, state) in enumerate((
                (qf_ref, kf_ref, vf_ref, sf_ref, state_f), (qb_ref, kb_ref, vb_ref, sb_ref, state_b))):
            _, mask, qd, kd, cd = _decays(lg_ref[rev, h], bool(rev))
            qv, kv, vv = q_ref[...], k_ref[...], v_ref[...]
            st = state[...]
            p = _dot(qv, kv, _NT) * mask
            out = _dot(p.astype(BF16), vv) + _dot((qv * qd).astype(BF16), st.astype(BF16))
            s_ref[...] = st
            state[...] = cd * st + _dot((kv * kd).astype(BF16), vv, _TN)
            lat_chunk = chunk_of(bool(rev), i) - ctx_chunks
            is_lat = (lat_chunk >= 0) & (lat_chunk < lat_chunks)
            first = (2 * i < nc - 1) if rev else (2 * i <= nc - 1)

            @pl.when(is_lat & first)
            def _(lat_chunk=lat_chunk, out=out):
                o_ref[pl.ds(pl.multiple_of(lat_chunk * CHUNK, CHUNK), CHUNK), :] = out

            @pl.when(is_lat & jnp.logical_not(first))
            def _(lat_chunk=lat_chunk, out=out):
                o_ref[pl.ds(pl.multiple_of(lat_chunk * CHUNK, CHUNK), CHUNK), :] += out

    def specs(rev):
        return [pl.BlockSpec((CHUNK, dk), cmap(rev)), pl.BlockSpec((CHUNK, dk), cmap(rev)),
                pl.BlockSpec((CHUNK, dv), cmap(rev))]

    state_shape = jax.ShapeDtypeStruct((nc, HEADS, dk, dv), F32)
    return pl.pallas_call(
        body, name=name, grid=(HEADS, nc),
        in_specs=[pl.BlockSpec(memory_space=pltpu.SMEM)] + specs(False) + specs(True),
        out_specs=[pl.BlockSpec((n_lat, dv), lambda h, i: (0, h)), pl.BlockSpec((None, None, dk, dv), smap(False)),
                   pl.BlockSpec((None, None, dk, dv), smap(True))],
        out_shape=[jax.ShapeDtypeStruct((n_lat, 2 * d), F32), state_shape, state_shape],
        scratch_shapes=[pltpu.VMEM((dk, dv), F32), pltpu.VMEM((dk, dv), F32)],
        compiler_params=_params(("parallel", "arbitrary")))(lgs, q, k, v, q, k, v)


def _ret_bwd(q, k, v, do, states_f, states_b, lgs, name):
    tt, d = q.shape
    dk, dv = d // HEADS, 2 * d // HEADS
    nc = tt // CHUNK
    lat_chunks = do.shape[0] // CHUNK
    ctx_chunks = (nc - lat_chunks) // 2

    def chunk_of(rev, i):
        return i if rev else nc - 1 - i

    def cmap(rev):
        return lambda h, i: (chunk_of(rev, i), h)

    def do_map(rev):
        return lambda h, i: (jnp.clip(chunk_of(rev, i) - ctx_chunks, 0, lat_chunks - 1), h)

    def smap(rev):
        return lambda h, i: (chunk_of(rev, i), h, 0, 0)

    pair = 2

    def body(lg_ref, *refs):
        hh = pl.program_id(0)
        dstates = refs[-2:]

        @pl.when(pl.program_id(1) == 0)
        def _():
            for rev in (0, 1):
                dstates[rev][...] = jnp.zeros_like(dstates[rev])
                refs[10 + 4 * rev + 3][...] = jnp.zeros_like(refs[10 + 4 * rev + 3])

        for rev in (0, 1):
            q_ref, k_ref, v_ref, do_ref, s_ref = refs[5 * rev:5 * rev + 5]
            dq_ref, dk_ref, dv_ref, dlg_ref = refs[10 + 4 * rev:10 + 4 * rev + 4]
            chunk = chunk_of(bool(rev), pl.program_id(1))
            is_lat = (chunk >= ctx_chunks) & (chunk < ctx_chunks + lat_chunks)
            for e in range(pair):
                ks, vs = slice(e * dk, (e + 1) * dk), slice(e * dv, (e + 1) * dv)
                (rr, rel), mask, qd, kd, cd = _decays(lg_ref[rev, pair * hh + e], bool(rev))
                qv, kv, vv = q_ref[:, ks], k_ref[:, ks], v_ref[:, vs]
                dov = jnp.where(is_lat, do_ref[:, vs], jnp.zeros((CHUNK, dv), BF16))
                st, dst = s_ref[e], dstates[rev][e]
                st_b, dst_b = st.astype(BF16), dst.astype(BF16)
                p = _dot(qv, kv, _NT) * mask
                dp = _dot(dov, vv, _NT)
                da = (dp * mask).astype(BF16)
                dq_state = _dot(dov, st_b, _NT) * qd
                dk_state = _dot(vv, dst_b, _NT) * kd
                dq_ref[:, ks] = (_dot(da, kv) + dq_state).astype(BF16)
                dk_ref[:, ks] = (_dot(da, qv, _TN) + dk_state).astype(BF16)
                dv_ref[:, vs] = (_dot(p.astype(BF16), dov, _TN) + _dot((kv * kd).astype(BF16), dst_b)).astype(BF16)
                dnew = cd * dst + _dot((qv * qd).astype(BF16), dov, _TN)
                dstates[rev][e] = dnew
                through = rr * (jnp.sum(qv.astype(F32) * dq_state, axis=1, keepdims=True)
                                - jnp.sum(kv.astype(F32) * dk_state, axis=1, keepdims=True))
                dlg_ref[e] += _total(rel * p * dp) + _total(through) + CHUNK * _total(st * dnew)

    def specs(rev):
        return [pl.BlockSpec((CHUNK, pair * dk), cmap(rev)), pl.BlockSpec((CHUNK, pair * dk), cmap(rev)),
                pl.BlockSpec((CHUNK, pair * dv), cmap(rev)), pl.BlockSpec((CHUNK, pair * dv), do_map(rev)),
                pl.BlockSpec((None, pair, dk, dv), smap(rev))]

    def outs(rev):
        return [pl.BlockSpec((CHUNK, pair * dk), cmap(rev)), pl.BlockSpec((CHUNK, pair * dk), cmap(rev)),
                pl.BlockSpec((CHUNK, pair * dv), cmap(rev)), pl.BlockSpec((pair, 8, 128), lambda h, i: (h, 0, 0))]

    return pl.pallas_call(
        body, name=name, grid=(HEADS // pair, nc),
        in_specs=[pl.BlockSpec(memory_space=pltpu.SMEM)] + specs(False) + specs(True),
        out_specs=outs(False) + outs(True),
        out_shape=[jax.ShapeDtypeStruct((tt, d), BF16), jax.ShapeDtypeStruct((tt, d), BF16),
                   jax.ShapeDtypeStruct((tt, 2 * d), BF16), jax.ShapeDtypeStruct((HEADS, 8, 128), F32)] * 2,
        scratch_shapes=[pltpu.VMEM((pair, dk, dv), F32), pltpu.VMEM((pair, dk, dv), F32)],
        compiler_params=_params(("parallel", "arbitrary")))(lgs, q, k, v, do, states_f, q, k, v, do, states_b)


def _silu(v):
    return v * _sig(v)


def _mod_proj(cs, w_mod, b_loc, dec, name):
    nb = w_mod.shape[1]

    def body(cs_ref, w_ref, b_ref, dec_ref, out_ref, lg_ref):
        out_ref[...] = jnp.dot(_silu(cs_ref[...]), w_ref[...], preferred_element_type=F32, precision=HIGHEST) + b_ref[...]
        a = dec_ref[...]
        lg_ref[...] = jnp.minimum(a, 0.0) - jnp.log1p(jnp.exp(-jnp.abs(a)))

    return pl.pallas_call(
        body, name=name,
        out_shape=[jax.ShapeDtypeStruct((16, nb), F32), jax.ShapeDtypeStruct(dec.shape, F32)],
        compiler_params=_params())(cs, w_mod, b_loc, dec)


def _mod_grad(cs, dm, w_mod, name):
    d, nb = w_mod.shape

    def body(cs_ref, dm_ref, w_ref, gw_ref, part_ref):
        dmv = dm_ref[...]
        gw_ref[...] = lax.dot_general(_silu(cs_ref[...]), dmv, _TN, preferred_element_type=F32, precision=HIGHEST)
        part_ref[...] = lax.dot_general(dmv, w_ref[...], _NT, preferred_element_type=F32, precision=HIGHEST)

    return pl.pallas_call(
        body, name=name,
        out_shape=[jax.ShapeDtypeStruct((d, nb), F32), jax.ShapeDtypeStruct((16, d), F32)],
        compiler_params=_params())(cs, dm, w_mod)


def _reduce_small(gathered, dec, n_feat, name):
    _, rows, d = gathered.shape

    def body(g_ref, dec_ref, red_ref, misc_ref):
        total = g_ref[0]
        for i in range(1, N_DEV):
            total = total + g_ref[i]
        red_ref[...] = total
        misc_ref[...] = jnp.zeros_like(misc_ref)
        misc_ref[0:1, :] = jnp.zeros((1, 128), F32) + (0.5 / n_feat) * _total(total[16:17, :])
        misc_ref[1:3, :] = total[14:16, 0:128] * _sig(-dec_ref[0:2, :])

    return pl.pallas_call(
        body, name=name,
        out_shape=[jax.ShapeDtypeStruct((rows, d), F32), jax.ShapeDtypeStruct((8, 128), F32)],
        compiler_params=_params())(gathered, dec)


def _c_ctx_grad(parts, c_ctx, name):
    d = c_ctx.shape[1]

    def body(p_ref, c_ref, out_ref):
        total = p_ref[0]
        for i in range(1, N_DEV):
            total = total + p_ref[i]
        cv = c_ref[...]
        sg = _sig(cv)
        out_ref[...] = total[8:9, :] * (sg * (1.0 + cv * (1.0 - sg)))

    return pl.pallas_call(body, name=name, out_shape=jax.ShapeDtypeStruct((1, d), F32),
                          compiler_params=_params())(parts, c_ctx)


def _adamw(parts, w, m, v, name, own=None):
    n_parts, rows, cols = parts.shape
    row_bytes = cols * (parts.dtype.itemsize * (n_parts + 1) + 7 * 4)
    tr = _tile(rows, max(16, (8 * 1024 * 1024) // row_bytes), 16 if rows % 16 == 0 else 8)
    blocks = rows // tr

    def body(*refs):
        p_ref, w_ref, m_ref, v_ref, g_ref, d_ref, mo_ref, vo_ref = refs[-8:]
        if own is None:
            g = p_ref[0].astype(F32)
        else:
            g = refs[1][...].astype(F32) + p_ref[0].astype(F32)
        for i in range(1, n_parts):
            g = g + p_ref[i].astype(F32)
        m2 = ADAM_B1 * m_ref[...] + (1.0 - ADAM_B1) * g
        v2 = ADAM_B2 * v_ref[...] + (1.0 - ADAM_B2) * jnp.square(g)
        m_hat = m2 / (1.0 - ADAM_B1 ** ADAM_STEP)
        v_hat = v2 / (1.0 - ADAM_B2 ** ADAM_STEP)
        g_ref[...] = g
        d_ref[...] = -ADAM_LR * (m_hat / (jnp.sqrt(v_hat) + ADAM_EPS) + ADAM_WD * w_ref[...])
        mo_ref[...] = m2
        vo_ref[...] = v2

    out_shape = [jax.ShapeDtypeStruct((rows, cols), F32)] * 4
    if own is None:
        spec = _rspec(tr, cols)
        return pl.pallas_call(
            body, name=name, grid=(blocks,),
            in_specs=[pl.BlockSpec((n_parts, tr, cols), lambda i: (0, i, 0)), spec, spec, spec],
            out_specs=[spec] * 4, out_shape=out_shape, compiler_params=_params(("parallel",)))(parts, w, m, v)
    g_full, axis, me = own

    def own_map(i, me_ref):
        return (me_ref[0] * blocks + i, 0) if axis == 0 else (i, me_ref[0])

    spec = pl.BlockSpec((tr, cols), lambda i, me_ref: (i, 0))
    return pl.pallas_call(
        body, name=name, out_shape=out_shape,
        grid_spec=pltpu.PrefetchScalarGridSpec(
            num_scalar_prefetch=1, grid=(blocks,),
            in_specs=[pl.BlockSpec((tr, cols), own_map),
                      pl.BlockSpec((n_parts, tr, cols), lambda i, me_ref: (0, i, 0)), spec, spec, spec],
            out_specs=[spec] * 4),
        compiler_params=_params(("parallel",)))(me.reshape(1), g_full, parts, w, m, v)


def _rope_tables(pos, dk):
    half = dk // 2
    inv_freq = 1.0 / (ROPE_BASE ** jnp.linspace(0.0, 1.0, half, dtype=F32))
    ang = pos[:, None] * inv_freq[None, :]
    return jnp.cos(ang), jnp.sin(ang)


def _pad_lanes(v, width):
    return jnp.pad(v, ((0, 0), (0, width - v.shape[1])))


def kernel(x, c, ctx, c_ctx, w_mod, b_mod, norm1_g, w_in, conv_w, w_conv_out, ret_decay_fwd, ret_decay_bwd, w_ret_out, w_o, norm2_g, w_ff1, w_ff2, final_g, loss_target, m_c_ctx, m_w_mod, m_b_mod, m_norm1_g, m_w_in, m_conv_w, m_w_conv_out, m_ret_decay_fwd, m_ret_decay_bwd, m_w_ret_out, m_w_o, m_norm2_g, m_w_ff1, m_w_ff2, m_final_g, v_c_ctx, v_w_mod, v_b_mod, v_norm1_g, v_w_in, v_conv_w, v_w_conv_out, v_ret_decay_fwd, v_ret_decay_bwd, v_w_ret_out, v_w_o, v_norm2_g, v_w_ff1, v_w_ff2, v_final_g):
    n_lat, d = x.shape[1], x.shape[2]
    n_ctx = ctx.shape[1]
    assert n_ctx % CHUNK == 0 and n_lat % CHUNK == 0 and n_lat % GRID_W == 0
    dk = d // HEADS
    nb = w_mod.shape[2]
    me = 4 * lax.axis_index("x") + 2 * lax.axis_index("y") + lax.axis_index("c")
    xl, ctxl, target = x[0], ctx[0], loss_target[0]

    first = _small_allgather(jnp.concatenate([c, _pad_lanes(conv_w[0], d), jnp.zeros((4, d), F32)], axis=0), "ag_c")
    conv_w_f = first[:, 1:4, :d // N_DEV].transpose(1, 0, 2).reshape(3, d)

    c_all = first[:, 0, :]
    cs = jnp.concatenate([c_all, c_ctx[None], jnp.zeros((7, d), F32)], axis=0)
    dec = jnp.pad(jnp.concatenate([ret_decay_fwd, ret_decay_bwd], axis=0), ((0, 6), (0, 128 - HEADS)))
    b_loc = lax.dynamic_slice(b_mod, (0, me * nb), (1, nb))
    modp, lgs = _mod_proj(cs, w_mod[0], b_loc, dec, "mod_proj")
    modp_all = _small_allgather(modp, "ag_mod")
    mod_l = lax.dynamic_index_in_dim(modp_all, me, axis=1, keepdims=False).reshape(N_MOD, d)
    mod_c = modp_all[:, 8, :].reshape(N_MOD, d)
    lgs = lgs[0:2, :HEADS]
    zero_row = jnp.zeros((1, d), F32)
    vec1_l = jnp.concatenate([norm1_g, mod_l[0:1], mod_l[1:2], zero_row], axis=0)
    vec1_c = jnp.concatenate([norm1_g, mod_c[0:1], mod_c[1:2], zero_row], axis=0)
    vec2 = jnp.concatenate([norm2_g, mod_l[3:4], mod_l[4:5], mod_l[2:3]], axis=0)
    vec3 = jnp.concatenate([mod_l[5:6], final_g[None]], axis=0)

    a_all = _modulate(ctxl, vec1_c, "modulate1_ctx", n_lat + n_ctx,
                      into=_modulate(xl, vec1_l, "modulate1", n_lat + n_ctx))
    arrival = me ^ jnp.array([0, 1, 4, 5, 2, 3, 6, 7], jnp.int32)
    riders = [(_place_shard(w[0], 0, me, "place_" + wname), 0)
              for wname, w in (("w_conv_out", w_conv_out), ("w_ret_out", w_ret_out), ("w_o", w_o))]
    w_in_f, w_conv_out_f, w_ret_out_f, w_o_f, z = _gather_project(
        a_all, _place_shard(w_in[0], 1, me, "place_w_in"), arrival, "proj_in", riders=riders)

    later, after = {}, (z, conv_w_f)
    for wname, w, axis in (("w_ff1", w_ff1, 1), ("w_ff2", w_ff2, 0)):
        later[wname], token = _exchange_start(_place_shard(w[0], axis, me, "place_" + wname), axis, True,
                                              "ag_" + wname + "_start", after=after)
        after = (token,)
    pc = _conv_fwd(z, conv_w_f, n_lat, "conv_fwd", token)
    pos = jnp.concatenate([n_ctx + jnp.arange(n_lat, dtype=F32), jnp.arange(n_ctx, dtype=F32)])
    cos, sin = _rope_tables(pos, dk)
    q_s, k_s, v_s = _rotary_fwd(z, cos, sin, n_lat, "rotary_fwd")
    o, st_f, st_b = _ret_fwd(q_s, k_s, v_s, lgs, n_lat, "ret_fwd")
    r = _gn_fwd(o, z, "gn_fwd")
    y_conv = _mm(pc, w_conv_out_f, "nn", [F32], "proj_conv_out", tn=2048)
    tn_d = _tile(d, 1024, 128)
    gate_offs = (CGC * d // tn_d, CGR * d // tn_d)

    def merge(acc, yc, gc, gr):
        return acc, _sig(gc) * yc + _sig(gr) * acc

    y_ret, mg = _mm(r, w_ret_out_f, "nn", [F32, BF16], "proj_ret_out", tm=512, tn=tn_d, tk=2 * d,
                    extras=[(y_conv, 0), (z, gate_offs[0]), (z, gate_offs[1])], epilogue=merge)
    y_l = _mm(mg, w_o_f, "nn", [F32], "proj_o", tn=2048)
    x1, a2 = _resid_modulate(xl, y_l, vec2, "resid_modulate2")

    def sqrelu(acc):
        return acc, jnp.square(jnp.maximum(acc, 0.0))

    w_ff1_f = _exchange_wait(later["w_ff1"], a2, "ag_w_ff1_wait")
    hff, s = _mm(a2, w_ff1_f, "nn", [BF16, BF16], "ff1", tn=2048, epilogue=sqrelu)
    w_ff2_f = _exchange_wait(later["w_ff2"], s, "ag_w_ff2_wait")
    f = _mm(s, w_ff2_f, "nn", [F32], "ff2")
    dx2, df, acc3 = _loss_head(x1, f, target, vec3, "loss_head")

    def d_sqrelu(acc, h):
        return (acc * (2.0 * jnp.maximum(h.astype(F32), 0.0)),)

    dh = _mm(df, w_ff2_f, "nt", [BF16], "ff2_dx", extras=[(hff, 0)], epilogue=d_sqrelu)
    sent = {}
    sent["w_ff2"], token = _exchange_start(_mm(s, df, "tn", [BF16], "ff2_dw", tn=2048), 0, False, "rs_w_ff2_start")
    da2 = _mm(dh, w_ff1_f, "nt", [F32], "ff1_dx", dep=token)
    sent["w_ff1"], token = _exchange_start(_mm(a2, dh, "tn", [BF16], "ff1_dw", tn=2048), 1, False, "rs_w_ff1_start")
    dx1, dyl, acc2 = _modulate_bwd(da2, x1, vec2, "modulate2_bwd", dx_in=dx2, y=y_l)

    def d_merge(acc, yc, yr, gc, gr):
        sc, sr = _sig(gc), _sig(gr)
        return acc * sc, acc * sr, acc * yc * (sc * (1.0 - sc)), acc * yr * (sr * (1.0 - sr))

    dyc, dyr, dgc, dgr = _mm(dyl, w_o_f, "nt", [BF16] * 4, "proj_o_dx", tm=512, tn=tn_d,
                             extras=[(y_conv, 0), (y_ret, 0), (z, gate_offs[0]), (z, gate_offs[1])], epilogue=d_merge,
                             dep=token)
    sent["w_o"], token = _exchange_start(_mm(mg, dyl, "tn", [BF16], "proj_o_dw", tn=2048), 0, False, "rs_w_o_start")
    dpc = _mm(dyc, w_conv_out_f, "nt", [F32], "proj_conv_out_dx", tn=2048, dep=token)
    sent["w_conv_out"], token = _exchange_start(_mm(pc, dyc, "tn", [BF16], "proj_conv_out_dw", tn=2048), 0, False,
                                                "rs_w_conv_out_start")
    dz_conv, acc_conv = _conv_bwd(dpc, z, conv_w_f, "conv_bwd")
    dv_head = 2 * d // HEADS
    do, dz_g = _mm(dyr, w_ret_out_f, "nt", [BF16, BF16], "proj_ret_out_dx", tn=dv_head,
                   extras=[(o, 0), (z, CG * d // dv_head)], epilogue=_gn_bwd_head, dep=token)
    sent["w_ret_out"], token = _exchange_start(_mm(r, dyr, "tn", [BF16], "proj_ret_out_dw", tn=2048), 0, False,
                                               "rs_w_ret_out_start")
    dq_f, dk_f, dv_f, dlg_f, dq_b, dk_b, dv_b, dlg_b = _ret_bwd(q_s, k_s, v_s, do, st_f, st_b, lgs, "ret_bwd")
    dz = _dz_assemble(dq_f, dq_b, dk_f, dk_b, dv_f, dv_b, cos, sin, dz_conv, dz_g, dgc, dgr, n_lat, n_ctx,
                      "dz_assemble")
    g_in = _mm(a_all, dz, "tn", [BF16], "proj_in_dw", tn=2048, tk=(n_lat + n_ctx) // 4, dep=token)
    sent["w_in"], token = _exchange_start(g_in, 1, False, "rs_w_in_start")
    da_all = _mm(dz, w_in_f, "nt", [F32], "proj_in_dx", tn=2048, dep=token)
    grad_x, acc1 = _modulate_bwd(da_all, xl, vec1_l, "modulate1_bwd", dx_in=dx1)
    _, acc1c = _modulate_bwd(da_all, ctxl, vec1_c, "modulate1_ctx_bwd", da_off=n_lat)

    lane_pad = functools.partial(_pad_lanes, width=d)
    packet = jnp.concatenate([
        acc1[2:3] + acc1c[2:3], acc2[2:3], acc3[1:2],
        acc1[0:1], acc1[1:2], acc2[3:4], acc2[0:1], acc2[1:2], acc3[0:1],
        acc1c[0:1], acc1c[1:2],
        acc_conv[0:3],
        lane_pad(dlg_f[:, 0, 0][None]), lane_pad(dlg_b[:, 0, 0][None]),
        acc3[2:3],
        jnp.zeros((7, d), F32)], axis=0)
    n_rows = packet.shape[0]
    packets = lax.dynamic_update_slice(jnp.zeros((N_DEV * n_rows, d), F32), packet, (me * n_rows, 0))
    packets_sent, after = _exchange_start(packets, 0, True, "ag_small_start")
    res = {}

    def update(wname, axis, w, m, v, after):
        g_full, parts = _exchange_wait(sent[wname], after, "rs_" + wname + "_wait")
        res[wname] = _adamw(parts, w[0], m[0], v[0], "adamw_" + wname, own=(g_full, axis, me))
        return res[wname][0]

    for wname, axis, w, m, v in (("w_ff2", 0, w_ff2, m_w_ff2, v_w_ff2), ("w_ff1", 1, w_ff1, m_w_ff1, v_w_ff1),
                                 ("w_o", 0, w_o, m_w_o, v_w_o), ("w_conv_out", 0, w_conv_out, m_w_conv_out, v_w_conv_out),
                                 ("w_ret_out", 0, w_ret_out, m_w_ret_out, v_w_ret_out)):
        after = update(wname, axis, w, m, v, after)
    gathered = _exchange_wait(packets_sent, after, "ag_small_wait").reshape(N_DEV, n_rows, d)
    red, misc = _reduce_small(gathered, dec, d, "reduce_small")
    dmod_ctx = jnp.concatenate([red[9], red[10], jnp.zeros((4 * d,), F32)])
    dmod_all = jnp.concatenate([gathered[:, 3:9, :].reshape(N_DEV, N_MOD * d), dmod_ctx[None]], axis=0)
    dm = jnp.pad(lax.dynamic_slice(dmod_all, (0, me * nb), (N_DEV + 1, nb)), ((0, 7), (0, 0)))
    g_mod, c_part = _mod_grad(cs, dm, w_mod[0], "mod_grad")
    g_b_mod = red[3:9].reshape(1, N_MOD * d) + dmod_ctx[None]
    g_conv_w = lax.dynamic_slice(red[11:14], (0, me * (d // N_DEV)), (3, d // N_DEV))

    c_parts = lax.dynamic_update_slice(jnp.zeros((N_DEV * 16, d), F32), c_part, (me * 16, 0))
    c_parts_sent, token = _exchange_start(c_parts, 0, True, "ag_c_ctx_start")
    res["w_mod"] = _adamw(g_mod[None], w_mod[0], m_w_mod[0], v_w_mod[0], "adamw_w_mod")
    after = update("w_in", 1, w_in, m_w_in, v_w_in, token)
    c_parts = _exchange_wait(c_parts_sent, after, "ag_c_ctx_wait").reshape(N_DEV, 16, d)
    g_c_ctx = _c_ctx_grad(c_parts, c_ctx[None], "c_ctx_grad")
    res = {k: tuple(t[None] for t in val) for k, val in res.items()}

    small = [("c_ctx", g_c_ctx, c_ctx, m_c_ctx, v_c_ctx), ("b_mod", g_b_mod, b_mod, m_b_mod, v_b_mod),
             ("norm1_g", red[0:1], norm1_g, m_norm1_g, v_norm1_g), ("conv_w", g_conv_w, conv_w, m_conv_w, v_conv_w),
             ("ret_decay_fwd", misc[1:2, :HEADS], ret_decay_fwd, m_ret_decay_fwd, v_ret_decay_fwd),
             ("ret_decay_bwd", misc[2:3, :HEADS], ret_decay_bwd, m_ret_decay_bwd, v_ret_decay_bwd),
             ("norm2_g", red[1:2], norm2_g, m_norm2_g, v_norm2_g), ("final_g", red[2:3], final_g, m_final_g, v_final_g)]

    def flat(t):
        t = t.reshape(-1)
        return jnp.pad(t, (0, (-t.shape[0]) % 1024))

    packed = [jnp.concatenate([flat(item[j]) for item in small]).reshape(-1, 128) for j in range(1, 5)]
    outs = _adamw(packed[0][None], packed[1], packed[2], packed[3], "adamw_small")
    start = 0
    for name, _, w, _, _ in small:
        size = w.size
        res[name] = tuple(o.reshape(-1)[start:start + size].reshape(w.shape) for o in outs)
        start += size + (-size) % 1024

    order = ["c_ctx", "w_mod", "b_mod", "norm1_g", "w_in", "conv_w", "w_conv_out", "ret_decay_fwd", "ret_decay_bwd",
             "w_ret_out", "w_o", "norm2_g", "w_ff1", "w_ff2", "final_g"]
    loss = misc[0, 0]
    return (loss, grad_x[None], *[res[n][0] for n in order], *[res[n][1] for n in order],
            *[res[n][2] for n in order], *[res[n][3] for n in order])
```

```python
import functools

import jax
import jax.numpy as jnp
from jax import lax
from jax.experimental import pallas as pl
from jax.experimental.pallas import tpu as pltpu

F32 = jnp.float32
BF16 = jnp.bfloat16
MESH = pl.DeviceIdType.MESH

N_DEV = 8
HEADS = 8
N_MOD = 6
N_IN = 11
GRID_W = 64
CHUNK = 256
ROPE_BASE = 10000.0
EPS = 1e-6
ADAM_LR, ADAM_B1, ADAM_B2, ADAM_EPS, ADAM_WD, ADAM_STEP = 0.001, 0.9, 0.999, 1e-08, 0.01, 10
VMEM_LIMIT = 56 * 1024 * 1024
HIGHEST = lax.Precision.HIGHEST
CB, CC, CX, CQ, CK, CV, CG, CGC, CGR = 0, 1, 2, 3, 4, 5, 7, 9, 10


def _tile(n, target, mult):
    t = (min(target, n) // mult) * mult
    while t >= mult:
        if n % t == 0:
            return t
        t -= mult
    return n


def _params(sem=None):
    return pltpu.CompilerParams(dimension_semantics=sem, vmem_limit_bytes=VMEM_LIMIT)


def _sig(v):
    return 1.0 / (1.0 + jnp.exp(-v))


def _coords():
    return lax.axis_index("x"), lax.axis_index("y"), lax.axis_index("c")


def _flip(p, m):
    return tuple(1 - v if (m >> s) & 1 else v for v, s in zip(p, (2, 1, 0)))


def _index(p):
    return 4 * p[0] + 2 * p[1] + p[2]


def _small_allgather(x, name):
    r, n = x.shape

    def body(x_ref, out_ref, send_sems, recv_sems):
        me = _coords()
        out_ref[pl.ds(_index(me), 1)] = x_ref[...][None]
        sent = []
        for m in range(1, N_DEV):
            cp = pltpu.make_async_remote_copy(
                src_ref=x_ref, dst_ref=out_ref.at[_index(me)], send_sem=send_sems.at[m - 1],
                recv_sem=recv_sems.at[m - 1], device_id=_flip(me, m), device_id_type=MESH)
            cp.start()
            sent.append(cp)
        for m in range(1, N_DEV):
            pltpu.make_async_remote_copy(
                src_ref=x_ref, dst_ref=out_ref.at[_index(_flip(me, m))], send_sem=send_sems.at[m - 1],
                recv_sem=recv_sems.at[m - 1], device_id=_flip(me, m), device_id_type=MESH).wait_recv()
        for cp in sent:
            cp.wait_send()

    return pl.pallas_call(
        body, name=name, out_shape=jax.ShapeDtypeStruct((N_DEV, r, n), x.dtype),
        in_specs=[pl.BlockSpec(memory_space=pltpu.VMEM)], out_specs=pl.BlockSpec(memory_space=pltpu.VMEM),
        scratch_shapes=[pltpu.SemaphoreType.DMA((N_DEV - 1,)), pltpu.SemaphoreType.DMA((N_DEV - 1,))],
    )(x)


def _window(ref, j, r, c, axis):
    if axis == 0:
        return ref.at[pl.ds(pl.multiple_of(j * r, 8), r), :]
    return ref.at[:, pl.ds(pl.multiple_of(j * c, 128), c)]


_HBM = pl.BlockSpec(memory_space=pltpu.HBM)
_SEM = pl.BlockSpec(memory_space=pltpu.SEMAPHORE)
_ANY = pl.BlockSpec(memory_space=pl.ANY)
_EFFECT = pltpu.SideEffectType.DATAFLOW_SIDE_EFFECTING


def _place_shard(w, axis, me, name):
    r, c = w.shape
    full = (N_DEV * r, c) if axis == 0 else (r, N_DEV * c)
    tr = _tile(r, max(16, (2 * 1024 * 1024) // (c * 4)), 16)
    blocks = r // tr

    def out_map(i, me_ref):
        return (me_ref[0] * blocks + i, 0) if axis == 0 else (i, me_ref[0])

    def body(me_ref, w_ref, out_ref):
        out_ref[...] = w_ref[...].astype(BF16)

    return pl.pallas_call(
        body, name=name, out_shape=jax.ShapeDtypeStruct(full, BF16),
        grid_spec=pltpu.PrefetchScalarGridSpec(
            num_scalar_prefetch=1, grid=(blocks,), in_specs=[pl.BlockSpec((tr, c), lambda i, me_ref: (i, 0))],
            out_specs=pl.BlockSpec((tr, c), out_map)),
        compiler_params=_params(("parallel",)))(me.reshape(1), w)


def _exchange_start(src, axis, gather, name, after=()):
    r, c = (src.shape[0] // N_DEV, src.shape[1]) if axis == 0 else (src.shape[0], src.shape[1] // N_DEV)
    n_hbm = 1 if gather else 2
    n_after = len(after)

    def body(*refs):
        src_ref, land_ref = refs[0], refs[n_hbm - 1]
        send_sems, recv_sems = refs[n_hbm + n_after:n_hbm + n_after + 2]
        token = refs[-1]
        me = _coords()
        for m in range(1, N_DEV):
            peer = _flip(me, m)
            if gather:
                mine = theirs = _window(land_ref, _index(me), r, c, axis)
            else:
                mine, theirs = _window(src_ref, _index(peer), r, c, axis), land_ref.at[m - 1]
            pltpu.make_async_remote_copy(
                src_ref=mine, dst_ref=theirs, send_sem=send_sems.at[m - 1], recv_sem=recv_sems.at[m - 1],
                device_id=peer, device_id_type=MESH).start()
        token[...] = jnp.zeros_like(token)

    hbm = [pltpu.with_memory_space_constraint(src, pltpu.HBM)]
    if not gather:
        hbm.append(pltpu.with_memory_space_constraint(lax.empty((N_DEV - 1, r, c), src.dtype), pltpu.HBM))
    outs = pl.pallas_call(
        body, name=name,
        out_shape=(pltpu.SemaphoreType.DMA((N_DEV - 1,)), pltpu.SemaphoreType.DMA((N_DEV - 1,)),
                   *[pltpu.HBM(t.shape, t.dtype) for t in hbm], jax.ShapeDtypeStruct((8, 128), F32)),
        in_specs=[_HBM] * n_hbm + [_ANY] * n_after,
        out_specs=(_SEM, _SEM, *[_HBM] * n_hbm, pl.BlockSpec(memory_space=pltpu.VMEM)),
        input_output_aliases={i: 2 + i for i in range(n_hbm)},
        compiler_params=pltpu.CompilerParams(has_side_effects=_EFFECT),
    )(*hbm, *after)
    return (outs[:2], outs[2:2 + n_hbm], (axis, gather, r, c)), outs[-1]


def _exchange_wait(handle, after, name):
    (send_sems, recv_sems), hbm, (axis, gather, r, c) = handle
    n_hbm = len(hbm)

    def body(*refs):
        src_ref, land_ref = refs[0], refs[n_hbm - 1]
        send_sems, recv_sems = refs[n_hbm:n_hbm + 2]
        me = _coords()
        for m in range(1, N_DEV):
            peer = _flip(me, m)
            if gather:
                mine, theirs = _window(land_ref, _index(me), r, c, axis), _window(land_ref, _index(peer), r, c, axis)
            else:
                mine, theirs = _window(src_ref, _index(peer), r, c, axis), land_ref.at[m - 1]
            copy = pltpu.make_async_remote_copy(
                src_ref=mine, dst_ref=theirs, send_sem=send_sems.at[m - 1], recv_sem=recv_sems.at[m - 1],
                device_id=peer, device_id_type=MESH)
            copy.wait_send()
            copy.wait_recv()

    outs = pl.pallas_call(
        body, name=name, out_shape=tuple(pltpu.HBM(t.shape, t.dtype) for t in hbm),
        in_specs=[_HBM] * n_hbm + [_SEM, _SEM, _ANY], out_specs=tuple([_HBM] * n_hbm),
        input_output_aliases={i: i for i in range(n_hbm)},
        compiler_params=pltpu.CompilerParams(has_side_effects=_EFFECT),
    )(*hbm, send_sems, recv_sems, after)
    return outs[0] if gather else tuple(outs)


def _mm(a, b, mode, out_dtypes, name, tm=1024, tn=1024, tk=2048, extras=(), epilogue=None, dep=None):
    if mode == "nn":
        (m, k), n = a.shape, b.shape[1]
    elif mode == "nt":
        (m, k), n = a.shape, b.shape[0]
    else:
        (k, m), n = a.shape, b.shape[1]
    tm, tn = _tile(m, tm, 8), _tile(n, tn, min(128, tn))
    tk = _tile(k, tk, 16 if mode == "tn" else 128)
    nk = k // tk
    swap = nk == 1 and (k * n + (n // tn) * m * k) < (m * k + (m // tm) * k * n)

    def ij(p, q):
        return (q, p) if swap else (p, q)

    def spec(shape, fn):
        return pl.BlockSpec(shape, lambda p, q, kk: fn(*ij(p, q), kk))

    a_spec = spec((tk, tm), lambda i, j, kk: (kk, i)) if mode == "tn" else spec((tm, tk), lambda i, j, kk: (i, kk))
    b_spec = spec((tn, tk), lambda i, j, kk: (j, kk)) if mode == "nt" else spec((tk, tn), lambda i, j, kk: (kk, j))
    dims = {"nn": (((1,), (0,)), ((), ())), "nt": (((1,), (1,)), ((), ())), "tn": (((0,), (0,)), ((), ()))}[mode]
    ex_specs = [spec((tm, tn), functools.partial(lambda i, j, kk, off: (i, j + off), off=off)) for _, off in extras]
    deps = [] if dep is None else [dep]
    dep_specs = [pl.BlockSpec(dep.shape, lambda p, q, kk: (0, 0))] if deps else []
    n_ex, n_out = len(extras), len(out_dtypes)
    n_in = 2 + n_ex + len(deps)

    def body(*refs):
        a_ref, b_ref = refs[0], refs[1]
        ex_refs = refs[2:2 + n_ex]
        out_refs = refs[n_in:n_in + n_out]

        def product():
            return lax.dot_general(a_ref[...], b_ref[...], dims, preferred_element_type=F32)

        def finish(res):
            res = epilogue(res, *[e[...] for e in ex_refs]) if epilogue is not None else (res,)
            for o_ref, val in zip(out_refs, res):
                o_ref[...] = val.astype(o_ref.dtype)

        if nk == 1:
            finish(product())
            return
        acc = refs[-1]
        kk = pl.program_id(2)

        @pl.when(kk == 0)
        def _():
            acc[...] = product()

        @pl.when((kk > 0) & (kk < nk - 1))
        def _():
            acc[...] += product()

        @pl.when(kk == nk - 1)
        def _():
            finish(acc[...] + product())

    outs = pl.pallas_call(
        body, name=name, grid=(*ij(m // tm, n // tn), nk),
        in_specs=[a_spec, b_spec] + ex_specs + dep_specs,
        out_specs=[spec((tm, tn), lambda i, j, kk: (i, j)) for _ in out_dtypes],
        out_shape=[jax.ShapeDtypeStruct((m, n), dt) for dt in out_dtypes],
        scratch_shapes=[pltpu.VMEM((tm, tn), F32)] if nk > 1 else [],
        compiler_params=_params(("parallel", "parallel", "arbitrary")),
    )(a, b, *[e for e, _ in extras], *deps)
    return outs if n_out > 1 else outs[0]


def _gather_project(a, w_full, order, name, riders=(), tm=768):
    m, k = a.shape
    c = w_full.shape[1] // N_DEV
    tm = _tile(m, tm, 8)
    n_i = m // tm
    chips = [4, 2, 6]
    n_rid = len(riders)
    shard = [(k, c, 1)] + [((t.shape[0] // N_DEV, t.shape[1], 0) if ax == 0 else (t.shape[0], t.shape[1] // N_DEV, 1))
                           for t, ax in riders]

    def body(order_ref, a_ref, *refs):
        outs = refs[1 + n_rid:2 + 2 * n_rid]
        z_ref, wbuf, send_sems, recv_sems, load_sems = refs[2 + 2 * n_rid:]
        w_ref = outs[0]
        p, i = pl.program_id(0), pl.program_id(1)
        me = _coords()
        sibling = _flip(me, 1)

        def copy(t, s, block, to):
            r_t, c_t, ax_t = shard[t]
            win = _window(outs[t], _index(block), r_t, c_t, ax_t)
            return pltpu.make_async_remote_copy(src_ref=win, dst_ref=win, send_sem=send_sems.at[7 * t + s],
                                                recv_sem=recv_sems.at[7 * t + s], device_id=to, device_id_type=MESH)

        def first(t):
            return [copy(t, 0, me, sibling)] + [copy(t, 1 + j, me, _flip(me, mm)) for j, mm in enumerate(chips)]

        def passed(t):
            return [copy(t, 4 + j, _flip(me, mm), sibling) for j, mm in enumerate(chips)]

        @pl.when((p == 0) & (i == 0))
        def _():
            for t in range(1 + n_rid):
                for cp in first(t):
                    cp.start()

        def load(piece):
            return pltpu.make_async_copy(_window(w_ref, order_ref[piece], k, c, 1), wbuf.at[piece % 2],
                                         load_sems.at[piece % 2])

        @pl.when((p == 0) & (i == 0))
        def _():
            load(p).start()

        arrivals = [(0, None)] + [x for j in range(3) for x in ((1 + j, passed(0)[j]), (4 + j, None))]
        for piece, (sem, forward) in enumerate(arrivals, start=1):
            @pl.when((p == piece - 1) & (i == n_i - 1))
            def _(sem=sem, forward=forward, piece=piece):
                src_block = (sibling if piece == 1 else
                             _flip(me if piece % 2 == 0 else sibling, chips[(piece - 2) // 2]))
                copy(0, sem, src_block, me).wait_recv()
                if forward is not None:
                    forward.start()
                load(p + 1).start()

        @pl.when(i == 0)
        def _():
            load(p).wait()

        @pl.when((p == N_DEV - 1) & (i == 0))
        def _():
            for t in range(1, 1 + n_rid):
                for j, mm in enumerate(chips):
                    copy(t, 1 + j, _flip(me, mm), me).wait_recv()
                    passed(t)[j].start()

        z_ref[...] = jnp.dot(a_ref[...], wbuf[p % 2], preferred_element_type=F32)

        @pl.when((p == N_DEV - 1) & (i == n_i - 1))
        def _():
            for t in range(1, 1 + n_rid):
                copy(t, 0, sibling, me).wait_recv()
                for j, mm in enumerate(chips):
                    copy(t, 4 + j, _flip(sibling, mm), me).wait_recv()
            for t in range(1 + n_rid):
                for cp in first(t) + passed(t):
                    cp.wait_send()

    gathered = [w_full] + [t for t, _ in riders]
    n_sem = 7 * len(gathered)
    return pl.pallas_call(
        body, name=name,
        out_shape=[jax.ShapeDtypeStruct(t.shape, t.dtype) for t in gathered]
        + [jax.ShapeDtypeStruct((m, N_DEV * c), F32)],
        grid_spec=pltpu.PrefetchScalarGridSpec(
            num_scalar_prefetch=1, grid=(N_DEV, n_i),
            in_specs=[pl.BlockSpec((tm, k), lambda p, i, order_ref: (i, 0))]
            + [pl.BlockSpec(memory_space=pl.ANY)] * len(gathered),
            out_specs=[pl.BlockSpec(memory_space=pl.ANY)] * len(gathered)
            + [pl.BlockSpec((tm, c), lambda p, i, order_ref: (i, order_ref[p]))],
            scratch_shapes=[pltpu.VMEM((2, k, c), w_full.dtype), pltpu.SemaphoreType.DMA((n_sem,)),
                            pltpu.SemaphoreType.DMA((n_sem,)), pltpu.SemaphoreType.DMA((2,))]),
        input_output_aliases={2 + t: t for t in range(len(gathered))},
        compiler_params=_params(("arbitrary", "arbitrary")))(order, a, *gathered)


def _rspec(tr, w, cb=0, off=0):
    return pl.BlockSpec((tr, w), lambda i: (i + off, cb))


def _cspec(shape):
    return pl.BlockSpec(shape, lambda i: (0,) * len(shape))


def _rms(xf):
    rstd = lax.rsqrt(jnp.mean(xf * xf, axis=-1, keepdims=True) + EPS)
    return xf * rstd, rstd


def _rms_bwd(dn, n, rstd):
    return rstd * (dn - n * jnp.mean(dn * n, axis=-1, keepdims=True))


def _colsum(v):
    return jnp.sum(v, axis=0, keepdims=True)


def _total(v):
    return jnp.sum(jnp.sum(v, axis=1, keepdims=True), axis=0, keepdims=True)


def _modulate(x, vec, name, total_rows, into=None):
    rows, d = x.shape
    tr = _tile(rows, 256, 8)
    off = 0 if into is None else (total_rows - rows) // tr

    def body(x_ref, v_ref, *refs):
        n, _ = _rms(x_ref[...])
        refs[-1][...] = ((n * v_ref[0:1, :]) * (1.0 + v_ref[2:3, :]) + v_ref[1:2, :]).astype(BF16)

    return pl.pallas_call(
        body, name=name, grid=(rows // tr,),
        in_specs=[_rspec(tr, d), _cspec(vec.shape)] + ([] if into is None else [_ANY]),
        out_specs=_rspec(tr, d, off=off), out_shape=jax.ShapeDtypeStruct((total_rows, d), BF16),
        input_output_aliases={} if into is None else {2: 0},
        compiler_params=_params(("parallel",)))(x, vec, *([] if into is None else [into]))


def _resid_modulate(x, y, vec, name):
    rows, d = x.shape
    tr = _tile(rows, 256, 8)

    def body(x_ref, y_ref, v_ref, x1_ref, a_ref):
        x1 = x_ref[...] + v_ref[3:4, :] * y_ref[...]
        x1_ref[...] = x1
        n, _ = _rms(x1)
        a_ref[...] = ((n * v_ref[0:1, :]) * (1.0 + v_ref[2:3, :]) + v_ref[1:2, :]).astype(BF16)

    return pl.pallas_call(
        body, name=name, grid=(rows // tr,), in_specs=[_rspec(tr, d), _rspec(tr, d), _cspec(vec.shape)],
        out_specs=[_rspec(tr, d), _rspec(tr, d)],
        out_shape=[jax.ShapeDtypeStruct((rows, d), F32), jax.ShapeDtypeStruct((rows, d), BF16)],
        compiler_params=_params(("parallel",)))(x, y, vec)


def _loss_head(x1, f, target, vec, name):
    rows, d = x1.shape
    tr = _tile(rows, 256, 8)

    def body(x1_ref, f_ref, t_ref, v_ref, dx2_ref, df_ref, acc_ref):
        @pl.when(pl.program_id(0) == 0)
        def _():
            acc_ref[...] = jnp.zeros_like(acc_ref)

        gate, gain = v_ref[0:1, :], v_ref[1:2, :]
        fv = f_ref[...]
        n, rstd = _rms(x1_ref[...] + gate * fv)
        err = n * gain - t_ref[...]
        dy = err / d
        dx2 = _rms_bwd(dy * gain, n, rstd)
        dx2_ref[...] = dx2
        df_ref[...] = (dx2 * gate).astype(BF16)
        acc_ref[0:1, :] += _colsum(dx2 * fv)
        acc_ref[1:2, :] += _colsum(dy * n)
        acc_ref[2:3, :] += _colsum(err * err)

    return pl.pallas_call(
        body, name=name, grid=(rows // tr,),
        in_specs=[_rspec(tr, d), _rspec(tr, d), _rspec(tr, d), _cspec(vec.shape)],
        out_specs=[_rspec(tr, d), _rspec(tr, d), _cspec((8, d))],
        out_shape=[jax.ShapeDtypeStruct((rows, d), F32), jax.ShapeDtypeStruct((rows, d), BF16),
                   jax.ShapeDtypeStruct((8, d), F32)],
        compiler_params=_params(("arbitrary",)))(x1, f, target, vec)


def _modulate_bwd(da, x, vec, name, da_off=0, dx_in=None, y=None):
    rows, d = x.shape
    tr = _tile(rows, 256, 8)
    off = da_off // tr
    has_dx, has_y = dx_in is not None, y is not None

    def body(*refs):
        da_ref, x_ref, v_ref = refs[0], refs[1], refs[2]
        pos = 3
        dxin_ref = refs[pos] if has_dx else None
        pos += has_dx
        y_ref = refs[pos] if has_y else None
        pos += has_y
        dx_ref = refs[pos]
        dy_ref = refs[pos + 1] if has_y else None
        acc_ref = refs[-1]

        @pl.when(pl.program_id(0) == 0)
        def _():
            acc_ref[...] = jnp.zeros_like(acc_ref)

        gain, scale1 = v_ref[0:1, :], 1.0 + v_ref[2:3, :]
        dav = da_ref[...]
        n, rstd = _rms(x_ref[...])
        dx = _rms_bwd(dav * gain * scale1, n, rstd)
        if has_dx:
            dx = dx + dxin_ref[...]
        dx_ref[...] = dx
        acc_ref[0:1, :] += _colsum(dav)
        acc_ref[1:2, :] += _colsum(dav * (n * gain))
        acc_ref[2:3, :] += _colsum(dav * n * scale1)
        if has_y:
            acc_ref[3:4, :] += _colsum(dx * y_ref[...])
            dy_ref[...] = (dx * v_ref[3:4, :]).astype(BF16)

    ins = [da, x, vec] + ([dx_in] if has_dx else []) + ([y] if has_y else [])
    in_specs = [_rspec(tr, d, off=off), _rspec(tr, d), _cspec(vec.shape)] + [_rspec(tr, d)] * (has_dx + has_y)
    out_specs = [_rspec(tr, d)] + ([_rspec(tr, d)] if has_y else []) + [_cspec((8, d))]
    out_shape = ([jax.ShapeDtypeStruct((rows, d), F32)] + ([jax.ShapeDtypeStruct((rows, d), BF16)] if has_y else [])
                 + [jax.ShapeDtypeStruct((8, d), F32)])
    return pl.pallas_call(
        body, name=name, grid=(rows // tr,), in_specs=in_specs, out_specs=out_specs, out_shape=out_shape,
        compiler_params=_params(("arbitrary",)))(*ins)


def _conv_terms(cc, cx, w_ref, tr):
    t = lax.broadcasted_iota(jnp.int32, (tr, 1), 0) % GRID_W
    first, last = t == 0, t == GRID_W - 1
    u = cc * cx
    prev = jnp.where(first, 0.0, pltpu.roll(u, 1, 0))
    nxt = jnp.where(last, 0.0, pltpu.roll(u, tr - 1, 0))
    y = w_ref[0:1, :] * prev + w_ref[1:2, :] * u + w_ref[2:3, :] * nxt
    return u, prev, nxt, y, first, last


def _conv_fwd(z, conv_w, rows, name, dep):
    d = conv_w.shape[1]
    tr = _tile(rows, 256, GRID_W)

    def body(cb_ref, cc_ref, cx_ref, w_ref, dep_ref, out_ref):
        y = _conv_terms(cc_ref[...], cx_ref[...], w_ref, tr)[3]
        out_ref[...] = (cb_ref[...] * y).astype(BF16)

    return pl.pallas_call(
        body, name=name, grid=(rows // tr,),
        in_specs=[_rspec(tr, d, CB), _rspec(tr, d, CC), _rspec(tr, d, CX), _cspec(conv_w.shape), _cspec(dep.shape)],
        out_specs=_rspec(tr, d), out_shape=jax.ShapeDtypeStruct((rows, d), BF16),
        compiler_params=_params(("parallel",)))(z, z, z, conv_w, dep)


def _conv_bwd(dpc, z, conv_w, name):
    rows, d = dpc.shape
    tr = _tile(rows, 256, GRID_W)

    def body(dpc_ref, cb_ref, cc_ref, cx_ref, w_ref, dz_ref, acc_ref):
        @pl.when(pl.program_id(0) == 0)
        def _():
            acc_ref[...] = jnp.zeros_like(acc_ref)

        cc, cx, dpcv = cc_ref[...], cx_ref[...], dpc_ref[...]
        u, prev, nxt, y, first, last = _conv_terms(cc, cx, w_ref, tr)
        dy = dpcv * cb_ref[...]
        dy_next = jnp.where(last, 0.0, pltpu.roll(dy, tr - 1, 0))
        dy_prev = jnp.where(first, 0.0, pltpu.roll(dy, 1, 0))
        du = w_ref[0:1, :] * dy_next + w_ref[1:2, :] * dy + w_ref[2:3, :] * dy_prev
        dz_ref[:, 0:d] = (dpcv * y).astype(BF16)
        dz_ref[:, d:2 * d] = (du * cx).astype(BF16)
        dz_ref[:, 2 * d:3 * d] = (du * cc).astype(BF16)
        acc_ref[0:1, :] += _colsum(dy * prev)
        acc_ref[1:2, :] += _colsum(dy * u)
        acc_ref[2:3, :] += _colsum(dy * nxt)

    return pl.pallas_call(
        body, name=name, grid=(rows // tr,),
        in_specs=[_rspec(tr, d), _rspec(tr, d, CB), _rspec(tr, d, CC), _rspec(tr, d, CX), _cspec(conv_w.shape)],
        out_specs=[_rspec(tr, 3 * d), _cspec((8, d))],
        out_shape=[jax.ShapeDtypeStruct((rows, 3 * d), BF16), jax.ShapeDtypeStruct((8, d), F32)],
        compiler_params=_params(("arbitrary",)))(dpc, z, z, z, conv_w)


def _rotary_fwd(z, cos, sin, n_lat, name):
    rows = z.shape[0]
    d = z.shape[1] // N_IN
    dk = d // HEADS
    half = dk // 2
    tr = _tile(n_lat, 256, 8)
    tr = _tile(rows - n_lat, tr, 8)
    lat_blocks, all_blocks = n_lat // tr, rows // tr
    ctx_blocks = all_blocks - lat_blocks
    scan_rows = rows + ctx_blocks * tr

    def z_block(g):
        return jnp.where(g < all_blocks, g, g - ctx_blocks)

    def in_spec(w, cb=0):
        return pl.BlockSpec((tr, w), lambda g: (z_block(g), cb))

    def out_spec(w):
        return pl.BlockSpec((tr, w), lambda g: (jnp.where(g < all_blocks, g + ctx_blocks, g - all_blocks), 0))

    def body(q_ref, k_ref, v0_ref, v1_ref, cos_ref, sin_ref, qo_ref, ko_ref, vo_ref):
        cs, sn = cos_ref[...], sin_ref[...]
        keep = jnp.where(z_block(pl.program_id(0)) < lat_blocks, 1.0, 0.0)
        for src, dst, scale in ((q_ref, qo_ref, keep), (k_ref, ko_ref, dk ** -0.5)):
            for h in range(HEADS):
                lo, mid, hi = h * dk, h * dk + half, (h + 1) * dk
                t1, t2 = src[:, lo:mid], src[:, mid:hi]
                dst[:, lo:mid] = ((t1 * cs - t2 * sn) * scale).astype(BF16)
                dst[:, mid:hi] = ((t1 * sn + t2 * cs) * scale).astype(BF16)
        vo_ref[:, 0:d] = v0_ref[...].astype(BF16)
        vo_ref[:, d:2 * d] = v1_ref[...].astype(BF16)

    return pl.pallas_call(
        body, name=name, grid=(all_blocks + ctx_blocks,),
        in_specs=[in_spec(d, CQ), in_spec(d, CK), in_spec(d, CV), in_spec(d, CV + 1), in_spec(half), in_spec(half)],
        out_specs=[out_spec(d), out_spec(d), out_spec(2 * d)],
        out_shape=[jax.ShapeDtypeStruct((scan_rows, d), BF16), jax.ShapeDtypeStruct((scan_rows, d), BF16),
                   jax.ShapeDtypeStruct((scan_rows, 2 * d), BF16)],
        compiler_params=_params(("parallel",)))(z, z, z, z, cos, sin)


def _dz_assemble(dq_f, dq_b, dk_f, dk_b, dv_f, dv_b, cos, sin, dz_conv, dz_g, dz_gc, dz_gr, n_lat, n_ctx, name):
    rows = n_lat + n_ctx
    d = dq_f.shape[1]
    dk = d // HEADS
    half = dk // 2
    tr = _tile(n_ctx, 128, 8)
    lat_blocks, ctx_blocks = n_lat // tr, n_ctx // tr

    def fmap(i):
        return (jnp.where(i < lat_blocks, i + ctx_blocks, i - lat_blocks), 0)

    def bmap(i):
        return (i + ctx_blocks, 0)

    def lmap(i):
        return (jnp.minimum(i, lat_blocks - 1), 0)

    def body(qf_ref, qb_ref, kf_ref, kb_ref, vf_ref, vb_ref, cos_ref, sin_ref, c_ref, g_ref, gc_ref, gr_ref, out_ref):
        cs, sn = cos_ref[...], sin_ref[...]
        is_lat = pl.program_id(0) < lat_blocks
        keep = jnp.where(is_lat, 1.0, 0.0)
        for fa, fb, base, scale in ((qf_ref, qb_ref, CQ * d, keep), (kf_ref, kb_ref, CK * d, dk ** -0.5)):
            for h in range(HEADS):
                lo, mid, hi = h * dk, h * dk + half, (h + 1) * dk
                d1 = fa[:, lo:mid].astype(F32) + fb[:, lo:mid].astype(F32)
                d2 = fa[:, mid:hi].astype(F32) + fb[:, mid:hi].astype(F32)
                out_ref[:, base + lo:base + mid] = ((d1 * cs + d2 * sn) * scale).astype(BF16)
                out_ref[:, base + mid:base + hi] = ((d2 * cs - d1 * sn) * scale).astype(BF16)
        out_ref[:, CV * d:CG * d] = (vf_ref[...].astype(F32) + vb_ref[...].astype(F32)).astype(BF16)
        for src, lo, hi in ((c_ref, CB * d, CQ * d), (g_ref, CG * d, CGC * d), (gc_ref, CGC * d, CGR * d),
                            (gr_ref, CGR * d, N_IN * d)):
            out_ref[:, lo:hi] = jnp.where(is_lat, src[...], jnp.zeros_like(src))

    return pl.pallas_call(
        body, name=name, grid=(rows // tr,),
        in_specs=[pl.BlockSpec((tr, d), fmap), pl.BlockSpec((tr, d), bmap), pl.BlockSpec((tr, d), fmap),
                  pl.BlockSpec((tr, d), bmap), pl.BlockSpec((tr, 2 * d), fmap), pl.BlockSpec((tr, 2 * d), bmap),
                  _rspec(tr, half), _rspec(tr, half), pl.BlockSpec((tr, 3 * d), lmap), pl.BlockSpec((tr, 2 * d), lmap),
                  pl.BlockSpec((tr, d), lmap), pl.BlockSpec((tr, d), lmap)],
        out_specs=_rspec(tr, N_IN * d), out_shape=jax.ShapeDtypeStruct((rows, N_IN * d), BF16),
        compiler_params=_params(("parallel",)))(dq_f, dq_b, dk_f, dk_b, dv_f, dv_b, cos, sin, dz_conv, dz_g, dz_gc, dz_gr)


def _gn_fwd(o, z, name):
    rows = o.shape[0]
    d = z.shape[1] // N_IN
    dv = 2 * d // HEADS
    tr = _tile(rows, 128, 8)

    def body(o_ref, g0_ref, g1_ref, r_ref):
        for h in range(HEADS):
            lo, hi = h * dv, (h + 1) * dv
            g_ref, glo = (g0_ref, lo) if hi <= d else (g1_ref, lo - d)
            o = o_ref[:, lo:hi]
            cen = o - jnp.mean(o, axis=-1, keepdims=True)
            on = cen * lax.rsqrt(jnp.mean(cen * cen, axis=-1, keepdims=True) + EPS)
            g = g_ref[:, glo:glo + dv]
            r_ref[:, lo:hi] = (g * _sig(g) * on).astype(BF16)

    return pl.pallas_call(
        body, name=name, grid=(rows // tr,),
        in_specs=[_rspec(tr, 2 * d), _rspec(tr, d, CG), _rspec(tr, d, CG + 1)],
        out_specs=_rspec(tr, 2 * d), out_shape=jax.ShapeDtypeStruct((rows, 2 * d), BF16),
        compiler_params=_params(("parallel",)))(o, z, z)


def _gn_bwd_head(dr, o, g):
    cen = o - jnp.mean(o, axis=-1, keepdims=True)
    rstd = lax.rsqrt(jnp.mean(cen * cen, axis=-1, keepdims=True) + EPS)
    on = cen * rstd
    sg = _sig(g)
    don = dr * (g * sg)
    do = rstd * (don - jnp.mean(don, axis=-1, keepdims=True) - on * jnp.mean(don * on, axis=-1, keepdims=True))
    return do, dr * on * (sg * (1.0 + g * (1.0 - sg)))


def _decays(lg, rev):
    row = lax.broadcasted_iota(jnp.int32, (CHUNK, CHUNK), 0)
    col = lax.broadcasted_iota(jnp.int32, (CHUNK, CHUNK), 1)
    rel = ((col - row) if rev else (row - col)).astype(F32)
    mask = jnp.where(rel >= 0, jnp.exp(lg * jnp.maximum(rel, 0.0)), 0.0)
    r = lax.broadcasted_iota(jnp.int32, (CHUNK, 1), 0)
    rr = ((CHUNK - 1 - r) if rev else r).astype(F32)
    chunk_decay = jnp.exp(lg * jnp.full((1, 1), float(CHUNK), F32))
    return (rr, rel), mask, jnp.exp(lg * (rr + 1.0)), jnp.exp(lg * (CHUNK - 1.0 - rr)), chunk_decay


_NT = (((1,), (1,)), ((), ()))
_TN = (((0,), (0,)), ((), ()))


def _dot(a, b, dims=None):
    if dims is None:
        return jnp.dot(a, b, preferred_element_type=F32)
    return lax.dot_general(a, b, dims, preferred_element_type=F32)


def _ret_fwd(q, k, v, lgs, n_lat, name):
    tt, d = q.shape
    dk, dv = d // HEADS, 2 * d // HEADS
    nc = tt // CHUNK
    lat_chunks = n_lat // CHUNK
    ctx_chunks = (nc - lat_chunks) // 2

    def chunk_of(rev, i):
        return (nc - 1 - i) if rev else i

    def cmap(rev):
        return lambda h, i: (chunk_of(rev, i), h)

    def smap(rev):
        return lambda h, i: (chunk_of(rev, i), h, 0, 0)

    def body(lg_ref, qf_ref, kf_ref, vf_ref, qb_ref, kb_ref, vb_ref, o_ref, sf_ref, sb_ref, state_f, state_b):
        h, i = pl.program_id(0), pl.program_id(1)

        @pl.when(i == 0)
        def _():
            state_f[...] = jnp.zeros_like(state_f)
            state_b[...] = jnp.zeros_like(state_b)

        for rev, (q_ref, k_ref, v_ref, s_ref, state) in enumerate((
                (qf_ref, kf_ref, vf_ref, sf_ref, state_f), (qb_ref, kb_ref, vb_ref, sb_ref, state_b))):
            _, mask, qd, kd, cd = _decays(lg_ref[rev, h], bool(rev))
            qv, kv, vv = q_ref[...], k_ref[...], v_ref[...]
            st = state[...]
            p = _dot(qv, kv, _NT) * mask
            out = _dot(p.astype(BF16), vv) + _dot((qv * qd).astype(BF16), st.astype(BF16))
            s_ref[...] = st
            state[...] = cd * st + _dot((kv * kd).astype(BF16), vv, _TN)
            lat_chunk = chunk_of(bool(rev), i) - ctx_chunks
            is_lat = (lat_chunk >= 0) & (lat_chunk < lat_chunks)
            first = (2 * i < nc - 1) if rev else (2 * i <= nc - 1)

            @pl.when(is_lat & first)
            def _(lat_chunk=lat_chunk, out=out):
                o_ref[pl.ds(pl.multiple_of(lat_chunk * CHUNK, CHUNK), CHUNK), :] = out

            @pl.when(is_lat & jnp.logical_not(first))
            def _(lat_chunk=lat_chunk, out=out):
                o_ref[pl.ds(pl.multiple_of(lat_chunk * CHUNK, CHUNK), CHUNK), :] += out

    def specs(rev):
        return [pl.BlockSpec((CHUNK, dk), cmap(rev)), pl.BlockSpec((CHUNK, dk), cmap(rev)),
                pl.BlockSpec((CHUNK, dv), cmap(rev))]

    state_shape = jax.ShapeDtypeStruct((nc, HEADS, dk, dv), F32)
    return pl.pallas_call(
        body, name=name, grid=(HEADS, nc),
        in_specs=[pl.BlockSpec(memory_space=pltpu.SMEM)] + specs(False) + specs(True),
        out_specs=[pl.BlockSpec((n_lat, dv), lambda h, i: (0, h)), pl.BlockSpec((None, None, dk, dv), smap(False)),
                   pl.BlockSpec((None, None, dk, dv), smap(True))],
        out_shape=[jax.ShapeDtypeStruct((n_lat, 2 * d), F32), state_shape, state_shape],
        scratch_shapes=[pltpu.VMEM((dk, dv), F32), pltpu.VMEM((dk, dv), F32)],
        compiler_params=_params(("parallel", "arbitrary")))(lgs, q, k, v, q, k, v)


def _ret_bwd(q, k, v, do, states_f, states_b, lgs, name):
    tt, d = q.shape
    dk, dv = d // HEADS, 2 * d // HEADS
    nc = tt // CHUNK
    lat_chunks = do.shape[0] // CHUNK
    ctx_chunks = (nc - lat_chunks) // 2

    def chunk_of(rev, i):
        return i if rev else nc - 1 - i

    def cmap(rev):
        return lambda h, i: (chunk_of(rev, i), h)

    def do_map(rev):
        return lambda h, i: (jnp.clip(chunk_of(rev, i) - ctx_chunks, 0, lat_chunks - 1), h)

    def smap(rev):
        return lambda h, i: (chunk_of(rev, i), h, 0, 0)

    pair = 4

    def body(lg_ref, *refs):
        hh = pl.program_id(0)
        dstates = refs[-2:]

        @pl.when(pl.program_id(1) == 0)
        def _():
            for rev in (0, 1):
                dstates[rev][...] = jnp.zeros_like(dstates[rev])
                refs[10 + 4 * rev + 3][...] = jnp.zeros_like(refs[10 + 4 * rev + 3])

        for rev in (0, 1):
            q_ref, k_ref, v_ref, do_ref, s_ref = refs[5 * rev:5 * rev + 5]
            dq_ref, dk_ref, dv_ref, dlg_ref = refs[10 + 4 * rev:10 + 4 * rev + 4]
            chunk = chunk_of(bool(rev), pl.program_id(1))
            is_lat = (chunk >= ctx_chunks) & (chunk < ctx_chunks + lat_chunks)
            for e in range(pair):
                ks, vs = slice(e * dk, (e + 1) * dk), slice(e * dv, (e + 1) * dv)
                (rr, rel), mask, qd, kd, cd = _decays(lg_ref[rev, pair * hh + e], bool(rev))
                qv, kv, vv = q_ref[:, ks], k_ref[:, ks], v_ref[:, vs]
                dov = jnp.where(is_lat, do_ref[:, vs], jnp.zeros((CHUNK, dv), BF16))
                st, dst = s_ref[e], dstates[rev][e]
                st_b, dst_b = st.astype(BF16), dst.astype(BF16)
                p = _dot(qv, kv, _NT) * mask
                dp = _dot(dov, vv, _NT)
                da = (dp * mask).astype(BF16)
                dq_state = _dot(dov, st_b, _NT) * qd
                dk_state = _dot(vv, dst_b, _NT) * kd
                dq_ref[:, ks] = (_dot(da, kv) + dq_state).astype(BF16)
                dk_ref[:, ks] = (_dot(da, qv, _TN) + dk_state).astype(BF16)
                dv_ref[:, vs] = (_dot(p.astype(BF16), dov, _TN) + _dot((kv * kd).astype(BF16), dst_b)).astype(BF16)
                dnew = cd * dst + _dot((qv * qd).astype(BF16), dov, _TN)
                dstates[rev][e] = dnew
                through = rr * (jnp.sum(qv.astype(F32) * dq_state, axis=1, keepdims=True)
                                - jnp.sum(kv.astype(F32) * dk_state, axis=1, keepdims=True))
                dlg_ref[e] += _total(rel * p * dp) + _total(through) + CHUNK * _total(st * dnew)

    def specs(rev):
        return [pl.BlockSpec((CHUNK, pair * dk), cmap(rev)), pl.BlockSpec((CHUNK, pair * dk), cmap(rev)),
                pl.BlockSpec((CHUNK, pair * dv), cmap(rev)), pl.BlockSpec((CHUNK, pair * dv), do_map(rev)),
                pl.BlockSpec((None, pair, dk, dv), smap(rev))]

    def outs(rev):
        return [pl.BlockSpec((CHUNK, pair * dk), cmap(rev)), pl.BlockSpec((CHUNK, pair * dk), cmap(rev)),
                pl.BlockSpec((CHUNK, pair * dv), cmap(rev)), pl.BlockSpec((pair, 8, 128), lambda h, i: (h, 0, 0))]

    return pl.pallas_call(
        body, name=name, grid=(HEADS // pair, nc),
        in_specs=[pl.BlockSpec(memory_space=pltpu.SMEM)] + specs(False) + specs(True),
        out_specs=outs(False) + outs(True),
        out_shape=[jax.ShapeDtypeStruct((tt, d), BF16), jax.ShapeDtypeStruct((tt, d), BF16),
                   jax.ShapeDtypeStruct((tt, 2 * d), BF16), jax.ShapeDtypeStruct((HEADS, 8, 128), F32)] * 2,
        scratch_shapes=[pltpu.VMEM((pair, dk, dv), F32), pltpu.VMEM((pair, dk, dv), F32)],
        compiler_params=_params(("parallel", "arbitrary")))(lgs, q, k, v, do, states_f, q, k, v, do, states_b)


def _silu(v):
    return v * _sig(v)


def _mod_proj(cs, w_mod, b_loc, dec, name):
    nb = w_mod.shape[1]

    def body(cs_ref, w_ref, b_ref, dec_ref, out_ref, lg_ref):
        out_ref[...] = jnp.dot(_silu(cs_ref[...]), w_ref[...], preferred_element_type=F32, precision=HIGHEST) + b_ref[...]
        a = dec_ref[...]
        lg_ref[...] = jnp.minimum(a, 0.0) - jnp.log1p(jnp.exp(-jnp.abs(a)))

    return pl.pallas_call(
        body, name=name,
        out_shape=[jax.ShapeDtypeStruct((16, nb), F32), jax.ShapeDtypeStruct(dec.shape, F32)],
        compiler_params=_params())(cs, w_mod, b_loc, dec)


def _mod_grad(cs, dm, w_mod, name):
    d, nb = w_mod.shape

    def body(cs_ref, dm_ref, w_ref, gw_ref, part_ref):
        dmv = dm_ref[...]
        gw_ref[...] = lax.dot_general(_silu(cs_ref[...]), dmv, _TN, preferred_element_type=F32, precision=HIGHEST)
        part_ref[...] = lax.dot_general(dmv, w_ref[...], _NT, preferred_element_type=F32, precision=HIGHEST)

    return pl.pallas_call(
        body, name=name,
        out_shape=[jax.ShapeDtypeStruct((d, nb), F32), jax.ShapeDtypeStruct((16, d), F32)],
        compiler_params=_params())(cs, dm, w_mod)


def _reduce_small(gathered, dec, n_feat, name):
    _, rows, d = gathered.shape

    def body(g_ref, dec_ref, red_ref, misc_ref):
        total = g_ref[0]
        for i in range(1, N_DEV):
            total = total + g_ref[i]
        red_ref[...] = total
        misc_ref[...] = jnp.zeros_like(misc_ref)
        misc_ref[0:1, :] = jnp.zeros((1, 128), F32) + (0.5 / n_feat) * _total(total[16:17, :])
        misc_ref[1:3, :] = total[14:16, 0:128] * _sig(-dec_ref[0:2, :])

    return pl.pallas_call(
        body, name=name,
        out_shape=[jax.ShapeDtypeStruct((rows, d), F32), jax.ShapeDtypeStruct((8, 128), F32)],
        compiler_params=_params())(gathered, dec)


def _c_ctx_grad(parts, c_ctx, name):
    d = c_ctx.shape[1]

    def body(p_ref, c_ref, out_ref):
        total = p_ref[0]
        for i in range(1, N_DEV):
            total = total + p_ref[i]
        cv = c_ref[...]
        sg = _sig(cv)
        out_ref[...] = total[8:9, :] * (sg * (1.0 + cv * (1.0 - sg)))

    return pl.pallas_call(body, name=name, out_shape=jax.ShapeDtypeStruct((1, d), F32),
                          compiler_params=_params())(parts, c_ctx)


def _adamw(parts, w, m, v, name, own=None):
    n_parts, rows, cols = parts.shape
    row_bytes = cols * (parts.dtype.itemsize * (n_parts + 1) + 7 * 4)
    tr = _tile(rows, max(16, (8 * 1024 * 1024) // row_bytes), 16 if rows % 16 == 0 else 8)
    blocks = rows // tr

    def body(*refs):
        p_ref, w_ref, m_ref, v_ref, g_ref, d_ref, mo_ref, vo_ref = refs[-8:]
        if own is None:
            g = p_ref[0].astype(F32)
        else:
            g = refs[1][...].astype(F32) + p_ref[0].astype(F32)
        for i in range(1, n_parts):
            g = g + p_ref[i].astype(F32)
        m2 = ADAM_B1 * m_ref[...] + (1.0 - ADAM_B1) * g
        v2 = ADAM_B2 * v_ref[...] + (1.0 - ADAM_B2) * jnp.square(g)
        m_hat = m2 / (1.0 - ADAM_B1 ** ADAM_STEP)
        v_hat = v2 / (1.0 - ADAM_B2 ** ADAM_STEP)
        g_ref[...] = g
        d_ref[...] = -ADAM_LR * (m_hat / (jnp.sqrt(v_hat) + ADAM_EPS) + ADAM_WD * w_ref[...])
        mo_ref[...] = m2
        vo_ref[...] = v2

    out_shape = [jax.ShapeDtypeStruct((rows, cols), F32)] * 4
    if own is None:
        spec = _rspec(tr, cols)
        return pl.pallas_call(
            body, name=name, grid=(blocks,),
            in_specs=[pl.BlockSpec((n_parts, tr, cols), lambda i: (0, i, 0)), spec, spec, spec],
            out_specs=[spec] * 4, out_shape=out_shape, compiler_params=_params(("parallel",)))(parts, w, m, v)
    g_full, axis, me = own

    def own_map(i, me_ref):
        return (me_ref[0] * blocks + i, 0) if axis == 0 else (i, me_ref[0])

    spec = pl.BlockSpec((tr, cols), lambda i, me_ref: (i, 0))
    return pl.pallas_call(
        body, name=name, out_shape=out_shape,
        grid_spec=pltpu.PrefetchScalarGridSpec(
            num_scalar_prefetch=1, grid=(blocks,),
            in_specs=[pl.BlockSpec((tr, cols), own_map),
                      pl.BlockSpec((n_parts, tr, cols), lambda i, me_ref: (0, i, 0)), spec, spec, spec],
            out_specs=[spec] * 4),
        compiler_params=_params(("parallel",)))(me.reshape(1), g_full, parts, w, m, v)


def _rope_tables(pos, dk):
    half = dk // 2
    inv_freq = 1.0 / (ROPE_BASE ** jnp.linspace(0.0, 1.0, half, dtype=F32))
    ang = pos[:, None] * inv_freq[None, :]
    return jnp.cos(ang), jnp.sin(ang)


def _pad_lanes(v, width):
    return jnp.pad(v, ((0, 0), (0, width - v.shape[1])))


def kernel(x, c, ctx, c_ctx, w_mod, b_mod, norm1_g, w_in, conv_w, w_conv_out, ret_decay_fwd, ret_decay_bwd, w_ret_out, w_o, norm2_g, w_ff1, w_ff2, final_g, loss_target, m_c_ctx, m_w_mod, m_b_mod, m_norm1_g, m_w_in, m_conv_w, m_w_conv_out, m_ret_decay_fwd, m_ret_decay_bwd, m_w_ret_out, m_w_o, m_norm2_g, m_w_ff1, m_w_ff2, m_final_g, v_c_ctx, v_w_mod, v_b_mod, v_norm1_g, v_w_in, v_conv_w, v_w_conv_out, v_ret_decay_fwd, v_ret_decay_bwd, v_w_ret_out, v_w_o, v_norm2_g, v_w_ff1, v_w_ff2, v_final_g):
    n_lat, d = x.shape[1], x.shape[2]
    n_ctx = ctx.shape[1]
    assert n_ctx % CHUNK == 0 and n_lat % CHUNK == 0 and n_lat % GRID_W == 0
    dk = d // HEADS
    nb = w_mod.shape[2]
    me = 4 * lax.axis_index("x") + 2 * lax.axis_index("y") + lax.axis_index("c")
    xl, ctxl, target = x[0], ctx[0], loss_target[0]

    first = _small_allgather(jnp.concatenate([c, _pad_lanes(conv_w[0], d), jnp.zeros((4, d), F32)], axis=0), "ag_c")
    conv_w_f = first[:, 1:4, :d // N_DEV].transpose(1, 0, 2).reshape(3, d)

    c_all = first[:, 0, :]
    cs = jnp.concatenate([c_all, c_ctx[None], jnp.zeros((7, d), F32)], axis=0)
    dec = jnp.pad(jnp.concatenate([ret_decay_fwd, ret_decay_bwd], axis=0), ((0, 6), (0, 128 - HEADS)))
    b_loc = lax.dynamic_slice(b_mod, (0, me * nb), (1, nb))
    modp, lgs = _mod_proj(cs, w_mod[0], b_loc, dec, "mod_proj")
    modp_all = _small_allgather(modp, "ag_mod")
    mod_l = lax.dynamic_index_in_dim(modp_all, me, axis=1, keepdims=False).reshape(N_MOD, d)
    mod_c = modp_all[:, 8, :].reshape(N_MOD, d)
    lgs = lgs[0:2, :HEADS]
    zero_row = jnp.zeros((1, d), F32)
    vec1_l = jnp.concatenate([norm1_g, mod_l[0:1], mod_l[1:2], zero_row], axis=0)
    vec1_c = jnp.concatenate([norm1_g, mod_c[0:1], mod_c[1:2], zero_row], axis=0)
    vec2 = jnp.concatenate([norm2_g, mod_l[3:4], mod_l[4:5], mod_l[2:3]], axis=0)
    vec3 = jnp.concatenate([mod_l[5:6], final_g[None]], axis=0)

    a_all = _modulate(ctxl, vec1_c, "modulate1_ctx", n_lat + n_ctx,
                      into=_modulate(xl, vec1_l, "modulate1", n_lat + n_ctx))
    arrival = me ^ jnp.array([0, 1, 4, 5, 2, 3, 6, 7], jnp.int32)
    riders = [(_place_shard(w[0], 0, me, "place_" + wname), 0)
              for wname, w in (("w_conv_out", w_conv_out), ("w_ret_out", w_ret_out), ("w_o", w_o))]
    w_in_f, w_conv_out_f, w_ret_out_f, w_o_f, z = _gather_project(
        a_all, _place_shard(w_in[0], 1, me, "place_w_in"), arrival, "proj_in", riders=riders)

    later, after = {}, (z, conv_w_f)
    for wname, w, axis in (("w_ff1", w_ff1, 1), ("w_ff2", w_ff2, 0)):
        later[wname], token = _exchange_start(_place_shard(w[0], axis, me, "place_" + wname), axis, True,
                                              "ag_" + wname + "_start", after=after)
        after = (token,)
    pc = _conv_fwd(z, conv_w_f, n_lat, "conv_fwd", token)
    pos = jnp.concatenate([n_ctx + jnp.arange(n_lat, dtype=F32), jnp.arange(n_ctx, dtype=F32)])
    cos, sin = _rope_tables(pos, dk)
    q_s, k_s, v_s = _rotary_fwd(z, cos, sin, n_lat, "rotary_fwd")
    o, st_f, st_b = _ret_fwd(q_s, k_s, v_s, lgs, n_lat, "ret_fwd")
    r = _gn_fwd(o, z, "gn_fwd")
    y_conv = _mm(pc, w_conv_out_f, "nn", [F32], "proj_conv_out", tn=2048)
    tn_d = _tile(d, 1024, 128)
    gate_offs = (CGC * d // tn_d, CGR * d // tn_d)

    def merge(acc, yc, gc, gr):
        return acc, _sig(gc) * yc + _sig(gr) * acc

    y_ret, mg = _mm(r, w_ret_out_f, "nn", [F32, BF16], "proj_ret_out", tm=512, tn=tn_d, tk=2 * d,
                    extras=[(y_conv, 0), (z, gate_offs[0]), (z, gate_offs[1])], epilogue=merge)
    y_l = _mm(mg, w_o_f, "nn", [F32], "proj_o", tn=2048)
    x1, a2 = _resid_modulate(xl, y_l, vec2, "resid_modulate2")

    def sqrelu(acc):
        return acc, jnp.square(jnp.maximum(acc, 0.0))

    w_ff1_f = _exchange_wait(later["w_ff1"], a2, "ag_w_ff1_wait")
    hff, s = _mm(a2, w_ff1_f, "nn", [BF16, BF16], "ff1", tn=2048, epilogue=sqrelu)
    w_ff2_f = _exchange_wait(later["w_ff2"], s, "ag_w_ff2_wait")
    f = _mm(s, w_ff2_f, "nn", [F32], "ff2")
    dx2, df, acc3 = _loss_head(x1, f, target, vec3, "loss_head")

    def d_sqrelu(acc, h):
        return (acc * (2.0 * jnp.maximum(h.astype(F32), 0.0)),)

    dh = _mm(df, w_ff2_f, "nt", [BF16], "ff2_dx", extras=[(hff, 0)], epilogue=d_sqrelu)
    sent = {}
    sent["w_ff2"], token = _exchange_start(_mm(s, df, "tn", [BF16], "ff2_dw", tn=2048), 0, False, "rs_w_ff2_start")
    da2 = _mm(dh, w_ff1_f, "nt", [F32], "ff1_dx", dep=token)
    sent["w_ff1"], token = _exchange_start(_mm(a2, dh, "tn", [BF16], "ff1_dw", tn=2048), 1, False, "rs_w_ff1_start")
    dx1, dyl, acc2 = _modulate_bwd(da2, x1, vec2, "modulate2_bwd", dx_in=dx2, y=y_l)

    def d_merge(acc, yc, yr, gc, gr):
        sc, sr = _sig(gc), _sig(gr)
        return acc * sc, acc * sr, acc * yc * (sc * (1.0 - sc)), acc * yr * (sr * (1.0 - sr))

    dyc, dyr, dgc, dgr = _mm(dyl, w_o_f, "nt", [BF16] * 4, "proj_o_dx", tm=512, tn=tn_d,
                             extras=[(y_conv, 0), (y_ret, 0), (z, gate_offs[0]), (z, gate_offs[1])], epilogue=d_merge,
                             dep=token)
    sent["w_o"], token = _exchange_start(_mm(mg, dyl, "tn", [BF16], "proj_o_dw", tn=2048), 0, False, "rs_w_o_start")
    dpc = _mm(dyc, w_conv_out_f, "nt", [F32], "proj_conv_out_dx", tn=2048, dep=token)
    sent["w_conv_out"], token = _exchange_start(_mm(pc, dyc, "tn", [BF16], "proj_conv_out_dw", tn=2048), 0, False,
                                                "rs_w_conv_out_start")
    dz_conv, acc_conv = _conv_bwd(dpc, z, conv_w_f, "conv_bwd")
    dv_head = 2 * d // HEADS
    do, dz_g = _mm(dyr, w_ret_out_f, "nt", [BF16, BF16], "proj_ret_out_dx", tn=dv_head,
                   extras=[(o, 0), (z, CG * d // dv_head)], epilogue=_gn_bwd_head, dep=token)
    sent["w_ret_out"], token = _exchange_start(_mm(r, dyr, "tn", [BF16], "proj_ret_out_dw", tn=2048), 0, False,
                                               "rs_w_ret_out_start")
    dq_f, dk_f, dv_f, dlg_f, dq_b, dk_b, dv_b, dlg_b = _ret_bwd(q_s, k_s, v_s, do, st_f, st_b, lgs, "ret_bwd")
    dz = _dz_assemble(dq_f, dq_b, dk_f, dk_b, dv_f, dv_b, cos, sin, dz_conv, dz_g, dgc, dgr, n_lat, n_ctx,
                      "dz_assemble")
    g_in = _mm(a_all, dz, "tn", [BF16], "proj_in_dw", tn=2048, tk=(n_lat + n_ctx) // 4, dep=token)
    sent["w_in"], token = _exchange_start(g_in, 1, False, "rs_w_in_start")
    da_all = _mm(dz, w_in_f, "nt", [F32], "proj_in_dx", tn=2048, dep=token)
    grad_x, acc1 = _modulate_bwd(da_all, xl, vec1_l, "modulate1_bwd", dx_in=dx1)
    _, acc1c = _modulate_bwd(da_all, ctxl, vec1_c, "modulate1_ctx_bwd", da_off=n_lat)

    lane_pad = functools.partial(_pad_lanes, width=d)
    packet = jnp.concatenate([
        acc1[2:3] + acc1c[2:3], acc2[2:3], acc3[1:2],
        acc1[0:1], acc1[1:2], acc2[3:4], acc2[0:1], acc2[1:2], acc3[0:1],
        acc1c[0:1], acc1c[1:2],
        acc_conv[0:3],
        lane_pad(dlg_f[:, 0, 0][None]), lane_pad(dlg_b[:, 0, 0][None]),
        acc3[2:3],
        jnp.zeros((7, d), F32)], axis=0)
    n_rows = packet.shape[0]
    packets = lax.dynamic_update_slice(jnp.zeros((N_DEV * n_rows, d), F32), packet, (me * n_rows, 0))
    packets_sent, after = _exchange_start(packets, 0, True, "ag_small_start")
    res = {}

    def update(wname, axis, w, m, v, after):
        g_full, parts = _exchange_wait(sent[wname], after, "rs_" + wname + "_wait")
        res[wname] = _adamw(parts, w[0], m[0], v[0], "adamw_" + wname, own=(g_full, axis, me))
        return res[wname][0]

    for wname, axis, w, m, v in (("w_ff2", 0, w_ff2, m_w_ff2, v_w_ff2), ("w_ff1", 1, w_ff1, m_w_ff1, v_w_ff1),
                                 ("w_o", 0, w_o, m_w_o, v_w_o), ("w_conv_out", 0, w_conv_out, m_w_conv_out, v_w_conv_out),
                                 ("w_ret_out", 0, w_ret_out, m_w_ret_out, v_w_ret_out)):
        after = update(wname, axis, w, m, v, after)
    gathered = _exchange_wait(packets_sent, after, "ag_small_wait").reshape(N_DEV, n_rows, d)
    red, misc = _reduce_small(gathered, dec, d, "reduce_small")
    dmod_ctx = jnp.concatenate([red[9], red[10], jnp.zeros((4 * d,), F32)])
    dmod_all = jnp.concatenate([gathered[:, 3:9, :].reshape(N_DEV, N_MOD * d), dmod_ctx[None]], axis=0)
    dm = jnp.pad(lax.dynamic_slice(dmod_all, (0, me * nb), (N_DEV + 1, nb)), ((0, 7), (0, 0)))
    g_mod, c_part = _mod_grad(cs, dm, w_mod[0], "mod_grad")
    g_b_mod = red[3:9].reshape(1, N_MOD * d) + dmod_ctx[None]
    g_conv_w = lax.dynamic_slice(red[11:14], (0, me * (d // N_DEV)), (3, d // N_DEV))

    c_parts = lax.dynamic_update_slice(jnp.zeros((N_DEV * 16, d), F32), c_part, (me * 16, 0))
    c_parts_sent, token = _exchange_start(c_parts, 0, True, "ag_c_ctx_start")
    res["w_mod"] = _adamw(g_mod[None], w_mod[0], m_w_mod[0], v_w_mod[0], "adamw_w_mod")
    after = update("w_in", 1, w_in, m_w_in, v_w_in, token)
    c_parts = _exchange_wait(c_parts_sent, after, "ag_c_ctx_wait").reshape(N_DEV, 16, d)
    g_c_ctx = _c_ctx_grad(c_parts, c_ctx[None], "c_ctx_grad")
    res = {k: tuple(t[None] for t in val) for k, val in res.items()}

    small = [("c_ctx", g_c_ctx, c_ctx, m_c_ctx, v_c_ctx), ("b_mod", g_b_mod, b_mod, m_b_mod, v_b_mod),
             ("norm1_g", red[0:1], norm1_g, m_norm1_g, v_norm1_g), ("conv_w", g_conv_w, conv_w, m_conv_w, v_conv_w),
             ("ret_decay_fwd", misc[1:2, :HEADS], ret_decay_fwd, m_ret_decay_fwd, v_ret_decay_fwd),
             ("ret_decay_bwd", misc[2:3, :HEADS], ret_decay_bwd, m_ret_decay_bwd, v_ret_decay_bwd),
             ("norm2_g", red[1:2], norm2_g, m_norm2_g, v_norm2_g), ("final_g", red[2:3], final_g, m_final_g, v_final_g)]

    def flat(t):
        t = t.reshape(-1)
        return jnp.pad(t, (0, (-t.shape[0]) % 1024))

    packed = [jnp.concatenate([flat(item[j]) for item in small]).reshape(-1, 128) for j in range(1, 5)]
    outs = _adamw(packed[0][None], packed[1], packed[2], packed[3], "adamw_small")
    start = 0
    for name, _, w, _, _ in small:
        size = w.size
        res[name] = tuple(o.reshape(-1)[start:start + size].reshape(w.shape) for o in outs)
        start += size + (-size) % 1024

    order = ["c_ctx", "w_mod", "b_mod", "norm1_g", "w_in", "conv_w", "w_conv_out", "ret_decay_fwd", "ret_decay_bwd",
             "w_ret_out", "w_o", "norm2_g", "w_ff1", "w_ff2", "final_g"]
    loss = misc[0, 0]
    return (loss, grad_x[None], *[res[n][0] for n in order], *[res[n][1] for n in order],
            *[res[n][2] for n in order], *[res[n][3] for n in order])
```

```python
import functools

import jax
import jax.numpy as jnp
from jax import lax
from jax.experimental import pallas as pl
from jax.experimental.pallas import tpu as pltpu

F32 = jnp.float32
BF16 = jnp.bfloat16
MESH = pl.DeviceIdType.MESH

N_DEV = 8
HEADS = 8
N_MOD = 6
N_IN = 11
GRID_W = 64
CHUNK = 256
ROPE_BASE = 10000.0
EPS = 1e-6
ADAM_LR, ADAM_B1, ADAM_B2, ADAM_EPS, ADAM_WD, ADAM_STEP = 0.001, 0.9, 0.999, 1e-08, 0.01, 10
VMEM_LIMIT = 56 * 1024 * 1024
HIGHEST = lax.Precision.HIGHEST
CB, CC, CX, CQ, CK, CV, CG, CGC, CGR = 0, 1, 2, 3, 4, 5, 7, 9, 10


def _tile(n, target, mult):
    t = (min(target, n) // mult) * mult
    while t >= mult:
        if n % t == 0:
            return t
        t -= mult
    return n


def _params(sem=None):
    return pltpu.CompilerParams(dimension_semantics=sem, vmem_limit_bytes=VMEM_LIMIT)


def _sig(v):
    return 1.0 / (1.0 + jnp.exp(-v))


def _coords():
    return lax.axis_index("x"), lax.axis_index("y"), lax.axis_index("c")


def _flip(p, m):
    return tuple(1 - v if (m >> s) & 1 else v for v, s in zip(p, (2, 1, 0)))


def _index(p):
    return 4 * p[0] + 2 * p[1] + p[2]


def _small_allgather(x, name):
    r, n = x.shape

    def body(x_ref, out_ref, send_sems, recv_sems):
        me = _coords()
        out_ref[pl.ds(_index(me), 1)] = x_ref[...][None]
        sent = []
        for m in range(1, N_DEV):
            cp = pltpu.make_async_remote_copy(
                src_ref=x_ref, dst_ref=out_ref.at[_index(me)], send_sem=send_sems.at[m - 1],
                recv_sem=recv_sems.at[m - 1], device_id=_flip(me, m), device_id_type=MESH)
            cp.start()
            sent.append(cp)
        for m in range(1, N_DEV):
            pltpu.make_async_remote_copy(
                src_ref=x_ref, dst_ref=out_ref.at[_index(_flip(me, m))], send_sem=send_sems.at[m - 1],
                recv_sem=recv_sems.at[m - 1], device_id=_flip(me, m), device_id_type=MESH).wait_recv()
        for cp in sent:
            cp.wait_send()

    return pl.pallas_call(
        body, name=name, out_shape=jax.ShapeDtypeStruct((N_DEV, r, n), x.dtype),
        in_specs=[pl.BlockSpec(memory_space=pltpu.VMEM)], out_specs=pl.BlockSpec(memory_space=pltpu.VMEM),
        scratch_shapes=[pltpu.SemaphoreType.DMA((N_DEV - 1,)), pltpu.SemaphoreType.DMA((N_DEV - 1,))],
    )(x)


def _window(ref, j, r, c, axis):
    if axis == 0:
        return ref.at[pl.ds(pl.multiple_of(j * r, 8), r), :]
    return ref.at[:, pl.ds(pl.multiple_of(j * c, 128), c)]


_HBM = pl.BlockSpec(memory_space=pltpu.HBM)
_SEM = pl.BlockSpec(memory_space=pltpu.SEMAPHORE)
_ANY = pl.BlockSpec(memory_space=pl.ANY)
_EFFECT = pltpu.SideEffectType.DATAFLOW_SIDE_EFFECTING


def _place_shard(w, axis, me, name):
    r, c = w.shape
    full = (N_DEV * r, c) if axis == 0 else (r, N_DEV * c)
    tr = _tile(r, max(16, (2 * 1024 * 1024) // (c * 4)), 16)
    blocks = r // tr

    def out_map(i, me_ref):
        return (me_ref[0] * blocks + i, 0) if axis == 0 else (i, me_ref[0])

    def body(me_ref, w_ref, out_ref):
        out_ref[...] = w_ref[...].astype(BF16)

    return pl.pallas_call(
        body, name=name, out_shape=jax.ShapeDtypeStruct(full, BF16),
        grid_spec=pltpu.PrefetchScalarGridSpec(
            num_scalar_prefetch=1, grid=(blocks,), in_specs=[pl.BlockSpec((tr, c), lambda i, me_ref: (i, 0))],
            out_specs=pl.BlockSpec((tr, c), out_map)),
        compiler_params=_params(("parallel",)))(me.reshape(1), w)


def _exchange_start(src, axis, gather, name, after=()):
    r, c = (src.shape[0] // N_DEV, src.shape[1]) if axis == 0 else (src.shape[0], src.shape[1] // N_DEV)
    n_hbm = 1 if gather else 2
    n_after = len(after)

    def body(*refs):
        src_ref, land_ref = refs[0], refs[n_hbm - 1]
        send_sems, recv_sems = refs[n_hbm + n_after:n_hbm + n_after + 2]
        token = refs[-1]
        me = _coords()
        for m in range(1, N_DEV):
            peer = _flip(me, m)
            if gather:
                mine = theirs = _window(land_ref, _index(me), r, c, axis)
            else:
                mine, theirs = _window(src_ref, _index(peer), r, c, axis), land_ref.at[m - 1]
            pltpu.make_async_remote_copy(
                src_ref=mine, dst_ref=theirs, send_sem=send_sems.at[m - 1], recv_sem=recv_sems.at[m - 1],
                device_id=peer, device_id_type=MESH).start()
        token[...] = jnp.zeros_like(token)

    hbm = [pltpu.with_memory_space_constraint(src, pltpu.HBM)]
    if not gather:
        hbm.append(pltpu.with_memory_space_constraint(lax.empty((N_DEV - 1, r, c), src.dtype), pltpu.HBM))
    outs = pl.pallas_call(
        body, name=name,
        out_shape=(pltpu.SemaphoreType.DMA((N_DEV - 1,)), pltpu.SemaphoreType.DMA((N_DEV - 1,)),
                   *[pltpu.HBM(t.shape, t.dtype) for t in hbm], jax.ShapeDtypeStruct((8, 128), F32)),
        in_specs=[_HBM] * n_hbm + [_ANY] * n_after,
        out_specs=(_SEM, _SEM, *[_HBM] * n_hbm, pl.BlockSpec(memory_space=pltpu.VMEM)),
        input_output_aliases={i: 2 + i for i in range(n_hbm)},
        compiler_params=pltpu.CompilerParams(has_side_effects=_EFFECT),
    )(*hbm, *after)
    return (outs[:2], outs[2:2 + n_hbm], (axis, gather, r, c)), outs[-1]


def _exchange_wait(handle, after, name):
    (send_sems, recv_sems), hbm, (axis, gather, r, c) = handle
    n_hbm = len(hbm)

    def body(*refs):
        src_ref, land_ref = refs[0], refs[n_hbm - 1]
        send_sems, recv_sems = refs[n_hbm:n_hbm + 2]
        me = _coords()
        for m in range(1, N_DEV):
            peer = _flip(me, m)
            if gather:
                mine, theirs = _window(land_ref, _index(me), r, c, axis), _window(land_ref, _index(peer), r, c, axis)
            else:
                mine, theirs = _window(src_ref, _index(peer), r, c, axis), land_ref.at[m - 1]
            copy = pltpu.make_async_remote_copy(
                src_ref=mine, dst_ref=theirs, send_sem=send_sems.at[m - 1], recv_sem=recv_sems.at[m - 1],
                device_id=peer, device_id_type=MESH)
            copy.wait_send()
            copy.wait_recv()

    outs = pl.pallas_call(
        body, name=name, out_shape=tuple(pltpu.HBM(t.shape, t.dtype) for t in hbm),
        in_specs=[_HBM] * n_hbm + [_SEM, _SEM, _ANY], out_specs=tuple([_HBM] * n_hbm),
        input_output_aliases={i: i for i in range(n_hbm)},
        compiler_params=pltpu.CompilerParams(has_side_effects=_EFFECT),
    )(*hbm, send_sems, recv_sems, after)
    return outs[0] if gather else tuple(outs)


def _mm(a, b, mode, out_dtypes, name, tm=1024, tn=1024, tk=2048, extras=(), epilogue=None, dep=None):
    if mode == "nn":
        (m, k), n = a.shape, b.shape[1]
    elif mode == "nt":
        (m, k), n = a.shape, b.shape[0]
    else:
        (k, m), n = a.shape, b.shape[1]
    tm, tn = _tile(m, tm, 8), _tile(n, tn, min(128, tn))
    tk = _tile(k, tk, 16 if mode == "tn" else 128)
    nk = k // tk
    swap = nk == 1 and (k * n + (n // tn) * m * k) < (m * k + (m // tm) * k * n)

    def ij(p, q):
        return (q, p) if swap else (p, q)

    def spec(shape, fn):
        return pl.BlockSpec(shape, lambda p, q, kk: fn(*ij(p, q), kk))

    a_spec = spec((tk, tm), lambda i, j, kk: (kk, i)) if mode == "tn" else spec((tm, tk), lambda i, j, kk: (i, kk))
    b_spec = spec((tn, tk), lambda i, j, kk: (j, kk)) if mode == "nt" else spec((tk, tn), lambda i, j, kk: (kk, j))
    dims = {"nn": (((1,), (0,)), ((), ())), "nt": (((1,), (1,)), ((), ())), "tn": (((0,), (0,)), ((), ()))}[mode]
    ex_specs = [spec((tm, tn), functools.partial(lambda i, j, kk, off: (i, j + off), off=off)) for _, off in extras]
    deps = [] if dep is None else [dep]
    dep_specs = [pl.BlockSpec(dep.shape, lambda p, q, kk: (0, 0))] if deps else []
    n_ex, n_out = len(extras), len(out_dtypes)
    n_in = 2 + n_ex + len(deps)

    def body(*refs):
        a_ref, b_ref = refs[0], refs[1]
        ex_refs = refs[2:2 + n_ex]
        out_refs = refs[n_in:n_in + n_out]

        def product():
            return lax.dot_general(a_ref[...], b_ref[...], dims, preferred_element_type=F32)

        def finish(res):
            res = epilogue(res, *[e[...] for e in ex_refs]) if epilogue is not None else (res,)
            for o_ref, val in zip(out_refs, res):
                o_ref[...] = val.astype(o_ref.dtype)

        if nk == 1:
            finish(product())
            return
        acc = refs[-1]
        kk = pl.program_id(2)

        @pl.when(kk == 0)
        def _():
            acc[...] = product()

        @pl.when((kk > 0) & (kk < nk - 1))
        def _():
            acc[...] += product()

        @pl.when(kk == nk - 1)
        def _():
            finish(acc[...] + product())

    outs = pl.pallas_call(
        body, name=name, grid=(*ij(m // tm, n // tn), nk),
        in_specs=[a_spec, b_spec] + ex_specs + dep_specs,
        out_specs=[spec((tm, tn), lambda i, j, kk: (i, j)) for _ in out_dtypes],
        out_shape=[jax.ShapeDtypeStruct((m, n), dt) for dt in out_dtypes],
        scratch_shapes=[pltpu.VMEM((tm, tn), F32)] if nk > 1 else [],
        compiler_params=_params(("parallel", "parallel", "arbitrary")),
    )(a, b, *[e for e, _ in extras], *deps)
    return outs if n_out > 1 else outs[0]


def _gather_project(a, w_full, order, name, riders=(), tm=768):
    m, k = a.shape
    c = w_full.shape[1] // N_DEV
    tm = _tile(m, tm, 8)
    n_i = m // tm
    chips = [4, 2, 6]
    n_rid = len(riders)
    shard = [(k, c, 1)] + [((t.shape[0] // N_DEV, t.shape[1], 0) if ax == 0 else (t.shape[0], t.shape[1] // N_DEV, 1))
                           for t, ax in riders]

    def body(order_ref, a_ref, *refs):
        outs = refs[1 + n_rid:2 + 2 * n_rid]
        z_ref, wbuf, send_sems, recv_sems, load_sems = refs[2 + 2 * n_rid:]
        w_ref = outs[0]
        p, i = pl.program_id(0), pl.program_id(1)
        me = _coords()
        sibling = _flip(me, 1)

        def copy(t, s, block, to):
            r_t, c_t, ax_t = shard[t]
            win = _window(outs[t], _index(block), r_t, c_t, ax_t)
            return pltpu.make_async_remote_copy(src_ref=win, dst_ref=win, send_sem=send_sems.at[7 * t + s],
                                                recv_sem=recv_sems.at[7 * t + s], device_id=to, device_id_type=MESH)

        def first(t):
            return [copy(t, 0, me, sibling)] + [copy(t, 1 + j, me, _flip(me, mm)) for j, mm in enumerate(chips)]

        def passed(t):
            return [copy(t, 4 + j, _flip(me, mm), sibling) for j, mm in enumerate(chips)]

        @pl.when((p == 0) & (i == 0))
        def _():
            for t in range(1 + n_rid):
                for cp in first(t):
                    cp.start()

        def load(piece):
            return pltpu.make_async_copy(_window(w_ref, order_ref[piece], k, c, 1), wbuf.at[piece % 2],
                                         load_sems.at[piece % 2])

        @pl.when((p == 0) & (i == 0))
        def _():
            load(p).start()

        arrivals = [(0, None)] + [x for j in range(3) for x in ((1 + j, passed(0)[j]), (4 + j, None))]
        for piece, (sem, forward) in enumerate(arrivals, start=1):
            @pl.when((p == piece - 1) & (i == n_i - 1))
            def _(sem=sem, forward=forward, piece=piece):
                src_block = (sibling if piece == 1 else
                             _flip(me if piece % 2 == 0 else sibling, chips[(piece - 2) // 2]))
                copy(0, sem, src_block, me).wait_recv()
                if forward is not None:
                    forward.start()
                load(p + 1).start()

        @pl.when(i == 0)
        def _():
            load(p).wait()

        @pl.when((p == N_DEV - 1) & (i == 0))
        def _():
            for t in range(1, 1 + n_rid):
                for j, mm in enumerate(chips):
                    copy(t, 1 + j, _flip(me, mm), me).wait_recv()
                    passed(t)[j].start()

        z_ref[...] = jnp.dot(a_ref[...], wbuf[p % 2], preferred_element_type=F32)

        @pl.when((p == N_DEV - 1) & (i == n_i - 1))
        def _():
            for t in range(1, 1 + n_rid):
                copy(t, 0, sibling, me).wait_recv()
                for j, mm in enumerate(chips):
                    copy(t, 4 + j, _flip(sibling, mm), me).wait_recv()
            for t in range(1 + n_rid):
                for cp in first(t) + passed(t):
                    cp.wait_send()

    gathered = [w_full] + [t for t, _ in riders]
    n_sem = 7 * len(gathered)
    return pl.pallas_call(
        body, name=name,
        out_shape=[jax.ShapeDtypeStruct(t.shape, t.dtype) for t in gathered]
        + [jax.ShapeDtypeStruct((m, N_DEV * c), F32)],
        grid_spec=pltpu.PrefetchScalarGridSpec(
            num_scalar_prefetch=1, grid=(N_DEV, n_i),
            in_specs=[pl.BlockSpec((tm, k), lambda p, i, order_ref: (i, 0))]
            + [pl.BlockSpec(memory_space=pl.ANY)] * len(gathered),
            out_specs=[pl.BlockSpec(memory_space=pl.ANY)] * len(gathered)
            + [pl.BlockSpec((tm, c), lambda p, i, order_ref: (i, order_ref[p]))],
            scratch_shapes=[pltpu.VMEM((2, k, c), w_full.dtype), pltpu.SemaphoreType.DMA((n_sem,)),
                            pltpu.SemaphoreType.DMA((n_sem,)), pltpu.SemaphoreType.DMA((2,))]),
        input_output_aliases={2 + t: t for t in range(len(gathered))},
        compiler_params=_params(("arbitrary", "arbitrary")))(order, a, *gathered)


def _rspec(tr, w, cb=0, off=0):
    return pl.BlockSpec((tr, w), lambda i: (i + off, cb))


def _cspec(shape):
    return pl.BlockSpec(shape, lambda i: (0,) * len(shape))


def _rms(xf):
    rstd = lax.rsqrt(jnp.mean(xf * xf, axis=-1, keepdims=True) + EPS)
    return xf * rstd, rstd


def _rms_bwd(dn, n, rstd):
    return rstd * (dn - n * jnp.mean(dn * n, axis=-1, keepdims=True))


def _colsum(v):
    return jnp.sum(v, axis=0, keepdims=True)


def _total(v):
    return jnp.sum(jnp.sum(v, axis=1, keepdims=True), axis=0, keepdims=True)


def _modulate(x, vec, name, total_rows, into=None):
    rows, d = x.shape
    tr = _tile(rows, 256, 8)
    off = 0 if into is None else (total_rows - rows) // tr

    def body(x_ref, v_ref, *refs):
        n, _ = _rms(x_ref[...])
        refs[-1][...] = ((n * v_ref[0:1, :]) * (1.0 + v_ref[2:3, :]) + v_ref[1:2, :]).astype(BF16)

    return pl.pallas_call(
        body, name=name, grid=(rows // tr,),
        in_specs=[_rspec(tr, d), _cspec(vec.shape)] + ([] if into is None else [_ANY]),
        out_specs=_rspec(tr, d, off=off), out_shape=jax.ShapeDtypeStruct((total_rows, d), BF16),
        input_output_aliases={} if into is None else {2: 0},
        compiler_params=_params(("parallel",)))(x, vec, *([] if into is None else [into]))


def _resid_modulate(x, y, vec, name):
    rows, d = x.shape
    tr = _tile(rows, 256, 8)

    def body(x_ref, y_ref, v_ref, x1_ref, a_ref):
        x1 = x_ref[...] + v_ref[3:4, :] * y_ref[...]
        x1_ref[...] = x1
        n, _ = _rms(x1)
        a_ref[...] = ((n * v_ref[0:1, :]) * (1.0 + v_ref[2:3, :]) + v_ref[1:2, :]).astype(BF16)

    return pl.pallas_call(
        body, name=name, grid=(rows // tr,), in_specs=[_rspec(tr, d), _rspec(tr, d), _cspec(vec.shape)],
        out_specs=[_rspec(tr, d), _rspec(tr, d)],
        out_shape=[jax.ShapeDtypeStruct((rows, d), F32), jax.ShapeDtypeStruct((rows, d), BF16)],
        compiler_params=_params(("parallel",)))(x, y, vec)


def _loss_head(x1, f, target, vec, name):
    rows, d = x1.shape
    tr = _tile(rows, 256, 8)

    def body(x1_ref, f_ref, t_ref, v_ref, dx2_ref, df_ref, acc_ref):
        @pl.when(pl.program_id(0) == 0)
        def _():
            acc_ref[...] = jnp.zeros_like(acc_ref)

        gate, gain = v_ref[0:1, :], v_ref[1:2, :]
        fv = f_ref[...]
        n, rstd = _rms(x1_ref[...] + gate * fv)
        err = n * gain - t_ref[...]
        dy = err / d
        dx2 = _rms_bwd(dy * gain, n, rstd)
        dx2_ref[...] = dx2
        df_ref[...] = (dx2 * gate).astype(BF16)
        acc_ref[0:1, :] += _colsum(dx2 * fv)
        acc_ref[1:2, :] += _colsum(dy * n)
        acc_ref[2:3, :] += _colsum(err * err)

    return pl.pallas_call(
        body, name=name, grid=(rows // tr,),
        in_specs=[_rspec(tr, d), _rspec(tr, d), _rspec(tr, d), _cspec(vec.shape)],
        out_specs=[_rspec(tr, d), _rspec(tr, d), _cspec((8, d))],
        out_shape=[jax.ShapeDtypeStruct((rows, d), F32), jax.ShapeDtypeStruct((rows, d), BF16),
                   jax.ShapeDtypeStruct((8, d), F32)],
        compiler_params=_params(("arbitrary",)))(x1, f, target, vec)


def _modulate_bwd(da, x, vec, name, da_off=0, dx_in=None, y=None):
    rows, d = x.shape
    tr = _tile(rows, 256, 8)
    off = da_off // tr
    has_dx, has_y = dx_in is not None, y is not None

    def body(*refs):
        da_ref, x_ref, v_ref = refs[0], refs[1], refs[2]
        pos = 3
        dxin_ref = refs[pos] if has_dx else None
        pos += has_dx
        y_ref = refs[pos] if has_y else None
        pos += has_y
        dx_ref = refs[pos]
        dy_ref = refs[pos + 1] if has_y else None
        acc_ref = refs[-1]

        @pl.when(pl.program_id(0) == 0)
        def _():
            acc_ref[...] = jnp.zeros_like(acc_ref)

        gain, scale1 = v_ref[0:1, :], 1.0 + v_ref[2:3, :]
        dav = da_ref[...]
        n, rstd = _rms(x_ref[...])
        dx = _rms_bwd(dav * gain * scale1, n, rstd)
        if has_dx:
            dx = dx + dxin_ref[...]
        dx_ref[...] = dx
        acc_ref[0:1, :] += _colsum(dav)
        acc_ref[1:2, :] += _colsum(dav * (n * gain))
        acc_ref[2:3, :] += _colsum(dav * n * scale1)
        if has_y:
            acc_ref[3:4, :] += _colsum(dx * y_ref[...])
            dy_ref[...] = (dx * v_ref[3:4, :]).astype(BF16)

    ins = [da, x, vec] + ([dx_in] if has_dx else []) + ([y] if has_y else [])
    in_specs = [_rspec(tr, d, off=off), _rspec(tr, d), _cspec(vec.shape)] + [_rspec(tr, d)] * (has_dx + has_y)
    out_specs = [_rspec(tr, d)] + ([_rspec(tr, d)] if has_y else []) + [_cspec((8, d))]
    out_shape = ([jax.ShapeDtypeStruct((rows, d), F32)] + ([jax.ShapeDtypeStruct((rows, d), BF16)] if has_y else [])
                 + [jax.ShapeDtypeStruct((8, d), F32)])
    return pl.pallas_call(
        body, name=name, grid=(rows // tr,), in_specs=in_specs, out_specs=out_specs, out_shape=out_shape,
        compiler_params=_params(("arbitrary",)))(*ins)


def _conv_terms(cc, cx, w_ref, tr):
    t = lax.broadcasted_iota(jnp.int32, (tr, 1), 0) % GRID_W
    first, last = t == 0, t == GRID_W - 1
    u = cc * cx
    prev = jnp.where(first, 0.0, pltpu.roll(u, 1, 0))
    nxt = jnp.where(last, 0.0, pltpu.roll(u, tr - 1, 0))
    y = w_ref[0:1, :] * prev + w_ref[1:2, :] * u + w_ref[2:3, :] * nxt
    return u, prev, nxt, y, first, last


def _conv_fwd(z, conv_w, rows, name, dep):
    d = conv_w.shape[1]
    tr = _tile(rows, 256, GRID_W)

    def body(cb_ref, cc_ref, cx_ref, w_ref, dep_ref, out_ref):
        y = _conv_terms(cc_ref[...], cx_ref[...], w_ref, tr)[3]
        out_ref[...] = (cb_ref[...] * y).astype(BF16)

    return pl.pallas_call(
        body, name=name, grid=(rows // tr,),
        in_specs=[_rspec(tr, d, CB), _rspec(tr, d, CC), _rspec(tr, d, CX), _cspec(conv_w.shape), _cspec(dep.shape)],
        out_specs=_rspec(tr, d), out_shape=jax.ShapeDtypeStruct((rows, d), BF16),
        compiler_params=_params(("parallel",)))(z, z, z, conv_w, dep)


def _conv_bwd(dpc, z, conv_w, name):
    rows, d = dpc.shape
    tr = _tile(rows, 256, GRID_W)

    def body(dpc_ref, cb_ref, cc_ref, cx_ref, w_ref, dz_ref, acc_ref):
        @pl.when(pl.program_id(0) == 0)
        def _():
            acc_ref[...] = jnp.zeros_like(acc_ref)

        cc, cx, dpcv = cc_ref[...], cx_ref[...], dpc_ref[...]
        u, prev, nxt, y, first, last = _conv_terms(cc, cx, w_ref, tr)
        dy = dpcv * cb_ref[...]
        dy_next = jnp.where(last, 0.0, pltpu.roll(dy, tr - 1, 0))
        dy_prev = jnp.where(first, 0.0, pltpu.roll(dy, 1, 0))
        du = w_ref[0:1, :] * dy_next + w_ref[1:2, :] * dy + w_ref[2:3, :] * dy_prev
        dz_ref[:, 0:d] = (dpcv * y).astype(BF16)
        dz_ref[:, d:2 * d] = (du * cx).astype(BF16)
        dz_ref[:, 2 * d:3 * d] = (du * cc).astype(BF16)
        acc_ref[0:1, :] += _colsum(dy * prev)
        acc_ref[1:2, :] += _colsum(dy * u)
        acc_ref[2:3, :] += _colsum(dy * nxt)

    return pl.pallas_call(
        body, name=name, grid=(rows // tr,),
        in_specs=[_rspec(tr, d), _rspec(tr, d, CB), _rspec(tr, d, CC), _rspec(tr, d, CX), _cspec(conv_w.shape)],
        out_specs=[_rspec(tr, 3 * d), _cspec((8, d))],
        out_shape=[jax.ShapeDtypeStruct((rows, 3 * d), BF16), jax.ShapeDtypeStruct((8, d), F32)],
        compiler_params=_params(("arbitrary",)))(dpc, z, z, z, conv_w)


def _rotary_fwd(z, cos, sin, n_lat, name):
    rows = z.shape[0]
    d = z.shape[1] // N_IN
    dk = d // HEADS
    half = dk // 2
    tr = _tile(n_lat, 256, 8)
    tr = _tile(rows - n_lat, tr, 8)
    lat_blocks, all_blocks = n_lat // tr, rows // tr
    ctx_blocks = all_blocks - lat_blocks
    scan_rows = rows + ctx_blocks * tr

    def z_block(g):
        return jnp.where(g < all_blocks, g, g - ctx_blocks)

    def in_spec(w, cb=0):
        return pl.BlockSpec((tr, w), lambda g: (z_block(g), cb))

    def out_spec(w):
        return pl.BlockSpec((tr, w), lambda g: (jnp.where(g < all_blocks, g + ctx_blocks, g - all_blocks), 0))

    def body(q_ref, k_ref, v0_ref, v1_ref, cos_ref, sin_ref, qo_ref, ko_ref, vo_ref):
        cs, sn = cos_ref[...], sin_ref[...]
        keep = jnp.where(z_block(pl.program_id(0)) < lat_blocks, 1.0, 0.0)
        for src, dst, scale in ((q_ref, qo_ref, keep), (k_ref, ko_ref, dk ** -0.5)):
            for h in range(HEADS):
                lo, mid, hi = h * dk, h * dk + half, (h + 1) * dk
                t1, t2 = src[:, lo:mid], src[:, mid:hi]
                dst[:, lo:mid] = ((t1 * cs - t2 * sn) * scale).astype(BF16)
                dst[:, mid:hi] = ((t1 * sn + t2 * cs) * scale).astype(BF16)
        vo_ref[:, 0:d] = v0_ref[...].astype(BF16)
        vo_ref[:, d:2 * d] = v1_ref[...].astype(BF16)

    return pl.pallas_call(
        body, name=name, grid=(all_blocks + ctx_blocks,),
        in_specs=[in_spec(d, CQ), in_spec(d, CK), in_spec(d, CV), in_spec(d, CV + 1), in_spec(half), in_spec(half)],
        out_specs=[out_spec(d), out_spec(d), out_spec(2 * d)],
        out_shape=[jax.ShapeDtypeStruct((scan_rows, d), BF16), jax.ShapeDtypeStruct((scan_rows, d), BF16),
                   jax.ShapeDtypeStruct((scan_rows, 2 * d), BF16)],
        compiler_params=_params(("parallel",)))(z, z, z, z, cos, sin)


def _dz_assemble(dq_f, dq_b, dk_f, dk_b, dv_f, dv_b, cos, sin, dz_conv, dz_g, dz_gc, dz_gr, n_lat, n_ctx, name):
    rows = n_lat + n_ctx
    d = dq_f.shape[1]
    dk = d // HEADS
    half = dk // 2
    tr = _tile(n_ctx, 128, 8)
    lat_blocks, ctx_blocks = n_lat // tr, n_ctx // tr

    def fmap(i):
        return (jnp.where(i < lat_blocks, i + ctx_blocks, i - lat_blocks), 0)

    def bmap(i):
        return (i + ctx_blocks, 0)

    def lmap(i):
        return (jnp.minimum(i, lat_blocks - 1), 0)

    def body(qf_ref, qb_ref, kf_ref, kb_ref, vf_ref, vb_ref, cos_ref, sin_ref, c_ref, g_ref, gc_ref, gr_ref, out_ref):
        cs, sn = cos_ref[...], sin_ref[...]
        is_lat = pl.program_id(0) < lat_blocks
        keep = jnp.where(is_lat, 1.0, 0.0)
        for fa, fb, base, scale in ((qf_ref, qb_ref, CQ * d, keep), (kf_ref, kb_ref, CK * d, dk ** -0.5)):
            for h in range(HEADS):
                lo, mid, hi = h * dk, h * dk + half, (h + 1) * dk
                d1 = fa[:, lo:mid].astype(F32) + fb[:, lo:mid].astype(F32)
                d2 = fa[:, mid:hi].astype(F32) + fb[:, mid:hi].astype(F32)
                out_ref[:, base + lo:base + mid] = ((d1 * cs + d2 * sn) * scale).astype(BF16)
                out_ref[:, base + mid:base + hi] = ((d2 * cs - d1 * sn) * scale).astype(BF16)
        out_ref[:, CV * d:CG * d] = (vf_ref[...].astype(F32) + vb_ref[...].astype(F32)).astype(BF16)
        for src, lo, hi in ((c_ref, CB * d, CQ * d), (g_ref, CG * d, CGC * d), (gc_ref, CGC * d, CGR * d),
                            (gr_ref, CGR * d, N_IN * d)):
            out_ref[:, lo:hi] = jnp.where(is_lat, src[...], jnp.zeros_like(src))

    return pl.pallas_call(
        body, name=name, grid=(rows // tr,),
        in_specs=[pl.BlockSpec((tr, d), fmap), pl.BlockSpec((tr, d), bmap), pl.BlockSpec((tr, d), fmap),
                  pl.BlockSpec((tr, d), bmap), pl.BlockSpec((tr, 2 * d), fmap), pl.BlockSpec((tr, 2 * d), bmap),
                  _rspec(tr, half), _rspec(tr, half), pl.BlockSpec((tr, 3 * d), lmap), pl.BlockSpec((tr, 2 * d), lmap),
                  pl.BlockSpec((tr, d), lmap), pl.BlockSpec((tr, d), lmap)],
        out_specs=_rspec(tr, N_IN * d), out_shape=jax.ShapeDtypeStruct((rows, N_IN * d), BF16),
        compiler_params=_params(("parallel",)))(dq_f, dq_b, dk_f, dk_b, dv_f, dv_b, cos, sin, dz_conv, dz_g, dz_gc, dz_gr)


def _gn_fwd(o, z, name):
    rows = o.shape[0]
    d = z.shape[1] // N_IN
    dv = 2 * d // HEADS
    tr = _tile(rows, 128, 8)

    def body(o_ref, g0_ref, g1_ref, r_ref):
        for h in range(HEADS):
            lo, hi = h * dv, (h + 1) * dv
            g_ref, glo = (g0_ref, lo) if hi <= d else (g1_ref, lo - d)
            o = o_ref[:, lo:hi]
            cen = o - jnp.mean(o, axis=-1, keepdims=True)
            on = cen * lax.rsqrt(jnp.mean(cen * cen, axis=-1, keepdims=True) + EPS)
            g = g_ref[:, glo:glo + dv]
            r_ref[:, lo:hi] = (g * _sig(g) * on).astype(BF16)

    return pl.pallas_call(
        body, name=name, grid=(rows // tr,),
        in_specs=[_rspec(tr, 2 * d), _rspec(tr, d, CG), _rspec(tr, d, CG + 1)],
        out_specs=_rspec(tr, 2 * d), out_shape=jax.ShapeDtypeStruct((rows, 2 * d), BF16),
        compiler_params=_params(("parallel",)))(o, z, z)


def _gn_bwd_head(dr, o, g):
    cen = o - jnp.mean(o, axis=-1, keepdims=True)
    rstd = lax.rsqrt(jnp.mean(cen * cen, axis=-1, keepdims=True) + EPS)
    on = cen * rstd
    sg = _sig(g)
    don = dr * (g * sg)
    do = rstd * (don - jnp.mean(don, axis=-1, keepdims=True) - on * jnp.mean(don * on, axis=-1, keepdims=True))
    return do, dr * on * (sg * (1.0 + g * (1.0 - sg)))


def _decays(lg, rev):
    row = lax.broadcasted_iota(jnp.int32, (CHUNK, CHUNK), 0)
    col = lax.broadcasted_iota(jnp.int32, (CHUNK, CHUNK), 1)
    rel = ((col - row) if rev else (row - col)).astype(F32)
    mask = jnp.where(rel >= 0, jnp.exp(lg * jnp.maximum(rel, 0.0)), 0.0)
    r = lax.broadcasted_iota(jnp.int32, (CHUNK, 1), 0)
    rr = ((CHUNK - 1 - r) if rev else r).astype(F32)
    chunk_decay = jnp.exp(lg * jnp.full((1, 1), float(CHUNK), F32))
    return (rr, rel), mask, jnp.exp(lg * (rr + 1.0)), jnp.exp(lg * (CHUNK - 1.0 - rr)), chunk_decay


_NT = (((1,), (1,)), ((), ()))
_TN = (((0,), (0,)), ((), ()))


def _dot(a, b, dims=None):
    if dims is None:
        return jnp.dot(a, b, preferred_element_type=F32)
    return lax.dot_general(a, b, dims, preferred_element_type=F32)


def _ret_fwd(q, k, v, lgs, n_lat, name):
    tt, d = q.shape
    dk, dv = d // HEADS, 2 * d // HEADS
    nc = tt // CHUNK
    lat_chunks = n_lat // CHUNK
    ctx_chunks = (nc - lat_chunks) // 2

    def chunk_of(rev, i):
        return (nc - 1 - i) if rev else i

    def cmap(rev):
        return lambda h, i: (chunk_of(rev, i), h)

    def smap(rev):
        return lambda h, i: (chunk_of(rev, i), h, 0, 0)

    pair = 2

    def body(lg_ref, qf_ref, kf_ref, vf_ref, qb_ref, kb_ref, vb_ref, o_ref, sf_ref, sb_ref, state_f, state_b):
        h, i = pl.program_id(0), pl.program_id(1)

        @pl.when(i == 0)
        def _():
            state_f[...] = jnp.zeros_like(state_f)
            state_b[...] = jnp.zeros_like(state_b)

        for rev, (q_ref, k_ref, v_ref, s_ref, state) in enumerate((
                (qf_ref, kf_ref, vf_ref, sf_ref, state_f), (qb_ref, kb_ref, vb_ref, sb_ref, state_b))):
            lat_chunk = chunk_of(bool(rev), i) - ctx_chunks
            is_lat = (lat_chunk >= 0) & (lat_chunk < lat_chunks)
            first = (2 * i < nc - 1) if rev else (2 * i <= nc - 1)
            for e in range(pair):
                ks, vs = slice(e * dk, (e + 1) * dk), slice(e * dv, (e + 1) * dv)
                _, mask, qd, kd, cd = _decays(lg_ref[rev, pair * h + e], bool(rev))
                qv, kv, vv = q_ref[:, ks], k_ref[:, ks], v_ref[:, vs]
                st = state[e]
                p = _dot(qv, kv, _NT) * mask
                out = _dot(p.astype(BF16), vv) + _dot((qv * qd).astype(BF16), st.astype(BF16))
                s_ref[e] = st
                state[e] = cd * st + _dot((kv * kd).astype(BF16), vv, _TN)

                @pl.when(is_lat & first)
                def _(lat_chunk=lat_chunk, out=out, vs=vs):
                    o_ref[pl.ds(pl.multiple_of(lat_chunk * CHUNK, CHUNK), CHUNK), vs] = out

                @pl.when(is_lat & jnp.logical_not(first))
                def _(lat_chunk=lat_chunk, out=out, vs=vs):
                    o_ref[pl.ds(pl.multiple_of(lat_chunk * CHUNK, CHUNK), CHUNK), vs] += out

    def specs(rev):
        return [pl.BlockSpec((CHUNK, pair * dk), cmap(rev)), pl.BlockSpec((CHUNK, pair * dk), cmap(rev)),
                pl.BlockSpec((CHUNK, pair * dv), cmap(rev))]

    state_shape = jax.ShapeDtypeStruct((nc, HEADS, dk, dv), F32)
    return pl.pallas_call(
        body, name=name, grid=(HEADS // pair, nc),
        in_specs=[pl.BlockSpec(memory_space=pltpu.SMEM)] + specs(False) + specs(True),
        out_specs=[pl.BlockSpec((n_lat, pair * dv), lambda h, i: (0, h), pipeline_mode=pl.Buffered(1)),
                   pl.BlockSpec((None, pair, dk, dv), smap(False)), pl.BlockSpec((None, pair, dk, dv), smap(True))],
        out_shape=[jax.ShapeDtypeStruct((n_lat, 2 * d), F32), state_shape, state_shape],
        scratch_shapes=[pltpu.VMEM((pair, dk, dv), F32), pltpu.VMEM((pair, dk, dv), F32)],
        compiler_params=_params(("parallel", "arbitrary")))(lgs, q, k, v, q, k, v)


def _ret_bwd(q, k, v, do, states_f, states_b, lgs, name):
    tt, d = q.shape
    dk, dv = d // HEADS, 2 * d // HEADS
    nc = tt // CHUNK
    lat_chunks = do.shape[0] // CHUNK
    ctx_chunks = (nc - lat_chunks) // 2

    def chunk_of(rev, i):
        return i if rev else nc - 1 - i

    def cmap(rev):
        return lambda h, i: (chunk_of(rev, i), h)

    def do_map(rev):
        return lambda h, i: (jnp.clip(chunk_of(rev, i) - ctx_chunks, 0, lat_chunks - 1), h)

    def smap(rev):
        return lambda h, i: (chunk_of(rev, i), h, 0, 0)

    pair = 4

    def body(lg_ref, *refs):
        hh = pl.program_id(0)
        dstates = refs[-2:]

        @pl.when(pl.program_id(1) == 0)
        def _():
            for rev in (0, 1):
                dstates[rev][...] = jnp.zeros_like(dstates[rev])
                refs[10 + 4 * rev + 3][...] = jnp.zeros_like(refs[10 + 4 * rev + 3])

        for rev in (0, 1):
            q_ref, k_ref, v_ref, do_ref, s_ref = refs[5 * rev:5 * rev + 5]
            dq_ref, dk_ref, dv_ref, dlg_ref = refs[10 + 4 * rev:10 + 4 * rev + 4]
            chunk = chunk_of(bool(rev), pl.program_id(1))
            is_lat = (chunk >= ctx_chunks) & (chunk < ctx_chunks + lat_chunks)
            for e in range(pair):
                ks, vs = slice(e * dk, (e + 1) * dk), slice(e * dv, (e + 1) * dv)
                (rr, rel), mask, qd, kd, cd = _decays(lg_ref[rev, pair * hh + e], bool(rev))
                qv, kv, vv = q_ref[:, ks], k_ref[:, ks], v_ref[:, vs]
                dov = jnp.where(is_lat, do_ref[:, vs], jnp.zeros((CHUNK, dv), BF16))
                st, dst = s_ref[e], dstates[rev][e]
                st_b, dst_b = st.astype(BF16), dst.astype(BF16)
                p = _dot(qv, kv, _NT) * mask
                dp = _dot(dov, vv, _NT)
                da = (dp * mask).astype(BF16)
                dq_state = _dot(dov, st_b, _NT) * qd
                dk_state = _dot(vv, dst_b, _NT) * kd
                dq_ref[:, ks] = (_dot(da, kv) + dq_state).astype(BF16)
                dk_ref[:, ks] = (_dot(da, qv, _TN) + dk_state).astype(BF16)
                dv_ref[:, vs] = (_dot(p.astype(BF16), dov, _TN) + _dot((kv * kd).astype(BF16), dst_b)).astype(BF16)
                dnew = cd * dst + _dot((qv * qd).astype(BF16), dov, _TN)
                dstates[rev][e] = dnew
                through = rr * (jnp.sum(qv.astype(F32) * dq_state, axis=1, keepdims=True)
                                - jnp.sum(kv.astype(F32) * dk_state, axis=1, keepdims=True))
                dlg_ref[e] += _total(rel * p * dp) + _total(through) + CHUNK * _total(st * dnew)

    def specs(rev):
        return [pl.BlockSpec((CHUNK, pair * dk), cmap(rev)), pl.BlockSpec((CHUNK, pair * dk), cmap(rev)),
                pl.BlockSpec((CHUNK, pair * dv), cmap(rev)), pl.BlockSpec((CHUNK, pair * dv), do_map(rev)),
                pl.BlockSpec((None, pair, dk, dv), smap(rev))]

    def outs(rev):
        return [pl.BlockSpec((CHUNK, pair * dk), cmap(rev)), pl.BlockSpec((CHUNK, pair * dk), cmap(rev)),
                pl.BlockSpec((CHUNK, pair * dv), cmap(rev)), pl.BlockSpec((pair, 8, 128), lambda h, i: (h, 0, 0))]

    return pl.pallas_call(
        body, name=name, grid=(HEADS // pair, nc),
        in_specs=[pl.BlockSpec(memory_space=pltpu.SMEM)] + specs(False) + specs(True),
        out_specs=outs(False) + outs(True),
        out_shape=[jax.ShapeDtypeStruct((tt, d), BF16), jax.ShapeDtypeStruct((tt, d), BF16),
                   jax.ShapeDtypeStruct((tt, 2 * d), BF16), jax.ShapeDtypeStruct((HEADS, 8, 128), F32)] * 2,
        scratch_shapes=[pltpu.VMEM((pair, dk, dv), F32), pltpu.VMEM((pair, dk, dv), F32)],
        compiler_params=_params(("parallel", "arbitrary")))(lgs, q, k, v, do, states_f, q, k, v, do, states_b)


def _silu(v):
    return v * _sig(v)


def _mod_proj(cs, w_mod, b_loc, dec, name):
    nb = w_mod.shape[1]

    def body(cs_ref, w_ref, b_ref, dec_ref, out_ref, lg_ref):
        out_ref[...] = jnp.dot(_silu(cs_ref[...]), w_ref[...], preferred_element_type=F32, precision=HIGHEST) + b_ref[...]
        a = dec_ref[...]
        lg_ref[...] = jnp.minimum(a, 0.0) - jnp.log1p(jnp.exp(-jnp.abs(a)))

    return pl.pallas_call(
        body, name=name,
        out_shape=[jax.ShapeDtypeStruct((16, nb), F32), jax.ShapeDtypeStruct(dec.shape, F32)],
        compiler_params=_params())(cs, w_mod, b_loc, dec)


def _mod_grad(cs, dm, w_mod, name):
    d, nb = w_mod.shape

    def body(cs_ref, dm_ref, w_ref, gw_ref, part_ref):
        dmv = dm_ref[...]
        gw_ref[...] = lax.dot_general(_silu(cs_ref[...]), dmv, _TN, preferred_element_type=F32, precision=HIGHEST)
        part_ref[...] = lax.dot_general(dmv, w_ref[...], _NT, preferred_element_type=F32, precision=HIGHEST)

    return pl.pallas_call(
        body, name=name,
        out_shape=[jax.ShapeDtypeStruct((d, nb), F32), jax.ShapeDtypeStruct((16, d), F32)],
        compiler_params=_params())(cs, dm, w_mod)


def _reduce_small(gathered, dec, n_feat, name):
    _, rows, d = gathered.shape

    def body(g_ref, dec_ref, red_ref, misc_ref):
        total = g_ref[0]
        for i in range(1, N_DEV):
            total = total + g_ref[i]
        red_ref[...] = total
        misc_ref[...] = jnp.zeros_like(misc_ref)
        misc_ref[0:1, :] = jnp.zeros((1, 128), F32) + (0.5 / n_feat) * _total(total[16:17, :])
        misc_ref[1:3, :] = total[14:16, 0:128] * _sig(-dec_ref[0:2, :])

    return pl.pallas_call(
        body, name=name,
        out_shape=[jax.ShapeDtypeStruct((rows, d), F32), jax.ShapeDtypeStruct((8, 128), F32)],
        compiler_params=_params())(gathered, dec)


def _c_ctx_grad(parts, c_ctx, name):
    d = c_ctx.shape[1]

    def body(p_ref, c_ref, out_ref):
        total = p_ref[0]
        for i in range(1, N_DEV):
            total = total + p_ref[i]
        cv = c_ref[...]
        sg = _sig(cv)
        out_ref[...] = total[8:9, :] * (sg * (1.0 + cv * (1.0 - sg)))

    return pl.pallas_call(body, name=name, out_shape=jax.ShapeDtypeStruct((1, d), F32),
                          compiler_params=_params())(parts, c_ctx)


def _adamw(parts, w, m, v, name, own=None):
    n_parts, rows, cols = parts.shape
    row_bytes = cols * (parts.dtype.itemsize * (n_parts + 1) + 7 * 4)
    tr = _tile(rows, max(16, (8 * 1024 * 1024) // row_bytes), 16 if rows % 16 == 0 else 8)
    blocks = rows // tr

    def body(*refs):
        p_ref, w_ref, m_ref, v_ref, g_ref, d_ref, mo_ref, vo_ref = refs[-8:]
        if own is None:
            g = p_ref[0].astype(F32)
        else:
            g = refs[1][...].astype(F32) + p_ref[0].astype(F32)
        for i in range(1, n_parts):
            g = g + p_ref[i].astype(F32)
        m2 = ADAM_B1 * m_ref[...] + (1.0 - ADAM_B1) * g
        v2 = ADAM_B2 * v_ref[...] + (1.0 - ADAM_B2) * jnp.square(g)
        m_hat = m2 / (1.0 - ADAM_B1 ** ADAM_STEP)
        v_hat = v2 / (1.0 - ADAM_B2 ** ADAM_STEP)
        g_ref[...] = g
        d_ref[...] = -ADAM_LR * (m_hat / (jnp.sqrt(v_hat) + ADAM_EPS) + ADAM_WD * w_ref[...])
        mo_ref[...] = m2
        vo_ref[...] = v2

    out_shape = [jax.ShapeDtypeStruct((rows, cols), F32)] * 4
    if own is None:
        spec = _rspec(tr, cols)
        return pl.pallas_call(
            body, name=name, grid=(blocks,),
            in_specs=[pl.BlockSpec((n_parts, tr, cols), lambda i: (0, i, 0)), spec, spec, spec],
            out_specs=[spec] * 4, out_shape=out_shape, compiler_params=_params(("parallel",)))(parts, w, m, v)
    g_full, axis, me = own

    def own_map(i, me_ref):
        return (me_ref[0] * blocks + i, 0) if axis == 0 else (i, me_ref[0])

    spec = pl.BlockSpec((tr, cols), lambda i, me_ref: (i, 0))
    return pl.pallas_call(
        body, name=name, out_shape=out_shape,
        grid_spec=pltpu.PrefetchScalarGridSpec(
            num_scalar_prefetch=1, grid=(blocks,),
            in_specs=[pl.BlockSpec((tr, cols), own_map),
                      pl.BlockSpec((n_parts, tr, cols), lambda i, me_ref: (0, i, 0)), spec, spec, spec],
            out_specs=[spec] * 4),
        compiler_params=_params(("parallel",)))(me.reshape(1), g_full, parts, w, m, v)


def _rope_tables(pos, dk):
    half = dk // 2
    inv_freq = 1.0 / (ROPE_BASE ** jnp.linspace(0.0, 1.0, half, dtype=F32))
    ang = pos[:, None] * inv_freq[None, :]
    return jnp.cos(ang), jnp.sin(ang)


def _pad_lanes(v, width):
    return jnp.pad(v, ((0, 0), (0, width - v.shape[1])))


def kernel(x, c, ctx, c_ctx, w_mod, b_mod, norm1_g, w_in, conv_w, w_conv_out, ret_decay_fwd, ret_decay_bwd, w_ret_out, w_o, norm2_g, w_ff1, w_ff2, final_g, loss_target, m_c_ctx, m_w_mod, m_b_mod, m_norm1_g, m_w_in, m_conv_w, m_w_conv_out, m_ret_decay_fwd, m_ret_decay_bwd, m_w_ret_out, m_w_o, m_norm2_g, m_w_ff1, m_w_ff2, m_final_g, v_c_ctx, v_w_mod, v_b_mod, v_norm1_g, v_w_in, v_conv_w, v_w_conv_out, v_ret_decay_fwd, v_ret_decay_bwd, v_w_ret_out, v_w_o, v_norm2_g, v_w_ff1, v_w_ff2, v_final_g):
    n_lat, d = x.shape[1], x.shape[2]
    n_ctx = ctx.shape[1]
    assert n_ctx % CHUNK == 0 and n_lat % CHUNK == 0 and n_lat % GRID_W == 0
    dk = d // HEADS
    nb = w_mod.shape[2]
    me = 4 * lax.axis_index("x") + 2 * lax.axis_index("y") + lax.axis_index("c")
    xl, ctxl, target = x[0], ctx[0], loss_target[0]

    first = _small_allgather(jnp.concatenate([c, _pad_lanes(conv_w[0], d), jnp.zeros((4, d), F32)], axis=0), "ag_c")
    conv_w_f = first[:, 1:4, :d // N_DEV].transpose(1, 0, 2).reshape(3, d)

    c_all = first[:, 0, :]
    cs = jnp.concatenate([c_all, c_ctx[None], jnp.zeros((7, d), F32)], axis=0)
    dec = jnp.pad(jnp.concatenate([ret_decay_fwd, ret_decay_bwd], axis=0), ((0, 6), (0, 128 - HEADS)))
    b_loc = lax.dynamic_slice(b_mod, (0, me * nb), (1, nb))
    modp, lgs = _mod_proj(cs, w_mod[0], b_loc, dec, "mod_proj")
    modp_all = _small_allgather(modp, "ag_mod")
    mod_l = lax.dynamic_index_in_dim(modp_all, me, axis=1, keepdims=False).reshape(N_MOD, d)
    mod_c = modp_all[:, 8, :].reshape(N_MOD, d)
    lgs = lgs[0:2, :HEADS]
    zero_row = jnp.zeros((1, d), F32)
    vec1_l = jnp.concatenate([norm1_g, mod_l[0:1], mod_l[1:2], zero_row], axis=0)
    vec1_c = jnp.concatenate([norm1_g, mod_c[0:1], mod_c[1:2], zero_row], axis=0)
    vec2 = jnp.concatenate([norm2_g, mod_l[3:4], mod_l[4:5], mod_l[2:3]], axis=0)
    vec3 = jnp.concatenate([mod_l[5:6], final_g[None]], axis=0)

    a_all = _modulate(ctxl, vec1_c, "modulate1_ctx", n_lat + n_ctx,
                      into=_modulate(xl, vec1_l, "modulate1", n_lat + n_ctx))
    arrival = me ^ jnp.array([0, 1, 4, 5, 2, 3, 6, 7], jnp.int32)
    riders = [(_place_shard(w[0], 0, me, "place_" + wname), 0)
              for wname, w in (("w_conv_out", w_conv_out), ("w_ret_out", w_ret_out), ("w_o", w_o))]
    w_in_f, w_conv_out_f, w_ret_out_f, w_o_f, z = _gather_project(
        a_all, _place_shard(w_in[0], 1, me, "place_w_in"), arrival, "proj_in", riders=riders)

    later, after = {}, (z, conv_w_f)
    for wname, w, axis in (("w_ff1", w_ff1, 1), ("w_ff2", w_ff2, 0)):
        later[wname], token = _exchange_start(_place_shard(w[0], axis, me, "place_" + wname), axis, True,
                                              "ag_" + wname + "_start", after=after)
        after = (token,)
    pc = _conv_fwd(z, conv_w_f, n_lat, "conv_fwd", token)
    pos = jnp.concatenate([n_ctx + jnp.arange(n_lat, dtype=F32), jnp.arange(n_ctx, dtype=F32)])
    cos, sin = _rope_tables(pos, dk)
    q_s, k_s, v_s = _rotary_fwd(z, cos, sin, n_lat, "rotary_fwd")
    o, st_f, st_b = _ret_fwd(q_s, k_s, v_s, lgs, n_lat, "ret_fwd")
    r = _gn_fwd(o, z, "gn_fwd")
    y_conv = _mm(pc, w_conv_out_f, "nn", [F32], "proj_conv_out", tn=2048)
    tn_d = _tile(d, 1024, 128)
    gate_offs = (CGC * d // tn_d, CGR * d // tn_d)

    def merge(acc, yc, gc, gr):
        return acc, _sig(gc) * yc + _sig(gr) * acc

    y_ret, mg = _mm(r, w_ret_out_f, "nn", [F32, BF16], "proj_ret_out", tm=512, tn=tn_d, tk=2 * d,
                    extras=[(y_conv, 0), (z, gate_offs[0]), (z, gate_offs[1])], epilogue=merge)
    y_l = _mm(mg, w_o_f, "nn", [F32], "proj_o", tn=2048)
    x1, a2 = _resid_modulate(xl, y_l, vec2, "resid_modulate2")

    def sqrelu(acc):
        return acc, jnp.square(jnp.maximum(acc, 0.0))

    w_ff1_f = _exchange_wait(later["w_ff1"], a2, "ag_w_ff1_wait")
    hff, s = _mm(a2, w_ff1_f, "nn", [BF16, BF16], "ff1", tn=2048, epilogue=sqrelu)
    w_ff2_f = _exchange_wait(later["w_ff2"], s, "ag_w_ff2_wait")
    f = _mm(s, w_ff2_f, "nn", [F32], "ff2")
    dx2, df, acc3 = _loss_head(x1, f, target, vec3, "loss_head")

    def d_sqrelu(acc, h):
        return (acc * (2.0 * jnp.maximum(h.astype(F32), 0.0)),)

    dh = _mm(df, w_ff2_f, "nt", [BF16], "ff2_dx", extras=[(hff, 0)], epilogue=d_sqrelu)
    sent = {}
    sent["w_ff2"], token = _exchange_start(_mm(s, df, "tn", [BF16], "ff2_dw", tn=2048), 0, False, "rs_w_ff2_start")
    da2 = _mm(dh, w_ff1_f, "nt", [F32], "ff1_dx", dep=token)
    sent["w_ff1"], token = _exchange_start(_mm(a2, dh, "tn", [BF16], "ff1_dw", tn=2048), 1, False, "rs_w_ff1_start")
    dx1, dyl, acc2 = _modulate_bwd(da2, x1, vec2, "modulate2_bwd", dx_in=dx2, y=y_l)

    def d_merge(acc, yc, yr, gc, gr):
        sc, sr = _sig(gc), _sig(gr)
        return acc * sc, acc * sr, acc * yc * (sc * (1.0 - sc)), acc * yr * (sr * (1.0 - sr))

    dyc, dyr, dgc, dgr = _mm(dyl, w_o_f, "nt", [BF16] * 4, "proj_o_dx", tm=512, tn=tn_d,
                             extras=[(y_conv, 0), (y_ret, 0), (z, gate_offs[0]), (z, gate_offs[1])], epilogue=d_merge,
                             dep=token)
    sent["w_o"], token = _exchange_start(_mm(mg, dyl, "tn", [BF16], "proj_o_dw", tn=2048), 0, False, "rs_w_o_start")
    dpc = _mm(dyc, w_conv_out_f, "nt", [F32], "proj_conv_out_dx", tn=2048, dep=token)
    sent["w_conv_out"], token = _exchange_start(_mm(pc, dyc, "tn", [BF16], "proj_conv_out_dw", tn=2048), 0, False,
                                                "rs_w_conv_out_start")
    dz_conv, acc_conv = _conv_bwd(dpc, z, conv_w_f, "conv_bwd")
    dv_head = 2 * d // HEADS
    do, dz_g = _mm(dyr, w_ret_out_f, "nt", [BF16, BF16], "proj_ret_out_dx", tn=dv_head,
                   extras=[(o, 0), (z, CG * d // dv_head)], epilogue=_gn_bwd_head, dep=token)
    sent["w_ret_out"], token = _exchange_start(_mm(r, dyr, "tn", [BF16], "proj_ret_out_dw", tn=2048), 0, False,
                                               "rs_w_ret_out_start")
    dq_f, dk_f, dv_f, dlg_f, dq_b, dk_b, dv_b, dlg_b = _ret_bwd(q_s, k_s, v_s, do, st_f, st_b, lgs, "ret_bwd")
    dz = _dz_assemble(dq_f, dq_b, dk_f, dk_b, dv_f, dv_b, cos, sin, dz_conv, dz_g, dgc, dgr, n_lat, n_ctx,
                      "dz_assemble")
    g_in = _mm(a_all, dz, "tn", [BF16], "proj_in_dw", tn=2048, tk=(n_lat + n_ctx) // 4, dep=token)
    sent["w_in"], token = _exchange_start(g_in, 1, False, "rs_w_in_start")
    da_all = _mm(dz, w_in_f, "nt", [F32], "proj_in_dx", tn=2048, dep=token)
    grad_x, acc1 = _modulate_bwd(da_all, xl, vec1_l, "modulate1_bwd", dx_in=dx1)
    _, acc1c = _modulate_bwd(da_all, ctxl, vec1_c, "modulate1_ctx_bwd", da_off=n_lat)

    lane_pad = functools.partial(_pad_lanes, width=d)
    packet = jnp.concatenate([
        acc1[2:3] + acc1c[2:3], acc2[2:3], acc3[1:2],
        acc1[0:1], acc1[1:2], acc2[3:4], acc2[0:1], acc2[1:2], acc3[0:1],
        acc1c[0:1], acc1c[1:2],
        acc_conv[0:3],
        lane_pad(dlg_f[:, 0, 0][None]), lane_pad(dlg_b[:, 0, 0][None]),
        acc3[2:3],
        jnp.zeros((7, d), F32)], axis=0)
    n_rows = packet.shape[0]
    packets = lax.dynamic_update_slice(jnp.zeros((N_DEV * n_rows, d), F32), packet, (me * n_rows, 0))
    packets_sent, after = _exchange_start(packets, 0, True, "ag_small_start")
    res = {}

    def update(wname, axis, w, m, v, after):
        g_full, parts = _exchange_wait(sent[wname], after, "rs_" + wname + "_wait")
        res[wname] = _adamw(parts, w[0], m[0], v[0], "adamw_" + wname, own=(g_full, axis, me))
        return res[wname][0]

    for wname, axis, w, m, v in (("w_ff2", 0, w_ff2, m_w_ff2, v_w_ff2), ("w_ff1", 1, w_ff1, m_w_ff1, v_w_ff1),
                                 ("w_o", 0, w_o, m_w_o, v_w_o), ("w_conv_out", 0, w_conv_out, m_w_conv_out, v_w_conv_out),
                                 ("w_ret_out", 0, w_ret_out, m_w_ret_out, v_w_ret_out)):
        after = update(wname, axis, w, m, v, after)
    gathered = _exchange_wait(packets_sent, after, "ag_small_wait").reshape(N_DEV, n_rows, d)
    red, misc = _reduce_small(gathered, dec, d, "reduce_small")
    dmod_ctx = jnp.concatenate([red[9], red[10], jnp.zeros((4 * d,), F32)])
    dmod_all = jnp.concatenate([gathered[:, 3:9, :].reshape(N_DEV, N_MOD * d), dmod_ctx[None]], axis=0)
    dm = jnp.pad(lax.dynamic_slice(dmod_all, (0, me * nb), (N_DEV + 1, nb)), ((0, 7), (0, 0)))
    g_mod, c_part = _mod_grad(cs, dm, w_mod[0], "mod_grad")
    g_b_mod = red[3:9].reshape(1, N_MOD * d) + dmod_ctx[None]
    g_conv_w = lax.dynamic_slice(red[11:14], (0, me * (d // N_DEV)), (3, d // N_DEV))

    c_parts = lax.dynamic_update_slice(jnp.zeros((N_DEV * 16, d), F32), c_part, (me * 16, 0))
    c_parts_sent, token = _exchange_start(c_parts, 0, True, "ag_c_ctx_start")
    res["w_mod"] = _adamw(g_mod[None], w_mod[0], m_w_mod[0], v_w_mod[0], "adamw_w_mod")
    after = update("w_in", 1, w_in, m_w_in, v_w_in, token)
    c_parts = _exchange_wait(c_parts_sent, after, "ag_c_ctx_wait").reshape(N_DEV, 16, d)
    g_c_ctx = _c_ctx_grad(c_parts, c_ctx[None], "c_ctx_grad")
    res = {k: tuple(t[None] for t in val) for k, val in res.items()}

    small = [("c_ctx", g_c_ctx, c_ctx, m_c_ctx, v_c_ctx), ("b_mod", g_b_mod, b_mod, m_b_mod, v_b_mod),
             ("norm1_g", red[0:1], norm1_g, m_norm1_g, v_norm1_g), ("conv_w", g_conv_w, conv_w, m_conv_w, v_conv_w),
             ("ret_decay_fwd", misc[1:2, :HEADS], ret_decay_fwd, m_ret_decay_fwd, v_ret_decay_fwd),
             ("ret_decay_bwd", misc[2:3, :HEADS], ret_decay_bwd, m_ret_decay_bwd, v_ret_decay_bwd),
             ("norm2_g", red[1:2], norm2_g, m_norm2_g, v_norm2_g), ("final_g", red[2:3], final_g, m_final_g, v_final_g)]

    def flat(t):
        t = t.reshape(-1)
        return jnp.pad(t, (0, (-t.shape[0]) % 1024))

    packed = [jnp.concatenate([flat(item[j]) for item in small]).reshape(-1, 128) for j in range(1, 5)]
    outs = _adamw(packed[0][None], packed[1], packed[2], packed[3], "adamw_small")
    start = 0
    for name, _, w, _, _ in small:
        size = w.size
        res[name] = tuple(o.reshape(-1)[start:start + size].reshape(w.shape) for o in outs)
        start += size + (-size) % 1024

    order = ["c_ctx", "w_mod", "b_mod", "norm1_g", "w_in", "conv_w", "w_conv_out", "ret_decay_fwd", "ret_decay_bwd",
             "w_ret_out", "w_o", "norm2_g", "w_ff1", "w_ff2", "final_g"]
    loss = misc[0, 0]
    return (loss, grad_x[None], *[res[n][0] for n in order], *[res[n][1] for n in order],
            *[res[n][2] for n in order], *[res[n][3] for n in order])
```
